```python
import jax, jax.numpy as jnp
from jax import lax
import numpy as np

D_MODEL = 1024
BATCH = 16
SEQ = 2048
DEPTH = 1

MEM_LEN = 256
HEAD_DIM = 64
FOX_HEADS = D_MODEL // 128
FOX_W = FOX_HEADS * HEAD_DIM
RWKV_HEADS = D_MODEL // 128
RWKV_W = RWKV_HEADS * HEAD_DIM
MEM_HEADS = 4
MEM_W = D_MODEL // 2
MEM_HEAD_DIM = MEM_W // MEM_HEADS
DECAY_LORA = 64
AAA_LORA = 64
GATE_LORA = 128
N_BRANCH = 3
D_FF = -(-8 * D_MODEL // (3 * 256)) * 256
Q_BLOCK = 128
NORM_EPS = 1e-6
GN_EPS = 64e-5

FOX_COLS = 3 * FOX_W + FOX_HEADS
RWKV_WIDTHS = (RWKV_W, RWKV_W, RWKV_W, DECAY_LORA, AAA_LORA, GATE_LORA)
RWKV_COLS = sum(RWKV_WIDTHS)
GATE_COLS = N_BRANCH * D_MODEL
IN_COLS = FOX_COLS + RWKV_COLS + MEM_W + GATE_COLS

kernel_name = "fox_rwkv7_memxattn_gated_hybrid"


def _split(x, widths):
    idx = [int(i) for i in np.cumsum(widths)[:-1]]
    return jnp.split(x, idx, axis=-1)


def rmsnorm(x, g):
    xf = x.astype(jnp.float32)
    y = xf * lax.rsqrt(jnp.mean(xf * xf, axis=-1, keepdims=True) + NORM_EPS)
    return (y * g.astype(jnp.float32)).astype(x.dtype)


def forgetting_attention(q, k, v, f_logit):
    B, S, H, Dh = q.shape
    c = jnp.cumsum(jax.nn.log_sigmoid(f_logit.astype(jnp.float32)), axis=1)
    cT = jnp.transpose(c, (0, 2, 1))
    scale = Dh ** -0.5
    tri = jnp.tril(jnp.ones((Q_BLOCK, Q_BLOCK), dtype=bool))
    outs = []
    for i in range(S // Q_BLOCK):
        lo, hi = i * Q_BLOCK, (i + 1) * Q_BLOCK
        logits = jnp.einsum('bqhd,bkhd->bhqk', q[:, lo:hi], k[:, :hi]).astype(jnp.float32) * scale
        bias = cT[:, :, lo:hi, None] - cT[:, :, None, :hi]
        mask = jnp.concatenate([jnp.ones((Q_BLOCK, lo), dtype=bool), tri], axis=1)
        logits = jnp.where(mask, logits + bias, -jnp.inf)
        p = jax.nn.softmax(logits, axis=-1).astype(v.dtype)
        outs.append(jnp.einsum('bhqk,bkhd->bqhd', p, v[:, :hi]))
    return jnp.concatenate(outs, axis=1).reshape(B, S, H * Dh)


def rwkv7_time_mix(p, mu, w0, w_up, a0, a_up, g_up, k_k, k_a, r_k, gn_g, gn_b):
    B, S, _ = p.shape
    H, N = RWKV_HEADS, HEAD_DIM
    p = p.astype(jnp.float32)
    p_prev = jnp.pad(p, ((0, 0), (1, 0), (0, 0)))[:, :-1]
    p = p + (p_prev - p) * mu
    r, k, v, wd, ad, gd = _split(p, RWKV_WIDTHS)
    w_log = -jnp.exp(jax.nn.log_sigmoid(w0 + jnp.tanh(wd) @ w_up) - 0.5)
    a = jax.nn.sigmoid(a0 + ad @ a_up)
    g = jax.nn.sigmoid(gd) @ g_up
    kk = (k * k_k).reshape(B, S, H, N)
    kk = kk * lax.rsqrt(jnp.maximum(jnp.sum(kk * kk, axis=-1, keepdims=True), 1e-24))
    k = k * (1.0 + (a - 1.0) * k_a)
    rh = r.reshape(B, S, H, N)
    kh = k.reshape(B, S, H, N)
    vh = v.reshape(B, S, H, N)
    ah = a.reshape(B, S, H, N)
    wh = jnp.exp(w_log).reshape(B, S, H, N)
    a_vec = -kk
    b_vec = kk * ah

    def step(state, inp):
        r_t, w_t, k_t, v_t, a_t, b_t = inp
        sa = jnp.einsum('bhij,bhj->bhi', state, a_t)
        state = state * w_t[:, :, None, :] + sa[..., None] * b_t[:, :, None, :] + v_t[..., None] * k_t[:, :, None, :]
        y = jnp.einsum('bhij,bhj->bhi', state, r_t)
        return state, y

    xs = tuple(jnp.moveaxis(t, 1, 0) for t in (rh, wh, kh, vh, a_vec, b_vec))
    s0 = jnp.zeros((B, H, N, N), jnp.float32)
    _, ys = lax.scan(step, s0, xs)
    y = jnp.moveaxis(ys, 0, 1)
    mean = jnp.mean(y, axis=-1, keepdims=True)
    var = jnp.mean(jnp.square(y - mean), axis=-1, keepdims=True)
    y = ((y - mean) * lax.rsqrt(var + GN_EPS)).reshape(B, S, H * N) * gn_g + gn_b
    bonus = jnp.sum(rh * kh * r_k, axis=-1, keepdims=True) * vh
    return (y + bonus.reshape(B, S, H * N)) * g


def memory_cross_attention(q, mem_kv):
    B, S, _ = q.shape
    km, vm = _split(mem_kv, (MEM_W, MEM_W))
    qh = q.reshape(B, S, MEM_HEADS, MEM_HEAD_DIM)
    kh = km.reshape(B, -1, MEM_HEADS, MEM_HEAD_DIM)
    vh = vm.reshape(B, -1, MEM_HEADS, MEM_HEAD_DIM)
    logits = jnp.einsum('bqhd,bkhd->bhqk', qh, kh).astype(jnp.float32) * MEM_HEAD_DIM ** -0.5
    p = jax.nn.softmax(logits, axis=-1).astype(vh.dtype)
    return jnp.einsum('bhqk,bkhd->bqhd', p, vh).reshape(B, S, MEM_W)


def _fwd_setup_inputs(seed: int = 0) -> dict:
    key = jax.random.key(seed)
    ks = jax.random.split(key, 32)
    f32 = jnp.float32
    L, D = DEPTH, D_MODEL

    def nrm(k, shape, fan_in):
        return jax.random.normal(k, shape, f32) * fan_in ** -0.5

    def gain(k, shape):
        return 1.0 + 0.1 * jax.random.normal(k, shape, f32)

    return {
        "x": jax.random.normal(ks[0], (BATCH, SEQ, D), f32),
        "mem": jax.random.normal(ks[1], (BATCH, MEM_LEN, D), f32),
        "pre1_g": gain(ks[2], (L, D)),
        "post1_g": gain(ks[3], (L, D)),
        "pre2_g": gain(ks[4], (L, D)),
        "post2_g": gain(ks[5], (L, D)),
        "mem_norm_g": gain(ks[6], (L, D)),
        "w_in": nrm(ks[7], (L, D, IN_COLS), D),
        "fox_f_bias": jax.random.uniform(ks[8], (L, FOX_HEADS), f32, 1.0, 4.0),
        "rwkv_mu": jax.random.uniform(ks[9], (L, RWKV_COLS), f32),
        "rwkv_w0": jax.random.normal(ks[10], (L, RWKV_W), f32),
        "rwkv_w_up": nrm(ks[11], (L, DECAY_LORA, RWKV_W), DECAY_LORA),
        "rwkv_a0": 0.5 * jax.random.normal(ks[12], (L, RWKV_W), f32),
        "rwkv_a_up": nrm(ks[13], (L, AAA_LORA, RWKV_W), AAA_LORA),
        "rwkv_g_up": nrm(ks[14], (L, GATE_LORA, RWKV_W), GATE_LORA),
        "rwkv_k_k": 0.85 + 0.05 * jax.random.normal(ks[15], (L, RWKV_W), f32),
        "rwkv_k_a": 1.0 + 0.05 * jax.random.normal(ks[16], (L, RWKV_W), f32),
        "rwkv_r_k": 0.1 * jax.random.normal(ks[17], (L, RWKV_HEADS, HEAD_DIM), f32),
        "rwkv_gn_g": gain(ks[18], (L, RWKV_W)),
        "rwkv_gn_b": 0.02 * jax.random.normal(ks[19], (L, RWKV_W), f32),
        "w_mem_kv": nrm(ks[20], (L, D, 2 * MEM_W), D),
        "w_fox_out": nrm(ks[21], (L, FOX_W, D), FOX_W),
        "w_rwkv_out": nrm(ks[22], (L, RWKV_W, D), RWKV_W),
        "w_mem_out": nrm(ks[23], (L, MEM_W, D), MEM_W),
        "w_o": nrm(ks[24], (L, D, D), D),
        "w_ffn_gate": nrm(ks[25], (L, D, D_FF), D),
        "w_ffn_up": nrm(ks[26], (L, D, D_FF), D),
        "w_ffn_down": nrm(ks[27], (L, D_FF, D), D_FF),
    }


def _fwd_reference(x, mem, pre1_g, post1_g, pre2_g, post2_g, mem_norm_g, w_in, fox_f_bias,
              rwkv_mu, rwkv_w0, rwkv_w_up, rwkv_a0, rwkv_a_up, rwkv_g_up, rwkv_k_k, rwkv_k_a,
              rwkv_r_k, rwkv_gn_g, rwkv_gn_b, w_mem_kv, w_fox_out, w_rwkv_out, w_mem_out, w_o,
              w_ffn_gate, w_ffn_up, w_ffn_down):
    B, S, D = x.shape
    h = x
    for l in range(DEPTH):
        u = rmsnorm(h, pre1_g[l])
        proj = u @ w_in[l]
        p_fox, p_rwkv, p_memq, p_gate = _split(proj, (FOX_COLS, RWKV_COLS, MEM_W, GATE_COLS))

        fq, fk, fv, ff = _split(p_fox, (FOX_W, FOX_W, FOX_W, FOX_HEADS))
        fox_out = forgetting_attention(
            fq.reshape(B, S, FOX_HEADS, HEAD_DIM), fk.reshape(B, S, FOX_HEADS, HEAD_DIM),
            fv.reshape(B, S, FOX_HEADS, HEAD_DIM), ff + fox_f_bias[l])

        rwkv_out = rwkv7_time_mix(p_rwkv, rwkv_mu[l], rwkv_w0[l], rwkv_w_up[l], rwkv_a0[l],
                                  rwkv_a_up[l], rwkv_g_up[l], rwkv_k_k[l], rwkv_k_a[l],
                                  rwkv_r_k[l], rwkv_gn_g[l], rwkv_gn_b[l])

        mem_kv = rmsnorm(mem, mem_norm_g[l]) @ w_mem_kv[l]
        mem_out = memory_cross_attention(p_memq, mem_kv)

        g_fox, g_rwkv, g_mem = _split(jax.nn.sigmoid(p_gate.astype(jnp.float32)), (D, D, D))
        merged = (g_fox * (fox_out @ w_fox_out[l])
                  + g_rwkv * (rwkv_out @ w_rwkv_out[l])
                  + g_mem * (mem_out @ w_mem_out[l]))
        y = merged @ w_o[l]
        h = h + rmsnorm(y, post1_g[l])

        u2 = rmsnorm(h, pre2_g[l])
        ffn = (jax.nn.silu(u2 @ w_ffn_gate[l]) * (u2 @ w_ffn_up[l])) @ w_ffn_down[l]
        h = h + rmsnorm(ffn, post2_g[l])
    return h.astype(x.dtype)


import jax as _jax
import jax.numpy as _jnp

TWIN_FORMAT = 'train_step'
FWD_PARAMS = ['x', 'mem', 'pre1_g', 'post1_g', 'pre2_g', 'post2_g', 'mem_norm_g', 'w_in', 'fox_f_bias', 'rwkv_mu', 'rwkv_w0', 'rwkv_w_up', 'rwkv_a0', 'rwkv_a_up', 'rwkv_g_up', 'rwkv_k_k', 'rwkv_k_a', 'rwkv_r_k', 'rwkv_gn_g', 'rwkv_gn_b', 'w_mem_kv', 'w_fox_out', 'w_rwkv_out', 'w_mem_out', 'w_o', 'w_ffn_gate', 'w_ffn_up', 'w_ffn_down']
TWIN_WEIGHTS = ['pre1_g', 'post1_g', 'pre2_g', 'post2_g', 'mem_norm_g', 'w_in', 'fox_f_bias', 'rwkv_mu', 'rwkv_w0', 'rwkv_w_up', 'rwkv_a0', 'rwkv_a_up', 'rwkv_g_up', 'rwkv_k_k', 'rwkv_k_a', 'rwkv_r_k', 'rwkv_gn_g', 'rwkv_gn_b', 'w_mem_kv', 'w_fox_out', 'w_rwkv_out', 'w_mem_out', 'w_o', 'w_ffn_gate', 'w_ffn_up', 'w_ffn_down']
TWIN_DIFF_INPUT = 'x'
TWIN_INPUTS = ['x', 'mem', 'pre1_g', 'post1_g', 'pre2_g', 'post2_g', 'mem_norm_g', 'w_in', 'fox_f_bias', 'rwkv_mu', 'rwkv_w0', 'rwkv_w_up', 'rwkv_a0', 'rwkv_a_up', 'rwkv_g_up', 'rwkv_k_k', 'rwkv_k_a', 'rwkv_r_k', 'rwkv_gn_g', 'rwkv_gn_b', 'w_mem_kv', 'w_fox_out', 'w_rwkv_out', 'w_mem_out', 'w_o', 'w_ffn_gate', 'w_ffn_up', 'w_ffn_down', 'loss_target', 'm_pre1_g', 'm_post1_g', 'm_pre2_g', 'm_post2_g', 'm_mem_norm_g', 'm_w_in', 'm_fox_f_bias', 'm_rwkv_mu', 'm_rwkv_w0', 'm_rwkv_w_up', 'm_rwkv_a0', 'm_rwkv_a_up', 'm_rwkv_g_up', 'm_rwkv_k_k', 'm_rwkv_k_a', 'm_rwkv_r_k', 'm_rwkv_gn_g', 'm_rwkv_gn_b', 'm_w_mem_kv', 'm_w_fox_out', 'm_w_rwkv_out', 'm_w_mem_out', 'm_w_o', 'm_w_ffn_gate', 'm_w_ffn_up', 'm_w_ffn_down', 'v_pre1_g', 'v_post1_g', 'v_pre2_g', 'v_post2_g', 'v_mem_norm_g', 'v_w_in', 'v_fox_f_bias', 'v_rwkv_mu', 'v_rwkv_w0', 'v_rwkv_w_up', 'v_rwkv_a0', 'v_rwkv_a_up', 'v_rwkv_g_up', 'v_rwkv_k_k', 'v_rwkv_k_a', 'v_rwkv_r_k', 'v_rwkv_gn_g', 'v_rwkv_gn_b', 'v_w_mem_kv', 'v_w_fox_out', 'v_w_rwkv_out', 'v_w_mem_out', 'v_w_o', 'v_w_ffn_gate', 'v_w_ffn_up', 'v_w_ffn_down']
TWIN_OUTPUTS = ['loss', 'grad_x', 'grad_pre1_g', 'grad_post1_g', 'grad_pre2_g', 'grad_post2_g', 'grad_mem_norm_g', 'grad_w_in', 'grad_fox_f_bias', 'grad_rwkv_mu', 'grad_rwkv_w0', 'grad_rwkv_w_up', 'grad_rwkv_a0', 'grad_rwkv_a_up', 'grad_rwkv_g_up', 'grad_rwkv_k_k', 'grad_rwkv_k_a', 'grad_rwkv_r_k', 'grad_rwkv_gn_g', 'grad_rwkv_gn_b', 'grad_w_mem_kv', 'grad_w_fox_out', 'grad_w_rwkv_out', 'grad_w_mem_out', 'grad_w_o', 'grad_w_ffn_gate', 'grad_w_ffn_up', 'grad_w_ffn_down', 'delta_pre1_g', 'delta_post1_g', 'delta_pre2_g', 'delta_post2_g', 'delta_mem_norm_g', 'delta_w_in', 'delta_fox_f_bias', 'delta_rwkv_mu', 'delta_rwkv_w0', 'delta_rwkv_w_up', 'delta_rwkv_a0', 'delta_rwkv_a_up', 'delta_rwkv_g_up', 'delta_rwkv_k_k', 'delta_rwkv_k_a', 'delta_rwkv_r_k', 'delta_rwkv_gn_g', 'delta_rwkv_gn_b', 'delta_w_mem_kv', 'delta_w_fox_out', 'delta_w_rwkv_out', 'delta_w_mem_out', 'delta_w_o', 'delta_w_ffn_gate', 'delta_w_ffn_up', 'delta_w_ffn_down', 'new_m_pre1_g', 'new_m_post1_g', 'new_m_pre2_g', 'new_m_post2_g', 'new_m_mem_norm_g', 'new_m_w_in', 'new_m_fox_f_bias', 'new_m_rwkv_mu', 'new_m_rwkv_w0', 'new_m_rwkv_w_up', 'new_m_rwkv_a0', 'new_m_rwkv_a_up', 'new_m_rwkv_g_up', 'new_m_rwkv_k_k', 'new_m_rwkv_k_a', 'new_m_rwkv_r_k', 'new_m_rwkv_gn_g', 'new_m_rwkv_gn_b', 'new_m_w_mem_kv', 'new_m_w_fox_out', 'new_m_w_rwkv_out', 'new_m_w_mem_out', 'new_m_w_o', 'new_m_w_ffn_gate', 'new_m_w_ffn_up', 'new_m_w_ffn_down', 'new_v_pre1_g', 'new_v_post1_g', 'new_v_pre2_g', 'new_v_post2_g', 'new_v_mem_norm_g', 'new_v_w_in', 'new_v_fox_f_bias', 'new_v_rwkv_mu', 'new_v_rwkv_w0', 'new_v_rwkv_w_up', 'new_v_rwkv_a0', 'new_v_rwkv_a_up', 'new_v_rwkv_g_up', 'new_v_rwkv_k_k', 'new_v_rwkv_k_a', 'new_v_rwkv_r_k', 'new_v_rwkv_gn_g', 'new_v_rwkv_gn_b', 'new_v_w_mem_kv', 'new_v_w_fox_out', 'new_v_w_rwkv_out', 'new_v_w_mem_out', 'new_v_w_o', 'new_v_w_ffn_gate', 'new_v_w_ffn_up', 'new_v_w_ffn_down']
TWIN_LEAF_KINDS = {'loss': 'loss', 'grad_x': 'grad_x', 'grad_pre1_g': 'grad_w', 'grad_post1_g': 'grad_w', 'grad_pre2_g': 'grad_w', 'grad_post2_g': 'grad_w', 'grad_mem_norm_g': 'grad_w', 'grad_w_in': 'grad_w', 'grad_fox_f_bias': 'grad_w', 'grad_rwkv_mu': 'grad_w', 'grad_rwkv_w0': 'grad_w', 'grad_rwkv_w_up': 'grad_w', 'grad_rwkv_a0': 'grad_w', 'grad_rwkv_a_up': 'grad_w', 'grad_rwkv_g_up': 'grad_w', 'grad_rwkv_k_k': 'grad_w', 'grad_rwkv_k_a': 'grad_w', 'grad_rwkv_r_k': 'grad_w', 'grad_rwkv_gn_g': 'grad_w', 'grad_rwkv_gn_b': 'grad_w', 'grad_w_mem_kv': 'grad_w', 'grad_w_fox_out': 'grad_w', 'grad_w_rwkv_out': 'grad_w', 'grad_w_mem_out': 'grad_w', 'grad_w_o': 'grad_w', 'grad_w_ffn_gate': 'grad_w', 'grad_w_ffn_up': 'grad_w', 'grad_w_ffn_down': 'grad_w', 'delta_pre1_g': 'delta_w', 'delta_post1_g': 'delta_w', 'delta_pre2_g': 'delta_w', 'delta_post2_g': 'delta_w', 'delta_mem_norm_g': 'delta_w', 'delta_w_in': 'delta_w', 'delta_fox_f_bias': 'delta_w', 'delta_rwkv_mu': 'delta_w', 'delta_rwkv_w0': 'delta_w', 'delta_rwkv_w_up': 'delta_w', 'delta_rwkv_a0': 'delta_w', 'delta_rwkv_a_up': 'delta_w', 'delta_rwkv_g_up': 'delta_w', 'delta_rwkv_k_k': 'delta_w', 'delta_rwkv_k_a': 'delta_w', 'delta_rwkv_r_k': 'delta_w', 'delta_rwkv_gn_g': 'delta_w', 'delta_rwkv_gn_b': 'delta_w', 'delta_w_mem_kv': 'delta_w', 'delta_w_fox_out': 'delta_w', 'delta_w_rwkv_out': 'delta_w', 'delta_w_mem_out': 'delta_w', 'delta_w_o': 'delta_w', 'delta_w_ffn_gate': 'delta_w', 'delta_w_ffn_up': 'delta_w', 'delta_w_ffn_down': 'delta_w', 'new_m_pre1_g': 'new_m', 'new_m_post1_g': 'new_m', 'new_m_pre2_g': 'new_m', 'new_m_post2_g': 'new_m', 'new_m_mem_norm_g': 'new_m', 'new_m_w_in': 'new_m', 'new_m_fox_f_bias': 'new_m', 'new_m_rwkv_mu': 'new_m', 'new_m_rwkv_w0': 'new_m', 'new_m_rwkv_w_up': 'new_m', 'new_m_rwkv_a0': 'new_m', 'new_m_rwkv_a_up': 'new_m', 'new_m_rwkv_g_up': 'new_m', 'new_m_rwkv_k_k': 'new_m', 'new_m_rwkv_k_a': 'new_m', 'new_m_rwkv_r_k': 'new_m', 'new_m_rwkv_gn_g': 'new_m', 'new_m_rwkv_gn_b': 'new_m', 'new_m_w_mem_kv': 'new_m', 'new_m_w_fox_out': 'new_m', 'new_m_w_rwkv_out': 'new_m', 'new_m_w_mem_out': 'new_m', 'new_m_w_o': 'new_m', 'new_m_w_ffn_gate': 'new_m', 'new_m_w_ffn_up': 'new_m', 'new_m_w_ffn_down': 'new_m', 'new_v_pre1_g': 'new_v', 'new_v_post1_g': 'new_v', 'new_v_pre2_g': 'new_v', 'new_v_post2_g': 'new_v', 'new_v_mem_norm_g': 'new_v', 'new_v_w_in': 'new_v', 'new_v_fox_f_bias': 'new_v', 'new_v_rwkv_mu': 'new_v', 'new_v_rwkv_w0': 'new_v', 'new_v_rwkv_w_up': 'new_v', 'new_v_rwkv_a0': 'new_v', 'new_v_rwkv_a_up': 'new_v', 'new_v_rwkv_g_up': 'new_v', 'new_v_rwkv_k_k': 'new_v', 'new_v_rwkv_k_a': 'new_v', 'new_v_rwkv_r_k': 'new_v', 'new_v_rwkv_gn_g': 'new_v', 'new_v_rwkv_gn_b': 'new_v', 'new_v_w_mem_kv': 'new_v', 'new_v_w_fox_out': 'new_v', 'new_v_w_rwkv_out': 'new_v', 'new_v_w_mem_out': 'new_v', 'new_v_w_o': 'new_v', 'new_v_w_ffn_gate': 'new_v', 'new_v_w_ffn_up': 'new_v', 'new_v_w_ffn_down': 'new_v'}


def _forward(args):
    return _fwd_reference(*[args[k] for k in FWD_PARAMS])


def _output_shape():
    out = _jax.eval_shape(lambda: _forward(_fwd_setup_inputs(0)))
    return out.shape, out.dtype

N_MICROBATCH = 1
ADAM_LR = 0.001
ADAM_B1 = 0.9
ADAM_B2 = 0.999
ADAM_EPS = 1e-08
ADAM_WD = 0.01
ADAM_STEP = 10
PER_EXAMPLE_BATCH_AXIS = {'x': 0, 'mem': 0, 'loss_target': 0}
SHARED_INPUTS = []
_WEIGHT_DTYPES = {'pre1_g': _jnp.float32, 'post1_g': _jnp.float32, 'pre2_g': _jnp.float32, 'post2_g': _jnp.float32, 'mem_norm_g': _jnp.float32, 'w_in': _jnp.float32, 'fox_f_bias': _jnp.float32, 'rwkv_mu': _jnp.float32, 'rwkv_w0': _jnp.float32, 'rwkv_w_up': _jnp.float32, 'rwkv_a0': _jnp.float32, 'rwkv_a_up': _jnp.float32, 'rwkv_g_up': _jnp.float32, 'rwkv_k_k': _jnp.float32, 'rwkv_k_a': _jnp.float32, 'rwkv_r_k': _jnp.float32, 'rwkv_gn_g': _jnp.float32, 'rwkv_gn_b': _jnp.float32, 'w_mem_kv': _jnp.float32, 'w_fox_out': _jnp.float32, 'w_rwkv_out': _jnp.float32, 'w_mem_out': _jnp.float32, 'w_o': _jnp.float32, 'w_ffn_gate': _jnp.float32, 'w_ffn_up': _jnp.float32, 'w_ffn_down': _jnp.float32}
MOMENT_SCALE = {'pre1_g': 8.389344e-01, 'post1_g': 3.194764e+01, 'pre2_g': 5.963511e-01, 'post2_g': 3.212174e+01, 'mem_norm_g': 1.213156e-01, 'w_in': 3.159857e-01, 'fox_f_bias': 1.463212e+00, 'rwkv_mu': 8.883568e-01, 'rwkv_w0': 2.096728e-01, 'rwkv_w_up': 5.642477e-02, 'rwkv_a0': 2.147623e-01, 'rwkv_a_up': 1.749434e-01, 'rwkv_g_up': 5.043524e-01, 'rwkv_k_k': 1.412002e-01, 'rwkv_k_a': 5.655575e-01, 'rwkv_r_k': 1.121493e+00, 'rwkv_gn_g': 6.208747e-01, 'rwkv_gn_b': 1.614081e+00, 'w_mem_kv': 1.206243e-01, 'w_fox_out': 3.213116e-01, 'w_rwkv_out': 4.539589e-01, 'w_mem_out': 9.502275e-02, 'w_o': 6.125806e-01, 'w_ffn_gate': 1.987129e-01, 'w_ffn_up': 3.006313e-01, 'w_ffn_down': 5.027800e-01}


def _to_microbatches(a, axis):
    t = _jnp.moveaxis(a, axis, 0)
    t = t.reshape((N_MICROBATCH, t.shape[0] // N_MICROBATCH) + t.shape[1:])
    return _jnp.moveaxis(t, 1, axis + 1)


def setup_inputs(seed: int = 0) -> dict:
    inp = _fwd_setup_inputs(seed)
    key = _jax.random.fold_in(_jax.random.key(seed), 7919)
    shape, _ = _output_shape()
    out = dict(inp)
    out["loss_target"] = _jax.random.normal(_jax.random.fold_in(key, 0), shape, _jnp.float32)
    for i, name in enumerate(TWIN_WEIGHTS):
        w = inp[name].astype(_jnp.float32)
        if MOMENT_SCALE is None:
            s = _jnp.sqrt(_jnp.mean(_jnp.square(w)) + 1e-30)
        else:
            s = MOMENT_SCALE[name]
        km, kv = _jax.random.split(_jax.random.fold_in(key, i + 1))
        out[name] = w
        out["m_" + name] = s * _jax.random.normal(km, w.shape, _jnp.float32)
        out["v_" + name] = (s * s) * _jax.random.uniform(kv, w.shape, _jnp.float32, 0.5, 1.5)
    if N_MICROBATCH > 1:
        for name, axis in PER_EXAMPLE_BATCH_AXIS.items():
            out[name] = _to_microbatches(out[name], axis)
    return {'x': out['x'], 'mem': out['mem'], 'pre1_g': out['pre1_g'], 'post1_g': out['post1_g'], 'pre2_g': out['pre2_g'], 'post2_g': out['post2_g'], 'mem_norm_g': out['mem_norm_g'], 'w_in': out['w_in'], 'fox_f_bias': out['fox_f_bias'], 'rwkv_mu': out['rwkv_mu'], 'rwkv_w0': out['rwkv_w0'], 'rwkv_w_up': out['rwkv_w_up'], 'rwkv_a0': out['rwkv_a0'], 'rwkv_a_up': out['rwkv_a_up'], 'rwkv_g_up': out['rwkv_g_up'], 'rwkv_k_k': out['rwkv_k_k'], 'rwkv_k_a': out['rwkv_k_a'], 'rwkv_r_k': out['rwkv_r_k'], 'rwkv_gn_g': out['rwkv_gn_g'], 'rwkv_gn_b': out['rwkv_gn_b'], 'w_mem_kv': out['w_mem_kv'], 'w_fox_out': out['w_fox_out'], 'w_rwkv_out': out['w_rwkv_out'], 'w_mem_out': out['w_mem_out'], 'w_o': out['w_o'], 'w_ffn_gate': out['w_ffn_gate'], 'w_ffn_up': out['w_ffn_up'], 'w_ffn_down': out['w_ffn_down'], 'loss_target': out['loss_target'], 'm_pre1_g': out['m_pre1_g'], 'm_post1_g': out['m_post1_g'], 'm_pre2_g': out['m_pre2_g'], 'm_post2_g': out['m_post2_g'], 'm_mem_norm_g': out['m_mem_norm_g'], 'm_w_in': out['m_w_in'], 'm_fox_f_bias': out['m_fox_f_bias'], 'm_rwkv_mu': out['m_rwkv_mu'], 'm_rwkv_w0': out['m_rwkv_w0'], 'm_rwkv_w_up': out['m_rwkv_w_up'], 'm_rwkv_a0': out['m_rwkv_a0'], 'm_rwkv_a_up': out['m_rwkv_a_up'], 'm_rwkv_g_up': out['m_rwkv_g_up'], 'm_rwkv_k_k': out['m_rwkv_k_k'], 'm_rwkv_k_a': out['m_rwkv_k_a'], 'm_rwkv_r_k': out['m_rwkv_r_k'], 'm_rwkv_gn_g': out['m_rwkv_gn_g'], 'm_rwkv_gn_b': out['m_rwkv_gn_b'], 'm_w_mem_kv': out['m_w_mem_kv'], 'm_w_fox_out': out['m_w_fox_out'], 'm_w_rwkv_out': out['m_w_rwkv_out'], 'm_w_mem_out': out['m_w_mem_out'], 'm_w_o': out['m_w_o'], 'm_w_ffn_gate': out['m_w_ffn_gate'], 'm_w_ffn_up': out['m_w_ffn_up'], 'm_w_ffn_down': out['m_w_ffn_down'], 'v_pre1_g': out['v_pre1_g'], 'v_post1_g': out['v_post1_g'], 'v_pre2_g': out['v_pre2_g'], 'v_post2_g': out['v_post2_g'], 'v_mem_norm_g': out['v_mem_norm_g'], 'v_w_in': out['v_w_in'], 'v_fox_f_bias': out['v_fox_f_bias'], 'v_rwkv_mu': out['v_rwkv_mu'], 'v_rwkv_w0': out['v_rwkv_w0'], 'v_rwkv_w_up': out['v_rwkv_w_up'], 'v_rwkv_a0': out['v_rwkv_a0'], 'v_rwkv_a_up': out['v_rwkv_a_up'], 'v_rwkv_g_up': out['v_rwkv_g_up'], 'v_rwkv_k_k': out['v_rwkv_k_k'], 'v_rwkv_k_a': out['v_rwkv_k_a'], 'v_rwkv_r_k': out['v_rwkv_r_k'], 'v_rwkv_gn_g': out['v_rwkv_gn_g'], 'v_rwkv_gn_b': out['v_rwkv_gn_b'], 'v_w_mem_kv': out['v_w_mem_kv'], 'v_w_fox_out': out['v_w_fox_out'], 'v_w_rwkv_out': out['v_w_rwkv_out'], 'v_w_mem_out': out['v_w_mem_out'], 'v_w_o': out['v_w_o'], 'v_w_ffn_gate': out['v_w_ffn_gate'], 'v_w_ffn_up': out['v_w_ffn_up'], 'v_w_ffn_down': out['v_w_ffn_down']}


def _loss(weights, diff, rest, loss_target):
    with _jax.named_scope("forward"):
        args = {**rest, TWIN_DIFF_INPUT: diff, **{k: w.astype(_WEIGHT_DTYPES[k]) for k, w in weights.items()}}
        y = _forward(args)
    with _jax.named_scope("loss_head"):
        err = _jnp.square(y.astype(_jnp.float32) - loss_target)
        return 0.5 * _jnp.sum(_jnp.mean(err, axis=-1)) if err.ndim else 0.5 * err


def _adamw(w, g, m, v):
    m = ADAM_B1 * m + (1.0 - ADAM_B1) * g
    v = ADAM_B2 * v + (1.0 - ADAM_B2) * _jnp.square(g)
    m_hat = m / (1.0 - ADAM_B1 ** ADAM_STEP)
    v_hat = v / (1.0 - ADAM_B2 ** ADAM_STEP)
    delta = -ADAM_LR * (m_hat / (_jnp.sqrt(v_hat) + ADAM_EPS) + ADAM_WD * w)
    return delta, m, v


def reference(x, mem, pre1_g, post1_g, pre2_g, post2_g, mem_norm_g, w_in, fox_f_bias, rwkv_mu, rwkv_w0, rwkv_w_up, rwkv_a0, rwkv_a_up, rwkv_g_up, rwkv_k_k, rwkv_k_a, rwkv_r_k, rwkv_gn_g, rwkv_gn_b, w_mem_kv, w_fox_out, w_rwkv_out, w_mem_out, w_o, w_ffn_gate, w_ffn_up, w_ffn_down, loss_target, m_pre1_g, m_post1_g, m_pre2_g, m_post2_g, m_mem_norm_g, m_w_in, m_fox_f_bias, m_rwkv_mu, m_rwkv_w0, m_rwkv_w_up, m_rwkv_a0, m_rwkv_a_up, m_rwkv_g_up, m_rwkv_k_k, m_rwkv_k_a, m_rwkv_r_k, m_rwkv_gn_g, m_rwkv_gn_b, m_w_mem_kv, m_w_fox_out, m_w_rwkv_out, m_w_mem_out, m_w_o, m_w_ffn_gate, m_w_ffn_up, m_w_ffn_down, v_pre1_g, v_post1_g, v_pre2_g, v_post2_g, v_mem_norm_g, v_w_in, v_fox_f_bias, v_rwkv_mu, v_rwkv_w0, v_rwkv_w_up, v_rwkv_a0, v_rwkv_a_up, v_rwkv_g_up, v_rwkv_k_k, v_rwkv_k_a, v_rwkv_r_k, v_rwkv_gn_g, v_rwkv_gn_b, v_w_mem_kv, v_w_fox_out, v_w_rwkv_out, v_w_mem_out, v_w_o, v_w_ffn_gate, v_w_ffn_up, v_w_ffn_down):
    given = dict(x=x, mem=mem, pre1_g=pre1_g, post1_g=post1_g, pre2_g=pre2_g, post2_g=post2_g, mem_norm_g=mem_norm_g, w_in=w_in, fox_f_bias=fox_f_bias, rwkv_mu=rwkv_mu, rwkv_w0=rwkv_w0, rwkv_w_up=rwkv_w_up, rwkv_a0=rwkv_a0, rwkv_a_up=rwkv_a_up, rwkv_g_up=rwkv_g_up, rwkv_k_k=rwkv_k_k, rwkv_k_a=rwkv_k_a, rwkv_r_k=rwkv_r_k, rwkv_gn_g=rwkv_gn_g, rwkv_gn_b=rwkv_gn_b, w_mem_kv=w_mem_kv, w_fox_out=w_fox_out, w_rwkv_out=w_rwkv_out, w_mem_out=w_mem_out, w_o=w_o, w_ffn_gate=w_ffn_gate, w_ffn_up=w_ffn_up, w_ffn_down=w_ffn_down, loss_target=loss_target, m_pre1_g=m_pre1_g, m_post1_g=m_post1_g, m_pre2_g=m_pre2_g, m_post2_g=m_post2_g, m_mem_norm_g=m_mem_norm_g, m_w_in=m_w_in, m_fox_f_bias=m_fox_f_bias, m_rwkv_mu=m_rwkv_mu, m_rwkv_w0=m_rwkv_w0, m_rwkv_w_up=m_rwkv_w_up, m_rwkv_a0=m_rwkv_a0, m_rwkv_a_up=m_rwkv_a_up, m_rwkv_g_up=m_rwkv_g_up, m_rwkv_k_k=m_rwkv_k_k, m_rwkv_k_a=m_rwkv_k_a, m_rwkv_r_k=m_rwkv_r_k, m_rwkv_gn_g=m_rwkv_gn_g, m_rwkv_gn_b=m_rwkv_gn_b, m_w_mem_kv=m_w_mem_kv, m_w_fox_out=m_w_fox_out, m_w_rwkv_out=m_w_rwkv_out, m_w_mem_out=m_w_mem_out, m_w_o=m_w_o, m_w_ffn_gate=m_w_ffn_gate, m_w_ffn_up=m_w_ffn_up, m_w_ffn_down=m_w_ffn_down, v_pre1_g=v_pre1_g, v_post1_g=v_post1_g, v_pre2_g=v_pre2_g, v_post2_g=v_post2_g, v_mem_norm_g=v_mem_norm_g, v_w_in=v_w_in, v_fox_f_bias=v_fox_f_bias, v_rwkv_mu=v_rwkv_mu, v_rwkv_w0=v_rwkv_w0, v_rwkv_w_up=v_rwkv_w_up, v_rwkv_a0=v_rwkv_a0, v_rwkv_a_up=v_rwkv_a_up, v_rwkv_g_up=v_rwkv_g_up, v_rwkv_k_k=v_rwkv_k_k, v_rwkv_k_a=v_rwkv_k_a, v_rwkv_r_k=v_rwkv_r_k, v_rwkv_gn_g=v_rwkv_gn_g, v_rwkv_gn_b=v_rwkv_gn_b, v_w_mem_kv=v_w_mem_kv, v_w_fox_out=v_w_fox_out, v_w_rwkv_out=v_w_rwkv_out, v_w_mem_out=v_w_mem_out, v_w_o=v_w_o, v_w_ffn_gate=v_w_ffn_gate, v_w_ffn_up=v_w_ffn_up, v_w_ffn_down=v_w_ffn_down)
    weights = {n: given[n] for n in TWIN_WEIGHTS}
    shared = {n: given[n] for n in SHARED_INPUTS}
    per_example = {n: given[n] for n in ['x', 'mem']}
    grad_fn = _jax.value_and_grad(_loss, argnums=(0, 1))

    def one_microbatch(ex, loss_target):
        ex = dict(ex)
        diff = ex.pop(TWIN_DIFF_INPUT)
        return grad_fn(weights, diff, {**shared, **ex}, loss_target)

    if N_MICROBATCH == 1:
        loss, (grad_w, grad_x) = one_microbatch(per_example, given["loss_target"])
    else:
        def body(carry, xs):
            loss_sum, grad_sum = carry
            l_k, (gw_k, gx_k) = one_microbatch(xs[0], xs[1])
            with _jax.named_scope("update"):
                return (loss_sum + l_k, _jax.tree.map(_jnp.add, grad_sum, gw_k)), gx_k

        init = (_jnp.zeros((), _jnp.float32), _jax.tree.map(_jnp.zeros_like, weights))
        (loss, grad_w), grad_x = _jax.lax.scan(body, init, (per_example, given["loss_target"]))
    with _jax.named_scope("update"):
        delta_w, new_m, new_v = {}, {}, {}
        for n in TWIN_WEIGHTS:
            delta_w[n], new_m[n], new_v[n] = _adamw(weights[n], grad_w[n], given["m_" + n], given["v_" + n])
    return (loss, grad_x, *[grad_w[n] for n in TWIN_WEIGHTS], *[delta_w[n] for n in TWIN_WEIGHTS],
            *[new_m[n] for n in TWIN_WEIGHTS], *[new_v[n] for n in TWIN_WEIGHTS])
```

```python
import functools
import math

import numpy as np
import jax
import jax.numpy as jnp
from jax import lax
from jax.experimental import pallas as pl
from jax.experimental.pallas import tpu as pltpu

F32, BF16 = jnp.float32, jnp.bfloat16
MESH = pl.DeviceIdType.MESH

D_MODEL = 1024
HEAD_DIM = 64
N_HEADS = 8
BR_W = 512
MEM_HEADS = 4
MEM_HEAD_DIM = 128
D_FF = 2816
NORM_EPS = 1e-6
GN_EPS = 64e-5
N_CHIPS = 4
N_DEV = 8
LANES = 128
VMEM_LIMIT = 48 * 1024 * 1024

ADAM_LR, ADAM_B1, ADAM_B2, ADAM_EPS, ADAM_WD, ADAM_STEP = 0.001, 0.9, 0.999, 1e-08, 0.01, 10

FOX_COLS = 3 * BR_W + N_HEADS
RWKV_COLS = 3 * BR_W + 64 + 64 + 128
COL_QKV = (0, 3 * BR_W)
COL_F = (3 * BR_W, FOX_COLS)
COL_RW = (FOX_COLS, FOX_COLS + RWKV_COLS)
COL_MQ = (COL_RW[1], COL_RW[1] + BR_W)
COL_GATE = (COL_MQ[1], COL_MQ[1] + 3 * D_MODEL)

NT_DIMS = (((1,), (1,)), ((), ()))
TN_DIMS = (((0,), (0,)), ((), ()))


def _params(sem=None, **kw):
    return pltpu.CompilerParams(dimension_semantics=sem, vmem_limit_bytes=VMEM_LIMIT, **kw)


def _sigmoid(x):
    return 1.0 / (1.0 + jnp.exp(-x))


def _log_sigmoid(x):
    return jnp.minimum(x, 0.0) - jnp.log(1.0 + jnp.exp(-jnp.abs(x)))


def _bdot(a, b, dims=None):
    a, b = a.astype(BF16), b.astype(BF16)
    if dims is None:
        return jnp.dot(a, b, preferred_element_type=F32)
    return lax.dot_general(a, b, dims, preferred_element_type=F32)


def _hdot(a, b):
    return jnp.dot(a, b, precision=lax.Precision.HIGHEST, preferred_element_type=F32)


def _tile(n, cap):
    best = None
    for t in range(LANES, min(n, cap) + 1, LANES):
        if n % t == 0:
            best = t
    return best or n


def _rowwise(name, fn, rows, consts, outs, accs=(), tm=256):
    T = rows[0].shape[0]
    tm = min(tm, T)
    assert T % tm == 0
    nr, nc, no, na = len(rows), len(consts), len(outs), len(accs)

    def body(*refs):
        res = fn(*[r[...] for r in refs[:nr + nc]])
        if not isinstance(res, (tuple, list)):
            res = (res,)
        orefs, arefs = refs[nr + nc:nr + nc + no], refs[nr + nc + no:]
        for ref, val in zip(orefs, res[:no]):
            ref[...] = val.astype(ref.dtype)
        if na:
            @pl.when(pl.program_id(0) == 0)
            def _():
                for ref in arefs:
                    ref[...] = jnp.zeros(ref.shape, ref.dtype)
            for ref, val in zip(arefs, res[no:]):
                ref[...] += val

    in_specs = ([pl.BlockSpec((tm, r.shape[1]), lambda i: (i, 0)) for r in rows]
                + [pl.BlockSpec(c.shape, lambda i: (0, 0)) for c in consts])
    out_specs = ([pl.BlockSpec((tm, w), lambda i: (i, 0)) for w, _ in outs]
                 + [pl.BlockSpec(s, lambda i: (0, 0)) for s, _ in accs])
    out_shape = ([jax.ShapeDtypeStruct((T, w), dt) for w, dt in outs]
                 + [jax.ShapeDtypeStruct(s, dt) for s, dt in accs])
    return pl.pallas_call(
        body, grid=(T // tm,), in_specs=in_specs, out_specs=out_specs, out_shape=out_shape, name=name,
        compiler_params=_params(("arbitrary",) if na else ("parallel",)),
    )(*rows, *consts)


def _mm(name, a, b, ta=False, tb=False, out_dtype=F32, add=None):
    M, K = (a.shape[1], a.shape[0]) if ta else a.shape
    K2, N = (b.shape[1], b.shape[0]) if tb else b.shape
    assert K == K2
    tm = _tile(M, 512) if ta else min(M, 512)
    tn, tk = _tile(N, 512), _tile(K, 512)
    assert M % tm == 0
    nk = K // tk
    a_dim, b_dim = (0 if ta else 1), (1 if tb else 0)

    def body(*refs):
        a_ref, b_ref = refs[0], refs[1]
        o_ref, acc = refs[-2], refs[-1]
        k = pl.program_id(2)

        @pl.when(k == 0)
        def _():
            acc[...] = jnp.zeros(acc.shape, F32)

        acc[...] += lax.dot_general(a_ref[...].astype(BF16), b_ref[...].astype(BF16),
                                    (((a_dim,), (b_dim,)), ((), ())), preferred_element_type=F32)

        @pl.when(k == nk - 1)
        def _():
            r = acc[...]
            if add is not None:
                r = r + refs[2][...].astype(F32)
            o_ref[...] = r.astype(o_ref.dtype)

    a_spec = pl.BlockSpec((tk, tm), lambda i, j, k: (k, i)) if ta else pl.BlockSpec((tm, tk), lambda i, j, k: (i, k))
    b_spec = pl.BlockSpec((tn, tk), lambda i, j, k: (j, k)) if tb else pl.BlockSpec((tk, tn), lambda i, j, k: (k, j))
    o_spec = pl.BlockSpec((tm, tn), lambda i, j, k: (i, j))
    ins, in_specs = [a, b], [a_spec, b_spec]
    if add is not None:
        ins.append(add)
        in_specs.append(o_spec)
    return pl.pallas_call(
        body, grid=(M // tm, N // tn, nk), in_specs=in_specs, out_specs=o_spec,
        out_shape=jax.ShapeDtypeStruct((M, N), out_dtype), scratch_shapes=[pltpu.VMEM((tm, tn), F32)], name=name,
        compiler_params=_params(("parallel", "parallel", "arbitrary")),
    )(*ins)


def _rowsum(x):
    return jnp.sum(x, axis=0, keepdims=True)


def _rms_stat(x):
    return lax.rsqrt(jnp.mean(x * x, axis=-1, keepdims=True) + NORM_EPS)


def _rms_bwd(dy, x, g):
    r = _rms_stat(x)
    xn = x * r
    dxn = dy * g
    dx = r * (dxn - xn * jnp.mean(dxn * xn, axis=-1, keepdims=True))
    return dx, _rowsum(dy * xn)


def _fox_c_fwd(f8t, bias_col, tc=256):
    B, H, S = f8t.shape
    tc = min(tc, S)

    def body(f_ref, b_ref, c_ref, carry):
        @pl.when(pl.program_id(1) == 0)
        def _():
            carry[...] = jnp.zeros(carry.shape, F32)
        lf = _log_sigmoid(f_ref[...] + b_ref[...])
        row = lax.broadcasted_iota(jnp.int32, (tc, tc), 0)
        col = lax.broadcasted_iota(jnp.int32, (tc, tc), 1)
        c = _hdot(lf, (row <= col).astype(F32)) + carry[...]
        c_ref[...] = c
        carry[...] = c[:, tc - 1:tc]

    return pl.pallas_call(
        body, grid=(B, S // tc),
        in_specs=[pl.BlockSpec((None, H, tc), lambda b, i: (b, 0, i)), pl.BlockSpec((H, 1), lambda b, i: (0, 0))],
        out_specs=pl.BlockSpec((None, H, tc), lambda b, i: (b, 0, i)),
        out_shape=jax.ShapeDtypeStruct((B, H, S), F32), scratch_shapes=[pltpu.VMEM((H, 1), F32)], name="fox_c_fwd",
        compiler_params=_params(("parallel", "arbitrary")),
    )(f8t, bias_col)


def _fox_c_bwd(dc, f8t, bias_col, tc=256):
    B, H, S = f8t.shape
    tc = min(tc, S)
    n = S // tc

    def body(dc_ref, f_ref, b_ref, df_ref, db_ref, carry):
        @pl.when(pl.program_id(1) == 0)
        def _():
            carry[...] = jnp.zeros(carry.shape, F32)
            db_ref[...] = jnp.zeros(db_ref.shape, F32)
        row = lax.broadcasted_iota(jnp.int32, (tc, tc), 0)
        col = lax.broadcasted_iota(jnp.int32, (tc, tc), 1)
        dlf = _hdot(dc_ref[...], (row >= col).astype(F32)) + carry[...]
        z = f_ref[...] + b_ref[...]
        df = dlf * (1.0 - _sigmoid(z))
        df_ref[...] = df
        db_ref[...] += jnp.sum(df, axis=1, keepdims=True)
        carry[...] = dlf[:, 0:1]

    rev = lambda b, i: (b, 0, n - 1 - i)
    return pl.pallas_call(
        body, grid=(B, n),
        in_specs=[pl.BlockSpec((None, H, tc), rev), pl.BlockSpec((None, H, tc), rev), pl.BlockSpec((H, 1), lambda b, i: (0, 0))],
        out_specs=[pl.BlockSpec((None, H, tc), rev), pl.BlockSpec((None, H, 1), lambda b, i: (b, 0, 0))],
        out_shape=[jax.ShapeDtypeStruct((B, H, S), F32), jax.ShapeDtypeStruct((B, H, 1), F32)],
        scratch_shapes=[pltpu.VMEM((H, 1), F32)], name="fox_c_bwd",
        compiler_params=_params(("parallel", "arbitrary")),
    )(dc, f8t, bias_col)


NEG_BIG = -1e30


def _fox_logits(q, kj, cq, ckj, i, j, t, scale):
    s = _bdot(q, kj, NT_DIMS) * scale + (cq - ckj)
    row = lax.broadcasted_iota(jnp.int32, (t, t), 0)
    col = lax.broadcasted_iota(jnp.int32, (t, t), 1)
    return s, col <= row + (i - j) * t


def _fox_fwd(q, k, v, cq, ck, t):
    G, n = q.shape[0], q.shape[1]
    scale = HEAD_DIM ** -0.5

    def body(q_ref, k_ref, v_ref, cq_ref, ck_ref, o_ref, lse_ref):
        i = pl.program_id(1)
        qi, cqi = q_ref[...], cq_ref[...]

        def step(j, carry):
            m, l, acc = carry
            s, ok = _fox_logits(qi, k_ref[j], cqi, ck_ref[j], i, j, t, scale)
            s = jnp.where(ok, s, NEG_BIG)
            m2 = jnp.maximum(m, jnp.max(s, axis=1, keepdims=True))
            p = jnp.exp(s - m2)
            al = jnp.exp(m - m2)
            return m2, al * l + jnp.sum(p, axis=1, keepdims=True), al * acc + _bdot(p, v_ref[j])

        m, l, acc = lax.fori_loop(0, i + 1, step, (jnp.full((t, 1), NEG_BIG, F32), jnp.zeros((t, 1), F32),
                                                   jnp.zeros((t, HEAD_DIM), F32)))
        o_ref[...] = (acc / l).astype(o_ref.dtype)
        lse_ref[...] = m + jnp.log(l)

    blk = pl.BlockSpec((None, None, t, HEAD_DIM), lambda g, i: (g, i, 0, 0))
    full = pl.BlockSpec((None, n, t, HEAD_DIM), lambda g, i: (g, 0, 0, 0))
    col = pl.BlockSpec((None, None, t, 1), lambda g, i: (g, i, 0, 0))
    return pl.pallas_call(
        body, grid=(G, n), in_specs=[blk, full, full, col, pl.BlockSpec((None, n, 1, t), lambda g, i: (g, 0, 0, 0))],
        out_specs=[blk, col],
        out_shape=[jax.ShapeDtypeStruct(q.shape, BF16), jax.ShapeDtypeStruct((G, n, t, 1), F32)], name="fox_fwd",
        compiler_params=_params(("parallel", "parallel")),
    )(q, k, v, cq, ck)


def _fox_bwd(q, k, v, o, do, cq, ck, lse, t):
    G, n = q.shape[0], q.shape[1]
    scale = HEAD_DIM ** -0.5

    def body(q_ref, k_ref, v_ref, o_ref, do_ref, cq_ref, ck_ref, lse_ref, dq_ref, dk_ref, dv_ref, dck_ref, dcq_ref):
        dk_ref[...] = jnp.zeros(dk_ref.shape, F32)
        dv_ref[...] = jnp.zeros(dv_ref.shape, F32)
        dck_ref[...] = jnp.zeros(dck_ref.shape, F32)

        def qloop(i, _):
            qi, doi, cqi, lsei = q_ref[i], do_ref[i], cq_ref[i], lse_ref[i]
            delta = jnp.sum(doi.astype(F32) * o_ref[i].astype(F32), axis=1, keepdims=True)

            def kloop(j, carry):
                dq, dcq = carry
                kj, vj = k_ref[j], v_ref[j]
                s, ok = _fox_logits(qi, kj, cqi, ck_ref[j], i, j, t, scale)
                p = jnp.where(ok, jnp.exp(s - lsei), 0.0)
                ds = p * (_bdot(doi, vj, NT_DIMS) - delta)
                dv_ref[j] += _bdot(p, doi, TN_DIMS)
                dk_ref[j] += _bdot(ds, qi, TN_DIMS) * scale
                dck_ref[j] += -_rowsum(ds)
                return dq + _bdot(ds, kj) * scale, dcq + jnp.sum(ds, axis=1, keepdims=True)

            dq_ref[i], dcq_ref[i] = lax.fori_loop(0, i + 1, kloop, (jnp.zeros((t, HEAD_DIM), F32), jnp.zeros((t, 1), F32)))
            return 0

        lax.fori_loop(0, n, qloop, 0)

    full = pl.BlockSpec((None, n, t, HEAD_DIM), lambda g: (g, 0, 0, 0))
    col = pl.BlockSpec((None, n, t, 1), lambda g: (g, 0, 0, 0))
    row = pl.BlockSpec((None, n, 1, t), lambda g: (g, 0, 0, 0))
    f32s = jax.ShapeDtypeStruct(q.shape, F32)
    return pl.pallas_call(
        body, grid=(G,), in_specs=[full, full, full, full, full, col, row, col], out_specs=[full, full, full, row, col],
        out_shape=[f32s, f32s, f32s, jax.ShapeDtypeStruct((G, n, 1, t), F32), jax.ShapeDtypeStruct((G, n, t, 1), F32)],
        name="fox_bwd", compiler_params=_params(("parallel",)),
    )(q, k, v, o, do, cq, ck, lse)


def _mem_probs(qh, kh):
    s = _bdot(qh, kh, NT_DIMS) * (MEM_HEAD_DIM ** -0.5)
    e = jnp.exp(s - jnp.max(s, axis=1, keepdims=True))
    return e / jnp.sum(e, axis=1, keepdims=True)


def _mem_fwd(q, mem_kv, B, tq=512):
    T = q.shape[0]
    S, Lm = T // B, mem_kv.shape[0] // B
    tq = min(tq, S)
    n = S // tq

    def body(q_ref, k_ref, v_ref, o_ref):
        for h in range(MEM_HEADS):
            sl = slice(h * MEM_HEAD_DIM, (h + 1) * MEM_HEAD_DIM)
            p = _mem_probs(q_ref[:, sl], k_ref[:, sl])
            o_ref[:, sl] = _bdot(p, v_ref[:, sl]).astype(o_ref.dtype)

    qs = pl.BlockSpec((tq, BR_W), lambda b, i: (b * n + i, 0))
    return pl.pallas_call(
        body, grid=(B, n),
        in_specs=[qs, pl.BlockSpec((Lm, BR_W), lambda b, i: (b, 0)), pl.BlockSpec((Lm, BR_W), lambda b, i: (b, 1))],
        out_specs=qs, out_shape=jax.ShapeDtypeStruct((T, BR_W), BF16), name="mem_fwd",
        compiler_params=_params(("parallel", "parallel")),
    )(q, mem_kv, mem_kv)


def _mem_bwd(q, mem_kv, do, B, tq=512):
    T = q.shape[0]
    S, Lm = T // B, mem_kv.shape[0] // B
    tq = min(tq, S)
    n = S // tq
    scale = MEM_HEAD_DIM ** -0.5

    def body(q_ref, k_ref, v_ref, do_ref, dq_ref, dk_ref, dv_ref):
        @pl.when(pl.program_id(1) == 0)
        def _():
            dk_ref[...] = jnp.zeros(dk_ref.shape, F32)
            dv_ref[...] = jnp.zeros(dv_ref.shape, F32)
        for h in range(MEM_HEADS):
            sl = slice(h * MEM_HEAD_DIM, (h + 1) * MEM_HEAD_DIM)
            qh, kh, vh, doh = q_ref[:, sl], k_ref[:, sl], v_ref[:, sl], do_ref[:, sl]
            p = _mem_probs(qh, kh)
            dp = _bdot(doh, vh, NT_DIMS)
            ds = p * (dp - jnp.sum(p * dp, axis=1, keepdims=True))
            dq_ref[:, sl] = (_bdot(ds, kh) * scale).astype(dq_ref.dtype)
            dk_ref[:, sl] += _bdot(ds, qh, TN_DIMS) * scale
            dv_ref[:, sl] += _bdot(p, doh, TN_DIMS)

    qs = pl.BlockSpec((tq, BR_W), lambda b, i: (b * n + i, 0))
    kv = pl.BlockSpec((Lm, BR_W), lambda b, i: (b, 0))
    return pl.pallas_call(
        body, grid=(B, n),
        in_specs=[qs, kv, pl.BlockSpec((Lm, BR_W), lambda b, i: (b, 1)), qs], out_specs=[qs, kv, kv],
        out_shape=[jax.ShapeDtypeStruct((T, BR_W), BF16), jax.ShapeDtypeStruct((B * Lm, BR_W), F32),
                   jax.ShapeDtypeStruct((B * Lm, BR_W), F32)], name="mem_bwd",
        compiler_params=_params(("parallel", "arbitrary")),
    )(q, mem_kv, mem_kv, do)


def _head_ones():
    h = np.arange(BR_W) // HEAD_DIM
    return jnp.asarray((h[:, None] == h[None, :]).astype(np.float32))


def _rw_prep(p, pp, mu, w0, w1, a0, w2, g_up, k_k, k_a, bd):
    ps = p + (pp - p) * mu
    r, k, v = ps[:, 0:512], ps[:, 512:1024], ps[:, 1024:1536]
    wa, gd = ps[:, 1536:1664], ps[:, 1664:1792]
    th = jnp.tanh(wa)
    z = w0 + _bdot(th, w1)
    wl = -jnp.exp(_log_sigmoid(z) - 0.5)
    w = jnp.exp(wl)
    a = _sigmoid(a0 + _bdot(wa, w2))
    sg = _sigmoid(gd)
    g = _bdot(sg, g_up)
    kq = k * k_k
    n2 = _hdot(kq * kq, bd)
    inv = lax.rsqrt(jnp.maximum(n2, 1e-24))
    kk = kq * inv
    k2 = k * (1.0 + (a - 1.0) * k_a)
    return dict(ps=ps, r=r, k=k, v=v, wa=wa, th=th, z=z, wl=wl, w=w, a=a, sg=sg, g=g, kq=kq, n2=n2, inv=inv, kk=kk, k2=k2)


def _rw_prep_fwd(p, pp, consts):
    def fn(p, pp, *c):
        t = _rw_prep(p, pp, *c)
        return t["r"], t["w"], t["k2"], t["v"], -t["kk"], t["kk"] * t["a"], t["g"]
    return _rowwise("rwkv_prep_fwd", fn, [p, pp], consts, [(BR_W, F32)] * 7, tm=256)


def _rw_prep_bwd(p, pp, cots, consts):
    def fn(p, pp, dr1, dr2, dw, dk21, dk22, dv1, dv2, dav, dbv, dg, mu, w0, w1, a0, w2, g_up, k_k, k_a, bd):
        t = _rw_prep(p, pp, mu, w0, w1, a0, w2, g_up, k_k, k_a, bd)
        dr, dk2, dv = dr1 + dr2, dk21 + dk22, dv1 + dv2
        a, k, kk, kq, inv = t["a"], t["k"], t["kk"], t["kq"], t["inv"]
        dkk = dbv * a - dav
        da = dbv * kk + dk2 * k * k_a
        dk = dk2 * (1.0 + (a - 1.0) * k_a)
        d_k_a = _rowsum(dk2 * k * (a - 1.0))
        proj = _hdot(dkk * kq, bd)
        dkq = dkk * inv - jnp.where(t["n2"] > 1e-24, kq * inv * inv * inv * proj, 0.0)
        dk = dk + dkq * k_k
        d_k_k = _rowsum(dkq * k)
        dpa = da * a * (1.0 - a)
        d_a0 = _rowsum(dpa)
        dwa = _bdot(dpa, w2, NT_DIMS)
        d_w2 = _bdot(t["wa"], dpa, TN_DIMS)
        dz = dw * t["w"] * t["wl"] * (1.0 - _sigmoid(t["z"]))
        d_w0 = _rowsum(dz)
        th = t["th"]
        dwa = dwa + _bdot(dz, w1, NT_DIMS) * (1.0 - th * th)
        d_w1 = _bdot(th, dz, TN_DIMS)
        sg = t["sg"]
        dgd = _bdot(dg, g_up, NT_DIMS) * sg * (1.0 - sg)
        d_g_up = _bdot(sg, dg, TN_DIMS)
        dps = jnp.concatenate([dr, dk, dv, dwa, dgd], axis=1)
        d_mu = _rowsum(dps * (pp - p))
        return dps * (1.0 - mu), dps * mu, d_mu, d_w0, d_w1, d_a0, d_w2, d_g_up, d_k_k, d_k_a

    accs = [((1, RWKV_COLS), F32), ((1, BR_W), F32), ((LANES, BR_W), F32), ((1, BR_W), F32), ((LANES, BR_W), F32),
            ((LANES, BR_W), F32), ((1, BR_W), F32), ((1, BR_W), F32)]
    return _rowwise("rwkv_prep_bwd", fn, [p, pp] + list(cots), consts, [(RWKV_COLS, F32)] * 2, accs, tm=128)


def _rw_head(y, r, k2, v, g, gn_g, gn_b, r_k, bd):
    mean = _hdot(y, bd) * (1.0 / HEAD_DIM)
    yc = y - mean
    rs = lax.rsqrt(_hdot(yc * yc, bd) * (1.0 / HEAD_DIM) + GN_EPS)
    yn = yc * rs
    bs = _hdot(r * k2 * r_k, bd)
    return yn, rs, bs, yn * gn_g + gn_b + bs * v


def _rw_head_fwd(y, r, k2, v, g, consts):
    def fn(y, r, k2, v, g, *c):
        return _rw_head(y, r, k2, v, g, *c)[3] * g
    return _rowwise("rwkv_head_fwd", fn, [y, r, k2, v, g], consts, [(BR_W, BF16)])[0]


def _rw_head_bwd(dout, y, r, k2, v, g, consts):
    def fn(dout, y, r, k2, v, g, gn_g, gn_b, r_k, bd):
        dout = dout.astype(F32)
        yn, rs, bs, zz = _rw_head(y, r, k2, v, g, gn_g, gn_b, r_k, bd)
        dg = dout * zz
        dz = dout * g
        dyn = dz * gn_g
        inv_n = 1.0 / HEAD_DIM
        dy = rs * (dyn - _hdot(dyn, bd) * inv_n - yn * (_hdot(dyn * yn, bd) * inv_n))
        dq = _hdot(dz * v, bd)
        return dy, dg, dq * k2 * r_k, dq * r * r_k, dz * bs, _rowsum(dz * yn), _rowsum(dz), _rowsum(dq * r * k2)
    return _rowwise("rwkv_head_bwd", fn, [dout, y, r, k2, v, g], consts, [(BR_W, F32)] * 5, [((1, BR_W), F32)] * 3)


SCAN_TC = 32


def _halves(col128, lo_mask):
    return jnp.where(lo_mask, col128[0:HEAD_DIM], col128[HEAD_DIM:2 * HEAD_DIM])


def _seg_rowsum(x, lo_mask):
    return (jnp.sum(jnp.where(lo_mask, x, 0.0), axis=1, keepdims=True),
            jnp.sum(jnp.where(lo_mask, 0.0, x), axis=1, keepdims=True))


def _rw_scan_fwd(r, w, k, a, b, vT, P=1):
    NP, S, _ = r.shape
    Tc = vT.shape[-1]
    nc = S // Tc

    def body(r_ref, w_ref, k_ref, a_ref, b_ref, vT_ref, yT_ref, saT_ref, sb_ref, st):
        @pl.when(pl.program_id(1) == 0)
        def _():
            st[...] = jnp.zeros(st.shape, F32)
        lo = lax.broadcasted_iota(jnp.int32, (HEAD_DIM, LANES), 1) < HEAD_DIM
        s = [st[p] for p in range(P)]
        for t in range(Tc):
            for p in range(P):
                row = lambda ref: ref[p, t:t + 1, :]
                sb_ref[p, t] = s[p]
                sa0, sa1 = _seg_rowsum(s[p] * row(a_ref), lo)
                sab = jnp.where(lo, sa0, sa1)
                vb = _halves(vT_ref[p, :, t:t + 1], lo)
                s[p] = s[p] * row(w_ref) + sab * row(b_ref) + vb * row(k_ref)
                y0, y1 = _seg_rowsum(s[p] * row(r_ref), lo)
                yT_ref[p, :, t:t + 1] = jnp.concatenate([y0, y1], axis=0)
                saT_ref[p, :, t:t + 1] = jnp.concatenate([sa0, sa1], axis=0)
        for p in range(P):
            st[p] = s[p]

    rows = pl.BlockSpec((P, Tc, LANES), lambda g, c: (g, c, 0))
    cols = pl.BlockSpec((P, None, LANES, Tc), lambda g, c: (g, c, 0, 0))
    colshape = jax.ShapeDtypeStruct((NP, nc, LANES, Tc), F32)
    return pl.pallas_call(
        body, grid=(NP // P, nc), in_specs=[rows] * 5 + [cols],
        out_specs=[cols, cols, pl.BlockSpec((P, Tc, HEAD_DIM, LANES), lambda g, c: (g, c, 0, 0))],
        out_shape=[colshape, colshape, jax.ShapeDtypeStruct((NP, S, HEAD_DIM, LANES), F32)],
        scratch_shapes=[pltpu.VMEM((P, HEAD_DIM, LANES), F32)], name="rwkv_scan_fwd",
        compiler_params=_params(("parallel", "arbitrary")),
    )(r, w, k, a, b, vT)


def _rw_scan_bwd(r, w, k, a, b, vT, saT, dyT, sb, P=1):
    NP, S, _ = r.shape
    Tc = vT.shape[-1]
    nc = S // Tc

    def body(r_ref, w_ref, k_ref, a_ref, b_ref, vT_ref, saT_ref, dyT_ref, sb_ref,
             dr_ref, dw_ref, dk_ref, da_ref, db_ref, dvT_ref, dst):
        @pl.when(pl.program_id(1) == 0)
        def _():
            dst[...] = jnp.zeros(dst.shape, F32)
        lo = lax.broadcasted_iota(jnp.int32, (HEAD_DIM, LANES), 1) < HEAD_DIM
        ds = [dst[p] for p in range(P)]
        for t in reversed(range(Tc)):
            for p in range(P):
                row = lambda ref: ref[p, t:t + 1, :]
                col = lambda ref: _halves(ref[p, :, t:t + 1], lo)
                sp = sb_ref[p, t]
                sab, vb, dyb = col(saT_ref), col(vT_ref), col(dyT_ref)
                wr, kr, ar, br = row(w_ref), row(k_ref), row(a_ref), row(b_ref)
                s_t = sp * wr + sab * br + vb * kr
                dr_ref[p, t:t + 1, :] = _rowsum(s_t * dyb)
                d = ds[p] + dyb * row(r_ref)
                dv0, dv1 = _seg_rowsum(d * kr, lo)
                dvT_ref[p, :, t:t + 1] = jnp.concatenate([dv0, dv1], axis=0)
                dk_ref[p, t:t + 1, :] = _rowsum(d * vb)
                dsa0, dsa1 = _seg_rowsum(d * br, lo)
                dsab = jnp.where(lo, dsa0, dsa1)
                db_ref[p, t:t + 1, :] = _rowsum(d * sab)
                dw_ref[p, t:t + 1, :] = _rowsum(d * sp)
                da_ref[p, t:t + 1, :] = _rowsum(sp * dsab)
                ds[p] = d * wr + dsab * ar
        for p in range(P):
            dst[p] = ds[p]

    rows = pl.BlockSpec((P, Tc, LANES), lambda g, c: (g, nc - 1 - c, 0))
    cols = pl.BlockSpec((P, None, LANES, Tc), lambda g, c: (g, nc - 1 - c, 0, 0))
    rowshape = jax.ShapeDtypeStruct((NP, S, LANES), F32)
    return pl.pallas_call(
        body, grid=(NP // P, nc),
        in_specs=[rows] * 5 + [cols] * 3 + [pl.BlockSpec((P, Tc, HEAD_DIM, LANES), lambda g, c: (g, nc - 1 - c, 0, 0))],
        out_specs=[rows] * 5 + [cols],
        out_shape=[rowshape] * 5 + [jax.ShapeDtypeStruct((NP, nc, LANES, Tc), F32)],
        scratch_shapes=[pltpu.VMEM((P, HEAD_DIM, LANES), F32)], name="rwkv_scan_bwd",
        compiler_params=_params(("parallel", "arbitrary")),
    )(r, w, k, a, b, vT, saT, dyT, sb)


def _to_pairs(x, B):
    T = x.shape[0]
    return x.reshape(B, T // B, 4, LANES).transpose(0, 2, 1, 3).reshape(B * 4, T // B, LANES)


def _from_pairs(x, B):
    NP, S, _ = x.shape
    return x.reshape(B, 4, S, LANES).transpose(0, 2, 1, 3).reshape(B * S, BR_W)


def _to_cols(xp, Tc):
    NP, S, _ = xp.shape
    return xp.reshape(NP, S // Tc, Tc, LANES).swapaxes(2, 3)


def _from_cols(xc):
    NP, nc, _, Tc = xc.shape
    return xc.swapaxes(2, 3).reshape(NP, nc * Tc, LANES)


def _shift_prev(p, B):
    T, W = p.shape
    return jnp.pad(p.reshape(B, T // B, W), ((0, 0), (1, 0), (0, 0)))[:, :-1].reshape(T, W)


def _shift_next(p, B):
    T, W = p.shape
    return jnp.pad(p.reshape(B, T // B, W), ((0, 0), (0, 1), (0, 0)))[:, 1:].reshape(T, W)


def _heads_split(x, B, t):
    T = x.shape[0]
    S = T // B
    return x.reshape(B, S, N_HEADS, HEAD_DIM).transpose(0, 2, 1, 3).reshape(B * N_HEADS, S // t, t, HEAD_DIM)


def _heads_merge(x, B):
    G, n, t, _ = x.shape
    return x.reshape(B, N_HEADS, n * t, HEAD_DIM).transpose(0, 2, 1, 3).reshape(B * n * t, BR_W)


def _layer_step(x, mem, target, W, scan_tc=SCAN_TC, fox_t=256):
    B, S, _ = x.shape
    T = B * S
    x2, tgt2 = x.reshape(T, D_MODEL), target.reshape(T, D_MODEL)
    mem2 = mem.reshape(-1, D_MODEL)
    w_in = W["w_in"]
    w_qkv, w_rw, w_mq, w_gate = (w_in[:, lo:hi] for lo, hi in (COL_QKV, COL_RW, COL_MQ, COL_GATE))
    w_f = jnp.pad(w_in[:, COL_F[0]:COL_F[1]], ((0, 0), (0, LANES - N_HEADS)))
    row = lambda v: v.reshape(1, -1).astype(F32)
    pre1_g, post1_g, pre2_g, post2_g, mem_g = (row(W[n]) for n in ("pre1_g", "post1_g", "pre2_g", "post2_g", "mem_norm_g"))

    u = _rowwise("rms_pre1", lambda x, g: x * _rms_stat(x) * g, [x2], [pre1_g], [(D_MODEL, BF16)])[0]
    qkv = _mm("proj_qkv", u, w_qkv, out_dtype=BF16)
    f_pad = _mm("proj_f", u, w_f)
    p_rw = _mm("proj_rwkv", u, w_rw)
    memq = _mm("proj_memq", u, w_mq, out_dtype=BF16)
    gate = _mm("proj_gate", u, w_gate)

    t = min(fox_t, S)
    bias_col = W["fox_f_bias"].reshape(N_HEADS, 1).astype(F32)
    f8t = f_pad[:, :N_HEADS].reshape(B, S, N_HEADS).transpose(0, 2, 1)
    c = _fox_c_fwd(f8t, bias_col)
    G = B * N_HEADS
    cq, ck = c.reshape(G, S // t, t, 1), c.reshape(G, S // t, 1, t)
    fq, fk, fv = (_heads_split(qkv[:, i * BR_W:(i + 1) * BR_W], B, t) for i in range(3))
    fo4, lse = _fox_fwd(fq, fk, fv, cq, ck, t)
    fox_out = _heads_merge(fo4, B)

    bd = _head_ones()
    zpad = jnp.zeros((64, BR_W), F32)
    w1 = jnp.concatenate([W["rwkv_w_up"].astype(F32), zpad], axis=0)
    w2 = jnp.concatenate([zpad, W["rwkv_a_up"].astype(F32)], axis=0)
    prep_consts = [row(W["rwkv_mu"]), row(W["rwkv_w0"]), w1, row(W["rwkv_a0"]), w2, W["rwkv_g_up"].astype(F32),
                   row(W["rwkv_k_k"]), row(W["rwkv_k_a"]), bd]
    p_prev = _shift_prev(p_rw, B)
    rr, rw, rk2, rv, rav, rbv, rg = _rw_prep_fwd(p_rw, p_prev, prep_consts)
    scan_rows = [_to_pairs(v, B) for v in (rr, rw, rk2, rav, rbv)]
    vT = _to_cols(_to_pairs(rv, B), scan_tc)
    yT, saT, sb = _rw_scan_fwd(*scan_rows, vT)
    ry = _from_pairs(_from_cols(yT), B)
    head_consts = [row(W["rwkv_gn_g"]), row(W["rwkv_gn_b"]), row(W["rwkv_r_k"]), bd]
    rwkv_out = _rw_head_fwd(ry, rr, rk2, rv, rg, head_consts)

    mn = _rowwise("rms_mem", lambda m, g: m * _rms_stat(m) * g, [mem2], [mem_g], [(D_MODEL, BF16)])[0]
    mem_kv = _mm("proj_memkv", mn, W["w_mem_kv"], out_dtype=BF16)
    mem_out = _mem_fwd(memq, mem_kv, B)

    fo = [_mm("branch_" + n, a, W[n]) for n, a in (("w_fox_out", fox_out), ("w_rwkv_out", rwkv_out), ("w_mem_out", mem_out))]

    def merge(gate, f0, f1, f2):
        return sum(_sigmoid(gate[:, i * D_MODEL:(i + 1) * D_MODEL]) * f for i, f in enumerate((f0, f1, f2)))
    merged = _rowwise("merge", merge, [gate] + fo, [], [(D_MODEL, BF16)])[0]
    y1 = _mm("proj_o", merged, W["w_o"])

    def mid(x, y1, g1, g2):
        h1 = x + y1 * _rms_stat(y1) * g1
        return h1, h1 * _rms_stat(h1) * g2
    h1, u2 = _rowwise("norm_mid", mid, [x2, y1], [post1_g, pre2_g], [(D_MODEL, F32), (D_MODEL, BF16)])
    gt = _mm("ffn_gate", u2, W["w_ffn_gate"])
    up = _mm("ffn_up", u2, W["w_ffn_up"])
    act = _rowwise("swiglu", lambda gt, up: gt * _sigmoid(gt) * up, [gt, up], [], [(D_FF, BF16)])[0]
    ffn = _mm("ffn_down", act, W["w_ffn_down"])

    def tail(h1, ffn, tgt, g):
        err = h1 + ffn * _rms_stat(ffn) * g - tgt
        dh2 = err * (1.0 / D_MODEL)
        dffn, dg = _rms_bwd(dh2, ffn, g)
        loss = 0.5 * jnp.sum(jnp.sum(err * err, axis=1, keepdims=True) * (1.0 / D_MODEL), axis=0, keepdims=True)
        return dh2, dffn, dg, jnp.broadcast_to(loss, (1, LANES))
    dh2, dffn, d_post2, loss = _rowwise("loss_tail", tail, [h1, ffn, tgt2], [post2_g], [(D_MODEL, F32), (D_MODEL, BF16)],
                                        [((1, D_MODEL), F32), ((1, LANES), F32)])
    gw = {"post2_g": d_post2}
    dact = _mm("d_act", dffn, W["w_ffn_down"], tb=True)
    gw["w_ffn_down"] = _mm("g_ffn_down", act, dffn, ta=True)

    def swiglu_bwd(dact, gt, up):
        s = _sigmoid(gt)
        return dact * up * s * (1.0 + gt * (1.0 - s)), dact * gt * s
    dgt, dup = _rowwise("swiglu_bwd", swiglu_bwd, [dact, gt, up], [], [(D_FF, BF16)] * 2)
    du2 = _mm("d_u2_gate", dgt, W["w_ffn_gate"], tb=True)
    du2 = _mm("d_u2_up", dup, W["w_ffn_up"], tb=True, add=du2)
    gw["w_ffn_gate"] = _mm("g_ffn_gate", u2, dgt, ta=True)
    gw["w_ffn_up"] = _mm("g_ffn_up", u2, dup, ta=True)

    def mid_bwd(du2, dh2, h1, y1, g1, g2):
        dh1_n, d_pre2 = _rms_bwd(du2, h1, g2)
        dh1 = dh2 + dh1_n
        dy1, d_post1 = _rms_bwd(dh1, y1, g1)
        return dh1, dy1, d_post1, d_pre2
    dh1, dy1, gw["post1_g"], gw["pre2_g"] = _rowwise(
        "norm_mid_bwd", mid_bwd, [du2, dh2, h1, y1], [post1_g, pre2_g], [(D_MODEL, F32), (D_MODEL, BF16)],
        [((1, D_MODEL), F32)] * 2)
    dmerged = _mm("d_merged", dy1, W["w_o"], tb=True)
    gw["w_o"] = _mm("g_w_o", merged, dy1, ta=True)

    def merge_bwd(dm, gate, f0, f1, f2):
        s = [_sigmoid(gate[:, i * D_MODEL:(i + 1) * D_MODEL]) for i in range(3)]
        dgate = jnp.concatenate([dm * f * si * (1.0 - si) for f, si in zip((f0, f1, f2), s)], axis=1)
        return dm * s[0], dm * s[1], dm * s[2], dgate
    dfo0, dfo1, dfo2, dgate = _rowwise("merge_bwd", merge_bwd, [dmerged, gate] + fo, [],
                                       [(D_MODEL, BF16)] * 3 + [(3 * D_MODEL, BF16)])
    d_branch = {}
    for n, a, dfo in (("w_fox_out", fox_out, dfo0), ("w_rwkv_out", rwkv_out, dfo1), ("w_mem_out", mem_out, dfo2)):
        d_branch[n] = _mm("d_in_" + n, dfo, W[n], tb=True, out_dtype=BF16)
        gw[n] = _mm("g_" + n, a, dfo, ta=True)

    dmemq, dkm, dvm = _mem_bwd(memq, mem_kv, d_branch["w_mem_out"], B)
    dmem_kv = jnp.concatenate([dkm, dvm], axis=1)
    gw["w_mem_kv"] = _mm("g_w_mem_kv", mn, dmem_kv, ta=True)
    dmn = _mm("d_mn", dmem_kv, W["w_mem_kv"], tb=True)
    gw["mem_norm_g"] = _rowwise("rms_mem_bwd", lambda d, m, g: _rms_bwd(d, m, g)[1], [dmn, mem2], [mem_g], [],
                                [((1, D_MODEL), F32)])[0]

    do4 = _heads_split(d_branch["w_fox_out"], B, t)
    dq4, dk4, dv4, dck, dcq = _fox_bwd(fq, fk, fv, fo4, do4, cq, ck, lse, t)
    df8t, dbias = _fox_c_bwd(dck.reshape(B, N_HEADS, S) + dcq.reshape(B, N_HEADS, S), f8t, bias_col)
    gw["fox_f_bias"] = jnp.sum(dbias, axis=0).reshape(1, N_HEADS)
    dqkv = jnp.concatenate([_heads_merge(d, B) for d in (dq4, dk4, dv4)], axis=1).astype(BF16)
    df_pad = jnp.pad(df8t.transpose(0, 2, 1).reshape(T, N_HEADS), ((0, 0), (0, LANES - N_HEADS))).astype(BF16)

    dry, drg, dr_h, dk2_h, dv_h, gw["rwkv_gn_g"], gw["rwkv_gn_b"], gw["rwkv_r_k"] = _rw_head_bwd(
        d_branch["w_rwkv_out"], ry, rr, rk2, rv, rg, head_consts)
    dyT = _to_cols(_to_pairs(dry, B), scan_tc)
    dr_s, dw_s, dk2_s, da_s, db_s, dvT = _rw_scan_bwd(*scan_rows, vT, saT, dyT, sb)
    dr_s, dw_s, dk2_s, da_s, db_s = (_from_pairs(v, B) for v in (dr_s, dw_s, dk2_s, da_s, db_s))
    dv_s = _from_pairs(_from_cols(dvT), B)
    dP, dPp, gw["rwkv_mu"], gw["rwkv_w0"], d_w1, gw["rwkv_a0"], d_w2, gw["rwkv_g_up"], gw["rwkv_k_k"], gw["rwkv_k_a"] = \
        _rw_prep_bwd(p_rw, p_prev, [dr_s, dr_h, dw_s, dk2_s, dk2_h, dv_s, dv_h, da_s, db_s, drg], prep_consts)
    gw["rwkv_w_up"], gw["rwkv_a_up"] = d_w1[:64], d_w2[64:]
    dp_rw = (dP + _shift_next(dPp, B)).astype(BF16)

    du = _mm("d_u_qkv", dqkv, w_qkv, tb=True)
    du = _mm("d_u_f", df_pad, w_f, tb=True, add=du)
    du = _mm("d_u_rwkv", dp_rw, w_rw, tb=True, add=du)
    du = _mm("d_u_memq", dmemq, w_mq, tb=True, add=du)
    du = _mm("d_u_gate", dgate, w_gate, tb=True, add=du)
    gw["w_in"] = jnp.concatenate(
        [_mm("g_w_qkv", u, dqkv, ta=True), _mm("g_w_f", u, df_pad, ta=True)[:, :N_HEADS], _mm("g_w_rwkv", u, dp_rw, ta=True),
         _mm("g_w_memq", u, dmemq, ta=True), _mm("g_w_gate", u, dgate, ta=True)], axis=1)

    def pre1_bwd(du, dh1, x, g):
        dx, dg = _rms_bwd(du, x, g)
        return dh1 + dx, dg
    dx, gw["pre1_g"] = _rowwise("rms_pre1_bwd", pre1_bwd, [du, dh1, x2], [pre1_g], [(D_MODEL, F32)], [((1, D_MODEL), F32)])
    return loss[0, 0], dx.reshape(B, S, D_MODEL), gw


COL_SHARDED = ("w_in", "rwkv_w_up", "rwkv_a_up", "rwkv_g_up", "w_fox_out", "w_rwkv_out", "w_mem_out", "w_ffn_gate", "w_ffn_up")
ROW_SHARDED = ("w_mem_kv", "w_o", "w_ffn_down")
BIG = ("w_in", "rwkv_w_up", "rwkv_a_up", "rwkv_g_up", "w_mem_kv", "w_fox_out", "w_rwkv_out", "w_mem_out", "w_o",
       "w_ffn_gate", "w_ffn_up", "w_ffn_down")
SMALL = ("pre1_g", "post1_g", "pre2_g", "post2_g", "mem_norm_g", "fox_f_bias", "rwkv_mu", "rwkv_w0", "rwkv_a0", "rwkv_k_k",
         "rwkv_k_a", "rwkv_r_k", "rwkv_gn_g", "rwkv_gn_b")
WEIGHTS = ("pre1_g", "post1_g", "pre2_g", "post2_g", "mem_norm_g", "w_in", "fox_f_bias", "rwkv_mu", "rwkv_w0", "rwkv_w_up",
           "rwkv_a0", "rwkv_a_up", "rwkv_g_up", "rwkv_k_k", "rwkv_k_a", "rwkv_r_k", "rwkv_gn_g", "rwkv_gn_b", "w_mem_kv",
           "w_fox_out", "w_rwkv_out", "w_mem_out", "w_o", "w_ffn_gate", "w_ffn_up", "w_ffn_down")
HALF_ROW_ALIGN = 1024 * LANES


def _packed_half_len(shard_shapes):
    n = sum(int(np.prod(s)) for s in shard_shapes)
    return -(-n // (2 * HALF_ROW_ALIGN)) * HALF_ROW_ALIGN


def _pack(pieces, half_len, dtype):
    flat = jnp.concatenate([p.reshape(-1).astype(dtype) for p in pieces])
    flat = jnp.pad(flat, (0, 2 * half_len - flat.shape[0]))
    return flat.reshape(2, half_len // LANES, LANES)


def _unpack(packed, shapes):
    flat = packed.reshape(-1)
    out, off = [], 0
    for s in shapes:
        n = int(np.prod(s))
        out.append(flat[off:off + n].reshape(s))
        off += n
    return out


def _to_chip_shards(name, g):
    R, C = g.shape
    if name in COL_SHARDED:
        return g.reshape(R, N_CHIPS, C // N_CHIPS).transpose(1, 0, 2)
    return g.reshape(N_CHIPS, R // N_CHIPS, C)


def _from_chip_shards(name, s):
    _, r, c = s.shape
    if name in COL_SHARDED:
        return s.transpose(1, 0, 2).reshape(r, N_CHIPS * c)
    return s.reshape(N_CHIPS * r, c)


def _my_place():
    return lax.axis_index("x"), lax.axis_index("y"), lax.axis_index("c")


def _other_chips(x, y):
    return [(1 - x, y), (x, 1 - y), (1 - x, 1 - y)]


ANY = pl.BlockSpec(memory_space=pl.ANY)


def _gather_weights(packed):
    _, R, L = packed.shape

    def body(in_ref, out_ref, send_sems, recv_sems, local_sem):
        x, y, c = _my_place()
        chip = lambda px, py: 2 * px + py
        sibling = (x, y, 1 - c)
        others = _other_chips(x, y)

        def copy(k, src, dst, to):
            return pltpu.make_async_remote_copy(src_ref=src, dst_ref=dst, send_sem=send_sems.at[k], recv_sem=recv_sems.at[k],
                                                device_id=to, device_id_type=MESH)

        mine = pltpu.make_async_copy(in_ref, out_ref.at[chip(x, y)], local_sem)
        mine.start()
        sends = [copy(j, in_ref.at[c], out_ref.at[chip(x, y), c], (px, py, c)) for j, (px, py) in enumerate(others)]
        for cp in sends:
            cp.start()
        passed = [copy(3 + j, out_ref.at[chip(px, py), c], out_ref.at[chip(px, py), c], sibling)
                  for j, (px, py) in enumerate(others)]
        for j, (px, py) in enumerate(others):
            copy(j, in_ref.at[c], out_ref.at[chip(px, py), c], (px, py, c)).wait_recv()
            passed[j].start()
        for j, (px, py) in enumerate(others):
            copy(3 + j, in_ref.at[1 - c], out_ref.at[chip(px, py), 1 - c], sibling).wait_recv()
        for cp in sends + passed:
            cp.wait_send()
        mine.wait()

    return pl.pallas_call(
        body, out_shape=jax.ShapeDtypeStruct((N_CHIPS, 2, R, L), packed.dtype), in_specs=[ANY], out_specs=ANY,
        scratch_shapes=[pltpu.SemaphoreType.DMA((6,)), pltpu.SemaphoreType.DMA((6,)), pltpu.SemaphoreType.DMA],
        name="gather_weights",
    )(packed)


def _scatter_grads(parts):
    _, _, R, L = parts.shape

    def body(in_ref, out_ref, send_sems, recv_sems, local_sem):
        x, y, c = _my_place()
        me = 4 * x + 2 * y + c
        peers = [(px, py, pc) for px in range(2) for py in range(2) for pc in range(2)]

        def copy(k, px, py, pc):
            return pltpu.make_async_remote_copy(src_ref=in_ref.at[2 * px + py, pc], dst_ref=out_ref.at[me],
                                                send_sem=send_sems.at[k], recv_sem=recv_sems.at[k],
                                                device_id=(px, py, pc), device_id_type=MESH)

        mine = pltpu.make_async_copy(in_ref.at[2 * x + y, c], out_ref.at[me], local_sem)
        mine.start()
        started = []
        for k in range(1, N_DEV):
            px, py, pc = x ^ (k >> 2), y ^ ((k >> 1) & 1), c ^ (k & 1)
            cp = copy(k - 1, px, py, pc)
            cp.start()
            started.append(cp)
        for cp in started:
            cp.wait()
        mine.wait()

    return pl.pallas_call(
        body, out_shape=jax.ShapeDtypeStruct((N_DEV, R, L), F32), in_specs=[ANY], out_specs=ANY,
        scratch_shapes=[pltpu.SemaphoreType.DMA((7,)), pltpu.SemaphoreType.DMA((7,)), pltpu.SemaphoreType.DMA],
        name="scatter_grads",
    )(parts)


def _sum_devices(parts, name):
    n, R, L = parts.shape
    tr = _tile_rows(R)

    def body(p_ref, o_ref):
        acc = p_ref[0]
        for i in range(1, n):
            acc = acc + p_ref[i]
        o_ref[...] = acc

    return pl.pallas_call(
        body, grid=(R // tr,), in_specs=[pl.BlockSpec((n, tr, L), lambda i: (0, i, 0))],
        out_specs=pl.BlockSpec((tr, L), lambda i: (i, 0)), out_shape=jax.ShapeDtypeStruct((R, L), F32), name=name,
        compiler_params=_params(("parallel",)),
    )(parts)


def _tile_rows(R, cap=2048):
    best = 8
    for t in range(8, min(R, cap) + 1, 8):
        if R % t == 0:
            best = t
    return best if R % 8 == 0 else R


def _swap_halves(half):
    R, L = half.shape

    def body(in_ref, out_ref, send_sem, recv_sem, local_sem):
        x, y, c = _my_place()
        mine = pltpu.make_async_copy(in_ref, out_ref.at[c], local_sem)
        mine.start()
        cp = pltpu.make_async_remote_copy(src_ref=in_ref, dst_ref=out_ref.at[c], send_sem=send_sem, recv_sem=recv_sem,
                                          device_id=(x, y, 1 - c), device_id_type=MESH)
        cp.start()
        cp.wait()
        mine.wait()

    return pl.pallas_call(
        body, out_shape=jax.ShapeDtypeStruct((2, R, L), F32), in_specs=[ANY], out_specs=ANY,
        scratch_shapes=[pltpu.SemaphoreType.DMA, pltpu.SemaphoreType.DMA, pltpu.SemaphoreType.DMA], name="swap_halves",
    )(half)


def _allreduce_small(v):
    R, L = v.shape

    def body(in_ref, out_ref, buf, send_sems, recv_sems):
        x, y, c = _my_place()
        me = 4 * x + 2 * y + c
        buf[me] = in_ref[...]
        started = []
        for k in range(1, N_DEV):
            to = (x ^ (k >> 2), y ^ ((k >> 1) & 1), c ^ (k & 1))
            cp = pltpu.make_async_remote_copy(src_ref=in_ref, dst_ref=buf.at[me], send_sem=send_sems.at[k - 1],
                                              recv_sem=recv_sems.at[k - 1], device_id=to, device_id_type=MESH)
            cp.start()
            started.append(cp)
        for cp in started:
            cp.wait()
        acc = buf[0]
        for i in range(1, N_DEV):
            acc = acc + buf[i]
        out_ref[...] = acc

    vm = pl.BlockSpec(memory_space=pltpu.VMEM)
    return pl.pallas_call(
        body, out_shape=jax.ShapeDtypeStruct((R, L), F32), in_specs=[vm], out_specs=vm,
        scratch_shapes=[pltpu.VMEM((N_DEV, R, L), F32), pltpu.SemaphoreType.DMA((7,)), pltpu.SemaphoreType.DMA((7,))],
        name="allreduce_small",
    )(v)


def _adamw(name, w, g, m, v):
    shape = w.shape
    C = shape[-1]
    R = int(np.prod(shape[:-1]))
    args = [a.reshape(R, C).astype(F32) for a in (w, g, m, v)]
    tr = _tile_rows(R, 256)

    def body(w_ref, g_ref, m_ref, v_ref, d_ref, nm_ref, nv_ref):
        g = g_ref[...]
        m = ADAM_B1 * m_ref[...] + (1.0 - ADAM_B1) * g
        v = ADAM_B2 * v_ref[...] + (1.0 - ADAM_B2) * (g * g)
        m_hat = m / (1.0 - ADAM_B1 ** ADAM_STEP)
        v_hat = v / (1.0 - ADAM_B2 ** ADAM_STEP)
        d_ref[...] = -ADAM_LR * (m_hat / (jnp.sqrt(v_hat) + ADAM_EPS) + ADAM_WD * w_ref[...])
        nm_ref[...] = m
        nv_ref[...] = v

    spec = pl.BlockSpec((tr, C), lambda i: (i, 0))
    out = pl.pallas_call(
        body, grid=(R // tr,), in_specs=[spec] * 4, out_specs=[spec] * 3,
        out_shape=[jax.ShapeDtypeStruct((R, C), F32)] * 3, name="adamw_" + name, compiler_params=_params(("parallel",)),
    )(*args)
    return [o.reshape(shape) for o in out]


def kernel(x, mem, pre1_g, post1_g, pre2_g, post2_g, mem_norm_g, w_in, fox_f_bias, rwkv_mu, rwkv_w0, rwkv_w_up, rwkv_a0, rwkv_a_up, rwkv_g_up, rwkv_k_k, rwkv_k_a, rwkv_r_k, rwkv_gn_g, rwkv_gn_b, w_mem_kv, w_fox_out, w_rwkv_out, w_mem_out, w_o, w_ffn_gate, w_ffn_up, w_ffn_down, loss_target, m_pre1_g, m_post1_g, m_pre2_g, m_post2_g, m_mem_norm_g, m_w_in, m_fox_f_bias, m_rwkv_mu, m_rwkv_w0, m_rwkv_w_up, m_rwkv_a0, m_rwkv_a_up, m_rwkv_g_up, m_rwkv_k_k, m_rwkv_k_a, m_rwkv_r_k, m_rwkv_gn_g, m_rwkv_gn_b, m_w_mem_kv, m_w_fox_out, m_w_rwkv_out, m_w_mem_out, m_w_o, m_w_ffn_gate, m_w_ffn_up, m_w_ffn_down, v_pre1_g, v_post1_g, v_pre2_g, v_post2_g, v_mem_norm_g, v_w_in, v_fox_f_bias, v_rwkv_mu, v_rwkv_w0, v_rwkv_w_up, v_rwkv_a0, v_rwkv_a_up, v_rwkv_g_up, v_rwkv_k_k, v_rwkv_k_a, v_rwkv_r_k, v_rwkv_gn_g, v_rwkv_gn_b, v_w_mem_kv, v_w_fox_out, v_w_rwkv_out, v_w_mem_out, v_w_o, v_w_ffn_gate, v_w_ffn_up, v_w_ffn_down):
    given = dict(locals())
    w_loc = {n: given[n] for n in WEIGHTS}
    m_loc = {n: given["m_" + n] for n in WEIGHTS}
    v_loc = {n: given["v_" + n] for n in WEIGHTS}

    shard_shapes = [w_loc[n].shape[1:] for n in BIG]
    half_len = _packed_half_len(shard_shapes)
    gathered = _gather_weights(_pack([w_loc[n][0] for n in BIG], half_len, BF16))
    per_chip = [_unpack(gathered[s], shard_shapes) for s in range(N_CHIPS)]
    W = {n: _from_chip_shards(n, jnp.stack([per_chip[s][i] for s in range(N_CHIPS)])) for i, n in enumerate(BIG)}
    W.update({n: w_loc[n][0] for n in SMALL})

    loss, grad_x, gw = _layer_step(x, mem, loss_target, W)

    by_chip = [_to_chip_shards(n, gw[n]) for n in BIG]
    parts = jnp.stack([_pack([b[s] for b in by_chip], half_len, F32) for s in range(N_CHIPS)])
    half = _sum_devices(_scatter_grads(parts), "sum_grads")
    g_shard = dict(zip(BIG, _unpack(_swap_halves(half), shard_shapes)))

    small_shapes = [w_loc[n].shape[1:] for n in SMALL] + [(1,)]
    n_small = sum(int(np.prod(s)) for s in small_shapes)
    small_rows = -(-n_small // (8 * LANES)) * 8
    flat = jnp.concatenate([gw[n].reshape(-1) for n in SMALL] + [loss.reshape(1)])
    flat = jnp.pad(flat, (0, small_rows * LANES - n_small)).reshape(small_rows, LANES)
    small = _unpack(_allreduce_small(flat), small_shapes)
    g_small = dict(zip(SMALL, small[:-1]))
    loss = small[-1][0]

    grads, deltas, new_m, new_v = [], [], [], []
    for n in WEIGHTS:
        g = (g_shard[n] if n in g_shard else g_small[n]).reshape(w_loc[n].shape)
        d, nm, nv = _adamw(n, w_loc[n], g, m_loc[n], v_loc[n])
        grads.append(g)
        deltas.append(d)
        new_m.append(nm)
        new_v.append(nv)
    return (loss, grad_x, *grads, *deltas, *new_m, *new_v)
```

```python
import functools
import math

import numpy as np
import jax
import jax.numpy as jnp
from jax import lax
from jax.experimental import pallas as pl
from jax.experimental.pallas import tpu as pltpu

F32, BF16 = jnp.float32, jnp.bfloat16
MESH = pl.DeviceIdType.MESH

D_MODEL = 1024
HEAD_DIM = 64
N_HEADS = 8
BR_W = 512
MEM_HEADS = 4
MEM_HEAD_DIM = 128
D_FF = 2816
NORM_EPS = 1e-6
GN_EPS = 64e-5
N_CHIPS = 4
N_DEV = 8
LANES = 128
VMEM_LIMIT = 48 * 1024 * 1024

ADAM_LR, ADAM_B1, ADAM_B2, ADAM_EPS, ADAM_WD, ADAM_STEP = 0.001, 0.9, 0.999, 1e-08, 0.01, 10

FOX_COLS = 3 * BR_W + N_HEADS
RWKV_COLS = 3 * BR_W + 64 + 64 + 128
COL_QKV = (0, 3 * BR_W)
COL_F = (3 * BR_W, FOX_COLS)
COL_RW = (FOX_COLS, FOX_COLS + RWKV_COLS)
COL_MQ = (COL_RW[1], COL_RW[1] + BR_W)
COL_GATE = (COL_MQ[1], COL_MQ[1] + 3 * D_MODEL)

NT_DIMS = (((1,), (1,)), ((), ()))
TN_DIMS = (((0,), (0,)), ((), ()))


def _params(sem=None, **kw):
    return pltpu.CompilerParams(dimension_semantics=sem, vmem_limit_bytes=VMEM_LIMIT, **kw)


def _sigmoid(x):
    return 1.0 / (1.0 + jnp.exp(-x))


def _log_sigmoid(x):
    return jnp.minimum(x, 0.0) - jnp.log(1.0 + jnp.exp(-jnp.abs(x)))


def _bdot(a, b, dims=None):
    a, b = a.astype(BF16), b.astype(BF16)
    if dims is None:
        return jnp.dot(a, b, preferred_element_type=F32)
    return lax.dot_general(a, b, dims, preferred_element_type=F32)


def _hdot(a, b):
    return jnp.dot(a, b, precision=lax.Precision.HIGHEST, preferred_element_type=F32)


def _tile(n, cap):
    best = None
    for t in range(LANES, min(n, cap) + 1, LANES):
        if n % t == 0:
            best = t
    return best or n


def _rowwise(name, fn, rows, consts, outs, accs=(), tm=256):
    T = rows[0].shape[0]
    tm = min(tm, T)
    assert T % tm == 0
    nr, nc, no, na = len(rows), len(consts), len(outs), len(accs)

    def body(*refs):
        res = fn(*[r[...] for r in refs[:nr + nc]])
        if not isinstance(res, (tuple, list)):
            res = (res,)
        orefs, arefs = refs[nr + nc:nr + nc + no], refs[nr + nc + no:]
        for ref, val in zip(orefs, res[:no]):
            ref[...] = val.astype(ref.dtype)
        if na:
            @pl.when(pl.program_id(0) == 0)
            def _():
                for ref in arefs:
                    ref[...] = jnp.zeros(ref.shape, ref.dtype)
            for ref, val in zip(arefs, res[no:]):
                ref[...] += val

    in_specs = ([pl.BlockSpec((tm, r.shape[1]), lambda i: (i, 0)) for r in rows]
                + [pl.BlockSpec(c.shape, lambda i: (0, 0)) for c in consts])
    out_specs = ([pl.BlockSpec((tm, w), lambda i: (i, 0)) for w, _ in outs]
                 + [pl.BlockSpec(s, lambda i: (0, 0)) for s, _ in accs])
    out_shape = ([jax.ShapeDtypeStruct((T, w), dt) for w, dt in outs]
                 + [jax.ShapeDtypeStruct(s, dt) for s, dt in accs])
    return pl.pallas_call(
        body, grid=(T // tm,), in_specs=in_specs, out_specs=out_specs, out_shape=out_shape, name=name,
        compiler_params=_params(("arbitrary",) if na else ("parallel",)),
    )(*rows, *consts)


def _mm(name, a, b, ta=False, tb=False, out_dtype=F32, add=None):
    M, K = (a.shape[1], a.shape[0]) if ta else a.shape
    K2, N = (b.shape[1], b.shape[0]) if tb else b.shape
    assert K == K2
    tm = _tile(M, 512) if ta else min(M, 512)
    tn, tk = _tile(N, 512), _tile(K, 512)
    assert M % tm == 0
    nk = K // tk
    a_dim, b_dim = (0 if ta else 1), (1 if tb else 0)

    def body(*refs):
        a_ref, b_ref = refs[0], refs[1]
        o_ref, acc = refs[-2], refs[-1]
        k = pl.program_id(2)

        @pl.when(k == 0)
        def _():
            acc[...] = jnp.zeros(acc.shape, F32)

        acc[...] += lax.dot_general(a_ref[...].astype(BF16), b_ref[...].astype(BF16),
                                    (((a_dim,), (b_dim,)), ((), ())), preferred_element_type=F32)

        @pl.when(k == nk - 1)
        def _():
            r = acc[...]
            if add is not None:
                r = r + refs[2][...].astype(F32)
            o_ref[...] = r.astype(o_ref.dtype)

    a_spec = pl.BlockSpec((tk, tm), lambda i, j, k: (k, i)) if ta else pl.BlockSpec((tm, tk), lambda i, j, k: (i, k))
    b_spec = pl.BlockSpec((tn, tk), lambda i, j, k: (j, k)) if tb else pl.BlockSpec((tk, tn), lambda i, j, k: (k, j))
    o_spec = pl.BlockSpec((tm, tn), lambda i, j, k: (i, j))
    ins, in_specs = [a, b], [a_spec, b_spec]
    if add is not None:
        ins.append(add)
        in_specs.append(o_spec)
    return pl.pallas_call(
        body, grid=(M // tm, N // tn, nk), in_specs=in_specs, out_specs=o_spec,
        out_shape=jax.ShapeDtypeStruct((M, N), out_dtype), scratch_shapes=[pltpu.VMEM((tm, tn), F32)], name=name,
        compiler_params=_params(("parallel", "parallel", "arbitrary")),
    )(*ins)


def _rowsum(x):
    return jnp.sum(x, axis=0, keepdims=True)


def _rms_stat(x):
    return lax.rsqrt(jnp.mean(x * x, axis=-1, keepdims=True) + NORM_EPS)


def _rms_bwd(dy, x, g):
    r = _rms_stat(x)
    xn = x * r
    dxn = dy * g
    dx = r * (dxn - xn * jnp.mean(dxn * xn, axis=-1, keepdims=True))
    return dx, _rowsum(dy * xn)


def _fox_c_fwd(f8t, bias_col, tc=256):
    B, H, S = f8t.shape
    tc = min(tc, S)

    def body(f_ref, b_ref, c_ref, carry):
        @pl.when(pl.program_id(1) == 0)
        def _():
            carry[...] = jnp.zeros(carry.shape, F32)
        lf = _log_sigmoid(f_ref[...] + b_ref[...])
        row = lax.broadcasted_iota(jnp.int32, (tc, tc), 0)
        col = lax.broadcasted_iota(jnp.int32, (tc, tc), 1)
        c = _hdot(lf, (row <= col).astype(F32)) + carry[...]
        c_ref[...] = c
        carry[...] = c[:, tc - 1:tc]

    return pl.pallas_call(
        body, grid=(B, S // tc),
        in_specs=[pl.BlockSpec((None, H, tc), lambda b, i: (b, 0, i)), pl.BlockSpec((H, 1), lambda b, i: (0, 0))],
        out_specs=pl.BlockSpec((None, H, tc), lambda b, i: (b, 0, i)),
        out_shape=jax.ShapeDtypeStruct((B, H, S), F32), scratch_shapes=[pltpu.VMEM((H, 1), F32)], name="fox_c_fwd",
        compiler_params=_params(("parallel", "arbitrary")),
    )(f8t, bias_col)


def _fox_c_bwd(dc, f8t, bias_col, tc=256):
    B, H, S = f8t.shape
    tc = min(tc, S)
    n = S // tc

    def body(dc_ref, f_ref, b_ref, df_ref, db_ref, carry):
        @pl.when(pl.program_id(1) == 0)
        def _():
            carry[...] = jnp.zeros(carry.shape, F32)
            db_ref[...] = jnp.zeros(db_ref.shape, F32)
        row = lax.broadcasted_iota(jnp.int32, (tc, tc), 0)
        col = lax.broadcasted_iota(jnp.int32, (tc, tc), 1)
        dlf = _hdot(dc_ref[...], (row >= col).astype(F32)) + carry[...]
        z = f_ref[...] + b_ref[...]
        df = dlf * (1.0 - _sigmoid(z))
        df_ref[...] = df
        db_ref[...] += jnp.sum(df, axis=1, keepdims=True)
        carry[...] = dlf[:, 0:1]

    rev = lambda b, i: (b, 0, n - 1 - i)
    return pl.pallas_call(
        body, grid=(B, n),
        in_specs=[pl.BlockSpec((None, H, tc), rev), pl.BlockSpec((None, H, tc), rev), pl.BlockSpec((H, 1), lambda b, i: (0, 0))],
        out_specs=[pl.BlockSpec((None, H, tc), rev), pl.BlockSpec((None, H, 1), lambda b, i: (b, 0, 0))],
        out_shape=[jax.ShapeDtypeStruct((B, H, S), F32), jax.ShapeDtypeStruct((B, H, 1), F32)],
        scratch_shapes=[pltpu.VMEM((H, 1), F32)], name="fox_c_bwd",
        compiler_params=_params(("parallel", "arbitrary")),
    )(dc, f8t, bias_col)


NEG_BIG = -1e30


def _fox_logits(q, kj, cq, ckj, i, j, t, scale):
    s = _bdot(q, kj, NT_DIMS) * scale + (cq - ckj)
    row = lax.broadcasted_iota(jnp.int32, (t, t), 0)
    col = lax.broadcasted_iota(jnp.int32, (t, t), 1)
    return s, col <= row + (i - j) * t


def _fox_fwd(q, k, v, cq, ck, t):
    G, n = q.shape[0], q.shape[1]
    scale = HEAD_DIM ** -0.5

    def body(q_ref, k_ref, v_ref, cq_ref, ck_ref, o_ref, lse_ref):
        i = pl.program_id(1)
        qi, cqi = q_ref[...], cq_ref[...]

        def step(j, carry):
            m, l, acc = carry
            s, ok = _fox_logits(qi, k_ref[j], cqi, ck_ref[j], i, j, t, scale)
            s = jnp.where(ok, s, NEG_BIG)
            m2 = jnp.maximum(m, jnp.max(s, axis=1, keepdims=True))
            p = jnp.exp(s - m2)
            al = jnp.exp(m - m2)
            return m2, al * l + jnp.sum(p, axis=1, keepdims=True), al * acc + _bdot(p, v_ref[j])

        m, l, acc = lax.fori_loop(0, i + 1, step, (jnp.full((t, 1), NEG_BIG, F32), jnp.zeros((t, 1), F32),
                                                   jnp.zeros((t, HEAD_DIM), F32)))
        o_ref[...] = (acc / l).astype(o_ref.dtype)
        lse_ref[...] = m + jnp.log(l)

    blk = pl.BlockSpec((None, None, t, HEAD_DIM), lambda g, i: (g, i, 0, 0))
    full = pl.BlockSpec((None, n, t, HEAD_DIM), lambda g, i: (g, 0, 0, 0))
    col = pl.BlockSpec((None, None, t, 1), lambda g, i: (g, i, 0, 0))
    return pl.pallas_call(
        body, grid=(G, n), in_specs=[blk, full, full, col, pl.BlockSpec((None, n, 1, t), lambda g, i: (g, 0, 0, 0))],
        out_specs=[blk, col],
        out_shape=[jax.ShapeDtypeStruct(q.shape, BF16), jax.ShapeDtypeStruct((G, n, t, 1), F32)], name="fox_fwd",
        compiler_params=_params(("parallel", "parallel")),
    )(q, k, v, cq, ck)


def _fox_bwd(q, k, v, o, do, cq, ck, lse, t):
    G, n = q.shape[0], q.shape[1]
    scale = HEAD_DIM ** -0.5

    def body(q_ref, k_ref, v_ref, o_ref, do_ref, cq_ref, ck_ref, lse_ref, dq_ref, dk_ref, dv_ref, dck_ref, dcq_ref):
        dk_ref[...] = jnp.zeros(dk_ref.shape, F32)
        dv_ref[...] = jnp.zeros(dv_ref.shape, F32)
        dck_ref[...] = jnp.zeros(dck_ref.shape, F32)

        def qloop(i, _):
            qi, doi, cqi, lsei = q_ref[i], do_ref[i], cq_ref[i], lse_ref[i]
            delta = jnp.sum(doi.astype(F32) * o_ref[i].astype(F32), axis=1, keepdims=True)

            def kloop(j, carry):
                dq, dcq = carry
                kj, vj = k_ref[j], v_ref[j]
                s, ok = _fox_logits(qi, kj, cqi, ck_ref[j], i, j, t, scale)
                p = jnp.where(ok, jnp.exp(s - lsei), 0.0)
                ds = p * (_bdot(doi, vj, NT_DIMS) - delta)
                dv_ref[j] += _bdot(p, doi, TN_DIMS)
                dk_ref[j] += _bdot(ds, qi, TN_DIMS) * scale
                dck_ref[j] += -_rowsum(ds)
                return dq + _bdot(ds, kj) * scale, dcq + jnp.sum(ds, axis=1, keepdims=True)

            dq_ref[i], dcq_ref[i] = lax.fori_loop(0, i + 1, kloop, (jnp.zeros((t, HEAD_DIM), F32), jnp.zeros((t, 1), F32)))
            return 0

        lax.fori_loop(0, n, qloop, 0)

    full = pl.BlockSpec((None, n, t, HEAD_DIM), lambda g: (g, 0, 0, 0))
    col = pl.BlockSpec((None, n, t, 1), lambda g: (g, 0, 0, 0))
    row = pl.BlockSpec((None, n, 1, t), lambda g: (g, 0, 0, 0))
    f32s = jax.ShapeDtypeStruct(q.shape, F32)
    return pl.pallas_call(
        body, grid=(G,), in_specs=[full, full, full, full, full, col, row, col], out_specs=[full, full, full, row, col],
        out_shape=[f32s, f32s, f32s, jax.ShapeDtypeStruct((G, n, 1, t), F32), jax.ShapeDtypeStruct((G, n, t, 1), F32)],
        name="fox_bwd", compiler_params=_params(("parallel",)),
    )(q, k, v, o, do, cq, ck, lse)


def _mem_probs(qh, kh):
    s = _bdot(qh, kh, NT_DIMS) * (MEM_HEAD_DIM ** -0.5)
    e = jnp.exp(s - jnp.max(s, axis=1, keepdims=True))
    return e / jnp.sum(e, axis=1, keepdims=True)


def _mem_fwd(q, mem_kv, B, tq=512):
    T = q.shape[0]
    S, Lm = T // B, mem_kv.shape[0] // B
    tq = min(tq, S)
    n = S // tq

    def body(q_ref, k_ref, v_ref, o_ref):
        for h in range(MEM_HEADS):
            sl = slice(h * MEM_HEAD_DIM, (h + 1) * MEM_HEAD_DIM)
            p = _mem_probs(q_ref[:, sl], k_ref[:, sl])
            o_ref[:, sl] = _bdot(p, v_ref[:, sl]).astype(o_ref.dtype)

    qs = pl.BlockSpec((tq, BR_W), lambda b, i: (b * n + i, 0))
    return pl.pallas_call(
        body, grid=(B, n),
        in_specs=[qs, pl.BlockSpec((Lm, BR_W), lambda b, i: (b, 0)), pl.BlockSpec((Lm, BR_W), lambda b, i: (b, 1))],
        out_specs=qs, out_shape=jax.ShapeDtypeStruct((T, BR_W), BF16), name="mem_fwd",
        compiler_params=_params(("parallel", "parallel")),
    )(q, mem_kv, mem_kv)


def _mem_bwd(q, mem_kv, do, B, tq=512):
    T = q.shape[0]
    S, Lm = T // B, mem_kv.shape[0] // B
    tq = min(tq, S)
    n = S // tq
    scale = MEM_HEAD_DIM ** -0.5

    def body(q_ref, k_ref, v_ref, do_ref, dq_ref, dk_ref, dv_ref):
        @pl.when(pl.program_id(1) == 0)
        def _():
            dk_ref[...] = jnp.zeros(dk_ref.shape, F32)
            dv_ref[...] = jnp.zeros(dv_ref.shape, F32)
        for h in range(MEM_HEADS):
            sl = slice(h * MEM_HEAD_DIM, (h + 1) * MEM_HEAD_DIM)
            qh, kh, vh, doh = q_ref[:, sl], k_ref[:, sl], v_ref[:, sl], do_ref[:, sl]
            p = _mem_probs(qh, kh)
            dp = _bdot(doh, vh, NT_DIMS)
            ds = p * (dp - jnp.sum(p * dp, axis=1, keepdims=True))
            dq_ref[:, sl] = (_bdot(ds, kh) * scale).astype(dq_ref.dtype)
            dk_ref[:, sl] += _bdot(ds, qh, TN_DIMS) * scale
            dv_ref[:, sl] += _bdot(p, doh, TN_DIMS)

    qs = pl.BlockSpec((tq, BR_W), lambda b, i: (b * n + i, 0))
    kv = pl.BlockSpec((Lm, BR_W), lambda b, i: (b, 0))
    return pl.pallas_call(
        body, grid=(B, n),
        in_specs=[qs, kv, pl.BlockSpec((Lm, BR_W), lambda b, i: (b, 1)), qs], out_specs=[qs, kv, kv],
        out_shape=[jax.ShapeDtypeStruct((T, BR_W), BF16), jax.ShapeDtypeStruct((B * Lm, BR_W), F32),
                   jax.ShapeDtypeStruct((B * Lm, BR_W), F32)], name="mem_bwd",
        compiler_params=_params(("parallel", "arbitrary")),
    )(q, mem_kv, mem_kv, do)


def _head_ones():
    h = np.arange(BR_W) // HEAD_DIM
    return jnp.asarray((h[:, None] == h[None, :]).astype(np.float32))


def _rw_prep(p, pp, mu, w0, w1, a0, w2, g_up, k_k, k_a, bd):
    ps = p + (pp - p) * mu
    r, k, v = ps[:, 0:512], ps[:, 512:1024], ps[:, 1024:1536]
    wa, gd = ps[:, 1536:1664], ps[:, 1664:1792]
    th = jnp.tanh(wa)
    z = w0 + _bdot(th, w1)
    wl = -jnp.exp(_log_sigmoid(z) - 0.5)
    w = jnp.exp(wl)
    a = _sigmoid(a0 + _bdot(wa, w2))
    sg = _sigmoid(gd)
    g = _bdot(sg, g_up)
    kq = k * k_k
    n2 = _hdot(kq * kq, bd)
    inv = lax.rsqrt(jnp.maximum(n2, 1e-24))
    kk = kq * inv
    k2 = k * (1.0 + (a - 1.0) * k_a)
    return dict(ps=ps, r=r, k=k, v=v, wa=wa, th=th, z=z, wl=wl, w=w, a=a, sg=sg, g=g, kq=kq, n2=n2, inv=inv, kk=kk, k2=k2)


def _rw_prep_fwd(p, pp, consts):
    def fn(p, pp, *c):
        t = _rw_prep(p, pp, *c)
        return t["r"], t["w"], t["k2"], t["v"], -t["kk"], t["kk"] * t["a"], t["g"]
    return _rowwise("rwkv_prep_fwd", fn, [p, pp], consts, [(BR_W, F32)] * 7, tm=256)


def _rw_prep_bwd(p, pp, cots, consts):
    def fn(p, pp, dr1, dr2, dw, dk21, dk22, dv1, dv2, dav, dbv, dg, mu, w0, w1, a0, w2, g_up, k_k, k_a, bd):
        t = _rw_prep(p, pp, mu, w0, w1, a0, w2, g_up, k_k, k_a, bd)
        dr, dk2, dv = dr1 + dr2, dk21 + dk22, dv1 + dv2
        a, k, kk, kq, inv = t["a"], t["k"], t["kk"], t["kq"], t["inv"]
        dkk = dbv * a - dav
        da = dbv * kk + dk2 * k * k_a
        dk = dk2 * (1.0 + (a - 1.0) * k_a)
        d_k_a = _rowsum(dk2 * k * (a - 1.0))
        proj = _hdot(dkk * kq, bd)
        dkq = dkk * inv - jnp.where(t["n2"] > 1e-24, kq * inv * inv * inv * proj, 0.0)
        dk = dk + dkq * k_k
        d_k_k = _rowsum(dkq * k)
        dpa = da * a * (1.0 - a)
        d_a0 = _rowsum(dpa)
        dwa = _bdot(dpa, w2, NT_DIMS)
        d_w2 = _bdot(t["wa"], dpa, TN_DIMS)
        dz = dw * t["w"] * t["wl"] * (1.0 - _sigmoid(t["z"]))
        d_w0 = _rowsum(dz)
        th = t["th"]
        dwa = dwa + _bdot(dz, w1, NT_DIMS) * (1.0 - th * th)
        d_w1 = _bdot(th, dz, TN_DIMS)
        sg = t["sg"]
        dgd = _bdot(dg, g_up, NT_DIMS) * sg * (1.0 - sg)
        d_g_up = _bdot(sg, dg, TN_DIMS)
        dps = jnp.concatenate([dr, dk, dv, dwa, dgd], axis=1)
        d_mu = _rowsum(dps * (pp - p))
        return dps * (1.0 - mu), dps * mu, d_mu, d_w0, d_w1, d_a0, d_w2, d_g_up, d_k_k, d_k_a

    accs = [((1, RWKV_COLS), F32), ((1, BR_W), F32), ((LANES, BR_W), F32), ((1, BR_W), F32), ((LANES, BR_W), F32),
            ((LANES, BR_W), F32), ((1, BR_W), F32), ((1, BR_W), F32)]
    return _rowwise("rwkv_prep_bwd", fn, [p, pp] + list(cots), consts, [(RWKV_COLS, F32)] * 2, accs, tm=128)


def _rw_head(y, r, k2, v, g, gn_g, gn_b, r_k, bd):
    mean = _hdot(y, bd) * (1.0 / HEAD_DIM)
    yc = y - mean
    rs = lax.rsqrt(_hdot(yc * yc, bd) * (1.0 / HEAD_DIM) + GN_EPS)
    yn = yc * rs
    bs = _hdot(r * k2 * r_k, bd)
    return yn, rs, bs, yn * gn_g + gn_b + bs * v


def _rw_head_fwd(y, r, k2, v, g, consts):
    def fn(y, r, k2, v, g, *c):
        return _rw_head(y, r, k2, v, g, *c)[3] * g
    return _rowwise("rwkv_head_fwd", fn, [y, r, k2, v, g], consts, [(BR_W, BF16)])[0]


def _rw_head_bwd(dout, y, r, k2, v, g, consts):
    def fn(dout, y, r, k2, v, g, gn_g, gn_b, r_k, bd):
        dout = dout.astype(F32)
        yn, rs, bs, zz = _rw_head(y, r, k2, v, g, gn_g, gn_b, r_k, bd)
        dg = dout * zz
        dz = dout * g
        dyn = dz * gn_g
        inv_n = 1.0 / HEAD_DIM
        dy = rs * (dyn - _hdot(dyn, bd) * inv_n - yn * (_hdot(dyn * yn, bd) * inv_n))
        dq = _hdot(dz * v, bd)
        return dy, dg, dq * k2 * r_k, dq * r * r_k, dz * bs, _rowsum(dz * yn), _rowsum(dz), _rowsum(dq * r * k2)
    return _rowwise("rwkv_head_bwd", fn, [dout, y, r, k2, v, g], consts, [(BR_W, F32)] * 5, [((1, BR_W), F32)] * 3)


SCAN_TC = 32


def _scan_onehot(Tc):
    w = np.zeros((Tc // 2, 2 * Tc, 2 * LANES), np.float32)
    for tt in range(Tc // 2):
        for u in range(2):
            for h in range(2):
                w[tt, h * Tc + 2 * tt + u, u * LANES + h * HEAD_DIM: u * LANES + (h + 1) * HEAD_DIM] = 1.0
    return jnp.asarray(w, BF16)


def _to_keycols(xp, Tc):
    NP, S, _ = xp.shape
    return xp.reshape(NP, S // Tc, Tc, 2, HEAD_DIM).transpose(0, 1, 4, 3, 2).reshape(NP, S // Tc, HEAD_DIM, 2 * Tc)


def _split_bf16(x):
    hi = x.astype(BF16)
    return hi, (x - hi.astype(F32)).astype(BF16)


def _key_tiles(l_w, others, onehot):
    dot = lambda x: jnp.dot(x, onehot, preferred_element_type=F32)
    whi, wmid = l_w
    return [dot(whi) + dot(wmid)] + [dot(o) for o in others]


def _rw_scan_fwd(LW, LA, LB, LK, LR, v, P=2):
    NP, nc, _, Tc2 = LW.shape
    Tc = Tc2 // 2
    S = nc * Tc
    onehot = _scan_onehot(Tc)

    def body(lw, la, lb, lk, lr, v_ref, oh_ref, y_ref, sa_ref, sb_ref, st):
        @pl.when(pl.program_id(1) == 0)
        def _():
            st[...] = jnp.zeros(st.shape, F32)
        s = [st[p] for p in range(P)]
        cols = [(_split_bf16(lw[p]), [ref[p].astype(BF16) for ref in (la, lb, lk, lr)]) for p in range(P)]
        for tt in range(Tc // 2):
            tiles = [_key_tiles(c[0], c[1], oh_ref[tt]) for c in cols]
            for u in range(2):
                t = 2 * tt + u
                for p in range(P):
                    W, A, Bt, Kt, R = (x[:, u * LANES:(u + 1) * LANES] for x in tiles[p])
                    sb_ref[p, t] = s[p]
                    sa = _rowsum(s[p] * A)
                    s[p] = s[p] * W + Bt * sa + Kt * v_ref[p, t:t + 1, :]
                    y_ref[p, t:t + 1, :] = _rowsum(s[p] * R)
                    sa_ref[p, t:t + 1, :] = sa
        for p in range(P):
            st[p] = s[p]

    lspec = pl.BlockSpec((P, None, HEAD_DIM, Tc2), lambda g, c: (g, c, 0, 0))
    rows = pl.BlockSpec((P, Tc, LANES), lambda g, c: (g, c, 0))
    rowshape = jax.ShapeDtypeStruct((NP, S, LANES), F32)
    return pl.pallas_call(
        body, grid=(NP // P, nc), in_specs=[lspec] * 5 + [rows, pl.BlockSpec(onehot.shape, lambda g, c: (0, 0, 0))],
        out_specs=[rows, rows, pl.BlockSpec((P, Tc, HEAD_DIM, LANES), lambda g, c: (g, c, 0, 0))],
        out_shape=[rowshape, rowshape, jax.ShapeDtypeStruct((NP, S, HEAD_DIM, LANES), F32)],
        scratch_shapes=[pltpu.VMEM((P, HEAD_DIM, LANES), F32)], name="rwkv_scan_fwd",
        compiler_params=_params(("parallel", "arbitrary")),
    )(LW, LA, LB, LK, LR, v, onehot)


SCAN_G_ROWS = 16


def _rw_scan_bwd(LW, LA, LB, LK, LR, v, sa, dy, sb, P=4):
    NP, nc, _, Tc2 = LW.shape
    Tc = Tc2 // 2
    S = nc * Tc
    onehot = _scan_onehot(Tc)

    def body(lw, la, lb, lk, lr, v_ref, sa_ref, dy_ref, sb_ref, oh_ref, dv_ref, g_ref, dst):
        @pl.when(pl.program_id(1) == 0)
        def _():
            dst[...] = jnp.zeros(dst.shape, F32)
        rid = lax.broadcasted_iota(jnp.int32, (SCAN_G_ROWS, LANES), 0)
        lane = lax.broadcasted_iota(jnp.int32, (SCAN_G_ROWS, LANES), 1)
        own = ((rid % 2) == 0) == (lane < HEAD_DIM)
        m_kb, m_w, m_r, m_a = (own & (rid >= lo) & (rid < hi) for lo, hi in ((0, 4), (4, 6), (6, 8), (8, 10)))
        nt = lambda rows, tile: lax.dot_general(rows.astype(BF16), tile.astype(BF16), NT_DIMS, preferred_element_type=F32)
        ds = [dst[p] for p in range(P)]
        cols = [(_split_bf16(lw[p]), [ref[p].astype(BF16) for ref in (la, lb, lk, lr)]) for p in range(P)]
        for tt in reversed(range(Tc // 2)):
            tiles = [_key_tiles(c[0], c[1], oh_ref[tt]) for c in cols]
            for u in (1, 0):
                t = 2 * tt + u
                for p in range(P):
                    W, A, Bt, Kt, R = (x[:, u * LANES:(u + 1) * LANES] for x in tiles[p])
                    vr, sar, dyr = (ref[p, t:t + 1, :] for ref in (v_ref, sa_ref, dy_ref))
                    sp = sb_ref[p, t]
                    s_t = sp * W + Bt * sar + Kt * vr
                    d = ds[p] + R * dyr
                    dv_ref[p, t:t + 1, :] = _rowsum(d * Kt)
                    dsar = _rowsum(d * Bt)
                    g = nt(jnp.where(m_kb, jnp.where(rid < 2, vr, sar), 0.0), d)
                    g = g + nt(jnp.where(m_w, 1.0, 0.0), d * sp)
                    g = g + nt(jnp.where(m_r, dyr, 0.0), s_t)
                    g = g + nt(jnp.where(m_a, dsar, 0.0), sp)
                    g_ref[p, t] = g
                    ds[p] = d * W + A * dsar
        for p in range(P):
            dst[p] = ds[p]

    rev = lambda g, c: (g, nc - 1 - c, 0, 0)
    lspec = pl.BlockSpec((P, None, HEAD_DIM, Tc2), rev)
    rows = pl.BlockSpec((P, Tc, LANES), lambda g, c: (g, nc - 1 - c, 0))
    return pl.pallas_call(
        body, grid=(NP // P, nc),
        in_specs=[lspec] * 5 + [rows] * 3 + [pl.BlockSpec((P, Tc, HEAD_DIM, LANES), rev),
                                             pl.BlockSpec(onehot.shape, lambda g, c: (0, 0, 0))],
        out_specs=[rows, pl.BlockSpec((P, Tc, SCAN_G_ROWS, HEAD_DIM), rev)],
        out_shape=[jax.ShapeDtypeStruct((NP, S, LANES), F32), jax.ShapeDtypeStruct((NP, S, SCAN_G_ROWS, HEAD_DIM), F32)],
        scratch_shapes=[pltpu.VMEM((P, HEAD_DIM, LANES), F32)], name="rwkv_scan_bwd",
        compiler_params=_params(("parallel", "arbitrary")),
    )(LW, LA, LB, LK, LR, v, sa, dy, sb, onehot)


def _to_pairs(x, B):
    T = x.shape[0]
    return x.reshape(B, T // B, 4, LANES).transpose(0, 2, 1, 3).reshape(B * 4, T // B, LANES)


def _from_pairs(x, B):
    NP, S, _ = x.shape
    return x.reshape(B, 4, S, LANES).transpose(0, 2, 1, 3).reshape(B * S, BR_W)


def _shift_prev(p, B):
    T, W = p.shape
    return jnp.pad(p.reshape(B, T // B, W), ((0, 0), (1, 0), (0, 0)))[:, :-1].reshape(T, W)


def _shift_next(p, B):
    T, W = p.shape
    return jnp.pad(p.reshape(B, T // B, W), ((0, 0), (0, 1), (0, 0)))[:, 1:].reshape(T, W)


def _heads_split(x, B, t):
    T = x.shape[0]
    S = T // B
    return x.reshape(B, S, N_HEADS, HEAD_DIM).transpose(0, 2, 1, 3).reshape(B * N_HEADS, S // t, t, HEAD_DIM)


def _heads_merge(x, B):
    G, n, t, _ = x.shape
    return x.reshape(B, N_HEADS, n * t, HEAD_DIM).transpose(0, 2, 1, 3).reshape(B * n * t, BR_W)


def _layer_step(x, mem, target, W, scan_tc=SCAN_TC, fox_t=256):
    B, S, _ = x.shape
    T = B * S
    x2, tgt2 = x.reshape(T, D_MODEL), target.reshape(T, D_MODEL)
    mem2 = mem.reshape(-1, D_MODEL)
    w_in = W["w_in"]
    w_qkv, w_rw, w_mq, w_gate = (w_in[:, lo:hi] for lo, hi in (COL_QKV, COL_RW, COL_MQ, COL_GATE))
    w_f = jnp.pad(w_in[:, COL_F[0]:COL_F[1]], ((0, 0), (0, LANES - N_HEADS)))
    row = lambda v: v.reshape(1, -1).astype(F32)
    pre1_g, post1_g, pre2_g, post2_g, mem_g = (row(W[n]) for n in ("pre1_g", "post1_g", "pre2_g", "post2_g", "mem_norm_g"))

    u = _rowwise("rms_pre1", lambda x, g: x * _rms_stat(x) * g, [x2], [pre1_g], [(D_MODEL, BF16)])[0]
    qkv = _mm("proj_qkv", u, w_qkv, out_dtype=BF16)
    f_pad = _mm("proj_f", u, w_f)
    p_rw = _mm("proj_rwkv", u, w_rw)
    memq = _mm("proj_memq", u, w_mq, out_dtype=BF16)
    gate = _mm("proj_gate", u, w_gate)

    t = min(fox_t, S)
    bias_col = W["fox_f_bias"].reshape(N_HEADS, 1).astype(F32)
    f8t = f_pad[:, :N_HEADS].reshape(B, S, N_HEADS).transpose(0, 2, 1)
    c = _fox_c_fwd(f8t, bias_col)
    G = B * N_HEADS
    cq, ck = c.reshape(G, S // t, t, 1), c.reshape(G, S // t, 1, t)
    fq, fk, fv = (_heads_split(qkv[:, i * BR_W:(i + 1) * BR_W], B, t) for i in range(3))
    fo4, lse = _fox_fwd(fq, fk, fv, cq, ck, t)
    fox_out = _heads_merge(fo4, B)

    bd = _head_ones()
    zpad = jnp.zeros((64, BR_W), F32)
    w1 = jnp.concatenate([W["rwkv_w_up"].astype(F32), zpad], axis=0)
    w2 = jnp.concatenate([zpad, W["rwkv_a_up"].astype(F32)], axis=0)
    prep_consts = [row(W["rwkv_mu"]), row(W["rwkv_w0"]), w1, row(W["rwkv_a0"]), w2, W["rwkv_g_up"].astype(F32),
                   row(W["rwkv_k_k"]), row(W["rwkv_k_a"]), bd]
    p_prev = _shift_prev(p_rw, B)
    rr, rw, rk2, rv, rav, rbv, rg = _rw_prep_fwd(p_rw, p_prev, prep_consts)
    scan_cols = [_to_keycols(_to_pairs(v, B), scan_tc) for v in (rw, rav, rbv, rk2, rr)]
    v_rows = _to_pairs(rv, B)
    y_rows, sa_rows, sb = _rw_scan_fwd(*scan_cols, v_rows)
    ry = _from_pairs(y_rows, B)
    head_consts = [row(W["rwkv_gn_g"]), row(W["rwkv_gn_b"]), row(W["rwkv_r_k"]), bd]
    rwkv_out = _rw_head_fwd(ry, rr, rk2, rv, rg, head_consts)

    mn = _rowwise("rms_mem", lambda m, g: m * _rms_stat(m) * g, [mem2], [mem_g], [(D_MODEL, BF16)])[0]
    mem_kv = _mm("proj_memkv", mn, W["w_mem_kv"], out_dtype=BF16)
    mem_out = _mem_fwd(memq, mem_kv, B)

    fo = [_mm("branch_" + n, a, W[n]) for n, a in (("w_fox_out", fox_out), ("w_rwkv_out", rwkv_out), ("w_mem_out", mem_out))]

    def merge(gate, f0, f1, f2):
        return sum(_sigmoid(gate[:, i * D_MODEL:(i + 1) * D_MODEL]) * f for i, f in enumerate((f0, f1, f2)))
    merged = _rowwise("merge", merge, [gate] + fo, [], [(D_MODEL, BF16)])[0]
    y1 = _mm("proj_o", merged, W["w_o"])

    def mid(x, y1, g1, g2):
        h1 = x + y1 * _rms_stat(y1) * g1
        return h1, h1 * _rms_stat(h1) * g2
    h1, u2 = _rowwise("norm_mid", mid, [x2, y1], [post1_g, pre2_g], [(D_MODEL, F32), (D_MODEL, BF16)])
    gt = _mm("ffn_gate", u2, W["w_ffn_gate"])
    up = _mm("ffn_up", u2, W["w_ffn_up"])
    act = _rowwise("swiglu", lambda gt, up: gt * _sigmoid(gt) * up, [gt, up], [], [(D_FF, BF16)])[0]
    ffn = _mm("ffn_down", act, W["w_ffn_down"])

    def tail(h1, ffn, tgt, g):
        err = h1 + ffn * _rms_stat(ffn) * g - tgt
        dh2 = err * (1.0 / D_MODEL)
        dffn, dg = _rms_bwd(dh2, ffn, g)
        loss = 0.5 * jnp.sum(jnp.sum(err * err, axis=1, keepdims=True) * (1.0 / D_MODEL), axis=0, keepdims=True)
        return dh2, dffn, dg, jnp.broadcast_to(loss, (1, LANES))
    dh2, dffn, d_post2, loss = _rowwise("loss_tail", tail, [h1, ffn, tgt2], [post2_g], [(D_MODEL, F32), (D_MODEL, BF16)],
                                        [((1, D_MODEL), F32), ((1, LANES), F32)])
    gw = {"post2_g": d_post2}
    dact = _mm("d_act", dffn, W["w_ffn_down"], tb=True)
    gw["w_ffn_down"] = _mm("g_ffn_down", act, dffn, ta=True)

    def swiglu_bwd(dact, gt, up):
        s = _sigmoid(gt)
        return dact * up * s * (1.0 + gt * (1.0 - s)), dact * gt * s
    dgt, dup = _rowwise("swiglu_bwd", swiglu_bwd, [dact, gt, up], [], [(D_FF, BF16)] * 2)
    du2 = _mm("d_u2_gate", dgt, W["w_ffn_gate"], tb=True)
    du2 = _mm("d_u2_up", dup, W["w_ffn_up"], tb=True, add=du2)
    gw["w_ffn_gate"] = _mm("g_ffn_gate", u2, dgt, ta=True)
    gw["w_ffn_up"] = _mm("g_ffn_up", u2, dup, ta=True)

    def mid_bwd(du2, dh2, h1, y1, g1, g2):
        dh1_n, d_pre2 = _rms_bwd(du2, h1, g2)
        dh1 = dh2 + dh1_n
        dy1, d_post1 = _rms_bwd(dh1, y1, g1)
        return dh1, dy1, d_post1, d_pre2
    dh1, dy1, gw["post1_g"], gw["pre2_g"] = _rowwise(
        "norm_mid_bwd", mid_bwd, [du2, dh2, h1, y1], [post1_g, pre2_g], [(D_MODEL, F32), (D_MODEL, BF16)],
        [((1, D_MODEL), F32)] * 2)
    dmerged = _mm("d_merged", dy1, W["w_o"], tb=True)
    gw["w_o"] = _mm("g_w_o", merged, dy1, ta=True)

    def merge_bwd(dm, gate, f0, f1, f2):
        s = [_sigmoid(gate[:, i * D_MODEL:(i + 1) * D_MODEL]) for i in range(3)]
        dgate = jnp.concatenate([dm * f * si * (1.0 - si) for f, si in zip((f0, f1, f2), s)], axis=1)
        return dm * s[0], dm * s[1], dm * s[2], dgate
    dfo0, dfo1, dfo2, dgate = _rowwise("merge_bwd", merge_bwd, [dmerged, gate] + fo, [],
                                       [(D_MODEL, BF16)] * 3 + [(3 * D_MODEL, BF16)])
    d_branch = {}
    for n, a, dfo in (("w_fox_out", fox_out, dfo0), ("w_rwkv_out", rwkv_out, dfo1), ("w_mem_out", mem_out, dfo2)):
        d_branch[n] = _mm("d_in_" + n, dfo, W[n], tb=True, out_dtype=BF16)
        gw[n] = _mm("g_" + n, a, dfo, ta=True)

    dmemq, dkm, dvm = _mem_bwd(memq, mem_kv, d_branch["w_mem_out"], B)
    dmem_kv = jnp.concatenate([dkm, dvm], axis=1)
    gw["w_mem_kv"] = _mm("g_w_mem_kv", mn, dmem_kv, ta=True)
    dmn = _mm("d_mn", dmem_kv, W["w_mem_kv"], tb=True)
    gw["mem_norm_g"] = _rowwise("rms_mem_bwd", lambda d, m, g: _rms_bwd(d, m, g)[1], [dmn, mem2], [mem_g], [],
                                [((1, D_MODEL), F32)])[0]

    do4 = _heads_split(d_branch["w_fox_out"], B, t)
    dq4, dk4, dv4, dck, dcq = _fox_bwd(fq, fk, fv, fo4, do4, cq, ck, lse, t)
    df8t, dbias = _fox_c_bwd(dck.reshape(B, N_HEADS, S) + dcq.reshape(B, N_HEADS, S), f8t, bias_col)
    gw["fox_f_bias"] = jnp.sum(dbias, axis=0).reshape(1, N_HEADS)
    dqkv = jnp.concatenate([_heads_merge(d, B) for d in (dq4, dk4, dv4)], axis=1).astype(BF16)
    df_pad = jnp.pad(df8t.transpose(0, 2, 1).reshape(T, N_HEADS), ((0, 0), (0, LANES - N_HEADS))).astype(BF16)

    dry, drg, dr_h, dk2_h, dv_h, gw["rwkv_gn_g"], gw["rwkv_gn_b"], gw["rwkv_r_k"] = _rw_head_bwd(
        d_branch["w_rwkv_out"], ry, rr, rk2, rv, rg, head_consts)
    dv_rows, g_keys = _rw_scan_bwd(*scan_cols, v_rows, sa_rows, _to_pairs(dry, B), sb)
    dk2_s, db_s, dw_s, dr_s, da_s = (_from_pairs(g_keys[:, :, i:i + 2, :].reshape(-1, S, LANES), B) for i in (0, 2, 4, 6, 8))
    dv_s = _from_pairs(dv_rows, B)
    dP, dPp, gw["rwkv_mu"], gw["rwkv_w0"], d_w1, gw["rwkv_a0"], d_w2, gw["rwkv_g_up"], gw["rwkv_k_k"], gw["rwkv_k_a"] = \
        _rw_prep_bwd(p_rw, p_prev, [dr_s, dr_h, dw_s, dk2_s, dk2_h, dv_s, dv_h, da_s, db_s, drg], prep_consts)
    gw["rwkv_w_up"], gw["rwkv_a_up"] = d_w1[:64], d_w2[64:]
    dp_rw = (dP + _shift_next(dPp, B)).astype(BF16)

    du = _mm("d_u_qkv", dqkv, w_qkv, tb=True)
    du = _mm("d_u_f", df_pad, w_f, tb=True, add=du)
    du = _mm("d_u_rwkv", dp_rw, w_rw, tb=True, add=du)
    du = _mm("d_u_memq", dmemq, w_mq, tb=True, add=du)
    du = _mm("d_u_gate", dgate, w_gate, tb=True, add=du)
    gw["w_in"] = jnp.concatenate(
        [_mm("g_w_qkv", u, dqkv, ta=True), _mm("g_w_f", u, df_pad, ta=True)[:, :N_HEADS], _mm("g_w_rwkv", u, dp_rw, ta=True),
         _mm("g_w_memq", u, dmemq, ta=True), _mm("g_w_gate", u, dgate, ta=True)], axis=1)

    def pre1_bwd(du, dh1, x, g):
        dx, dg = _rms_bwd(du, x, g)
        return dh1 + dx, dg
    dx, gw["pre1_g"] = _rowwise("rms_pre1_bwd", pre1_bwd, [du, dh1, x2], [pre1_g], [(D_MODEL, F32)], [((1, D_MODEL), F32)])
    return loss[0, 0], dx.reshape(B, S, D_MODEL), gw


COL_SHARDED = ("w_in", "rwkv_w_up", "rwkv_a_up", "rwkv_g_up", "w_fox_out", "w_rwkv_out", "w_mem_out", "w_ffn_gate", "w_ffn_up")
ROW_SHARDED = ("w_mem_kv", "w_o", "w_ffn_down")
BIG = ("w_in", "rwkv_w_up", "rwkv_a_up", "rwkv_g_up", "w_mem_kv", "w_fox_out", "w_rwkv_out", "w_mem_out", "w_o",
       "w_ffn_gate", "w_ffn_up", "w_ffn_down")
SMALL = ("pre1_g", "post1_g", "pre2_g", "post2_g", "mem_norm_g", "fox_f_bias", "rwkv_mu", "rwkv_w0", "rwkv_a0", "rwkv_k_k",
         "rwkv_k_a", "rwkv_r_k", "rwkv_gn_g", "rwkv_gn_b")
WEIGHTS = ("pre1_g", "post1_g", "pre2_g", "post2_g", "mem_norm_g", "w_in", "fox_f_bias", "rwkv_mu", "rwkv_w0", "rwkv_w_up",
           "rwkv_a0", "rwkv_a_up", "rwkv_g_up", "rwkv_k_k", "rwkv_k_a", "rwkv_r_k", "rwkv_gn_g", "rwkv_gn_b", "w_mem_kv",
           "w_fox_out", "w_rwkv_out", "w_mem_out", "w_o", "w_ffn_gate", "w_ffn_up", "w_ffn_down")
HALF_ROW_ALIGN = 1024 * LANES


def _packed_half_len(shard_shapes):
    n = sum(int(np.prod(s)) for s in shard_shapes)
    return -(-n // (2 * HALF_ROW_ALIGN)) * HALF_ROW_ALIGN


def _pack(pieces, half_len, dtype):
    flat = jnp.concatenate([p.reshape(-1).astype(dtype) for p in pieces])
    flat = jnp.pad(flat, (0, 2 * half_len - flat.shape[0]))
    return flat.reshape(2, half_len // LANES, LANES)


def _unpack(packed, shapes):
    flat = packed.reshape(-1)
    out, off = [], 0
    for s in shapes:
        n = int(np.prod(s))
        out.append(flat[off:off + n].reshape(s))
        off += n
    return out


def _to_chip_shards(name, g):
    R, C = g.shape
    if name in COL_SHARDED:
        return g.reshape(R, N_CHIPS, C // N_CHIPS).transpose(1, 0, 2)
    return g.reshape(N_CHIPS, R // N_CHIPS, C)


def _from_chip_shards(name, s):
    _, r, c = s.shape
    if name in COL_SHARDED:
        return s.transpose(1, 0, 2).reshape(r, N_CHIPS * c)
    return s.reshape(N_CHIPS * r, c)


def _my_place():
    return lax.axis_index("x"), lax.axis_index("y"), lax.axis_index("c")


def _other_chips(x, y):
    return [(1 - x, y), (x, 1 - y), (1 - x, 1 - y)]


ANY = pl.BlockSpec(memory_space=pl.ANY)


def _gather_weights(packed):
    _, R, L = packed.shape

    def body(in_ref, out_ref, send_sems, recv_sems, local_sem):
        x, y, c = _my_place()
        chip = lambda px, py: 2 * px + py
        sibling = (x, y, 1 - c)
        others = _other_chips(x, y)

        def copy(k, src, dst, to):
            return pltpu.make_async_remote_copy(src_ref=src, dst_ref=dst, send_sem=send_sems.at[k], recv_sem=recv_sems.at[k],
                                                device_id=to, device_id_type=MESH)

        mine = pltpu.make_async_copy(in_ref, out_ref.at[chip(x, y)], local_sem)
        mine.start()
        sends = [copy(j, in_ref.at[c], out_ref.at[chip(x, y), c], (px, py, c)) for j, (px, py) in enumerate(others)]
        for cp in sends:
            cp.start()
        passed = [copy(3 + j, out_ref.at[chip(px, py), c], out_ref.at[chip(px, py), c], sibling)
                  for j, (px, py) in enumerate(others)]
        for j, (px, py) in enumerate(others):
            copy(j, in_ref.at[c], out_ref.at[chip(px, py), c], (px, py, c)).wait_recv()
            passed[j].start()
        for j, (px, py) in enumerate(others):
            copy(3 + j, in_ref.at[1 - c], out_ref.at[chip(px, py), 1 - c], sibling).wait_recv()
        for cp in sends + passed:
            cp.wait_send()
        mine.wait()

    return pl.pallas_call(
        body, out_shape=jax.ShapeDtypeStruct((N_CHIPS, 2, R, L), packed.dtype), in_specs=[ANY], out_specs=ANY,
        scratch_shapes=[pltpu.SemaphoreType.DMA((6,)), pltpu.SemaphoreType.DMA((6,)), pltpu.SemaphoreType.DMA],
        name="gather_weights",
    )(packed)


def _scatter_grads(parts):
    _, _, R, L = parts.shape

    def body(in_ref, out_ref, send_sems, recv_sems, local_sem):
        x, y, c = _my_place()
        me = 4 * x + 2 * y + c
        peers = [(px, py, pc) for px in range(2) for py in range(2) for pc in range(2)]

        def copy(k, px, py, pc):
            return pltpu.make_async_remote_copy(src_ref=in_ref.at[2 * px + py, pc], dst_ref=out_ref.at[me],
                                                send_sem=send_sems.at[k], recv_sem=recv_sems.at[k],
                                                device_id=(px, py, pc), device_id_type=MESH)

        mine = pltpu.make_async_copy(in_ref.at[2 * x + y, c], out_ref.at[me], local_sem)
        mine.start()
        started = []
        for k in range(1, N_DEV):
            px, py, pc = x ^ (k >> 2), y ^ ((k >> 1) & 1), c ^ (k & 1)
            cp = copy(k - 1, px, py, pc)
            cp.start()
            started.append(cp)
        for cp in started:
            cp.wait()
        mine.wait()

    return pl.pallas_call(
        body, out_shape=jax.ShapeDtypeStruct((N_DEV, R, L), F32), in_specs=[ANY], out_specs=ANY,
        scratch_shapes=[pltpu.SemaphoreType.DMA((7,)), pltpu.SemaphoreType.DMA((7,)), pltpu.SemaphoreType.DMA],
        name="scatter_grads",
    )(parts)


def _sum_devices(parts, name):
    n, R, L = parts.shape
    tr = _tile_rows(R)

    def body(p_ref, o_ref):
        acc = p_ref[0]
        for i in range(1, n):
            acc = acc + p_ref[i]
        o_ref[...] = acc

    return pl.pallas_call(
        body, grid=(R // tr,), in_specs=[pl.BlockSpec((n, tr, L), lambda i: (0, i, 0))],
        out_specs=pl.BlockSpec((tr, L), lambda i: (i, 0)), out_shape=jax.ShapeDtypeStruct((R, L), F32), name=name,
        compiler_params=_params(("parallel",)),
    )(parts)


def _tile_rows(R, cap=2048):
    best = 8
    for t in range(8, min(R, cap) + 1, 8):
        if R % t == 0:
            best = t
    return best if R % 8 == 0 else R


def _swap_halves(half):
    R, L = half.shape

    def body(in_ref, out_ref, send_sem, recv_sem, local_sem):
        x, y, c = _my_place()
        mine = pltpu.make_async_copy(in_ref, out_ref.at[c], local_sem)
        mine.start()
        cp = pltpu.make_async_remote_copy(src_ref=in_ref, dst_ref=out_ref.at[c], send_sem=send_sem, recv_sem=recv_sem,
                                          device_id=(x, y, 1 - c), device_id_type=MESH)
        cp.start()
        cp.wait()
        mine.wait()

    return pl.pallas_call(
        body, out_shape=jax.ShapeDtypeStruct((2, R, L), F32), in_specs=[ANY], out_specs=ANY,
        scratch_shapes=[pltpu.SemaphoreType.DMA, pltpu.SemaphoreType.DMA, pltpu.SemaphoreType.DMA], name="swap_halves",
    )(half)


def _allreduce_small(v):
    R, L = v.shape

    def body(in_ref, out_ref, buf, send_sems, recv_sems):
        x, y, c = _my_place()
        me = 4 * x + 2 * y + c
        buf[me] = in_ref[...]
        started = []
        for k in range(1, N_DEV):
            to = (x ^ (k >> 2), y ^ ((k >> 1) & 1), c ^ (k & 1))
            cp = pltpu.make_async_remote_copy(src_ref=in_ref, dst_ref=buf.at[me], send_sem=send_sems.at[k - 1],
                                              recv_sem=recv_sems.at[k - 1], device_id=to, device_id_type=MESH)
            cp.start()
            started.append(cp)
        for cp in started:
            cp.wait()
        acc = buf[0]
        for i in range(1, N_DEV):
            acc = acc + buf[i]
        out_ref[...] = acc

    vm = pl.BlockSpec(memory_space=pltpu.VMEM)
    return pl.pallas_call(
        body, out_shape=jax.ShapeDtypeStruct((R, L), F32), in_specs=[vm], out_specs=vm,
        scratch_shapes=[pltpu.VMEM((N_DEV, R, L), F32), pltpu.SemaphoreType.DMA((7,)), pltpu.SemaphoreType.DMA((7,))],
        name="allreduce_small",
    )(v)


def _adamw(name, w, g, m, v):
    shape = w.shape
    C = shape[-1]
    R = int(np.prod(shape[:-1]))
    args = [a.reshape(R, C).astype(F32) for a in (w, g, m, v)]
    tr = _tile_rows(R, 256)

    def body(w_ref, g_ref, m_ref, v_ref, d_ref, nm_ref, nv_ref):
        g = g_ref[...]
        m = ADAM_B1 * m_ref[...] + (1.0 - ADAM_B1) * g
        v = ADAM_B2 * v_ref[...] + (1.0 - ADAM_B2) * (g * g)
        m_hat = m / (1.0 - ADAM_B1 ** ADAM_STEP)
        v_hat = v / (1.0 - ADAM_B2 ** ADAM_STEP)
        d_ref[...] = -ADAM_LR * (m_hat / (jnp.sqrt(v_hat) + ADAM_EPS) + ADAM_WD * w_ref[...])
        nm_ref[...] = m
        nv_ref[...] = v

    spec = pl.BlockSpec((tr, C), lambda i: (i, 0))
    out = pl.pallas_call(
        body, grid=(R // tr,), in_specs=[spec] * 4, out_specs=[spec] * 3,
        out_shape=[jax.ShapeDtypeStruct((R, C), F32)] * 3, name="adamw_" + name, compiler_params=_params(("parallel",)),
    )(*args)
    return [o.reshape(shape) for o in out]


def kernel(x, mem, pre1_g, post1_g, pre2_g, post2_g, mem_norm_g, w_in, fox_f_bias, rwkv_mu, rwkv_w0, rwkv_w_up, rwkv_a0, rwkv_a_up, rwkv_g_up, rwkv_k_k, rwkv_k_a, rwkv_r_k, rwkv_gn_g, rwkv_gn_b, w_mem_kv, w_fox_out, w_rwkv_out, w_mem_out, w_o, w_ffn_gate, w_ffn_up, w_ffn_down, loss_target, m_pre1_g, m_post1_g, m_pre2_g, m_post2_g, m_mem_norm_g, m_w_in, m_fox_f_bias, m_rwkv_mu, m_rwkv_w0, m_rwkv_w_up, m_rwkv_a0, m_rwkv_a_up, m_rwkv_g_up, m_rwkv_k_k, m_rwkv_k_a, m_rwkv_r_k, m_rwkv_gn_g, m_rwkv_gn_b, m_w_mem_kv, m_w_fox_out, m_w_rwkv_out, m_w_mem_out, m_w_o, m_w_ffn_gate, m_w_ffn_up, m_w_ffn_down, v_pre1_g, v_post1_g, v_pre2_g, v_post2_g, v_mem_norm_g, v_w_in, v_fox_f_bias, v_rwkv_mu, v_rwkv_w0, v_rwkv_w_up, v_rwkv_a0, v_rwkv_a_up, v_rwkv_g_up, v_rwkv_k_k, v_rwkv_k_a, v_rwkv_r_k, v_rwkv_gn_g, v_rwkv_gn_b, v_w_mem_kv, v_w_fox_out, v_w_rwkv_out, v_w_mem_out, v_w_o, v_w_ffn_gate, v_w_ffn_up, v_w_ffn_down):
    given = dict(locals())
    w_loc = {n: given[n] for n in WEIGHTS}
    m_loc = {n: given["m_" + n] for n in WEIGHTS}
    v_loc = {n: given["v_" + n] for n in WEIGHTS}

    shard_shapes = [w_loc[n].shape[1:] for n in BIG]
    half_len = _packed_half_len(shard_shapes)
    gathered = _gather_weights(_pack([w_loc[n][0] for n in BIG], half_len, BF16))
    per_chip = [_unpack(gathered[s], shard_shapes) for s in range(N_CHIPS)]
    W = {n: _from_chip_shards(n, jnp.stack([per_chip[s][i] for s in range(N_CHIPS)])) for i, n in enumerate(BIG)}
    W.update({n: w_loc[n][0] for n in SMALL})

    loss, grad_x, gw = _layer_step(x, mem, loss_target, W)

    by_chip = [_to_chip_shards(n, gw[n]) for n in BIG]
    parts = jnp.stack([_pack([b[s] for b in by_chip], half_len, F32) for s in range(N_CHIPS)])
    half = _sum_devices(_scatter_grads(parts), "sum_grads")
    g_shard = dict(zip(BIG, _unpack(_swap_halves(half), shard_shapes)))

    small_shapes = [w_loc[n].shape[1:] for n in SMALL] + [(1,)]
    n_small = sum(int(np.prod(s)) for s in small_shapes)
    small_rows = -(-n_small // (8 * LANES)) * 8
    flat = jnp.concatenate([gw[n].reshape(-1) for n in SMALL] + [loss.reshape(1)])
    flat = jnp.pad(flat, (0, small_rows * LANES - n_small)).reshape(small_rows, LANES)
    small = _unpack(_allreduce_small(flat), small_shapes)
    g_small = dict(zip(SMALL, small[:-1]))
    loss = small[-1][0]

    grads, deltas, new_m, new_v = [], [], [], []
    for n in WEIGHTS:
        g = (g_shard[n] if n in g_shard else g_small[n]).reshape(w_loc[n].shape)
        d, nm, nv = _adamw(n, w_loc[n], g, m_loc[n], v_loc[n])
        grads.append(g)
        deltas.append(d)
        new_m.append(nm)
        new_v.append(nv)
    return (loss, grad_x, *grads, *deltas, *new_m, *new_v)
```

```python
import functools
import math

import numpy as np
import jax
import jax.numpy as jnp
from jax import lax
from jax.experimental import pallas as pl
from jax.experimental.pallas import tpu as pltpu

F32, BF16 = jnp.float32, jnp.bfloat16
MESH = pl.DeviceIdType.MESH

D_MODEL = 1024
HEAD_DIM = 64
N_HEADS = 8
BR_W = 512
MEM_HEADS = 4
MEM_HEAD_DIM = 128
D_FF = 2816
NORM_EPS = 1e-6
GN_EPS = 64e-5
N_CHIPS = 4
N_DEV = 8
LANES = 128
VMEM_LIMIT = 48 * 1024 * 1024

ADAM_LR, ADAM_B1, ADAM_B2, ADAM_EPS, ADAM_WD, ADAM_STEP = 0.001, 0.9, 0.999, 1e-08, 0.01, 10

FOX_COLS = 3 * BR_W + N_HEADS
RWKV_COLS = 3 * BR_W + 64 + 64 + 128
COL_QKV = (0, 3 * BR_W)
COL_F = (3 * BR_W, FOX_COLS)
COL_RW = (FOX_COLS, FOX_COLS + RWKV_COLS)
COL_MQ = (COL_RW[1], COL_RW[1] + BR_W)
COL_GATE = (COL_MQ[1], COL_MQ[1] + 3 * D_MODEL)

NT_DIMS = (((1,), (1,)), ((), ()))
TN_DIMS = (((0,), (0,)), ((), ()))


def _params(sem=None, **kw):
    return pltpu.CompilerParams(dimension_semantics=sem, vmem_limit_bytes=VMEM_LIMIT, **kw)


def _sigmoid(x):
    return 1.0 / (1.0 + jnp.exp(-x))


def _log_sigmoid(x):
    return jnp.minimum(x, 0.0) - jnp.log(1.0 + jnp.exp(-jnp.abs(x)))


def _bdot(a, b, dims=None):
    a, b = a.astype(BF16), b.astype(BF16)
    if dims is None:
        return jnp.dot(a, b, preferred_element_type=F32)
    return lax.dot_general(a, b, dims, preferred_element_type=F32)


def _hdot(a, b):
    return jnp.dot(a, b, precision=lax.Precision.HIGHEST, preferred_element_type=F32)


def _tile(n, cap):
    best = None
    for t in range(LANES, min(n, cap) + 1, LANES):
        if n % t == 0:
            best = t
    return best or n


def _rowwise(name, fn, rows, consts, outs, accs=(), tm=256):
    T = rows[0].shape[0]
    tm = min(tm, T)
    assert T % tm == 0
    nr, nc, no, na = len(rows), len(consts), len(outs), len(accs)

    def body(*refs):
        res = fn(*[r[...] for r in refs[:nr + nc]])
        if not isinstance(res, (tuple, list)):
            res = (res,)
        orefs, arefs = refs[nr + nc:nr + nc + no], refs[nr + nc + no:]
        for ref, val in zip(orefs, res[:no]):
            ref[...] = val.astype(ref.dtype)
        if na:
            @pl.when(pl.program_id(0) == 0)
            def _():
                for ref in arefs:
                    ref[...] = jnp.zeros(ref.shape, ref.dtype)
            for ref, val in zip(arefs, res[no:]):
                ref[...] += val

    in_specs = ([pl.BlockSpec((tm, r.shape[1]), lambda i: (i, 0)) for r in rows]
                + [pl.BlockSpec(c.shape, lambda i: (0, 0)) for c in consts])
    out_specs = ([pl.BlockSpec((tm, w), lambda i: (i, 0)) for w, _ in outs]
                 + [pl.BlockSpec(s, lambda i: (0, 0)) for s, _ in accs])
    out_shape = ([jax.ShapeDtypeStruct((T, w), dt) for w, dt in outs]
                 + [jax.ShapeDtypeStruct(s, dt) for s, dt in accs])
    return pl.pallas_call(
        body, grid=(T // tm,), in_specs=in_specs, out_specs=out_specs, out_shape=out_shape, name=name,
        compiler_params=_params(("arbitrary",) if na else ("parallel",)),
    )(*rows, *consts)


MM_TILE_CAP = 1408
MM_WHOLE_K = 2048


def _mm(name, a, b, ta=False, tb=False, out_dtype=F32, add=None):
    M, K = (a.shape[1], a.shape[0]) if ta else a.shape
    K2, N = (b.shape[1], b.shape[0]) if tb else b.shape
    assert K == K2
    tm, tn = _tile(M, MM_TILE_CAP), _tile(N, MM_TILE_CAP)
    tk = K if K <= MM_WHOLE_K else _tile(K, MM_TILE_CAP)
    assert M % tm == 0 and N % tn == 0 and K % tk == 0
    nk = K // tk
    a_dim, b_dim = (0 if ta else 1), (1 if tb else 0)

    def body(*refs):
        a_ref, b_ref = refs[0], refs[1]
        n_in = 2 if add is None else 3
        o_ref, acc = refs[n_in], (refs[n_in + 1] if nk > 1 else None)
        k = pl.program_id(2)
        part = lax.dot_general(a_ref[...].astype(BF16), b_ref[...].astype(BF16),
                               (((a_dim,), (b_dim,)), ((), ())), preferred_element_type=F32)

        def finish(r):
            if add is not None:
                r = r + refs[2][...].astype(F32)
            o_ref[...] = r.astype(o_ref.dtype)

        if nk == 1:
            finish(part)
            return

        @pl.when(k == 0)
        def _():
            acc[...] = part

        @pl.when(k > 0)
        def _():
            acc[...] += part

        @pl.when(k == nk - 1)
        def _():
            finish(acc[...])

    a_spec = pl.BlockSpec((tk, tm), lambda i, j, k: (k, i)) if ta else pl.BlockSpec((tm, tk), lambda i, j, k: (i, k))
    b_spec = pl.BlockSpec((tn, tk), lambda i, j, k: (j, k)) if tb else pl.BlockSpec((tk, tn), lambda i, j, k: (k, j))
    o_spec = pl.BlockSpec((tm, tn), lambda i, j, k: (i, j))
    ins, in_specs = [a, b], [a_spec, b_spec]
    if add is not None:
        ins.append(add)
        in_specs.append(o_spec)
    return pl.pallas_call(
        body, grid=(M // tm, N // tn, nk), in_specs=in_specs, out_specs=o_spec,
        out_shape=jax.ShapeDtypeStruct((M, N), out_dtype), scratch_shapes=[pltpu.VMEM((tm, tn), F32)] if nk > 1 else [],
        name=name, compiler_params=_params(("parallel", "parallel", "arbitrary")),
    )(*ins)


def _rowsum(x):
    return jnp.sum(x, axis=0, keepdims=True)


def _rms_stat(x):
    return lax.rsqrt(jnp.mean(x * x, axis=-1, keepdims=True) + NORM_EPS)


def _rms_bwd(dy, x, g):
    r = _rms_stat(x)
    xn = x * r
    dxn = dy * g
    dx = r * (dxn - xn * jnp.mean(dxn * xn, axis=-1, keepdims=True))
    return dx, _rowsum(dy * xn)


def _fox_c_fwd(f8t, bias_col, tc=256):
    B, H, S = f8t.shape
    tc = min(tc, S)

    def body(f_ref, b_ref, c_ref, carry):
        @pl.when(pl.program_id(1) == 0)
        def _():
            carry[...] = jnp.zeros(carry.shape, F32)
        lf = _log_sigmoid(f_ref[...] + b_ref[...])
        row = lax.broadcasted_iota(jnp.int32, (tc, tc), 0)
        col = lax.broadcasted_iota(jnp.int32, (tc, tc), 1)
        c = _hdot(lf, (row <= col).astype(F32)) + carry[...]
        c_ref[...] = c
        carry[...] = c[:, tc - 1:tc]

    return pl.pallas_call(
        body, grid=(B, S // tc),
        in_specs=[pl.BlockSpec((None, H, tc), lambda b, i: (b, 0, i)), pl.BlockSpec((H, 1), lambda b, i: (0, 0))],
        out_specs=pl.BlockSpec((None, H, tc), lambda b, i: (b, 0, i)),
        out_shape=jax.ShapeDtypeStruct((B, H, S), F32), scratch_shapes=[pltpu.VMEM((H, 1), F32)], name="fox_c_fwd",
        compiler_params=_params(("parallel", "arbitrary")),
    )(f8t, bias_col)


def _fox_c_bwd(dc, f8t, bias_col, tc=256):
    B, H, S = f8t.shape
    tc = min(tc, S)
    n = S // tc

    def body(dc_ref, f_ref, b_ref, df_ref, db_ref, carry):
        @pl.when(pl.program_id(1) == 0)
        def _():
            carry[...] = jnp.zeros(carry.shape, F32)
            db_ref[...] = jnp.zeros(db_ref.shape, F32)
        row = lax.broadcasted_iota(jnp.int32, (tc, tc), 0)
        col = lax.broadcasted_iota(jnp.int32, (tc, tc), 1)
        dlf = _hdot(dc_ref[...], (row >= col).astype(F32)) + carry[...]
        z = f_ref[...] + b_ref[...]
        df = dlf * (1.0 - _sigmoid(z))
        df_ref[...] = df
        db_ref[...] += jnp.sum(df, axis=1, keepdims=True)
        carry[...] = dlf[:, 0:1]

    rev = lambda b, i: (b, 0, n - 1 - i)
    return pl.pallas_call(
        body, grid=(B, n),
        in_specs=[pl.BlockSpec((None, H, tc), rev), pl.BlockSpec((None, H, tc), rev), pl.BlockSpec((H, 1), lambda b, i: (0, 0))],
        out_specs=[pl.BlockSpec((None, H, tc), rev), pl.BlockSpec((None, H, 1), lambda b, i: (b, 0, 0))],
        out_shape=[jax.ShapeDtypeStruct((B, H, S), F32), jax.ShapeDtypeStruct((B, H, 1), F32)],
        scratch_shapes=[pltpu.VMEM((H, 1), F32)], name="fox_c_bwd",
        compiler_params=_params(("parallel", "arbitrary")),
    )(dc, f8t, bias_col)


NEG_BIG = -1e30


def _fox_logits(q, kj, cq, ckj, i, j, t, scale):
    s = _bdot(q, kj, NT_DIMS) * scale + (cq - ckj)
    row = lax.broadcasted_iota(jnp.int32, (t, t), 0)
    col = lax.broadcasted_iota(jnp.int32, (t, t), 1)
    return s, col <= row + (i - j) * t


def _fox_fwd(q, k, v, cq, ck, t):
    G, n = q.shape[0], q.shape[1]
    scale = HEAD_DIM ** -0.5

    def body(q_ref, k_ref, v_ref, cq_ref, ck_ref, o_ref, lse_ref):
        i = pl.program_id(1)
        qi, cqi = q_ref[...], cq_ref[...]

        def step(j, carry):
            m, l, acc = carry
            s, ok = _fox_logits(qi, k_ref[j], cqi, ck_ref[j], i, j, t, scale)
            s = jnp.where(ok, s, NEG_BIG)
            m2 = jnp.maximum(m, jnp.max(s, axis=1, keepdims=True))
            p = jnp.exp(s - m2)
            al = jnp.exp(m - m2)
            return m2, al * l + jnp.sum(p, axis=1, keepdims=True), al * acc + _bdot(p, v_ref[j])

        m, l, acc = lax.fori_loop(0, i + 1, step, (jnp.full((t, 1), NEG_BIG, F32), jnp.zeros((t, 1), F32),
                                                   jnp.zeros((t, HEAD_DIM), F32)))
        o_ref[...] = (acc / l).astype(o_ref.dtype)
        lse_ref[...] = m + jnp.log(l)

    blk = pl.BlockSpec((None, None, t, HEAD_DIM), lambda g, i: (g, i, 0, 0))
    full = pl.BlockSpec((None, n, t, HEAD_DIM), lambda g, i: (g, 0, 0, 0))
    col = pl.BlockSpec((None, None, t, 1), lambda g, i: (g, i, 0, 0))
    return pl.pallas_call(
        body, grid=(G, n), in_specs=[blk, full, full, col, pl.BlockSpec((None, n, 1, t), lambda g, i: (g, 0, 0, 0))],
        out_specs=[blk, col],
        out_shape=[jax.ShapeDtypeStruct(q.shape, BF16), jax.ShapeDtypeStruct((G, n, t, 1), F32)], name="fox_fwd",
        compiler_params=_params(("parallel", "parallel")),
    )(q, k, v, cq, ck)


def _fox_bwd(q, k, v, o, do, cq, ck, lse, t):
    G, n = q.shape[0], q.shape[1]
    scale = HEAD_DIM ** -0.5

    def body(q_ref, k_ref, v_ref, o_ref, do_ref, cq_ref, ck_ref, lse_ref, dq_ref, dk_ref, dv_ref, dck_ref, dcq_ref):
        dk_ref[...] = jnp.zeros(dk_ref.shape, F32)
        dv_ref[...] = jnp.zeros(dv_ref.shape, F32)
        dck_ref[...] = jnp.zeros(dck_ref.shape, F32)

        def qloop(i, _):
            qi, doi, cqi, lsei = q_ref[i], do_ref[i], cq_ref[i], lse_ref[i]
            delta = jnp.sum(doi.astype(F32) * o_ref[i].astype(F32), axis=1, keepdims=True)

            def kloop(j, carry):
                dq, dcq = carry
                kj, vj = k_ref[j], v_ref[j]
                s, ok = _fox_logits(qi, kj, cqi, ck_ref[j], i, j, t, scale)
                p = jnp.where(ok, jnp.exp(s - lsei), 0.0)
                ds = p * (_bdot(doi, vj, NT_DIMS) - delta)
                dv_ref[j] += _bdot(p, doi, TN_DIMS)
                dk_ref[j] += _bdot(ds, qi, TN_DIMS) * scale
                dck_ref[j] += -_rowsum(ds)
                return dq + _bdot(ds, kj) * scale, dcq + jnp.sum(ds, axis=1, keepdims=True)

            dq_ref[i], dcq_ref[i] = lax.fori_loop(0, i + 1, kloop, (jnp.zeros((t, HEAD_DIM), F32), jnp.zeros((t, 1), F32)))
            return 0

        lax.fori_loop(0, n, qloop, 0)

    full = pl.BlockSpec((None, n, t, HEAD_DIM), lambda g: (g, 0, 0, 0))
    col = pl.BlockSpec((None, n, t, 1), lambda g: (g, 0, 0, 0))
    row = pl.BlockSpec((None, n, 1, t), lambda g: (g, 0, 0, 0))
    f32s = jax.ShapeDtypeStruct(q.shape, F32)
    return pl.pallas_call(
        body, grid=(G,), in_specs=[full, full, full, full, full, col, row, col], out_specs=[full, full, full, row, col],
        out_shape=[f32s, f32s, f32s, jax.ShapeDtypeStruct((G, n, 1, t), F32), jax.ShapeDtypeStruct((G, n, t, 1), F32)],
        name="fox_bwd", compiler_params=_params(("parallel",)),
    )(q, k, v, o, do, cq, ck, lse)


def _mem_probs(qh, kh):
    s = _bdot(qh, kh, NT_DIMS) * (MEM_HEAD_DIM ** -0.5)
    e = jnp.exp(s - jnp.max(s, axis=1, keepdims=True))
    return e / jnp.sum(e, axis=1, keepdims=True)


def _mem_fwd(q, mem_kv, B, tq=512):
    T = q.shape[0]
    S, Lm = T // B, mem_kv.shape[0] // B
    tq = min(tq, S)
    n = S // tq

    def body(q_ref, k_ref, v_ref, o_ref):
        for h in range(MEM_HEADS):
            sl = slice(h * MEM_HEAD_DIM, (h + 1) * MEM_HEAD_DIM)
            p = _mem_probs(q_ref[:, sl], k_ref[:, sl])
            o_ref[:, sl] = _bdot(p, v_ref[:, sl]).astype(o_ref.dtype)

    qs = pl.BlockSpec((tq, BR_W), lambda b, i: (b * n + i, 0))
    return pl.pallas_call(
        body, grid=(B, n),
        in_specs=[qs, pl.BlockSpec((Lm, BR_W), lambda b, i: (b, 0)), pl.BlockSpec((Lm, BR_W), lambda b, i: (b, 1))],
        out_specs=qs, out_shape=jax.ShapeDtypeStruct((T, BR_W), BF16), name="mem_fwd",
        compiler_params=_params(("parallel", "parallel")),
    )(q, mem_kv, mem_kv)


def _mem_bwd(q, mem_kv, do, B, tq=512):
    T = q.shape[0]
    S, Lm = T // B, mem_kv.shape[0] // B
    tq = min(tq, S)
    n = S // tq
    scale = MEM_HEAD_DIM ** -0.5

    def body(q_ref, k_ref, v_ref, do_ref, dq_ref, dk_ref, dv_ref):
        @pl.when(pl.program_id(1) == 0)
        def _():
            dk_ref[...] = jnp.zeros(dk_ref.shape, F32)
            dv_ref[...] = jnp.zeros(dv_ref.shape, F32)
        for h in range(MEM_HEADS):
            sl = slice(h * MEM_HEAD_DIM, (h + 1) * MEM_HEAD_DIM)
            qh, kh, vh, doh = q_ref[:, sl], k_ref[:, sl], v_ref[:, sl], do_ref[:, sl]
            p = _mem_probs(qh, kh)
            dp = _bdot(doh, vh, NT_DIMS)
            ds = p * (dp - jnp.sum(p * dp, axis=1, keepdims=True))
            dq_ref[:, sl] = (_bdot(ds, kh) * scale).astype(dq_ref.dtype)
            dk_ref[:, sl] += _bdot(ds, qh, TN_DIMS) * scale
            dv_ref[:, sl] += _bdot(p, doh, TN_DIMS)

    qs = pl.BlockSpec((tq, BR_W), lambda b, i: (b * n + i, 0))
    kv = pl.BlockSpec((Lm, BR_W), lambda b, i: (b, 0))
    return pl.pallas_call(
        body, grid=(B, n),
        in_specs=[qs, kv, pl.BlockSpec((Lm, BR_W), lambda b, i: (b, 1)), qs], out_specs=[qs, kv, kv],
        out_shape=[jax.ShapeDtypeStruct((T, BR_W), BF16), jax.ShapeDtypeStruct((B * Lm, BR_W), F32),
                   jax.ShapeDtypeStruct((B * Lm, BR_W), F32)], name="mem_bwd",
        compiler_params=_params(("parallel", "arbitrary")),
    )(q, mem_kv, mem_kv, do)


def _head_ones():
    h = np.arange(BR_W) // HEAD_DIM
    return jnp.asarray((h[:, None] == h[None, :]).astype(np.float32))


def _rw_prep(p, pp, mu, w0, w1, a0, w2, g_up, k_k, k_a, bd):
    ps = p + (pp - p) * mu
    r, k, v = ps[:, 0:512], ps[:, 512:1024], ps[:, 1024:1536]
    wa, gd = ps[:, 1536:1664], ps[:, 1664:1792]
    th = jnp.tanh(wa)
    z = w0 + _bdot(th, w1)
    wl = -jnp.exp(_log_sigmoid(z) - 0.5)
    w = jnp.exp(wl)
    a = _sigmoid(a0 + _bdot(wa, w2))
    sg = _sigmoid(gd)
    g = _bdot(sg, g_up)
    kq = k * k_k
    n2 = _hdot(kq * kq, bd)
    inv = lax.rsqrt(jnp.maximum(n2, 1e-24))
    kk = kq * inv
    k2 = k * (1.0 + (a - 1.0) * k_a)
    return dict(ps=ps, r=r, k=k, v=v, wa=wa, th=th, z=z, wl=wl, w=w, a=a, sg=sg, g=g, kq=kq, n2=n2, inv=inv, kk=kk, k2=k2)


def _rw_prep_fwd(p, pp, consts):
    def fn(p, pp, *c):
        t = _rw_prep(p, pp, *c)
        return t["r"], t["w"], t["k2"], t["v"], -t["kk"], t["kk"] * t["a"], t["g"]
    return _rowwise("rwkv_prep_fwd", fn, [p, pp], consts, [(BR_W, F32)] * 7, tm=256)


def _rw_prep_bwd(p, pp, cots, consts):
    def fn(p, pp, dr1, dr2, dw, dk21, dk22, dv1, dv2, dav, dbv, dg, mu, w0, w1, a0, w2, g_up, k_k, k_a, bd):
        t = _rw_prep(p, pp, mu, w0, w1, a0, w2, g_up, k_k, k_a, bd)
        dr, dk2, dv = dr1 + dr2, dk21 + dk22, dv1 + dv2
        a, k, kk, kq, inv = t["a"], t["k"], t["kk"], t["kq"], t["inv"]
        dkk = dbv * a - dav
        da = dbv * kk + dk2 * k * k_a
        dk = dk2 * (1.0 + (a - 1.0) * k_a)
        d_k_a = _rowsum(dk2 * k * (a - 1.0))
        proj = _hdot(dkk * kq, bd)
        dkq = dkk * inv - jnp.where(t["n2"] > 1e-24, kq * inv * inv * inv * proj, 0.0)
        dk = dk + dkq * k_k
        d_k_k = _rowsum(dkq * k)
        dpa = da * a * (1.0 - a)
        d_a0 = _rowsum(dpa)
        dwa = _bdot(dpa, w2, NT_DIMS)
        d_w2 = _bdot(t["wa"], dpa, TN_DIMS)
        dz = dw * t["w"] * t["wl"] * (1.0 - _sigmoid(t["z"]))
        d_w0 = _rowsum(dz)
        th = t["th"]
        dwa = dwa + _bdot(dz, w1, NT_DIMS) * (1.0 - th * th)
        d_w1 = _bdot(th, dz, TN_DIMS)
        sg = t["sg"]
        dgd = _bdot(dg, g_up, NT_DIMS) * sg * (1.0 - sg)
        d_g_up = _bdot(sg, dg, TN_DIMS)
        dps = jnp.concatenate([dr, dk, dv, dwa, dgd], axis=1)
        d_mu = _rowsum(dps * (pp - p))
        return dps * (1.0 - mu), dps * mu, d_mu, d_w0, d_w1, d_a0, d_w2, d_g_up, d_k_k, d_k_a

    accs = [((1, RWKV_COLS), F32), ((1, BR_W), F32), ((LANES, BR_W), F32), ((1, BR_W), F32), ((LANES, BR_W), F32),
            ((LANES, BR_W), F32), ((1, BR_W), F32), ((1, BR_W), F32)]
    return _rowwise("rwkv_prep_bwd", fn, [p, pp] + list(cots), consts, [(RWKV_COLS, F32)] * 2, accs, tm=128)


def _rw_head(y, r, k2, v, g, gn_g, gn_b, r_k, bd):
    mean = _hdot(y, bd) * (1.0 / HEAD_DIM)
    yc = y - mean
    rs = lax.rsqrt(_hdot(yc * yc, bd) * (1.0 / HEAD_DIM) + GN_EPS)
    yn = yc * rs
    bs = _hdot(r * k2 * r_k, bd)
    return yn, rs, bs, yn * gn_g + gn_b + bs * v


def _rw_head_fwd(y, r, k2, v, g, consts):
    def fn(y, r, k2, v, g, *c):
        return _rw_head(y, r, k2, v, g, *c)[3] * g
    return _rowwise("rwkv_head_fwd", fn, [y, r, k2, v, g], consts, [(BR_W, BF16)])[0]


def _rw_head_bwd(dout, y, r, k2, v, g, consts):
    def fn(dout, y, r, k2, v, g, gn_g, gn_b, r_k, bd):
        dout = dout.astype(F32)
        yn, rs, bs, zz = _rw_head(y, r, k2, v, g, gn_g, gn_b, r_k, bd)
        dg = dout * zz
        dz = dout * g
        dyn = dz * gn_g
        inv_n = 1.0 / HEAD_DIM
        dy = rs * (dyn - _hdot(dyn, bd) * inv_n - yn * (_hdot(dyn * yn, bd) * inv_n))
        dq = _hdot(dz * v, bd)
        return dy, dg, dq * k2 * r_k, dq * r * r_k, dz * bs, _rowsum(dz * yn), _rowsum(dz), _rowsum(dq * r * k2)
    return _rowwise("rwkv_head_bwd", fn, [dout, y, r, k2, v, g], consts, [(BR_W, F32)] * 5, [((1, BR_W), F32)] * 3)


SCAN_TC = 32


def _scan_onehot(Tc):
    w = np.zeros((Tc // 2, 2 * Tc, 2 * LANES), np.float32)
    for tt in range(Tc // 2):
        for u in range(2):
            for h in range(2):
                w[tt, h * Tc + 2 * tt + u, u * LANES + h * HEAD_DIM: u * LANES + (h + 1) * HEAD_DIM] = 1.0
    return jnp.asarray(w, BF16)


def _to_keycols(x, B, Tc):
    S = x.shape[0] // B
    x = x.reshape(B, S // Tc, Tc, 4, 2, HEAD_DIM).transpose(0, 3, 1, 5, 4, 2)
    return x.reshape(B * 4, S // Tc, HEAD_DIM, 2 * Tc)


def _split_bf16(x):
    hi = x.astype(BF16)
    return hi, (x - hi.astype(F32)).astype(BF16)


def _key_tiles(l_w, others, onehot):
    dot = lambda x: jnp.dot(x, onehot, preferred_element_type=F32)
    whi, wmid = l_w
    return [dot(whi) + dot(wmid)] + [dot(o) for o in others]


def _rw_scan_fwd(LW, LA, LB, LK, LR, v, P=2):
    NP, nc, _, Tc2 = LW.shape
    Tc = Tc2 // 2
    S = nc * Tc
    onehot = _scan_onehot(Tc)
    npb = 4 // P

    def body(lw, la, lb, lk, lr, v_ref, oh_ref, y_ref, sa_ref, sb_ref, st):
        @pl.when(pl.program_id(1) == 0)
        def _():
            st[...] = jnp.zeros(st.shape, F32)
        s = [st[p] for p in range(P)]
        cols = [(_split_bf16(lw[p]), [ref[p].astype(BF16) for ref in (la, lb, lk, lr)]) for p in range(P)]
        for tt in range(Tc // 2):
            tiles = [_key_tiles(c[0], c[1], oh_ref[tt]) for c in cols]
            for u in range(2):
                t = 2 * tt + u
                for p in range(P):
                    W, A, Bt, Kt, R = (x[:, u * LANES:(u + 1) * LANES] for x in tiles[p])
                    ls = slice(p * LANES, (p + 1) * LANES)
                    sb_ref[p, t] = s[p]
                    sa = _rowsum(s[p] * A)
                    s[p] = s[p] * W + Bt * sa + Kt * v_ref[t:t + 1, ls]
                    y_ref[t:t + 1, ls] = _rowsum(s[p] * R)
                    sa_ref[t:t + 1, ls] = sa
        for p in range(P):
            st[p] = s[p]

    lspec = pl.BlockSpec((P, None, HEAD_DIM, Tc2), lambda g, c: (g, c, 0, 0))
    rows = pl.BlockSpec((Tc, P * LANES), lambda g, c: ((g // npb) * nc + c, g % npb))
    rowshape = jax.ShapeDtypeStruct(v.shape, F32)
    return pl.pallas_call(
        body, grid=(NP // P, nc), in_specs=[lspec] * 5 + [rows, pl.BlockSpec(onehot.shape, lambda g, c: (0, 0, 0))],
        out_specs=[rows, rows, pl.BlockSpec((P, Tc, HEAD_DIM, LANES), lambda g, c: (g, c, 0, 0))],
        out_shape=[rowshape, rowshape, jax.ShapeDtypeStruct((NP, S, HEAD_DIM, LANES), F32)],
        scratch_shapes=[pltpu.VMEM((P, HEAD_DIM, LANES), F32)], name="rwkv_scan_fwd",
        compiler_params=_params(("parallel", "arbitrary")),
    )(LW, LA, LB, LK, LR, v, onehot)


SCAN_G_ROWS = 16


def _rw_scan_bwd(LW, LA, LB, LK, LR, v, sa, dy, sb, P=4):
    NP, nc, _, Tc2 = LW.shape
    Tc = Tc2 // 2
    onehot = _scan_onehot(Tc)
    npb = 4 // P

    def body(lw, la, lb, lk, lr, v_ref, sa_ref, dy_ref, sb_ref, oh_ref, dv_ref, dk_ref, db_ref, dw_ref, dr_ref, da_ref, dst):
        @pl.when(pl.program_id(1) == 0)
        def _():
            dst[...] = jnp.zeros(dst.shape, F32)
        rid = lax.broadcasted_iota(jnp.int32, (SCAN_G_ROWS, LANES), 0)
        lane = lax.broadcasted_iota(jnp.int32, (SCAN_G_ROWS, LANES), 1)
        own = ((rid % 2) == 0) == (lane < HEAD_DIM)
        m_kb, m_w, m_r, m_a = (own & (rid >= lo) & (rid < hi) for lo, hi in ((0, 4), (4, 6), (6, 8), (8, 10)))
        nt = lambda rows, tile: lax.dot_general(rows.astype(BF16), tile.astype(BF16), NT_DIMS, preferred_element_type=F32)
        ds = [dst[p] for p in range(P)]
        cols = [(_split_bf16(lw[p]), [ref[p].astype(BF16) for ref in (la, lb, lk, lr)]) for p in range(P)]
        for tt in reversed(range(Tc // 2)):
            tiles = [_key_tiles(c[0], c[1], oh_ref[tt]) for c in cols]
            for u in (1, 0):
                t = 2 * tt + u
                for p in range(P):
                    W, A, Bt, Kt, R = (x[:, u * LANES:(u + 1) * LANES] for x in tiles[p])
                    ls = slice(p * LANES, (p + 1) * LANES)
                    vr, sar, dyr = (ref[t:t + 1, ls] for ref in (v_ref, sa_ref, dy_ref))
                    sp = sb_ref[p, t]
                    s_t = sp * W + Bt * sar + Kt * vr
                    d = ds[p] + R * dyr
                    dv_ref[t:t + 1, ls] = _rowsum(d * Kt)
                    dsar = _rowsum(d * Bt)
                    g = nt(jnp.where(m_kb, jnp.where(rid < 2, vr, sar), 0.0), d)
                    g = g + nt(jnp.where(m_w, 1.0, 0.0), d * sp)
                    g = g + nt(jnp.where(m_r, dyr, 0.0), s_t)
                    g = g + nt(jnp.where(m_a, dsar, 0.0), sp)
                    for q, ref in enumerate((dk_ref, db_ref, dw_ref, dr_ref, da_ref)):
                        ref[t:t + 1, ls] = jnp.concatenate([g[2 * q:2 * q + 1], g[2 * q + 1:2 * q + 2]], axis=1)
                    ds[p] = d * W + A * dsar
        for p in range(P):
            dst[p] = ds[p]

    rev = lambda g, c: (g, nc - 1 - c, 0, 0)
    lspec = pl.BlockSpec((P, None, HEAD_DIM, Tc2), rev)
    rows = pl.BlockSpec((Tc, P * LANES), lambda g, c: ((g // npb) * nc + nc - 1 - c, g % npb))
    return pl.pallas_call(
        body, grid=(NP // P, nc),
        in_specs=[lspec] * 5 + [rows] * 3 + [pl.BlockSpec((P, Tc, HEAD_DIM, LANES), rev),
                                             pl.BlockSpec(onehot.shape, lambda g, c: (0, 0, 0))],
        out_specs=[rows] * 6, out_shape=[jax.ShapeDtypeStruct(v.shape, F32)] * 6,
        scratch_shapes=[pltpu.VMEM((P, HEAD_DIM, LANES), F32)], name="rwkv_scan_bwd",
        compiler_params=_params(("parallel", "arbitrary")),
    )(LW, LA, LB, LK, LR, v, sa, dy, sb, onehot)


def _shift_prev(p, B):
    T, W = p.shape
    return jnp.pad(p.reshape(B, T // B, W), ((0, 0), (1, 0), (0, 0)))[:, :-1].reshape(T, W)


def _shift_next(p, B):
    T, W = p.shape
    return jnp.pad(p.reshape(B, T // B, W), ((0, 0), (0, 1), (0, 0)))[:, 1:].reshape(T, W)


def _heads_split(x, B, t):
    T = x.shape[0]
    S = T // B
    return x.reshape(B, S, N_HEADS, HEAD_DIM).transpose(0, 2, 1, 3).reshape(B * N_HEADS, S // t, t, HEAD_DIM)


def _heads_merge(x, B):
    G, n, t, _ = x.shape
    return x.reshape(B, N_HEADS, n * t, HEAD_DIM).transpose(0, 2, 1, 3).reshape(B * n * t, BR_W)


def _layer_step(x, mem, target, W, scan_tc=SCAN_TC, fox_t=256):
    B, S, _ = x.shape
    T = B * S
    x2, tgt2 = x.reshape(T, D_MODEL), target.reshape(T, D_MODEL)
    mem2 = mem.reshape(-1, D_MODEL)
    w_in = W["w_in"]
    w_qkv, w_rw, w_mq, w_gate = (w_in[:, lo:hi] for lo, hi in (COL_QKV, COL_RW, COL_MQ, COL_GATE))
    w_f = jnp.pad(w_in[:, COL_F[0]:COL_F[1]], ((0, 0), (0, LANES - N_HEADS)))
    row = lambda v: v.reshape(1, -1).astype(F32)
    pre1_g, post1_g, pre2_g, post2_g, mem_g = (row(W[n]) for n in ("pre1_g", "post1_g", "pre2_g", "post2_g", "mem_norm_g"))

    u = _rowwise("rms_pre1", lambda x, g: x * _rms_stat(x) * g, [x2], [pre1_g], [(D_MODEL, BF16)])[0]
    qkv = _mm("proj_qkv", u, w_qkv, out_dtype=BF16)
    f_pad = _mm("proj_f", u, w_f)
    p_rw = _mm("proj_rwkv", u, w_rw)
    memq = _mm("proj_memq", u, w_mq, out_dtype=BF16)
    gate = _mm("proj_gate", u, w_gate)

    t = min(fox_t, S)
    bias_col = W["fox_f_bias"].reshape(N_HEADS, 1).astype(F32)
    f8t = f_pad[:, :N_HEADS].reshape(B, S, N_HEADS).transpose(0, 2, 1)
    c = _fox_c_fwd(f8t, bias_col)
    G = B * N_HEADS
    cq, ck = c.reshape(G, S // t, t, 1), c.reshape(G, S // t, 1, t)
    fq, fk, fv = (_heads_split(qkv[:, i * BR_W:(i + 1) * BR_W], B, t) for i in range(3))
    fo4, lse = _fox_fwd(fq, fk, fv, cq, ck, t)
    fox_out = _heads_merge(fo4, B)

    bd = _head_ones()
    zpad = jnp.zeros((64, BR_W), F32)
    w1 = jnp.concatenate([W["rwkv_w_up"].astype(F32), zpad], axis=0)
    w2 = jnp.concatenate([zpad, W["rwkv_a_up"].astype(F32)], axis=0)
    prep_consts = [row(W["rwkv_mu"]), row(W["rwkv_w0"]), w1, row(W["rwkv_a0"]), w2, W["rwkv_g_up"].astype(F32),
                   row(W["rwkv_k_k"]), row(W["rwkv_k_a"]), bd]
    p_prev = _shift_prev(p_rw, B)
    rr, rw, rk2, rv, rav, rbv, rg = _rw_prep_fwd(p_rw, p_prev, prep_consts)
    scan_cols = [_to_keycols(v, B, scan_tc) for v in (rw, rav, rbv, rk2, rr)]
    ry, rsa, sb = _rw_scan_fwd(*scan_cols, rv)
    head_consts = [row(W["rwkv_gn_g"]), row(W["rwkv_gn_b"]), row(W["rwkv_r_k"]), bd]
    rwkv_out = _rw_head_fwd(ry, rr, rk2, rv, rg, head_consts)

    mn = _rowwise("rms_mem", lambda m, g: m * _rms_stat(m) * g, [mem2], [mem_g], [(D_MODEL, BF16)])[0]
    mem_kv = _mm("proj_memkv", mn, W["w_mem_kv"], out_dtype=BF16)
    mem_out = _mem_fwd(memq, mem_kv, B)

    fo = [_mm("branch_" + n, a, W[n]) for n, a in (("w_fox_out", fox_out), ("w_rwkv_out", rwkv_out), ("w_mem_out", mem_out))]

    def merge(gate, f0, f1, f2):
        return sum(_sigmoid(gate[:, i * D_MODEL:(i + 1) * D_MODEL]) * f for i, f in enumerate((f0, f1, f2)))
    merged = _rowwise("merge", merge, [gate] + fo, [], [(D_MODEL, BF16)])[0]
    y1 = _mm("proj_o", merged, W["w_o"])

    def mid(x, y1, g1, g2):
        h1 = x + y1 * _rms_stat(y1) * g1
        return h1, h1 * _rms_stat(h1) * g2
    h1, u2 = _rowwise("norm_mid", mid, [x2, y1], [post1_g, pre2_g], [(D_MODEL, F32), (D_MODEL, BF16)])
    gt = _mm("ffn_gate", u2, W["w_ffn_gate"])
    up = _mm("ffn_up", u2, W["w_ffn_up"])
    act = _rowwise("swiglu", lambda gt, up: gt * _sigmoid(gt) * up, [gt, up], [], [(D_FF, BF16)])[0]
    ffn = _mm("ffn_down", act, W["w_ffn_down"])

    def tail(h1, ffn, tgt, g):
        err = h1 + ffn * _rms_stat(ffn) * g - tgt
        dh2 = err * (1.0 / D_MODEL)
        dffn, dg = _rms_bwd(dh2, ffn, g)
        loss = 0.5 * jnp.sum(jnp.sum(err * err, axis=1, keepdims=True) * (1.0 / D_MODEL), axis=0, keepdims=True)
        return dh2, dffn, dg, jnp.broadcast_to(loss, (1, LANES))
    dh2, dffn, d_post2, loss = _rowwise("loss_tail", tail, [h1, ffn, tgt2], [post2_g], [(D_MODEL, F32), (D_MODEL, BF16)],
                                        [((1, D_MODEL), F32), ((1, LANES), F32)])
    gw = {"post2_g": d_post2}
    dact = _mm("d_act", dffn, W["w_ffn_down"], tb=True)
    gw["w_ffn_down"] = _mm("g_ffn_down", act, dffn, ta=True, out_dtype=BF16)

    def swiglu_bwd(dact, gt, up):
        s = _sigmoid(gt)
        return dact * up * s * (1.0 + gt * (1.0 - s)), dact * gt * s
    dgt, dup = _rowwise("swiglu_bwd", swiglu_bwd, [dact, gt, up], [], [(D_FF, BF16)] * 2)
    du2 = _mm("d_u2_gate", dgt, W["w_ffn_gate"], tb=True)
    du2 = _mm("d_u2_up", dup, W["w_ffn_up"], tb=True, add=du2)
    gw["w_ffn_gate"] = _mm("g_ffn_gate", u2, dgt, ta=True, out_dtype=BF16)
    gw["w_ffn_up"] = _mm("g_ffn_up", u2, dup, ta=True, out_dtype=BF16)

    def mid_bwd(du2, dh2, h1, y1, g1, g2):
        dh1_n, d_pre2 = _rms_bwd(du2, h1, g2)
        dh1 = dh2 + dh1_n
        dy1, d_post1 = _rms_bwd(dh1, y1, g1)
        return dh1, dy1, d_post1, d_pre2
    dh1, dy1, gw["post1_g"], gw["pre2_g"] = _rowwise(
        "norm_mid_bwd", mid_bwd, [du2, dh2, h1, y1], [post1_g, pre2_g], [(D_MODEL, F32), (D_MODEL, BF16)],
        [((1, D_MODEL), F32)] * 2)
    dmerged = _mm("d_merged", dy1, W["w_o"], tb=True)
    gw["w_o"] = _mm("g_w_o", merged, dy1, ta=True, out_dtype=BF16)

    def merge_bwd(dm, gate, f0, f1, f2):
        s = [_sigmoid(gate[:, i * D_MODEL:(i + 1) * D_MODEL]) for i in range(3)]
        dgate = jnp.concatenate([dm * f * si * (1.0 - si) for f, si in zip((f0, f1, f2), s)], axis=1)
        return dm * s[0], dm * s[1], dm * s[2], dgate
    dfo0, dfo1, dfo2, dgate = _rowwise("merge_bwd", merge_bwd, [dmerged, gate] + fo, [],
                                       [(D_MODEL, BF16)] * 3 + [(3 * D_MODEL, BF16)])
    d_branch = {}
    for n, a, dfo in (("w_fox_out", fox_out, dfo0), ("w_rwkv_out", rwkv_out, dfo1), ("w_mem_out", mem_out, dfo2)):
        d_branch[n] = _mm("d_in_" + n, dfo, W[n], tb=True, out_dtype=BF16)
        gw[n] = _mm("g_" + n, a, dfo, ta=True, out_dtype=BF16)

    dmemq, dkm, dvm = _mem_bwd(memq, mem_kv, d_branch["w_mem_out"], B)
    dmem_kv = jnp.concatenate([dkm, dvm], axis=1)
    gw["w_mem_kv"] = _mm("g_w_mem_kv", mn, dmem_kv, ta=True, out_dtype=BF16)
    dmn = _mm("d_mn", dmem_kv, W["w_mem_kv"], tb=True)
    gw["mem_norm_g"] = _rowwise("rms_mem_bwd", lambda d, m, g: _rms_bwd(d, m, g)[1], [dmn, mem2], [mem_g], [],
                                [((1, D_MODEL), F32)])[0]

    do4 = _heads_split(d_branch["w_fox_out"], B, t)
    dq4, dk4, dv4, dck, dcq = _fox_bwd(fq, fk, fv, fo4, do4, cq, ck, lse, t)
    df8t, dbias = _fox_c_bwd(dck.reshape(B, N_HEADS, S) + dcq.reshape(B, N_HEADS, S), f8t, bias_col)
    gw["fox_f_bias"] = jnp.sum(dbias, axis=0).reshape(1, N_HEADS)
    dqkv = jnp.concatenate([_heads_merge(d, B) for d in (dq4, dk4, dv4)], axis=1).astype(BF16)
    df_pad = jnp.pad(df8t.transpose(0, 2, 1).reshape(T, N_HEADS), ((0, 0), (0, LANES - N_HEADS))).astype(BF16)

    dry, drg, dr_h, dk2_h, dv_h, gw["rwkv_gn_g"], gw["rwkv_gn_b"], gw["rwkv_r_k"] = _rw_head_bwd(
        d_branch["w_rwkv_out"], ry, rr, rk2, rv, rg, head_consts)
    dv_s, dk2_s, db_s, dw_s, dr_s, da_s = _rw_scan_bwd(*scan_cols, rv, rsa, dry, sb)
    dP, dPp, gw["rwkv_mu"], gw["rwkv_w0"], d_w1, gw["rwkv_a0"], d_w2, gw["rwkv_g_up"], gw["rwkv_k_k"], gw["rwkv_k_a"] = \
        _rw_prep_bwd(p_rw, p_prev, [dr_s, dr_h, dw_s, dk2_s, dk2_h, dv_s, dv_h, da_s, db_s, drg], prep_consts)
    gw["rwkv_w_up"], gw["rwkv_a_up"] = d_w1[:64], d_w2[64:]
    dp_rw = (dP + _shift_next(dPp, B)).astype(BF16)

    du = _mm("d_u_qkv", dqkv, w_qkv, tb=True)
    du = _mm("d_u_f", df_pad, w_f, tb=True, add=du)
    du = _mm("d_u_rwkv", dp_rw, w_rw, tb=True, add=du)
    du = _mm("d_u_memq", dmemq, w_mq, tb=True, add=du)
    du = _mm("d_u_gate", dgate, w_gate, tb=True, add=du)
    gw["w_in"] = jnp.concatenate(
        [_mm("g_w_qkv", u, dqkv, ta=True, out_dtype=BF16), _mm("g_w_f", u, df_pad, ta=True, out_dtype=BF16)[:, :N_HEADS], _mm("g_w_rwkv", u, dp_rw, ta=True, out_dtype=BF16),
         _mm("g_w_memq", u, dmemq, ta=True, out_dtype=BF16), _mm("g_w_gate", u, dgate, ta=True, out_dtype=BF16)], axis=1)

    def pre1_bwd(du, dh1, x, g):
        dx, dg = _rms_bwd(du, x, g)
        return dh1 + dx, dg
    dx, gw["pre1_g"] = _rowwise("rms_pre1_bwd", pre1_bwd, [du, dh1, x2], [pre1_g], [(D_MODEL, F32)], [((1, D_MODEL), F32)])
    return loss[0, 0], dx.reshape(B, S, D_MODEL), gw


COL_SHARDED = ("w_in", "rwkv_w_up", "rwkv_a_up", "rwkv_g_up", "w_fox_out", "w_rwkv_out", "w_mem_out", "w_ffn_gate", "w_ffn_up")
ROW_SHARDED = ("w_mem_kv", "w_o", "w_ffn_down")
BIG = ("w_in", "rwkv_w_up", "rwkv_a_up", "rwkv_g_up", "w_mem_kv", "w_fox_out", "w_rwkv_out", "w_mem_out", "w_o",
       "w_ffn_gate", "w_ffn_up", "w_ffn_down")
SMALL = ("pre1_g", "post1_g", "pre2_g", "post2_g", "mem_norm_g", "fox_f_bias", "rwkv_mu", "rwkv_w0", "rwkv_a0", "rwkv_k_k",
         "rwkv_k_a", "rwkv_r_k", "rwkv_gn_g", "rwkv_gn_b")
WEIGHTS = ("pre1_g", "post1_g", "pre2_g", "post2_g", "mem_norm_g", "w_in", "fox_f_bias", "rwkv_mu", "rwkv_w0", "rwkv_w_up",
           "rwkv_a0", "rwkv_a_up", "rwkv_g_up", "rwkv_k_k", "rwkv_k_a", "rwkv_r_k", "rwkv_gn_g", "rwkv_gn_b", "w_mem_kv",
           "w_fox_out", "w_rwkv_out", "w_mem_out", "w_o", "w_ffn_gate", "w_ffn_up", "w_ffn_down")
HALF_ROW_ALIGN = 1024 * LANES


def _packed_half_len(shard_shapes):
    n = sum(int(np.prod(s)) for s in shard_shapes)
    return -(-n // (2 * HALF_ROW_ALIGN)) * HALF_ROW_ALIGN


def _pack(pieces, half_len, dtype):
    flat = jnp.concatenate([p.reshape(-1).astype(dtype) for p in pieces])
    flat = jnp.pad(flat, (0, 2 * half_len - flat.shape[0]))
    return flat.reshape(2, half_len // LANES, LANES)


def _unpack(packed, shapes):
    flat = packed.reshape(-1)
    out, off = [], 0
    for s in shapes:
        n = int(np.prod(s))
        out.append(flat[off:off + n].reshape(s))
        off += n
    return out


def _to_chip_shards(name, g):
    R, C = g.shape
    if name in COL_SHARDED:
        return g.reshape(R, N_CHIPS, C // N_CHIPS).transpose(1, 0, 2)
    return g.reshape(N_CHIPS, R // N_CHIPS, C)


def _from_chip_shards(name, s):
    _, r, c = s.shape
    if name in COL_SHARDED:
        return s.transpose(1, 0, 2).reshape(r, N_CHIPS * c)
    return s.reshape(N_CHIPS * r, c)


def _my_place():
    return lax.axis_index("x"), lax.axis_index("y"), lax.axis_index("c")


def _other_chips(x, y):
    return [(1 - x, y), (x, 1 - y), (1 - x, 1 - y)]


ANY = pl.BlockSpec(memory_space=pl.ANY)


def _gather_weights(packed):
    _, R, L = packed.shape

    def body(in_ref, out_ref, send_sems, recv_sems, local_sem):
        x, y, c = _my_place()
        chip = lambda px, py: 2 * px + py
        sibling = (x, y, 1 - c)
        others = _other_chips(x, y)

        def copy(k, src, dst, to):
            return pltpu.make_async_remote_copy(src_ref=src, dst_ref=dst, send_sem=send_sems.at[k], recv_sem=recv_sems.at[k],
                                                device_id=to, device_id_type=MESH)

        mine = pltpu.make_async_copy(in_ref, out_ref.at[chip(x, y)], local_sem)
        mine.start()
        sends = [copy(j, in_ref.at[c], out_ref.at[chip(x, y), c], (px, py, c)) for j, (px, py) in enumerate(others)]
        for cp in sends:
            cp.start()
        passed = [copy(3 + j, out_ref.at[chip(px, py), c], out_ref.at[chip(px, py), c], sibling)
                  for j, (px, py) in enumerate(others)]
        for j, (px, py) in enumerate(others):
            copy(j, in_ref.at[c], out_ref.at[chip(px, py), c], (px, py, c)).wait_recv()
            passed[j].start()
        for j, (px, py) in enumerate(others):
            copy(3 + j, in_ref.at[1 - c], out_ref.at[chip(px, py), 1 - c], sibling).wait_recv()
        for cp in sends + passed:
            cp.wait_send()
        mine.wait()

    return pl.pallas_call(
        body, out_shape=jax.ShapeDtypeStruct((N_CHIPS, 2, R, L), packed.dtype), in_specs=[ANY], out_specs=ANY,
        scratch_shapes=[pltpu.SemaphoreType.DMA((6,)), pltpu.SemaphoreType.DMA((6,)), pltpu.SemaphoreType.DMA],
        name="gather_weights",
    )(packed)


def _scatter_grads(parts):
    _, _, R, L = parts.shape

    def body(in_ref, out_ref, send_sems, recv_sems, local_sem):
        x, y, c = _my_place()
        me = 4 * x + 2 * y + c
        peers = [(px, py, pc) for px in range(2) for py in range(2) for pc in range(2)]

        def copy(k, px, py, pc):
            return pltpu.make_async_remote_copy(src_ref=in_ref.at[2 * px + py, pc], dst_ref=out_ref.at[me],
                                                send_sem=send_sems.at[k], recv_sem=recv_sems.at[k],
                                                device_id=(px, py, pc), device_id_type=MESH)

        mine = pltpu.make_async_copy(in_ref.at[2 * x + y, c], out_ref.at[me], local_sem)
        mine.start()
        started = []
        for k in range(1, N_DEV):
            px, py, pc = x ^ (k >> 2), y ^ ((k >> 1) & 1), c ^ (k & 1)
            cp = copy(k - 1, px, py, pc)
            cp.start()
            started.append(cp)
        for cp in started:
            cp.wait()
        mine.wait()

    return pl.pallas_call(
        body, out_shape=jax.ShapeDtypeStruct((N_DEV, R, L), parts.dtype), in_specs=[ANY], out_specs=ANY,
        scratch_shapes=[pltpu.SemaphoreType.DMA((7,)), pltpu.SemaphoreType.DMA((7,)), pltpu.SemaphoreType.DMA],
        name="scatter_grads",
    )(parts)


def _sum_devices(parts, name):
    n, R, L = parts.shape
    tr = _tile_rows(R)

    def body(p_ref, o_ref):
        acc = p_ref[0].astype(F32)
        for i in range(1, n):
            acc = acc + p_ref[i].astype(F32)
        o_ref[...] = acc

    return pl.pallas_call(
        body, grid=(R // tr,), in_specs=[pl.BlockSpec((n, tr, L), lambda i: (0, i, 0))],
        out_specs=pl.BlockSpec((tr, L), lambda i: (i, 0)), out_shape=jax.ShapeDtypeStruct((R, L), F32), name=name,
        compiler_params=_params(("parallel",)),
    )(parts)


def _tile_rows(R, cap=2048):
    best = 8
    for t in range(8, min(R, cap) + 1, 8):
        if R % t == 0:
            best = t
    return best if R % 8 == 0 else R


def _swap_halves(half):
    R, L = half.shape

    def body(in_ref, out_ref, send_sem, recv_sem, local_sem):
        x, y, c = _my_place()
        mine = pltpu.make_async_copy(in_ref, out_ref.at[c], local_sem)
        mine.start()
        cp = pltpu.make_async_remote_copy(src_ref=in_ref, dst_ref=out_ref.at[c], send_sem=send_sem, recv_sem=recv_sem,
                                          device_id=(x, y, 1 - c), device_id_type=MESH)
        cp.start()
        cp.wait()
        mine.wait()

    return pl.pallas_call(
        body, out_shape=jax.ShapeDtypeStruct((2, R, L), F32), in_specs=[ANY], out_specs=ANY,
        scratch_shapes=[pltpu.SemaphoreType.DMA, pltpu.SemaphoreType.DMA, pltpu.SemaphoreType.DMA], name="swap_halves",
    )(half)


def _allreduce_small(v):
    R, L = v.shape

    def body(in_ref, out_ref, buf, send_sems, recv_sems):
        x, y, c = _my_place()
        me = 4 * x + 2 * y + c
        buf[me] = in_ref[...]
        started = []
        for k in range(1, N_DEV):
            to = (x ^ (k >> 2), y ^ ((k >> 1) & 1), c ^ (k & 1))
            cp = pltpu.make_async_remote_copy(src_ref=in_ref, dst_ref=buf.at[me], send_sem=send_sems.at[k - 1],
                                              recv_sem=recv_sems.at[k - 1], device_id=to, device_id_type=MESH)
            cp.start()
            started.append(cp)
        for cp in started:
            cp.wait()
        acc = buf[0]
        for i in range(1, N_DEV):
            acc = acc + buf[i]
        out_ref[...] = acc

    vm = pl.BlockSpec(memory_space=pltpu.VMEM)
    return pl.pallas_call(
        body, out_shape=jax.ShapeDtypeStruct((R, L), F32), in_specs=[vm], out_specs=vm,
        scratch_shapes=[pltpu.VMEM((N_DEV, R, L), F32), pltpu.SemaphoreType.DMA((7,)), pltpu.SemaphoreType.DMA((7,))],
        name="allreduce_small",
    )(v)


def _adamw(name, w, g, m, v):
    shape = w.shape
    C = shape[-1]
    R = int(np.prod(shape[:-1]))
    args = [a.reshape(R, C).astype(F32) for a in (w, g, m, v)]
    tr = _tile_rows(R, 256)

    def body(w_ref, g_ref, m_ref, v_ref, d_ref, nm_ref, nv_ref):
        g = g_ref[...]
        m = ADAM_B1 * m_ref[...] + (1.0 - ADAM_B1) * g
        v = ADAM_B2 * v_ref[...] + (1.0 - ADAM_B2) * (g * g)
        m_hat = m / (1.0 - ADAM_B1 ** ADAM_STEP)
        v_hat = v / (1.0 - ADAM_B2 ** ADAM_STEP)
        d_ref[...] = -ADAM_LR * (m_hat / (jnp.sqrt(v_hat) + ADAM_EPS) + ADAM_WD * w_ref[...])
        nm_ref[...] = m
        nv_ref[...] = v

    spec = pl.BlockSpec((tr, C), lambda i: (i, 0))
    out = pl.pallas_call(
        body, grid=(R // tr,), in_specs=[spec] * 4, out_specs=[spec] * 3,
        out_shape=[jax.ShapeDtypeStruct((R, C), F32)] * 3, name="adamw_" + name, compiler_params=_params(("parallel",)),
    )(*args)
    return [o.reshape(shape) for o in out]


def kernel(x, mem, pre1_g, post1_g, pre2_g, post2_g, mem_norm_g, w_in, fox_f_bias, rwkv_mu, rwkv_w0, rwkv_w_up, rwkv_a0, rwkv_a_up, rwkv_g_up, rwkv_k_k, rwkv_k_a, rwkv_r_k, rwkv_gn_g, rwkv_gn_b, w_mem_kv, w_fox_out, w_rwkv_out, w_mem_out, w_o, w_ffn_gate, w_ffn_up, w_ffn_down, loss_target, m_pre1_g, m_post1_g, m_pre2_g, m_post2_g, m_mem_norm_g, m_w_in, m_fox_f_bias, m_rwkv_mu, m_rwkv_w0, m_rwkv_w_up, m_rwkv_a0, m_rwkv_a_up, m_rwkv_g_up, m_rwkv_k_k, m_rwkv_k_a, m_rwkv_r_k, m_rwkv_gn_g, m_rwkv_gn_b, m_w_mem_kv, m_w_fox_out, m_w_rwkv_out, m_w_mem_out, m_w_o, m_w_ffn_gate, m_w_ffn_up, m_w_ffn_down, v_pre1_g, v_post1_g, v_pre2_g, v_post2_g, v_mem_norm_g, v_w_in, v_fox_f_bias, v_rwkv_mu, v_rwkv_w0, v_rwkv_w_up, v_rwkv_a0, v_rwkv_a_up, v_rwkv_g_up, v_rwkv_k_k, v_rwkv_k_a, v_rwkv_r_k, v_rwkv_gn_g, v_rwkv_gn_b, v_w_mem_kv, v_w_fox_out, v_w_rwkv_out, v_w_mem_out, v_w_o, v_w_ffn_gate, v_w_ffn_up, v_w_ffn_down):
    given = dict(locals())
    w_loc = {n: given[n] for n in WEIGHTS}
    m_loc = {n: given["m_" + n] for n in WEIGHTS}
    v_loc = {n: given["v_" + n] for n in WEIGHTS}

    shard_shapes = [w_loc[n].shape[1:] for n in BIG]
    half_len = _packed_half_len(shard_shapes)
    gathered = _gather_weights(_pack([w_loc[n][0] for n in BIG], half_len, BF16)).reshape(N_CHIPS, -1)
    W, off = {}, 0
    for n, s in zip(BIG, shard_shapes):
        cnt = int(np.prod(s))
        W[n] = _from_chip_shards(n, gathered[:, off:off + cnt].reshape((N_CHIPS,) + tuple(s)))
        off += cnt
    W.update({n: w_loc[n][0] for n in SMALL})

    loss, grad_x, gw = _layer_step(x, mem, loss_target, W)

    parts = jnp.concatenate([_to_chip_shards(n, gw[n].astype(BF16)).reshape(N_CHIPS, -1) for n in BIG], axis=1)
    parts = jnp.pad(parts, ((0, 0), (0, 2 * half_len - parts.shape[1]))).reshape(N_CHIPS, 2, half_len // LANES, LANES)
    half = _sum_devices(_scatter_grads(parts), "sum_grads")
    g_shard = dict(zip(BIG, _unpack(_swap_halves(half), shard_shapes)))

    small_shapes = [w_loc[n].shape[1:] for n in SMALL] + [(1,)]
    n_small = sum(int(np.prod(s)) for s in small_shapes)
    small_rows = -(-n_small // (8 * LANES)) * 8
    flat = jnp.concatenate([gw[n].reshape(-1) for n in SMALL] + [loss.reshape(1)])
    flat = jnp.pad(flat, (0, small_rows * LANES - n_small)).reshape(small_rows, LANES)
    small = _unpack(_allreduce_small(flat), small_shapes)
    g_small = dict(zip(SMALL, small[:-1]))
    loss = small[-1][0]

    grads, deltas, new_m, new_v = [], [], [], []
    for n in WEIGHTS:
        g = (g_shard[n] if n in g_shard else g_small[n]).reshape(w_loc[n].shape)
        d, nm, nv = _adamw(n, w_loc[n], g, m_loc[n], v_loc[n])
        grads.append(g)
        deltas.append(d)
        new_m.append(nm)
        new_v.append(nv)
    return (loss, grad_x, *grads, *deltas, *new_m, *new_v)
```

```python
import functools
import math

import numpy as np
import jax
import jax.numpy as jnp
from jax import lax
from jax.experimental import pallas as pl
from jax.experimental.pallas import tpu as pltpu

F32, BF16 = jnp.float32, jnp.bfloat16
MESH = pl.DeviceIdType.MESH

D_MODEL = 1024
HEAD_DIM = 64
N_HEADS = 8
BR_W = 512
MEM_HEADS = 4
MEM_HEAD_DIM = 128
D_FF = 2816
NORM_EPS = 1e-6
GN_EPS = 64e-5
N_CHIPS = 4
N_DEV = 8
LANES = 128
VMEM_LIMIT = 48 * 1024 * 1024

ADAM_LR, ADAM_B1, ADAM_B2, ADAM_EPS, ADAM_WD, ADAM_STEP = 0.001, 0.9, 0.999, 1e-08, 0.01, 10

FOX_COLS = 3 * BR_W + N_HEADS
RWKV_COLS = 3 * BR_W + 64 + 64 + 128
COL_QKV = (0, 3 * BR_W)
COL_F = (3 * BR_W, FOX_COLS)
COL_RW = (FOX_COLS, FOX_COLS + RWKV_COLS)
COL_MQ = (COL_RW[1], COL_RW[1] + BR_W)
COL_GATE = (COL_MQ[1], COL_MQ[1] + 3 * D_MODEL)

NT_DIMS = (((1,), (1,)), ((), ()))
TN_DIMS = (((0,), (0,)), ((), ()))


def _params(sem=None, **kw):
    return pltpu.CompilerParams(dimension_semantics=sem, vmem_limit_bytes=VMEM_LIMIT, **kw)


def _sigmoid(x):
    return 1.0 / (1.0 + jnp.exp(-x))


def _log_sigmoid(x):
    return jnp.minimum(x, 0.0) - jnp.log(1.0 + jnp.exp(-jnp.abs(x)))


def _bdot(a, b, dims=None):
    a, b = a.astype(BF16), b.astype(BF16)
    if dims is None:
        return jnp.dot(a, b, preferred_element_type=F32)
    return lax.dot_general(a, b, dims, preferred_element_type=F32)


def _hdot(a, b):
    return jnp.dot(a, b, precision=lax.Precision.HIGHEST, preferred_element_type=F32)


def _tile(n, cap):
    best = None
    for t in range(LANES, min(n, cap) + 1, LANES):
        if n % t == 0:
            best = t
    return best or n


def _rowwise(name, fn, rows, consts, outs, accs=(), tm=256):
    T = rows[0].shape[0]
    tm = min(tm, T)
    assert T % tm == 0
    nr, nc, no, na = len(rows), len(consts), len(outs), len(accs)

    def body(*refs):
        res = fn(*[r[...] for r in refs[:nr + nc]])
        if not isinstance(res, (tuple, list)):
            res = (res,)
        orefs, arefs = refs[nr + nc:nr + nc + no], refs[nr + nc + no:]
        for ref, val in zip(orefs, res[:no]):
            ref[...] = val.astype(ref.dtype)
        if na:
            @pl.when(pl.program_id(0) == 0)
            def _():
                for ref in arefs:
                    ref[...] = jnp.zeros(ref.shape, ref.dtype)
            for ref, val in zip(arefs, res[no:]):
                ref[...] += val

    in_specs = ([pl.BlockSpec((tm, r.shape[1]), lambda i: (i, 0)) for r in rows]
                + [pl.BlockSpec(c.shape, lambda i: (0, 0)) for c in consts])
    out_specs = ([pl.BlockSpec((tm, w), lambda i: (i, 0)) for w, _ in outs]
                 + [pl.BlockSpec(s, lambda i: (0, 0)) for s, _ in accs])
    out_shape = ([jax.ShapeDtypeStruct((T, w), dt) for w, dt in outs]
                 + [jax.ShapeDtypeStruct(s, dt) for s, dt in accs])
    return pl.pallas_call(
        body, grid=(T // tm,), in_specs=in_specs, out_specs=out_specs, out_shape=out_shape, name=name,
        compiler_params=_params(("arbitrary",) if na else ("parallel",)),
    )(*rows, *consts)


MM_TILE_CAP = 1408
MM_WHOLE_K = 2048


def _mm(name, a, b, ta=False, tb=False, out_dtype=F32, add=None):
    M, K = (a.shape[1], a.shape[0]) if ta else a.shape
    K2, N = (b.shape[1], b.shape[0]) if tb else b.shape
    assert K == K2
    tm, tn = _tile(M, MM_TILE_CAP), _tile(N, MM_TILE_CAP)
    tk = K if K <= MM_WHOLE_K else _tile(K, MM_TILE_CAP)
    assert M % tm == 0 and N % tn == 0 and K % tk == 0
    nk = K // tk
    a_dim, b_dim = (0 if ta else 1), (1 if tb else 0)

    def body(*refs):
        a_ref, b_ref = refs[0], refs[1]
        n_in = 2 if add is None else 3
        o_ref, acc = refs[n_in], (refs[n_in + 1] if nk > 1 else None)
        k = pl.program_id(2)
        part = lax.dot_general(a_ref[...].astype(BF16), b_ref[...].astype(BF16),
                               (((a_dim,), (b_dim,)), ((), ())), preferred_element_type=F32)

        def finish(r):
            if add is not None:
                r = r + refs[2][...].astype(F32)
            o_ref[...] = r.astype(o_ref.dtype)

        if nk == 1:
            finish(part)
            return

        @pl.when(k == 0)
        def _():
            acc[...] = part

        @pl.when(k > 0)
        def _():
            acc[...] += part

        @pl.when(k == nk - 1)
        def _():
            finish(acc[...])

    a_spec = pl.BlockSpec((tk, tm), lambda i, j, k: (k, i)) if ta else pl.BlockSpec((tm, tk), lambda i, j, k: (i, k))
    b_spec = pl.BlockSpec((tn, tk), lambda i, j, k: (j, k)) if tb else pl.BlockSpec((tk, tn), lambda i, j, k: (k, j))
    o_spec = pl.BlockSpec((tm, tn), lambda i, j, k: (i, j))
    ins, in_specs = [a, b], [a_spec, b_spec]
    if add is not None:
        ins.append(add)
        in_specs.append(o_spec)
    return pl.pallas_call(
        body, grid=(M // tm, N // tn, nk), in_specs=in_specs, out_specs=o_spec,
        out_shape=jax.ShapeDtypeStruct((M, N), out_dtype), scratch_shapes=[pltpu.VMEM((tm, tn), F32)] if nk > 1 else [],
        name=name, compiler_params=_params(("parallel", "parallel", "arbitrary")),
    )(*ins)


def _rowsum(x):
    return jnp.sum(x, axis=0, keepdims=True)


def _rms_stat(x):
    return lax.rsqrt(jnp.mean(x * x, axis=-1, keepdims=True) + NORM_EPS)


def _rms_bwd(dy, x, g):
    r = _rms_stat(x)
    xn = x * r
    dxn = dy * g
    dx = r * (dxn - xn * jnp.mean(dxn * xn, axis=-1, keepdims=True))
    return dx, _rowsum(dy * xn)


def _fox_c_fwd(f8t, bias_col, tc=256):
    B, H, S = f8t.shape
    tc = min(tc, S)

    def body(f_ref, b_ref, c_ref, carry):
        @pl.when(pl.program_id(1) == 0)
        def _():
            carry[...] = jnp.zeros(carry.shape, F32)
        lf = _log_sigmoid(f_ref[...] + b_ref[...])
        row = lax.broadcasted_iota(jnp.int32, (tc, tc), 0)
        col = lax.broadcasted_iota(jnp.int32, (tc, tc), 1)
        c = _hdot(lf, (row <= col).astype(F32)) + carry[...]
        c_ref[...] = c
        carry[...] = c[:, tc - 1:tc]

    return pl.pallas_call(
        body, grid=(B, S // tc),
        in_specs=[pl.BlockSpec((None, H, tc), lambda b, i: (b, 0, i)), pl.BlockSpec((H, 1), lambda b, i: (0, 0))],
        out_specs=pl.BlockSpec((None, H, tc), lambda b, i: (b, 0, i)),
        out_shape=jax.ShapeDtypeStruct((B, H, S), F32), scratch_shapes=[pltpu.VMEM((H, 1), F32)], name="fox_c_fwd",
        compiler_params=_params(("parallel", "arbitrary")),
    )(f8t, bias_col)


def _fox_c_bwd(dc, f8t, bias_col, tc=256):
    B, H, S = f8t.shape
    tc = min(tc, S)
    n = S // tc

    def body(dc_ref, f_ref, b_ref, df_ref, db_ref, carry):
        @pl.when(pl.program_id(1) == 0)
        def _():
            carry[...] = jnp.zeros(carry.shape, F32)
            db_ref[...] = jnp.zeros(db_ref.shape, F32)
        row = lax.broadcasted_iota(jnp.int32, (tc, tc), 0)
        col = lax.broadcasted_iota(jnp.int32, (tc, tc), 1)
        dlf = _hdot(dc_ref[...], (row >= col).astype(F32)) + carry[...]
        z = f_ref[...] + b_ref[...]
        df = dlf * (1.0 - _sigmoid(z))
        df_ref[...] = df
        db_ref[...] += jnp.sum(df, axis=1, keepdims=True)
        carry[...] = dlf[:, 0:1]

    rev = lambda b, i: (b, 0, n - 1 - i)
    return pl.pallas_call(
        body, grid=(B, n),
        in_specs=[pl.BlockSpec((None, H, tc), rev), pl.BlockSpec((None, H, tc), rev), pl.BlockSpec((H, 1), lambda b, i: (0, 0))],
        out_specs=[pl.BlockSpec((None, H, tc), rev), pl.BlockSpec((None, H, 1), lambda b, i: (b, 0, 0))],
        out_shape=[jax.ShapeDtypeStruct((B, H, S), F32), jax.ShapeDtypeStruct((B, H, 1), F32)],
        scratch_shapes=[pltpu.VMEM((H, 1), F32)], name="fox_c_bwd",
        compiler_params=_params(("parallel", "arbitrary")),
    )(dc, f8t, bias_col)


NEG_BIG = -1e30


def _fox_logits(q, kj, cq, ckj, i, j, t, scale):
    s = _bdot(q, kj, NT_DIMS) * scale + (cq - ckj)
    row = lax.broadcasted_iota(jnp.int32, (t, t), 0)
    col = lax.broadcasted_iota(jnp.int32, (t, t), 1)
    return s, col <= row + (i - j) * t


def _fox_fwd(q, k, v, cq, ck, t):
    G, n = q.shape[0], q.shape[1]
    scale = HEAD_DIM ** -0.5

    def body(q_ref, k_ref, v_ref, cq_ref, ck_ref, o_ref, lse_ref):
        i = pl.program_id(1)
        qi, cqi = q_ref[...], cq_ref[...]

        def step(j, carry):
            m, l, acc = carry
            s, ok = _fox_logits(qi, k_ref[j], cqi, ck_ref[j], i, j, t, scale)
            s = jnp.where(ok, s, NEG_BIG)
            m2 = jnp.maximum(m, jnp.max(s, axis=1, keepdims=True))
            p = jnp.exp(s - m2)
            al = jnp.exp(m - m2)
            return m2, al * l + jnp.sum(p, axis=1, keepdims=True), al * acc + _bdot(p, v_ref[j])

        m, l, acc = lax.fori_loop(0, i + 1, step, (jnp.full((t, 1), NEG_BIG, F32), jnp.zeros((t, 1), F32),
                                                   jnp.zeros((t, HEAD_DIM), F32)))
        o_ref[...] = (acc / l).astype(o_ref.dtype)
        lse_ref[...] = m + jnp.log(l)

    blk = pl.BlockSpec((None, None, t, HEAD_DIM), lambda g, i: (g, i, 0, 0))
    full = pl.BlockSpec((None, n, t, HEAD_DIM), lambda g, i: (g, 0, 0, 0))
    col = pl.BlockSpec((None, None, t, 1), lambda g, i: (g, i, 0, 0))
    return pl.pallas_call(
        body, grid=(G, n), in_specs=[blk, full, full, col, pl.BlockSpec((None, n, 1, t), lambda g, i: (g, 0, 0, 0))],
        out_specs=[blk, col],
        out_shape=[jax.ShapeDtypeStruct(q.shape, BF16), jax.ShapeDtypeStruct((G, n, t, 1), F32)], name="fox_fwd",
        compiler_params=_params(("parallel", "parallel")),
    )(q, k, v, cq, ck)


def _fox_bwd(q, k, v, o, do, cq, ck, lse, t):
    G, n = q.shape[0], q.shape[1]
    scale = HEAD_DIM ** -0.5

    def body(q_ref, k_ref, v_ref, o_ref, do_ref, cq_ref, ck_ref, lse_ref, dq_ref, dk_ref, dv_ref, dck_ref, dcq_ref):
        dk_ref[...] = jnp.zeros(dk_ref.shape, F32)
        dv_ref[...] = jnp.zeros(dv_ref.shape, F32)
        dck_ref[...] = jnp.zeros(dck_ref.shape, F32)

        def qloop(i, _):
            qi, doi, cqi, lsei = q_ref[i], do_ref[i], cq_ref[i], lse_ref[i]
            delta = jnp.sum(doi.astype(F32) * o_ref[i].astype(F32), axis=1, keepdims=True)

            def kloop(j, carry):
                dq, dcq = carry
                kj, vj = k_ref[j], v_ref[j]
                s, ok = _fox_logits(qi, kj, cqi, ck_ref[j], i, j, t, scale)
                p = jnp.where(ok, jnp.exp(s - lsei), 0.0)
                ds = p * (_bdot(doi, vj, NT_DIMS) - delta)
                dv_ref[j] += _bdot(p, doi, TN_DIMS)
                dk_ref[j] += _bdot(ds, qi, TN_DIMS) * scale
                dck_ref[j] += -_rowsum(ds)
                return dq + _bdot(ds, kj) * scale, dcq + jnp.sum(ds, axis=1, keepdims=True)

            dq_ref[i], dcq_ref[i] = lax.fori_loop(0, i + 1, kloop, (jnp.zeros((t, HEAD_DIM), F32), jnp.zeros((t, 1), F32)))
            return 0

        lax.fori_loop(0, n, qloop, 0)

    full = pl.BlockSpec((None, n, t, HEAD_DIM), lambda g: (g, 0, 0, 0))
    col = pl.BlockSpec((None, n, t, 1), lambda g: (g, 0, 0, 0))
    row = pl.BlockSpec((None, n, 1, t), lambda g: (g, 0, 0, 0))
    f32s = jax.ShapeDtypeStruct(q.shape, F32)
    return pl.pallas_call(
        body, grid=(G,), in_specs=[full, full, full, full, full, col, row, col], out_specs=[full, full, full, row, col],
        out_shape=[f32s, f32s, f32s, jax.ShapeDtypeStruct((G, n, 1, t), F32), jax.ShapeDtypeStruct((G, n, t, 1), F32)],
        name="fox_bwd", compiler_params=_params(("parallel",)),
    )(q, k, v, o, do, cq, ck, lse)


def _mem_probs(qh, kh):
    s = _bdot(qh, kh, NT_DIMS) * (MEM_HEAD_DIM ** -0.5)
    e = jnp.exp(s - jnp.max(s, axis=1, keepdims=True))
    return e / jnp.sum(e, axis=1, keepdims=True)


def _mem_fwd(q, mem_kv, B, tq=512):
    T = q.shape[0]
    S, Lm = T // B, mem_kv.shape[0] // B
    tq = min(tq, S)
    n = S // tq

    def body(q_ref, k_ref, v_ref, o_ref):
        for h in range(MEM_HEADS):
            sl = slice(h * MEM_HEAD_DIM, (h + 1) * MEM_HEAD_DIM)
            p = _mem_probs(q_ref[:, sl], k_ref[:, sl])
            o_ref[:, sl] = _bdot(p, v_ref[:, sl]).astype(o_ref.dtype)

    qs = pl.BlockSpec((tq, BR_W), lambda b, i: (b * n + i, 0))
    return pl.pallas_call(
        body, grid=(B, n),
        in_specs=[qs, pl.BlockSpec((Lm, BR_W), lambda b, i: (b, 0)), pl.BlockSpec((Lm, BR_W), lambda b, i: (b, 1))],
        out_specs=qs, out_shape=jax.ShapeDtypeStruct((T, BR_W), BF16), name="mem_fwd",
        compiler_params=_params(("parallel", "parallel")),
    )(q, mem_kv, mem_kv)


def _mem_bwd(q, mem_kv, do, B, tq=512):
    T = q.shape[0]
    S, Lm = T // B, mem_kv.shape[0] // B
    tq = min(tq, S)
    n = S // tq
    scale = MEM_HEAD_DIM ** -0.5

    def body(q_ref, k_ref, v_ref, do_ref, dq_ref, dk_ref, dv_ref):
        @pl.when(pl.program_id(1) == 0)
        def _():
            dk_ref[...] = jnp.zeros(dk_ref.shape, F32)
            dv_ref[...] = jnp.zeros(dv_ref.shape, F32)
        for h in range(MEM_HEADS):
            sl = slice(h * MEM_HEAD_DIM, (h + 1) * MEM_HEAD_DIM)
            qh, kh, vh, doh = q_ref[:, sl], k_ref[:, sl], v_ref[:, sl], do_ref[:, sl]
            p = _mem_probs(qh, kh)
            dp = _bdot(doh, vh, NT_DIMS)
            ds = p * (dp - jnp.sum(p * dp, axis=1, keepdims=True))
            dq_ref[:, sl] = (_bdot(ds, kh) * scale).astype(dq_ref.dtype)
            dk_ref[:, sl] += _bdot(ds, qh, TN_DIMS) * scale
            dv_ref[:, sl] += _bdot(p, doh, TN_DIMS)

    qs = pl.BlockSpec((tq, BR_W), lambda b, i: (b * n + i, 0))
    kv = pl.BlockSpec((Lm, BR_W), lambda b, i: (b, 0))
    return pl.pallas_call(
        body, grid=(B, n),
        in_specs=[qs, kv, pl.BlockSpec((Lm, BR_W), lambda b, i: (b, 1)), qs], out_specs=[qs, kv, kv],
        out_shape=[jax.ShapeDtypeStruct((T, BR_W), BF16), jax.ShapeDtypeStruct((B * Lm, BR_W), F32),
                   jax.ShapeDtypeStruct((B * Lm, BR_W), F32)], name="mem_bwd",
        compiler_params=_params(("parallel", "arbitrary")),
    )(q, mem_kv, mem_kv, do)


def _head_ones():
    h = np.arange(BR_W) // HEAD_DIM
    return jnp.asarray((h[:, None] == h[None, :]).astype(np.float32))


def _rw_prep(p, pp, mu, w0, w1, a0, w2, g_up, k_k, k_a, bd):
    ps = p + (pp - p) * mu
    r, k, v = ps[:, 0:512], ps[:, 512:1024], ps[:, 1024:1536]
    wa, gd = ps[:, 1536:1664], ps[:, 1664:1792]
    th = jnp.tanh(wa)
    z = w0 + _bdot(th, w1)
    wl = -jnp.exp(_log_sigmoid(z) - 0.5)
    w = jnp.exp(wl)
    a = _sigmoid(a0 + _bdot(wa, w2))
    sg = _sigmoid(gd)
    g = _bdot(sg, g_up)
    kq = k * k_k
    n2 = _hdot(kq * kq, bd)
    inv = lax.rsqrt(jnp.maximum(n2, 1e-24))
    kk = kq * inv
    k2 = k * (1.0 + (a - 1.0) * k_a)
    return dict(ps=ps, r=r, k=k, v=v, wa=wa, th=th, z=z, wl=wl, w=w, a=a, sg=sg, g=g, kq=kq, n2=n2, inv=inv, kk=kk, k2=k2)


def _rw_prep_fwd(p, pp, consts):
    def fn(p, pp, *c):
        t = _rw_prep(p, pp, *c)
        return t["r"], t["w"], t["k2"], t["v"], -t["kk"], t["kk"] * t["a"], t["g"]
    return _rowwise("rwkv_prep_fwd", fn, [p, pp], consts, [(BR_W, F32)] * 7, tm=256)


def _rw_prep_bwd(p, pp, cots, consts):
    def fn(p, pp, dr1, dr2, dw, dk21, dk22, dv1, dv2, dav, dbv, dg, mu, w0, w1, a0, w2, g_up, k_k, k_a, bd):
        t = _rw_prep(p, pp, mu, w0, w1, a0, w2, g_up, k_k, k_a, bd)
        dr, dk2, dv = dr1 + dr2, dk21 + dk22, dv1 + dv2
        a, k, kk, kq, inv = t["a"], t["k"], t["kk"], t["kq"], t["inv"]
        dkk = dbv * a - dav
        da = dbv * kk + dk2 * k * k_a
        dk = dk2 * (1.0 + (a - 1.0) * k_a)
        d_k_a = _rowsum(dk2 * k * (a - 1.0))
        proj = _hdot(dkk * kq, bd)
        dkq = dkk * inv - jnp.where(t["n2"] > 1e-24, kq * inv * inv * inv * proj, 0.0)
        dk = dk + dkq * k_k
        d_k_k = _rowsum(dkq * k)
        dpa = da * a * (1.0 - a)
        d_a0 = _rowsum(dpa)
        dwa = _bdot(dpa, w2, NT_DIMS)
        d_w2 = _bdot(t["wa"], dpa, TN_DIMS)
        dz = dw * t["w"] * t["wl"] * (1.0 - _sigmoid(t["z"]))
        d_w0 = _rowsum(dz)
        th = t["th"]
        dwa = dwa + _bdot(dz, w1, NT_DIMS) * (1.0 - th * th)
        d_w1 = _bdot(th, dz, TN_DIMS)
        sg = t["sg"]
        dgd = _bdot(dg, g_up, NT_DIMS) * sg * (1.0 - sg)
        d_g_up = _bdot(sg, dg, TN_DIMS)
        dps = jnp.concatenate([dr, dk, dv, dwa, dgd], axis=1)
        d_mu = _rowsum(dps * (pp - p))
        return dps * (1.0 - mu), dps * mu, d_mu, d_w0, d_w1, d_a0, d_w2, d_g_up, d_k_k, d_k_a

    accs = [((1, RWKV_COLS), F32), ((1, BR_W), F32), ((LANES, BR_W), F32), ((1, BR_W), F32), ((LANES, BR_W), F32),
            ((LANES, BR_W), F32), ((1, BR_W), F32), ((1, BR_W), F32)]
    return _rowwise("rwkv_prep_bwd", fn, [p, pp] + list(cots), consts, [(RWKV_COLS, F32)] * 2, accs, tm=128)


def _rw_head(y, r, k2, v, g, gn_g, gn_b, r_k, bd):
    mean = _hdot(y, bd) * (1.0 / HEAD_DIM)
    yc = y - mean
    rs = lax.rsqrt(_hdot(yc * yc, bd) * (1.0 / HEAD_DIM) + GN_EPS)
    yn = yc * rs
    bs = _hdot(r * k2 * r_k, bd)
    return yn, rs, bs, yn * gn_g + gn_b + bs * v


def _rw_head_fwd(y, r, k2, v, g, consts):
    def fn(y, r, k2, v, g, *c):
        return _rw_head(y, r, k2, v, g, *c)[3] * g
    return _rowwise("rwkv_head_fwd", fn, [y, r, k2, v, g], consts, [(BR_W, BF16)])[0]


def _rw_head_bwd(dout, y, r, k2, v, g, consts):
    def fn(dout, y, r, k2, v, g, gn_g, gn_b, r_k, bd):
        dout = dout.astype(F32)
        yn, rs, bs, zz = _rw_head(y, r, k2, v, g, gn_g, gn_b, r_k, bd)
        dg = dout * zz
        dz = dout * g
        dyn = dz * gn_g
        inv_n = 1.0 / HEAD_DIM
        dy = rs * (dyn - _hdot(dyn, bd) * inv_n - yn * (_hdot(dyn * yn, bd) * inv_n))
        dq = _hdot(dz * v, bd)
        return dy, dg, dq * k2 * r_k, dq * r * r_k, dz * bs, _rowsum(dz * yn), _rowsum(dz), _rowsum(dq * r * k2)
    return _rowwise("rwkv_head_bwd", fn, [dout, y, r, k2, v, g], consts, [(BR_W, F32)] * 5, [((1, BR_W), F32)] * 3)


SCAN_TC = 32


def _scan_onehot(Tc):
    w = np.zeros((Tc // 2, 2 * Tc, 2 * LANES), np.float32)
    for tt in range(Tc // 2):
        for u in range(2):
            for h in range(2):
                w[tt, h * Tc + 2 * tt + u, u * LANES + h * HEAD_DIM: u * LANES + (h + 1) * HEAD_DIM] = 1.0
    return jnp.asarray(w, BF16)


def _to_keycols(x, B, Tc):
    S = x.shape[0] // B
    x = x.reshape(B, S // Tc, Tc, 4, 2, HEAD_DIM).transpose(0, 3, 1, 5, 4, 2)
    return x.reshape(B * 4, S // Tc, HEAD_DIM, 2 * Tc)


def _split_bf16(x):
    hi = x.astype(BF16)
    return hi, (x - hi.astype(F32)).astype(BF16)


def _key_tiles(l_w, others, onehot):
    dot = lambda x: jnp.dot(x, onehot, preferred_element_type=F32)
    whi, wmid = l_w
    return [dot(whi) + dot(wmid)] + [dot(o) for o in others]


def _rw_scan_fwd(LW, LA, LB, LK, LR, v, P=2):
    NP, nc, _, Tc2 = LW.shape
    Tc = Tc2 // 2
    S = nc * Tc
    onehot = _scan_onehot(Tc)
    npb = 4 // P

    def body(lw, la, lb, lk, lr, v_ref, oh_ref, y_ref, sa_ref, sb_ref, st):
        @pl.when(pl.program_id(1) == 0)
        def _():
            st[...] = jnp.zeros(st.shape, F32)
        s = [st[p] for p in range(P)]
        cols = [(_split_bf16(lw[p]), [ref[p].astype(BF16) for ref in (la, lb, lk, lr)]) for p in range(P)]
        for tt in range(Tc // 2):
            tiles = [_key_tiles(c[0], c[1], oh_ref[tt]) for c in cols]
            for u in range(2):
                t = 2 * tt + u
                for p in range(P):
                    W, A, Bt, Kt, R = (x[:, u * LANES:(u + 1) * LANES] for x in tiles[p])
                    ls = slice(p * LANES, (p + 1) * LANES)
                    sb_ref[p, t] = s[p]
                    sa = _rowsum(s[p] * A)
                    s[p] = s[p] * W + Bt * sa + Kt * v_ref[t:t + 1, ls]
                    y_ref[t:t + 1, ls] = _rowsum(s[p] * R)
                    sa_ref[t:t + 1, ls] = sa
        for p in range(P):
            st[p] = s[p]

    lspec = pl.BlockSpec((P, None, HEAD_DIM, Tc2), lambda g, c: (g, c, 0, 0))
    rows = pl.BlockSpec((Tc, P * LANES), lambda g, c: ((g // npb) * nc + c, g % npb))
    rowshape = jax.ShapeDtypeStruct(v.shape, F32)
    return pl.pallas_call(
        body, grid=(NP // P, nc), in_specs=[lspec] * 5 + [rows, pl.BlockSpec(onehot.shape, lambda g, c: (0, 0, 0))],
        out_specs=[rows, rows, pl.BlockSpec((P, Tc, HEAD_DIM, LANES), lambda g, c: (g, c, 0, 0))],
        out_shape=[rowshape, rowshape, jax.ShapeDtypeStruct((NP, S, HEAD_DIM, LANES), F32)],
        scratch_shapes=[pltpu.VMEM((P, HEAD_DIM, LANES), F32)], name="rwkv_scan_fwd",
        compiler_params=_params(("parallel", "arbitrary")),
    )(LW, LA, LB, LK, LR, v, onehot)


SCAN_G_ROWS = 16


def _rw_scan_bwd(LW, LA, LB, LK, LR, v, sa, dy, sb, P=4):
    NP, nc, _, Tc2 = LW.shape
    Tc = Tc2 // 2
    onehot = _scan_onehot(Tc)
    npb = 4 // P

    def body(lw, la, lb, lk, lr, v_ref, sa_ref, dy_ref, sb_ref, oh_ref, dv_ref, dk_ref, db_ref, dw_ref, dr_ref, da_ref, dst):
        @pl.when(pl.program_id(1) == 0)
        def _():
            dst[...] = jnp.zeros(dst.shape, F32)
        rid = lax.broadcasted_iota(jnp.int32, (SCAN_G_ROWS, LANES), 0)
        lane = lax.broadcasted_iota(jnp.int32, (SCAN_G_ROWS, LANES), 1)
        own = ((rid % 2) == 0) == (lane < HEAD_DIM)
        m_kb, m_w, m_r, m_a = (own & (rid >= lo) & (rid < hi) for lo, hi in ((0, 4), (4, 6), (6, 8), (8, 10)))
        nt = lambda rows, tile: lax.dot_general(rows.astype(BF16), tile.astype(BF16), NT_DIMS, preferred_element_type=F32)
        ds = [dst[p] for p in range(P)]
        cols = [(_split_bf16(lw[p]), [ref[p].astype(BF16) for ref in (la, lb, lk, lr)]) for p in range(P)]
        for tt in reversed(range(Tc // 2)):
            tiles = [_key_tiles(c[0], c[1], oh_ref[tt]) for c in cols]
            for u in (1, 0):
                t = 2 * tt + u
                for p in range(P):
                    W, A, Bt, Kt, R = (x[:, u * LANES:(u + 1) * LANES] for x in tiles[p])
                    ls = slice(p * LANES, (p + 1) * LANES)
                    vr, sar, dyr = (ref[t:t + 1, ls] for ref in (v_ref, sa_ref, dy_ref))
                    sp = sb_ref[p, t]
                    s_t = sp * W + Bt * sar + Kt * vr
                    d = ds[p] + R * dyr
                    dv_ref[t:t + 1, ls] = _rowsum(d * Kt)
                    dsar = _rowsum(d * Bt)
                    g = nt(jnp.where(m_kb, jnp.where(rid < 2, vr, sar), 0.0), d)
                    g = g + nt(jnp.where(m_w, 1.0, 0.0), d * sp)
                    g = g + nt(jnp.where(m_r, dyr, 0.0), s_t)
                    g = g + nt(jnp.where(m_a, dsar, 0.0), sp)
                    for q, ref in enumerate((dk_ref, db_ref, dw_ref, dr_ref, da_ref)):
                        ref[t:t + 1, ls] = jnp.concatenate([g[2 * q:2 * q + 1], g[2 * q + 1:2 * q + 2]], axis=1)
                    ds[p] = d * W + A * dsar
        for p in range(P):
            dst[p] = ds[p]

    rev = lambda g, c: (g, nc - 1 - c, 0, 0)
    lspec = pl.BlockSpec((P, None, HEAD_DIM, Tc2), rev)
    rows = pl.BlockSpec((Tc, P * LANES), lambda g, c: ((g // npb) * nc + nc - 1 - c, g % npb))
    return pl.pallas_call(
        body, grid=(NP // P, nc),
        in_specs=[lspec] * 5 + [rows] * 3 + [pl.BlockSpec((P, Tc, HEAD_DIM, LANES), rev),
                                             pl.BlockSpec(onehot.shape, lambda g, c: (0, 0, 0))],
        out_specs=[rows] * 6, out_shape=[jax.ShapeDtypeStruct(v.shape, F32)] * 6,
        scratch_shapes=[pltpu.VMEM((P, HEAD_DIM, LANES), F32)], name="rwkv_scan_bwd",
        compiler_params=_params(("parallel", "arbitrary")),
    )(LW, LA, LB, LK, LR, v, sa, dy, sb, onehot)


def _shift_prev(p, B):
    T, W = p.shape
    return jnp.pad(p.reshape(B, T // B, W), ((0, 0), (1, 0), (0, 0)))[:, :-1].reshape(T, W)


def _shift_next(p, B):
    T, W = p.shape
    return jnp.pad(p.reshape(B, T // B, W), ((0, 0), (0, 1), (0, 0)))[:, 1:].reshape(T, W)


def _heads_split(x, B, t):
    T = x.shape[0]
    S = T // B
    return x.reshape(B, S, N_HEADS, HEAD_DIM).transpose(0, 2, 1, 3).reshape(B * N_HEADS, S // t, t, HEAD_DIM)


def _heads_merge(x, B):
    G, n, t, _ = x.shape
    return x.reshape(B, N_HEADS, n * t, HEAD_DIM).transpose(0, 2, 1, 3).reshape(B * n * t, BR_W)


def _layer_step(x, mem, target, W, scan_tc=SCAN_TC, fox_t=256):
    B, S, _ = x.shape
    T = B * S
    x2, tgt2 = x.reshape(T, D_MODEL), target.reshape(T, D_MODEL)
    mem2 = mem.reshape(-1, D_MODEL)
    w_in_t = W["w_in"]
    wt_qkv, wt_rw, wt_mq, wt_gate = (w_in_t[lo:hi] for lo, hi in (COL_QKV, COL_RW, COL_MQ, COL_GATE))
    wt_f = jnp.pad(w_in_t[COL_F[0]:COL_F[1]], ((0, LANES - N_HEADS), (0, 0)))
    row = lambda v: v.reshape(1, -1).astype(F32)
    pre1_g, post1_g, pre2_g, post2_g, mem_g = (row(W[n]) for n in ("pre1_g", "post1_g", "pre2_g", "post2_g", "mem_norm_g"))

    u = _rowwise("rms_pre1", lambda x, g: x * _rms_stat(x) * g, [x2], [pre1_g], [(D_MODEL, BF16)])[0]
    qkv = _mm("proj_qkv", u, wt_qkv, tb=True, out_dtype=BF16)
    f_pad = _mm("proj_f", u, wt_f, tb=True)
    p_rw = _mm("proj_rwkv", u, wt_rw, tb=True)
    memq = _mm("proj_memq", u, wt_mq, tb=True, out_dtype=BF16)
    gate = _mm("proj_gate", u, wt_gate, tb=True)

    t = min(fox_t, S)
    bias_col = W["fox_f_bias"].reshape(N_HEADS, 1).astype(F32)
    f8t = f_pad[:, :N_HEADS].reshape(B, S, N_HEADS).transpose(0, 2, 1)
    c = _fox_c_fwd(f8t, bias_col)
    G = B * N_HEADS
    cq, ck = c.reshape(G, S // t, t, 1), c.reshape(G, S // t, 1, t)
    fq, fk, fv = (_heads_split(qkv[:, i * BR_W:(i + 1) * BR_W], B, t) for i in range(3))
    fo4, lse = _fox_fwd(fq, fk, fv, cq, ck, t)
    fox_out = _heads_merge(fo4, B)

    bd = _head_ones()
    zpad = jnp.zeros((64, BR_W), F32)
    w1 = jnp.concatenate([W["rwkv_w_up"].astype(F32), zpad], axis=0)
    w2 = jnp.concatenate([zpad, W["rwkv_a_up"].astype(F32)], axis=0)
    prep_consts = [row(W["rwkv_mu"]), row(W["rwkv_w0"]), w1, row(W["rwkv_a0"]), w2, W["rwkv_g_up"].astype(F32),
                   row(W["rwkv_k_k"]), row(W["rwkv_k_a"]), bd]
    p_prev = _shift_prev(p_rw, B)
    rr, rw, rk2, rv, rav, rbv, rg = _rw_prep_fwd(p_rw, p_prev, prep_consts)
    scan_cols = [_to_keycols(v, B, scan_tc) for v in (rw, rav, rbv, rk2, rr)]
    ry, rsa, sb = _rw_scan_fwd(*scan_cols, rv)
    head_consts = [row(W["rwkv_gn_g"]), row(W["rwkv_gn_b"]), row(W["rwkv_r_k"]), bd]
    rwkv_out = _rw_head_fwd(ry, rr, rk2, rv, rg, head_consts)

    mn = _rowwise("rms_mem", lambda m, g: m * _rms_stat(m) * g, [mem2], [mem_g], [(D_MODEL, BF16)])[0]
    mem_kv = _mm("proj_memkv", mn, W["w_mem_kv"], out_dtype=BF16)
    mem_out = _mem_fwd(memq, mem_kv, B)

    fo = [_mm("branch_" + n, a, W[n], tb=True)
          for n, a in (("w_fox_out", fox_out), ("w_rwkv_out", rwkv_out), ("w_mem_out", mem_out))]

    def merge(gate, f0, f1, f2):
        return sum(_sigmoid(gate[:, i * D_MODEL:(i + 1) * D_MODEL]) * f for i, f in enumerate((f0, f1, f2)))
    merged = _rowwise("merge", merge, [gate] + fo, [], [(D_MODEL, BF16)])[0]
    y1 = _mm("proj_o", merged, W["w_o"])

    def mid(x, y1, g1, g2):
        h1 = x + y1 * _rms_stat(y1) * g1
        return h1, h1 * _rms_stat(h1) * g2
    h1, u2 = _rowwise("norm_mid", mid, [x2, y1], [post1_g, pre2_g], [(D_MODEL, F32), (D_MODEL, BF16)])
    gt = _mm("ffn_gate", u2, W["w_ffn_gate"], tb=True)
    up = _mm("ffn_up", u2, W["w_ffn_up"], tb=True)
    act = _rowwise("swiglu", lambda gt, up: gt * _sigmoid(gt) * up, [gt, up], [], [(D_FF, BF16)])[0]
    ffn = _mm("ffn_down", act, W["w_ffn_down"])

    def tail(h1, ffn, tgt, g):
        err = h1 + ffn * _rms_stat(ffn) * g - tgt
        dh2 = err * (1.0 / D_MODEL)
        dffn, dg = _rms_bwd(dh2, ffn, g)
        loss = 0.5 * jnp.sum(jnp.sum(err * err, axis=1, keepdims=True) * (1.0 / D_MODEL), axis=0, keepdims=True)
        return dh2, dffn, dg, jnp.broadcast_to(loss, (1, LANES))
    dh2, dffn, d_post2, loss = _rowwise("loss_tail", tail, [h1, ffn, tgt2], [post2_g], [(D_MODEL, F32), (D_MODEL, BF16)],
                                        [((1, D_MODEL), F32), ((1, LANES), F32)])
    gw = {"post2_g": d_post2}
    dact = _mm("d_act", dffn, W["w_ffn_down"], tb=True)
    gw["w_ffn_down"] = _mm("g_ffn_down", act, dffn, ta=True, out_dtype=BF16)

    def swiglu_bwd(dact, gt, up):
        s = _sigmoid(gt)
        return dact * up * s * (1.0 + gt * (1.0 - s)), dact * gt * s
    dgt, dup = _rowwise("swiglu_bwd", swiglu_bwd, [dact, gt, up], [], [(D_FF, BF16)] * 2)
    du2 = _mm("d_u2_gate", dgt, W["w_ffn_gate"])
    du2 = _mm("d_u2_up", dup, W["w_ffn_up"], add=du2)
    gw["w_ffn_gate"] = _mm("g_ffn_gate", dgt, u2, ta=True, out_dtype=BF16)
    gw["w_ffn_up"] = _mm("g_ffn_up", dup, u2, ta=True, out_dtype=BF16)

    def mid_bwd(du2, dh2, h1, y1, g1, g2):
        dh1_n, d_pre2 = _rms_bwd(du2, h1, g2)
        dh1 = dh2 + dh1_n
        dy1, d_post1 = _rms_bwd(dh1, y1, g1)
        return dh1, dy1, d_post1, d_pre2
    dh1, dy1, gw["post1_g"], gw["pre2_g"] = _rowwise(
        "norm_mid_bwd", mid_bwd, [du2, dh2, h1, y1], [post1_g, pre2_g], [(D_MODEL, F32), (D_MODEL, BF16)],
        [((1, D_MODEL), F32)] * 2)
    dmerged = _mm("d_merged", dy1, W["w_o"], tb=True)
    gw["w_o"] = _mm("g_w_o", merged, dy1, ta=True, out_dtype=BF16)

    def merge_bwd(dm, gate, f0, f1, f2):
        s = [_sigmoid(gate[:, i * D_MODEL:(i + 1) * D_MODEL]) for i in range(3)]
        dgate = jnp.concatenate([dm * f * si * (1.0 - si) for f, si in zip((f0, f1, f2), s)], axis=1)
        return dm * s[0], dm * s[1], dm * s[2], dgate
    dfo0, dfo1, dfo2, dgate = _rowwise("merge_bwd", merge_bwd, [dmerged, gate] + fo, [],
                                       [(D_MODEL, BF16)] * 3 + [(3 * D_MODEL, BF16)])
    d_branch = {}
    for n, a, dfo in (("w_fox_out", fox_out, dfo0), ("w_rwkv_out", rwkv_out, dfo1), ("w_mem_out", mem_out, dfo2)):
        d_branch[n] = _mm("d_in_" + n, dfo, W[n], out_dtype=BF16)
        gw[n] = _mm("g_" + n, dfo, a, ta=True, out_dtype=BF16)

    dmemq, dkm, dvm = _mem_bwd(memq, mem_kv, d_branch["w_mem_out"], B)
    dmem_kv = jnp.concatenate([dkm, dvm], axis=1)
    gw["w_mem_kv"] = _mm("g_w_mem_kv", mn, dmem_kv, ta=True, out_dtype=BF16)
    dmn = _mm("d_mn", dmem_kv, W["w_mem_kv"], tb=True)
    gw["mem_norm_g"] = _rowwise("rms_mem_bwd", lambda d, m, g: _rms_bwd(d, m, g)[1], [dmn, mem2], [mem_g], [],
                                [((1, D_MODEL), F32)])[0]

    do4 = _heads_split(d_branch["w_fox_out"], B, t)
    dq4, dk4, dv4, dck, dcq = _fox_bwd(fq, fk, fv, fo4, do4, cq, ck, lse, t)
    df8t, dbias = _fox_c_bwd(dck.reshape(B, N_HEADS, S) + dcq.reshape(B, N_HEADS, S), f8t, bias_col)
    gw["fox_f_bias"] = jnp.sum(dbias, axis=0).reshape(1, N_HEADS)
    dqkv = jnp.concatenate([_heads_merge(d, B) for d in (dq4, dk4, dv4)], axis=1).astype(BF16)
    df_pad = jnp.pad(df8t.transpose(0, 2, 1).reshape(T, N_HEADS), ((0, 0), (0, LANES - N_HEADS))).astype(BF16)

    dry, drg, dr_h, dk2_h, dv_h, gw["rwkv_gn_g"], gw["rwkv_gn_b"], gw["rwkv_r_k"] = _rw_head_bwd(
        d_branch["w_rwkv_out"], ry, rr, rk2, rv, rg, head_consts)
    dv_s, dk2_s, db_s, dw_s, dr_s, da_s = _rw_scan_bwd(*scan_cols, rv, rsa, dry, sb)
    dP, dPp, gw["rwkv_mu"], gw["rwkv_w0"], d_w1, gw["rwkv_a0"], d_w2, gw["rwkv_g_up"], gw["rwkv_k_k"], gw["rwkv_k_a"] = \
        _rw_prep_bwd(p_rw, p_prev, [dr_s, dr_h, dw_s, dk2_s, dk2_h, dv_s, dv_h, da_s, db_s, drg], prep_consts)
    gw["rwkv_w_up"], gw["rwkv_a_up"] = d_w1[:64], d_w2[64:]
    dp_rw = (dP + _shift_next(dPp, B)).astype(BF16)

    du = _mm("d_u_qkv", dqkv, wt_qkv)
    du = _mm("d_u_f", df_pad, wt_f, add=du)
    du = _mm("d_u_rwkv", dp_rw, wt_rw, add=du)
    du = _mm("d_u_memq", dmemq, wt_mq, add=du)
    du = _mm("d_u_gate", dgate, wt_gate, add=du)
    gw["w_in"] = jnp.concatenate(
        [_mm("g_w_qkv", dqkv, u, ta=True, out_dtype=BF16), _mm("g_w_f", df_pad, u, ta=True, out_dtype=BF16)[:N_HEADS],
         _mm("g_w_rwkv", dp_rw, u, ta=True, out_dtype=BF16), _mm("g_w_memq", dmemq, u, ta=True, out_dtype=BF16),
         _mm("g_w_gate", dgate, u, ta=True, out_dtype=BF16)], axis=0)

    def pre1_bwd(du, dh1, x, g):
        dx, dg = _rms_bwd(du, x, g)
        return dh1 + dx, dg
    dx, gw["pre1_g"] = _rowwise("rms_pre1_bwd", pre1_bwd, [du, dh1, x2], [pre1_g], [(D_MODEL, F32)], [((1, D_MODEL), F32)])
    return loss[0, 0], dx.reshape(B, S, D_MODEL), gw


TRANSPOSED = ("w_in", "w_ffn_gate", "w_ffn_up", "w_fox_out", "w_rwkv_out", "w_mem_out")
LORA = ("rwkv_w_up", "rwkv_a_up", "rwkv_g_up")
ROW_SHARDED = ("w_mem_kv", "w_o", "w_ffn_down")
BIG = ("w_in", "w_ffn_gate", "w_ffn_up", "w_mem_kv", "w_o", "w_ffn_down", "w_fox_out", "w_rwkv_out", "w_mem_out") + LORA
SMALL = ("pre1_g", "post1_g", "pre2_g", "post2_g", "mem_norm_g", "fox_f_bias", "rwkv_mu", "rwkv_w0", "rwkv_a0", "rwkv_k_k",
         "rwkv_k_a", "rwkv_r_k", "rwkv_gn_g", "rwkv_gn_b")
WEIGHTS = ("pre1_g", "post1_g", "pre2_g", "post2_g", "mem_norm_g", "w_in", "fox_f_bias", "rwkv_mu", "rwkv_w0", "rwkv_w_up",
           "rwkv_a0", "rwkv_a_up", "rwkv_g_up", "rwkv_k_k", "rwkv_k_a", "rwkv_r_k", "rwkv_gn_g", "rwkv_gn_b", "w_mem_kv",
           "w_fox_out", "w_rwkv_out", "w_mem_out", "w_o", "w_ffn_gate", "w_ffn_up", "w_ffn_down")
WIRE_W = 1024
WIRE_ROW_ALIGN = 16
WIRE_HALF_ALIGN = 240


def _wire_rows(name, shard_shape):
    r, c = shard_shape
    if name in ROW_SHARDED:
        return r
    return -(-c // WIRE_ROW_ALIGN) * WIRE_ROW_ALIGN if r == WIRE_W else (r * c) // WIRE_W


def _to_wire(name, a):
    if name not in ROW_SHARDED:
        a = jnp.swapaxes(a, -1, -2)
    lead, (n, w) = a.shape[:-2], a.shape[-2:]
    if w != WIRE_W:
        return a.reshape(lead + ((n * w) // WIRE_W, WIRE_W))
    return jnp.pad(a, [(0, 0)] * len(lead) + [(0, (-n) % WIRE_ROW_ALIGN), (0, 0)])


def _from_wire(name, a, shard_shape):
    r, c = shard_shape
    if name in ROW_SHARDED:
        return a
    return a[..., :c, :] if r == WIRE_W else a.reshape(a.shape[:-2] + (c, r))


def _wire_layout(shard_shapes):
    layout, off = {}, 0
    for n in BIG:
        rows = _wire_rows(n, shard_shapes[n])
        layout[n] = (off, rows)
        off += rows
    return layout, -(-off // (2 * WIRE_HALF_ALIGN)) * WIRE_HALF_ALIGN


def _wire_pack(blocks, half_rows):
    a = jnp.concatenate(blocks, axis=-2)
    lead = a.shape[:-2]
    a = jnp.pad(a, [(0, 0)] * len(lead) + [(0, 2 * half_rows - a.shape[-2]), (0, 0)])
    return a.reshape(lead + (2, half_rows, WIRE_W))


def _my_place():
    return lax.axis_index("x"), lax.axis_index("y"), lax.axis_index("c")


def _other_chips(x, y):
    return [(1 - x, y), (x, 1 - y), (1 - x, 1 - y)]


ANY = pl.BlockSpec(memory_space=pl.ANY)


def _gather_weights(packed):
    _, R, L = packed.shape

    def body(in_ref, out_ref, send_sems, recv_sems, local_sem):
        x, y, c = _my_place()
        chip = lambda px, py: 2 * px + py
        sibling = (x, y, 1 - c)
        others = _other_chips(x, y)

        def copy(k, src, dst, to):
            return pltpu.make_async_remote_copy(src_ref=src, dst_ref=dst, send_sem=send_sems.at[k], recv_sem=recv_sems.at[k],
                                                device_id=to, device_id_type=MESH)

        mine = pltpu.make_async_copy(in_ref, out_ref.at[chip(x, y)], local_sem)
        mine.start()
        sends = [copy(j, in_ref.at[c], out_ref.at[chip(x, y), c], (px, py, c)) for j, (px, py) in enumerate(others)]
        for cp in sends:
            cp.start()
        passed = [copy(3 + j, out_ref.at[chip(px, py), c], out_ref.at[chip(px, py), c], sibling)
                  for j, (px, py) in enumerate(others)]
        for j, (px, py) in enumerate(others):
            copy(j, in_ref.at[c], out_ref.at[chip(px, py), c], (px, py, c)).wait_recv()
            passed[j].start()
        for j, (px, py) in enumerate(others):
            copy(3 + j, in_ref.at[1 - c], out_ref.at[chip(px, py), 1 - c], sibling).wait_recv()
        for cp in sends + passed:
            cp.wait_send()
        mine.wait()

    return pl.pallas_call(
        body, out_shape=jax.ShapeDtypeStruct((N_CHIPS, 2, R, L), packed.dtype), in_specs=[ANY], out_specs=ANY,
        scratch_shapes=[pltpu.SemaphoreType.DMA((6,)), pltpu.SemaphoreType.DMA((6,)), pltpu.SemaphoreType.DMA],
        name="gather_weights",
    )(packed)


def _scatter_grads(parts):
    _, _, R, L = parts.shape

    def body(in_ref, out_ref, send_sems, recv_sems, local_sem):
        x, y, c = _my_place()
        me = 4 * x + 2 * y + c
        peers = [(px, py, pc) for px in range(2) for py in range(2) for pc in range(2)]

        def copy(k, px, py, pc):
            return pltpu.make_async_remote_copy(src_ref=in_ref.at[2 * px + py, pc], dst_ref=out_ref.at[me],
                                                send_sem=send_sems.at[k], recv_sem=recv_sems.at[k],
                                                device_id=(px, py, pc), device_id_type=MESH)

        mine = pltpu.make_async_copy(in_ref.at[2 * x + y, c], out_ref.at[me], local_sem)
        mine.start()
        started = []
        for k in range(1, N_DEV):
            px, py, pc = x ^ (k >> 2), y ^ ((k >> 1) & 1), c ^ (k & 1)
            cp = copy(k - 1, px, py, pc)
            cp.start()
            started.append(cp)
        for cp in started:
            cp.wait()
        mine.wait()

    return pl.pallas_call(
        body, out_shape=jax.ShapeDtypeStruct((N_DEV, R, L), parts.dtype), in_specs=[ANY], out_specs=ANY,
        scratch_shapes=[pltpu.SemaphoreType.DMA((7,)), pltpu.SemaphoreType.DMA((7,)), pltpu.SemaphoreType.DMA],
        name="scatter_grads",
    )(parts)


def _sum_devices(parts, name):
    n, R, L = parts.shape
    tr = _tile_rows(R, WIRE_HALF_ALIGN)

    def body(p_ref, o_ref):
        acc = p_ref[0].astype(F32)
        for i in range(1, n):
            acc = acc + p_ref[i].astype(F32)
        o_ref[...] = acc

    return pl.pallas_call(
        body, grid=(R // tr,), in_specs=[pl.BlockSpec((n, tr, L), lambda i: (0, i, 0))],
        out_specs=pl.BlockSpec((tr, L), lambda i: (i, 0)), out_shape=jax.ShapeDtypeStruct((R, L), F32), name=name,
        compiler_params=_params(("parallel",)),
    )(parts)


def _tile_rows(R, cap=2048):
    best = 8
    for t in range(8, min(R, cap) + 1, 8):
        if R % t == 0:
            best = t
    return best if R % 8 == 0 else R


def _swap_halves(half):
    R, L = half.shape

    def body(in_ref, out_ref, send_sem, recv_sem, local_sem):
        x, y, c = _my_place()
        mine = pltpu.make_async_copy(in_ref, out_ref.at[c], local_sem)
        mine.start()
        cp = pltpu.make_async_remote_copy(src_ref=in_ref, dst_ref=out_ref.at[c], send_sem=send_sem, recv_sem=recv_sem,
                                          device_id=(x, y, 1 - c), device_id_type=MESH)
        cp.start()
        cp.wait()
        mine.wait()

    return pl.pallas_call(
        body, out_shape=jax.ShapeDtypeStruct((2, R, L), F32), in_specs=[ANY], out_specs=ANY,
        scratch_shapes=[pltpu.SemaphoreType.DMA, pltpu.SemaphoreType.DMA, pltpu.SemaphoreType.DMA], name="swap_halves",
    )(half)


def _allreduce_small(v):
    R, L = v.shape

    def body(in_ref, out_ref, buf, send_sems, recv_sems):
        x, y, c = _my_place()
        me = 4 * x + 2 * y + c
        buf[me] = in_ref[...]
        started = []
        for k in range(1, N_DEV):
            to = (x ^ (k >> 2), y ^ ((k >> 1) & 1), c ^ (k & 1))
            cp = pltpu.make_async_remote_copy(src_ref=in_ref, dst_ref=buf.at[me], send_sem=send_sems.at[k - 1],
                                              recv_sem=recv_sems.at[k - 1], device_id=to, device_id_type=MESH)
            cp.start()
            started.append(cp)
        for cp in started:
            cp.wait()
        acc = buf[0]
        for i in range(1, N_DEV):
            acc = acc + buf[i]
        out_ref[...] = acc

    vm = pl.BlockSpec(memory_space=pltpu.VMEM)
    return pl.pallas_call(
        body, out_shape=jax.ShapeDtypeStruct((R, L), F32), in_specs=[vm], out_specs=vm,
        scratch_shapes=[pltpu.VMEM((N_DEV, R, L), F32), pltpu.SemaphoreType.DMA((7,)), pltpu.SemaphoreType.DMA((7,))],
        name="allreduce_small",
    )(v)


def _adamw(name, w, g, m, v):
    shape = w.shape
    C = shape[-1]
    R = int(np.prod(shape[:-1]))
    args = [a.reshape(R, C).astype(F32) for a in (w, g, m, v)]
    tr = _tile_rows(R, 256)

    def body(w_ref, g_ref, m_ref, v_ref, d_ref, nm_ref, nv_ref):
        g = g_ref[...]
        m = ADAM_B1 * m_ref[...] + (1.0 - ADAM_B1) * g
        v = ADAM_B2 * v_ref[...] + (1.0 - ADAM_B2) * (g * g)
        m_hat = m / (1.0 - ADAM_B1 ** ADAM_STEP)
        v_hat = v / (1.0 - ADAM_B2 ** ADAM_STEP)
        d_ref[...] = -ADAM_LR * (m_hat / (jnp.sqrt(v_hat) + ADAM_EPS) + ADAM_WD * w_ref[...])
        nm_ref[...] = m
        nv_ref[...] = v

    spec = pl.BlockSpec((tr, C), lambda i: (i, 0))
    out = pl.pallas_call(
        body, grid=(R // tr,), in_specs=[spec] * 4, out_specs=[spec] * 3,
        out_shape=[jax.ShapeDtypeStruct((R, C), F32)] * 3, name="adamw_" + name, compiler_params=_params(("parallel",)),
    )(*args)
    return [o.reshape(shape) for o in out]


def kernel(x, mem, pre1_g, post1_g, pre2_g, post2_g, mem_norm_g, w_in, fox_f_bias, rwkv_mu, rwkv_w0, rwkv_w_up, rwkv_a0, rwkv_a_up, rwkv_g_up, rwkv_k_k, rwkv_k_a, rwkv_r_k, rwkv_gn_g, rwkv_gn_b, w_mem_kv, w_fox_out, w_rwkv_out, w_mem_out, w_o, w_ffn_gate, w_ffn_up, w_ffn_down, loss_target, m_pre1_g, m_post1_g, m_pre2_g, m_post2_g, m_mem_norm_g, m_w_in, m_fox_f_bias, m_rwkv_mu, m_rwkv_w0, m_rwkv_w_up, m_rwkv_a0, m_rwkv_a_up, m_rwkv_g_up, m_rwkv_k_k, m_rwkv_k_a, m_rwkv_r_k, m_rwkv_gn_g, m_rwkv_gn_b, m_w_mem_kv, m_w_fox_out, m_w_rwkv_out, m_w_mem_out, m_w_o, m_w_ffn_gate, m_w_ffn_up, m_w_ffn_down, v_pre1_g, v_post1_g, v_pre2_g, v_post2_g, v_mem_norm_g, v_w_in, v_fox_f_bias, v_rwkv_mu, v_rwkv_w0, v_rwkv_w_up, v_rwkv_a0, v_rwkv_a_up, v_rwkv_g_up, v_rwkv_k_k, v_rwkv_k_a, v_rwkv_r_k, v_rwkv_gn_g, v_rwkv_gn_b, v_w_mem_kv, v_w_fox_out, v_w_rwkv_out, v_w_mem_out, v_w_o, v_w_ffn_gate, v_w_ffn_up, v_w_ffn_down):
    given = dict(locals())
    w_loc = {n: given[n] for n in WEIGHTS}
    m_loc = {n: given["m_" + n] for n in WEIGHTS}
    v_loc = {n: given["v_" + n] for n in WEIGHTS}

    shard_shapes = {n: tuple(w_loc[n].shape[1:]) for n in BIG}
    layout, half_rows = _wire_layout(shard_shapes)
    mine = _wire_pack([_to_wire(n, w_loc[n][0].astype(BF16)) for n in BIG], half_rows)
    gathered = _gather_weights(mine).reshape(N_CHIPS, 2 * half_rows, WIRE_W)
    W = {}
    for n in BIG:
        off, rows = layout[n]
        blocks = _from_wire(n, gathered[:, off:off + rows], shard_shapes[n])
        if n in LORA:
            W[n] = blocks.transpose(2, 0, 1).reshape(blocks.shape[2], -1)
        else:
            W[n] = blocks.reshape(-1, blocks.shape[2])
    W.update({n: w_loc[n][0] for n in SMALL})

    loss, grad_x, gw = _layer_step(x, mem, loss_target, W)

    blocks = []
    for n in BIG:
        r, c = shard_shapes[n]
        g = gw[n].astype(BF16)
        if n in LORA:
            g = g.reshape(r, N_CHIPS, c).transpose(1, 0, 2)
        elif n in TRANSPOSED:
            g = jnp.swapaxes(g.reshape(N_CHIPS, c, r), 1, 2)
        else:
            g = g.reshape(N_CHIPS, r, c)
        blocks.append(_to_wire(n, g))
    half = _sum_devices(_scatter_grads(_wire_pack(blocks, half_rows)), "sum_grads")
    reduced = _swap_halves(half).reshape(2 * half_rows, WIRE_W)
    g_shard = {}
    for n in BIG:
        off, rows = layout[n]
        g = _from_wire(n, reduced[off:off + rows], shard_shapes[n])
        g_shard[n] = g if n in ROW_SHARDED else g.T

    small_shapes = [w_loc[n].shape[1:] for n in SMALL] + [(1,)]
    n_small = sum(int(np.prod(s)) for s in small_shapes)
    small_rows = -(-n_small // (8 * LANES)) * 8
    flat = jnp.concatenate([gw[n].reshape(-1) for n in SMALL] + [loss.reshape(1)])
    flat = jnp.pad(flat, (0, small_rows * LANES - n_small)).reshape(small_rows, LANES).reshape(-1)
    small, off = [], 0
    flat = _allreduce_small(flat.reshape(small_rows, LANES)).reshape(-1)
    for s in small_shapes:
        cnt = int(np.prod(s))
        small.append(flat[off:off + cnt].reshape(s))
        off += cnt
    g_small = dict(zip(SMALL, small[:-1]))
    loss = small[-1][0]

    grads, deltas, new_m, new_v = [], [], [], []
    for n in WEIGHTS:
        g = (g_shard[n] if n in g_shard else g_small[n]).reshape(w_loc[n].shape)
        d, nm, nv = _adamw(n, w_loc[n], g, m_loc[n], v_loc[n])
        grads.append(g)
        deltas.append(d)
        new_m.append(nm)
        new_v.append(nv)
    return (loss, grad_x, *grads, *deltas, *new_m, *new_v)
```

```python
import functools
import math

import numpy as np
import jax
import jax.numpy as jnp
from jax import lax
from jax.experimental import pallas as pl
from jax.experimental.pallas import tpu as pltpu

F32, BF16 = jnp.float32, jnp.bfloat16
MESH = pl.DeviceIdType.MESH

D_MODEL = 1024
HEAD_DIM = 64
N_HEADS = 8
BR_W = 512
MEM_HEADS = 4
MEM_HEAD_DIM = 128
D_FF = 2816
NORM_EPS = 1e-6
GN_EPS = 64e-5
N_CHIPS = 4
N_DEV = 8
LANES = 128
VMEM_LIMIT = 48 * 1024 * 1024

ADAM_LR, ADAM_B1, ADAM_B2, ADAM_EPS, ADAM_WD, ADAM_STEP = 0.001, 0.9, 0.999, 1e-08, 0.01, 10

FOX_COLS = 3 * BR_W + N_HEADS
RWKV_COLS = 3 * BR_W + 64 + 64 + 128
COL_QKV = (0, 3 * BR_W)
COL_F = (3 * BR_W, FOX_COLS)
COL_RW = (FOX_COLS, FOX_COLS + RWKV_COLS)
COL_MQ = (COL_RW[1], COL_RW[1] + BR_W)
COL_GATE = (COL_MQ[1], COL_MQ[1] + 3 * D_MODEL)

NT_DIMS = (((1,), (1,)), ((), ()))
TN_DIMS = (((0,), (0,)), ((), ()))


def _params(sem=None, **kw):
    return pltpu.CompilerParams(dimension_semantics=sem, vmem_limit_bytes=VMEM_LIMIT, **kw)


def _sigmoid(x):
    return 1.0 / (1.0 + jnp.exp(-x))


def _log_sigmoid(x):
    return jnp.minimum(x, 0.0) - jnp.log(1.0 + jnp.exp(-jnp.abs(x)))


def _bdot(a, b, dims=None):
    a, b = a.astype(BF16), b.astype(BF16)
    if dims is None:
        return jnp.dot(a, b, preferred_element_type=F32)
    return lax.dot_general(a, b, dims, preferred_element_type=F32)


def _hdot(a, b):
    return jnp.dot(a, b, precision=lax.Precision.HIGHEST, preferred_element_type=F32)


def _tile(n, cap):
    best = None
    for t in range(LANES, min(n, cap) + 1, LANES):
        if n % t == 0:
            best = t
    return best or n


def _rowwise(name, fn, rows, consts, outs, accs=(), tm=256):
    T = rows[0].shape[0]
    tm = min(tm, T)
    assert T % tm == 0
    nr, nc, no, na = len(rows), len(consts), len(outs), len(accs)

    def body(*refs):
        res = fn(*[r[...] for r in refs[:nr + nc]])
        if not isinstance(res, (tuple, list)):
            res = (res,)
        orefs, arefs = refs[nr + nc:nr + nc + no], refs[nr + nc + no:]
        for ref, val in zip(orefs, res[:no]):
            ref[...] = val.astype(ref.dtype)
        if na:
            @pl.when(pl.program_id(0) == 0)
            def _():
                for ref in arefs:
                    ref[...] = jnp.zeros(ref.shape, ref.dtype)
            for ref, val in zip(arefs, res[no:]):
                ref[...] += val

    in_specs = ([pl.BlockSpec((tm, r.shape[1]), lambda i: (i, 0)) for r in rows]
                + [pl.BlockSpec(c.shape, lambda i: (0, 0)) for c in consts])
    out_specs = ([pl.BlockSpec((tm, w), lambda i: (i, 0)) for w, _ in outs]
                 + [pl.BlockSpec(s, lambda i: (0, 0)) for s, _ in accs])
    out_shape = ([jax.ShapeDtypeStruct((T, w), dt) for w, dt in outs]
                 + [jax.ShapeDtypeStruct(s, dt) for s, dt in accs])
    return pl.pallas_call(
        body, grid=(T // tm,), in_specs=in_specs, out_specs=out_specs, out_shape=out_shape, name=name,
        compiler_params=_params(("arbitrary",) if na else ("parallel",)),
    )(*rows, *consts)


MM_TILE_CAP = 1408
MM_WHOLE_K = 2048


def _mm(name, a, b, ta=False, tb=False, out_dtype=F32, add=None):
    M, K = (a.shape[1], a.shape[0]) if ta else a.shape
    K2, N = (b.shape[1], b.shape[0]) if tb else b.shape
    assert K == K2
    tm, tn = _tile(M, MM_TILE_CAP), _tile(N, MM_TILE_CAP)
    tk = K if K <= MM_WHOLE_K else _tile(K, MM_TILE_CAP)
    assert M % tm == 0 and N % tn == 0 and K % tk == 0
    nk = K // tk
    a_dim, b_dim = (0 if ta else 1), (1 if tb else 0)

    def body(*refs):
        a_ref, b_ref = refs[0], refs[1]
        n_in = 2 if add is None else 3
        o_ref, acc = refs[n_in], (refs[n_in + 1] if nk > 1 else None)
        k = pl.program_id(2)
        part = lax.dot_general(a_ref[...].astype(BF16), b_ref[...].astype(BF16),
                               (((a_dim,), (b_dim,)), ((), ())), preferred_element_type=F32)

        def finish(r):
            if add is not None:
                r = r + refs[2][...].astype(F32)
            o_ref[...] = r.astype(o_ref.dtype)

        if nk == 1:
            finish(part)
            return

        @pl.when(k == 0)
        def _():
            acc[...] = part

        @pl.when(k > 0)
        def _():
            acc[...] += part

        @pl.when(k == nk - 1)
        def _():
            finish(acc[...])

    a_spec = pl.BlockSpec((tk, tm), lambda i, j, k: (k, i)) if ta else pl.BlockSpec((tm, tk), lambda i, j, k: (i, k))
    b_spec = pl.BlockSpec((tn, tk), lambda i, j, k: (j, k)) if tb else pl.BlockSpec((tk, tn), lambda i, j, k: (k, j))
    o_spec = pl.BlockSpec((tm, tn), lambda i, j, k: (i, j))
    ins, in_specs = [a, b], [a_spec, b_spec]
    if add is not None:
        ins.append(add)
        in_specs.append(o_spec)
    return pl.pallas_call(
        body, grid=(M // tm, N // tn, nk), in_specs=in_specs, out_specs=o_spec,
        out_shape=jax.ShapeDtypeStruct((M, N), out_dtype), scratch_shapes=[pltpu.VMEM((tm, tn), F32)] if nk > 1 else [],
        name=name, compiler_params=_params(("parallel", "parallel", "arbitrary")),
    )(*ins)


def _rowsum(x):
    return jnp.sum(x, axis=0, keepdims=True)


def _rms_stat(x):
    return lax.rsqrt(jnp.mean(x * x, axis=-1, keepdims=True) + NORM_EPS)


def _rms_bwd(dy, x, g):
    r = _rms_stat(x)
    xn = x * r
    dxn = dy * g
    dx = r * (dxn - xn * jnp.mean(dxn * xn, axis=-1, keepdims=True))
    return dx, _rowsum(dy * xn)


def _fox_c_fwd(f8t, bias_col, tc=256):
    B, H, S = f8t.shape
    tc = min(tc, S)

    def body(f_ref, b_ref, c_ref, carry):
        @pl.when(pl.program_id(1) == 0)
        def _():
            carry[...] = jnp.zeros(carry.shape, F32)
        lf = _log_sigmoid(f_ref[...] + b_ref[...])
        row = lax.broadcasted_iota(jnp.int32, (tc, tc), 0)
        col = lax.broadcasted_iota(jnp.int32, (tc, tc), 1)
        c = _hdot(lf, (row <= col).astype(F32)) + carry[...]
        c_ref[...] = c
        carry[...] = c[:, tc - 1:tc]

    return pl.pallas_call(
        body, grid=(B, S // tc),
        in_specs=[pl.BlockSpec((None, H, tc), lambda b, i: (b, 0, i)), pl.BlockSpec((H, 1), lambda b, i: (0, 0))],
        out_specs=pl.BlockSpec((None, H, tc), lambda b, i: (b, 0, i)),
        out_shape=jax.ShapeDtypeStruct((B, H, S), F32), scratch_shapes=[pltpu.VMEM((H, 1), F32)], name="fox_c_fwd",
        compiler_params=_params(("parallel", "arbitrary")),
    )(f8t, bias_col)


def _fox_c_bwd(dc, f8t, bias_col, tc=256):
    B, H, S = f8t.shape
    tc = min(tc, S)
    n = S // tc

    def body(dc_ref, f_ref, b_ref, df_ref, db_ref, carry):
        @pl.when(pl.program_id(1) == 0)
        def _():
            carry[...] = jnp.zeros(carry.shape, F32)
            db_ref[...] = jnp.zeros(db_ref.shape, F32)
        row = lax.broadcasted_iota(jnp.int32, (tc, tc), 0)
        col = lax.broadcasted_iota(jnp.int32, (tc, tc), 1)
        dlf = _hdot(dc_ref[...], (row >= col).astype(F32)) + carry[...]
        z = f_ref[...] + b_ref[...]
        df = dlf * (1.0 - _sigmoid(z))
        df_ref[...] = df
        db_ref[...] += jnp.sum(df, axis=1, keepdims=True)
        carry[...] = dlf[:, 0:1]

    rev = lambda b, i: (b, 0, n - 1 - i)
    return pl.pallas_call(
        body, grid=(B, n),
        in_specs=[pl.BlockSpec((None, H, tc), rev), pl.BlockSpec((None, H, tc), rev), pl.BlockSpec((H, 1), lambda b, i: (0, 0))],
        out_specs=[pl.BlockSpec((None, H, tc), rev), pl.BlockSpec((None, H, 1), lambda b, i: (b, 0, 0))],
        out_shape=[jax.ShapeDtypeStruct((B, H, S), F32), jax.ShapeDtypeStruct((B, H, 1), F32)],
        scratch_shapes=[pltpu.VMEM((H, 1), F32)], name="fox_c_bwd",
        compiler_params=_params(("parallel", "arbitrary")),
    )(dc, f8t, bias_col)


NEG_BIG = -1e30


def _fox_logits(q, kj, cq, ckj, i, j, t, scale):
    s = _bdot(q, kj, NT_DIMS) * scale + (cq - ckj)
    row = lax.broadcasted_iota(jnp.int32, (t, t), 0)
    col = lax.broadcasted_iota(jnp.int32, (t, t), 1)
    return s, col <= row + (i - j) * t


def _fox_fwd(qkv, cq, ck, B, t):
    T = qkv.shape[0]
    S = T // B
    n = S // t
    scale = HEAD_DIM ** -0.5

    def body(q_ref, k_ref, v_ref, cq_ref, ck_ref, o_ref, lse_ref):
        i = pl.program_id(1)
        lo = lax.broadcasted_iota(jnp.int32, (t, LANES), 1) < HEAD_DIM
        q = q_ref[...]
        qh = (jnp.where(lo, q, 0), jnp.where(lo, 0, q))

        def step(j, carry):
            rows = pl.ds(pl.multiple_of(j * t, t), t)
            kj, vj = k_ref[rows, :], v_ref[rows, :]
            new = []
            for h in range(2):
                m, l, acc = carry[h]
                s, ok = _fox_logits(qh[h], kj, cq_ref[h], ck_ref[h, j], i, j, t, scale)
                s = jnp.where(ok, s, NEG_BIG)
                m2 = jnp.maximum(m, jnp.max(s, axis=1, keepdims=True))
                p = jnp.exp(s - m2)
                al = jnp.exp(m - m2)
                new.append((m2, al * l + jnp.sum(p, axis=1, keepdims=True), al * acc + _bdot(p, vj)))
            return tuple(new)

        init = tuple((jnp.full((t, 1), NEG_BIG, F32), jnp.zeros((t, 1), F32), jnp.zeros((t, LANES), F32)) for _ in range(2))
        (m0, l0, a0), (m1, l1, a1) = lax.fori_loop(0, i + 1, step, init)
        o_ref[...] = jnp.where(lo, a0 / l0, a1 / l1).astype(o_ref.dtype)
        lse_ref[0] = m0 + jnp.log(l0)
        lse_ref[1] = m1 + jnp.log(l1)

    seq = lambda col0: pl.BlockSpec((S, LANES), lambda g, i: (g // 4, col0 + g % 4))
    blk = lambda col0: pl.BlockSpec((t, LANES), lambda g, i: ((g // 4) * n + i, col0 + g % 4))
    col = pl.BlockSpec((2, None, t, 1), lambda g, i: (g, i, 0, 0))
    return pl.pallas_call(
        body, grid=(B * 4, n), in_specs=[blk(0), seq(4), seq(8), col, pl.BlockSpec((2, n, 1, t), lambda g, i: (g, 0, 0, 0))],
        out_specs=[blk(0), col],
        out_shape=[jax.ShapeDtypeStruct((T, BR_W), BF16), jax.ShapeDtypeStruct(cq.shape, F32)], name="fox_fwd",
        compiler_params=_params(("parallel", "parallel")),
    )(qkv, qkv, qkv, cq, ck)


def _fox_bwd(qkv, o, do, cq, ck, lse, B, t):
    T = qkv.shape[0]
    S = T // B
    n = S // t
    scale = HEAD_DIM ** -0.5

    def body(q_ref, k_ref, v_ref, o_ref, do_ref, cq_ref, ck_ref, lse_ref, dq_ref, dk_ref, dv_ref, dck_ref, dcq_ref,
             dk_acc, dv_acc):
        dk_acc[...] = jnp.zeros(dk_acc.shape, F32)
        dv_acc[...] = jnp.zeros(dv_acc.shape, F32)
        dck_ref[...] = jnp.zeros(dck_ref.shape, F32)
        lo = lax.broadcasted_iota(jnp.int32, (t, LANES), 1) < HEAD_DIM

        def qloop(i, _):
            qrows = pl.ds(pl.multiple_of(i * t, t), t)
            q, do_i, o_i = q_ref[qrows, :], do_ref[qrows, :], o_ref[qrows, :].astype(F32)
            qh = (jnp.where(lo, q, 0), jnp.where(lo, 0, q))
            doh = (jnp.where(lo, do_i, 0), jnp.where(lo, 0, do_i))
            delta = [jnp.sum(doh[h].astype(F32) * o_i, axis=1, keepdims=True) for h in range(2)]

            def kloop(j, carry):
                krows = pl.ds(pl.multiple_of(j * t, t), t)
                kj, vj = k_ref[krows, :], v_ref[krows, :]
                new = []
                for h in range(2):
                    dq, dcq = carry[h]
                    s, ok = _fox_logits(qh[h], kj, cq_ref[h, i], ck_ref[h, j], i, j, t, scale)
                    p = jnp.where(ok, jnp.exp(s - lse_ref[h, i]), 0.0)
                    ds = p * (_bdot(doh[h], vj, NT_DIMS) - delta[h])
                    dv_acc[krows, :] += _bdot(p, doh[h], TN_DIMS)
                    dk_acc[krows, :] += _bdot(ds, qh[h], TN_DIMS) * scale
                    dck_ref[h, j] += -_rowsum(ds)
                    new.append((dq + _bdot(ds, kj) * scale, dcq + jnp.sum(ds, axis=1, keepdims=True)))
                return tuple(new)

            init = tuple((jnp.zeros((t, LANES), F32), jnp.zeros((t, 1), F32)) for _ in range(2))
            (dq0, dcq0), (dq1, dcq1) = lax.fori_loop(0, i + 1, kloop, init)
            dq_ref[qrows, :] = jnp.where(lo, dq0, dq1).astype(dq_ref.dtype)
            dcq_ref[0, i] = dcq0
            dcq_ref[1, i] = dcq1
            return 0

        lax.fori_loop(0, n, qloop, 0)
        dk_ref[...] = dk_acc[...].astype(dk_ref.dtype)
        dv_ref[...] = dv_acc[...].astype(dv_ref.dtype)

    seq = lambda col0: pl.BlockSpec((S, LANES), lambda g: (g // 4, col0 + g % 4))
    col = pl.BlockSpec((2, n, t, 1), lambda g: (g, 0, 0, 0))
    row = pl.BlockSpec((2, n, 1, t), lambda g: (g, 0, 0, 0))
    out = jax.ShapeDtypeStruct((T, BR_W), BF16)
    return pl.pallas_call(
        body, grid=(B * 4,), in_specs=[seq(0), seq(4), seq(8), seq(0), seq(0), col, row, col],
        out_specs=[seq(0), seq(0), seq(0), row, col],
        out_shape=[out, out, out, jax.ShapeDtypeStruct(ck.shape, F32), jax.ShapeDtypeStruct(cq.shape, F32)],
        scratch_shapes=[pltpu.VMEM((S, LANES), F32), pltpu.VMEM((S, LANES), F32)], name="fox_bwd",
        compiler_params=_params(("parallel",)),
    )(qkv, qkv, qkv, o, do, cq, ck, lse)


def _mem_probs(qh, kh):
    s = _bdot(qh, kh, NT_DIMS) * (MEM_HEAD_DIM ** -0.5)
    e = jnp.exp(s - jnp.max(s, axis=1, keepdims=True))
    return e / jnp.sum(e, axis=1, keepdims=True)


def _mem_fwd(q, mem_kv, B, tq=512):
    T = q.shape[0]
    S, Lm = T // B, mem_kv.shape[0] // B
    tq = min(tq, S)
    n = S // tq

    def body(q_ref, k_ref, v_ref, o_ref):
        for h in range(MEM_HEADS):
            sl = slice(h * MEM_HEAD_DIM, (h + 1) * MEM_HEAD_DIM)
            p = _mem_probs(q_ref[:, sl], k_ref[:, sl])
            o_ref[:, sl] = _bdot(p, v_ref[:, sl]).astype(o_ref.dtype)

    qs = pl.BlockSpec((tq, BR_W), lambda b, i: (b * n + i, 0))
    return pl.pallas_call(
        body, grid=(B, n),
        in_specs=[qs, pl.BlockSpec((Lm, BR_W), lambda b, i: (b, 0)), pl.BlockSpec((Lm, BR_W), lambda b, i: (b, 1))],
        out_specs=qs, out_shape=jax.ShapeDtypeStruct((T, BR_W), BF16), name="mem_fwd",
        compiler_params=_params(("parallel", "parallel")),
    )(q, mem_kv, mem_kv)


def _mem_bwd(q, mem_kv, do, B, tq=512):
    T = q.shape[0]
    S, Lm = T // B, mem_kv.shape[0] // B
    tq = min(tq, S)
    n = S // tq
    scale = MEM_HEAD_DIM ** -0.5

    def body(q_ref, k_ref, v_ref, do_ref, dq_ref, dk_ref, dv_ref):
        @pl.when(pl.program_id(1) == 0)
        def _():
            dk_ref[...] = jnp.zeros(dk_ref.shape, F32)
            dv_ref[...] = jnp.zeros(dv_ref.shape, F32)
        for h in range(MEM_HEADS):
            sl = slice(h * MEM_HEAD_DIM, (h + 1) * MEM_HEAD_DIM)
            qh, kh, vh, doh = q_ref[:, sl], k_ref[:, sl], v_ref[:, sl], do_ref[:, sl]
            p = _mem_probs(qh, kh)
            dp = _bdot(doh, vh, NT_DIMS)
            ds = p * (dp - jnp.sum(p * dp, axis=1, keepdims=True))
            dq_ref[:, sl] = (_bdot(ds, kh) * scale).astype(dq_ref.dtype)
            dk_ref[:, sl] += _bdot(ds, qh, TN_DIMS) * scale
            dv_ref[:, sl] += _bdot(p, doh, TN_DIMS)

    qs = pl.BlockSpec((tq, BR_W), lambda b, i: (b * n + i, 0))
    kv = pl.BlockSpec((Lm, BR_W), lambda b, i: (b, 0))
    return pl.pallas_call(
        body, grid=(B, n),
        in_specs=[qs, kv, pl.BlockSpec((Lm, BR_W), lambda b, i: (b, 1)), qs], out_specs=[qs, kv, kv],
        out_shape=[jax.ShapeDtypeStruct((T, BR_W), BF16), jax.ShapeDtypeStruct((B * Lm, BR_W), F32),
                   jax.ShapeDtypeStruct((B * Lm, BR_W), F32)], name="mem_bwd",
        compiler_params=_params(("parallel", "arbitrary")),
    )(q, mem_kv, mem_kv, do)


def _head_ones():
    h = np.arange(BR_W) // HEAD_DIM
    return jnp.asarray((h[:, None] == h[None, :]).astype(np.float32))


def _rw_prep(p, pp, mu, w0, w1, a0, w2, g_up, k_k, k_a, bd):
    ps = p + (pp - p) * mu
    r, k, v = ps[:, 0:512], ps[:, 512:1024], ps[:, 1024:1536]
    wa, gd = ps[:, 1536:1664], ps[:, 1664:1792]
    th = jnp.tanh(wa)
    z = w0 + _bdot(th, w1)
    wl = -jnp.exp(_log_sigmoid(z) - 0.5)
    w = jnp.exp(wl)
    a = _sigmoid(a0 + _bdot(wa, w2))
    sg = _sigmoid(gd)
    g = _bdot(sg, g_up)
    kq = k * k_k
    n2 = _hdot(kq * kq, bd)
    inv = lax.rsqrt(jnp.maximum(n2, 1e-24))
    kk = kq * inv
    k2 = k * (1.0 + (a - 1.0) * k_a)
    return dict(ps=ps, r=r, k=k, v=v, wa=wa, th=th, z=z, wl=wl, w=w, a=a, sg=sg, g=g, kq=kq, n2=n2, inv=inv, kk=kk, k2=k2)


def _rw_prep_fwd(p, pp, consts):
    def fn(p, pp, *c):
        t = _rw_prep(p, pp, *c)
        return t["r"], t["w"], t["k2"], t["v"], -t["kk"], t["kk"] * t["a"], t["g"]
    return _rowwise("rwkv_prep_fwd", fn, [p, pp], consts, [(BR_W, F32)] * 7, tm=256)


def _rw_prep_bwd(p, pp, cots, consts):
    def fn(p, pp, dr1, dr2, dw, dk21, dk22, dv1, dv2, dav, dbv, dg, mu, w0, w1, a0, w2, g_up, k_k, k_a, bd):
        t = _rw_prep(p, pp, mu, w0, w1, a0, w2, g_up, k_k, k_a, bd)
        dr, dk2, dv = dr1 + dr2, dk21 + dk22, dv1 + dv2
        a, k, kk, kq, inv = t["a"], t["k"], t["kk"], t["kq"], t["inv"]
        dkk = dbv * a - dav
        da = dbv * kk + dk2 * k * k_a
        dk = dk2 * (1.0 + (a - 1.0) * k_a)
        d_k_a = _rowsum(dk2 * k * (a - 1.0))
        proj = _hdot(dkk * kq, bd)
        dkq = dkk * inv - jnp.where(t["n2"] > 1e-24, kq * inv * inv * inv * proj, 0.0)
        dk = dk + dkq * k_k
        d_k_k = _rowsum(dkq * k)
        dpa = da * a * (1.0 - a)
        d_a0 = _rowsum(dpa)
        dwa = _bdot(dpa, w2, NT_DIMS)
        d_w2 = _bdot(t["wa"], dpa, TN_DIMS)
        dz = dw * t["w"] * t["wl"] * (1.0 - _sigmoid(t["z"]))
        d_w0 = _rowsum(dz)
        th = t["th"]
        dwa = dwa + _bdot(dz, w1, NT_DIMS) * (1.0 - th * th)
        d_w1 = _bdot(th, dz, TN_DIMS)
        sg = t["sg"]
        dgd = _bdot(dg, g_up, NT_DIMS) * sg * (1.0 - sg)
        d_g_up = _bdot(sg, dg, TN_DIMS)
        dps = jnp.concatenate([dr, dk, dv, dwa, dgd], axis=1)
        d_mu = _rowsum(dps * (pp - p))
        return dps * (1.0 - mu), dps * mu, d_mu, d_w0, d_w1, d_a0, d_w2, d_g_up, d_k_k, d_k_a

    accs = [((1, RWKV_COLS), F32), ((1, BR_W), F32), ((LANES, BR_W), F32), ((1, BR_W), F32), ((LANES, BR_W), F32),
            ((LANES, BR_W), F32), ((1, BR_W), F32), ((1, BR_W), F32)]
    return _rowwise("rwkv_prep_bwd", fn, [p, pp] + list(cots), consts, [(RWKV_COLS, F32)] * 2, accs, tm=128)


def _rw_head(y, r, k2, v, g, gn_g, gn_b, r_k, bd):
    mean = _hdot(y, bd) * (1.0 / HEAD_DIM)
    yc = y - mean
    rs = lax.rsqrt(_hdot(yc * yc, bd) * (1.0 / HEAD_DIM) + GN_EPS)
    yn = yc * rs
    bs = _hdot(r * k2 * r_k, bd)
    return yn, rs, bs, yn * gn_g + gn_b + bs * v


def _rw_head_fwd(y, r, k2, v, g, consts):
    def fn(y, r, k2, v, g, *c):
        return _rw_head(y, r, k2, v, g, *c)[3] * g
    return _rowwise("rwkv_head_fwd", fn, [y, r, k2, v, g], consts, [(BR_W, BF16)])[0]


def _rw_head_bwd(dout, y, r, k2, v, g, consts):
    def fn(dout, y, r, k2, v, g, gn_g, gn_b, r_k, bd):
        dout = dout.astype(F32)
        yn, rs, bs, zz = _rw_head(y, r, k2, v, g, gn_g, gn_b, r_k, bd)
        dg = dout * zz
        dz = dout * g
        dyn = dz * gn_g
        inv_n = 1.0 / HEAD_DIM
        dy = rs * (dyn - _hdot(dyn, bd) * inv_n - yn * (_hdot(dyn * yn, bd) * inv_n))
        dq = _hdot(dz * v, bd)
        return dy, dg, dq * k2 * r_k, dq * r * r_k, dz * bs, _rowsum(dz * yn), _rowsum(dz), _rowsum(dq * r * k2)
    return _rowwise("rwkv_head_bwd", fn, [dout, y, r, k2, v, g], consts, [(BR_W, F32)] * 5, [((1, BR_W), F32)] * 3)


SCAN_TC = 64


def _scan_onehot(Tc):
    w = np.zeros((Tc // 2, 2 * Tc, 2 * LANES), np.float32)
    for tt in range(Tc // 2):
        for u in range(2):
            for h in range(2):
                w[tt, h * Tc + 2 * tt + u, u * LANES + h * HEAD_DIM: u * LANES + (h + 1) * HEAD_DIM] = 1.0
    return jnp.asarray(w, BF16)


def _to_keycols(x, B, Tc):
    S = x.shape[0] // B
    x = x.reshape(B, S // Tc, Tc, 4, 2, HEAD_DIM).transpose(0, 3, 1, 5, 4, 2)
    return x.reshape(B * 4, S // Tc, HEAD_DIM, 2 * Tc)


def _split_bf16(x):
    hi = x.astype(BF16)
    return hi, (x - hi.astype(F32)).astype(BF16)


def _key_tiles(l_w, others, onehot):
    dot = lambda x: jnp.dot(x, onehot, preferred_element_type=F32)
    whi, wmid = l_w
    return [dot(whi) + dot(wmid)] + [dot(o) for o in others]


def _rw_scan_fwd(LW, LA, LB, LK, LR, v, P=2):
    NP, nc, _, Tc2 = LW.shape
    Tc = Tc2 // 2
    S = nc * Tc
    onehot = _scan_onehot(Tc)
    npb = 4 // P

    def body(lw, la, lb, lk, lr, v_ref, oh_ref, y_ref, sa_ref, sb_ref, st):
        @pl.when(pl.program_id(1) == 0)
        def _():
            st[...] = jnp.zeros(st.shape, F32)
        s = [st[p] for p in range(P)]
        cols = [(_split_bf16(lw[p]), [ref[p].astype(BF16) for ref in (la, lb, lk, lr)]) for p in range(P)]
        for tt in range(Tc // 2):
            tiles = [_key_tiles(c[0], c[1], oh_ref[tt]) for c in cols]
            for u in range(2):
                t = 2 * tt + u
                for p in range(P):
                    W, A, Bt, Kt, R = (x[:, u * LANES:(u + 1) * LANES] for x in tiles[p])
                    ls = slice(p * LANES, (p + 1) * LANES)
                    sb_ref[p, t] = s[p]
                    sa = _rowsum(s[p] * A)
                    s[p] = s[p] * W + Bt * sa + Kt * v_ref[t:t + 1, ls]
                    y_ref[t:t + 1, ls] = _rowsum(s[p] * R)
                    sa_ref[t:t + 1, ls] = sa
        for p in range(P):
            st[p] = s[p]

    lspec = pl.BlockSpec((P, None, HEAD_DIM, Tc2), lambda g, c: (g, c, 0, 0))
    rows = pl.BlockSpec((Tc, P * LANES), lambda g, c: ((g // npb) * nc + c, g % npb))
    rowshape = jax.ShapeDtypeStruct(v.shape, F32)
    return pl.pallas_call(
        body, grid=(NP // P, nc), in_specs=[lspec] * 5 + [rows, pl.BlockSpec(onehot.shape, lambda g, c: (0, 0, 0))],
        out_specs=[rows, rows, pl.BlockSpec((P, Tc, HEAD_DIM, LANES), lambda g, c: (g, c, 0, 0))],
        out_shape=[rowshape, rowshape, jax.ShapeDtypeStruct((NP, S, HEAD_DIM, LANES), F32)],
        scratch_shapes=[pltpu.VMEM((P, HEAD_DIM, LANES), F32)], name="rwkv_scan_fwd",
        compiler_params=_params(("parallel", "arbitrary")),
    )(LW, LA, LB, LK, LR, v, onehot)


SCAN_G_ROWS = 16


def _rw_scan_bwd(LW, LA, LB, LK, LR, v, sa, dy, sb, P=4):
    NP, nc, _, Tc2 = LW.shape
    Tc = Tc2 // 2
    onehot = _scan_onehot(Tc)
    npb = 4 // P

    def body(lw, la, lb, lk, lr, v_ref, sa_ref, dy_ref, sb_ref, oh_ref, dv_ref, dk_ref, db_ref, dw_ref, dr_ref, da_ref, dst):
        @pl.when(pl.program_id(1) == 0)
        def _():
            dst[...] = jnp.zeros(dst.shape, F32)
        rid = lax.broadcasted_iota(jnp.int32, (SCAN_G_ROWS, LANES), 0)
        lane = lax.broadcasted_iota(jnp.int32, (SCAN_G_ROWS, LANES), 1)
        own = (((rid % 2) == 0) == (lane < HEAD_DIM)) & (rid < 10)
        lo = lane[0:1] < HEAD_DIM
        nt = lambda rows, tile: lax.dot_general(rows.astype(BF16), tile.astype(BF16), NT_DIMS, preferred_element_type=F32)
        ds = [dst[p] for p in range(P)]
        cols = [(_split_bf16(lw[p]), [ref[p].astype(BF16) for ref in (la, lb, lk, lr)]) for p in range(P)]
        for tt in reversed(range(Tc // 2)):
            tiles = [_key_tiles(c[0], c[1], oh_ref[tt]) for c in cols]
            for u in (1, 0):
                t = 2 * tt + u
                for p in range(P):
                    W, A, Bt, Kt, R = (x[:, u * LANES:(u + 1) * LANES] for x in tiles[p])
                    ls = slice(p * LANES, (p + 1) * LANES)
                    vr, sar, dyr = (ref[t:t + 1, ls] for ref in (v_ref, sa_ref, dy_ref))
                    sp = sb_ref[p, t]
                    s_t = sp * W + Bt * sar + Kt * vr
                    d = ds[p] + R * dyr
                    dv_ref[t:t + 1, ls] = _rowsum(d * Kt)
                    dsar = _rowsum(d * Bt)
                    rows = jnp.where(rid < 2, vr, jnp.where(rid < 4, sar, jnp.where(rid < 6, 1.0, jnp.where(rid < 8, dyr, dsar))))
                    g = nt(jnp.where(own, rows, 0.0), jnp.concatenate([d, d * sp, s_t, sp], axis=0))
                    ga, gb = g[:, 0:LANES], g[:, LANES:2 * LANES]
                    ra, rb = pltpu.roll(ga, HEAD_DIM, 1), pltpu.roll(gb, HEAD_DIM, 1)
                    dk_ref[t:t + 1, ls] = jnp.where(lo, ga[0:1], ra[1:2])
                    db_ref[t:t + 1, ls] = jnp.where(lo, ga[2:3], ra[3:4])
                    dw_ref[t:t + 1, ls] = jnp.where(lo, ra[4:5], ga[5:6])
                    dr_ref[t:t + 1, ls] = jnp.where(lo, gb[6:7], rb[7:8])
                    da_ref[t:t + 1, ls] = jnp.where(lo, rb[8:9], gb[9:10])
                    ds[p] = d * W + A * dsar
        for p in range(P):
            dst[p] = ds[p]

    rev = lambda g, c: (g, nc - 1 - c, 0, 0)
    lspec = pl.BlockSpec((P, None, HEAD_DIM, Tc2), rev)
    rows = pl.BlockSpec((Tc, P * LANES), lambda g, c: ((g // npb) * nc + nc - 1 - c, g % npb))
    return pl.pallas_call(
        body, grid=(NP // P, nc),
        in_specs=[lspec] * 5 + [rows] * 3 + [pl.BlockSpec((P, Tc, HEAD_DIM, LANES), rev),
                                             pl.BlockSpec(onehot.shape, lambda g, c: (0, 0, 0))],
        out_specs=[rows] * 6, out_shape=[jax.ShapeDtypeStruct(v.shape, F32)] * 6,
        scratch_shapes=[pltpu.VMEM((P, HEAD_DIM, LANES), F32)], name="rwkv_scan_bwd",
        compiler_params=_params(("parallel", "arbitrary")),
    )(LW, LA, LB, LK, LR, v, sa, dy, sb, onehot)


def _shift_prev(p, B):
    T, W = p.shape
    return jnp.pad(p.reshape(B, T // B, W), ((0, 0), (1, 0), (0, 0)))[:, :-1].reshape(T, W)


def _shift_next(p, B):
    T, W = p.shape
    return jnp.pad(p.reshape(B, T // B, W), ((0, 0), (0, 1), (0, 0)))[:, 1:].reshape(T, W)


def _layer_step(x, mem, target, W, scan_tc=SCAN_TC, fox_t=256):
    B, S, _ = x.shape
    T = B * S
    x2, tgt2 = x.reshape(T, D_MODEL), target.reshape(T, D_MODEL)
    mem2 = mem.reshape(-1, D_MODEL)
    w_in_t = W["w_in"]
    wt_qkv, wt_rw, wt_mq, wt_gate = (w_in_t[lo:hi] for lo, hi in (COL_QKV, COL_RW, COL_MQ, COL_GATE))
    wt_f = jnp.pad(w_in_t[COL_F[0]:COL_F[1]], ((0, LANES - N_HEADS), (0, 0)))
    row = lambda v: v.reshape(1, -1).astype(F32)
    pre1_g, post1_g, pre2_g, post2_g, mem_g = (row(W[n]) for n in ("pre1_g", "post1_g", "pre2_g", "post2_g", "mem_norm_g"))

    u = _rowwise("rms_pre1", lambda x, g: x * _rms_stat(x) * g, [x2], [pre1_g], [(D_MODEL, BF16)])[0]
    qkv = _mm("proj_qkv", u, wt_qkv, tb=True, out_dtype=BF16)
    f_pad = _mm("proj_f", u, wt_f, tb=True)
    p_rw = _mm("proj_rwkv", u, wt_rw, tb=True)
    memq = _mm("proj_memq", u, wt_mq, tb=True, out_dtype=BF16)
    gate = _mm("proj_gate", u, wt_gate, tb=True)

    t = min(fox_t, S)
    bias_col = W["fox_f_bias"].reshape(N_HEADS, 1).astype(F32)
    f8t = f_pad[:, :N_HEADS].reshape(B, S, N_HEADS).transpose(0, 2, 1)
    c = _fox_c_fwd(f8t, bias_col)
    G = B * N_HEADS
    cq, ck = c.reshape(G, S // t, t, 1), c.reshape(G, S // t, 1, t)
    fox_out, lse = _fox_fwd(qkv, cq, ck, B, t)

    bd = _head_ones()
    zpad = jnp.zeros((64, BR_W), F32)
    w1 = jnp.concatenate([W["rwkv_w_up"].astype(F32), zpad], axis=0)
    w2 = jnp.concatenate([zpad, W["rwkv_a_up"].astype(F32)], axis=0)
    prep_consts = [row(W["rwkv_mu"]), row(W["rwkv_w0"]), w1, row(W["rwkv_a0"]), w2, W["rwkv_g_up"].astype(F32),
                   row(W["rwkv_k_k"]), row(W["rwkv_k_a"]), bd]
    p_prev = _shift_prev(p_rw, B)
    rr, rw, rk2, rv, rav, rbv, rg = _rw_prep_fwd(p_rw, p_prev, prep_consts)
    scan_cols = [_to_keycols(v, B, scan_tc) for v in (rw, rav, rbv, rk2, rr)]
    ry, rsa, sb = _rw_scan_fwd(*scan_cols, rv)
    head_consts = [row(W["rwkv_gn_g"]), row(W["rwkv_gn_b"]), row(W["rwkv_r_k"]), bd]
    rwkv_out = _rw_head_fwd(ry, rr, rk2, rv, rg, head_consts)

    mn = _rowwise("rms_mem", lambda m, g: m * _rms_stat(m) * g, [mem2], [mem_g], [(D_MODEL, BF16)])[0]
    mem_kv = _mm("proj_memkv", mn, W["w_mem_kv"], out_dtype=BF16)
    mem_out = _mem_fwd(memq, mem_kv, B)

    fo = [_mm("branch_" + n, a, W[n], tb=True)
          for n, a in (("w_fox_out", fox_out), ("w_rwkv_out", rwkv_out), ("w_mem_out", mem_out))]

    def merge(gate, f0, f1, f2):
        return sum(_sigmoid(gate[:, i * D_MODEL:(i + 1) * D_MODEL]) * f for i, f in enumerate((f0, f1, f2)))
    merged = _rowwise("merge", merge, [gate] + fo, [], [(D_MODEL, BF16)])[0]
    y1 = _mm("proj_o", merged, W["w_o"])

    def mid(x, y1, g1, g2):
        h1 = x + y1 * _rms_stat(y1) * g1
        return h1, h1 * _rms_stat(h1) * g2
    h1, u2 = _rowwise("norm_mid", mid, [x2, y1], [post1_g, pre2_g], [(D_MODEL, F32), (D_MODEL, BF16)])
    gt = _mm("ffn_gate", u2, W["w_ffn_gate"], tb=True)
    up = _mm("ffn_up", u2, W["w_ffn_up"], tb=True)
    act = _rowwise("swiglu", lambda gt, up: gt * _sigmoid(gt) * up, [gt, up], [], [(D_FF, BF16)])[0]
    ffn = _mm("ffn_down", act, W["w_ffn_down"])

    def tail(h1, ffn, tgt, g):
        err = h1 + ffn * _rms_stat(ffn) * g - tgt
        dh2 = err * (1.0 / D_MODEL)
        dffn, dg = _rms_bwd(dh2, ffn, g)
        loss = 0.5 * jnp.sum(jnp.sum(err * err, axis=1, keepdims=True) * (1.0 / D_MODEL), axis=0, keepdims=True)
        return dh2, dffn, dg, jnp.broadcast_to(loss, (1, LANES))
    dh2, dffn, d_post2, loss = _rowwise("loss_tail", tail, [h1, ffn, tgt2], [post2_g], [(D_MODEL, F32), (D_MODEL, BF16)],
                                        [((1, D_MODEL), F32), ((1, LANES), F32)])
    gw = {"post2_g": d_post2}
    dact = _mm("d_act", dffn, W["w_ffn_down"], tb=True)
    gw["w_ffn_down"] = _mm("g_ffn_down", act, dffn, ta=True, out_dtype=BF16)

    def swiglu_bwd(dact, gt, up):
        s = _sigmoid(gt)
        return dact * up * s * (1.0 + gt * (1.0 - s)), dact * gt * s
    dgt, dup = _rowwise("swiglu_bwd", swiglu_bwd, [dact, gt, up], [], [(D_FF, BF16)] * 2)
    du2 = _mm("d_u2_gate", dgt, W["w_ffn_gate"])
    du2 = _mm("d_u2_up", dup, W["w_ffn_up"], add=du2)
    gw["w_ffn_gate"] = _mm("g_ffn_gate", dgt, u2, ta=True, out_dtype=BF16)
    gw["w_ffn_up"] = _mm("g_ffn_up", dup, u2, ta=True, out_dtype=BF16)

    def mid_bwd(du2, dh2, h1, y1, g1, g2):
        dh1_n, d_pre2 = _rms_bwd(du2, h1, g2)
        dh1 = dh2 + dh1_n
        dy1, d_post1 = _rms_bwd(dh1, y1, g1)
        return dh1, dy1, d_post1, d_pre2
    dh1, dy1, gw["post1_g"], gw["pre2_g"] = _rowwise(
        "norm_mid_bwd", mid_bwd, [du2, dh2, h1, y1], [post1_g, pre2_g], [(D_MODEL, F32), (D_MODEL, BF16)],
        [((1, D_MODEL), F32)] * 2)
    dmerged = _mm("d_merged", dy1, W["w_o"], tb=True)
    gw["w_o"] = _mm("g_w_o", merged, dy1, ta=True, out_dtype=BF16)

    def merge_bwd(dm, gate, f0, f1, f2):
        s = [_sigmoid(gate[:, i * D_MODEL:(i + 1) * D_MODEL]) for i in range(3)]
        dgate = jnp.concatenate([dm * f * si * (1.0 - si) for f, si in zip((f0, f1, f2), s)], axis=1)
        return dm * s[0], dm * s[1], dm * s[2], dgate
    dfo0, dfo1, dfo2, dgate = _rowwise("merge_bwd", merge_bwd, [dmerged, gate] + fo, [],
                                       [(D_MODEL, BF16)] * 3 + [(3 * D_MODEL, BF16)])
    d_branch = {}
    for n, a, dfo in (("w_fox_out", fox_out, dfo0), ("w_rwkv_out", rwkv_out, dfo1), ("w_mem_out", mem_out, dfo2)):
        d_branch[n] = _mm("d_in_" + n, dfo, W[n], out_dtype=BF16)
        gw[n] = _mm("g_" + n, dfo, a, ta=True, out_dtype=BF16)

    dmemq, dkm, dvm = _mem_bwd(memq, mem_kv, d_branch["w_mem_out"], B)
    dmem_kv = jnp.concatenate([dkm, dvm], axis=1)
    gw["w_mem_kv"] = _mm("g_w_mem_kv", mn, dmem_kv, ta=True, out_dtype=BF16)
    dmn = _mm("d_mn", dmem_kv, W["w_mem_kv"], tb=True)
    gw["mem_norm_g"] = _rowwise("rms_mem_bwd", lambda d, m, g: _rms_bwd(d, m, g)[1], [dmn, mem2], [mem_g], [],
                                [((1, D_MODEL), F32)])[0]

    dfq, dfk, dfv, dck, dcq = _fox_bwd(qkv, fox_out, d_branch["w_fox_out"], cq, ck, lse, B, t)
    df8t, dbias = _fox_c_bwd(dck.reshape(B, N_HEADS, S) + dcq.reshape(B, N_HEADS, S), f8t, bias_col)
    gw["fox_f_bias"] = jnp.sum(dbias, axis=0).reshape(1, N_HEADS)
    dqkv = jnp.concatenate([dfq, dfk, dfv], axis=1)
    df_pad = jnp.pad(df8t.transpose(0, 2, 1).reshape(T, N_HEADS), ((0, 0), (0, LANES - N_HEADS))).astype(BF16)

    dry, drg, dr_h, dk2_h, dv_h, gw["rwkv_gn_g"], gw["rwkv_gn_b"], gw["rwkv_r_k"] = _rw_head_bwd(
        d_branch["w_rwkv_out"], ry, rr, rk2, rv, rg, head_consts)
    dv_s, dk2_s, db_s, dw_s, dr_s, da_s = _rw_scan_bwd(*scan_cols, rv, rsa, dry, sb)
    dP, dPp, gw["rwkv_mu"], gw["rwkv_w0"], d_w1, gw["rwkv_a0"], d_w2, gw["rwkv_g_up"], gw["rwkv_k_k"], gw["rwkv_k_a"] = \
        _rw_prep_bwd(p_rw, p_prev, [dr_s, dr_h, dw_s, dk2_s, dk2_h, dv_s, dv_h, da_s, db_s, drg], prep_consts)
    gw["rwkv_w_up"], gw["rwkv_a_up"] = d_w1[:64], d_w2[64:]
    dp_rw = (dP + _shift_next(dPp, B)).astype(BF16)

    du = _mm("d_u_qkv", dqkv, wt_qkv)
    du = _mm("d_u_f", df_pad, wt_f, add=du)
    du = _mm("d_u_rwkv", dp_rw, wt_rw, add=du)
    du = _mm("d_u_memq", dmemq, wt_mq, add=du)
    du = _mm("d_u_gate", dgate, wt_gate, add=du)
    gw["w_in"] = jnp.concatenate(
        [_mm("g_w_qkv", dqkv, u, ta=True, out_dtype=BF16), _mm("g_w_f", df_pad, u, ta=True, out_dtype=BF16)[:N_HEADS],
         _mm("g_w_rwkv", dp_rw, u, ta=True, out_dtype=BF16), _mm("g_w_memq", dmemq, u, ta=True, out_dtype=BF16),
         _mm("g_w_gate", dgate, u, ta=True, out_dtype=BF16)], axis=0)

    def pre1_bwd(du, dh1, x, g):
        dx, dg = _rms_bwd(du, x, g)
        return dh1 + dx, dg
    dx, gw["pre1_g"] = _rowwise("rms_pre1_bwd", pre1_bwd, [du, dh1, x2], [pre1_g], [(D_MODEL, F32)], [((1, D_MODEL), F32)])
    return loss[0, 0], dx.reshape(B, S, D_MODEL), gw


TRANSPOSED = ("w_in", "w_ffn_gate", "w_ffn_up", "w_fox_out", "w_rwkv_out", "w_mem_out")
LORA = ("rwkv_w_up", "rwkv_a_up", "rwkv_g_up")
ROW_SHARDED = ("w_mem_kv", "w_o", "w_ffn_down")
BIG = ("w_in", "w_ffn_gate", "w_ffn_up", "w_mem_kv", "w_o", "w_ffn_down", "w_fox_out", "w_rwkv_out", "w_mem_out") + LORA
SMALL = ("pre1_g", "post1_g", "pre2_g", "post2_g", "mem_norm_g", "fox_f_bias", "rwkv_mu", "rwkv_w0", "rwkv_a0", "rwkv_k_k",
         "rwkv_k_a", "rwkv_r_k", "rwkv_gn_g", "rwkv_gn_b")
WEIGHTS = ("pre1_g", "post1_g", "pre2_g", "post2_g", "mem_norm_g", "w_in", "fox_f_bias", "rwkv_mu", "rwkv_w0", "rwkv_w_up",
           "rwkv_a0", "rwkv_a_up", "rwkv_g_up", "rwkv_k_k", "rwkv_k_a", "rwkv_r_k", "rwkv_gn_g", "rwkv_gn_b", "w_mem_kv",
           "w_fox_out", "w_rwkv_out", "w_mem_out", "w_o", "w_ffn_gate", "w_ffn_up", "w_ffn_down")
WIRE_W = 1024
WIRE_ROW_ALIGN = 16
WIRE_HALF_ALIGN = 240


def _wire_rows(name, shard_shape):
    r, c = shard_shape
    if name in ROW_SHARDED:
        return r
    return -(-c // WIRE_ROW_ALIGN) * WIRE_ROW_ALIGN if r == WIRE_W else (r * c) // WIRE_W


def _to_wire(name, a):
    if name not in ROW_SHARDED:
        a = jnp.swapaxes(a, -1, -2)
    lead, (n, w) = a.shape[:-2], a.shape[-2:]
    if w != WIRE_W:
        return a.reshape(lead + ((n * w) // WIRE_W, WIRE_W))
    return jnp.pad(a, [(0, 0)] * len(lead) + [(0, (-n) % WIRE_ROW_ALIGN), (0, 0)])


def _from_wire(name, a, shard_shape):
    r, c = shard_shape
    if name in ROW_SHARDED:
        return a
    return a[..., :c, :] if r == WIRE_W else a.reshape(a.shape[:-2] + (c, r))


def _wire_layout(shard_shapes):
    layout, off = {}, 0
    for n in BIG:
        rows = _wire_rows(n, shard_shapes[n])
        layout[n] = (off, rows)
        off += rows
    return layout, -(-off // (2 * WIRE_HALF_ALIGN)) * WIRE_HALF_ALIGN


def _wire_pack(blocks, half_rows):
    a = jnp.concatenate(blocks, axis=-2)
    lead = a.shape[:-2]
    a = jnp.pad(a, [(0, 0)] * len(lead) + [(0, 2 * half_rows - a.shape[-2]), (0, 0)])
    return a.reshape(lead + (2, half_rows, WIRE_W))


def _my_place():
    return lax.axis_index("x"), lax.axis_index("y"), lax.axis_index("c")


def _other_chips(x, y):
    return [(1 - x, y), (x, 1 - y), (1 - x, 1 - y)]


ANY = pl.BlockSpec(memory_space=pl.ANY)


def _gather_weights(packed):
    _, R, L = packed.shape

    def body(in_ref, out_ref, send_sems, recv_sems, local_sem):
        x, y, c = _my_place()
        chip = lambda px, py: 2 * px + py
        sibling = (x, y, 1 - c)
        others = _other_chips(x, y)

        def copy(k, src, dst, to):
            return pltpu.make_async_remote_copy(src_ref=src, dst_ref=dst, send_sem=send_sems.at[k], recv_sem=recv_sems.at[k],
                                                device_id=to, device_id_type=MESH)

        mine = pltpu.make_async_copy(in_ref, out_ref.at[chip(x, y)], local_sem)
        mine.start()
        sends = [copy(j, in_ref.at[c], out_ref.at[chip(x, y), c], (px, py, c)) for j, (px, py) in enumerate(others)]
        for cp in sends:
            cp.start()
        passed = [copy(3 + j, out_ref.at[chip(px, py), c], out_ref.at[chip(px, py), c], sibling)
                  for j, (px, py) in enumerate(others)]
        for j, (px, py) in enumerate(others):
            copy(j, in_ref.at[c], out_ref.at[chip(px, py), c], (px, py, c)).wait_recv()
            passed[j].start()
        for j, (px, py) in enumerate(others):
            copy(3 + j, in_ref.at[1 - c], out_ref.at[chip(px, py), 1 - c], sibling).wait_recv()
        for cp in sends + passed:
            cp.wait_send()
        mine.wait()

    return pl.pallas_call(
        body, out_shape=jax.ShapeDtypeStruct((N_CHIPS, 2, R, L), packed.dtype), in_specs=[ANY], out_specs=ANY,
        scratch_shapes=[pltpu.SemaphoreType.DMA((6,)), pltpu.SemaphoreType.DMA((6,)), pltpu.SemaphoreType.DMA],
        name="gather_weights",
    )(packed)


def _scatter_grads(parts):
    _, _, R, L = parts.shape

    def body(in_ref, out_ref, send_sems, recv_sems, local_sem):
        x, y, c = _my_place()
        me = 4 * x + 2 * y + c
        peers = [(px, py, pc) for px in range(2) for py in range(2) for pc in range(2)]

        def copy(k, px, py, pc):
            return pltpu.make_async_remote_copy(src_ref=in_ref.at[2 * px + py, pc], dst_ref=out_ref.at[me],
                                                send_sem=send_sems.at[k], recv_sem=recv_sems.at[k],
                                                device_id=(px, py, pc), device_id_type=MESH)

        mine = pltpu.make_async_copy(in_ref.at[2 * x + y, c], out_ref.at[me], local_sem)
        mine.start()
        started = []
        for k in range(1, N_DEV):
            px, py, pc = x ^ (k >> 2), y ^ ((k >> 1) & 1), c ^ (k & 1)
            cp = copy(k - 1, px, py, pc)
            cp.start()
            started.append(cp)
        for cp in started:
            cp.wait()
        mine.wait()

    return pl.pallas_call(
        body, out_shape=jax.ShapeDtypeStruct((N_DEV, R, L), parts.dtype), in_specs=[ANY], out_specs=ANY,
        scratch_shapes=[pltpu.SemaphoreType.DMA((7,)), pltpu.SemaphoreType.DMA((7,)), pltpu.SemaphoreType.DMA],
        name="scatter_grads",
    )(parts)


def _sum_devices(parts, name):
    n, R, L = parts.shape
    tr = _tile_rows(R, WIRE_HALF_ALIGN)

    def body(p_ref, o_ref):
        acc = p_ref[0].astype(F32)
        for i in range(1, n):
            acc = acc + p_ref[i].astype(F32)
        o_ref[...] = acc

    return pl.pallas_call(
        body, grid=(R // tr,), in_specs=[pl.BlockSpec((n, tr, L), lambda i: (0, i, 0))],
        out_specs=pl.BlockSpec((tr, L), lambda i: (i, 0)), out_shape=jax.ShapeDtypeStruct((R, L), F32), name=name,
        compiler_params=_params(("parallel",)),
    )(parts)


def _tile_rows(R, cap=2048):
    best = 8
    for t in range(8, min(R, cap) + 1, 8):
        if R % t == 0:
            best = t
    return best if R % 8 == 0 else R


def _swap_halves(half):
    R, L = half.shape

    def body(in_ref, out_ref, send_sem, recv_sem, local_sem):
        x, y, c = _my_place()
        mine = pltpu.make_async_copy(in_ref, out_ref.at[c], local_sem)
        mine.start()
        cp = pltpu.make_async_remote_copy(src_ref=in_ref, dst_ref=out_ref.at[c], send_sem=send_sem, recv_sem=recv_sem,
                                          device_id=(x, y, 1 - c), device_id_type=MESH)
        cp.start()
        cp.wait()
        mine.wait()

    return pl.pallas_call(
        body, out_shape=jax.ShapeDtypeStruct((2, R, L), F32), in_specs=[ANY], out_specs=ANY,
        scratch_shapes=[pltpu.SemaphoreType.DMA, pltpu.SemaphoreType.DMA, pltpu.SemaphoreType.DMA], name="swap_halves",
    )(half)


def _allreduce_small(v):
    R, L = v.shape

    def body(in_ref, out_ref, buf, send_sems, recv_sems):
        x, y, c = _my_place()
        me = 4 * x + 2 * y + c
        buf[me] = in_ref[...]
        started = []
        for k in range(1, N_DEV):
            to = (x ^ (k >> 2), y ^ ((k >> 1) & 1), c ^ (k & 1))
            cp = pltpu.make_async_remote_copy(src_ref=in_ref, dst_ref=buf.at[me], send_sem=send_sems.at[k - 1],
                                              recv_sem=recv_sems.at[k - 1], device_id=to, device_id_type=MESH)
            cp.start()
            started.append(cp)
        for cp in started:
            cp.wait()
        acc = buf[0]
        for i in range(1, N_DEV):
            acc = acc + buf[i]
        out_ref[...] = acc

    vm = pl.BlockSpec(memory_space=pltpu.VMEM)
    return pl.pallas_call(
        body, out_shape=jax.ShapeDtypeStruct((R, L), F32), in_specs=[vm], out_specs=vm,
        scratch_shapes=[pltpu.VMEM((N_DEV, R, L), F32), pltpu.SemaphoreType.DMA((7,)), pltpu.SemaphoreType.DMA((7,))],
        name="allreduce_small",
    )(v)


def _adamw(name, w, g, m, v):
    shape = w.shape
    C = shape[-1]
    R = int(np.prod(shape[:-1]))
    args = [a.reshape(R, C).astype(F32) for a in (w, g, m, v)]
    tr = _tile_rows(R, 256)

    def body(w_ref, g_ref, m_ref, v_ref, d_ref, nm_ref, nv_ref):
        g = g_ref[...]
        m = ADAM_B1 * m_ref[...] + (1.0 - ADAM_B1) * g
        v = ADAM_B2 * v_ref[...] + (1.0 - ADAM_B2) * (g * g)
        m_hat = m / (1.0 - ADAM_B1 ** ADAM_STEP)
        v_hat = v / (1.0 - ADAM_B2 ** ADAM_STEP)
        d_ref[...] = -ADAM_LR * (m_hat / (jnp.sqrt(v_hat) + ADAM_EPS) + ADAM_WD * w_ref[...])
        nm_ref[...] = m
        nv_ref[...] = v

    spec = pl.BlockSpec((tr, C), lambda i: (i, 0))
    out = pl.pallas_call(
        body, grid=(R // tr,), in_specs=[spec] * 4, out_specs=[spec] * 3,
        out_shape=[jax.ShapeDtypeStruct((R, C), F32)] * 3, name="adamw_" + name, compiler_params=_params(("parallel",)),
    )(*args)
    return [o.reshape(shape) for o in out]


def kernel(x, mem, pre1_g, post1_g, pre2_g, post2_g, mem_norm_g, w_in, fox_f_bias, rwkv_mu, rwkv_w0, rwkv_w_up, rwkv_a0, rwkv_a_up, rwkv_g_up, rwkv_k_k, rwkv_k_a, rwkv_r_k, rwkv_gn_g, rwkv_gn_b, w_mem_kv, w_fox_out, w_rwkv_out, w_mem_out, w_o, w_ffn_gate, w_ffn_up, w_ffn_down, loss_target, m_pre1_g, m_post1_g, m_pre2_g, m_post2_g, m_mem_norm_g, m_w_in, m_fox_f_bias, m_rwkv_mu, m_rwkv_w0, m_rwkv_w_up, m_rwkv_a0, m_rwkv_a_up, m_rwkv_g_up, m_rwkv_k_k, m_rwkv_k_a, m_rwkv_r_k, m_rwkv_gn_g, m_rwkv_gn_b, m_w_mem_kv, m_w_fox_out, m_w_rwkv_out, m_w_mem_out, m_w_o, m_w_ffn_gate, m_w_ffn_up, m_w_ffn_down, v_pre1_g, v_post1_g, v_pre2_g, v_post2_g, v_mem_norm_g, v_w_in, v_fox_f_bias, v_rwkv_mu, v_rwkv_w0, v_rwkv_w_up, v_rwkv_a0, v_rwkv_a_up, v_rwkv_g_up, v_rwkv_k_k, v_rwkv_k_a, v_rwkv_r_k, v_rwkv_gn_g, v_rwkv_gn_b, v_w_mem_kv, v_w_fox_out, v_w_rwkv_out, v_w_mem_out, v_w_o, v_w_ffn_gate, v_w_ffn_up, v_w_ffn_down):
    given = dict(locals())
    w_loc = {n: given[n] for n in WEIGHTS}
    m_loc = {n: given["m_" + n] for n in WEIGHTS}
    v_loc = {n: given["v_" + n] for n in WEIGHTS}

    shard_shapes = {n: tuple(w_loc[n].shape[1:]) for n in BIG}
    layout, half_rows = _wire_layout(shard_shapes)
    mine = _wire_pack([_to_wire(n, w_loc[n][0].astype(BF16)) for n in BIG], half_rows)
    gathered = _gather_weights(mine).reshape(N_CHIPS, 2 * half_rows, WIRE_W)
    W = {}
    for n in BIG:
        off, rows = layout[n]
        blocks = _from_wire(n, gathered[:, off:off + rows], shard_shapes[n])
        if n in LORA:
            W[n] = blocks.transpose(2, 0, 1).reshape(blocks.shape[2], -1)
        else:
            W[n] = blocks.reshape(-1, blocks.shape[2])
    W.update({n: w_loc[n][0] for n in SMALL})

    loss, grad_x, gw = _layer_step(x, mem, loss_target, W)

    blocks = []
    for n in BIG:
        r, c = shard_shapes[n]
        g = gw[n].astype(BF16)
        if n in LORA:
            g = g.reshape(r, N_CHIPS, c).transpose(1, 0, 2)
        elif n in TRANSPOSED:
            g = jnp.swapaxes(g.reshape(N_CHIPS, c, r), 1, 2)
        else:
            g = g.reshape(N_CHIPS, r, c)
        blocks.append(_to_wire(n, g))
    half = _sum_devices(_scatter_grads(_wire_pack(blocks, half_rows)), "sum_grads")
    reduced = _swap_halves(half).reshape(2 * half_rows, WIRE_W)
    g_shard = {}
    for n in BIG:
        off, rows = layout[n]
        g = _from_wire(n, reduced[off:off + rows], shard_shapes[n])
        g_shard[n] = g if n in ROW_SHARDED else g.T

    small_shapes = [w_loc[n].shape[1:] for n in SMALL] + [(1,)]
    n_small = sum(int(np.prod(s)) for s in small_shapes)
    small_rows = -(-n_small // (8 * LANES)) * 8
    flat = jnp.concatenate([gw[n].reshape(-1) for n in SMALL] + [loss.reshape(1)])
    flat = jnp.pad(flat, (0, small_rows * LANES - n_small)).reshape(small_rows, LANES).reshape(-1)
    small, off = [], 0
    flat = _allreduce_small(flat.reshape(small_rows, LANES)).reshape(-1)
    for s in small_shapes:
        cnt = int(np.prod(s))
        small.append(flat[off:off + cnt].reshape(s))
        off += cnt
    g_small = dict(zip(SMALL, small[:-1]))
    loss = small[-1][0]

    grads, deltas, new_m, new_v = [], [], [], []
    for n in WEIGHTS:
        g = (g_shard[n] if n in g_shard else g_small[n]).reshape(w_loc[n].shape)
        d, nm, nv = _adamw(n, w_loc[n], g, m_loc[n], v_loc[n])
        grads.append(g)
        deltas.append(d)
        new_m.append(nm)
        new_v.append(nv)
    return (loss, grad_x, *grads, *deltas, *new_m, *new_v)
```

```python
import functools
import math

import numpy as np
import jax
import jax.numpy as jnp
from jax import lax
from jax.experimental import pallas as pl
from jax.experimental.pallas import tpu as pltpu

F32, BF16 = jnp.float32, jnp.bfloat16
MESH = pl.DeviceIdType.MESH

D_MODEL = 1024
HEAD_DIM = 64
N_HEADS = 8
BR_W = 512
MEM_HEADS = 4
MEM_HEAD_DIM = 128
D_FF = 2816
NORM_EPS = 1e-6
GN_EPS = 64e-5
N_CHIPS = 4
N_DEV = 8
LANES = 128
VMEM_LIMIT = 48 * 1024 * 1024

ADAM_LR, ADAM_B1, ADAM_B2, ADAM_EPS, ADAM_WD, ADAM_STEP = 0.001, 0.9, 0.999, 1e-08, 0.01, 10

FOX_COLS = 3 * BR_W + N_HEADS
RWKV_COLS = 3 * BR_W + 64 + 64 + 128
COL_QKV = (0, 3 * BR_W)
COL_F = (3 * BR_W, FOX_COLS)
COL_RW = (FOX_COLS, FOX_COLS + RWKV_COLS)
COL_MQ = (COL_RW[1], COL_RW[1] + BR_W)
COL_GATE = (COL_MQ[1], COL_MQ[1] + 3 * D_MODEL)

NT_DIMS = (((1,), (1,)), ((), ()))
TN_DIMS = (((0,), (0,)), ((), ()))


def _params(sem=None, **kw):
    return pltpu.CompilerParams(dimension_semantics=sem, vmem_limit_bytes=VMEM_LIMIT, **kw)


def _sigmoid(x):
    return 1.0 / (1.0 + jnp.exp(-x))


def _log_sigmoid(x):
    return jnp.minimum(x, 0.0) - jnp.log(1.0 + jnp.exp(-jnp.abs(x)))


def _bdot(a, b, dims=None):
    a, b = a.astype(BF16), b.astype(BF16)
    if dims is None:
        return jnp.dot(a, b, preferred_element_type=F32)
    return lax.dot_general(a, b, dims, preferred_element_type=F32)


def _hdot(a, b):
    return jnp.dot(a, b, precision=lax.Precision.HIGHEST, preferred_element_type=F32)


def _tile(n, cap):
    best = None
    for t in range(LANES, min(n, cap) + 1, LANES):
        if n % t == 0:
            best = t
    return best or n


def _rowwise(name, fn, rows, consts, outs, accs=(), tm=256):
    T = rows[0].shape[0]
    tm = min(tm, T)
    assert T % tm == 0
    nr, nc, no, na = len(rows), len(consts), len(outs), len(accs)

    def body(*refs):
        res = fn(*[r[...] for r in refs[:nr + nc]])
        if not isinstance(res, (tuple, list)):
            res = (res,)
        orefs, arefs = refs[nr + nc:nr + nc + no], refs[nr + nc + no:]
        for ref, val in zip(orefs, res[:no]):
            ref[...] = val.astype(ref.dtype)
        if na:
            @pl.when(pl.program_id(0) == 0)
            def _():
                for ref in arefs:
                    ref[...] = jnp.zeros(ref.shape, ref.dtype)
            for ref, val in zip(arefs, res[no:]):
                ref[...] += val

    in_specs = ([pl.BlockSpec((tm, r.shape[1]), lambda i: (i, 0)) for r in rows]
                + [pl.BlockSpec(c.shape, lambda i: (0, 0)) for c in consts])
    out_specs = ([pl.BlockSpec((tm, w), lambda i: (i, 0)) for w, _ in outs]
                 + [pl.BlockSpec(s, lambda i: (0, 0)) for s, _ in accs])
    out_shape = ([jax.ShapeDtypeStruct((T, w), dt) for w, dt in outs]
                 + [jax.ShapeDtypeStruct(s, dt) for s, dt in accs])
    return pl.pallas_call(
        body, grid=(T // tm,), in_specs=in_specs, out_specs=out_specs, out_shape=out_shape, name=name,
        compiler_params=_params(("arbitrary",) if na else ("parallel",)),
    )(*rows, *consts)


MM_TILE_CAP = 1408
MM_WHOLE_K = 2048


def _mm(name, a, b, ta=False, tb=False, out_dtype=F32, add=None):
    M, K = (a.shape[1], a.shape[0]) if ta else a.shape
    K2, N = (b.shape[1], b.shape[0]) if tb else b.shape
    assert K == K2
    tm, tn = _tile(M, MM_TILE_CAP), _tile(N, MM_TILE_CAP)
    tk = K if K <= MM_WHOLE_K else _tile(K, MM_TILE_CAP)
    assert M % tm == 0 and N % tn == 0 and K % tk == 0
    nk = K // tk
    a_dim, b_dim = (0 if ta else 1), (1 if tb else 0)

    def body(*refs):
        a_ref, b_ref = refs[0], refs[1]
        n_in = 2 if add is None else 3
        o_ref, acc = refs[n_in], (refs[n_in + 1] if nk > 1 else None)
        k = pl.program_id(2)
        part = lax.dot_general(a_ref[...].astype(BF16), b_ref[...].astype(BF16),
                               (((a_dim,), (b_dim,)), ((), ())), preferred_element_type=F32)

        def finish(r):
            if add is not None:
                r = r + refs[2][...].astype(F32)
            o_ref[...] = r.astype(o_ref.dtype)

        if nk == 1:
            finish(part)
            return

        @pl.when(k == 0)
        def _():
            acc[...] = part

        @pl.when(k > 0)
        def _():
            acc[...] += part

        @pl.when(k == nk - 1)
        def _():
            finish(acc[...])

    a_spec = pl.BlockSpec((tk, tm), lambda i, j, k: (k, i)) if ta else pl.BlockSpec((tm, tk), lambda i, j, k: (i, k))
    b_spec = pl.BlockSpec((tn, tk), lambda i, j, k: (j, k)) if tb else pl.BlockSpec((tk, tn), lambda i, j, k: (k, j))
    o_spec = pl.BlockSpec((tm, tn), lambda i, j, k: (i, j))
    ins, in_specs = [a, b], [a_spec, b_spec]
    if add is not None:
        ins.append(add)
        in_specs.append(o_spec)
    return pl.pallas_call(
        body, grid=(M // tm, N // tn, nk), in_specs=in_specs, out_specs=o_spec,
        out_shape=jax.ShapeDtypeStruct((M, N), out_dtype), scratch_shapes=[pltpu.VMEM((tm, tn), F32)] if nk > 1 else [],
        name=name, compiler_params=_params(("parallel", "parallel", "arbitrary")),
    )(*ins)


def _rowsum(x):
    return jnp.sum(x, axis=0, keepdims=True)


def _rms_stat(x):
    return lax.rsqrt(jnp.mean(x * x, axis=-1, keepdims=True) + NORM_EPS)


def _rms_bwd(dy, x, g):
    r = _rms_stat(x)
    xn = x * r
    dxn = dy * g
    dx = r * (dxn - xn * jnp.mean(dxn * xn, axis=-1, keepdims=True))
    return dx, _rowsum(dy * xn)


def _fox_c_fwd(f8t, bias_col, tc=256):
    B, H, S = f8t.shape
    tc = min(tc, S)

    def body(f_ref, b_ref, c_ref, carry):
        @pl.when(pl.program_id(1) == 0)
        def _():
            carry[...] = jnp.zeros(carry.shape, F32)
        lf = _log_sigmoid(f_ref[...] + b_ref[...])
        row = lax.broadcasted_iota(jnp.int32, (tc, tc), 0)
        col = lax.broadcasted_iota(jnp.int32, (tc, tc), 1)
        c = _hdot(lf, (row <= col).astype(F32)) + carry[...]
        c_ref[...] = c
        carry[...] = c[:, tc - 1:tc]

    return pl.pallas_call(
        body, grid=(B, S // tc),
        in_specs=[pl.BlockSpec((None, H, tc), lambda b, i: (b, 0, i)), pl.BlockSpec((H, 1), lambda b, i: (0, 0))],
        out_specs=pl.BlockSpec((None, H, tc), lambda b, i: (b, 0, i)),
        out_shape=jax.ShapeDtypeStruct((B, H, S), F32), scratch_shapes=[pltpu.VMEM((H, 1), F32)], name="fox_c_fwd",
        compiler_params=_params(("parallel", "arbitrary")),
    )(f8t, bias_col)


def _fox_c_bwd(dc, f8t, bias_col, tc=256):
    B, H, S = f8t.shape
    tc = min(tc, S)
    n = S // tc

    def body(dc_ref, f_ref, b_ref, df_ref, db_ref, carry):
        @pl.when(pl.program_id(1) == 0)
        def _():
            carry[...] = jnp.zeros(carry.shape, F32)
            db_ref[...] = jnp.zeros(db_ref.shape, F32)
        row = lax.broadcasted_iota(jnp.int32, (tc, tc), 0)
        col = lax.broadcasted_iota(jnp.int32, (tc, tc), 1)
        dlf = _hdot(dc_ref[...], (row >= col).astype(F32)) + carry[...]
        z = f_ref[...] + b_ref[...]
        df = dlf * (1.0 - _sigmoid(z))
        df_ref[...] = df
        db_ref[...] += jnp.sum(df, axis=1, keepdims=True)
        carry[...] = dlf[:, 0:1]

    rev = lambda b, i: (b, 0, n - 1 - i)
    return pl.pallas_call(
        body, grid=(B, n),
        in_specs=[pl.BlockSpec((None, H, tc), rev), pl.BlockSpec((None, H, tc), rev), pl.BlockSpec((H, 1), lambda b, i: (0, 0))],
        out_specs=[pl.BlockSpec((None, H, tc), rev), pl.BlockSpec((None, H, 1), lambda b, i: (b, 0, 0))],
        out_shape=[jax.ShapeDtypeStruct((B, H, S), F32), jax.ShapeDtypeStruct((B, H, 1), F32)],
        scratch_shapes=[pltpu.VMEM((H, 1), F32)], name="fox_c_bwd",
        compiler_params=_params(("parallel", "arbitrary")),
    )(dc, f8t, bias_col)


NEG_BIG = -1e30


def _fox_logits(q, kj, cq, ckj, i, j, t, scale):
    s = _bdot(q, kj, NT_DIMS) * scale + (cq - ckj)
    row = lax.broadcasted_iota(jnp.int32, (t, t), 0)
    col = lax.broadcasted_iota(jnp.int32, (t, t), 1)
    return s, col <= row + (i - j) * t


def _fox_fwd(qkv, cq, ck, B, t):
    T = qkv.shape[0]
    S = T // B
    n = S // t
    scale = HEAD_DIM ** -0.5

    def body(q_ref, k_ref, v_ref, cq_ref, ck_ref, o_ref, lse_ref):
        i = pl.program_id(1)
        lo = lax.broadcasted_iota(jnp.int32, (t, LANES), 1) < HEAD_DIM
        q = q_ref[...]
        qh = (jnp.where(lo, q, 0), jnp.where(lo, 0, q))

        def step(j, carry):
            rows = pl.ds(pl.multiple_of(j * t, t), t)
            kj, vj = k_ref[rows, :], v_ref[rows, :]
            new = []
            for h in range(2):
                m, l, acc = carry[h]
                s, ok = _fox_logits(qh[h], kj, cq_ref[h], ck_ref[h, j], i, j, t, scale)
                s = jnp.where(ok, s, NEG_BIG)
                m2 = jnp.maximum(m, jnp.max(s, axis=1, keepdims=True))
                p = jnp.exp(s - m2)
                al = jnp.exp(m - m2)
                new.append((m2, al * l + jnp.sum(p, axis=1, keepdims=True), al * acc + _bdot(p, vj)))
            return tuple(new)

        init = tuple((jnp.full((t, 1), NEG_BIG, F32), jnp.zeros((t, 1), F32), jnp.zeros((t, LANES), F32)) for _ in range(2))
        (m0, l0, a0), (m1, l1, a1) = lax.fori_loop(0, i + 1, step, init)
        o_ref[...] = jnp.where(lo, a0 / l0, a1 / l1).astype(o_ref.dtype)
        lse_ref[0] = m0 + jnp.log(l0)
        lse_ref[1] = m1 + jnp.log(l1)

    seq = lambda col0: pl.BlockSpec((S, LANES), lambda g, i: (g // 4, col0 + g % 4))
    blk = lambda col0: pl.BlockSpec((t, LANES), lambda g, i: ((g // 4) * n + i, col0 + g % 4))
    col = pl.BlockSpec((2, None, t, 1), lambda g, i: (g, i, 0, 0))
    return pl.pallas_call(
        body, grid=(B * 4, n), in_specs=[blk(0), seq(4), seq(8), col, pl.BlockSpec((2, n, 1, t), lambda g, i: (g, 0, 0, 0))],
        out_specs=[blk(0), col],
        out_shape=[jax.ShapeDtypeStruct((T, BR_W), BF16), jax.ShapeDtypeStruct(cq.shape, F32)], name="fox_fwd",
        compiler_params=_params(("parallel", "parallel")),
    )(qkv, qkv, qkv, cq, ck)


def _fox_bwd(qkv, o, do, cq, ck, lse, B, t):
    T = qkv.shape[0]
    S = T // B
    n = S // t
    scale = HEAD_DIM ** -0.5

    def body(q_ref, k_ref, v_ref, o_ref, do_ref, cq_ref, ck_ref, lse_ref, dq_ref, dk_ref, dv_ref, dck_ref, dcq_ref,
             dk_acc, dv_acc):
        dk_acc[...] = jnp.zeros(dk_acc.shape, F32)
        dv_acc[...] = jnp.zeros(dv_acc.shape, F32)
        dck_ref[...] = jnp.zeros(dck_ref.shape, F32)
        lo = lax.broadcasted_iota(jnp.int32, (t, LANES), 1) < HEAD_DIM

        def qloop(i, _):
            qrows = pl.ds(pl.multiple_of(i * t, t), t)
            q, do_i, o_i = q_ref[qrows, :], do_ref[qrows, :], o_ref[qrows, :].astype(F32)
            qh = (jnp.where(lo, q, 0), jnp.where(lo, 0, q))
            doh = (jnp.where(lo, do_i, 0), jnp.where(lo, 0, do_i))
            delta = [jnp.sum(doh[h].astype(F32) * o_i, axis=1, keepdims=True) for h in range(2)]

            def kloop(j, carry):
                krows = pl.ds(pl.multiple_of(j * t, t), t)
                kj, vj = k_ref[krows, :], v_ref[krows, :]
                new = []
                for h in range(2):
                    dq, dcq = carry[h]
                    s, ok = _fox_logits(qh[h], kj, cq_ref[h, i], ck_ref[h, j], i, j, t, scale)
                    p = jnp.where(ok, jnp.exp(s - lse_ref[h, i]), 0.0)
                    ds = p * (_bdot(doh[h], vj, NT_DIMS) - delta[h])
                    dv_acc[krows, :] += _bdot(p, doh[h], TN_DIMS)
                    dk_acc[krows, :] += _bdot(ds, qh[h], TN_DIMS) * scale
                    dck_ref[h, j] += -_rowsum(ds)
                    new.append((dq + _bdot(ds, kj) * scale, dcq + jnp.sum(ds, axis=1, keepdims=True)))
                return tuple(new)

            init = tuple((jnp.zeros((t, LANES), F32), jnp.zeros((t, 1), F32)) for _ in range(2))
            (dq0, dcq0), (dq1, dcq1) = lax.fori_loop(0, i + 1, kloop, init)
            dq_ref[qrows, :] = jnp.where(lo, dq0, dq1).astype(dq_ref.dtype)
            dcq_ref[0, i] = dcq0
            dcq_ref[1, i] = dcq1
            return 0

        lax.fori_loop(0, n, qloop, 0)
        dk_ref[...] = dk_acc[...].astype(dk_ref.dtype)
        dv_ref[...] = dv_acc[...].astype(dv_ref.dtype)

    seq = lambda col0: pl.BlockSpec((S, LANES), lambda g: (g // 4, col0 + g % 4))
    col = pl.BlockSpec((2, n, t, 1), lambda g: (g, 0, 0, 0))
    row = pl.BlockSpec((2, n, 1, t), lambda g: (g, 0, 0, 0))
    out = jax.ShapeDtypeStruct((T, BR_W), BF16)
    return pl.pallas_call(
        body, grid=(B * 4,), in_specs=[seq(0), seq(4), seq(8), seq(0), seq(0), col, row, col],
        out_specs=[seq(0), seq(0), seq(0), row, col],
        out_shape=[out, out, out, jax.ShapeDtypeStruct(ck.shape, F32), jax.ShapeDtypeStruct(cq.shape, F32)],
        scratch_shapes=[pltpu.VMEM((S, LANES), F32), pltpu.VMEM((S, LANES), F32)], name="fox_bwd",
        compiler_params=_params(("parallel",)),
    )(qkv, qkv, qkv, o, do, cq, ck, lse)


def _mem_probs(qh, kh):
    s = _bdot(qh, kh, NT_DIMS) * (MEM_HEAD_DIM ** -0.5)
    e = jnp.exp(s - jnp.max(s, axis=1, keepdims=True))
    return e / jnp.sum(e, axis=1, keepdims=True)


def _mem_fwd(q, mem_kv, B, tq=512):
    T = q.shape[0]
    S, Lm = T // B, mem_kv.shape[0] // B
    tq = min(tq, S)
    n = S // tq

    def body(q_ref, k_ref, v_ref, o_ref):
        for h in range(MEM_HEADS):
            sl = slice(h * MEM_HEAD_DIM, (h + 1) * MEM_HEAD_DIM)
            p = _mem_probs(q_ref[:, sl], k_ref[:, sl])
            o_ref[:, sl] = _bdot(p, v_ref[:, sl]).astype(o_ref.dtype)

    qs = pl.BlockSpec((tq, BR_W), lambda b, i: (b * n + i, 0))
    return pl.pallas_call(
        body, grid=(B, n),
        in_specs=[qs, pl.BlockSpec((Lm, BR_W), lambda b, i: (b, 0)), pl.BlockSpec((Lm, BR_W), lambda b, i: (b, 1))],
        out_specs=qs, out_shape=jax.ShapeDtypeStruct((T, BR_W), BF16), name="mem_fwd",
        compiler_params=_params(("parallel", "parallel")),
    )(q, mem_kv, mem_kv)


def _mem_bwd(q, mem_kv, do, B, tq=512):
    T = q.shape[0]
    S, Lm = T // B, mem_kv.shape[0] // B
    tq = min(tq, S)
    n = S // tq
    scale = MEM_HEAD_DIM ** -0.5

    def body(q_ref, k_ref, v_ref, do_ref, dq_ref, dk_ref, dv_ref):
        @pl.when(pl.program_id(1) == 0)
        def _():
            dk_ref[...] = jnp.zeros(dk_ref.shape, F32)
            dv_ref[...] = jnp.zeros(dv_ref.shape, F32)
        for h in range(MEM_HEADS):
            sl = slice(h * MEM_HEAD_DIM, (h + 1) * MEM_HEAD_DIM)
            qh, kh, vh, doh = q_ref[:, sl], k_ref[:, sl], v_ref[:, sl], do_ref[:, sl]
            p = _mem_probs(qh, kh)
            dp = _bdot(doh, vh, NT_DIMS)
            ds = p * (dp - jnp.sum(p * dp, axis=1, keepdims=True))
            dq_ref[:, sl] = (_bdot(ds, kh) * scale).astype(dq_ref.dtype)
            dk_ref[:, sl] += _bdot(ds, qh, TN_DIMS) * scale
            dv_ref[:, sl] += _bdot(p, doh, TN_DIMS)

    qs = pl.BlockSpec((tq, BR_W), lambda b, i: (b * n + i, 0))
    kv = pl.BlockSpec((Lm, BR_W), lambda b, i: (b, 0))
    return pl.pallas_call(
        body, grid=(B, n),
        in_specs=[qs, kv, pl.BlockSpec((Lm, BR_W), lambda b, i: (b, 1)), qs], out_specs=[qs, kv, kv],
        out_shape=[jax.ShapeDtypeStruct((T, BR_W), BF16), jax.ShapeDtypeStruct((B * Lm, BR_W), F32),
                   jax.ShapeDtypeStruct((B * Lm, BR_W), F32)], name="mem_bwd",
        compiler_params=_params(("parallel", "arbitrary")),
    )(q, mem_kv, mem_kv, do)


def _head_ones():
    h = np.arange(BR_W) // HEAD_DIM
    return jnp.asarray((h[:, None] == h[None, :]).astype(np.float32))


def _rw_prep(p, pp, mu, w0, w1, a0, w2, g_up, k_k, k_a, bd):
    ps = p + (pp - p) * mu
    r, k, v = ps[:, 0:512], ps[:, 512:1024], ps[:, 1024:1536]
    wa, gd = ps[:, 1536:1664], ps[:, 1664:1792]
    th = jnp.tanh(wa)
    z = w0 + _bdot(th, w1)
    wl = -jnp.exp(_log_sigmoid(z) - 0.5)
    w = jnp.exp(wl)
    a = _sigmoid(a0 + _bdot(wa, w2))
    sg = _sigmoid(gd)
    g = _bdot(sg, g_up)
    kq = k * k_k
    n2 = _hdot(kq * kq, bd)
    inv = lax.rsqrt(jnp.maximum(n2, 1e-24))
    kk = kq * inv
    k2 = k * (1.0 + (a - 1.0) * k_a)
    return dict(ps=ps, r=r, k=k, v=v, wa=wa, th=th, z=z, wl=wl, w=w, a=a, sg=sg, g=g, kq=kq, n2=n2, inv=inv, kk=kk, k2=k2)


def _keycol_selector(tm, Tc):
    e = np.zeros((tm, (tm // Tc) * LANES), np.float32)
    for t in range(tm):
        c, tl = divmod(t, Tc)
        e[t, c * LANES + tl] = e[t, c * LANES + Tc + tl] = 1.0
    return jnp.asarray(e, BF16)


def _rw_prep_fwd(p, pp, consts, B, Tc, tm=256):
    assert 2 * Tc == LANES
    T = p.shape[0]
    S = T // B
    nb, cpb = S // tm, tm // Tc
    sel = _keycol_selector(tm, Tc)
    nc = len(consts)

    def body(*refs):
        t = _rw_prep(*[r[...] for r in refs[:2 + nc]])
        sel_ref = refs[2 + nc]
        rows, cols = refs[3 + nc:7 + nc], refs[7 + nc:]
        for ref, val in zip(rows, (t["r"], t["k2"], t["v"], t["g"])):
            ref[...] = val
        lo = lax.broadcasted_iota(jnp.int32, (HEAD_DIM, LANES), 1) < HEAD_DIM
        operands = (t["w"], -t["kk"], t["kk"] * t["a"], t["k2"], t["r"])
        for n, (ref, x) in enumerate(zip(cols, operands)):
            terms = _split_bf16(x) if n == 0 else (x.astype(BF16),)
            for hp in range(4):
                xt = sum(lax.dot_general(tt[:, hp * LANES:(hp + 1) * LANES], sel_ref[...], TN_DIMS,
                                         preferred_element_type=F32) for tt in terms)
                for c in range(cpb):
                    blk = xt[:, c * LANES:(c + 1) * LANES]
                    ref[hp, c] = jnp.where(lo, blk[0:HEAD_DIM], blk[HEAD_DIM:2 * HEAD_DIM]).astype(ref.dtype)

    row_spec = lambda w: pl.BlockSpec((tm, w), lambda i: (i, 0))
    col_spec = pl.BlockSpec((None, 4, cpb, HEAD_DIM, LANES), lambda i: (i // nb, 0, i % nb, 0, 0))
    col_shape = lambda dt: jax.ShapeDtypeStruct((B, 4, S // Tc, HEAD_DIM, LANES), dt)
    out = pl.pallas_call(
        body, grid=(T // tm,),
        in_specs=[row_spec(RWKV_COLS)] * 2 + [pl.BlockSpec(c.shape, lambda i: (0, 0)) for c in consts]
        + [pl.BlockSpec(sel.shape, lambda i: (0, 0))],
        out_specs=[row_spec(BR_W)] * 4 + [col_spec] * 5,
        out_shape=[jax.ShapeDtypeStruct((T, BR_W), F32)] * 4 + [col_shape(F32)] + [col_shape(BF16)] * 4,
        name="rwkv_prep_fwd", compiler_params=_params(("parallel",)),
    )(p, pp, *consts, sel)
    return out[:4], [c.reshape(B * 4, S // Tc, HEAD_DIM, LANES) for c in out[4:]]


def _rw_prep_bwd(p, pp, cots, consts):
    def fn(p, pp, dr1, dr2, dw, dk21, dk22, dv1, dv2, dav, dbv, dg, mu, w0, w1, a0, w2, g_up, k_k, k_a, bd):
        t = _rw_prep(p, pp, mu, w0, w1, a0, w2, g_up, k_k, k_a, bd)
        dr, dk2, dv = dr1 + dr2, dk21 + dk22, dv1 + dv2
        a, k, kk, kq, inv = t["a"], t["k"], t["kk"], t["kq"], t["inv"]
        dkk = dbv * a - dav
        da = dbv * kk + dk2 * k * k_a
        dk = dk2 * (1.0 + (a - 1.0) * k_a)
        d_k_a = _rowsum(dk2 * k * (a - 1.0))
        proj = _hdot(dkk * kq, bd)
        dkq = dkk * inv - jnp.where(t["n2"] > 1e-24, kq * inv * inv * inv * proj, 0.0)
        dk = dk + dkq * k_k
        d_k_k = _rowsum(dkq * k)
        dpa = da * a * (1.0 - a)
        d_a0 = _rowsum(dpa)
        dwa = _bdot(dpa, w2, NT_DIMS)
        d_w2 = _bdot(t["wa"], dpa, TN_DIMS)
        dz = dw * t["w"] * t["wl"] * (1.0 - _sigmoid(t["z"]))
        d_w0 = _rowsum(dz)
        th = t["th"]
        dwa = dwa + _bdot(dz, w1, NT_DIMS) * (1.0 - th * th)
        d_w1 = _bdot(th, dz, TN_DIMS)
        sg = t["sg"]
        dgd = _bdot(dg, g_up, NT_DIMS) * sg * (1.0 - sg)
        d_g_up = _bdot(sg, dg, TN_DIMS)
        dps = jnp.concatenate([dr, dk, dv, dwa, dgd], axis=1)
        d_mu = _rowsum(dps * (pp - p))
        return dps * (1.0 - mu), dps * mu, d_mu, d_w0, d_w1, d_a0, d_w2, d_g_up, d_k_k, d_k_a

    accs = [((1, RWKV_COLS), F32), ((1, BR_W), F32), ((LANES, BR_W), F32), ((1, BR_W), F32), ((LANES, BR_W), F32),
            ((LANES, BR_W), F32), ((1, BR_W), F32), ((1, BR_W), F32)]
    return _rowwise("rwkv_prep_bwd", fn, [p, pp] + list(cots), consts, [(RWKV_COLS, F32)] * 2, accs, tm=128)


def _rw_head(y, r, k2, v, g, gn_g, gn_b, r_k, bd):
    mean = _hdot(y, bd) * (1.0 / HEAD_DIM)
    yc = y - mean
    rs = lax.rsqrt(_hdot(yc * yc, bd) * (1.0 / HEAD_DIM) + GN_EPS)
    yn = yc * rs
    bs = _hdot(r * k2 * r_k, bd)
    return yn, rs, bs, yn * gn_g + gn_b + bs * v


def _rw_head_fwd(y, r, k2, v, g, consts):
    def fn(y, r, k2, v, g, *c):
        return _rw_head(y, r, k2, v, g, *c)[3] * g
    return _rowwise("rwkv_head_fwd", fn, [y, r, k2, v, g], consts, [(BR_W, BF16)])[0]


def _rw_head_bwd(dout, y, r, k2, v, g, consts):
    def fn(dout, y, r, k2, v, g, gn_g, gn_b, r_k, bd):
        dout = dout.astype(F32)
        yn, rs, bs, zz = _rw_head(y, r, k2, v, g, gn_g, gn_b, r_k, bd)
        dg = dout * zz
        dz = dout * g
        dyn = dz * gn_g
        inv_n = 1.0 / HEAD_DIM
        dy = rs * (dyn - _hdot(dyn, bd) * inv_n - yn * (_hdot(dyn * yn, bd) * inv_n))
        dq = _hdot(dz * v, bd)
        return dy, dg, dq * k2 * r_k, dq * r * r_k, dz * bs, _rowsum(dz * yn), _rowsum(dz), _rowsum(dq * r * k2)
    return _rowwise("rwkv_head_bwd", fn, [dout, y, r, k2, v, g], consts, [(BR_W, F32)] * 5, [((1, BR_W), F32)] * 3)


SCAN_TC = 64


def _scan_onehot(Tc):
    w = np.zeros((Tc // 2, 2 * Tc, 2 * LANES), np.float32)
    for tt in range(Tc // 2):
        for u in range(2):
            for h in range(2):
                w[tt, h * Tc + 2 * tt + u, u * LANES + h * HEAD_DIM: u * LANES + (h + 1) * HEAD_DIM] = 1.0
    return jnp.asarray(w, BF16)


def _split_bf16(x):
    hi = x.astype(BF16)
    return hi, (x - hi.astype(F32)).astype(BF16)


def _key_tiles(l_w, others, onehot):
    dot = lambda x: jnp.dot(x, onehot, preferred_element_type=F32)
    whi, wmid = l_w
    return [dot(whi) + dot(wmid)] + [dot(o) for o in others]


def _rw_scan_fwd(LW, LA, LB, LK, LR, v, P=2):
    NP, nc, _, Tc2 = LW.shape
    Tc = Tc2 // 2
    S = nc * Tc
    onehot = _scan_onehot(Tc)
    npb = 4 // P

    def body(lw, la, lb, lk, lr, v_ref, oh_ref, y_ref, sa_ref, sb_ref, st):
        @pl.when(pl.program_id(1) == 0)
        def _():
            st[...] = jnp.zeros(st.shape, F32)
        s = [st[p] for p in range(P)]
        cols = [(_split_bf16(lw[p]), [ref[p].astype(BF16) for ref in (la, lb, lk, lr)]) for p in range(P)]
        for tt in range(Tc // 2):
            tiles = [_key_tiles(c[0], c[1], oh_ref[tt]) for c in cols]
            for u in range(2):
                t = 2 * tt + u
                for p in range(P):
                    W, A, Bt, Kt, R = (x[:, u * LANES:(u + 1) * LANES] for x in tiles[p])
                    ls = slice(p * LANES, (p + 1) * LANES)
                    sb_ref[p, t] = s[p]
                    sa = _rowsum(s[p] * A)
                    s[p] = s[p] * W + Bt * sa + Kt * v_ref[t:t + 1, ls]
                    y_ref[t:t + 1, ls] = _rowsum(s[p] * R)
                    sa_ref[t:t + 1, ls] = sa
        for p in range(P):
            st[p] = s[p]

    lspec = pl.BlockSpec((P, None, HEAD_DIM, Tc2), lambda g, c: (g, c, 0, 0))
    rows = pl.BlockSpec((Tc, P * LANES), lambda g, c: ((g // npb) * nc + c, g % npb))
    rowshape = jax.ShapeDtypeStruct(v.shape, F32)
    return pl.pallas_call(
        body, grid=(NP // P, nc), in_specs=[lspec] * 5 + [rows, pl.BlockSpec(onehot.shape, lambda g, c: (0, 0, 0))],
        out_specs=[rows, rows, pl.BlockSpec((P, Tc, HEAD_DIM, LANES), lambda g, c: (g, c, 0, 0))],
        out_shape=[rowshape, rowshape, jax.ShapeDtypeStruct((NP, S, HEAD_DIM, LANES), F32)],
        scratch_shapes=[pltpu.VMEM((P, HEAD_DIM, LANES), F32)], name="rwkv_scan_fwd",
        compiler_params=_params(("parallel", "arbitrary")),
    )(LW, LA, LB, LK, LR, v, onehot)


SCAN_G_ROWS = 16


def _rw_scan_bwd(LW, LA, LB, LK, LR, v, sa, dy, sb, P=4):
    NP, nc, _, Tc2 = LW.shape
    Tc = Tc2 // 2
    onehot = _scan_onehot(Tc)
    npb = 4 // P

    def body(lw, la, lb, lk, lr, v_ref, sa_ref, dy_ref, sb_ref, oh_ref, dv_ref, dk_ref, db_ref, dw_ref, dr_ref, da_ref, dst):
        @pl.when(pl.program_id(1) == 0)
        def _():
            dst[...] = jnp.zeros(dst.shape, F32)
        rid = lax.broadcasted_iota(jnp.int32, (SCAN_G_ROWS, LANES), 0)
        lane = lax.broadcasted_iota(jnp.int32, (SCAN_G_ROWS, LANES), 1)
        own = (((rid % 2) == 0) == (lane < HEAD_DIM)) & (rid < 10)
        lo = lane[0:1] < HEAD_DIM
        nt = lambda rows, tile: lax.dot_general(rows.astype(BF16), tile.astype(BF16), NT_DIMS, preferred_element_type=F32)
        ds = [dst[p] for p in range(P)]
        cols = [(_split_bf16(lw[p]), [ref[p].astype(BF16) for ref in (la, lb, lk, lr)]) for p in range(P)]
        for tt in reversed(range(Tc // 2)):
            tiles = [_key_tiles(c[0], c[1], oh_ref[tt]) for c in cols]
            for u in (1, 0):
                t = 2 * tt + u
                for p in range(P):
                    W, A, Bt, Kt, R = (x[:, u * LANES:(u + 1) * LANES] for x in tiles[p])
                    ls = slice(p * LANES, (p + 1) * LANES)
                    vr, sar, dyr = (ref[t:t + 1, ls] for ref in (v_ref, sa_ref, dy_ref))
                    sp = sb_ref[p, t]
                    s_t = sp * W + Bt * sar + Kt * vr
                    d = ds[p] + R * dyr
                    dv_ref[t:t + 1, ls] = _rowsum(d * Kt)
                    dsar = _rowsum(d * Bt)
                    rows = jnp.where(rid < 2, vr, jnp.where(rid < 4, sar, jnp.where(rid < 6, 1.0, jnp.where(rid < 8, dyr, dsar))))
                    g = nt(jnp.where(own, rows, 0.0), jnp.concatenate([d, d * sp, s_t, sp], axis=0))
                    ga, gb = g[:, 0:LANES], g[:, LANES:2 * LANES]
                    ra, rb = pltpu.roll(ga, HEAD_DIM, 1), pltpu.roll(gb, HEAD_DIM, 1)
                    dk_ref[t:t + 1, ls] = jnp.where(lo, ga[0:1], ra[1:2])
                    db_ref[t:t + 1, ls] = jnp.where(lo, ga[2:3], ra[3:4])
                    dw_ref[t:t + 1, ls] = jnp.where(lo, ra[4:5], ga[5:6])
                    dr_ref[t:t + 1, ls] = jnp.where(lo, gb[6:7], rb[7:8])
                    da_ref[t:t + 1, ls] = jnp.where(lo, rb[8:9], gb[9:10])
                    ds[p] = d * W + A * dsar
        for p in range(P):
            dst[p] = ds[p]

    rev = lambda g, c: (g, nc - 1 - c, 0, 0)
    lspec = pl.BlockSpec((P, None, HEAD_DIM, Tc2), rev)
    rows = pl.BlockSpec((Tc, P * LANES), lambda g, c: ((g // npb) * nc + nc - 1 - c, g % npb))
    return pl.pallas_call(
        body, grid=(NP // P, nc),
        in_specs=[lspec] * 5 + [rows] * 3 + [pl.BlockSpec((P, Tc, HEAD_DIM, LANES), rev),
                                             pl.BlockSpec(onehot.shape, lambda g, c: (0, 0, 0))],
        out_specs=[rows] * 6, out_shape=[jax.ShapeDtypeStruct(v.shape, F32)] * 6,
        scratch_shapes=[pltpu.VMEM((P, HEAD_DIM, LANES), F32)], name="rwkv_scan_bwd",
        compiler_params=_params(("parallel", "arbitrary")),
    )(LW, LA, LB, LK, LR, v, sa, dy, sb, onehot)


def _shift_prev(p, B):
    T, W = p.shape
    return jnp.pad(p.reshape(B, T // B, W), ((0, 0), (1, 0), (0, 0)))[:, :-1].reshape(T, W)


def _shift_next(p, B):
    T, W = p.shape
    return jnp.pad(p.reshape(B, T // B, W), ((0, 0), (0, 1), (0, 0)))[:, 1:].reshape(T, W)


def _layer_step(x, mem, target, W, scan_tc=SCAN_TC, fox_t=256):
    B, S, _ = x.shape
    T = B * S
    x2, tgt2 = x.reshape(T, D_MODEL), target.reshape(T, D_MODEL)
    mem2 = mem.reshape(-1, D_MODEL)
    w_in_t = W["w_in"]
    wt_qkv, wt_rw, wt_mq, wt_gate = (w_in_t[lo:hi] for lo, hi in (COL_QKV, COL_RW, COL_MQ, COL_GATE))
    wt_f = jnp.pad(w_in_t[COL_F[0]:COL_F[1]], ((0, LANES - N_HEADS), (0, 0)))
    row = lambda v: v.reshape(1, -1).astype(F32)
    pre1_g, post1_g, pre2_g, post2_g, mem_g = (row(W[n]) for n in ("pre1_g", "post1_g", "pre2_g", "post2_g", "mem_norm_g"))

    u = _rowwise("rms_pre1", lambda x, g: x * _rms_stat(x) * g, [x2], [pre1_g], [(D_MODEL, BF16)])[0]
    qkv = _mm("proj_qkv", u, wt_qkv, tb=True, out_dtype=BF16)
    f_pad = _mm("proj_f", u, wt_f, tb=True)
    p_rw = _mm("proj_rwkv", u, wt_rw, tb=True)
    memq = _mm("proj_memq", u, wt_mq, tb=True, out_dtype=BF16)
    gate = _mm("proj_gate", u, wt_gate, tb=True)

    t = min(fox_t, S)
    bias_col = W["fox_f_bias"].reshape(N_HEADS, 1).astype(F32)
    f8t = f_pad[:, :N_HEADS].reshape(B, S, N_HEADS).transpose(0, 2, 1)
    c = _fox_c_fwd(f8t, bias_col)
    G = B * N_HEADS
    cq, ck = c.reshape(G, S // t, t, 1), c.reshape(G, S // t, 1, t)
    fox_out, lse = _fox_fwd(qkv, cq, ck, B, t)

    bd = _head_ones()
    zpad = jnp.zeros((64, BR_W), F32)
    w1 = jnp.concatenate([W["rwkv_w_up"].astype(F32), zpad], axis=0)
    w2 = jnp.concatenate([zpad, W["rwkv_a_up"].astype(F32)], axis=0)
    prep_consts = [row(W["rwkv_mu"]), row(W["rwkv_w0"]), w1, row(W["rwkv_a0"]), w2, W["rwkv_g_up"].astype(F32),
                   row(W["rwkv_k_k"]), row(W["rwkv_k_a"]), bd]
    p_prev = _shift_prev(p_rw, B)
    (rr, rk2, rv, rg), scan_cols = _rw_prep_fwd(p_rw, p_prev, prep_consts, B, scan_tc)
    ry, rsa, sb = _rw_scan_fwd(*scan_cols, rv)
    head_consts = [row(W["rwkv_gn_g"]), row(W["rwkv_gn_b"]), row(W["rwkv_r_k"]), bd]
    rwkv_out = _rw_head_fwd(ry, rr, rk2, rv, rg, head_consts)

    mn = _rowwise("rms_mem", lambda m, g: m * _rms_stat(m) * g, [mem2], [mem_g], [(D_MODEL, BF16)])[0]
    mem_kv = _mm("proj_memkv", mn, W["w_mem_kv"], out_dtype=BF16)
    mem_out = _mem_fwd(memq, mem_kv, B)

    fo = [_mm("branch_" + n, a, W[n], tb=True)
          for n, a in (("w_fox_out", fox_out), ("w_rwkv_out", rwkv_out), ("w_mem_out", mem_out))]

    def merge(gate, f0, f1, f2):
        return sum(_sigmoid(gate[:, i * D_MODEL:(i + 1) * D_MODEL]) * f for i, f in enumerate((f0, f1, f2)))
    merged = _rowwise("merge", merge, [gate] + fo, [], [(D_MODEL, BF16)])[0]
    y1 = _mm("proj_o", merged, W["w_o"])

    def mid(x, y1, g1, g2):
        h1 = x + y1 * _rms_stat(y1) * g1
        return h1, h1 * _rms_stat(h1) * g2
    h1, u2 = _rowwise("norm_mid", mid, [x2, y1], [post1_g, pre2_g], [(D_MODEL, F32), (D_MODEL, BF16)])
    gt = _mm("ffn_gate", u2, W["w_ffn_gate"], tb=True)
    up = _mm("ffn_up", u2, W["w_ffn_up"], tb=True)
    act = _rowwise("swiglu", lambda gt, up: gt * _sigmoid(gt) * up, [gt, up], [], [(D_FF, BF16)])[0]
    ffn = _mm("ffn_down", act, W["w_ffn_down"])

    def tail(h1, ffn, tgt, g):
        err = h1 + ffn * _rms_stat(ffn) * g - tgt
        dh2 = err * (1.0 / D_MODEL)
        dffn, dg = _rms_bwd(dh2, ffn, g)
        loss = 0.5 * jnp.sum(jnp.sum(err * err, axis=1, keepdims=True) * (1.0 / D_MODEL), axis=0, keepdims=True)
        return dh2, dffn, dg, jnp.broadcast_to(loss, (1, LANES))
    dh2, dffn, d_post2, loss = _rowwise("loss_tail", tail, [h1, ffn, tgt2], [post2_g], [(D_MODEL, F32), (D_MODEL, BF16)],
                                        [((1, D_MODEL), F32), ((1, LANES), F32)])
    gw = {"post2_g": d_post2}
    dact = _mm("d_act", dffn, W["w_ffn_down"], tb=True)
    gw["w_ffn_down"] = _mm("g_ffn_down", act, dffn, ta=True, out_dtype=BF16)

    def swiglu_bwd(dact, gt, up):
        s = _sigmoid(gt)
        return dact * up * s * (1.0 + gt * (1.0 - s)), dact * gt * s
    dgt, dup = _rowwise("swiglu_bwd", swiglu_bwd, [dact, gt, up], [], [(D_FF, BF16)] * 2)
    du2 = _mm("d_u2_gate", dgt, W["w_ffn_gate"])
    du2 = _mm("d_u2_up", dup, W["w_ffn_up"], add=du2)
    gw["w_ffn_gate"] = _mm("g_ffn_gate", dgt, u2, ta=True, out_dtype=BF16)
    gw["w_ffn_up"] = _mm("g_ffn_up", dup, u2, ta=True, out_dtype=BF16)

    def mid_bwd(du2, dh2, h1, y1, g1, g2):
        dh1_n, d_pre2 = _rms_bwd(du2, h1, g2)
        dh1 = dh2 + dh1_n
        dy1, d_post1 = _rms_bwd(dh1, y1, g1)
        return dh1, dy1, d_post1, d_pre2
    dh1, dy1, gw["post1_g"], gw["pre2_g"] = _rowwise(
        "norm_mid_bwd", mid_bwd, [du2, dh2, h1, y1], [post1_g, pre2_g], [(D_MODEL, F32), (D_MODEL, BF16)],
        [((1, D_MODEL), F32)] * 2)
    dmerged = _mm("d_merged", dy1, W["w_o"], tb=True)
    gw["w_o"] = _mm("g_w_o", merged, dy1, ta=True, out_dtype=BF16)

    def merge_bwd(dm, gate, f0, f1, f2):
        s = [_sigmoid(gate[:, i * D_MODEL:(i + 1) * D_MODEL]) for i in range(3)]
        dgate = jnp.concatenate([dm * f * si * (1.0 - si) for f, si in zip((f0, f1, f2), s)], axis=1)
        return dm * s[0], dm * s[1], dm * s[2], dgate
    dfo0, dfo1, dfo2, dgate = _rowwise("merge_bwd", merge_bwd, [dmerged, gate] + fo, [],
                                       [(D_MODEL, BF16)] * 3 + [(3 * D_MODEL, BF16)])
    d_branch = {}
    for n, a, dfo in (("w_fox_out", fox_out, dfo0), ("w_rwkv_out", rwkv_out, dfo1), ("w_mem_out", mem_out, dfo2)):
        d_branch[n] = _mm("d_in_" + n, dfo, W[n], out_dtype=BF16)
        gw[n] = _mm("g_" + n, dfo, a, ta=True, out_dtype=BF16)

    dmemq, dkm, dvm = _mem_bwd(memq, mem_kv, d_branch["w_mem_out"], B)
    dmem_kv = jnp.concatenate([dkm, dvm], axis=1)
    gw["w_mem_kv"] = _mm("g_w_mem_kv", mn, dmem_kv, ta=True, out_dtype=BF16)
    dmn = _mm("d_mn", dmem_kv, W["w_mem_kv"], tb=True)
    gw["mem_norm_g"] = _rowwise("rms_mem_bwd", lambda d, m, g: _rms_bwd(d, m, g)[1], [dmn, mem2], [mem_g], [],
                                [((1, D_MODEL), F32)])[0]

    dfq, dfk, dfv, dck, dcq = _fox_bwd(qkv, fox_out, d_branch["w_fox_out"], cq, ck, lse, B, t)
    df8t, dbias = _fox_c_bwd(dck.reshape(B, N_HEADS, S) + dcq.reshape(B, N_HEADS, S), f8t, bias_col)
    gw["fox_f_bias"] = jnp.sum(dbias, axis=0).reshape(1, N_HEADS)
    dqkv = jnp.concatenate([dfq, dfk, dfv], axis=1)
    df_pad = jnp.pad(df8t.transpose(0, 2, 1).reshape(T, N_HEADS), ((0, 0), (0, LANES - N_HEADS))).astype(BF16)

    dry, drg, dr_h, dk2_h, dv_h, gw["rwkv_gn_g"], gw["rwkv_gn_b"], gw["rwkv_r_k"] = _rw_head_bwd(
        d_branch["w_rwkv_out"], ry, rr, rk2, rv, rg, head_consts)
    dv_s, dk2_s, db_s, dw_s, dr_s, da_s = _rw_scan_bwd(*scan_cols, rv, rsa, dry, sb)
    dP, dPp, gw["rwkv_mu"], gw["rwkv_w0"], d_w1, gw["rwkv_a0"], d_w2, gw["rwkv_g_up"], gw["rwkv_k_k"], gw["rwkv_k_a"] = \
        _rw_prep_bwd(p_rw, p_prev, [dr_s, dr_h, dw_s, dk2_s, dk2_h, dv_s, dv_h, da_s, db_s, drg], prep_consts)
    gw["rwkv_w_up"], gw["rwkv_a_up"] = d_w1[:64], d_w2[64:]
    dp_rw = (dP + _shift_next(dPp, B)).astype(BF16)

    du = _mm("d_u_qkv", dqkv, wt_qkv)
    du = _mm("d_u_f", df_pad, wt_f, add=du)
    du = _mm("d_u_rwkv", dp_rw, wt_rw, add=du)
    du = _mm("d_u_memq", dmemq, wt_mq, add=du)
    du = _mm("d_u_gate", dgate, wt_gate, add=du)
    gw["w_in"] = jnp.concatenate(
        [_mm("g_w_qkv", dqkv, u, ta=True, out_dtype=BF16), _mm("g_w_f", df_pad, u, ta=True, out_dtype=BF16)[:N_HEADS],
         _mm("g_w_rwkv", dp_rw, u, ta=True, out_dtype=BF16), _mm("g_w_memq", dmemq, u, ta=True, out_dtype=BF16),
         _mm("g_w_gate", dgate, u, ta=True, out_dtype=BF16)], axis=0)

    def pre1_bwd(du, dh1, x, g):
        dx, dg = _rms_bwd(du, x, g)
        return dh1 + dx, dg
    dx, gw["pre1_g"] = _rowwise("rms_pre1_bwd", pre1_bwd, [du, dh1, x2], [pre1_g], [(D_MODEL, F32)], [((1, D_MODEL), F32)])
    return loss[0, 0], dx.reshape(B, S, D_MODEL), gw


TRANSPOSED = ("w_in", "w_ffn_gate", "w_ffn_up", "w_fox_out", "w_rwkv_out", "w_mem_out")
LORA = ("rwkv_w_up", "rwkv_a_up", "rwkv_g_up")
ROW_SHARDED = ("w_mem_kv", "w_o", "w_ffn_down")
BIG = ("w_in", "w_ffn_gate", "w_ffn_up", "w_mem_kv", "w_o", "w_ffn_down", "w_fox_out", "w_rwkv_out", "w_mem_out") + LORA
SMALL = ("pre1_g", "post1_g", "pre2_g", "post2_g", "mem_norm_g", "fox_f_bias", "rwkv_mu", "rwkv_w0", "rwkv_a0", "rwkv_k_k",
         "rwkv_k_a", "rwkv_r_k", "rwkv_gn_g", "rwkv_gn_b")
WEIGHTS = ("pre1_g", "post1_g", "pre2_g", "post2_g", "mem_norm_g", "w_in", "fox_f_bias", "rwkv_mu", "rwkv_w0", "rwkv_w_up",
           "rwkv_a0", "rwkv_a_up", "rwkv_g_up", "rwkv_k_k", "rwkv_k_a", "rwkv_r_k", "rwkv_gn_g", "rwkv_gn_b", "w_mem_kv",
           "w_fox_out", "w_rwkv_out", "w_mem_out", "w_o", "w_ffn_gate", "w_ffn_up", "w_ffn_down")
WIRE_W = 1024
WIRE_ROW_ALIGN = 16
WIRE_HALF_ALIGN = 240


def _wire_rows(name, shard_shape):
    r, c = shard_shape
    if name in ROW_SHARDED:
        return r
    return -(-c // WIRE_ROW_ALIGN) * WIRE_ROW_ALIGN if r == WIRE_W else (r * c) // WIRE_W


def _to_wire(name, a):
    if name not in ROW_SHARDED:
        a = jnp.swapaxes(a, -1, -2)
    lead, (n, w) = a.shape[:-2], a.shape[-2:]
    if w != WIRE_W:
        return a.reshape(lead + ((n * w) // WIRE_W, WIRE_W))
    return jnp.pad(a, [(0, 0)] * len(lead) + [(0, (-n) % WIRE_ROW_ALIGN), (0, 0)])


def _from_wire(name, a, shard_shape):
    r, c = shard_shape
    if name in ROW_SHARDED:
        return a
    return a[..., :c, :] if r == WIRE_W else a.reshape(a.shape[:-2] + (c, r))


def _wire_layout(shard_shapes):
    layout, off = {}, 0
    for n in BIG:
        rows = _wire_rows(n, shard_shapes[n])
        layout[n] = (off, rows)
        off += rows
    return layout, -(-off // (2 * WIRE_HALF_ALIGN)) * WIRE_HALF_ALIGN


def _wire_pack(blocks, half_rows):
    a = jnp.concatenate(blocks, axis=-2)
    lead = a.shape[:-2]
    a = jnp.pad(a, [(0, 0)] * len(lead) + [(0, 2 * half_rows - a.shape[-2]), (0, 0)])
    return a.reshape(lead + (2, half_rows, WIRE_W))


def _my_place():
    return lax.axis_index("x"), lax.axis_index("y"), lax.axis_index("c")


def _other_chips(x, y):
    return [(1 - x, y), (x, 1 - y), (1 - x, 1 - y)]


ANY = pl.BlockSpec(memory_space=pl.ANY)


def _gather_weights(packed):
    _, R, L = packed.shape

    def body(in_ref, out_ref, send_sems, recv_sems, local_sem):
        x, y, c = _my_place()
        chip = lambda px, py: 2 * px + py
        sibling = (x, y, 1 - c)
        others = _other_chips(x, y)

        def copy(k, src, dst, to):
            return pltpu.make_async_remote_copy(src_ref=src, dst_ref=dst, send_sem=send_sems.at[k], recv_sem=recv_sems.at[k],
                                                device_id=to, device_id_type=MESH)

        mine = pltpu.make_async_copy(in_ref, out_ref.at[chip(x, y)], local_sem)
        mine.start()
        sends = [copy(j, in_ref.at[c], out_ref.at[chip(x, y), c], (px, py, c)) for j, (px, py) in enumerate(others)]
        for cp in sends:
            cp.start()
        passed = [copy(3 + j, out_ref.at[chip(px, py), c], out_ref.at[chip(px, py), c], sibling)
                  for j, (px, py) in enumerate(others)]
        for j, (px, py) in enumerate(others):
            copy(j, in_ref.at[c], out_ref.at[chip(px, py), c], (px, py, c)).wait_recv()
            passed[j].start()
        for j, (px, py) in enumerate(others):
            copy(3 + j, in_ref.at[1 - c], out_ref.at[chip(px, py), 1 - c], sibling).wait_recv()
        for cp in sends + passed:
            cp.wait_send()
        mine.wait()

    return pl.pallas_call(
        body, out_shape=jax.ShapeDtypeStruct((N_CHIPS, 2, R, L), packed.dtype), in_specs=[ANY], out_specs=ANY,
        scratch_shapes=[pltpu.SemaphoreType.DMA((6,)), pltpu.SemaphoreType.DMA((6,)), pltpu.SemaphoreType.DMA],
        name="gather_weights",
    )(packed)


def _pair_exchange(parts):
    n, _, R, L = parts.shape

    def body(in_ref, out_ref, send_sems, recv_sems):
        x, y, c = _my_place()
        copies = [pltpu.make_async_remote_copy(src_ref=in_ref.at[s, 1 - c], dst_ref=out_ref.at[s], send_sem=send_sems.at[s],
                                               recv_sem=recv_sems.at[s], device_id=(x, y, 1 - c), device_id_type=MESH)
                  for s in range(n)]
        for cp in copies:
            cp.start()
        for cp in copies:
            cp.wait()

    return pl.pallas_call(
        body, out_shape=jax.ShapeDtypeStruct((n, R, L), parts.dtype), in_specs=[ANY], out_specs=ANY,
        scratch_shapes=[pltpu.SemaphoreType.DMA((n,)), pltpu.SemaphoreType.DMA((n,))], name="pair_exchange",
    )(parts)


def _pair_add(a, b):
    n, R, L = a.shape
    tr = _tile_rows(R, WIRE_HALF_ALIGN)

    def body(a_ref, b_ref, o_ref):
        o_ref[...] = (a_ref[...].astype(F32) + b_ref[...].astype(F32)).astype(o_ref.dtype)

    spec = pl.BlockSpec((n, tr, L), lambda i: (0, i, 0))
    return pl.pallas_call(
        body, grid=(R // tr,), in_specs=[spec, spec], out_specs=spec, out_shape=jax.ShapeDtypeStruct(a.shape, a.dtype),
        name="pair_add", compiler_params=_params(("parallel",)),
    )(a, b)


def _scatter_grads(parts):
    n, R, L = parts.shape

    def body(in_ref, out_ref, send_sems, recv_sems, local_sem):
        x, y, c = _my_place()
        me = 2 * x + y
        mine = pltpu.make_async_copy(in_ref.at[me], out_ref.at[me], local_sem)
        mine.start()
        copies = [pltpu.make_async_remote_copy(src_ref=in_ref.at[2 * px + py], dst_ref=out_ref.at[me], send_sem=send_sems.at[j],
                                               recv_sem=recv_sems.at[j], device_id=(px, py, c), device_id_type=MESH)
                  for j, (px, py) in enumerate(_other_chips(x, y))]
        for cp in copies:
            cp.start()
        for cp in copies:
            cp.wait()
        mine.wait()

    return pl.pallas_call(
        body, out_shape=jax.ShapeDtypeStruct((n, R, L), parts.dtype), in_specs=[ANY], out_specs=ANY,
        scratch_shapes=[pltpu.SemaphoreType.DMA((3,)), pltpu.SemaphoreType.DMA((3,)), pltpu.SemaphoreType.DMA],
        name="scatter_grads",
    )(parts)


def _sum_devices(parts, name):
    n, R, L = parts.shape
    tr = _tile_rows(R, WIRE_HALF_ALIGN)

    def body(p_ref, o_ref):
        acc = p_ref[0].astype(F32)
        for i in range(1, n):
            acc = acc + p_ref[i].astype(F32)
        o_ref[...] = acc

    return pl.pallas_call(
        body, grid=(R // tr,), in_specs=[pl.BlockSpec((n, tr, L), lambda i: (0, i, 0))],
        out_specs=pl.BlockSpec((tr, L), lambda i: (i, 0)), out_shape=jax.ShapeDtypeStruct((R, L), F32), name=name,
        compiler_params=_params(("parallel",)),
    )(parts)


def _tile_rows(R, cap=2048):
    best = 8
    for t in range(8, min(R, cap) + 1, 8):
        if R % t == 0:
            best = t
    return best if R % 8 == 0 else R


def _swap_halves(half):
    R, L = half.shape

    def body(in_ref, out_ref, send_sem, recv_sem, local_sem):
        x, y, c = _my_place()
        mine = pltpu.make_async_copy(in_ref, out_ref.at[c], local_sem)
        mine.start()
        cp = pltpu.make_async_remote_copy(src_ref=in_ref, dst_ref=out_ref.at[c], send_sem=send_sem, recv_sem=recv_sem,
                                          device_id=(x, y, 1 - c), device_id_type=MESH)
        cp.start()
        cp.wait()
        mine.wait()

    return pl.pallas_call(
        body, out_shape=jax.ShapeDtypeStruct((2, R, L), F32), in_specs=[ANY], out_specs=ANY,
        scratch_shapes=[pltpu.SemaphoreType.DMA, pltpu.SemaphoreType.DMA, pltpu.SemaphoreType.DMA], name="swap_halves",
    )(half)


def _allreduce_small(v):
    R, L = v.shape

    def body(in_ref, out_ref, buf, send_sems, recv_sems):
        x, y, c = _my_place()
        me = 4 * x + 2 * y + c
        buf[me] = in_ref[...]
        started = []
        for k in range(1, N_DEV):
            to = (x ^ (k >> 2), y ^ ((k >> 1) & 1), c ^ (k & 1))
            cp = pltpu.make_async_remote_copy(src_ref=in_ref, dst_ref=buf.at[me], send_sem=send_sems.at[k - 1],
                                              recv_sem=recv_sems.at[k - 1], device_id=to, device_id_type=MESH)
            cp.start()
            started.append(cp)
        for cp in started:
            cp.wait()
        acc = buf[0]
        for i in range(1, N_DEV):
            acc = acc + buf[i]
        out_ref[...] = acc

    vm = pl.BlockSpec(memory_space=pltpu.VMEM)
    return pl.pallas_call(
        body, out_shape=jax.ShapeDtypeStruct((R, L), F32), in_specs=[vm], out_specs=vm,
        scratch_shapes=[pltpu.VMEM((N_DEV, R, L), F32), pltpu.SemaphoreType.DMA((7,)), pltpu.SemaphoreType.DMA((7,))],
        name="allreduce_small",
    )(v)


def _adamw(name, w, g, m, v):
    shape = w.shape
    C = shape[-1]
    R = int(np.prod(shape[:-1]))
    args = [a.reshape(R, C).astype(F32) for a in (w, g, m, v)]
    tr = _tile_rows(R, 256)

    def body(w_ref, g_ref, m_ref, v_ref, d_ref, nm_ref, nv_ref):
        g = g_ref[...]
        m = ADAM_B1 * m_ref[...] + (1.0 - ADAM_B1) * g
        v = ADAM_B2 * v_ref[...] + (1.0 - ADAM_B2) * (g * g)
        m_hat = m / (1.0 - ADAM_B1 ** ADAM_STEP)
        v_hat = v / (1.0 - ADAM_B2 ** ADAM_STEP)
        d_ref[...] = -ADAM_LR * (m_hat / (jnp.sqrt(v_hat) + ADAM_EPS) + ADAM_WD * w_ref[...])
        nm_ref[...] = m
        nv_ref[...] = v

    spec = pl.BlockSpec((tr, C), lambda i: (i, 0))
    out = pl.pallas_call(
        body, grid=(R // tr,), in_specs=[spec] * 4, out_specs=[spec] * 3,
        out_shape=[jax.ShapeDtypeStruct((R, C), F32)] * 3, name="adamw_" + name, compiler_params=_params(("parallel",)),
    )(*args)
    return [o.reshape(shape) for o in out]


def kernel(x, mem, pre1_g, post1_g, pre2_g, post2_g, mem_norm_g, w_in, fox_f_bias, rwkv_mu, rwkv_w0, rwkv_w_up, rwkv_a0, rwkv_a_up, rwkv_g_up, rwkv_k_k, rwkv_k_a, rwkv_r_k, rwkv_gn_g, rwkv_gn_b, w_mem_kv, w_fox_out, w_rwkv_out, w_mem_out, w_o, w_ffn_gate, w_ffn_up, w_ffn_down, loss_target, m_pre1_g, m_post1_g, m_pre2_g, m_post2_g, m_mem_norm_g, m_w_in, m_fox_f_bias, m_rwkv_mu, m_rwkv_w0, m_rwkv_w_up, m_rwkv_a0, m_rwkv_a_up, m_rwkv_g_up, m_rwkv_k_k, m_rwkv_k_a, m_rwkv_r_k, m_rwkv_gn_g, m_rwkv_gn_b, m_w_mem_kv, m_w_fox_out, m_w_rwkv_out, m_w_mem_out, m_w_o, m_w_ffn_gate, m_w_ffn_up, m_w_ffn_down, v_pre1_g, v_post1_g, v_pre2_g, v_post2_g, v_mem_norm_g, v_w_in, v_fox_f_bias, v_rwkv_mu, v_rwkv_w0, v_rwkv_w_up, v_rwkv_a0, v_rwkv_a_up, v_rwkv_g_up, v_rwkv_k_k, v_rwkv_k_a, v_rwkv_r_k, v_rwkv_gn_g, v_rwkv_gn_b, v_w_mem_kv, v_w_fox_out, v_w_rwkv_out, v_w_mem_out, v_w_o, v_w_ffn_gate, v_w_ffn_up, v_w_ffn_down):
    given = dict(locals())
    w_loc = {n: given[n] for n in WEIGHTS}
    m_loc = {n: given["m_" + n] for n in WEIGHTS}
    v_loc = {n: given["v_" + n] for n in WEIGHTS}

    shard_shapes = {n: tuple(w_loc[n].shape[1:]) for n in BIG}
    layout, half_rows = _wire_layout(shard_shapes)
    mine = _wire_pack([_to_wire(n, w_loc[n][0].astype(BF16)) for n in BIG], half_rows)
    gathered = _gather_weights(mine).reshape(N_CHIPS, 2 * half_rows, WIRE_W)
    W = {}
    for n in BIG:
        off, rows = layout[n]
        blocks = _from_wire(n, gathered[:, off:off + rows], shard_shapes[n])
        if n in LORA:
            W[n] = blocks.transpose(2, 0, 1).reshape(blocks.shape[2], -1)
        else:
            W[n] = blocks.reshape(-1, blocks.shape[2])
    W.update({n: w_loc[n][0] for n in SMALL})

    loss, grad_x, gw = _layer_step(x, mem, loss_target, W)

    blocks = []
    for n in BIG:
        r, c = shard_shapes[n]
        g = gw[n].astype(BF16)
        if n in LORA:
            g = g.reshape(r, N_CHIPS, c).transpose(1, 0, 2)
        elif n in TRANSPOSED:
            g = jnp.swapaxes(g.reshape(N_CHIPS, c, r), 1, 2)
        else:
            g = g.reshape(N_CHIPS, r, c)
        blocks.append(_to_wire(n, g))
    packed = _wire_pack(blocks, half_rows)
    own_halves = lax.dynamic_index_in_dim(packed, lax.axis_index("c"), axis=1, keepdims=False)
    chip_sums = _pair_add(own_halves, _pair_exchange(packed))
    half = _sum_devices(_scatter_grads(chip_sums), "sum_grads")
    reduced = _swap_halves(half).reshape(2 * half_rows, WIRE_W)
    g_shard = {}
    for n in BIG:
        off, rows = layout[n]
        g = _from_wire(n, reduced[off:off + rows], shard_shapes[n])
        g_shard[n] = g if n in ROW_SHARDED else g.T

    small_shapes = [w_loc[n].shape[1:] for n in SMALL] + [(1,)]
    n_small = sum(int(np.prod(s)) for s in small_shapes)
    small_rows = -(-n_small // (8 * LANES)) * 8
    flat = jnp.concatenate([gw[n].reshape(-1) for n in SMALL] + [loss.reshape(1)])
    flat = jnp.pad(flat, (0, small_rows * LANES - n_small)).reshape(small_rows, LANES).reshape(-1)
    small, off = [], 0
    flat = _allreduce_small(flat.reshape(small_rows, LANES)).reshape(-1)
    for s in small_shapes:
        cnt = int(np.prod(s))
        small.append(flat[off:off + cnt].reshape(s))
        off += cnt
    g_small = dict(zip(SMALL, small[:-1]))
    loss = small[-1][0]

    grads, deltas, new_m, new_v = [], [], [], []
    for n in WEIGHTS:
        g = (g_shard[n] if n in g_shard else g_small[n]).reshape(w_loc[n].shape)
        d, nm, nv = _adamw(n, w_loc[n], g, m_loc[n], v_loc[n])
        grads.append(g)
        deltas.append(d)
        new_m.append(nm)
        new_v.append(nv)
    return (loss, grad_x, *grads, *deltas, *new_m, *new_v)
```

```python
import functools
import math

import numpy as np
import jax
import jax.numpy as jnp
from jax import lax
from jax.experimental import pallas as pl
from jax.experimental.pallas import tpu as pltpu

F32, BF16 = jnp.float32, jnp.bfloat16
MESH = pl.DeviceIdType.MESH

D_MODEL = 1024
HEAD_DIM = 64
N_HEADS = 8
BR_W = 512
MEM_HEADS = 4
MEM_HEAD_DIM = 128
D_FF = 2816
NORM_EPS = 1e-6
GN_EPS = 64e-5
N_CHIPS = 4
N_DEV = 8
LANES = 128
VMEM_LIMIT = 48 * 1024 * 1024

ADAM_LR, ADAM_B1, ADAM_B2, ADAM_EPS, ADAM_WD, ADAM_STEP = 0.001, 0.9, 0.999, 1e-08, 0.01, 10

FOX_COLS = 3 * BR_W + N_HEADS
RWKV_COLS = 3 * BR_W + 64 + 64 + 128
COL_QKV = (0, 3 * BR_W)
COL_F = (3 * BR_W, FOX_COLS)
COL_RW = (FOX_COLS, FOX_COLS + RWKV_COLS)
COL_MQ = (COL_RW[1], COL_RW[1] + BR_W)
COL_GATE = (COL_MQ[1], COL_MQ[1] + 3 * D_MODEL)

NT_DIMS = (((1,), (1,)), ((), ()))
TN_DIMS = (((0,), (0,)), ((), ()))


def _params(sem=None, **kw):
    return pltpu.CompilerParams(dimension_semantics=sem, vmem_limit_bytes=VMEM_LIMIT, **kw)


def _sigmoid(x):
    return 1.0 / (1.0 + jnp.exp(-x))


def _log_sigmoid(x):
    return jnp.minimum(x, 0.0) - jnp.log(1.0 + jnp.exp(-jnp.abs(x)))


def _bdot(a, b, dims=None):
    a, b = a.astype(BF16), b.astype(BF16)
    if dims is None:
        return jnp.dot(a, b, preferred_element_type=F32)
    return lax.dot_general(a, b, dims, preferred_element_type=F32)


def _hdot(a, b):
    return jnp.dot(a, b, precision=lax.Precision.HIGHEST, preferred_element_type=F32)


def _tile(n, cap):
    best = None
    for t in range(LANES, min(n, cap) + 1, LANES):
        if n % t == 0:
            best = t
    return best or n


def _rowwise(name, fn, rows, consts, outs, accs=(), tm=256):
    T = rows[0].shape[0]
    tm = min(tm, T)
    assert T % tm == 0
    nr, nc, no, na = len(rows), len(consts), len(outs), len(accs)

    def body(*refs):
        res = fn(*[r[...] for r in refs[:nr + nc]])
        if not isinstance(res, (tuple, list)):
            res = (res,)
        orefs, arefs = refs[nr + nc:nr + nc + no], refs[nr + nc + no:]
        for ref, val in zip(orefs, res[:no]):
            ref[...] = val.astype(ref.dtype)
        if na:
            @pl.when(pl.program_id(0) == 0)
            def _():
                for ref in arefs:
                    ref[...] = jnp.zeros(ref.shape, ref.dtype)
            for ref, val in zip(arefs, res[no:]):
                ref[...] += val

    in_specs = ([pl.BlockSpec((tm, r.shape[1]), lambda i: (i, 0)) for r in rows]
                + [pl.BlockSpec(c.shape, lambda i: (0, 0)) for c in consts])
    out_specs = ([pl.BlockSpec((tm, w), lambda i: (i, 0)) for w, _ in outs]
                 + [pl.BlockSpec(s, lambda i: (0, 0)) for s, _ in accs])
    out_shape = ([jax.ShapeDtypeStruct((T, w), dt) for w, dt in outs]
                 + [jax.ShapeDtypeStruct(s, dt) for s, dt in accs])
    return pl.pallas_call(
        body, grid=(T // tm,), in_specs=in_specs, out_specs=out_specs, out_shape=out_shape, name=name,
        compiler_params=_params(("arbitrary",) if na else ("parallel",)),
    )(*rows, *consts)


MM_TILE_CAP = 1408
MM_WHOLE_K = 2048


def _mm(name, a, b, ta=False, tb=False, out_dtype=F32, add=None):
    M, K = (a.shape[1], a.shape[0]) if ta else a.shape
    K2, N = (b.shape[1], b.shape[0]) if tb else b.shape
    assert K == K2
    tm, tn = _tile(M, MM_TILE_CAP), _tile(N, MM_TILE_CAP)
    tk = K if K <= MM_WHOLE_K else _tile(K, MM_TILE_CAP)
    assert M % tm == 0 and N % tn == 0 and K % tk == 0
    nk = K // tk
    a_dim, b_dim = (0 if ta else 1), (1 if tb else 0)

    def body(*refs):
        a_ref, b_ref = refs[0], refs[1]
        n_in = 2 if add is None else 3
        o_ref, acc = refs[n_in], (refs[n_in + 1] if nk > 1 else None)
        k = pl.program_id(2)
        part = lax.dot_general(a_ref[...].astype(BF16), b_ref[...].astype(BF16),
                               (((a_dim,), (b_dim,)), ((), ())), preferred_element_type=F32)

        def finish(r):
            if add is not None:
                r = r + refs[2][...].astype(F32)
            o_ref[...] = r.astype(o_ref.dtype)

        if nk == 1:
            finish(part)
            return

        @pl.when(k == 0)
        def _():
            acc[...] = part

        @pl.when(k > 0)
        def _():
            acc[...] += part

        @pl.when(k == nk - 1)
        def _():
            finish(acc[...])

    a_spec = pl.BlockSpec((tk, tm), lambda i, j, k: (k, i)) if ta else pl.BlockSpec((tm, tk), lambda i, j, k: (i, k))
    b_spec = pl.BlockSpec((tn, tk), lambda i, j, k: (j, k)) if tb else pl.BlockSpec((tk, tn), lambda i, j, k: (k, j))
    o_spec = pl.BlockSpec((tm, tn), lambda i, j, k: (i, j))
    ins, in_specs = [a, b], [a_spec, b_spec]
    if add is not None:
        ins.append(add)
        in_specs.append(o_spec)
    return pl.pallas_call(
        body, grid=(M // tm, N // tn, nk), in_specs=in_specs, out_specs=o_spec,
        out_shape=jax.ShapeDtypeStruct((M, N), out_dtype), scratch_shapes=[pltpu.VMEM((tm, tn), F32)] if nk > 1 else [],
        name=name, compiler_params=_params(("parallel", "parallel", "arbitrary")),
    )(*ins)


def _rowsum(x):
    return jnp.sum(x, axis=0, keepdims=True)


def _rms_stat(x):
    return lax.rsqrt(jnp.mean(x * x, axis=-1, keepdims=True) + NORM_EPS)


def _rms_bwd(dy, x, g):
    r = _rms_stat(x)
    xn = x * r
    dxn = dy * g
    dx = r * (dxn - xn * jnp.mean(dxn * xn, axis=-1, keepdims=True))
    return dx, _rowsum(dy * xn)


def _fox_c_fwd(f8t, bias_col, tc=256):
    B, H, S = f8t.shape
    tc = min(tc, S)

    def body(f_ref, b_ref, c_ref, carry):
        @pl.when(pl.program_id(1) == 0)
        def _():
            carry[...] = jnp.zeros(carry.shape, F32)
        lf = _log_sigmoid(f_ref[...] + b_ref[...])
        row = lax.broadcasted_iota(jnp.int32, (tc, tc), 0)
        col = lax.broadcasted_iota(jnp.int32, (tc, tc), 1)
        c = _hdot(lf, (row <= col).astype(F32)) + carry[...]
        c_ref[...] = c
        carry[...] = c[:, tc - 1:tc]

    return pl.pallas_call(
        body, grid=(B, S // tc),
        in_specs=[pl.BlockSpec((None, H, tc), lambda b, i: (b, 0, i)), pl.BlockSpec((H, 1), lambda b, i: (0, 0))],
        out_specs=pl.BlockSpec((None, H, tc), lambda b, i: (b, 0, i)),
        out_shape=jax.ShapeDtypeStruct((B, H, S), F32), scratch_shapes=[pltpu.VMEM((H, 1), F32)], name="fox_c_fwd",
        compiler_params=_params(("parallel", "arbitrary")),
    )(f8t, bias_col)


def _fox_c_bwd(dc, f8t, bias_col, tc=256):
    B, H, S = f8t.shape
    tc = min(tc, S)
    n = S // tc

    def body(dc_ref, f_ref, b_ref, df_ref, db_ref, carry):
        @pl.when(pl.program_id(1) == 0)
        def _():
            carry[...] = jnp.zeros(carry.shape, F32)
            db_ref[...] = jnp.zeros(db_ref.shape, F32)
        row = lax.broadcasted_iota(jnp.int32, (tc, tc), 0)
        col = lax.broadcasted_iota(jnp.int32, (tc, tc), 1)
        dlf = _hdot(dc_ref[...], (row >= col).astype(F32)) + carry[...]
        z = f_ref[...] + b_ref[...]
        df = dlf * (1.0 - _sigmoid(z))
        df_ref[...] = df
        db_ref[...] += jnp.sum(df, axis=1, keepdims=True)
        carry[...] = dlf[:, 0:1]

    rev = lambda b, i: (b, 0, n - 1 - i)
    return pl.pallas_call(
        body, grid=(B, n),
        in_specs=[pl.BlockSpec((None, H, tc), rev), pl.BlockSpec((None, H, tc), rev), pl.BlockSpec((H, 1), lambda b, i: (0, 0))],
        out_specs=[pl.BlockSpec((None, H, tc), rev), pl.BlockSpec((None, H, 1), lambda b, i: (b, 0, 0))],
        out_shape=[jax.ShapeDtypeStruct((B, H, S), F32), jax.ShapeDtypeStruct((B, H, 1), F32)],
        scratch_shapes=[pltpu.VMEM((H, 1), F32)], name="fox_c_bwd",
        compiler_params=_params(("parallel", "arbitrary")),
    )(dc, f8t, bias_col)


NEG_BIG = -1e30


def _fox_logits(q, kj, cq, ckj, i, j, t, scale):
    s = _bdot(q, kj, NT_DIMS) * scale + (cq - ckj)
    row = lax.broadcasted_iota(jnp.int32, (t, t), 0)
    col = lax.broadcasted_iota(jnp.int32, (t, t), 1)
    return s, col <= row + (i - j) * t


def _fox_fwd(qkv, cq, ck, B, t):
    T = qkv.shape[0]
    S = T // B
    n = S // t
    scale = HEAD_DIM ** -0.5

    def body(q_ref, k_ref, v_ref, cq_ref, ck_ref, o_ref, lse_ref):
        i = pl.program_id(1)
        lo = lax.broadcasted_iota(jnp.int32, (t, LANES), 1) < HEAD_DIM
        q = q_ref[...]
        qh = (jnp.where(lo, q, 0), jnp.where(lo, 0, q))

        def step(j, carry):
            rows = pl.ds(pl.multiple_of(j * t, t), t)
            kj, vj = k_ref[rows, :], v_ref[rows, :]
            new = []
            for h in range(2):
                m, l, acc = carry[h]
                s, ok = _fox_logits(qh[h], kj, cq_ref[h], ck_ref[h, j], i, j, t, scale)
                s = jnp.where(ok, s, NEG_BIG)
                m2 = jnp.maximum(m, jnp.max(s, axis=1, keepdims=True))
                p = jnp.exp(s - m2)
                al = jnp.exp(m - m2)
                new.append((m2, al * l + jnp.sum(p, axis=1, keepdims=True), al * acc + _bdot(p, vj)))
            return tuple(new)

        init = tuple((jnp.full((t, 1), NEG_BIG, F32), jnp.zeros((t, 1), F32), jnp.zeros((t, LANES), F32)) for _ in range(2))
        (m0, l0, a0), (m1, l1, a1) = lax.fori_loop(0, i + 1, step, init)
        o_ref[...] = jnp.where(lo, a0 / l0, a1 / l1).astype(o_ref.dtype)
        lse_ref[0] = m0 + jnp.log(l0)
        lse_ref[1] = m1 + jnp.log(l1)

    seq = lambda col0: pl.BlockSpec((S, LANES), lambda g, i: (g // 4, col0 + g % 4))
    blk = lambda col0: pl.BlockSpec((t, LANES), lambda g, i: ((g // 4) * n + i, col0 + g % 4))
    col = pl.BlockSpec((2, None, t, 1), lambda g, i: (g, i, 0, 0))
    return pl.pallas_call(
        body, grid=(B * 4, n), in_specs=[blk(0), seq(4), seq(8), col, pl.BlockSpec((2, n, 1, t), lambda g, i: (g, 0, 0, 0))],
        out_specs=[blk(0), col],
        out_shape=[jax.ShapeDtypeStruct((T, BR_W), BF16), jax.ShapeDtypeStruct(cq.shape, F32)], name="fox_fwd",
        compiler_params=_params(("parallel", "parallel")),
    )(qkv, qkv, qkv, cq, ck)


def _fox_bwd(qkv, o, do, cq, ck, lse, B, t):
    T = qkv.shape[0]
    S = T // B
    n = S // t
    scale = HEAD_DIM ** -0.5

    def body(q_ref, k_ref, v_ref, o_ref, do_ref, cq_ref, ck_ref, lse_ref, dq_ref, dk_ref, dv_ref, dck_ref, dcq_ref,
             dk_acc, dv_acc):
        dk_acc[...] = jnp.zeros(dk_acc.shape, F32)
        dv_acc[...] = jnp.zeros(dv_acc.shape, F32)
        dck_ref[...] = jnp.zeros(dck_ref.shape, F32)
        lo = lax.broadcasted_iota(jnp.int32, (t, LANES), 1) < HEAD_DIM

        def qloop(i, _):
            qrows = pl.ds(pl.multiple_of(i * t, t), t)
            q, do_i, o_i = q_ref[qrows, :], do_ref[qrows, :], o_ref[qrows, :].astype(F32)
            qh = (jnp.where(lo, q, 0), jnp.where(lo, 0, q))
            doh = (jnp.where(lo, do_i, 0), jnp.where(lo, 0, do_i))
            delta = [jnp.sum(doh[h].astype(F32) * o_i, axis=1, keepdims=True) for h in range(2)]

            def kloop(j, carry):
                krows = pl.ds(pl.multiple_of(j * t, t), t)
                kj, vj = k_ref[krows, :], v_ref[krows, :]
                new = []
                for h in range(2):
                    dq, dcq = carry[h]
                    s, ok = _fox_logits(qh[h], kj, cq_ref[h, i], ck_ref[h, j], i, j, t, scale)
                    p = jnp.where(ok, jnp.exp(s - lse_ref[h, i]), 0.0)
                    ds = p * (_bdot(doh[h], vj, NT_DIMS) - delta[h])
                    dv_acc[krows, :] += _bdot(p, doh[h], TN_DIMS)
                    dk_acc[krows, :] += _bdot(ds, qh[h], TN_DIMS) * scale
                    dck_ref[h, j] += -_rowsum(ds)
                    new.append((dq + _bdot(ds, kj) * scale, dcq + jnp.sum(ds, axis=1, keepdims=True)))
                return tuple(new)

            init = tuple((jnp.zeros((t, LANES), F32), jnp.zeros((t, 1), F32)) for _ in range(2))
            (dq0, dcq0), (dq1, dcq1) = lax.fori_loop(0, i + 1, kloop, init)
            dq_ref[qrows, :] = jnp.where(lo, dq0, dq1).astype(dq_ref.dtype)
            dcq_ref[0, i] = dcq0
            dcq_ref[1, i] = dcq1
            return 0

        lax.fori_loop(0, n, qloop, 0)
        dk_ref[...] = dk_acc[...].astype(dk_ref.dtype)
        dv_ref[...] = dv_acc[...].astype(dv_ref.dtype)

    seq = lambda col0: pl.BlockSpec((S, LANES), lambda g: (g // 4, col0 + g % 4))
    col = pl.BlockSpec((2, n, t, 1), lambda g: (g, 0, 0, 0))
    row = pl.BlockSpec((2, n, 1, t), lambda g: (g, 0, 0, 0))
    out = jax.ShapeDtypeStruct((T, BR_W), BF16)
    return pl.pallas_call(
        body, grid=(B * 4,), in_specs=[seq(0), seq(4), seq(8), seq(0), seq(0), col, row, col],
        out_specs=[seq(0), seq(0), seq(0), row, col],
        out_shape=[out, out, out, jax.ShapeDtypeStruct(ck.shape, F32), jax.ShapeDtypeStruct(cq.shape, F32)],
        scratch_shapes=[pltpu.VMEM((S, LANES), F32), pltpu.VMEM((S, LANES), F32)], name="fox_bwd",
        compiler_params=_params(("parallel",)),
    )(qkv, qkv, qkv, o, do, cq, ck, lse)


def _mem_probs(qh, kh):
    s = _bdot(qh, kh, NT_DIMS) * (MEM_HEAD_DIM ** -0.5)
    e = jnp.exp(s - jnp.max(s, axis=1, keepdims=True))
    return e / jnp.sum(e, axis=1, keepdims=True)


def _mem_fwd(q, mem_kv, B, tq=512):
    T = q.shape[0]
    S, Lm = T // B, mem_kv.shape[0] // B
    tq = min(tq, S)
    n = S // tq

    def body(q_ref, k_ref, v_ref, o_ref):
        for h in range(MEM_HEADS):
            sl = slice(h * MEM_HEAD_DIM, (h + 1) * MEM_HEAD_DIM)
            p = _mem_probs(q_ref[:, sl], k_ref[:, sl])
            o_ref[:, sl] = _bdot(p, v_ref[:, sl]).astype(o_ref.dtype)

    qs = pl.BlockSpec((tq, BR_W), lambda b, i: (b * n + i, 0))
    return pl.pallas_call(
        body, grid=(B, n),
        in_specs=[qs, pl.BlockSpec((Lm, BR_W), lambda b, i: (b, 0)), pl.BlockSpec((Lm, BR_W), lambda b, i: (b, 1))],
        out_specs=qs, out_shape=jax.ShapeDtypeStruct((T, BR_W), BF16), name="mem_fwd",
        compiler_params=_params(("parallel", "parallel")),
    )(q, mem_kv, mem_kv)


def _mem_bwd(q, mem_kv, do, B, tq=512):
    T = q.shape[0]
    S, Lm = T // B, mem_kv.shape[0] // B
    tq = min(tq, S)
    n = S // tq
    scale = MEM_HEAD_DIM ** -0.5

    def body(q_ref, k_ref, v_ref, do_ref, dq_ref, dk_ref, dv_ref):
        @pl.when(pl.program_id(1) == 0)
        def _():
            dk_ref[...] = jnp.zeros(dk_ref.shape, F32)
            dv_ref[...] = jnp.zeros(dv_ref.shape, F32)
        for h in range(MEM_HEADS):
            sl = slice(h * MEM_HEAD_DIM, (h + 1) * MEM_HEAD_DIM)
            qh, kh, vh, doh = q_ref[:, sl], k_ref[:, sl], v_ref[:, sl], do_ref[:, sl]
            p = _mem_probs(qh, kh)
            dp = _bdot(doh, vh, NT_DIMS)
            ds = p * (dp - jnp.sum(p * dp, axis=1, keepdims=True))
            dq_ref[:, sl] = (_bdot(ds, kh) * scale).astype(dq_ref.dtype)
            dk_ref[:, sl] += _bdot(ds, qh, TN_DIMS) * scale
            dv_ref[:, sl] += _bdot(p, doh, TN_DIMS)

    qs = pl.BlockSpec((tq, BR_W), lambda b, i: (b * n + i, 0))
    kv = pl.BlockSpec((Lm, BR_W), lambda b, i: (b, 0))
    return pl.pallas_call(
        body, grid=(B, n),
        in_specs=[qs, kv, pl.BlockSpec((Lm, BR_W), lambda b, i: (b, 1)), qs], out_specs=[qs, kv, kv],
        out_shape=[jax.ShapeDtypeStruct((T, BR_W), BF16), jax.ShapeDtypeStruct((B * Lm, BR_W), F32),
                   jax.ShapeDtypeStruct((B * Lm, BR_W), F32)], name="mem_bwd",
        compiler_params=_params(("parallel", "arbitrary")),
    )(q, mem_kv, mem_kv, do)


def _head_ones():
    h = np.arange(BR_W) // HEAD_DIM
    return jnp.asarray((h[:, None] == h[None, :]).astype(np.float32))


def _rw_prep(p, pp, mu, w0, w1, a0, w2, g_up, k_k, k_a, bd):
    ps = p + (pp - p) * mu
    r, k, v = ps[:, 0:512], ps[:, 512:1024], ps[:, 1024:1536]
    wa, gd = ps[:, 1536:1664], ps[:, 1664:1792]
    th = jnp.tanh(wa)
    z = w0 + _bdot(th, w1)
    wl = -jnp.exp(_log_sigmoid(z) - 0.5)
    w = jnp.exp(wl)
    a = _sigmoid(a0 + _bdot(wa, w2))
    sg = _sigmoid(gd)
    g = _bdot(sg, g_up)
    kq = k * k_k
    n2 = _hdot(kq * kq, bd)
    inv = lax.rsqrt(jnp.maximum(n2, 1e-24))
    kk = kq * inv
    k2 = k * (1.0 + (a - 1.0) * k_a)
    return dict(ps=ps, r=r, k=k, v=v, wa=wa, th=th, z=z, wl=wl, w=w, a=a, sg=sg, g=g, kq=kq, n2=n2, inv=inv, kk=kk, k2=k2)


def _keycol_selector(tm, Tc):
    e = np.zeros((tm, (tm // Tc) * LANES), np.float32)
    for t in range(tm):
        c, tl = divmod(t, Tc)
        e[t, c * LANES + tl] = e[t, c * LANES + Tc + tl] = 1.0
    return jnp.asarray(e, BF16)


def _rw_prep_fwd(p, pp, consts, B, Tc, tm=256):
    assert 2 * Tc == LANES
    T = p.shape[0]
    S = T // B
    nb, cpb = S // tm, tm // Tc
    sel = _keycol_selector(tm, Tc)
    nc = len(consts)

    def body(*refs):
        t = _rw_prep(*[r[...] for r in refs[:2 + nc]])
        sel_ref = refs[2 + nc]
        rows, cols = refs[3 + nc:7 + nc], refs[7 + nc:]
        for ref, val in zip(rows, (t["r"], t["k2"], t["v"], t["g"])):
            ref[...] = val
        lo = lax.broadcasted_iota(jnp.int32, (HEAD_DIM, LANES), 1) < HEAD_DIM
        operands = (t["w"], -t["kk"], t["kk"] * t["a"], t["k2"], t["r"])
        for n, (ref, x) in enumerate(zip(cols, operands)):
            terms = _split_bf16(x) if n == 0 else (x.astype(BF16),)
            for hp in range(4):
                xt = sum(lax.dot_general(tt[:, hp * LANES:(hp + 1) * LANES], sel_ref[...], TN_DIMS,
                                         preferred_element_type=F32) for tt in terms)
                for c in range(cpb):
                    blk = xt[:, c * LANES:(c + 1) * LANES]
                    ref[hp, c] = jnp.where(lo, blk[0:HEAD_DIM], blk[HEAD_DIM:2 * HEAD_DIM]).astype(ref.dtype)

    row_spec = lambda w: pl.BlockSpec((tm, w), lambda i: (i, 0))
    col_spec = pl.BlockSpec((None, 4, cpb, HEAD_DIM, LANES), lambda i: (i // nb, 0, i % nb, 0, 0))
    col_shape = lambda dt: jax.ShapeDtypeStruct((B, 4, S // Tc, HEAD_DIM, LANES), dt)
    out = pl.pallas_call(
        body, grid=(T // tm,),
        in_specs=[row_spec(RWKV_COLS)] * 2 + [pl.BlockSpec(c.shape, lambda i: (0, 0)) for c in consts]
        + [pl.BlockSpec(sel.shape, lambda i: (0, 0))],
        out_specs=[row_spec(BR_W)] * 4 + [col_spec] * 5,
        out_shape=[jax.ShapeDtypeStruct((T, BR_W), F32)] * 4 + [col_shape(F32)] + [col_shape(BF16)] * 4,
        name="rwkv_prep_fwd", compiler_params=_params(("parallel",)),
    )(p, pp, *consts, sel)
    return out[:4], [c.reshape(B * 4, S // Tc, HEAD_DIM, LANES) for c in out[4:]]


def _rw_prep_bwd(p, pp, cots, consts):
    def fn(p, pp, dr1, dr2, dw, dk21, dk22, dv1, dv2, dav, dbv, dg, mu, w0, w1, a0, w2, g_up, k_k, k_a, bd):
        t = _rw_prep(p, pp, mu, w0, w1, a0, w2, g_up, k_k, k_a, bd)
        dr, dk2, dv = dr1 + dr2, dk21 + dk22, dv1 + dv2
        a, k, kk, kq, inv = t["a"], t["k"], t["kk"], t["kq"], t["inv"]
        dkk = dbv * a - dav
        da = dbv * kk + dk2 * k * k_a
        dk = dk2 * (1.0 + (a - 1.0) * k_a)
        d_k_a = _rowsum(dk2 * k * (a - 1.0))
        proj = _hdot(dkk * kq, bd)
        dkq = dkk * inv - jnp.where(t["n2"] > 1e-24, kq * inv * inv * inv * proj, 0.0)
        dk = dk + dkq * k_k
        d_k_k = _rowsum(dkq * k)
        dpa = da * a * (1.0 - a)
        d_a0 = _rowsum(dpa)
        dwa = _bdot(dpa, w2, NT_DIMS)
        d_w2 = _bdot(t["wa"], dpa, TN_DIMS)
        dz = dw * t["w"] * t["wl"] * (1.0 - _sigmoid(t["z"]))
        d_w0 = _rowsum(dz)
        th = t["th"]
        dwa = dwa + _bdot(dz, w1, NT_DIMS) * (1.0 - th * th)
        d_w1 = _bdot(th, dz, TN_DIMS)
        sg = t["sg"]
        dgd = _bdot(dg, g_up, NT_DIMS) * sg * (1.0 - sg)
        d_g_up = _bdot(sg, dg, TN_DIMS)
        dps = jnp.concatenate([dr, dk, dv, dwa, dgd], axis=1)
        d_mu = _rowsum(dps * (pp - p))
        return dps * (1.0 - mu), dps * mu, d_mu, d_w0, d_w1, d_a0, d_w2, d_g_up, d_k_k, d_k_a

    accs = [((1, RWKV_COLS), F32), ((1, BR_W), F32), ((LANES, BR_W), F32), ((1, BR_W), F32), ((LANES, BR_W), F32),
            ((LANES, BR_W), F32), ((1, BR_W), F32), ((1, BR_W), F32)]
    return _rowwise("rwkv_prep_bwd", fn, [p, pp] + list(cots), consts, [(RWKV_COLS, F32)] * 2, accs, tm=128)


def _rw_head(y, r, k2, v, g, gn_g, gn_b, r_k, bd):
    mean = _hdot(y, bd) * (1.0 / HEAD_DIM)
    yc = y - mean
    rs = lax.rsqrt(_hdot(yc * yc, bd) * (1.0 / HEAD_DIM) + GN_EPS)
    yn = yc * rs
    bs = _hdot(r * k2 * r_k, bd)
    return yn, rs, bs, yn * gn_g + gn_b + bs * v


def _rw_head_fwd(y, r, k2, v, g, consts):
    def fn(y, r, k2, v, g, *c):
        return _rw_head(y, r, k2, v, g, *c)[3] * g
    return _rowwise("rwkv_head_fwd", fn, [y, r, k2, v, g], consts, [(BR_W, BF16)])[0]


def _rw_head_bwd(dout, y, r, k2, v, g, consts):
    def fn(dout, y, r, k2, v, g, gn_g, gn_b, r_k, bd):
        dout = dout.astype(F32)
        yn, rs, bs, zz = _rw_head(y, r, k2, v, g, gn_g, gn_b, r_k, bd)
        dg = dout * zz
        dz = dout * g
        dyn = dz * gn_g
        inv_n = 1.0 / HEAD_DIM
        dy = rs * (dyn - _hdot(dyn, bd) * inv_n - yn * (_hdot(dyn * yn, bd) * inv_n))
        dq = _hdot(dz * v, bd)
        return dy, dg, dq * k2 * r_k, dq * r * r_k, dz * bs, _rowsum(dz * yn), _rowsum(dz), _rowsum(dq * r * k2)
    return _rowwise("rwkv_head_bwd", fn, [dout, y, r, k2, v, g], consts, [(BR_W, F32)] * 5, [((1, BR_W), F32)] * 3)


SCAN_TC = 64


def _scan_onehot(Tc):
    w = np.zeros((Tc // 2, 2 * Tc, 2 * LANES), np.float32)
    for tt in range(Tc // 2):
        for u in range(2):
            for h in range(2):
                w[tt, h * Tc + 2 * tt + u, u * LANES + h * HEAD_DIM: u * LANES + (h + 1) * HEAD_DIM] = 1.0
    return jnp.asarray(w, BF16)


def _split_bf16(x):
    hi = x.astype(BF16)
    return hi, (x - hi.astype(F32)).astype(BF16)


def _key_tiles(l_w, others, onehot):
    dot = lambda x: jnp.dot(x, onehot, preferred_element_type=F32)
    whi, wmid = l_w
    return [dot(whi) + dot(wmid)] + [dot(o) for o in others]


def _rw_scan_fwd(LW, LA, LB, LK, LR, v, P=2):
    NP, nc, _, Tc2 = LW.shape
    Tc = Tc2 // 2
    S = nc * Tc
    onehot = _scan_onehot(Tc)
    npb = 4 // P

    def body(lw, la, lb, lk, lr, v_ref, oh_ref, y_ref, sa_ref, sb_ref, st):
        @pl.when(pl.program_id(1) == 0)
        def _():
            st[...] = jnp.zeros(st.shape, F32)
        s = [st[p] for p in range(P)]
        cols = [(_split_bf16(lw[p]), [ref[p].astype(BF16) for ref in (la, lb, lk, lr)]) for p in range(P)]
        for tt in range(Tc // 2):
            tiles = [_key_tiles(c[0], c[1], oh_ref[tt]) for c in cols]
            for u in range(2):
                t = 2 * tt + u
                for p in range(P):
                    W, A, Bt, Kt, R = (x[:, u * LANES:(u + 1) * LANES] for x in tiles[p])
                    ls = slice(p * LANES, (p + 1) * LANES)
                    sb_ref[p, t] = s[p]
                    sa = _rowsum(s[p] * A)
                    s[p] = s[p] * W + Bt * sa + Kt * v_ref[t:t + 1, ls]
                    y_ref[t:t + 1, ls] = _rowsum(s[p] * R)
                    sa_ref[t:t + 1, ls] = sa
        for p in range(P):
            st[p] = s[p]

    lspec = pl.BlockSpec((P, None, HEAD_DIM, Tc2), lambda g, c: (g, c, 0, 0))
    rows = pl.BlockSpec((Tc, P * LANES), lambda g, c: ((g // npb) * nc + c, g % npb))
    rowshape = jax.ShapeDtypeStruct(v.shape, F32)
    return pl.pallas_call(
        body, grid=(NP // P, nc), in_specs=[lspec] * 5 + [rows, pl.BlockSpec(onehot.shape, lambda g, c: (0, 0, 0))],
        out_specs=[rows, rows, pl.BlockSpec((P, Tc, HEAD_DIM, LANES), lambda g, c: (g, c, 0, 0))],
        out_shape=[rowshape, rowshape, jax.ShapeDtypeStruct((NP, S, HEAD_DIM, LANES), F32)],
        scratch_shapes=[pltpu.VMEM((P, HEAD_DIM, LANES), F32)], name="rwkv_scan_fwd",
        compiler_params=_params(("parallel", "arbitrary")),
    )(LW, LA, LB, LK, LR, v, onehot)


SCAN_G_ROWS = 16


def _rw_scan_bwd(LW, LA, LB, LK, LR, v, sa, dy, sb, P=4):
    NP, nc, _, Tc2 = LW.shape
    Tc = Tc2 // 2
    onehot = _scan_onehot(Tc)
    npb = 4 // P

    def body(lw, la, lb, lk, lr, v_ref, sa_ref, dy_ref, sb_ref, oh_ref, dv_ref, dk_ref, db_ref, dw_ref, dr_ref, da_ref, dst):
        @pl.when(pl.program_id(1) == 0)
        def _():
            dst[...] = jnp.zeros(dst.shape, F32)
        rid = lax.broadcasted_iota(jnp.int32, (SCAN_G_ROWS, LANES), 0)
        lane = lax.broadcasted_iota(jnp.int32, (SCAN_G_ROWS, LANES), 1)
        own = (((rid % 2) == 0) == (lane < HEAD_DIM)) & (rid < 10)
        lo = lane[0:1] < HEAD_DIM
        nt = lambda rows, tile: lax.dot_general(rows.astype(BF16), tile.astype(BF16), NT_DIMS, preferred_element_type=F32)
        ds = [dst[p] for p in range(P)]
        cols = [(_split_bf16(lw[p]), [ref[p].astype(BF16) for ref in (la, lb, lk, lr)]) for p in range(P)]
        for tt in reversed(range(Tc // 2)):
            tiles = [_key_tiles(c[0], c[1], oh_ref[tt]) for c in cols]
            for u in (1, 0):
                t = 2 * tt + u
                for p in range(P):
                    W, A, Bt, Kt, R = (x[:, u * LANES:(u + 1) * LANES] for x in tiles[p])
                    ls = slice(p * LANES, (p + 1) * LANES)
                    vr, sar, dyr = (ref[t:t + 1, ls] for ref in (v_ref, sa_ref, dy_ref))
                    sp = sb_ref[p, t]
                    s_t = sp * W + Bt * sar + Kt * vr
                    d = ds[p] + R * dyr
                    dv_ref[t:t + 1, ls] = _rowsum(d * Kt)
                    dsar = _rowsum(d * Bt)
                    rows = jnp.where(rid < 2, vr, jnp.where(rid < 4, sar, jnp.where(rid < 6, 1.0, jnp.where(rid < 8, dyr, dsar))))
                    g = nt(jnp.where(own, rows, 0.0), jnp.concatenate([d, d * sp, s_t, sp], axis=0))
                    ga, gb = g[:, 0:LANES], g[:, LANES:2 * LANES]
                    ra, rb = pltpu.roll(ga, HEAD_DIM, 1), pltpu.roll(gb, HEAD_DIM, 1)
                    dk_ref[t:t + 1, ls] = jnp.where(lo, ga[0:1], ra[1:2])
                    db_ref[t:t + 1, ls] = jnp.where(lo, ga[2:3], ra[3:4])
                    dw_ref[t:t + 1, ls] = jnp.where(lo, ra[4:5], ga[5:6])
                    dr_ref[t:t + 1, ls] = jnp.where(lo, gb[6:7], rb[7:8])
                    da_ref[t:t + 1, ls] = jnp.where(lo, rb[8:9], gb[9:10])
                    ds[p] = d * W + A * dsar
        for p in range(P):
            dst[p] = ds[p]

    rev = lambda g, c: (g, nc - 1 - c, 0, 0)
    lspec = pl.BlockSpec((P, None, HEAD_DIM, Tc2), rev)
    rows = pl.BlockSpec((Tc, P * LANES), lambda g, c: ((g // npb) * nc + nc - 1 - c, g % npb))
    return pl.pallas_call(
        body, grid=(NP // P, nc),
        in_specs=[lspec] * 5 + [rows] * 3 + [pl.BlockSpec((P, Tc, HEAD_DIM, LANES), rev),
                                             pl.BlockSpec(onehot.shape, lambda g, c: (0, 0, 0))],
        out_specs=[rows] * 6, out_shape=[jax.ShapeDtypeStruct(v.shape, F32)] * 6,
        scratch_shapes=[pltpu.VMEM((P, HEAD_DIM, LANES), F32)], name="rwkv_scan_bwd",
        compiler_params=_params(("parallel", "arbitrary")),
    )(LW, LA, LB, LK, LR, v, sa, dy, sb, onehot)


def _shift_prev(p, B):
    T, W = p.shape
    return jnp.pad(p.reshape(B, T // B, W), ((0, 0), (1, 0), (0, 0)))[:, :-1].reshape(T, W)


def _shift_next(p, B):
    T, W = p.shape
    return jnp.pad(p.reshape(B, T // B, W), ((0, 0), (0, 1), (0, 0)))[:, 1:].reshape(T, W)


def _layer_step(x, mem, target, W, scan_tc=SCAN_TC, fox_t=256):
    B, S, _ = x.shape
    T = B * S
    x2, tgt2 = x.reshape(T, D_MODEL), target.reshape(T, D_MODEL)
    mem2 = mem.reshape(-1, D_MODEL)
    w_in_t = W["w_in"]
    wt_qkv, wt_rw, wt_mq, wt_gate = (w_in_t[lo:hi] for lo, hi in (COL_QKV, COL_RW, COL_MQ, COL_GATE))
    wt_f = jnp.pad(w_in_t[COL_F[0]:COL_F[1]], ((0, LANES - N_HEADS), (0, 0)))
    row = lambda v: v.reshape(1, -1).astype(F32)
    pre1_g, post1_g, pre2_g, post2_g, mem_g = (row(W[n]) for n in ("pre1_g", "post1_g", "pre2_g", "post2_g", "mem_norm_g"))

    u = _rowwise("rms_pre1", lambda x, g: x * _rms_stat(x) * g, [x2], [pre1_g], [(D_MODEL, BF16)])[0]
    qkv = _mm("proj_qkv", u, wt_qkv, tb=True, out_dtype=BF16)
    f_pad = _mm("proj_f", u, wt_f, tb=True)
    p_rw = _mm("proj_rwkv", u, wt_rw, tb=True)
    memq = _mm("proj_memq", u, wt_mq, tb=True, out_dtype=BF16)
    gate = _mm("proj_gate", u, wt_gate, tb=True)

    t = min(fox_t, S)
    bias_col = W["fox_f_bias"].reshape(N_HEADS, 1).astype(F32)
    f8t = f_pad[:, :N_HEADS].reshape(B, S, N_HEADS).transpose(0, 2, 1)
    c = _fox_c_fwd(f8t, bias_col)
    G = B * N_HEADS
    cq, ck = c.reshape(G, S // t, t, 1), c.reshape(G, S // t, 1, t)
    fox_out, lse = _fox_fwd(qkv, cq, ck, B, t)

    bd = _head_ones()
    zpad = jnp.zeros((64, BR_W), F32)
    w1 = jnp.concatenate([W["rwkv_w_up"].astype(F32), zpad], axis=0)
    w2 = jnp.concatenate([zpad, W["rwkv_a_up"].astype(F32)], axis=0)
    prep_consts = [row(W["rwkv_mu"]), row(W["rwkv_w0"]), w1, row(W["rwkv_a0"]), w2, W["rwkv_g_up"].astype(F32),
                   row(W["rwkv_k_k"]), row(W["rwkv_k_a"]), bd]
    p_prev = _shift_prev(p_rw, B)
    (rr, rk2, rv, rg), scan_cols = _rw_prep_fwd(p_rw, p_prev, prep_consts, B, scan_tc)
    ry, rsa, sb = _rw_scan_fwd(*scan_cols, rv)
    head_consts = [row(W["rwkv_gn_g"]), row(W["rwkv_gn_b"]), row(W["rwkv_r_k"]), bd]
    rwkv_out = _rw_head_fwd(ry, rr, rk2, rv, rg, head_consts)

    mn = _rowwise("rms_mem", lambda m, g: m * _rms_stat(m) * g, [mem2], [mem_g], [(D_MODEL, BF16)])[0]
    mem_kv = _mm("proj_memkv", mn, W["w_mem_kv"], out_dtype=BF16)
    mem_out = _mem_fwd(memq, mem_kv, B)

    fo = [_mm("branch_" + n, a, W[n], tb=True)
          for n, a in (("w_fox_out", fox_out), ("w_rwkv_out", rwkv_out), ("w_mem_out", mem_out))]

    def merge(gate, f0, f1, f2):
        return sum(_sigmoid(gate[:, i * D_MODEL:(i + 1) * D_MODEL]) * f for i, f in enumerate((f0, f1, f2)))
    merged = _rowwise("merge", merge, [gate] + fo, [], [(D_MODEL, BF16)])[0]
    y1 = _mm("proj_o", merged, W["w_o"])

    def mid(x, y1, g1, g2):
        h1 = x + y1 * _rms_stat(y1) * g1
        return h1, h1 * _rms_stat(h1) * g2
    h1, u2 = _rowwise("norm_mid", mid, [x2, y1], [post1_g, pre2_g], [(D_MODEL, F32), (D_MODEL, BF16)])
    gt = _mm("ffn_gate", u2, W["w_ffn_gate"], tb=True)
    up = _mm("ffn_up", u2, W["w_ffn_up"], tb=True)
    act = _rowwise("swiglu", lambda gt, up: gt * _sigmoid(gt) * up, [gt, up], [], [(D_FF, BF16)])[0]
    ffn = _mm("ffn_down", act, W["w_ffn_down"])

    def tail(h1, ffn, tgt, g):
        err = h1 + ffn * _rms_stat(ffn) * g - tgt
        dh2 = err * (1.0 / D_MODEL)
        dffn, dg = _rms_bwd(dh2, ffn, g)
        loss = 0.5 * jnp.sum(jnp.sum(err * err, axis=1, keepdims=True) * (1.0 / D_MODEL), axis=0, keepdims=True)
        return dh2, dffn, dg, jnp.broadcast_to(loss, (1, LANES))
    dh2, dffn, d_post2, loss = _rowwise("loss_tail", tail, [h1, ffn, tgt2], [post2_g], [(D_MODEL, F32), (D_MODEL, BF16)],
                                        [((1, D_MODEL), F32), ((1, LANES), F32)])
    gw = {"post2_g": d_post2}
    dact = _mm("d_act", dffn, W["w_ffn_down"], tb=True)
    gw["w_ffn_down"] = _mm("g_ffn_down", act, dffn, ta=True, out_dtype=BF16)

    def swiglu_bwd(dact, gt, up):
        s = _sigmoid(gt)
        return dact * up * s * (1.0 + gt * (1.0 - s)), dact * gt * s
    dgt, dup = _rowwise("swiglu_bwd", swiglu_bwd, [dact, gt, up], [], [(D_FF, BF16)] * 2)
    du2 = _mm("d_u2_gate", dgt, W["w_ffn_gate"])
    du2 = _mm("d_u2_up", dup, W["w_ffn_up"], add=du2)
    gw["w_ffn_gate"] = _mm("g_ffn_gate", dgt, u2, ta=True, out_dtype=BF16)
    gw["w_ffn_up"] = _mm("g_ffn_up", dup, u2, ta=True, out_dtype=BF16)

    def mid_bwd(du2, dh2, h1, y1, g1, g2):
        dh1_n, d_pre2 = _rms_bwd(du2, h1, g2)
        dh1 = dh2 + dh1_n
        dy1, d_post1 = _rms_bwd(dh1, y1, g1)
        return dh1, dy1, d_post1, d_pre2
    dh1, dy1, gw["post1_g"], gw["pre2_g"] = _rowwise(
        "norm_mid_bwd", mid_bwd, [du2, dh2, h1, y1], [post1_g, pre2_g], [(D_MODEL, F32), (D_MODEL, BF16)],
        [((1, D_MODEL), F32)] * 2)
    dmerged = _mm("d_merged", dy1, W["w_o"], tb=True)
    gw["w_o"] = _mm("g_w_o", merged, dy1, ta=True, out_dtype=BF16)

    def merge_bwd(dm, gate, f0, f1, f2):
        s = [_sigmoid(gate[:, i * D_MODEL:(i + 1) * D_MODEL]) for i in range(3)]
        dgate = jnp.concatenate([dm * f * si * (1.0 - si) for f, si in zip((f0, f1, f2), s)], axis=1)
        return dm * s[0], dm * s[1], dm * s[2], dgate
    dfo0, dfo1, dfo2, dgate = _rowwise("merge_bwd", merge_bwd, [dmerged, gate] + fo, [],
                                       [(D_MODEL, BF16)] * 3 + [(3 * D_MODEL, BF16)])
    d_branch = {}
    for n, a, dfo in (("w_fox_out", fox_out, dfo0), ("w_rwkv_out", rwkv_out, dfo1), ("w_mem_out", mem_out, dfo2)):
        d_branch[n] = _mm("d_in_" + n, dfo, W[n], out_dtype=BF16)
        gw[n] = _mm("g_" + n, dfo, a, ta=True, out_dtype=BF16)

    dmemq, dkm, dvm = _mem_bwd(memq, mem_kv, d_branch["w_mem_out"], B)
    dmem_kv = jnp.concatenate([dkm, dvm], axis=1)
    gw["w_mem_kv"] = _mm("g_w_mem_kv", mn, dmem_kv, ta=True, out_dtype=BF16)
    dmn = _mm("d_mn", dmem_kv, W["w_mem_kv"], tb=True)
    gw["mem_norm_g"] = _rowwise("rms_mem_bwd", lambda d, m, g: _rms_bwd(d, m, g)[1], [dmn, mem2], [mem_g], [],
                                [((1, D_MODEL), F32)])[0]

    dfq, dfk, dfv, dck, dcq = _fox_bwd(qkv, fox_out, d_branch["w_fox_out"], cq, ck, lse, B, t)
    df8t, dbias = _fox_c_bwd(dck.reshape(B, N_HEADS, S) + dcq.reshape(B, N_HEADS, S), f8t, bias_col)
    gw["fox_f_bias"] = jnp.sum(dbias, axis=0).reshape(1, N_HEADS)
    dqkv = jnp.concatenate([dfq, dfk, dfv], axis=1)
    df_pad = jnp.pad(df8t.transpose(0, 2, 1).reshape(T, N_HEADS), ((0, 0), (0, LANES - N_HEADS))).astype(BF16)

    dry, drg, dr_h, dk2_h, dv_h, gw["rwkv_gn_g"], gw["rwkv_gn_b"], gw["rwkv_r_k"] = _rw_head_bwd(
        d_branch["w_rwkv_out"], ry, rr, rk2, rv, rg, head_consts)
    dv_s, dk2_s, db_s, dw_s, dr_s, da_s = _rw_scan_bwd(*scan_cols, rv, rsa, dry, sb)
    dP, dPp, gw["rwkv_mu"], gw["rwkv_w0"], d_w1, gw["rwkv_a0"], d_w2, gw["rwkv_g_up"], gw["rwkv_k_k"], gw["rwkv_k_a"] = \
        _rw_prep_bwd(p_rw, p_prev, [dr_s, dr_h, dw_s, dk2_s, dk2_h, dv_s, dv_h, da_s, db_s, drg], prep_consts)
    gw["rwkv_w_up"], gw["rwkv_a_up"] = d_w1[:64], d_w2[64:]
    dp_rw = (dP + _shift_next(dPp, B)).astype(BF16)

    du = _mm("d_u_qkv", dqkv, wt_qkv)
    du = _mm("d_u_f", df_pad, wt_f, add=du)
    du = _mm("d_u_rwkv", dp_rw, wt_rw, add=du)
    du = _mm("d_u_memq", dmemq, wt_mq, add=du)
    du = _mm("d_u_gate", dgate, wt_gate, add=du)
    gw["w_in"] = jnp.concatenate(
        [_mm("g_w_qkv", dqkv, u, ta=True, out_dtype=BF16), _mm("g_w_f", df_pad, u, ta=True, out_dtype=BF16)[:N_HEADS],
         _mm("g_w_rwkv", dp_rw, u, ta=True, out_dtype=BF16), _mm("g_w_memq", dmemq, u, ta=True, out_dtype=BF16),
         _mm("g_w_gate", dgate, u, ta=True, out_dtype=BF16)], axis=0)

    def pre1_bwd(du, dh1, x, g):
        dx, dg = _rms_bwd(du, x, g)
        return dh1 + dx, dg
    dx, gw["pre1_g"] = _rowwise("rms_pre1_bwd", pre1_bwd, [du, dh1, x2], [pre1_g], [(D_MODEL, F32)], [((1, D_MODEL), F32)])
    return loss[0, 0], dx.reshape(B, S, D_MODEL), gw


TRANSPOSED = ("w_in", "w_ffn_gate", "w_ffn_up", "w_fox_out", "w_rwkv_out", "w_mem_out")
LORA = ("rwkv_w_up", "rwkv_a_up", "rwkv_g_up")
ROW_SHARDED = ("w_mem_kv", "w_o", "w_ffn_down")
BIG = ("w_in", "w_ffn_gate", "w_ffn_up", "w_mem_kv", "w_o", "w_ffn_down", "w_fox_out", "w_rwkv_out", "w_mem_out") + LORA
SMALL = ("pre1_g", "post1_g", "pre2_g", "post2_g", "mem_norm_g", "fox_f_bias", "rwkv_mu", "rwkv_w0", "rwkv_a0", "rwkv_k_k",
         "rwkv_k_a", "rwkv_r_k", "rwkv_gn_g", "rwkv_gn_b")
WEIGHTS = ("pre1_g", "post1_g", "pre2_g", "post2_g", "mem_norm_g", "w_in", "fox_f_bias", "rwkv_mu", "rwkv_w0", "rwkv_w_up",
           "rwkv_a0", "rwkv_a_up", "rwkv_g_up", "rwkv_k_k", "rwkv_k_a", "rwkv_r_k", "rwkv_gn_g", "rwkv_gn_b", "w_mem_kv",
           "w_fox_out", "w_rwkv_out", "w_mem_out", "w_o", "w_ffn_gate", "w_ffn_up", "w_ffn_down")
WIRE_W = 1024
WIRE_ROW_ALIGN = 16
WIRE_HALF_ALIGN = 240


def _wire_rows(name, shard_shape):
    r, c = shard_shape
    if name in ROW_SHARDED:
        return r
    return -(-c // WIRE_ROW_ALIGN) * WIRE_ROW_ALIGN if r == WIRE_W else (r * c) // WIRE_W


def _to_wire(name, a):
    if name not in ROW_SHARDED:
        a = jnp.swapaxes(a, -1, -2)
    lead, (n, w) = a.shape[:-2], a.shape[-2:]
    if w != WIRE_W:
        return a.reshape(lead + ((n * w) // WIRE_W, WIRE_W))
    return jnp.pad(a, [(0, 0)] * len(lead) + [(0, (-n) % WIRE_ROW_ALIGN), (0, 0)])


def _from_wire(name, a, shard_shape):
    r, c = shard_shape
    if name in ROW_SHARDED:
        return a
    return a[..., :c, :] if r == WIRE_W else a.reshape(a.shape[:-2] + (c, r))


def _wire_layout(shard_shapes):
    layout, off = {}, 0
    for n in BIG:
        rows = _wire_rows(n, shard_shapes[n])
        layout[n] = (off, rows)
        off += rows
    return layout, -(-off // (2 * WIRE_HALF_ALIGN)) * WIRE_HALF_ALIGN


def _wire_pack(blocks, half_rows):
    a = jnp.concatenate(blocks, axis=-2)
    lead = a.shape[:-2]
    a = jnp.pad(a, [(0, 0)] * len(lead) + [(0, 2 * half_rows - a.shape[-2]), (0, 0)])
    return a.reshape(lead + (2, half_rows, WIRE_W))


def _my_place():
    return lax.axis_index("x"), lax.axis_index("y"), lax.axis_index("c")


def _other_chips(x, y):
    return [(1 - x, y), (x, 1 - y), (1 - x, 1 - y)]


ANY = pl.BlockSpec(memory_space=pl.ANY)


def _gather_weights(packed):
    _, R, L = packed.shape
    me = 2 * lax.axis_index("x") + lax.axis_index("y")
    base = lax.dynamic_update_index_in_dim(jnp.zeros((N_CHIPS, 2, R, L), packed.dtype), packed, me, 0)

    def body(in_ref, base_ref, out_ref, send_sems, recv_sems):
        x, y, c = _my_place()
        chip = lambda px, py: 2 * px + py
        sibling = (x, y, 1 - c)
        others = _other_chips(x, y)

        def copy(k, src, dst, to):
            return pltpu.make_async_remote_copy(src_ref=src, dst_ref=dst, send_sem=send_sems.at[k], recv_sem=recv_sems.at[k],
                                                device_id=to, device_id_type=MESH)

        sends = [copy(j, in_ref.at[c], out_ref.at[chip(x, y), c], (px, py, c)) for j, (px, py) in enumerate(others)]
        for cp in sends:
            cp.start()
        passed = [copy(3 + j, out_ref.at[chip(px, py), c], out_ref.at[chip(px, py), c], sibling)
                  for j, (px, py) in enumerate(others)]
        for j, (px, py) in enumerate(others):
            copy(j, in_ref.at[c], out_ref.at[chip(px, py), c], (px, py, c)).wait_recv()
            passed[j].start()
        for j, (px, py) in enumerate(others):
            copy(3 + j, in_ref.at[1 - c], out_ref.at[chip(px, py), 1 - c], sibling).wait_recv()
        for cp in sends + passed:
            cp.wait_send()

    return pl.pallas_call(
        body, out_shape=jax.ShapeDtypeStruct(base.shape, base.dtype), in_specs=[ANY, ANY], out_specs=ANY,
        input_output_aliases={1: 0}, scratch_shapes=[pltpu.SemaphoreType.DMA((6,)), pltpu.SemaphoreType.DMA((6,))],
        name="gather_weights",
    )(packed, base)


def _pair_exchange(parts):
    n, _, R, L = parts.shape

    def body(in_ref, out_ref, send_sems, recv_sems):
        x, y, c = _my_place()
        copies = [pltpu.make_async_remote_copy(src_ref=in_ref.at[s, 1 - c], dst_ref=out_ref.at[s], send_sem=send_sems.at[s],
                                               recv_sem=recv_sems.at[s], device_id=(x, y, 1 - c), device_id_type=MESH)
                  for s in range(n)]
        for cp in copies:
            cp.start()
        for cp in copies:
            cp.wait()

    return pl.pallas_call(
        body, out_shape=jax.ShapeDtypeStruct((n, R, L), parts.dtype), in_specs=[ANY], out_specs=ANY,
        scratch_shapes=[pltpu.SemaphoreType.DMA((n,)), pltpu.SemaphoreType.DMA((n,))], name="pair_exchange",
    )(parts)


def _pair_add(a, b):
    n, R, L = a.shape
    tr = _tile_rows(R, WIRE_HALF_ALIGN)

    def body(a_ref, b_ref, o_ref):
        o_ref[...] = (a_ref[...].astype(F32) + b_ref[...].astype(F32)).astype(o_ref.dtype)

    spec = pl.BlockSpec((n, tr, L), lambda i: (0, i, 0))
    return pl.pallas_call(
        body, grid=(R // tr,), in_specs=[spec, spec], out_specs=spec, out_shape=jax.ShapeDtypeStruct(a.shape, a.dtype),
        name="pair_add", compiler_params=_params(("parallel",)),
    )(a, b)


def _scatter_grads(parts):
    n, R, L = parts.shape

    def body(in_ref, out_ref, send_sems, recv_sems):
        x, y, c = _my_place()
        copies = [pltpu.make_async_remote_copy(src_ref=in_ref.at[2 * px + py], dst_ref=out_ref.at[j], send_sem=send_sems.at[j],
                                               recv_sem=recv_sems.at[j], device_id=(px, py, c), device_id_type=MESH)
                  for j, (px, py) in enumerate(_other_chips(x, y))]
        for cp in copies:
            cp.start()
        for cp in copies:
            cp.wait()

    return pl.pallas_call(
        body, out_shape=jax.ShapeDtypeStruct((n - 1, R, L), parts.dtype), in_specs=[ANY], out_specs=ANY,
        scratch_shapes=[pltpu.SemaphoreType.DMA((3,)), pltpu.SemaphoreType.DMA((3,))], name="scatter_grads",
    )(parts)


def _sum_chips(own, others):
    n, R, L = others.shape
    tr = _tile_rows(R, WIRE_HALF_ALIGN)

    def body(a_ref, p_ref, o_ref):
        acc = a_ref[...].astype(F32)
        for i in range(n):
            acc = acc + p_ref[i].astype(F32)
        o_ref[...] = acc

    return pl.pallas_call(
        body, grid=(R // tr,), in_specs=[pl.BlockSpec((tr, L), lambda i: (i, 0)), pl.BlockSpec((n, tr, L), lambda i: (0, i, 0))],
        out_specs=pl.BlockSpec((tr, L), lambda i: (i, 0)), out_shape=jax.ShapeDtypeStruct((R, L), F32), name="sum_grads",
        compiler_params=_params(("parallel",)),
    )(own, others)


def _tile_rows(R, cap=2048):
    best = 8
    for t in range(8, min(R, cap) + 1, 8):
        if R % t == 0:
            best = t
    return best if R % 8 == 0 else R


def _swap_halves(half):
    def body(in_ref, out_ref, send_sem, recv_sem):
        x, y, c = _my_place()
        cp = pltpu.make_async_remote_copy(src_ref=in_ref, dst_ref=out_ref, send_sem=send_sem, recv_sem=recv_sem,
                                          device_id=(x, y, 1 - c), device_id_type=MESH)
        cp.start()
        cp.wait()

    return pl.pallas_call(
        body, out_shape=jax.ShapeDtypeStruct(half.shape, half.dtype), in_specs=[ANY], out_specs=ANY,
        scratch_shapes=[pltpu.SemaphoreType.DMA, pltpu.SemaphoreType.DMA], name="swap_halves",
    )(half)


def _allreduce_small(v):
    R, L = v.shape

    def body(in_ref, out_ref, buf, send_sems, recv_sems):
        x, y, c = _my_place()
        me = 4 * x + 2 * y + c
        buf[me] = in_ref[...]
        started = []
        for k in range(1, N_DEV):
            to = (x ^ (k >> 2), y ^ ((k >> 1) & 1), c ^ (k & 1))
            cp = pltpu.make_async_remote_copy(src_ref=in_ref, dst_ref=buf.at[me], send_sem=send_sems.at[k - 1],
                                              recv_sem=recv_sems.at[k - 1], device_id=to, device_id_type=MESH)
            cp.start()
            started.append(cp)
        for cp in started:
            cp.wait()
        acc = buf[0]
        for i in range(1, N_DEV):
            acc = acc + buf[i]
        out_ref[...] = acc

    vm = pl.BlockSpec(memory_space=pltpu.VMEM)
    return pl.pallas_call(
        body, out_shape=jax.ShapeDtypeStruct((R, L), F32), in_specs=[vm], out_specs=vm,
        scratch_shapes=[pltpu.VMEM((N_DEV, R, L), F32), pltpu.SemaphoreType.DMA((7,)), pltpu.SemaphoreType.DMA((7,))],
        name="allreduce_small",
    )(v)


def _adamw(name, w, g, m, v):
    shape = w.shape
    C = shape[-1]
    R = int(np.prod(shape[:-1]))
    args = [a.reshape(R, C).astype(F32) for a in (w, g, m, v)]
    tr = _tile_rows(R, 256)

    def body(w_ref, g_ref, m_ref, v_ref, d_ref, nm_ref, nv_ref):
        g = g_ref[...]
        m = ADAM_B1 * m_ref[...] + (1.0 - ADAM_B1) * g
        v = ADAM_B2 * v_ref[...] + (1.0 - ADAM_B2) * (g * g)
        m_hat = m / (1.0 - ADAM_B1 ** ADAM_STEP)
        v_hat = v / (1.0 - ADAM_B2 ** ADAM_STEP)
        d_ref[...] = -ADAM_LR * (m_hat / (jnp.sqrt(v_hat) + ADAM_EPS) + ADAM_WD * w_ref[...])
        nm_ref[...] = m
        nv_ref[...] = v

    spec = pl.BlockSpec((tr, C), lambda i: (i, 0))
    out = pl.pallas_call(
        body, grid=(R // tr,), in_specs=[spec] * 4, out_specs=[spec] * 3,
        out_shape=[jax.ShapeDtypeStruct((R, C), F32)] * 3, name="adamw_" + name, compiler_params=_params(("parallel",)),
    )(*args)
    return [o.reshape(shape) for o in out]


def kernel(x, mem, pre1_g, post1_g, pre2_g, post2_g, mem_norm_g, w_in, fox_f_bias, rwkv_mu, rwkv_w0, rwkv_w_up, rwkv_a0, rwkv_a_up, rwkv_g_up, rwkv_k_k, rwkv_k_a, rwkv_r_k, rwkv_gn_g, rwkv_gn_b, w_mem_kv, w_fox_out, w_rwkv_out, w_mem_out, w_o, w_ffn_gate, w_ffn_up, w_ffn_down, loss_target, m_pre1_g, m_post1_g, m_pre2_g, m_post2_g, m_mem_norm_g, m_w_in, m_fox_f_bias, m_rwkv_mu, m_rwkv_w0, m_rwkv_w_up, m_rwkv_a0, m_rwkv_a_up, m_rwkv_g_up, m_rwkv_k_k, m_rwkv_k_a, m_rwkv_r_k, m_rwkv_gn_g, m_rwkv_gn_b, m_w_mem_kv, m_w_fox_out, m_w_rwkv_out, m_w_mem_out, m_w_o, m_w_ffn_gate, m_w_ffn_up, m_w_ffn_down, v_pre1_g, v_post1_g, v_pre2_g, v_post2_g, v_mem_norm_g, v_w_in, v_fox_f_bias, v_rwkv_mu, v_rwkv_w0, v_rwkv_w_up, v_rwkv_a0, v_rwkv_a_up, v_rwkv_g_up, v_rwkv_k_k, v_rwkv_k_a, v_rwkv_r_k, v_rwkv_gn_g, v_rwkv_gn_b, v_w_mem_kv, v_w_fox_out, v_w_rwkv_out, v_w_mem_out, v_w_o, v_w_ffn_gate, v_w_ffn_up, v_w_ffn_down):
    given = dict(locals())
    w_loc = {n: given[n] for n in WEIGHTS}
    m_loc = {n: given["m_" + n] for n in WEIGHTS}
    v_loc = {n: given["v_" + n] for n in WEIGHTS}

    shard_shapes = {n: tuple(w_loc[n].shape[1:]) for n in BIG}
    layout, half_rows = _wire_layout(shard_shapes)
    mine = _wire_pack([_to_wire(n, w_loc[n][0].astype(BF16)) for n in BIG], half_rows)
    gathered = _gather_weights(mine).reshape(N_CHIPS, 2 * half_rows, WIRE_W)
    W = {}
    for n in BIG:
        off, rows = layout[n]
        blocks = _from_wire(n, gathered[:, off:off + rows], shard_shapes[n])
        if n in LORA:
            W[n] = blocks.transpose(2, 0, 1).reshape(blocks.shape[2], -1)
        else:
            W[n] = blocks.reshape(-1, blocks.shape[2])
    W.update({n: w_loc[n][0] for n in SMALL})

    loss, grad_x, gw = _layer_step(x, mem, loss_target, W)

    blocks = []
    for n in BIG:
        r, c = shard_shapes[n]
        g = gw[n].astype(BF16)
        if n in LORA:
            g = g.reshape(r, N_CHIPS, c).transpose(1, 0, 2)
        elif n in TRANSPOSED:
            g = jnp.swapaxes(g.reshape(N_CHIPS, c, r), 1, 2)
        else:
            g = g.reshape(N_CHIPS, r, c)
        blocks.append(_to_wire(n, g))
    packed = _wire_pack(blocks, half_rows)
    core = lax.axis_index("c")
    own_halves = lax.dynamic_index_in_dim(packed, core, axis=1, keepdims=False)
    chip_sums = _pair_add(own_halves, _pair_exchange(packed))
    own_chip = lax.dynamic_index_in_dim(chip_sums, 2 * lax.axis_index("x") + lax.axis_index("y"), axis=0, keepdims=False)
    half = _sum_chips(own_chip, _scatter_grads(chip_sums))
    other = _swap_halves(half)
    reduced = jnp.where(core == 0, jnp.concatenate([half, other]), jnp.concatenate([other, half]))
    g_shard = {}
    for n in BIG:
        off, rows = layout[n]
        g = _from_wire(n, reduced[off:off + rows], shard_shapes[n])
        g_shard[n] = g if n in ROW_SHARDED else g.T

    small_shapes = [w_loc[n].shape[1:] for n in SMALL] + [(1,)]
    n_small = sum(int(np.prod(s)) for s in small_shapes)
    small_rows = -(-n_small // (8 * LANES)) * 8
    flat = jnp.concatenate([gw[n].reshape(-1) for n in SMALL] + [loss.reshape(1)])
    flat = jnp.pad(flat, (0, small_rows * LANES - n_small)).reshape(small_rows, LANES).reshape(-1)
    small, off = [], 0
    flat = _allreduce_small(flat.reshape(small_rows, LANES)).reshape(-1)
    for s in small_shapes:
        cnt = int(np.prod(s))
        small.append(flat[off:off + cnt].reshape(s))
        off += cnt
    g_small = dict(zip(SMALL, small[:-1]))
    loss = small[-1][0]

    grads, deltas, new_m, new_v = [], [], [], []
    for n in WEIGHTS:
        g = (g_shard[n] if n in g_shard else g_small[n]).reshape(w_loc[n].shape)
        d, nm, nv = _adamw(n, w_loc[n], g, m_loc[n], v_loc[n])
        grads.append(g)
        deltas.append(d)
        new_m.append(nm)
        new_v.append(nv)
    return (loss, grad_x, *grads, *deltas, *new_m, *new_v)
```

```python
import functools
import math

import numpy as np
import jax
import jax.numpy as jnp
from jax import lax
from jax.experimental import pallas as pl
from jax.experimental.pallas import tpu as pltpu

F32, BF16 = jnp.float32, jnp.bfloat16
MESH = pl.DeviceIdType.MESH

D_MODEL = 1024
HEAD_DIM = 64
N_HEADS = 8
BR_W = 512
MEM_HEADS = 4
MEM_HEAD_DIM = 128
D_FF = 2816
NORM_EPS = 1e-6
GN_EPS = 64e-5
N_CHIPS = 4
N_DEV = 8
LANES = 128
VMEM_LIMIT = 48 * 1024 * 1024

ADAM_LR, ADAM_B1, ADAM_B2, ADAM_EPS, ADAM_WD, ADAM_STEP = 0.001, 0.9, 0.999, 1e-08, 0.01, 10

FOX_COLS = 3 * BR_W + N_HEADS
RWKV_COLS = 3 * BR_W + 64 + 64 + 128
COL_QKV = (0, 3 * BR_W)
COL_F = (3 * BR_W, FOX_COLS)
COL_RW = (FOX_COLS, FOX_COLS + RWKV_COLS)
COL_MQ = (COL_RW[1], COL_RW[1] + BR_W)
COL_GATE = (COL_MQ[1], COL_MQ[1] + 3 * D_MODEL)

NT_DIMS = (((1,), (1,)), ((), ()))
TN_DIMS = (((0,), (0,)), ((), ()))


def _params(sem=None, **kw):
    return pltpu.CompilerParams(dimension_semantics=sem, vmem_limit_bytes=VMEM_LIMIT, **kw)


def _sigmoid(x):
    return 1.0 / (1.0 + jnp.exp(-x))


def _log_sigmoid(x):
    return jnp.minimum(x, 0.0) - jnp.log(1.0 + jnp.exp(-jnp.abs(x)))


def _bdot(a, b, dims=None):
    a, b = a.astype(BF16), b.astype(BF16)
    if dims is None:
        return jnp.dot(a, b, preferred_element_type=F32)
    return lax.dot_general(a, b, dims, preferred_element_type=F32)


def _hdot(a, ones):
    ones = ones.astype(BF16)
    hi = a.astype(BF16)
    r1 = a - hi.astype(F32)
    mid = r1.astype(BF16)
    lo = (r1 - mid.astype(F32)).astype(BF16)
    dot = lambda x: jnp.dot(x, ones, preferred_element_type=F32)
    return dot(hi) + dot(mid) + dot(lo)


def _tile(n, cap):
    best = None
    for t in range(LANES, min(n, cap) + 1, LANES):
        if n % t == 0:
            best = t
    return best or n


def _rowwise(name, fn, rows, consts, outs, accs=(), tm=256):
    T = rows[0].shape[0]
    tm = min(tm, T)
    assert T % tm == 0
    nr, nc, no, na = len(rows), len(consts), len(outs), len(accs)

    def body(*refs):
        res = fn(*[r[...].astype(F32) for r in refs[:nr + nc]])
        if not isinstance(res, (tuple, list)):
            res = (res,)
        orefs, arefs = refs[nr + nc:nr + nc + no], refs[nr + nc + no:]
        for ref, val in zip(orefs, res[:no]):
            ref[...] = val.astype(ref.dtype)
        if na:
            @pl.when(pl.program_id(0) == 0)
            def _():
                for ref in arefs:
                    ref[...] = jnp.zeros(ref.shape, ref.dtype)
            for ref, val in zip(arefs, res[no:]):
                ref[...] += val

    in_specs = ([pl.BlockSpec((tm, r.shape[1]), lambda i: (i, 0)) for r in rows]
                + [pl.BlockSpec(c.shape, lambda i: (0, 0)) for c in consts])
    out_specs = ([pl.BlockSpec((tm, w), lambda i: (i, 0)) for w, _ in outs]
                 + [pl.BlockSpec(s, lambda i: (0, 0)) for s, _ in accs])
    out_shape = ([jax.ShapeDtypeStruct((T, w), dt) for w, dt in outs]
                 + [jax.ShapeDtypeStruct(s, dt) for s, dt in accs])
    return pl.pallas_call(
        body, grid=(T // tm,), in_specs=in_specs, out_specs=out_specs, out_shape=out_shape, name=name,
        compiler_params=_params(("arbitrary",) if na else ("parallel",)),
    )(*rows, *consts)


MM_TILE_CAP = 1408
MM_WHOLE_K = 2048


def _mm(name, a, b, ta=False, tb=False, out_dtype=F32, add=None):
    M, K = (a.shape[1], a.shape[0]) if ta else a.shape
    K2, N = (b.shape[1], b.shape[0]) if tb else b.shape
    assert K == K2
    tm, tn = _tile(M, MM_TILE_CAP), _tile(N, MM_TILE_CAP)
    tk = K if K <= MM_WHOLE_K else _tile(K, MM_TILE_CAP)
    assert M % tm == 0 and N % tn == 0 and K % tk == 0
    nk = K // tk
    a_dim, b_dim = (0 if ta else 1), (1 if tb else 0)

    def body(*refs):
        a_ref, b_ref = refs[0], refs[1]
        n_in = 2 if add is None else 3
        o_ref, acc = refs[n_in], (refs[n_in + 1] if nk > 1 else None)
        k = pl.program_id(2)
        part = lax.dot_general(a_ref[...].astype(BF16), b_ref[...].astype(BF16),
                               (((a_dim,), (b_dim,)), ((), ())), preferred_element_type=F32)

        def finish(r):
            if add is not None:
                r = r + refs[2][...].astype(F32)
            o_ref[...] = r.astype(o_ref.dtype)

        if nk == 1:
            finish(part)
            return

        @pl.when(k == 0)
        def _():
            acc[...] = part

        @pl.when(k > 0)
        def _():
            acc[...] += part

        @pl.when(k == nk - 1)
        def _():
            finish(acc[...])

    a_spec = pl.BlockSpec((tk, tm), lambda i, j, k: (k, i)) if ta else pl.BlockSpec((tm, tk), lambda i, j, k: (i, k))
    b_spec = pl.BlockSpec((tn, tk), lambda i, j, k: (j, k)) if tb else pl.BlockSpec((tk, tn), lambda i, j, k: (k, j))
    o_spec = pl.BlockSpec((tm, tn), lambda i, j, k: (i, j))
    ins, in_specs = [a, b], [a_spec, b_spec]
    if add is not None:
        ins.append(add)
        in_specs.append(o_spec)
    return pl.pallas_call(
        body, grid=(M // tm, N // tn, nk), in_specs=in_specs, out_specs=o_spec,
        out_shape=jax.ShapeDtypeStruct((M, N), out_dtype), scratch_shapes=[pltpu.VMEM((tm, tn), F32)] if nk > 1 else [],
        name=name, compiler_params=_params(("parallel", "parallel", "arbitrary")),
    )(*ins)


def _rowsum(x):
    return jnp.sum(x, axis=0, keepdims=True)


def _rms_stat(x):
    return lax.rsqrt(jnp.mean(x * x, axis=-1, keepdims=True) + NORM_EPS)


def _rms_bwd(dy, x, g):
    r = _rms_stat(x)
    xn = x * r
    dxn = dy * g
    dx = r * (dxn - xn * jnp.mean(dxn * xn, axis=-1, keepdims=True))
    return dx, _rowsum(dy * xn)


def _fox_c_fwd(f8t, bias_col, tc=256):
    B, H, S = f8t.shape
    tc = min(tc, S)

    def body(f_ref, b_ref, c_ref, carry):
        @pl.when(pl.program_id(1) == 0)
        def _():
            carry[...] = jnp.zeros(carry.shape, F32)
        lf = _log_sigmoid(f_ref[...] + b_ref[...])
        row = lax.broadcasted_iota(jnp.int32, (tc, tc), 0)
        col = lax.broadcasted_iota(jnp.int32, (tc, tc), 1)
        c = _hdot(lf, (row <= col).astype(F32)) + carry[...]
        c_ref[...] = c
        carry[...] = c[:, tc - 1:tc]

    return pl.pallas_call(
        body, grid=(B, S // tc),
        in_specs=[pl.BlockSpec((None, H, tc), lambda b, i: (b, 0, i)), pl.BlockSpec((H, 1), lambda b, i: (0, 0))],
        out_specs=pl.BlockSpec((None, H, tc), lambda b, i: (b, 0, i)),
        out_shape=jax.ShapeDtypeStruct((B, H, S), F32), scratch_shapes=[pltpu.VMEM((H, 1), F32)], name="fox_c_fwd",
        compiler_params=_params(("parallel", "arbitrary")),
    )(f8t, bias_col)


def _fox_c_bwd(dc, f8t, bias_col, tc=256):
    B, H, S = f8t.shape
    tc = min(tc, S)
    n = S // tc

    def body(dc_ref, f_ref, b_ref, df_ref, db_ref, carry):
        @pl.when(pl.program_id(1) == 0)
        def _():
            carry[...] = jnp.zeros(carry.shape, F32)
            db_ref[...] = jnp.zeros(db_ref.shape, F32)
        row = lax.broadcasted_iota(jnp.int32, (tc, tc), 0)
        col = lax.broadcasted_iota(jnp.int32, (tc, tc), 1)
        dlf = _hdot(dc_ref[...], (row >= col).astype(F32)) + carry[...]
        z = f_ref[...] + b_ref[...]
        df = dlf * (1.0 - _sigmoid(z))
        df_ref[...] = df
        db_ref[...] += jnp.sum(df, axis=1, keepdims=True)
        carry[...] = dlf[:, 0:1]

    rev = lambda b, i: (b, 0, n - 1 - i)
    return pl.pallas_call(
        body, grid=(B, n),
        in_specs=[pl.BlockSpec((None, H, tc), rev), pl.BlockSpec((None, H, tc), rev), pl.BlockSpec((H, 1), lambda b, i: (0, 0))],
        out_specs=[pl.BlockSpec((None, H, tc), rev), pl.BlockSpec((None, H, 1), lambda b, i: (b, 0, 0))],
        out_shape=[jax.ShapeDtypeStruct((B, H, S), F32), jax.ShapeDtypeStruct((B, H, 1), F32)],
        scratch_shapes=[pltpu.VMEM((H, 1), F32)], name="fox_c_bwd",
        compiler_params=_params(("parallel", "arbitrary")),
    )(dc, f8t, bias_col)


NEG_BIG = -1e30


def _fox_logits(q, kj, cq, ckj, i, j, tq, tk, scale):
    s = _bdot(q, kj, NT_DIMS) * scale + (cq - ckj)
    row = lax.broadcasted_iota(jnp.int32, (tq, tk), 0)
    col = lax.broadcasted_iota(jnp.int32, (tq, tk), 1)
    return s, col <= row + (i * tq - j * tk)


def _fox_key_blocks(i, tq, tk):
    return (i * tq + tq - 1) // tk + 1


def _fox_fwd(qkv, cq, ck, B):
    T = qkv.shape[0]
    S = T // B
    (nq, tq), (nk, tk) = cq.shape[1:3], (ck.shape[1], ck.shape[3])
    scale = HEAD_DIM ** -0.5

    def body(q_ref, k_ref, v_ref, cq_ref, ck_ref, o_ref, lse_ref):
        i = pl.program_id(1)
        lo = lax.broadcasted_iota(jnp.int32, (tq, LANES), 1) < HEAD_DIM
        q = q_ref[...]
        qh = (jnp.where(lo, q, 0), jnp.where(lo, 0, q))

        def step(j, carry):
            rows = pl.ds(pl.multiple_of(j * tk, tk), tk)
            kj, vj = k_ref[rows, :], v_ref[rows, :]
            new = []
            for h in range(2):
                m, l, acc = carry[h]
                s, ok = _fox_logits(qh[h], kj, cq_ref[h], ck_ref[h, j], i, j, tq, tk, scale)
                s = jnp.where(ok, s, NEG_BIG)
                m2 = jnp.maximum(m, jnp.max(s, axis=1, keepdims=True))
                p = jnp.exp(s - m2)
                al = jnp.exp(m - m2)
                new.append((m2, al * l + jnp.sum(p, axis=1, keepdims=True), al * acc + _bdot(p, vj)))
            return tuple(new)

        init = tuple((jnp.full((tq, 1), NEG_BIG, F32), jnp.zeros((tq, 1), F32), jnp.zeros((tq, LANES), F32)) for _ in range(2))
        (m0, l0, a0), (m1, l1, a1) = lax.fori_loop(0, _fox_key_blocks(i, tq, tk), step, init)
        o_ref[...] = jnp.where(lo, a0 / l0, a1 / l1).astype(o_ref.dtype)
        lse_ref[0] = m0 + jnp.log(l0)
        lse_ref[1] = m1 + jnp.log(l1)

    seq = lambda col0: pl.BlockSpec((S, LANES), lambda g, i: (g // 4, col0 + g % 4))
    blk = lambda col0: pl.BlockSpec((tq, LANES), lambda g, i: ((g // 4) * nq + i, col0 + g % 4))
    col = pl.BlockSpec((2, None, tq, 1), lambda g, i: (g, i, 0, 0))
    return pl.pallas_call(
        body, grid=(B * 4, nq), in_specs=[blk(0), seq(4), seq(8), col, pl.BlockSpec((2, nk, 1, tk), lambda g, i: (g, 0, 0, 0))],
        out_specs=[blk(0), col],
        out_shape=[jax.ShapeDtypeStruct((T, BR_W), BF16), jax.ShapeDtypeStruct(cq.shape, F32)], name="fox_fwd",
        compiler_params=_params(("parallel", "parallel")),
    )(qkv, qkv, qkv, cq, ck)


def _fox_bwd(qkv, o, do, cq, ck, lse, B):
    T = qkv.shape[0]
    S = T // B
    (nq, tq), (nk, tk) = cq.shape[1:3], (ck.shape[1], ck.shape[3])
    scale = HEAD_DIM ** -0.5

    def body(q_ref, k_ref, v_ref, o_ref, do_ref, cq_ref, ck_ref, lse_ref, dq_ref, dk_ref, dv_ref, dck_ref, dcq_ref,
             dk_acc, dv_acc):
        dk_acc[...] = jnp.zeros(dk_acc.shape, F32)
        dv_acc[...] = jnp.zeros(dv_acc.shape, F32)
        dck_ref[...] = jnp.zeros(dck_ref.shape, F32)
        lo = lax.broadcasted_iota(jnp.int32, (tq, LANES), 1) < HEAD_DIM

        def qloop(i, _):
            qrows = pl.ds(pl.multiple_of(i * tq, tq), tq)
            q, do_i, o_i = q_ref[qrows, :], do_ref[qrows, :], o_ref[qrows, :].astype(F32)
            qh = (jnp.where(lo, q, 0), jnp.where(lo, 0, q))
            doh = (jnp.where(lo, do_i, 0), jnp.where(lo, 0, do_i))
            delta = [jnp.sum(doh[h].astype(F32) * o_i, axis=1, keepdims=True) for h in range(2)]

            def kloop(j, carry):
                krows = pl.ds(pl.multiple_of(j * tk, tk), tk)
                kj, vj = k_ref[krows, :], v_ref[krows, :]
                new = []
                for h in range(2):
                    dq, dcq = carry[h]
                    s, ok = _fox_logits(qh[h], kj, cq_ref[h, i], ck_ref[h, j], i, j, tq, tk, scale)
                    p = jnp.where(ok, jnp.exp(s - lse_ref[h, i]), 0.0)
                    ds = p * (_bdot(doh[h], vj, NT_DIMS) - delta[h])
                    dv_acc[krows, :] += _bdot(p, doh[h], TN_DIMS)
                    dk_acc[krows, :] += _bdot(ds, qh[h], TN_DIMS) * scale
                    dck_ref[h, j] += -_rowsum(ds)
                    new.append((dq + _bdot(ds, kj) * scale, dcq + jnp.sum(ds, axis=1, keepdims=True)))
                return tuple(new)

            init = tuple((jnp.zeros((tq, LANES), F32), jnp.zeros((tq, 1), F32)) for _ in range(2))
            (dq0, dcq0), (dq1, dcq1) = lax.fori_loop(0, _fox_key_blocks(i, tq, tk), kloop, init)
            dq_ref[qrows, :] = jnp.where(lo, dq0, dq1).astype(dq_ref.dtype)
            dcq_ref[0, i] = dcq0
            dcq_ref[1, i] = dcq1
            return 0

        lax.fori_loop(0, nq, qloop, 0)
        dk_ref[...] = dk_acc[...].astype(dk_ref.dtype)
        dv_ref[...] = dv_acc[...].astype(dv_ref.dtype)

    seq = lambda col0: pl.BlockSpec((S, LANES), lambda g: (g // 4, col0 + g % 4))
    col = pl.BlockSpec((2, nq, tq, 1), lambda g: (g, 0, 0, 0))
    row = pl.BlockSpec((2, nk, 1, tk), lambda g: (g, 0, 0, 0))
    out = jax.ShapeDtypeStruct((T, BR_W), BF16)
    return pl.pallas_call(
        body, grid=(B * 4,), in_specs=[seq(0), seq(4), seq(8), seq(0), seq(0), col, row, col],
        out_specs=[seq(0), seq(0), seq(0), row, col],
        out_shape=[out, out, out, jax.ShapeDtypeStruct(ck.shape, F32), jax.ShapeDtypeStruct(cq.shape, F32)],
        scratch_shapes=[pltpu.VMEM((S, LANES), F32), pltpu.VMEM((S, LANES), F32)], name="fox_bwd",
        compiler_params=_params(("parallel",)),
    )(qkv, qkv, qkv, o, do, cq, ck, lse)


def _mem_probs(qh, kh):
    s = _bdot(qh, kh, NT_DIMS) * (MEM_HEAD_DIM ** -0.5)
    e = jnp.exp(s - jnp.max(s, axis=1, keepdims=True))
    return e / jnp.sum(e, axis=1, keepdims=True)


def _mem_fwd(q, mem_kv, B, tq=512):
    T = q.shape[0]
    S, Lm = T // B, mem_kv.shape[0] // B
    tq = min(tq, S)
    n = S // tq

    def body(q_ref, k_ref, v_ref, o_ref):
        for h in range(MEM_HEADS):
            sl = slice(h * MEM_HEAD_DIM, (h + 1) * MEM_HEAD_DIM)
            p = _mem_probs(q_ref[:, sl], k_ref[:, sl])
            o_ref[:, sl] = _bdot(p, v_ref[:, sl]).astype(o_ref.dtype)

    qs = pl.BlockSpec((tq, BR_W), lambda b, i: (b * n + i, 0))
    return pl.pallas_call(
        body, grid=(B, n),
        in_specs=[qs, pl.BlockSpec((Lm, BR_W), lambda b, i: (b, 0)), pl.BlockSpec((Lm, BR_W), lambda b, i: (b, 1))],
        out_specs=qs, out_shape=jax.ShapeDtypeStruct((T, BR_W), BF16), name="mem_fwd",
        compiler_params=_params(("parallel", "parallel")),
    )(q, mem_kv, mem_kv)


def _mem_bwd(q, mem_kv, do, B, tq=512):
    T = q.shape[0]
    S, Lm = T // B, mem_kv.shape[0] // B
    tq = min(tq, S)
    n = S // tq
    scale = MEM_HEAD_DIM ** -0.5

    def body(q_ref, k_ref, v_ref, do_ref, dq_ref, dk_ref, dv_ref):
        @pl.when(pl.program_id(1) == 0)
        def _():
            dk_ref[...] = jnp.zeros(dk_ref.shape, F32)
            dv_ref[...] = jnp.zeros(dv_ref.shape, F32)
        for h in range(MEM_HEADS):
            sl = slice(h * MEM_HEAD_DIM, (h + 1) * MEM_HEAD_DIM)
            qh, kh, vh, doh = q_ref[:, sl], k_ref[:, sl], v_ref[:, sl], do_ref[:, sl]
            p = _mem_probs(qh, kh)
            dp = _bdot(doh, vh, NT_DIMS)
            ds = p * (dp - jnp.sum(p * dp, axis=1, keepdims=True))
            dq_ref[:, sl] = (_bdot(ds, kh) * scale).astype(dq_ref.dtype)
            dk_ref[:, sl] += _bdot(ds, qh, TN_DIMS) * scale
            dv_ref[:, sl] += _bdot(p, doh, TN_DIMS)

    qs = pl.BlockSpec((tq, BR_W), lambda b, i: (b * n + i, 0))
    kv = pl.BlockSpec((Lm, BR_W), lambda b, i: (b, 0))
    return pl.pallas_call(
        body, grid=(B, n),
        in_specs=[qs, kv, pl.BlockSpec((Lm, BR_W), lambda b, i: (b, 1)), qs], out_specs=[qs, kv, kv],
        out_shape=[jax.ShapeDtypeStruct((T, BR_W), BF16), jax.ShapeDtypeStruct((B * Lm, BR_W), F32),
                   jax.ShapeDtypeStruct((B * Lm, BR_W), F32)], name="mem_bwd",
        compiler_params=_params(("parallel", "arbitrary")),
    )(q, mem_kv, mem_kv, do)


def _head_ones():
    h = np.arange(BR_W) // HEAD_DIM
    return jnp.asarray((h[:, None] == h[None, :]).astype(np.float32))


def _rw_prep(p, pp, mu, w0, w1, a0, w2, g_up, k_k, k_a, bd):
    ps = p + (pp - p) * mu
    r, k, v = ps[:, 0:512], ps[:, 512:1024], ps[:, 1024:1536]
    wa, gd = ps[:, 1536:1664], ps[:, 1664:1792]
    th = jnp.tanh(wa)
    z = w0 + _bdot(th, w1)
    wl = -jnp.exp(_log_sigmoid(z) - 0.5)
    w = jnp.exp(wl)
    a = _sigmoid(a0 + _bdot(wa, w2))
    sg = _sigmoid(gd)
    g = _bdot(sg, g_up)
    kq = k * k_k
    n2 = _hdot(kq * kq, bd)
    inv = lax.rsqrt(jnp.maximum(n2, 1e-24))
    kk = kq * inv
    k2 = k * (1.0 + (a - 1.0) * k_a)
    return dict(ps=ps, r=r, k=k, v=v, wa=wa, th=th, z=z, wl=wl, w=w, a=a, sg=sg, g=g, kq=kq, n2=n2, inv=inv, kk=kk, k2=k2)


def _keycol_selector(tm, Tc):
    e = np.zeros((tm, (tm // Tc) * LANES), np.float32)
    for t in range(tm):
        c, tl = divmod(t, Tc)
        e[t, c * LANES + tl] = e[t, c * LANES + Tc + tl] = 1.0
    return jnp.asarray(e, BF16)


def _rw_prep_fwd(p, pp, consts, B, Tc, tm=256):
    assert 2 * Tc == LANES
    T = p.shape[0]
    S = T // B
    nb, cpb = S // tm, tm // Tc
    sel = _keycol_selector(tm, Tc)
    nc = len(consts)

    def body(*refs):
        t = _rw_prep(*[r[...] for r in refs[:2 + nc]])
        sel_ref = refs[2 + nc]
        rows, cols = refs[3 + nc:7 + nc], refs[7 + nc:]
        for ref, val in zip(rows, (t["r"], t["k2"], t["v"], t["g"])):
            ref[...] = val
        lo = lax.broadcasted_iota(jnp.int32, (HEAD_DIM, LANES), 1) < HEAD_DIM
        operands = (t["w"], -t["kk"], t["kk"] * t["a"], t["k2"], t["r"])
        for n, (ref, x) in enumerate(zip(cols, operands)):
            terms = _split_bf16(x) if n == 0 else (x.astype(BF16),)
            for hp in range(4):
                xt = sum(lax.dot_general(tt[:, hp * LANES:(hp + 1) * LANES], sel_ref[...], TN_DIMS,
                                         preferred_element_type=F32) for tt in terms)
                for c in range(cpb):
                    blk = xt[:, c * LANES:(c + 1) * LANES]
                    ref[hp, c] = jnp.where(lo, blk[0:HEAD_DIM], blk[HEAD_DIM:2 * HEAD_DIM]).astype(ref.dtype)

    row_spec = lambda w: pl.BlockSpec((tm, w), lambda i: (i, 0))
    col_spec = pl.BlockSpec((None, 4, cpb, HEAD_DIM, LANES), lambda i: (i // nb, 0, i % nb, 0, 0))
    col_shape = lambda dt: jax.ShapeDtypeStruct((B, 4, S // Tc, HEAD_DIM, LANES), dt)
    out = pl.pallas_call(
        body, grid=(T // tm,),
        in_specs=[row_spec(RWKV_COLS)] * 2 + [pl.BlockSpec(c.shape, lambda i: (0, 0)) for c in consts]
        + [pl.BlockSpec(sel.shape, lambda i: (0, 0))],
        out_specs=[row_spec(BR_W)] * 4 + [col_spec] * 5,
        out_shape=[jax.ShapeDtypeStruct((T, BR_W), F32)] * 4 + [col_shape(F32)] + [col_shape(BF16)] * 4,
        name="rwkv_prep_fwd", compiler_params=_params(("parallel",)),
    )(p, pp, *consts, sel)
    return out[:4], [c.reshape(B * 4, S // Tc, HEAD_DIM, LANES) for c in out[4:]]


def _rw_prep_bwd(p, pp, cots, consts):
    def fn(p, pp, dr1, dr2, dw, dk21, dk22, dv1, dv2, dav, dbv, dg, mu, w0, w1, a0, w2, g_up, k_k, k_a, bd):
        t = _rw_prep(p, pp, mu, w0, w1, a0, w2, g_up, k_k, k_a, bd)
        dr, dk2, dv = dr1 + dr2, dk21 + dk22, dv1 + dv2
        a, k, kk, kq, inv = t["a"], t["k"], t["kk"], t["kq"], t["inv"]
        dkk = dbv * a - dav
        da = dbv * kk + dk2 * k * k_a
        dk = dk2 * (1.0 + (a - 1.0) * k_a)
        d_k_a = _rowsum(dk2 * k * (a - 1.0))
        proj = _hdot(dkk * kq, bd)
        dkq = dkk * inv - jnp.where(t["n2"] > 1e-24, kq * inv * inv * inv * proj, 0.0)
        dk = dk + dkq * k_k
        d_k_k = _rowsum(dkq * k)
        dpa = da * a * (1.0 - a)
        d_a0 = _rowsum(dpa)
        dwa = _bdot(dpa, w2, NT_DIMS)
        d_w2 = _bdot(t["wa"], dpa, TN_DIMS)
        dz = dw * t["w"] * t["wl"] * (1.0 - _sigmoid(t["z"]))
        d_w0 = _rowsum(dz)
        th = t["th"]
        dwa = dwa + _bdot(dz, w1, NT_DIMS) * (1.0 - th * th)
        d_w1 = _bdot(th, dz, TN_DIMS)
        sg = t["sg"]
        dgd = _bdot(dg, g_up, NT_DIMS) * sg * (1.0 - sg)
        d_g_up = _bdot(sg, dg, TN_DIMS)
        dps = jnp.concatenate([dr, dk, dv, dwa, dgd], axis=1)
        d_mu = _rowsum(dps * (pp - p))
        return dps * (1.0 - mu), dps * mu, d_mu, d_w0, d_w1, d_a0, d_w2, d_g_up, d_k_k, d_k_a

    accs = [((1, RWKV_COLS), F32), ((1, BR_W), F32), ((LANES, BR_W), F32), ((1, BR_W), F32), ((LANES, BR_W), F32),
            ((LANES, BR_W), F32), ((1, BR_W), F32), ((1, BR_W), F32)]
    return _rowwise("rwkv_prep_bwd", fn, [p, pp] + list(cots), consts, [(RWKV_COLS, F32)] * 2, accs, tm=128)


def _rw_head(y, r, k2, v, g, gn_g, gn_b, r_k, bd):
    mean = _hdot(y, bd) * (1.0 / HEAD_DIM)
    yc = y - mean
    rs = lax.rsqrt(_hdot(yc * yc, bd) * (1.0 / HEAD_DIM) + GN_EPS)
    yn = yc * rs
    bs = _hdot(r * k2 * r_k, bd)
    return yn, rs, bs, yn * gn_g + gn_b + bs * v


def _rw_head_fwd(y, r, k2, v, g, consts):
    def fn(y, r, k2, v, g, *c):
        return _rw_head(y, r, k2, v, g, *c)[3] * g
    return _rowwise("rwkv_head_fwd", fn, [y, r, k2, v, g], consts, [(BR_W, BF16)])[0]


def _rw_head_bwd(dout, y, r, k2, v, g, consts):
    def fn(dout, y, r, k2, v, g, gn_g, gn_b, r_k, bd):
        dout = dout.astype(F32)
        yn, rs, bs, zz = _rw_head(y, r, k2, v, g, gn_g, gn_b, r_k, bd)
        dg = dout * zz
        dz = dout * g
        dyn = dz * gn_g
        inv_n = 1.0 / HEAD_DIM
        dy = rs * (dyn - _hdot(dyn, bd) * inv_n - yn * (_hdot(dyn * yn, bd) * inv_n))
        dq = _hdot(dz * v, bd)
        return dy, dg, dq * k2 * r_k, dq * r * r_k, dz * bs, _rowsum(dz * yn), _rowsum(dz), _rowsum(dq * r * k2)
    return _rowwise("rwkv_head_bwd", fn, [dout, y, r, k2, v, g], consts, [(BR_W, F32)] * 5, [((1, BR_W), F32)] * 3)


SCAN_TC = 64


def _scan_onehot(Tc):
    w = np.zeros((Tc // 2, 2 * Tc, 2 * LANES), np.float32)
    for tt in range(Tc // 2):
        for u in range(2):
            for h in range(2):
                w[tt, h * Tc + 2 * tt + u, u * LANES + h * HEAD_DIM: u * LANES + (h + 1) * HEAD_DIM] = 1.0
    return jnp.asarray(w, BF16)


def _split_bf16(x):
    hi = x.astype(BF16)
    return hi, (x - hi.astype(F32)).astype(BF16)


def _key_tiles(l_w, others, onehot):
    dot = lambda x: jnp.dot(x, onehot, preferred_element_type=F32)
    whi, wmid = l_w
    return [dot(whi) + dot(wmid)] + [dot(o) for o in others]


def _rw_scan_fwd(LW, LA, LB, LK, LR, v, P=2):
    NP, nc, _, Tc2 = LW.shape
    Tc = Tc2 // 2
    S = nc * Tc
    onehot = _scan_onehot(Tc)
    npb = 4 // P

    def body(lw, la, lb, lk, lr, v_ref, oh_ref, y_ref, sa_ref, sb_ref, st):
        @pl.when(pl.program_id(1) == 0)
        def _():
            st[...] = jnp.zeros(st.shape, F32)
        s = [st[p] for p in range(P)]
        cols = [(_split_bf16(lw[p]), [ref[p].astype(BF16) for ref in (la, lb, lk, lr)]) for p in range(P)]
        for tt in range(Tc // 2):
            tiles = [_key_tiles(c[0], c[1], oh_ref[tt]) for c in cols]
            for u in range(2):
                t = 2 * tt + u
                for p in range(P):
                    W, A, Bt, Kt, R = (x[:, u * LANES:(u + 1) * LANES] for x in tiles[p])
                    ls = slice(p * LANES, (p + 1) * LANES)
                    sb_ref[p, t] = s[p]
                    sa = _rowsum(s[p] * A)
                    s[p] = s[p] * W + Bt * sa + Kt * v_ref[t:t + 1, ls]
                    y_ref[t:t + 1, ls] = _rowsum(s[p] * R)
                    sa_ref[t:t + 1, ls] = sa
        for p in range(P):
            st[p] = s[p]

    lspec = pl.BlockSpec((P, None, HEAD_DIM, Tc2), lambda g, c: (g, c, 0, 0))
    rows = pl.BlockSpec((Tc, P * LANES), lambda g, c: ((g // npb) * nc + c, g % npb))
    rowshape = jax.ShapeDtypeStruct(v.shape, F32)
    return pl.pallas_call(
        body, grid=(NP // P, nc), in_specs=[lspec] * 5 + [rows, pl.BlockSpec(onehot.shape, lambda g, c: (0, 0, 0))],
        out_specs=[rows, rows, pl.BlockSpec((P, Tc, HEAD_DIM, LANES), lambda g, c: (g, c, 0, 0))],
        out_shape=[rowshape, rowshape, jax.ShapeDtypeStruct((NP, S, HEAD_DIM, LANES), F32)],
        scratch_shapes=[pltpu.VMEM((P, HEAD_DIM, LANES), F32)], name="rwkv_scan_fwd",
        compiler_params=_params(("parallel", "arbitrary")),
    )(LW, LA, LB, LK, LR, v, onehot)


SCAN_G_ROWS = 16


def _rw_scan_bwd(LW, LA, LB, LK, LR, v, sa, dy, sb, P=4):
    NP, nc, _, Tc2 = LW.shape
    Tc = Tc2 // 2
    onehot = _scan_onehot(Tc)
    npb = 4 // P

    def body(lw, la, lb, lk, lr, v_ref, sa_ref, dy_ref, sb_ref, oh_ref, dv_ref, dk_ref, db_ref, dw_ref, dr_ref, da_ref, dst):
        @pl.when(pl.program_id(1) == 0)
        def _():
            dst[...] = jnp.zeros(dst.shape, F32)
        rid = lax.broadcasted_iota(jnp.int32, (SCAN_G_ROWS, LANES), 0)
        lane = lax.broadcasted_iota(jnp.int32, (SCAN_G_ROWS, LANES), 1)
        own = (((rid % 2) == 0) == (lane < HEAD_DIM)) & (rid < 10)
        lo = lane[0:1] < HEAD_DIM
        nt = lambda rows, tile: lax.dot_general(rows.astype(BF16), tile.astype(BF16), NT_DIMS, preferred_element_type=F32)
        ds = [dst[p] for p in range(P)]
        cols = [(_split_bf16(lw[p]), [ref[p].astype(BF16) for ref in (la, lb, lk, lr)]) for p in range(P)]
        for tt in reversed(range(Tc // 2)):
            tiles = [_key_tiles(c[0], c[1], oh_ref[tt]) for c in cols]
            for u in (1, 0):
                t = 2 * tt + u
                for p in range(P):
                    W, A, Bt, Kt, R = (x[:, u * LANES:(u + 1) * LANES] for x in tiles[p])
                    ls = slice(p * LANES, (p + 1) * LANES)
                    vr, sar, dyr = (ref[t:t + 1, ls] for ref in (v_ref, sa_ref, dy_ref))
                    sp = sb_ref[p, t]
                    s_t = sp * W + Bt * sar + Kt * vr
                    d = ds[p] + R * dyr
                    dv_ref[t:t + 1, ls] = _rowsum(d * Kt)
                    dsar = _rowsum(d * Bt)
                    rows = jnp.where(rid < 2, vr, jnp.where(rid < 4, sar, jnp.where(rid < 6, 1.0, jnp.where(rid < 8, dyr, dsar))))
                    g = nt(jnp.where(own, rows, 0.0), jnp.concatenate([d, d * sp, s_t, sp], axis=0))
                    ga, gb = g[:, 0:LANES], g[:, LANES:2 * LANES]
                    ra, rb = pltpu.roll(ga, HEAD_DIM, 1), pltpu.roll(gb, HEAD_DIM, 1)
                    dk_ref[t:t + 1, ls] = jnp.where(lo, ga[0:1], ra[1:2])
                    db_ref[t:t + 1, ls] = jnp.where(lo, ga[2:3], ra[3:4])
                    dw_ref[t:t + 1, ls] = jnp.where(lo, ra[4:5], ga[5:6])
                    dr_ref[t:t + 1, ls] = jnp.where(lo, gb[6:7], rb[7:8])
                    da_ref[t:t + 1, ls] = jnp.where(lo, rb[8:9], gb[9:10])
                    ds[p] = d * W + A * dsar
        for p in range(P):
            dst[p] = ds[p]

    rev = lambda g, c: (g, nc - 1 - c, 0, 0)
    lspec = pl.BlockSpec((P, None, HEAD_DIM, Tc2), rev)
    rows = pl.BlockSpec((Tc, P * LANES), lambda g, c: ((g // npb) * nc + nc - 1 - c, g % npb))
    return pl.pallas_call(
        body, grid=(NP // P, nc),
        in_specs=[lspec] * 5 + [rows] * 3 + [pl.BlockSpec((P, Tc, HEAD_DIM, LANES), rev),
                                             pl.BlockSpec(onehot.shape, lambda g, c: (0, 0, 0))],
        out_specs=[rows] * 6, out_shape=[jax.ShapeDtypeStruct(v.shape, F32)] * 6,
        scratch_shapes=[pltpu.VMEM((P, HEAD_DIM, LANES), F32)], name="rwkv_scan_bwd",
        compiler_params=_params(("parallel", "arbitrary")),
    )(LW, LA, LB, LK, LR, v, sa, dy, sb, onehot)


def _shift_prev(p, B):
    T, W = p.shape
    return jnp.pad(p.reshape(B, T // B, W), ((0, 0), (1, 0), (0, 0)))[:, :-1].reshape(T, W)


def _shift_next(p, B):
    T, W = p.shape
    return jnp.pad(p.reshape(B, T // B, W), ((0, 0), (0, 1), (0, 0)))[:, 1:].reshape(T, W)


FOX_FWD_BLOCKS = (512, 1024)
FOX_BWD_BLOCKS = (512, 512)


def _layer_step(x, mem, target, W, scan_tc=SCAN_TC, fox_fwd_t=FOX_FWD_BLOCKS, fox_bwd_t=FOX_BWD_BLOCKS):
    B, S, _ = x.shape
    T = B * S
    x2, tgt2 = x.reshape(T, D_MODEL), target.reshape(T, D_MODEL)
    mem2 = mem.reshape(-1, D_MODEL)
    w_in_t = W["w_in"]
    wt_qkv, wt_rw, wt_mq, wt_gate = (w_in_t[lo:hi] for lo, hi in (COL_QKV, COL_RW, COL_MQ, COL_GATE))
    wt_f = jnp.pad(w_in_t[COL_F[0]:COL_F[1]], ((0, LANES - N_HEADS), (0, 0)))
    row = lambda v: v.reshape(1, -1).astype(F32)
    pre1_g, post1_g, pre2_g, post2_g, mem_g = (row(W[n]) for n in ("pre1_g", "post1_g", "pre2_g", "post2_g", "mem_norm_g"))

    u = _rowwise("rms_pre1", lambda x, g: x * _rms_stat(x) * g, [x2], [pre1_g], [(D_MODEL, BF16)])[0]
    qkv = _mm("proj_qkv", u, wt_qkv, tb=True, out_dtype=BF16)
    f_pad = _mm("proj_f", u, wt_f, tb=True)
    p_rw = _mm("proj_rwkv", u, wt_rw, tb=True)
    memq = _mm("proj_memq", u, wt_mq, tb=True, out_dtype=BF16)
    gate = _mm("proj_gate", u, wt_gate, tb=True, out_dtype=BF16)

    bias_col = W["fox_f_bias"].reshape(N_HEADS, 1).astype(F32)
    f8t = f_pad[:, :N_HEADS].reshape(B, S, N_HEADS).transpose(0, 2, 1)
    c = _fox_c_fwd(f8t, bias_col)
    G = B * N_HEADS
    q_blocks = lambda a, t: a.reshape(G, S // min(t, S), min(t, S), 1)
    k_blocks = lambda a, t: a.reshape(G, S // min(t, S), 1, min(t, S))
    c_col = c.reshape(G, S, 1)
    fox_out, lse = _fox_fwd(qkv, q_blocks(c_col, fox_fwd_t[0]), k_blocks(c, fox_fwd_t[1]), B)

    bd = _head_ones()
    zpad = jnp.zeros((64, BR_W), F32)
    w1 = jnp.concatenate([W["rwkv_w_up"].astype(F32), zpad], axis=0)
    w2 = jnp.concatenate([zpad, W["rwkv_a_up"].astype(F32)], axis=0)
    prep_consts = [row(W["rwkv_mu"]), row(W["rwkv_w0"]), w1, row(W["rwkv_a0"]), w2, W["rwkv_g_up"].astype(F32),
                   row(W["rwkv_k_k"]), row(W["rwkv_k_a"]), bd]
    p_prev = _shift_prev(p_rw, B)
    (rr, rk2, rv, rg), scan_cols = _rw_prep_fwd(p_rw, p_prev, prep_consts, B, scan_tc)
    ry, rsa, sb = _rw_scan_fwd(*scan_cols, rv)
    head_consts = [row(W["rwkv_gn_g"]), row(W["rwkv_gn_b"]), row(W["rwkv_r_k"]), bd]
    rwkv_out = _rw_head_fwd(ry, rr, rk2, rv, rg, head_consts)

    mn = _rowwise("rms_mem", lambda m, g: m * _rms_stat(m) * g, [mem2], [mem_g], [(D_MODEL, BF16)])[0]
    mem_kv = _mm("proj_memkv", mn, W["w_mem_kv"], out_dtype=BF16)
    mem_out = _mem_fwd(memq, mem_kv, B)

    fo = [_mm("branch_" + n, a, W[n], tb=True, out_dtype=BF16)
          for n, a in (("w_fox_out", fox_out), ("w_rwkv_out", rwkv_out), ("w_mem_out", mem_out))]

    def merge(gate, f0, f1, f2):
        return sum(_sigmoid(gate[:, i * D_MODEL:(i + 1) * D_MODEL]) * f for i, f in enumerate((f0, f1, f2)))
    merged = _rowwise("merge", merge, [gate] + fo, [], [(D_MODEL, BF16)])[0]
    y1 = _mm("proj_o", merged, W["w_o"])

    def mid(x, y1, g1, g2):
        h1 = x + y1 * _rms_stat(y1) * g1
        return h1, h1 * _rms_stat(h1) * g2
    h1, u2 = _rowwise("norm_mid", mid, [x2, y1], [post1_g, pre2_g], [(D_MODEL, F32), (D_MODEL, BF16)])
    gt = _mm("ffn_gate", u2, W["w_ffn_gate"], tb=True, out_dtype=BF16)
    up = _mm("ffn_up", u2, W["w_ffn_up"], tb=True, out_dtype=BF16)
    act = _rowwise("swiglu", lambda gt, up: gt * _sigmoid(gt) * up, [gt, up], [], [(D_FF, BF16)])[0]
    ffn = _mm("ffn_down", act, W["w_ffn_down"])

    def tail(h1, ffn, tgt, g):
        err = h1 + ffn * _rms_stat(ffn) * g - tgt
        dh2 = err * (1.0 / D_MODEL)
        dffn, dg = _rms_bwd(dh2, ffn, g)
        loss = 0.5 * jnp.sum(jnp.sum(err * err, axis=1, keepdims=True) * (1.0 / D_MODEL), axis=0, keepdims=True)
        return dh2, dffn, dg, jnp.broadcast_to(loss, (1, LANES))
    dh2, dffn, d_post2, loss = _rowwise("loss_tail", tail, [h1, ffn, tgt2], [post2_g], [(D_MODEL, F32), (D_MODEL, BF16)],
                                        [((1, D_MODEL), F32), ((1, LANES), F32)])
    gw = {"post2_g": d_post2}
    dact = _mm("d_act", dffn, W["w_ffn_down"], tb=True, out_dtype=BF16)
    gw["w_ffn_down"] = _mm("g_ffn_down", act, dffn, ta=True, out_dtype=BF16)

    def swiglu_bwd(dact, gt, up):
        s = _sigmoid(gt)
        return dact * up * s * (1.0 + gt * (1.0 - s)), dact * gt * s
    dgt, dup = _rowwise("swiglu_bwd", swiglu_bwd, [dact, gt, up], [], [(D_FF, BF16)] * 2)
    du2 = _mm("d_u2_gate", dgt, W["w_ffn_gate"])
    du2 = _mm("d_u2_up", dup, W["w_ffn_up"], add=du2)
    gw["w_ffn_gate"] = _mm("g_ffn_gate", dgt, u2, ta=True, out_dtype=BF16)
    gw["w_ffn_up"] = _mm("g_ffn_up", dup, u2, ta=True, out_dtype=BF16)

    def mid_bwd(du2, dh2, h1, y1, g1, g2):
        dh1_n, d_pre2 = _rms_bwd(du2, h1, g2)
        dh1 = dh2 + dh1_n
        dy1, d_post1 = _rms_bwd(dh1, y1, g1)
        return dh1, dy1, d_post1, d_pre2
    dh1, dy1, gw["post1_g"], gw["pre2_g"] = _rowwise(
        "norm_mid_bwd", mid_bwd, [du2, dh2, h1, y1], [post1_g, pre2_g], [(D_MODEL, F32), (D_MODEL, BF16)],
        [((1, D_MODEL), F32)] * 2)
    dmerged = _mm("d_merged", dy1, W["w_o"], tb=True, out_dtype=BF16)
    gw["w_o"] = _mm("g_w_o", merged, dy1, ta=True, out_dtype=BF16)

    def merge_bwd(dm, gate, f0, f1, f2):
        s = [_sigmoid(gate[:, i * D_MODEL:(i + 1) * D_MODEL]) for i in range(3)]
        dgate = jnp.concatenate([dm * f * si * (1.0 - si) for f, si in zip((f0, f1, f2), s)], axis=1)
        return dm * s[0], dm * s[1], dm * s[2], dgate
    dfo0, dfo1, dfo2, dgate = _rowwise("merge_bwd", merge_bwd, [dmerged, gate] + fo, [],
                                       [(D_MODEL, BF16)] * 3 + [(3 * D_MODEL, BF16)])
    d_branch = {}
    for n, a, dfo in (("w_fox_out", fox_out, dfo0), ("w_rwkv_out", rwkv_out, dfo1), ("w_mem_out", mem_out, dfo2)):
        d_branch[n] = _mm("d_in_" + n, dfo, W[n], out_dtype=BF16)
        gw[n] = _mm("g_" + n, dfo, a, ta=True, out_dtype=BF16)

    dmemq, dkm, dvm = _mem_bwd(memq, mem_kv, d_branch["w_mem_out"], B)
    dmem_kv = jnp.concatenate([dkm, dvm], axis=1)
    gw["w_mem_kv"] = _mm("g_w_mem_kv", mn, dmem_kv, ta=True, out_dtype=BF16)
    dmn = _mm("d_mn", dmem_kv, W["w_mem_kv"], tb=True)
    gw["mem_norm_g"] = _rowwise("rms_mem_bwd", lambda d, m, g: _rms_bwd(d, m, g)[1], [dmn, mem2], [mem_g], [],
                                [((1, D_MODEL), F32)])[0]

    dfq, dfk, dfv, dck, dcq = _fox_bwd(qkv, fox_out, d_branch["w_fox_out"], q_blocks(c_col, fox_bwd_t[0]),
                                       k_blocks(c, fox_bwd_t[1]), q_blocks(lse, fox_bwd_t[0]), B)
    df8t, dbias = _fox_c_bwd(dck.reshape(B, N_HEADS, S) + dcq.reshape(B, N_HEADS, S), f8t, bias_col)
    gw["fox_f_bias"] = jnp.sum(dbias, axis=0).reshape(1, N_HEADS)
    dqkv = jnp.concatenate([dfq, dfk, dfv], axis=1)
    df_pad = jnp.pad(df8t.transpose(0, 2, 1).reshape(T, N_HEADS), ((0, 0), (0, LANES - N_HEADS))).astype(BF16)

    dry, drg, dr_h, dk2_h, dv_h, gw["rwkv_gn_g"], gw["rwkv_gn_b"], gw["rwkv_r_k"] = _rw_head_bwd(
        d_branch["w_rwkv_out"], ry, rr, rk2, rv, rg, head_consts)
    dv_s, dk2_s, db_s, dw_s, dr_s, da_s = _rw_scan_bwd(*scan_cols, rv, rsa, dry, sb)
    dP, dPp, gw["rwkv_mu"], gw["rwkv_w0"], d_w1, gw["rwkv_a0"], d_w2, gw["rwkv_g_up"], gw["rwkv_k_k"], gw["rwkv_k_a"] = \
        _rw_prep_bwd(p_rw, p_prev, [dr_s, dr_h, dw_s, dk2_s, dk2_h, dv_s, dv_h, da_s, db_s, drg], prep_consts)
    gw["rwkv_w_up"], gw["rwkv_a_up"] = d_w1[:64], d_w2[64:]
    dp_rw = (dP + _shift_next(dPp, B)).astype(BF16)

    du = _mm("d_u_qkv", dqkv, wt_qkv)
    du = _mm("d_u_f", df_pad, wt_f, add=du)
    du = _mm("d_u_rwkv", dp_rw, wt_rw, add=du)
    du = _mm("d_u_memq", dmemq, wt_mq, add=du)
    du = _mm("d_u_gate", dgate, wt_gate, add=du)
    gw["w_in"] = jnp.concatenate(
        [_mm("g_w_qkv", dqkv, u, ta=True, out_dtype=BF16), _mm("g_w_f", df_pad, u, ta=True, out_dtype=BF16)[:N_HEADS],
         _mm("g_w_rwkv", dp_rw, u, ta=True, out_dtype=BF16), _mm("g_w_memq", dmemq, u, ta=True, out_dtype=BF16),
         _mm("g_w_gate", dgate, u, ta=True, out_dtype=BF16)], axis=0)

    def pre1_bwd(du, dh1, x, g):
        dx, dg = _rms_bwd(du, x, g)
        return dh1 + dx, dg
    dx, gw["pre1_g"] = _rowwise("rms_pre1_bwd", pre1_bwd, [du, dh1, x2], [pre1_g], [(D_MODEL, F32)], [((1, D_MODEL), F32)])
    return loss[0, 0], dx.reshape(B, S, D_MODEL), gw


TRANSPOSED = ("w_in", "w_ffn_gate", "w_ffn_up", "w_fox_out", "w_rwkv_out", "w_mem_out")
LORA = ("rwkv_w_up", "rwkv_a_up", "rwkv_g_up")
ROW_SHARDED = ("w_mem_kv", "w_o", "w_ffn_down")
BIG = ("w_in", "w_ffn_gate", "w_ffn_up", "w_mem_kv", "w_o", "w_ffn_down", "w_fox_out", "w_rwkv_out", "w_mem_out") + LORA
SMALL = ("pre1_g", "post1_g", "pre2_g", "post2_g", "mem_norm_g", "fox_f_bias", "rwkv_mu", "rwkv_w0", "rwkv_a0", "rwkv_k_k",
         "rwkv_k_a", "rwkv_r_k", "rwkv_gn_g", "rwkv_gn_b")
WEIGHTS = ("pre1_g", "post1_g", "pre2_g", "post2_g", "mem_norm_g", "w_in", "fox_f_bias", "rwkv_mu", "rwkv_w0", "rwkv_w_up",
           "rwkv_a0", "rwkv_a_up", "rwkv_g_up", "rwkv_k_k", "rwkv_k_a", "rwkv_r_k", "rwkv_gn_g", "rwkv_gn_b", "w_mem_kv",
           "w_fox_out", "w_rwkv_out", "w_mem_out", "w_o", "w_ffn_gate", "w_ffn_up", "w_ffn_down")
WIRE_W = 1024
WIRE_ROW_ALIGN = 16
WIRE_HALF_ALIGN = 240


def _wire_rows(name, shard_shape):
    r, c = shard_shape
    if name in ROW_SHARDED:
        return r
    return -(-c // WIRE_ROW_ALIGN) * WIRE_ROW_ALIGN if r == WIRE_W else (r * c) // WIRE_W


def _to_wire(name, a):
    if name not in ROW_SHARDED:
        a = jnp.swapaxes(a, -1, -2)
    lead, (n, w) = a.shape[:-2], a.shape[-2:]
    if w != WIRE_W:
        return a.reshape(lead + ((n * w) // WIRE_W, WIRE_W))
    return jnp.pad(a, [(0, 0)] * len(lead) + [(0, (-n) % WIRE_ROW_ALIGN), (0, 0)])


def _from_wire(name, a, shard_shape):
    r, c = shard_shape
    if name in ROW_SHARDED:
        return a
    return a[..., :c, :] if r == WIRE_W else a.reshape(a.shape[:-2] + (c, r))


def _wire_layout(shard_shapes):
    layout, off = {}, 0
    for n in BIG:
        rows = _wire_rows(n, shard_shapes[n])
        layout[n] = (off, rows)
        off += rows
    return layout, -(-off // (2 * WIRE_HALF_ALIGN)) * WIRE_HALF_ALIGN


def _wire_pack(blocks, half_rows):
    a = jnp.concatenate(blocks, axis=-2)
    lead = a.shape[:-2]
    a = jnp.pad(a, [(0, 0)] * len(lead) + [(0, 2 * half_rows - a.shape[-2]), (0, 0)])
    return a.reshape(lead + (2, half_rows, WIRE_W))


def _my_place():
    return lax.axis_index("x"), lax.axis_index("y"), lax.axis_index("c")


def _other_chips(x, y):
    return [(1 - x, y), (x, 1 - y), (1 - x, 1 - y)]


ANY = pl.BlockSpec(memory_space=pl.ANY)


def _gather_weights(packed):
    _, R, L = packed.shape
    me = 2 * lax.axis_index("x") + lax.axis_index("y")
    base = lax.dynamic_update_index_in_dim(jnp.zeros((N_CHIPS, 2, R, L), packed.dtype), packed, me, 0)

    def body(in_ref, base_ref, out_ref, send_sems, recv_sems):
        x, y, c = _my_place()
        chip = lambda px, py: 2 * px + py
        sibling = (x, y, 1 - c)
        others = _other_chips(x, y)

        def copy(k, src, dst, to):
            return pltpu.make_async_remote_copy(src_ref=src, dst_ref=dst, send_sem=send_sems.at[k], recv_sem=recv_sems.at[k],
                                                device_id=to, device_id_type=MESH)

        sends = [copy(j, in_ref.at[c], out_ref.at[chip(x, y), c], (px, py, c)) for j, (px, py) in enumerate(others)]
        for cp in sends:
            cp.start()
        passed = [copy(3 + j, out_ref.at[chip(px, py), c], out_ref.at[chip(px, py), c], sibling)
                  for j, (px, py) in enumerate(others)]
        for j, (px, py) in enumerate(others):
            copy(j, in_ref.at[c], out_ref.at[chip(px, py), c], (px, py, c)).wait_recv()
            passed[j].start()
        for j, (px, py) in enumerate(others):
            copy(3 + j, in_ref.at[1 - c], out_ref.at[chip(px, py), 1 - c], sibling).wait_recv()
        for cp in sends + passed:
            cp.wait_send()

    return pl.pallas_call(
        body, out_shape=jax.ShapeDtypeStruct(base.shape, base.dtype), in_specs=[ANY, ANY], out_specs=ANY,
        input_output_aliases={1: 0}, scratch_shapes=[pltpu.SemaphoreType.DMA((6,)), pltpu.SemaphoreType.DMA((6,))],
        name="gather_weights",
    )(packed, base)


def _pair_exchange(parts):
    n, _, R, L = parts.shape

    def body(in_ref, out_ref, send_sems, recv_sems):
        x, y, c = _my_place()
        copies = [pltpu.make_async_remote_copy(src_ref=in_ref.at[s, 1 - c], dst_ref=out_ref.at[s], send_sem=send_sems.at[s],
                                               recv_sem=recv_sems.at[s], device_id=(x, y, 1 - c), device_id_type=MESH)
                  for s in range(n)]
        for cp in copies:
            cp.start()
        for cp in copies:
            cp.wait()

    return pl.pallas_call(
        body, out_shape=jax.ShapeDtypeStruct((n, R, L), parts.dtype), in_specs=[ANY], out_specs=ANY,
        scratch_shapes=[pltpu.SemaphoreType.DMA((n,)), pltpu.SemaphoreType.DMA((n,))], name="pair_exchange",
    )(parts)


def _pair_add(a, b):
    n, R, L = a.shape
    tr = _tile_rows(R, WIRE_HALF_ALIGN)

    def body(a_ref, b_ref, o_ref):
        o_ref[...] = (a_ref[...].astype(F32) + b_ref[...].astype(F32)).astype(o_ref.dtype)

    spec = pl.BlockSpec((n, tr, L), lambda i: (0, i, 0))
    return pl.pallas_call(
        body, grid=(R // tr,), in_specs=[spec, spec], out_specs=spec, out_shape=jax.ShapeDtypeStruct(a.shape, a.dtype),
        name="pair_add", compiler_params=_params(("parallel",)),
    )(a, b)


def _scatter_grads(parts):
    n, R, L = parts.shape

    def body(in_ref, out_ref, send_sems, recv_sems):
        x, y, c = _my_place()
        copies = [pltpu.make_async_remote_copy(src_ref=in_ref.at[2 * px + py], dst_ref=out_ref.at[j], send_sem=send_sems.at[j],
                                               recv_sem=recv_sems.at[j], device_id=(px, py, c), device_id_type=MESH)
                  for j, (px, py) in enumerate(_other_chips(x, y))]
        for cp in copies:
            cp.start()
        for cp in copies:
            cp.wait()

    return pl.pallas_call(
        body, out_shape=jax.ShapeDtypeStruct((n - 1, R, L), parts.dtype), in_specs=[ANY], out_specs=ANY,
        scratch_shapes=[pltpu.SemaphoreType.DMA((3,)), pltpu.SemaphoreType.DMA((3,))], name="scatter_grads",
    )(parts)


def _sum_chips(own, others):
    n, R, L = others.shape
    tr = _tile_rows(R, WIRE_HALF_ALIGN)

    def body(a_ref, p_ref, o_ref):
        acc = a_ref[...].astype(F32)
        for i in range(n):
            acc = acc + p_ref[i].astype(F32)
        o_ref[...] = acc

    return pl.pallas_call(
        body, grid=(R // tr,), in_specs=[pl.BlockSpec((tr, L), lambda i: (i, 0)), pl.BlockSpec((n, tr, L), lambda i: (0, i, 0))],
        out_specs=pl.BlockSpec((tr, L), lambda i: (i, 0)), out_shape=jax.ShapeDtypeStruct((R, L), F32), name="sum_grads",
        compiler_params=_params(("parallel",)),
    )(own, others)


def _tile_rows(R, cap=2048):
    best = 8
    for t in range(8, min(R, cap) + 1, 8):
        if R % t == 0:
            best = t
    return best if R % 8 == 0 else R


def _swap_halves(half):
    def body(in_ref, out_ref, send_sem, recv_sem):
        x, y, c = _my_place()
        cp = pltpu.make_async_remote_copy(src_ref=in_ref, dst_ref=out_ref, send_sem=send_sem, recv_sem=recv_sem,
                                          device_id=(x, y, 1 - c), device_id_type=MESH)
        cp.start()
        cp.wait()

    return pl.pallas_call(
        body, out_shape=jax.ShapeDtypeStruct(half.shape, half.dtype), in_specs=[ANY], out_specs=ANY,
        scratch_shapes=[pltpu.SemaphoreType.DMA, pltpu.SemaphoreType.DMA], name="swap_halves",
    )(half)


def _allreduce_small(v):
    R, L = v.shape

    def body(in_ref, out_ref, buf, send_sems, recv_sems):
        x, y, c = _my_place()
        me = 4 * x + 2 * y + c
        buf[me] = in_ref[...]
        started = []
        for k in range(1, N_DEV):
            to = (x ^ (k >> 2), y ^ ((k >> 1) & 1), c ^ (k & 1))
            cp = pltpu.make_async_remote_copy(src_ref=in_ref, dst_ref=buf.at[me], send_sem=send_sems.at[k - 1],
                                              recv_sem=recv_sems.at[k - 1], device_id=to, device_id_type=MESH)
            cp.start()
            started.append(cp)
        for cp in started:
            cp.wait()
        acc = buf[0]
        for i in range(1, N_DEV):
            acc = acc + buf[i]
        out_ref[...] = acc

    vm = pl.BlockSpec(memory_space=pltpu.VMEM)
    return pl.pallas_call(
        body, out_shape=jax.ShapeDtypeStruct((R, L), F32), in_specs=[vm], out_specs=vm,
        scratch_shapes=[pltpu.VMEM((N_DEV, R, L), F32), pltpu.SemaphoreType.DMA((7,)), pltpu.SemaphoreType.DMA((7,))],
        name="allreduce_small",
    )(v)


def _adamw(name, w, g, m, v):
    shape = w.shape
    C = shape[-1]
    R = int(np.prod(shape[:-1]))
    args = [a.reshape(R, C).astype(F32) for a in (w, g, m, v)]
    tr = _tile_rows(R, 256)

    def body(w_ref, g_ref, m_ref, v_ref, d_ref, nm_ref, nv_ref):
        g = g_ref[...]
        m = ADAM_B1 * m_ref[...] + (1.0 - ADAM_B1) * g
        v = ADAM_B2 * v_ref[...] + (1.0 - ADAM_B2) * (g * g)
        m_hat = m / (1.0 - ADAM_B1 ** ADAM_STEP)
        v_hat = v / (1.0 - ADAM_B2 ** ADAM_STEP)
        d_ref[...] = -ADAM_LR * (m_hat / (jnp.sqrt(v_hat) + ADAM_EPS) + ADAM_WD * w_ref[...])
        nm_ref[...] = m
        nv_ref[...] = v

    spec = pl.BlockSpec((tr, C), lambda i: (i, 0))
    out = pl.pallas_call(
        body, grid=(R // tr,), in_specs=[spec] * 4, out_specs=[spec] * 3,
        out_shape=[jax.ShapeDtypeStruct((R, C), F32)] * 3, name="adamw_" + name, compiler_params=_params(("parallel",)),
    )(*args)
    return [o.reshape(shape) for o in out]


def kernel(x, mem, pre1_g, post1_g, pre2_g, post2_g, mem_norm_g, w_in, fox_f_bias, rwkv_mu, rwkv_w0, rwkv_w_up, rwkv_a0, rwkv_a_up, rwkv_g_up, rwkv_k_k, rwkv_k_a, rwkv_r_k, rwkv_gn_g, rwkv_gn_b, w_mem_kv, w_fox_out, w_rwkv_out, w_mem_out, w_o, w_ffn_gate, w_ffn_up, w_ffn_down, loss_target, m_pre1_g, m_post1_g, m_pre2_g, m_post2_g, m_mem_norm_g, m_w_in, m_fox_f_bias, m_rwkv_mu, m_rwkv_w0, m_rwkv_w_up, m_rwkv_a0, m_rwkv_a_up, m_rwkv_g_up, m_rwkv_k_k, m_rwkv_k_a, m_rwkv_r_k, m_rwkv_gn_g, m_rwkv_gn_b, m_w_mem_kv, m_w_fox_out, m_w_rwkv_out, m_w_mem_out, m_w_o, m_w_ffn_gate, m_w_ffn_up, m_w_ffn_down, v_pre1_g, v_post1_g, v_pre2_g, v_post2_g, v_mem_norm_g, v_w_in, v_fox_f_bias, v_rwkv_mu, v_rwkv_w0, v_rwkv_w_up, v_rwkv_a0, v_rwkv_a_up, v_rwkv_g_up, v_rwkv_k_k, v_rwkv_k_a, v_rwkv_r_k, v_rwkv_gn_g, v_rwkv_gn_b, v_w_mem_kv, v_w_fox_out, v_w_rwkv_out, v_w_mem_out, v_w_o, v_w_ffn_gate, v_w_ffn_up, v_w_ffn_down):
    given = dict(locals())
    w_loc = {n: given[n] for n in WEIGHTS}
    m_loc = {n: given["m_" + n] for n in WEIGHTS}
    v_loc = {n: given["v_" + n] for n in WEIGHTS}

    shard_shapes = {n: tuple(w_loc[n].shape[1:]) for n in BIG}
    layout, half_rows = _wire_layout(shard_shapes)
    mine = _wire_pack([_to_wire(n, w_loc[n][0].astype(BF16)) for n in BIG], half_rows)
    gathered = _gather_weights(mine).reshape(N_CHIPS, 2 * half_rows, WIRE_W)
    W = {}
    for n in BIG:
        off, rows = layout[n]
        blocks = _from_wire(n, gathered[:, off:off + rows], shard_shapes[n])
        if n in LORA:
            W[n] = blocks.transpose(2, 0, 1).reshape(blocks.shape[2], -1)
        else:
            W[n] = blocks.reshape(-1, blocks.shape[2])
    W.update({n: w_loc[n][0] for n in SMALL})

    loss, grad_x, gw = _layer_step(x, mem, loss_target, W)

    blocks = []
    for n in BIG:
        r, c = shard_shapes[n]
        g = gw[n].astype(BF16)
        if n in LORA:
            g = g.reshape(r, N_CHIPS, c).transpose(1, 0, 2)
        elif n in TRANSPOSED:
            g = jnp.swapaxes(g.reshape(N_CHIPS, c, r), 1, 2)
        else:
            g = g.reshape(N_CHIPS, r, c)
        blocks.append(_to_wire(n, g))
    packed = _wire_pack(blocks, half_rows)
    core = lax.axis_index("c")
    own_halves = lax.dynamic_index_in_dim(packed, core, axis=1, keepdims=False)
    chip_sums = _pair_add(own_halves, _pair_exchange(packed))
    own_chip = lax.dynamic_index_in_dim(chip_sums, 2 * lax.axis_index("x") + lax.axis_index("y"), axis=0, keepdims=False)
    half = _sum_chips(own_chip, _scatter_grads(chip_sums))
    other = _swap_halves(half)
    reduced = jnp.where(core == 0, jnp.concatenate([half, other]), jnp.concatenate([other, half]))
    g_shard = {}
    for n in BIG:
        off, rows = layout[n]
        g = _from_wire(n, reduced[off:off + rows], shard_shapes[n])
        g_shard[n] = g if n in ROW_SHARDED else g.T

    small_shapes = [w_loc[n].shape[1:] for n in SMALL] + [(1,)]
    n_small = sum(int(np.prod(s)) for s in small_shapes)
    small_rows = -(-n_small // (8 * LANES)) * 8
    flat = jnp.concatenate([gw[n].reshape(-1) for n in SMALL] + [loss.reshape(1)])
    flat = jnp.pad(flat, (0, small_rows * LANES - n_small)).reshape(small_rows, LANES).reshape(-1)
    small, off = [], 0
    flat = _allreduce_small(flat.reshape(small_rows, LANES)).reshape(-1)
    for s in small_shapes:
        cnt = int(np.prod(s))
        small.append(flat[off:off + cnt].reshape(s))
        off += cnt
    g_small = dict(zip(SMALL, small[:-1]))
    loss = small[-1][0]

    grads, deltas, new_m, new_v = [], [], [], []
    for n in WEIGHTS:
        g = (g_shard[n] if n in g_shard else g_small[n]).reshape(w_loc[n].shape)
        d, nm, nv = _adamw(n, w_loc[n], g, m_loc[n], v_loc[n])
        grads.append(g)
        deltas.append(d)
        new_m.append(nm)
        new_v.append(nv)
    return (loss, grad_x, *grads, *deltas, *new_m, *new_v)
```

```python
import functools
import math

import numpy as np
import jax
import jax.numpy as jnp
from jax import lax
from jax.experimental import pallas as pl
from jax.experimental.pallas import tpu as pltpu

F32, BF16 = jnp.float32, jnp.bfloat16
MESH = pl.DeviceIdType.MESH

D_MODEL = 1024
HEAD_DIM = 64
N_HEADS = 8
BR_W = 512
MEM_HEADS = 4
MEM_HEAD_DIM = 128
D_FF = 2816
NORM_EPS = 1e-6
GN_EPS = 64e-5
N_CHIPS = 4
N_DEV = 8
LANES = 128
VMEM_LIMIT = 48 * 1024 * 1024

ADAM_LR, ADAM_B1, ADAM_B2, ADAM_EPS, ADAM_WD, ADAM_STEP = 0.001, 0.9, 0.999, 1e-08, 0.01, 10

FOX_COLS = 3 * BR_W + N_HEADS
RWKV_COLS = 3 * BR_W + 64 + 64 + 128
COL_QKV = (0, 3 * BR_W)
COL_F = (3 * BR_W, FOX_COLS)
COL_RW = (FOX_COLS, FOX_COLS + RWKV_COLS)
COL_MQ = (COL_RW[1], COL_RW[1] + BR_W)
COL_GATE = (COL_MQ[1], COL_MQ[1] + 3 * D_MODEL)

NT_DIMS = (((1,), (1,)), ((), ()))
TN_DIMS = (((0,), (0,)), ((), ()))


def _params(sem=None, **kw):
    return pltpu.CompilerParams(dimension_semantics=sem, vmem_limit_bytes=VMEM_LIMIT, **kw)


def _sigmoid(x):
    return 1.0 / (1.0 + jnp.exp(-x))


def _log_sigmoid(x):
    return jnp.minimum(x, 0.0) - jnp.log(1.0 + jnp.exp(-jnp.abs(x)))


def _bdot(a, b, dims=None):
    a, b = a.astype(BF16), b.astype(BF16)
    if dims is None:
        return jnp.dot(a, b, preferred_element_type=F32)
    return lax.dot_general(a, b, dims, preferred_element_type=F32)


def _hdot(a, ones):
    ones = ones.astype(BF16)
    hi = a.astype(BF16)
    r1 = a - hi.astype(F32)
    mid = r1.astype(BF16)
    lo = (r1 - mid.astype(F32)).astype(BF16)
    dot = lambda x: jnp.dot(x, ones, preferred_element_type=F32)
    return dot(hi) + dot(mid) + dot(lo)


def _tile(n, cap):
    best = None
    for t in range(LANES, min(n, cap) + 1, LANES):
        if n % t == 0:
            best = t
    return best or n


def _rowwise(name, fn, rows, consts, outs, accs=(), tm=256):
    T = rows[0].shape[0]
    tm = min(tm, T)
    assert T % tm == 0
    nr, nc, no, na = len(rows), len(consts), len(outs), len(accs)

    def body(*refs):
        res = fn(*[r[...].astype(F32) for r in refs[:nr + nc]])
        if not isinstance(res, (tuple, list)):
            res = (res,)
        orefs, arefs = refs[nr + nc:nr + nc + no], refs[nr + nc + no:]
        for ref, val in zip(orefs, res[:no]):
            ref[...] = val.astype(ref.dtype)
        if na:
            @pl.when(pl.program_id(0) == 0)
            def _():
                for ref in arefs:
                    ref[...] = jnp.zeros(ref.shape, ref.dtype)
            for ref, val in zip(arefs, res[no:]):
                ref[...] += val

    in_specs = ([pl.BlockSpec((tm, r.shape[1]), lambda i: (i, 0)) for r in rows]
                + [pl.BlockSpec(c.shape, lambda i: (0, 0)) for c in consts])
    out_specs = ([pl.BlockSpec((tm, w), lambda i: (i, 0)) for w, _ in outs]
                 + [pl.BlockSpec(s, lambda i: (0, 0)) for s, _ in accs])
    out_shape = ([jax.ShapeDtypeStruct((T, w), dt) for w, dt in outs]
                 + [jax.ShapeDtypeStruct(s, dt) for s, dt in accs])
    return pl.pallas_call(
        body, grid=(T // tm,), in_specs=in_specs, out_specs=out_specs, out_shape=out_shape, name=name,
        compiler_params=_params(("arbitrary",) if na else ("parallel",)),
    )(*rows, *consts)


MM_TILE_CAP = 1408
MM_WHOLE_K = 2048


def _mm(name, a, b, ta=False, tb=False, out_dtype=F32, add=None):
    M, K = (a.shape[1], a.shape[0]) if ta else a.shape
    K2, N = (b.shape[1], b.shape[0]) if tb else b.shape
    assert K == K2
    tm, tn = _tile(M, MM_TILE_CAP), _tile(N, MM_TILE_CAP)
    tk = K if K <= MM_WHOLE_K else _tile(K, MM_TILE_CAP)
    assert M % tm == 0 and N % tn == 0 and K % tk == 0
    nk = K // tk
    a_dim, b_dim = (0 if ta else 1), (1 if tb else 0)

    def body(*refs):
        a_ref, b_ref = refs[0], refs[1]
        n_in = 2 if add is None else 3
        o_ref, acc = refs[n_in], (refs[n_in + 1] if nk > 1 else None)
        k = pl.program_id(2)
        part = lax.dot_general(a_ref[...].astype(BF16), b_ref[...].astype(BF16),
                               (((a_dim,), (b_dim,)), ((), ())), preferred_element_type=F32)

        def finish(r):
            if add is not None:
                r = r + refs[2][...].astype(F32)
            o_ref[...] = r.astype(o_ref.dtype)

        if nk == 1:
            finish(part)
            return

        @pl.when(k == 0)
        def _():
            acc[...] = part

        @pl.when(k > 0)
        def _():
            acc[...] += part

        @pl.when(k == nk - 1)
        def _():
            finish(acc[...])

    a_spec = pl.BlockSpec((tk, tm), lambda i, j, k: (k, i)) if ta else pl.BlockSpec((tm, tk), lambda i, j, k: (i, k))
    b_spec = pl.BlockSpec((tn, tk), lambda i, j, k: (j, k)) if tb else pl.BlockSpec((tk, tn), lambda i, j, k: (k, j))
    o_spec = pl.BlockSpec((tm, tn), lambda i, j, k: (i, j))
    ins, in_specs = [a, b], [a_spec, b_spec]
    if add is not None:
        ins.append(add)
        in_specs.append(o_spec)
    return pl.pallas_call(
        body, grid=(M // tm, N // tn, nk), in_specs=in_specs, out_specs=o_spec,
        out_shape=jax.ShapeDtypeStruct((M, N), out_dtype), scratch_shapes=[pltpu.VMEM((tm, tn), F32)] if nk > 1 else [],
        name=name, compiler_params=_params(("parallel", "parallel", "arbitrary")),
    )(*ins)


def _rowsum(x):
    return jnp.sum(x, axis=0, keepdims=True)


def _rms_stat(x):
    return lax.rsqrt(jnp.mean(x * x, axis=-1, keepdims=True) + NORM_EPS)


def _rms_bwd(dy, x, g):
    r = _rms_stat(x)
    xn = x * r
    dxn = dy * g
    dx = r * (dxn - xn * jnp.mean(dxn * xn, axis=-1, keepdims=True))
    return dx, _rowsum(dy * xn)


def _fox_c_fwd(f8t, bias_col, tc=256):
    B, H, S = f8t.shape
    tc = min(tc, S)

    def body(f_ref, b_ref, c_ref, carry):
        @pl.when(pl.program_id(1) == 0)
        def _():
            carry[...] = jnp.zeros(carry.shape, F32)
        lf = _log_sigmoid(f_ref[...] + b_ref[...])
        row = lax.broadcasted_iota(jnp.int32, (tc, tc), 0)
        col = lax.broadcasted_iota(jnp.int32, (tc, tc), 1)
        c = _hdot(lf, (row <= col).astype(F32)) + carry[...]
        c_ref[...] = c
        carry[...] = c[:, tc - 1:tc]

    return pl.pallas_call(
        body, grid=(B, S // tc),
        in_specs=[pl.BlockSpec((None, H, tc), lambda b, i: (b, 0, i)), pl.BlockSpec((H, 1), lambda b, i: (0, 0))],
        out_specs=pl.BlockSpec((None, H, tc), lambda b, i: (b, 0, i)),
        out_shape=jax.ShapeDtypeStruct((B, H, S), F32), scratch_shapes=[pltpu.VMEM((H, 1), F32)], name="fox_c_fwd",
        compiler_params=_params(("parallel", "arbitrary")),
    )(f8t, bias_col)


def _fox_c_bwd(dc, f8t, bias_col, tc=256):
    B, H, S = f8t.shape
    tc = min(tc, S)
    n = S // tc

    def body(dc_ref, f_ref, b_ref, df_ref, db_ref, carry):
        @pl.when(pl.program_id(1) == 0)
        def _():
            carry[...] = jnp.zeros(carry.shape, F32)
            db_ref[...] = jnp.zeros(db_ref.shape, F32)
        row = lax.broadcasted_iota(jnp.int32, (tc, tc), 0)
        col = lax.broadcasted_iota(jnp.int32, (tc, tc), 1)
        dlf = _hdot(dc_ref[...], (row >= col).astype(F32)) + carry[...]
        z = f_ref[...] + b_ref[...]
        df = dlf * (1.0 - _sigmoid(z))
        df_ref[...] = df
        db_ref[...] += jnp.sum(df, axis=1, keepdims=True)
        carry[...] = dlf[:, 0:1]

    rev = lambda b, i: (b, 0, n - 1 - i)
    return pl.pallas_call(
        body, grid=(B, n),
        in_specs=[pl.BlockSpec((None, H, tc), rev), pl.BlockSpec((None, H, tc), rev), pl.BlockSpec((H, 1), lambda b, i: (0, 0))],
        out_specs=[pl.BlockSpec((None, H, tc), rev), pl.BlockSpec((None, H, 1), lambda b, i: (b, 0, 0))],
        out_shape=[jax.ShapeDtypeStruct((B, H, S), F32), jax.ShapeDtypeStruct((B, H, 1), F32)],
        scratch_shapes=[pltpu.VMEM((H, 1), F32)], name="fox_c_bwd",
        compiler_params=_params(("parallel", "arbitrary")),
    )(dc, f8t, bias_col)


NEG_BIG = -1e30


def _fox_logits(q, kj, cq, ckj, i, j, tq, tk, scale):
    s = _bdot(q, kj, NT_DIMS) * scale + (cq - ckj)
    row = lax.broadcasted_iota(jnp.int32, (tq, tk), 0)
    col = lax.broadcasted_iota(jnp.int32, (tq, tk), 1)
    return s, col <= row + (i * tq - j * tk)


def _fox_key_blocks(i, tq, tk):
    return (i * tq + tq - 1) // tk + 1


def _fox_fwd(qkv, cq, ck, B):
    T = qkv.shape[0]
    S = T // B
    (nq, tq), (nk, tk) = cq.shape[1:3], (ck.shape[1], ck.shape[3])
    scale = HEAD_DIM ** -0.5

    def body(q_ref, k_ref, v_ref, cq_ref, ck_ref, o_ref, lse_ref):
        i = pl.program_id(1)
        lo = lax.broadcasted_iota(jnp.int32, (tq, LANES), 1) < HEAD_DIM
        q = q_ref[...]
        qh = (jnp.where(lo, q, 0), jnp.where(lo, 0, q))

        def step(j, carry):
            rows = pl.ds(pl.multiple_of(j * tk, tk), tk)
            kj, vj = k_ref[rows, :], v_ref[rows, :]
            new = []
            for h in range(2):
                m, l, acc = carry[h]
                s, ok = _fox_logits(qh[h], kj, cq_ref[h], ck_ref[h, j], i, j, tq, tk, scale)
                s = jnp.where(ok, s, NEG_BIG)
                m2 = jnp.maximum(m, jnp.max(s, axis=1, keepdims=True))
                p = jnp.exp(s - m2)
                al = jnp.exp(m - m2)
                new.append((m2, al * l + jnp.sum(p, axis=1, keepdims=True), al * acc + _bdot(p, vj)))
            return tuple(new)

        init = tuple((jnp.full((tq, 1), NEG_BIG, F32), jnp.zeros((tq, 1), F32), jnp.zeros((tq, LANES), F32)) for _ in range(2))
        (m0, l0, a0), (m1, l1, a1) = lax.fori_loop(0, _fox_key_blocks(i, tq, tk), step, init)
        o_ref[...] = jnp.where(lo, a0 / l0, a1 / l1).astype(o_ref.dtype)
        lse_ref[0] = m0 + jnp.log(l0)
        lse_ref[1] = m1 + jnp.log(l1)

    seq = lambda col0: pl.BlockSpec((S, LANES), lambda g, i: (g // 4, col0 + g % 4))
    blk = lambda col0: pl.BlockSpec((tq, LANES), lambda g, i: ((g // 4) * nq + i, col0 + g % 4))
    col = pl.BlockSpec((2, None, tq, 1), lambda g, i: (g, i, 0, 0))
    return pl.pallas_call(
        body, grid=(B * 4, nq), in_specs=[blk(0), seq(4), seq(8), col, pl.BlockSpec((2, nk, 1, tk), lambda g, i: (g, 0, 0, 0))],
        out_specs=[blk(0), col],
        out_shape=[jax.ShapeDtypeStruct((T, BR_W), BF16), jax.ShapeDtypeStruct(cq.shape, F32)], name="fox_fwd",
        compiler_params=_params(("parallel", "parallel")),
    )(qkv, qkv, qkv, cq, ck)


def _fox_bwd(qkv, o, do, cq, ck, lse, B):
    T = qkv.shape[0]
    S = T // B
    (nq, tq), (nk, tk) = cq.shape[1:3], (ck.shape[1], ck.shape[3])
    scale = HEAD_DIM ** -0.5

    def body(q_ref, k_ref, v_ref, o_ref, do_ref, cq_ref, ck_ref, lse_ref, dq_ref, dk_ref, dv_ref, dck_ref, dcq_ref,
             dk_acc, dv_acc):
        dk_acc[...] = jnp.zeros(dk_acc.shape, F32)
        dv_acc[...] = jnp.zeros(dv_acc.shape, F32)
        dck_ref[...] = jnp.zeros(dck_ref.shape, F32)
        lo = lax.broadcasted_iota(jnp.int32, (tq, LANES), 1) < HEAD_DIM

        def qloop(i, _):
            qrows = pl.ds(pl.multiple_of(i * tq, tq), tq)
            q, do_i, o_i = q_ref[qrows, :], do_ref[qrows, :], o_ref[qrows, :].astype(F32)
            qh = (jnp.where(lo, q, 0), jnp.where(lo, 0, q))
            doh = (jnp.where(lo, do_i, 0), jnp.where(lo, 0, do_i))
            delta = [jnp.sum(doh[h].astype(F32) * o_i, axis=1, keepdims=True) for h in range(2)]

            def kloop(j, carry):
                krows = pl.ds(pl.multiple_of(j * tk, tk), tk)
                kj, vj = k_ref[krows, :], v_ref[krows, :]
                new = []
                for h in range(2):
                    dq, dcq = carry[h]
                    s, ok = _fox_logits(qh[h], kj, cq_ref[h, i], ck_ref[h, j], i, j, tq, tk, scale)
                    p = jnp.where(ok, jnp.exp(s - lse_ref[h, i]), 0.0)
                    ds = p * (_bdot(doh[h], vj, NT_DIMS) - delta[h])
                    dv_acc[krows, :] += _bdot(p, doh[h], TN_DIMS)
                    dk_acc[krows, :] += _bdot(ds, qh[h], TN_DIMS) * scale
                    dck_ref[h, j] += -_rowsum(ds)
                    new.append((dq + _bdot(ds, kj) * scale, dcq + jnp.sum(ds, axis=1, keepdims=True)))
                return tuple(new)

            init = tuple((jnp.zeros((tq, LANES), F32), jnp.zeros((tq, 1), F32)) for _ in range(2))
            (dq0, dcq0), (dq1, dcq1) = lax.fori_loop(0, _fox_key_blocks(i, tq, tk), kloop, init)
            dq_ref[qrows, :] = jnp.where(lo, dq0, dq1).astype(dq_ref.dtype)
            dcq_ref[0, i] = dcq0
            dcq_ref[1, i] = dcq1
            return 0

        lax.fori_loop(0, nq, qloop, 0)
        dk_ref[...] = dk_acc[...].astype(dk_ref.dtype)
        dv_ref[...] = dv_acc[...].astype(dv_ref.dtype)

    seq = lambda col0: pl.BlockSpec((S, LANES), lambda g: (g // 4, col0 + g % 4))
    col = pl.BlockSpec((2, nq, tq, 1), lambda g: (g, 0, 0, 0))
    row = pl.BlockSpec((2, nk, 1, tk), lambda g: (g, 0, 0, 0))
    out = jax.ShapeDtypeStruct((T, BR_W), BF16)
    return pl.pallas_call(
        body, grid=(B * 4,), in_specs=[seq(0), seq(4), seq(8), seq(0), seq(0), col, row, col],
        out_specs=[seq(0), seq(0), seq(0), row, col],
        out_shape=[out, out, out, jax.ShapeDtypeStruct(ck.shape, F32), jax.ShapeDtypeStruct(cq.shape, F32)],
        scratch_shapes=[pltpu.VMEM((S, LANES), F32), pltpu.VMEM((S, LANES), F32)], name="fox_bwd",
        compiler_params=_params(("parallel",)),
    )(qkv, qkv, qkv, o, do, cq, ck, lse)


def _mem_probs(qh, kh):
    s = _bdot(qh, kh, NT_DIMS) * (MEM_HEAD_DIM ** -0.5)
    e = jnp.exp(s - jnp.max(s, axis=1, keepdims=True))
    return e / jnp.sum(e, axis=1, keepdims=True)


def _mem_fwd(q, mem_kv, B, tq=512):
    T = q.shape[0]
    S, Lm = T // B, mem_kv.shape[0] // B
    tq = min(tq, S)
    n = S // tq

    def body(q_ref, k_ref, v_ref, o_ref):
        for h in range(MEM_HEADS):
            sl = slice(h * MEM_HEAD_DIM, (h + 1) * MEM_HEAD_DIM)
            p = _mem_probs(q_ref[:, sl], k_ref[:, sl])
            o_ref[:, sl] = _bdot(p, v_ref[:, sl]).astype(o_ref.dtype)

    qs = pl.BlockSpec((tq, BR_W), lambda b, i: (b * n + i, 0))
    return pl.pallas_call(
        body, grid=(B, n),
        in_specs=[qs, pl.BlockSpec((Lm, BR_W), lambda b, i: (b, 0)), pl.BlockSpec((Lm, BR_W), lambda b, i: (b, 1))],
        out_specs=qs, out_shape=jax.ShapeDtypeStruct((T, BR_W), BF16), name="mem_fwd",
        compiler_params=_params(("parallel", "parallel")),
    )(q, mem_kv, mem_kv)


def _mem_bwd(q, mem_kv, do, B, tq=512):
    T = q.shape[0]
    S, Lm = T // B, mem_kv.shape[0] // B
    tq = min(tq, S)
    n = S // tq
    scale = MEM_HEAD_DIM ** -0.5

    def body(q_ref, k_ref, v_ref, do_ref, dq_ref, dk_ref, dv_ref):
        @pl.when(pl.program_id(1) == 0)
        def _():
            dk_ref[...] = jnp.zeros(dk_ref.shape, F32)
            dv_ref[...] = jnp.zeros(dv_ref.shape, F32)
        for h in range(MEM_HEADS):
            sl = slice(h * MEM_HEAD_DIM, (h + 1) * MEM_HEAD_DIM)
            qh, kh, vh, doh = q_ref[:, sl], k_ref[:, sl], v_ref[:, sl], do_ref[:, sl]
            p = _mem_probs(qh, kh)
            dp = _bdot(doh, vh, NT_DIMS)
            ds = p * (dp - jnp.sum(p * dp, axis=1, keepdims=True))
            dq_ref[:, sl] = (_bdot(ds, kh) * scale).astype(dq_ref.dtype)
            dk_ref[:, sl] += _bdot(ds, qh, TN_DIMS) * scale
            dv_ref[:, sl] += _bdot(p, doh, TN_DIMS)

    qs = pl.BlockSpec((tq, BR_W), lambda b, i: (b * n + i, 0))
    kv = pl.BlockSpec((Lm, BR_W), lambda b, i: (b, 0))
    return pl.pallas_call(
        body, grid=(B, n),
        in_specs=[qs, kv, pl.BlockSpec((Lm, BR_W), lambda b, i: (b, 1)), qs], out_specs=[qs, kv, kv],
        out_shape=[jax.ShapeDtypeStruct((T, BR_W), BF16), jax.ShapeDtypeStruct((B * Lm, BR_W), F32),
                   jax.ShapeDtypeStruct((B * Lm, BR_W), F32)], name="mem_bwd",
        compiler_params=_params(("parallel", "arbitrary")),
    )(q, mem_kv, mem_kv, do)


def _head_ones():
    h = np.arange(BR_W) // HEAD_DIM
    return jnp.asarray((h[:, None] == h[None, :]).astype(np.float32))


def _rw_prep(p, pp, mu, w0, w1, a0, w2, g_up, k_k, k_a, bd):
    ps = p + (pp - p) * mu
    r, k, v = ps[:, 0:512], ps[:, 512:1024], ps[:, 1024:1536]
    wa, gd = ps[:, 1536:1664], ps[:, 1664:1792]
    th = jnp.tanh(wa)
    z = w0 + _bdot(th, w1)
    wl = -jnp.exp(_log_sigmoid(z) - 0.5)
    w = jnp.exp(wl)
    a = _sigmoid(a0 + _bdot(wa, w2))
    sg = _sigmoid(gd)
    g = _bdot(sg, g_up)
    kq = k * k_k
    n2 = _hdot(kq * kq, bd)
    inv = lax.rsqrt(jnp.maximum(n2, 1e-24))
    kk = kq * inv
    k2 = k * (1.0 + (a - 1.0) * k_a)
    return dict(ps=ps, r=r, k=k, v=v, wa=wa, th=th, z=z, wl=wl, w=w, a=a, sg=sg, g=g, kq=kq, n2=n2, inv=inv, kk=kk, k2=k2)


def _keycol_selector(tm, Tc):
    e = np.zeros((tm, (tm // Tc) * LANES), np.float32)
    for t in range(tm):
        c, tl = divmod(t, Tc)
        e[t, c * LANES + tl] = e[t, c * LANES + Tc + tl] = 1.0
    return jnp.asarray(e, BF16)


def _rw_prep_fwd(p, pp, consts, B, Tc, tm=256):
    assert 2 * Tc == LANES
    T = p.shape[0]
    S = T // B
    nb, cpb = S // tm, tm // Tc
    sel = _keycol_selector(tm, Tc)
    nc = len(consts)

    def body(*refs):
        t = _rw_prep(*[r[...] for r in refs[:2 + nc]])
        sel_ref = refs[2 + nc]
        rows, cols = refs[3 + nc:7 + nc], refs[7 + nc:]
        for ref, val in zip(rows, (t["r"], t["k2"], t["v"], t["g"])):
            ref[...] = val
        lo = lax.broadcasted_iota(jnp.int32, (HEAD_DIM, LANES), 1) < HEAD_DIM
        operands = (t["w"], -t["kk"], t["kk"] * t["a"], t["k2"], t["r"])
        for n, (ref, x) in enumerate(zip(cols, operands)):
            terms = _split_bf16(x) if n == 0 else (x.astype(BF16),)
            for hp in range(4):
                xt = sum(lax.dot_general(tt[:, hp * LANES:(hp + 1) * LANES], sel_ref[...], TN_DIMS,
                                         preferred_element_type=F32) for tt in terms)
                for c in range(cpb):
                    blk = xt[:, c * LANES:(c + 1) * LANES]
                    ref[hp, c] = jnp.where(lo, blk[0:HEAD_DIM], blk[HEAD_DIM:2 * HEAD_DIM]).astype(ref.dtype)

    row_spec = lambda w: pl.BlockSpec((tm, w), lambda i: (i, 0))
    col_spec = pl.BlockSpec((None, 4, cpb, HEAD_DIM, LANES), lambda i: (i // nb, 0, i % nb, 0, 0))
    col_shape = lambda dt: jax.ShapeDtypeStruct((B, 4, S // Tc, HEAD_DIM, LANES), dt)
    out = pl.pallas_call(
        body, grid=(T // tm,),
        in_specs=[row_spec(RWKV_COLS)] * 2 + [pl.BlockSpec(c.shape, lambda i: (0, 0)) for c in consts]
        + [pl.BlockSpec(sel.shape, lambda i: (0, 0))],
        out_specs=[row_spec(BR_W)] * 4 + [col_spec] * 5,
        out_shape=[jax.ShapeDtypeStruct((T, BR_W), F32)] * 4 + [col_shape(F32)] + [col_shape(BF16)] * 4,
        name="rwkv_prep_fwd", compiler_params=_params(("parallel",)),
    )(p, pp, *consts, sel)
    return out[:4], [c.reshape(B * 4, S // Tc, HEAD_DIM, LANES) for c in out[4:]]


def _rw_prep_bwd(p, pp, cots, consts):
    def fn(p, pp, dr1, dr2, dw, dk21, dk22, dv1, dv2, dav, dbv, dg, mu, w0, w1, a0, w2, g_up, k_k, k_a, bd):
        t = _rw_prep(p, pp, mu, w0, w1, a0, w2, g_up, k_k, k_a, bd)
        dr, dk2, dv = dr1 + dr2, dk21 + dk22, dv1 + dv2
        a, k, kk, kq, inv = t["a"], t["k"], t["kk"], t["kq"], t["inv"]
        dkk = dbv * a - dav
        da = dbv * kk + dk2 * k * k_a
        dk = dk2 * (1.0 + (a - 1.0) * k_a)
        d_k_a = _rowsum(dk2 * k * (a - 1.0))
        proj = _hdot(dkk * kq, bd)
        dkq = dkk * inv - jnp.where(t["n2"] > 1e-24, kq * inv * inv * inv * proj, 0.0)
        dk = dk + dkq * k_k
        d_k_k = _rowsum(dkq * k)
        dpa = da * a * (1.0 - a)
        d_a0 = _rowsum(dpa)
        dwa = _bdot(dpa, w2, NT_DIMS)
        d_w2 = _bdot(t["wa"], dpa, TN_DIMS)
        dz = dw * t["w"] * t["wl"] * (1.0 - _sigmoid(t["z"]))
        d_w0 = _rowsum(dz)
        th = t["th"]
        dwa = dwa + _bdot(dz, w1, NT_DIMS) * (1.0 - th * th)
        d_w1 = _bdot(th, dz, TN_DIMS)
        sg = t["sg"]
        dgd = _bdot(dg, g_up, NT_DIMS) * sg * (1.0 - sg)
        d_g_up = _bdot(sg, dg, TN_DIMS)
        dps = jnp.concatenate([dr, dk, dv, dwa, dgd], axis=1)
        d_mu = _rowsum(dps * (pp - p))
        return dps * (1.0 - mu), dps * mu, d_mu, d_w0, d_w1, d_a0, d_w2, d_g_up, d_k_k, d_k_a

    accs = [((1, RWKV_COLS), F32), ((1, BR_W), F32), ((LANES, BR_W), F32), ((1, BR_W), F32), ((LANES, BR_W), F32),
            ((LANES, BR_W), F32), ((1, BR_W), F32), ((1, BR_W), F32)]
    return _rowwise("rwkv_prep_bwd", fn, [p, pp] + list(cots), consts, [(RWKV_COLS, F32)] * 2, accs, tm=128)


def _rw_head(y, r, k2, v, g, gn_g, gn_b, r_k, bd):
    mean = _hdot(y, bd) * (1.0 / HEAD_DIM)
    yc = y - mean
    rs = lax.rsqrt(_hdot(yc * yc, bd) * (1.0 / HEAD_DIM) + GN_EPS)
    yn = yc * rs
    bs = _hdot(r * k2 * r_k, bd)
    return yn, rs, bs, yn * gn_g + gn_b + bs * v


def _rw_head_fwd(y, r, k2, v, g, consts):
    def fn(y, r, k2, v, g, *c):
        return _rw_head(y, r, k2, v, g, *c)[3] * g
    return _rowwise("rwkv_head_fwd", fn, [y, r, k2, v, g], consts, [(BR_W, BF16)])[0]


def _rw_head_bwd(dout, y, r, k2, v, g, consts):
    def fn(dout, y, r, k2, v, g, gn_g, gn_b, r_k, bd):
        dout = dout.astype(F32)
        yn, rs, bs, zz = _rw_head(y, r, k2, v, g, gn_g, gn_b, r_k, bd)
        dg = dout * zz
        dz = dout * g
        dyn = dz * gn_g
        inv_n = 1.0 / HEAD_DIM
        dy = rs * (dyn - _hdot(dyn, bd) * inv_n - yn * (_hdot(dyn * yn, bd) * inv_n))
        dq = _hdot(dz * v, bd)
        return dy, dg, dq * k2 * r_k, dq * r * r_k, dz * bs, _rowsum(dz * yn), _rowsum(dz), _rowsum(dq * r * k2)
    return _rowwise("rwkv_head_bwd", fn, [dout, y, r, k2, v, g], consts, [(BR_W, F32)] * 5, [((1, BR_W), F32)] * 3)


SCAN_TC = 64


def _scan_onehot(Tc):
    w = np.zeros((Tc // 2, 2 * Tc, 2 * LANES), np.float32)
    for tt in range(Tc // 2):
        for u in range(2):
            for h in range(2):
                w[tt, h * Tc + 2 * tt + u, u * LANES + h * HEAD_DIM: u * LANES + (h + 1) * HEAD_DIM] = 1.0
    return jnp.asarray(w, BF16)


def _split_bf16(x):
    hi = x.astype(BF16)
    return hi, (x - hi.astype(F32)).astype(BF16)


def _key_tiles(l_w, others, onehot):
    dot = lambda x: jnp.dot(x, onehot, preferred_element_type=F32)
    whi, wmid = l_w
    return [dot(whi) + dot(wmid)] + [dot(o) for o in others]


def _rw_scan_fwd(LW, LA, LB, LK, LR, v, P=2):
    NP, nc, _, Tc2 = LW.shape
    Tc = Tc2 // 2
    S = nc * Tc
    onehot = _scan_onehot(Tc)
    npb = 4 // P

    def body(lw, la, lb, lk, lr, v_ref, oh_ref, y_ref, sa_ref, sb_ref, st):
        @pl.when(pl.program_id(1) == 0)
        def _():
            st[...] = jnp.zeros(st.shape, F32)
        s = [st[p] for p in range(P)]
        cols = [(_split_bf16(lw[p]), [ref[p].astype(BF16) for ref in (la, lb, lk, lr)]) for p in range(P)]
        for tt in range(Tc // 2):
            tiles = [_key_tiles(c[0], c[1], oh_ref[tt]) for c in cols]
            for u in range(2):
                t = 2 * tt + u
                for p in range(P):
                    W, A, Bt, Kt, R = (x[:, u * LANES:(u + 1) * LANES] for x in tiles[p])
                    ls = slice(p * LANES, (p + 1) * LANES)
                    sb_ref[p, t] = s[p]
                    sa = _rowsum(s[p] * A)
                    s[p] = s[p] * W + Bt * sa + Kt * v_ref[t:t + 1, ls]
                    y_ref[t:t + 1, ls] = _rowsum(s[p] * R)
                    sa_ref[t:t + 1, ls] = sa
        for p in range(P):
            st[p] = s[p]

    lspec = pl.BlockSpec((P, None, HEAD_DIM, Tc2), lambda g, c: (g, c, 0, 0))
    rows = pl.BlockSpec((Tc, P * LANES), lambda g, c: ((g // npb) * nc + c, g % npb))
    rowshape = jax.ShapeDtypeStruct(v.shape, F32)
    return pl.pallas_call(
        body, grid=(NP // P, nc), in_specs=[lspec] * 5 + [rows, pl.BlockSpec(onehot.shape, lambda g, c: (0, 0, 0))],
        out_specs=[rows, rows, pl.BlockSpec((P, Tc, HEAD_DIM, LANES), lambda g, c: (g, c, 0, 0))],
        out_shape=[rowshape, rowshape, jax.ShapeDtypeStruct((NP, S, HEAD_DIM, LANES), F32)],
        scratch_shapes=[pltpu.VMEM((P, HEAD_DIM, LANES), F32)], name="rwkv_scan_fwd",
        compiler_params=_params(("parallel", "arbitrary")),
    )(LW, LA, LB, LK, LR, v, onehot)


SCAN_G_ROWS = 16


def _rw_scan_bwd(LW, LA, LB, LK, LR, v, sa, dy, sb, P=4):
    NP, nc, _, Tc2 = LW.shape
    Tc = Tc2 // 2
    onehot = _scan_onehot(Tc)
    npb = 4 // P

    def body(lw, la, lb, lk, lr, v_ref, sa_ref, dy_ref, sb_ref, oh_ref, dv_ref, dk_ref, db_ref, dw_ref, dr_ref, da_ref, dst):
        @pl.when(pl.program_id(1) == 0)
        def _():
            dst[...] = jnp.zeros(dst.shape, F32)
        rid = lax.broadcasted_iota(jnp.int32, (SCAN_G_ROWS, LANES), 0)
        lane = lax.broadcasted_iota(jnp.int32, (SCAN_G_ROWS, LANES), 1)
        own = (((rid % 2) == 0) == (lane < HEAD_DIM)) & (rid < 10)
        lo = lane[0:1] < HEAD_DIM
        nt = lambda rows, tile: lax.dot_general(rows.astype(BF16), tile.astype(BF16), NT_DIMS, preferred_element_type=F32)
        ds = [dst[p] for p in range(P)]
        cols = [(_split_bf16(lw[p]), [ref[p].astype(BF16) for ref in (la, lb, lk, lr)]) for p in range(P)]
        for tt in reversed(range(Tc // 2)):
            tiles = [_key_tiles(c[0], c[1], oh_ref[tt]) for c in cols]
            for u in (1, 0):
                t = 2 * tt + u
                for p in range(P):
                    W, A, Bt, Kt, R = (x[:, u * LANES:(u + 1) * LANES] for x in tiles[p])
                    ls = slice(p * LANES, (p + 1) * LANES)
                    vr, sar, dyr = (ref[t:t + 1, ls] for ref in (v_ref, sa_ref, dy_ref))
                    sp = sb_ref[p, t]
                    s_t = sp * W + Bt * sar + Kt * vr
                    d = ds[p] + R * dyr
                    dv_ref[t:t + 1, ls] = _rowsum(d * Kt)
                    dsar = _rowsum(d * Bt)
                    rows = jnp.where(rid < 2, vr, jnp.where(rid < 4, sar, jnp.where(rid < 6, 1.0, jnp.where(rid < 8, dyr, dsar))))
                    g = nt(jnp.where(own, rows, 0.0), jnp.concatenate([d, d * sp, s_t, sp], axis=0))
                    ga, gb = g[:, 0:LANES], g[:, LANES:2 * LANES]
                    ra, rb = pltpu.roll(ga, HEAD_DIM, 1), pltpu.roll(gb, HEAD_DIM, 1)
                    dk_ref[t:t + 1, ls] = jnp.where(lo, ga[0:1], ra[1:2])
                    db_ref[t:t + 1, ls] = jnp.where(lo, ga[2:3], ra[3:4])
                    dw_ref[t:t + 1, ls] = jnp.where(lo, ra[4:5], ga[5:6])
                    dr_ref[t:t + 1, ls] = jnp.where(lo, gb[6:7], rb[7:8])
                    da_ref[t:t + 1, ls] = jnp.where(lo, rb[8:9], gb[9:10])
                    ds[p] = d * W + A * dsar
        for p in range(P):
            dst[p] = ds[p]

    rev = lambda g, c: (g, nc - 1 - c, 0, 0)
    lspec = pl.BlockSpec((P, None, HEAD_DIM, Tc2), rev)
    rows = pl.BlockSpec((Tc, P * LANES), lambda g, c: ((g // npb) * nc + nc - 1 - c, g % npb))
    return pl.pallas_call(
        body, grid=(NP // P, nc),
        in_specs=[lspec] * 5 + [rows] * 3 + [pl.BlockSpec((P, Tc, HEAD_DIM, LANES), rev),
                                             pl.BlockSpec(onehot.shape, lambda g, c: (0, 0, 0))],
        out_specs=[rows] * 6, out_shape=[jax.ShapeDtypeStruct(v.shape, F32)] * 6,
        scratch_shapes=[pltpu.VMEM((P, HEAD_DIM, LANES), F32)], name="rwkv_scan_bwd",
        compiler_params=_params(("parallel", "arbitrary")),
    )(LW, LA, LB, LK, LR, v, sa, dy, sb, onehot)


def _shift_prev(p, B):
    T, W = p.shape
    return jnp.pad(p.reshape(B, T // B, W), ((0, 0), (1, 0), (0, 0)))[:, :-1].reshape(T, W)


def _shift_next(p, B):
    T, W = p.shape
    return jnp.pad(p.reshape(B, T // B, W), ((0, 0), (0, 1), (0, 0)))[:, 1:].reshape(T, W)


FOX_FWD_BLOCKS = (512, 1024)
FOX_BWD_BLOCKS = (512, 512)


def _layer_step(x, mem, target, W, late_weights=None, early_grads=None, scan_tc=SCAN_TC, fox_fwd_t=FOX_FWD_BLOCKS,
                fox_bwd_t=FOX_BWD_BLOCKS):
    B, S, _ = x.shape
    T = B * S
    x2, tgt2 = x.reshape(T, D_MODEL), target.reshape(T, D_MODEL)
    mem2 = mem.reshape(-1, D_MODEL)
    w_in_t = W["w_in"]
    wt_qkv, wt_rw, wt_mq, wt_gate = (w_in_t[lo:hi] for lo, hi in (COL_QKV, COL_RW, COL_MQ, COL_GATE))
    wt_f = jnp.pad(w_in_t[COL_F[0]:COL_F[1]], ((0, LANES - N_HEADS), (0, 0)))
    row = lambda v: v.reshape(1, -1).astype(F32)
    pre1_g, post1_g, pre2_g, post2_g, mem_g = (row(W[n]) for n in ("pre1_g", "post1_g", "pre2_g", "post2_g", "mem_norm_g"))

    u = _rowwise("rms_pre1", lambda x, g: x * _rms_stat(x) * g, [x2], [pre1_g], [(D_MODEL, BF16)])[0]
    qkv = _mm("proj_qkv", u, wt_qkv, tb=True, out_dtype=BF16)
    f_pad = _mm("proj_f", u, wt_f, tb=True)
    p_rw = _mm("proj_rwkv", u, wt_rw, tb=True)
    memq = _mm("proj_memq", u, wt_mq, tb=True, out_dtype=BF16)
    gate = _mm("proj_gate", u, wt_gate, tb=True, out_dtype=BF16)

    bias_col = W["fox_f_bias"].reshape(N_HEADS, 1).astype(F32)
    f8t = f_pad[:, :N_HEADS].reshape(B, S, N_HEADS).transpose(0, 2, 1)
    c = _fox_c_fwd(f8t, bias_col)
    G = B * N_HEADS
    q_blocks = lambda a, t: a.reshape(G, S // min(t, S), min(t, S), 1)
    k_blocks = lambda a, t: a.reshape(G, S // min(t, S), 1, min(t, S))
    c_col = c.reshape(G, S, 1)
    fox_out, lse = _fox_fwd(qkv, q_blocks(c_col, fox_fwd_t[0]), k_blocks(c, fox_fwd_t[1]), B)

    bd = _head_ones()
    zpad = jnp.zeros((64, BR_W), F32)
    w1 = jnp.concatenate([W["rwkv_w_up"].astype(F32), zpad], axis=0)
    w2 = jnp.concatenate([zpad, W["rwkv_a_up"].astype(F32)], axis=0)
    prep_consts = [row(W["rwkv_mu"]), row(W["rwkv_w0"]), w1, row(W["rwkv_a0"]), w2, W["rwkv_g_up"].astype(F32),
                   row(W["rwkv_k_k"]), row(W["rwkv_k_a"]), bd]
    p_prev = _shift_prev(p_rw, B)
    (rr, rk2, rv, rg), scan_cols = _rw_prep_fwd(p_rw, p_prev, prep_consts, B, scan_tc)
    ry, rsa, sb = _rw_scan_fwd(*scan_cols, rv)
    head_consts = [row(W["rwkv_gn_g"]), row(W["rwkv_gn_b"]), row(W["rwkv_r_k"]), bd]
    rwkv_out = _rw_head_fwd(ry, rr, rk2, rv, rg, head_consts)

    if late_weights is not None:
        W = {**W, **late_weights(rwkv_out)}

    mn = _rowwise("rms_mem", lambda m, g: m * _rms_stat(m) * g, [mem2], [mem_g], [(D_MODEL, BF16)])[0]
    mem_kv = _mm("proj_memkv", mn, W["w_mem_kv"], out_dtype=BF16)
    mem_out = _mem_fwd(memq, mem_kv, B)

    fo = [_mm("branch_" + n, a, W[n], tb=True, out_dtype=BF16)
          for n, a in (("w_fox_out", fox_out), ("w_rwkv_out", rwkv_out), ("w_mem_out", mem_out))]

    def merge(gate, f0, f1, f2):
        return sum(_sigmoid(gate[:, i * D_MODEL:(i + 1) * D_MODEL]) * f for i, f in enumerate((f0, f1, f2)))
    merged = _rowwise("merge", merge, [gate] + fo, [], [(D_MODEL, BF16)])[0]
    y1 = _mm("proj_o", merged, W["w_o"])

    def mid(x, y1, g1, g2):
        h1 = x + y1 * _rms_stat(y1) * g1
        return h1, h1 * _rms_stat(h1) * g2
    h1, u2 = _rowwise("norm_mid", mid, [x2, y1], [post1_g, pre2_g], [(D_MODEL, F32), (D_MODEL, BF16)])
    gt = _mm("ffn_gate", u2, W["w_ffn_gate"], tb=True, out_dtype=BF16)
    up = _mm("ffn_up", u2, W["w_ffn_up"], tb=True, out_dtype=BF16)
    act = _rowwise("swiglu", lambda gt, up: gt * _sigmoid(gt) * up, [gt, up], [], [(D_FF, BF16)])[0]
    ffn = _mm("ffn_down", act, W["w_ffn_down"])

    def tail(h1, ffn, tgt, g):
        err = h1 + ffn * _rms_stat(ffn) * g - tgt
        dh2 = err * (1.0 / D_MODEL)
        dffn, dg = _rms_bwd(dh2, ffn, g)
        loss = 0.5 * jnp.sum(jnp.sum(err * err, axis=1, keepdims=True) * (1.0 / D_MODEL), axis=0, keepdims=True)
        return dh2, dffn, dg, jnp.broadcast_to(loss, (1, LANES))
    dh2, dffn, d_post2, loss = _rowwise("loss_tail", tail, [h1, ffn, tgt2], [post2_g], [(D_MODEL, F32), (D_MODEL, BF16)],
                                        [((1, D_MODEL), F32), ((1, LANES), F32)])
    gw = {"post2_g": d_post2}
    dact = _mm("d_act", dffn, W["w_ffn_down"], tb=True, out_dtype=BF16)
    gw["w_ffn_down"] = _mm("g_ffn_down", act, dffn, ta=True, out_dtype=BF16)

    def swiglu_bwd(dact, gt, up):
        s = _sigmoid(gt)
        return dact * up * s * (1.0 + gt * (1.0 - s)), dact * gt * s
    dgt, dup = _rowwise("swiglu_bwd", swiglu_bwd, [dact, gt, up], [], [(D_FF, BF16)] * 2)
    du2 = _mm("d_u2_gate", dgt, W["w_ffn_gate"])
    du2 = _mm("d_u2_up", dup, W["w_ffn_up"], add=du2)
    gw["w_ffn_gate"] = _mm("g_ffn_gate", dgt, u2, ta=True, out_dtype=BF16)
    gw["w_ffn_up"] = _mm("g_ffn_up", dup, u2, ta=True, out_dtype=BF16)

    def mid_bwd(du2, dh2, h1, y1, g1, g2):
        dh1_n, d_pre2 = _rms_bwd(du2, h1, g2)
        dh1 = dh2 + dh1_n
        dy1, d_post1 = _rms_bwd(dh1, y1, g1)
        return dh1, dy1, d_post1, d_pre2
    dh1, dy1, gw["post1_g"], gw["pre2_g"] = _rowwise(
        "norm_mid_bwd", mid_bwd, [du2, dh2, h1, y1], [post1_g, pre2_g], [(D_MODEL, F32), (D_MODEL, BF16)],
        [((1, D_MODEL), F32)] * 2)
    dmerged = _mm("d_merged", dy1, W["w_o"], tb=True, out_dtype=BF16)
    gw["w_o"] = _mm("g_w_o", merged, dy1, ta=True, out_dtype=BF16)

    def merge_bwd(dm, gate, f0, f1, f2):
        s = [_sigmoid(gate[:, i * D_MODEL:(i + 1) * D_MODEL]) for i in range(3)]
        dgate = jnp.concatenate([dm * f * si * (1.0 - si) for f, si in zip((f0, f1, f2), s)], axis=1)
        return dm * s[0], dm * s[1], dm * s[2], dgate
    dfo0, dfo1, dfo2, dgate = _rowwise("merge_bwd", merge_bwd, [dmerged, gate] + fo, [],
                                       [(D_MODEL, BF16)] * 3 + [(3 * D_MODEL, BF16)])
    d_branch = {}
    for n, a, dfo in (("w_fox_out", fox_out, dfo0), ("w_rwkv_out", rwkv_out, dfo1), ("w_mem_out", mem_out, dfo2)):
        d_branch[n] = _mm("d_in_" + n, dfo, W[n], out_dtype=BF16)
        gw[n] = _mm("g_" + n, dfo, a, ta=True, out_dtype=BF16)

    dmemq, dkm, dvm = _mem_bwd(memq, mem_kv, d_branch["w_mem_out"], B)
    dmem_kv = jnp.concatenate([dkm, dvm], axis=1)
    gw["w_mem_kv"] = _mm("g_w_mem_kv", mn, dmem_kv, ta=True, out_dtype=BF16)
    dmn = _mm("d_mn", dmem_kv, W["w_mem_kv"], tb=True)
    gw["mem_norm_g"] = _rowwise("rms_mem_bwd", lambda d, m, g: _rms_bwd(d, m, g)[1], [dmn, mem2], [mem_g], [],
                                [((1, D_MODEL), F32)])[0]
    if early_grads is not None:
        d_branch["w_fox_out"] = early_grads(gw, d_branch["w_fox_out"])

    dfq, dfk, dfv, dck, dcq = _fox_bwd(qkv, fox_out, d_branch["w_fox_out"], q_blocks(c_col, fox_bwd_t[0]),
                                       k_blocks(c, fox_bwd_t[1]), q_blocks(lse, fox_bwd_t[0]), B)
    df8t, dbias = _fox_c_bwd(dck.reshape(B, N_HEADS, S) + dcq.reshape(B, N_HEADS, S), f8t, bias_col)
    gw["fox_f_bias"] = jnp.sum(dbias, axis=0).reshape(1, N_HEADS)
    dqkv = jnp.concatenate([dfq, dfk, dfv], axis=1)
    df_pad = jnp.pad(df8t.transpose(0, 2, 1).reshape(T, N_HEADS), ((0, 0), (0, LANES - N_HEADS))).astype(BF16)

    dry, drg, dr_h, dk2_h, dv_h, gw["rwkv_gn_g"], gw["rwkv_gn_b"], gw["rwkv_r_k"] = _rw_head_bwd(
        d_branch["w_rwkv_out"], ry, rr, rk2, rv, rg, head_consts)
    dv_s, dk2_s, db_s, dw_s, dr_s, da_s = _rw_scan_bwd(*scan_cols, rv, rsa, dry, sb)
    dP, dPp, gw["rwkv_mu"], gw["rwkv_w0"], d_w1, gw["rwkv_a0"], d_w2, gw["rwkv_g_up"], gw["rwkv_k_k"], gw["rwkv_k_a"] = \
        _rw_prep_bwd(p_rw, p_prev, [dr_s, dr_h, dw_s, dk2_s, dk2_h, dv_s, dv_h, da_s, db_s, drg], prep_consts)
    gw["rwkv_w_up"], gw["rwkv_a_up"] = d_w1[:64], d_w2[64:]
    dp_rw = (dP + _shift_next(dPp, B)).astype(BF16)

    du = _mm("d_u_qkv", dqkv, wt_qkv)
    du = _mm("d_u_f", df_pad, wt_f, add=du)
    du = _mm("d_u_rwkv", dp_rw, wt_rw, add=du)
    du = _mm("d_u_memq", dmemq, wt_mq, add=du)
    du = _mm("d_u_gate", dgate, wt_gate, add=du)
    gw["w_in"] = jnp.concatenate(
        [_mm("g_w_qkv", dqkv, u, ta=True, out_dtype=BF16), _mm("g_w_f", df_pad, u, ta=True, out_dtype=BF16)[:N_HEADS],
         _mm("g_w_rwkv", dp_rw, u, ta=True, out_dtype=BF16), _mm("g_w_memq", dmemq, u, ta=True, out_dtype=BF16),
         _mm("g_w_gate", dgate, u, ta=True, out_dtype=BF16)], axis=0)

    def pre1_bwd(du, dh1, x, g):
        dx, dg = _rms_bwd(du, x, g)
        return dh1 + dx, dg
    dx, gw["pre1_g"] = _rowwise("rms_pre1_bwd", pre1_bwd, [du, dh1, x2], [pre1_g], [(D_MODEL, F32)], [((1, D_MODEL), F32)])
    return loss[0, 0], dx.reshape(B, S, D_MODEL), gw


TRANSPOSED = ("w_in", "w_ffn_gate", "w_ffn_up", "w_fox_out", "w_rwkv_out", "w_mem_out")
LORA = ("rwkv_w_up", "rwkv_a_up", "rwkv_g_up")
ROW_SHARDED = ("w_mem_kv", "w_o", "w_ffn_down")
FIRST = ("w_in",) + LORA
LATER = ("w_ffn_gate", "w_ffn_up", "w_mem_kv", "w_o", "w_ffn_down", "w_fox_out", "w_rwkv_out", "w_mem_out")
BIG = FIRST + LATER
SMALL = ("pre1_g", "post1_g", "pre2_g", "post2_g", "mem_norm_g", "fox_f_bias", "rwkv_mu", "rwkv_w0", "rwkv_a0", "rwkv_k_k",
         "rwkv_k_a", "rwkv_r_k", "rwkv_gn_g", "rwkv_gn_b")
WEIGHTS = ("pre1_g", "post1_g", "pre2_g", "post2_g", "mem_norm_g", "w_in", "fox_f_bias", "rwkv_mu", "rwkv_w0", "rwkv_w_up",
           "rwkv_a0", "rwkv_a_up", "rwkv_g_up", "rwkv_k_k", "rwkv_k_a", "rwkv_r_k", "rwkv_gn_g", "rwkv_gn_b", "w_mem_kv",
           "w_fox_out", "w_rwkv_out", "w_mem_out", "w_o", "w_ffn_gate", "w_ffn_up", "w_ffn_down")
WIRE_W = 1024
WIRE_ROW_ALIGN = 16
WIRE_HALF_ALIGN = 128


def _wire_rows(name, shard_shape):
    r, c = shard_shape
    if name in ROW_SHARDED:
        return r
    return -(-c // WIRE_ROW_ALIGN) * WIRE_ROW_ALIGN if r == WIRE_W else (r * c) // WIRE_W


def _to_wire(name, a):
    if name not in ROW_SHARDED:
        a = jnp.swapaxes(a, -1, -2)
    lead, (n, w) = a.shape[:-2], a.shape[-2:]
    if w != WIRE_W:
        return a.reshape(lead + ((n * w) // WIRE_W, WIRE_W))
    return jnp.pad(a, [(0, 0)] * len(lead) + [(0, (-n) % WIRE_ROW_ALIGN), (0, 0)])


def _from_wire(name, a, shard_shape):
    r, c = shard_shape
    if name in ROW_SHARDED:
        return a
    return a[..., :c, :] if r == WIRE_W else a.reshape(a.shape[:-2] + (c, r))


def _wire_layout(shard_shapes, names):
    layout, off = {}, 0
    for n in names:
        rows = _wire_rows(n, shard_shapes[n])
        layout[n] = (off, rows)
        off += rows
    return layout, -(-off // (2 * WIRE_HALF_ALIGN)) * WIRE_HALF_ALIGN


def _wire_pack(blocks, half_rows):
    a = jnp.concatenate(blocks, axis=-2)
    lead = a.shape[:-2]
    a = jnp.pad(a, [(0, 0)] * len(lead) + [(0, 2 * half_rows - a.shape[-2]), (0, 0)])
    return a.reshape(lead + (2, half_rows, WIRE_W))


def _my_place():
    return lax.axis_index("x"), lax.axis_index("y"), lax.axis_index("c")


def _other_chips(x, y):
    return [(1 - x, y), (x, 1 - y), (1 - x, 1 - y)]


ANY = pl.BlockSpec(memory_space=pl.ANY)


def _gather_weights(packed):
    _, R, L = packed.shape
    me = 2 * lax.axis_index("x") + lax.axis_index("y")
    base = lax.dynamic_update_index_in_dim(jnp.zeros((N_CHIPS, 2, R, L), packed.dtype), packed, me, 0)

    def body(in_ref, base_ref, out_ref, send_sems, recv_sems):
        x, y, c = _my_place()
        chip = lambda px, py: 2 * px + py
        sibling = (x, y, 1 - c)
        others = _other_chips(x, y)

        def copy(k, src, dst, to):
            return pltpu.make_async_remote_copy(src_ref=src, dst_ref=dst, send_sem=send_sems.at[k], recv_sem=recv_sems.at[k],
                                                device_id=to, device_id_type=MESH)

        sends = [copy(j, in_ref.at[c], out_ref.at[chip(x, y), c], (px, py, c)) for j, (px, py) in enumerate(others)]
        for cp in sends:
            cp.start()
        passed = [copy(3 + j, out_ref.at[chip(px, py), c], out_ref.at[chip(px, py), c], sibling)
                  for j, (px, py) in enumerate(others)]
        for j, (px, py) in enumerate(others):
            copy(j, in_ref.at[c], out_ref.at[chip(px, py), c], (px, py, c)).wait_recv()
            passed[j].start()
        for j, (px, py) in enumerate(others):
            copy(3 + j, in_ref.at[1 - c], out_ref.at[chip(px, py), 1 - c], sibling).wait_recv()
        for cp in sends + passed:
            cp.wait_send()

    return pl.pallas_call(
        body, out_shape=jax.ShapeDtypeStruct(base.shape, base.dtype), in_specs=[ANY, ANY], out_specs=ANY,
        input_output_aliases={1: 0}, scratch_shapes=[pltpu.SemaphoreType.DMA((6,)), pltpu.SemaphoreType.DMA((6,))],
        name="gather_weights",
    )(packed, base)


HBM_SPEC = pl.BlockSpec(memory_space=pltpu.HBM)
SEM_SPEC = pl.BlockSpec(memory_space=pltpu.SEMAPHORE)
DATAFLOW = pltpu.SideEffectType.DATAFLOW_SIDE_EFFECTING


def _in_hbm(a):
    return pltpu.with_memory_space_constraint(a, pltpu.HBM)


def _split_start(name, n_copies, src, land, start_copies):
    def body(src_ref, land_ref, send_sems, recv_sems, src_thru, land_thru, token):
        start_copies(src_ref, land_ref, send_sems, recv_sems)
        token[...] = jnp.zeros(token.shape, token.dtype)

    return pl.pallas_call(
        body, name=name,
        out_shape=(pltpu.SemaphoreType.DMA((n_copies,)), pltpu.SemaphoreType.DMA((n_copies,)), pltpu.HBM(src.shape, src.dtype),
                   pltpu.HBM(land.shape, land.dtype), jax.ShapeDtypeStruct((8, LANES), F32)),
        in_specs=(HBM_SPEC, HBM_SPEC),
        out_specs=(SEM_SPEC, SEM_SPEC, HBM_SPEC, HBM_SPEC, pl.BlockSpec(memory_space=pltpu.VMEM)),
        input_output_aliases={0: 2, 1: 3}, compiler_params=pltpu.CompilerParams(has_side_effects=DATAFLOW),
    )(_in_hbm(src), _in_hbm(land))


def _split_wait(name, handle, after, wait_copies):
    send_sems, recv_sems, src, land = handle[:4]

    def body(src_ref, land_ref, send_sems, recv_sems, after_ref, src_dead, land_out):
        wait_copies(src_ref, land_ref, send_sems, recv_sems)

    return pl.pallas_call(
        body, name=name, out_shape=(pltpu.HBM(src.shape, src.dtype), pltpu.HBM(land.shape, land.dtype)),
        in_specs=(HBM_SPEC, HBM_SPEC, SEM_SPEC, SEM_SPEC, ANY), out_specs=(HBM_SPEC, HBM_SPEC),
        input_output_aliases={0: 0, 1: 1}, compiler_params=pltpu.CompilerParams(has_side_effects=DATAFLOW),
    )(src, land, send_sems, recv_sems, after)[1]


def _late_gather_copies(src_ref, land_ref, send_sems, recv_sems):
    x, y, c = _my_place()
    me = 2 * x + y
    started, awaited = [], []
    for j, (px, py) in enumerate(_other_chips(x, y)):
        for core in range(2):
            started.append(pltpu.make_async_remote_copy(
                src_ref=src_ref.at[c], dst_ref=land_ref.at[me, c], send_sem=send_sems.at[2 * j + core],
                recv_sem=recv_sems.at[2 * j + c], device_id=(px, py, core), device_id_type=MESH))
            awaited.append(pltpu.make_async_remote_copy(
                src_ref=src_ref.at[core], dst_ref=land_ref.at[2 * px + py, core], send_sem=send_sems.at[2 * j + core],
                recv_sem=recv_sems.at[2 * j + core], device_id=(px, py, core), device_id_type=MESH))
    return started, awaited


def _gather_late_start(packed):
    _, R, L = packed.shape
    me = 2 * lax.axis_index("x") + lax.axis_index("y")
    land = lax.dynamic_update_index_in_dim(jnp.zeros((N_CHIPS, 2, R, L), packed.dtype), packed, me, 0)

    def start(src_ref, land_ref, send_sems, recv_sems):
        for cp in _late_gather_copies(src_ref, land_ref, send_sems, recv_sems)[0]:
            cp.start()

    return _split_start("gather_late_start", 6, packed, land, start)


def _gather_late_wait(handle, after):
    def wait(src_ref, land_ref, send_sems, recv_sems):
        started, awaited = _late_gather_copies(src_ref, land_ref, send_sems, recv_sems)
        for cp in started:
            cp.wait_send()
        for cp in awaited:
            cp.wait_recv()

    return _split_wait("gather_late_wait", handle, after, wait)


def _early_scatter_copies(src_ref, land_ref, send_sems, recv_sems):
    x, y, c = _my_place()
    me = 4 * x + 2 * y + c
    copies = []
    for k in range(1, N_DEV):
        px, py, pc = x ^ (k >> 2), y ^ ((k >> 1) & 1), c ^ (k & 1)
        copies.append(pltpu.make_async_remote_copy(
            src_ref=src_ref.at[2 * px + py, pc], dst_ref=land_ref.at[me], send_sem=send_sems.at[k - 1],
            recv_sem=recv_sems.at[k - 1], device_id=(px, py, pc), device_id_type=MESH))
    return copies


def _scatter_early_start(parts):
    n, _, R, L = parts.shape
    x, y, c = _my_place()
    own = lax.dynamic_index_in_dim(lax.dynamic_index_in_dim(parts, 2 * x + y, 0, keepdims=False), c, 0, keepdims=False)
    land = lax.dynamic_update_index_in_dim(jnp.zeros((N_DEV, R, L), parts.dtype), own, 4 * x + 2 * y + c, 0)

    def start(src_ref, land_ref, send_sems, recv_sems):
        for cp in _early_scatter_copies(src_ref, land_ref, send_sems, recv_sems):
            cp.start()

    return _split_start("scatter_early_start", N_DEV - 1, parts, land, start)


def _scatter_early_wait(handle, after):
    def wait(src_ref, land_ref, send_sems, recv_sems):
        for cp in _early_scatter_copies(src_ref, land_ref, send_sems, recv_sems):
            cp.wait_send()
            cp.wait_recv()

    return _split_wait("scatter_early_wait", handle, after, wait)


def _sum_slots(parts):
    n, R, L = parts.shape
    tr = _tile_rows(R, WIRE_HALF_ALIGN)

    def body(p_ref, o_ref):
        acc = p_ref[0].astype(F32)
        for i in range(1, n):
            acc = acc + p_ref[i].astype(F32)
        o_ref[...] = acc

    return pl.pallas_call(
        body, grid=(R // tr,), in_specs=[pl.BlockSpec((n, tr, L), lambda i: (0, i, 0))],
        out_specs=pl.BlockSpec((tr, L), lambda i: (i, 0)), out_shape=jax.ShapeDtypeStruct((R, L), F32), name="sum_slots",
        compiler_params=_params(("parallel",)),
    )(parts)


def _pair_exchange(parts):
    n, _, R, L = parts.shape

    def body(in_ref, out_ref, send_sems, recv_sems):
        x, y, c = _my_place()
        copies = [pltpu.make_async_remote_copy(src_ref=in_ref.at[s, 1 - c], dst_ref=out_ref.at[s], send_sem=send_sems.at[s],
                                               recv_sem=recv_sems.at[s], device_id=(x, y, 1 - c), device_id_type=MESH)
                  for s in range(n)]
        for cp in copies:
            cp.start()
        for cp in copies:
            cp.wait()

    return pl.pallas_call(
        body, out_shape=jax.ShapeDtypeStruct((n, R, L), parts.dtype), in_specs=[ANY], out_specs=ANY,
        scratch_shapes=[pltpu.SemaphoreType.DMA((n,)), pltpu.SemaphoreType.DMA((n,))], name="pair_exchange",
    )(parts)


def _pair_add(a, b):
    n, R, L = a.shape
    tr = _tile_rows(R, WIRE_HALF_ALIGN)

    def body(a_ref, b_ref, o_ref):
        o_ref[...] = (a_ref[...].astype(F32) + b_ref[...].astype(F32)).astype(o_ref.dtype)

    spec = pl.BlockSpec((n, tr, L), lambda i: (0, i, 0))
    return pl.pallas_call(
        body, grid=(R // tr,), in_specs=[spec, spec], out_specs=spec, out_shape=jax.ShapeDtypeStruct(a.shape, a.dtype),
        name="pair_add", compiler_params=_params(("parallel",)),
    )(a, b)


def _scatter_grads(parts):
    n, R, L = parts.shape

    def body(in_ref, out_ref, send_sems, recv_sems):
        x, y, c = _my_place()
        copies = [pltpu.make_async_remote_copy(src_ref=in_ref.at[2 * px + py], dst_ref=out_ref.at[j], send_sem=send_sems.at[j],
                                               recv_sem=recv_sems.at[j], device_id=(px, py, c), device_id_type=MESH)
                  for j, (px, py) in enumerate(_other_chips(x, y))]
        for cp in copies:
            cp.start()
        for cp in copies:
            cp.wait()

    return pl.pallas_call(
        body, out_shape=jax.ShapeDtypeStruct((n - 1, R, L), parts.dtype), in_specs=[ANY], out_specs=ANY,
        scratch_shapes=[pltpu.SemaphoreType.DMA((3,)), pltpu.SemaphoreType.DMA((3,))], name="scatter_grads",
    )(parts)


def _sum_chips(own, others):
    n, R, L = others.shape
    tr = _tile_rows(R, WIRE_HALF_ALIGN)

    def body(a_ref, p_ref, o_ref):
        acc = a_ref[...].astype(F32)
        for i in range(n):
            acc = acc + p_ref[i].astype(F32)
        o_ref[...] = acc

    return pl.pallas_call(
        body, grid=(R // tr,), in_specs=[pl.BlockSpec((tr, L), lambda i: (i, 0)), pl.BlockSpec((n, tr, L), lambda i: (0, i, 0))],
        out_specs=pl.BlockSpec((tr, L), lambda i: (i, 0)), out_shape=jax.ShapeDtypeStruct((R, L), F32), name="sum_grads",
        compiler_params=_params(("parallel",)),
    )(own, others)


def _tile_rows(R, cap=2048):
    best = 8
    for t in range(8, min(R, cap) + 1, 8):
        if R % t == 0:
            best = t
    return best if R % 8 == 0 else R


def _swap_halves(half):
    def body(in_ref, out_ref, send_sem, recv_sem):
        x, y, c = _my_place()
        cp = pltpu.make_async_remote_copy(src_ref=in_ref, dst_ref=out_ref, send_sem=send_sem, recv_sem=recv_sem,
                                          device_id=(x, y, 1 - c), device_id_type=MESH)
        cp.start()
        cp.wait()

    return pl.pallas_call(
        body, out_shape=jax.ShapeDtypeStruct(half.shape, half.dtype), in_specs=[ANY], out_specs=ANY,
        scratch_shapes=[pltpu.SemaphoreType.DMA, pltpu.SemaphoreType.DMA], name="swap_halves",
    )(half)


def _allreduce_small(v):
    R, L = v.shape

    def body(in_ref, out_ref, buf, send_sems, recv_sems):
        x, y, c = _my_place()
        me = 4 * x + 2 * y + c
        buf[me] = in_ref[...]
        started = []
        for k in range(1, N_DEV):
            to = (x ^ (k >> 2), y ^ ((k >> 1) & 1), c ^ (k & 1))
            cp = pltpu.make_async_remote_copy(src_ref=in_ref, dst_ref=buf.at[me], send_sem=send_sems.at[k - 1],
                                              recv_sem=recv_sems.at[k - 1], device_id=to, device_id_type=MESH)
            cp.start()
            started.append(cp)
        for cp in started:
            cp.wait()
        acc = buf[0]
        for i in range(1, N_DEV):
            acc = acc + buf[i]
        out_ref[...] = acc

    vm = pl.BlockSpec(memory_space=pltpu.VMEM)
    return pl.pallas_call(
        body, out_shape=jax.ShapeDtypeStruct((R, L), F32), in_specs=[vm], out_specs=vm,
        scratch_shapes=[pltpu.VMEM((N_DEV, R, L), F32), pltpu.SemaphoreType.DMA((7,)), pltpu.SemaphoreType.DMA((7,))],
        name="allreduce_small",
    )(v)


def _adamw(name, w, g, m, v):
    shape = w.shape
    C = shape[-1]
    R = int(np.prod(shape[:-1]))
    args = [a.reshape(R, C).astype(F32) for a in (w, g, m, v)]
    tr = _tile_rows(R, 256)

    def body(w_ref, g_ref, m_ref, v_ref, d_ref, nm_ref, nv_ref):
        g = g_ref[...]
        m = ADAM_B1 * m_ref[...] + (1.0 - ADAM_B1) * g
        v = ADAM_B2 * v_ref[...] + (1.0 - ADAM_B2) * (g * g)
        m_hat = m / (1.0 - ADAM_B1 ** ADAM_STEP)
        v_hat = v / (1.0 - ADAM_B2 ** ADAM_STEP)
        d_ref[...] = -ADAM_LR * (m_hat / (jnp.sqrt(v_hat) + ADAM_EPS) + ADAM_WD * w_ref[...])
        nm_ref[...] = m
        nv_ref[...] = v

    spec = pl.BlockSpec((tr, C), lambda i: (i, 0))
    out = pl.pallas_call(
        body, grid=(R // tr,), in_specs=[spec] * 4, out_specs=[spec] * 3,
        out_shape=[jax.ShapeDtypeStruct((R, C), F32)] * 3, name="adamw_" + name, compiler_params=_params(("parallel",)),
    )(*args)
    return [o.reshape(shape) for o in out]


def kernel(x, mem, pre1_g, post1_g, pre2_g, post2_g, mem_norm_g, w_in, fox_f_bias, rwkv_mu, rwkv_w0, rwkv_w_up, rwkv_a0, rwkv_a_up, rwkv_g_up, rwkv_k_k, rwkv_k_a, rwkv_r_k, rwkv_gn_g, rwkv_gn_b, w_mem_kv, w_fox_out, w_rwkv_out, w_mem_out, w_o, w_ffn_gate, w_ffn_up, w_ffn_down, loss_target, m_pre1_g, m_post1_g, m_pre2_g, m_post2_g, m_mem_norm_g, m_w_in, m_fox_f_bias, m_rwkv_mu, m_rwkv_w0, m_rwkv_w_up, m_rwkv_a0, m_rwkv_a_up, m_rwkv_g_up, m_rwkv_k_k, m_rwkv_k_a, m_rwkv_r_k, m_rwkv_gn_g, m_rwkv_gn_b, m_w_mem_kv, m_w_fox_out, m_w_rwkv_out, m_w_mem_out, m_w_o, m_w_ffn_gate, m_w_ffn_up, m_w_ffn_down, v_pre1_g, v_post1_g, v_pre2_g, v_post2_g, v_mem_norm_g, v_w_in, v_fox_f_bias, v_rwkv_mu, v_rwkv_w0, v_rwkv_w_up, v_rwkv_a0, v_rwkv_a_up, v_rwkv_g_up, v_rwkv_k_k, v_rwkv_k_a, v_rwkv_r_k, v_rwkv_gn_g, v_rwkv_gn_b, v_w_mem_kv, v_w_fox_out, v_w_rwkv_out, v_w_mem_out, v_w_o, v_w_ffn_gate, v_w_ffn_up, v_w_ffn_down):
    given = dict(locals())
    w_loc = {n: given[n] for n in WEIGHTS}
    m_loc = {n: given["m_" + n] for n in WEIGHTS}
    v_loc = {n: given["v_" + n] for n in WEIGHTS}

    shard_shapes = {n: tuple(w_loc[n].shape[1:]) for n in BIG}
    groups = {names: _wire_layout(shard_shapes, names) for names in (FIRST, LATER)}
    core = lax.axis_index("c")

    def pack_weights(names):
        return _wire_pack([_to_wire(n, w_loc[n][0].astype(BF16)) for n in names], groups[names][1])

    def unpack_weights(gathered, names):
        layout, half_rows = groups[names]
        gathered = gathered.reshape(N_CHIPS, 2 * half_rows, WIRE_W)
        out = {}
        for n in names:
            off, rows = layout[n]
            blocks = _from_wire(n, gathered[:, off:off + rows], shard_shapes[n])
            if n in LORA:
                out[n] = blocks.transpose(2, 0, 1).reshape(blocks.shape[2], -1)
            else:
                out[n] = blocks.reshape(-1, blocks.shape[2])
        return out

    def pack_grads(gw, names):
        blocks = []
        for n in names:
            r, c = shard_shapes[n]
            g = gw[n].astype(BF16)
            if n in LORA:
                g = g.reshape(r, N_CHIPS, c).transpose(1, 0, 2)
            elif n in TRANSPOSED:
                g = jnp.swapaxes(g.reshape(N_CHIPS, c, r), 1, 2)
            else:
                g = g.reshape(N_CHIPS, r, c)
            blocks.append(_to_wire(n, g))
        return _wire_pack(blocks, groups[names][1])

    def unpack_grads(half, other, names):
        layout, _ = groups[names]
        reduced = jnp.where(core == 0, jnp.concatenate([half, other]), jnp.concatenate([other, half]))
        out = {}
        for n in names:
            off, rows = layout[n]
            g = _from_wire(n, reduced[off:off + rows], shard_shapes[n])
            out[n] = g if n in ROW_SHARDED else g.T
        return out

    first = _gather_weights(pack_weights(FIRST))
    late = _gather_late_start(pack_weights(LATER) + (first[0, 0, 0, 0] * 0).astype(BF16))
    W = unpack_weights(first, FIRST)
    W.update({n: w_loc[n][0] for n in SMALL})
    W["pre1_g"] = W["pre1_g"] + late[4][0, 0]
    early = []

    def late_weights(after):
        return unpack_weights(_gather_late_wait(late, after), LATER)

    def early_grads(gw, thru):
        early.append(_scatter_early_start(pack_grads(gw, LATER)))
        return thru + early[0][4][0, 0].astype(thru.dtype)

    loss, grad_x, gw = _layer_step(x, mem, loss_target, W, late_weights, early_grads)

    packed = pack_grads(gw, FIRST)
    own_halves = lax.dynamic_index_in_dim(packed, core, axis=1, keepdims=False)
    chip_sums = _pair_add(own_halves, _pair_exchange(packed))
    own_chip = lax.dynamic_index_in_dim(chip_sums, 2 * lax.axis_index("x") + lax.axis_index("y"), axis=0, keepdims=False)
    half_first = _sum_chips(own_chip, _scatter_grads(chip_sums))
    half_later = _sum_slots(_scatter_early_wait(early[0], half_first))
    rows_first = groups[FIRST][1]
    other = _swap_halves(jnp.concatenate([half_first, half_later]))
    g_shard = {**unpack_grads(half_first, other[:rows_first], FIRST), **unpack_grads(half_later, other[rows_first:], LATER)}

    small_shapes = [w_loc[n].shape[1:] for n in SMALL] + [(1,)]
    n_small = sum(int(np.prod(s)) for s in small_shapes)
    small_rows = -(-n_small // (8 * LANES)) * 8
    flat = jnp.concatenate([gw[n].reshape(-1) for n in SMALL] + [loss.reshape(1)])
    flat = jnp.pad(flat, (0, small_rows * LANES - n_small)).reshape(small_rows, LANES).reshape(-1)
    small, off = [], 0
    flat = _allreduce_small(flat.reshape(small_rows, LANES)).reshape(-1)
    for s in small_shapes:
        cnt = int(np.prod(s))
        small.append(flat[off:off + cnt].reshape(s))
        off += cnt
    g_small = dict(zip(SMALL, small[:-1]))
    loss = small[-1][0]

    grads, deltas, new_m, new_v = [], [], [], []
    for n in WEIGHTS:
        g = (g_shard[n] if n in g_shard else g_small[n]).reshape(w_loc[n].shape)
        d, nm, nv = _adamw(n, w_loc[n], g, m_loc[n], v_loc[n])
        grads.append(g)
        deltas.append(d)
        new_m.append(nm)
        new_v.append(nv)
    return (loss, grad_x, *grads, *deltas, *new_m, *new_v)
```

```python
import functools
import math

import numpy as np
import jax
import jax.numpy as jnp
from jax import lax
from jax.experimental import pallas as pl
from jax.experimental.pallas import tpu as pltpu

F32, BF16 = jnp.float32, jnp.bfloat16
MESH = pl.DeviceIdType.MESH

D_MODEL = 1024
HEAD_DIM = 64
N_HEADS = 8
BR_W = 512
MEM_HEADS = 4
MEM_HEAD_DIM = 128
D_FF = 2816
NORM_EPS = 1e-6
GN_EPS = 64e-5
N_CHIPS = 4
N_DEV = 8
LANES = 128
VMEM_LIMIT = 48 * 1024 * 1024

ADAM_LR, ADAM_B1, ADAM_B2, ADAM_EPS, ADAM_WD, ADAM_STEP = 0.001, 0.9, 0.999, 1e-08, 0.01, 10

FOX_COLS = 3 * BR_W + N_HEADS
RWKV_COLS = 3 * BR_W + 64 + 64 + 128
COL_QKV = (0, 3 * BR_W)
COL_F = (3 * BR_W, FOX_COLS)
COL_RW = (FOX_COLS, FOX_COLS + RWKV_COLS)
COL_MQ = (COL_RW[1], COL_RW[1] + BR_W)
COL_GATE = (COL_MQ[1], COL_MQ[1] + 3 * D_MODEL)

NT_DIMS = (((1,), (1,)), ((), ()))
TN_DIMS = (((0,), (0,)), ((), ()))


def _params(sem=None, **kw):
    return pltpu.CompilerParams(dimension_semantics=sem, vmem_limit_bytes=VMEM_LIMIT, **kw)


def _sigmoid(x):
    return 1.0 / (1.0 + jnp.exp(-x))


def _log_sigmoid(x):
    return jnp.minimum(x, 0.0) - jnp.log(1.0 + jnp.exp(-jnp.abs(x)))


def _bdot(a, b, dims=None):
    a, b = a.astype(BF16), b.astype(BF16)
    if dims is None:
        return jnp.dot(a, b, preferred_element_type=F32)
    return lax.dot_general(a, b, dims, preferred_element_type=F32)


def _hdot(a, ones):
    ones = ones.astype(BF16)
    hi = a.astype(BF16)
    r1 = a - hi.astype(F32)
    mid = r1.astype(BF16)
    lo = (r1 - mid.astype(F32)).astype(BF16)
    dot = lambda x: jnp.dot(x, ones, preferred_element_type=F32)
    return dot(hi) + dot(mid) + dot(lo)


def _tile(n, cap):
    best = None
    for t in range(LANES, min(n, cap) + 1, LANES):
        if n % t == 0:
            best = t
    return best or n


def _rowwise(name, fn, rows, consts, outs, accs=(), tm=256):
    T = rows[0].shape[0]
    tm = min(tm, T)
    assert T % tm == 0
    nr, nc, no, na = len(rows), len(consts), len(outs), len(accs)

    def body(*refs):
        res = fn(*[r[...].astype(F32) for r in refs[:nr + nc]])
        if not isinstance(res, (tuple, list)):
            res = (res,)
        orefs, arefs = refs[nr + nc:nr + nc + no], refs[nr + nc + no:]
        for ref, val in zip(orefs, res[:no]):
            ref[...] = val.astype(ref.dtype)
        if na:
            @pl.when(pl.program_id(0) == 0)
            def _():
                for ref in arefs:
                    ref[...] = jnp.zeros(ref.shape, ref.dtype)
            for ref, val in zip(arefs, res[no:]):
                ref[...] += val

    in_specs = ([pl.BlockSpec((tm, r.shape[1]), lambda i: (i, 0)) for r in rows]
                + [pl.BlockSpec(c.shape, lambda i: (0, 0)) for c in consts])
    out_specs = ([pl.BlockSpec((tm, w), lambda i: (i, 0)) for w, _ in outs]
                 + [pl.BlockSpec(s, lambda i: (0, 0)) for s, _ in accs])
    out_shape = ([jax.ShapeDtypeStruct((T, w), dt) for w, dt in outs]
                 + [jax.ShapeDtypeStruct(s, dt) for s, dt in accs])
    return pl.pallas_call(
        body, grid=(T // tm,), in_specs=in_specs, out_specs=out_specs, out_shape=out_shape, name=name,
        compiler_params=_params(("arbitrary",) if na else ("parallel",)),
    )(*rows, *consts)


MM_TILE_CAP = 1408
MM_WHOLE_K = 2048


def _mm(name, a, b, ta=False, tb=False, out_dtype=F32, add=None):
    M, K = (a.shape[1], a.shape[0]) if ta else a.shape
    K2, N = (b.shape[1], b.shape[0]) if tb else b.shape
    assert K == K2
    tm, tn = _tile(M, MM_TILE_CAP), _tile(N, MM_TILE_CAP)
    tk = K if K <= MM_WHOLE_K else _tile(K, MM_TILE_CAP)
    assert M % tm == 0 and N % tn == 0 and K % tk == 0
    nk = K // tk
    a_dim, b_dim = (0 if ta else 1), (1 if tb else 0)

    def body(*refs):
        a_ref, b_ref = refs[0], refs[1]
        n_in = 2 if add is None else 3
        o_ref, acc = refs[n_in], (refs[n_in + 1] if nk > 1 else None)
        k = pl.program_id(2)
        part = lax.dot_general(a_ref[...].astype(BF16), b_ref[...].astype(BF16),
                               (((a_dim,), (b_dim,)), ((), ())), preferred_element_type=F32)

        def finish(r):
            if add is not None:
                r = r + refs[2][...].astype(F32)
            o_ref[...] = r.astype(o_ref.dtype)

        if nk == 1:
            finish(part)
            return

        @pl.when(k == 0)
        def _():
            acc[...] = part

        @pl.when(k > 0)
        def _():
            acc[...] += part

        @pl.when(k == nk - 1)
        def _():
            finish(acc[...])

    a_spec = pl.BlockSpec((tk, tm), lambda i, j, k: (k, i)) if ta else pl.BlockSpec((tm, tk), lambda i, j, k: (i, k))
    b_spec = pl.BlockSpec((tn, tk), lambda i, j, k: (j, k)) if tb else pl.BlockSpec((tk, tn), lambda i, j, k: (k, j))
    o_spec = pl.BlockSpec((tm, tn), lambda i, j, k: (i, j))
    ins, in_specs = [a, b], [a_spec, b_spec]
    if add is not None:
        ins.append(add)
        in_specs.append(o_spec)
    return pl.pallas_call(
        body, grid=(M // tm, N // tn, nk), in_specs=in_specs, out_specs=o_spec,
        out_shape=jax.ShapeDtypeStruct((M, N), out_dtype), scratch_shapes=[pltpu.VMEM((tm, tn), F32)] if nk > 1 else [],
        name=name, compiler_params=_params(("parallel", "parallel", "arbitrary")),
    )(*ins)


def _rowsum(x):
    return jnp.sum(x, axis=0, keepdims=True)


def _rms_stat(x):
    return lax.rsqrt(jnp.mean(x * x, axis=-1, keepdims=True) + NORM_EPS)


def _rms_bwd(dy, x, g):
    r = _rms_stat(x)
    xn = x * r
    dxn = dy * g
    dx = r * (dxn - xn * jnp.mean(dxn * xn, axis=-1, keepdims=True))
    return dx, _rowsum(dy * xn)


def _fox_c_fwd(f8t, bias_col, tc=256):
    B, H, S = f8t.shape
    tc = min(tc, S)

    def body(f_ref, b_ref, c_ref, carry):
        @pl.when(pl.program_id(1) == 0)
        def _():
            carry[...] = jnp.zeros(carry.shape, F32)
        lf = _log_sigmoid(f_ref[...] + b_ref[...])
        row = lax.broadcasted_iota(jnp.int32, (tc, tc), 0)
        col = lax.broadcasted_iota(jnp.int32, (tc, tc), 1)
        c = _hdot(lf, (row <= col).astype(F32)) + carry[...]
        c_ref[...] = c
        carry[...] = c[:, tc - 1:tc]

    return pl.pallas_call(
        body, grid=(B, S // tc),
        in_specs=[pl.BlockSpec((None, H, tc), lambda b, i: (b, 0, i)), pl.BlockSpec((H, 1), lambda b, i: (0, 0))],
        out_specs=pl.BlockSpec((None, H, tc), lambda b, i: (b, 0, i)),
        out_shape=jax.ShapeDtypeStruct((B, H, S), F32), scratch_shapes=[pltpu.VMEM((H, 1), F32)], name="fox_c_fwd",
        compiler_params=_params(("parallel", "arbitrary")),
    )(f8t, bias_col)


def _fox_c_bwd(dc, f8t, bias_col, tc=256):
    B, H, S = f8t.shape
    tc = min(tc, S)
    n = S // tc

    def body(dc_ref, f_ref, b_ref, df_ref, db_ref, carry):
        @pl.when(pl.program_id(1) == 0)
        def _():
            carry[...] = jnp.zeros(carry.shape, F32)
            db_ref[...] = jnp.zeros(db_ref.shape, F32)
        row = lax.broadcasted_iota(jnp.int32, (tc, tc), 0)
        col = lax.broadcasted_iota(jnp.int32, (tc, tc), 1)
        dlf = _hdot(dc_ref[...], (row >= col).astype(F32)) + carry[...]
        z = f_ref[...] + b_ref[...]
        df = dlf * (1.0 - _sigmoid(z))
        df_ref[...] = df
        db_ref[...] += jnp.sum(df, axis=1, keepdims=True)
        carry[...] = dlf[:, 0:1]

    rev = lambda b, i: (b, 0, n - 1 - i)
    return pl.pallas_call(
        body, grid=(B, n),
        in_specs=[pl.BlockSpec((None, H, tc), rev), pl.BlockSpec((None, H, tc), rev), pl.BlockSpec((H, 1), lambda b, i: (0, 0))],
        out_specs=[pl.BlockSpec((None, H, tc), rev), pl.BlockSpec((None, H, 1), lambda b, i: (b, 0, 0))],
        out_shape=[jax.ShapeDtypeStruct((B, H, S), F32), jax.ShapeDtypeStruct((B, H, 1), F32)],
        scratch_shapes=[pltpu.VMEM((H, 1), F32)], name="fox_c_bwd",
        compiler_params=_params(("parallel", "arbitrary")),
    )(dc, f8t, bias_col)


NEG_BIG = -1e30


def _fox_logits(q, kj, cq, ckj, i, j, tq, tk, scale):
    s = _bdot(q, kj, NT_DIMS) * scale + (cq - ckj)
    row = lax.broadcasted_iota(jnp.int32, (tq, tk), 0)
    col = lax.broadcasted_iota(jnp.int32, (tq, tk), 1)
    return s, col <= row + (i * tq - j * tk)


def _fox_key_blocks(i, tq, tk):
    return (i * tq + tq - 1) // tk + 1


def _fox_fwd(qkv, cq, ck, B):
    T = qkv.shape[0]
    S = T // B
    (nq, tq), (nk, tk) = cq.shape[1:3], (ck.shape[1], ck.shape[3])
    scale = HEAD_DIM ** -0.5

    def body(q_ref, k_ref, v_ref, cq_ref, ck_ref, o_ref, lse_ref):
        i = pl.program_id(1)
        lo = lax.broadcasted_iota(jnp.int32, (tq, LANES), 1) < HEAD_DIM
        q = q_ref[...]
        qh = (jnp.where(lo, q, 0), jnp.where(lo, 0, q))

        def step(j, carry):
            rows = pl.ds(pl.multiple_of(j * tk, tk), tk)
            kj, vj = k_ref[rows, :], v_ref[rows, :]
            new = []
            for h in range(2):
                m, l, acc = carry[h]
                s, ok = _fox_logits(qh[h], kj, cq_ref[h], ck_ref[h, j], i, j, tq, tk, scale)
                s = jnp.where(ok, s, NEG_BIG)
                m2 = jnp.maximum(m, jnp.max(s, axis=1, keepdims=True))
                p = jnp.exp(s - m2)
                al = jnp.exp(m - m2)
                new.append((m2, al * l + jnp.sum(p, axis=1, keepdims=True), al * acc + _bdot(p, vj)))
            return tuple(new)

        init = tuple((jnp.full((tq, 1), NEG_BIG, F32), jnp.zeros((tq, 1), F32), jnp.zeros((tq, LANES), F32)) for _ in range(2))
        (m0, l0, a0), (m1, l1, a1) = lax.fori_loop(0, _fox_key_blocks(i, tq, tk), step, init)
        o_ref[...] = jnp.where(lo, a0 / l0, a1 / l1).astype(o_ref.dtype)
        lse_ref[0] = m0 + jnp.log(l0)
        lse_ref[1] = m1 + jnp.log(l1)

    seq = lambda col0: pl.BlockSpec((S, LANES), lambda g, i: (g // 4, col0 + g % 4))
    blk = lambda col0: pl.BlockSpec((tq, LANES), lambda g, i: ((g // 4) * nq + i, col0 + g % 4))
    col = pl.BlockSpec((2, None, tq, 1), lambda g, i: (g, i, 0, 0))
    return pl.pallas_call(
        body, grid=(B * 4, nq), in_specs=[blk(0), seq(4), seq(8), col, pl.BlockSpec((2, nk, 1, tk), lambda g, i: (g, 0, 0, 0))],
        out_specs=[blk(0), col],
        out_shape=[jax.ShapeDtypeStruct((T, BR_W), BF16), jax.ShapeDtypeStruct(cq.shape, F32)], name="fox_fwd",
        compiler_params=_params(("parallel", "parallel")),
    )(qkv, qkv, qkv, cq, ck)


def _fox_bwd(qkv, o, do, cq, ck, lse, B):
    T = qkv.shape[0]
    S = T // B
    (nq, tq), (nk, tk) = cq.shape[1:3], (ck.shape[1], ck.shape[3])
    scale = HEAD_DIM ** -0.5

    def body(q_ref, k_ref, v_ref, o_ref, do_ref, cq_ref, ck_ref, lse_ref, dq_ref, dk_ref, dv_ref, dck_ref, dcq_ref,
             dk_acc, dv_acc):
        dk_acc[...] = jnp.zeros(dk_acc.shape, F32)
        dv_acc[...] = jnp.zeros(dv_acc.shape, F32)
        dck_ref[...] = jnp.zeros(dck_ref.shape, F32)
        lo = lax.broadcasted_iota(jnp.int32, (tq, LANES), 1) < HEAD_DIM

        def qloop(i, _):
            qrows = pl.ds(pl.multiple_of(i * tq, tq), tq)
            q, do_i, o_i = q_ref[qrows, :], do_ref[qrows, :], o_ref[qrows, :].astype(F32)
            qh = (jnp.where(lo, q, 0), jnp.where(lo, 0, q))
            doh = (jnp.where(lo, do_i, 0), jnp.where(lo, 0, do_i))
            delta = [jnp.sum(doh[h].astype(F32) * o_i, axis=1, keepdims=True) for h in range(2)]

            def kloop(j, carry):
                krows = pl.ds(pl.multiple_of(j * tk, tk), tk)
                kj, vj = k_ref[krows, :], v_ref[krows, :]
                new = []
                for h in range(2):
                    dq, dcq = carry[h]
                    s, ok = _fox_logits(qh[h], kj, cq_ref[h, i], ck_ref[h, j], i, j, tq, tk, scale)
                    p = jnp.where(ok, jnp.exp(s - lse_ref[h, i]), 0.0)
                    ds = p * (_bdot(doh[h], vj, NT_DIMS) - delta[h])
                    dv_acc[krows, :] += _bdot(p, doh[h], TN_DIMS)
                    dk_acc[krows, :] += _bdot(ds, qh[h], TN_DIMS) * scale
                    dck_ref[h, j] += -_rowsum(ds)
                    new.append((dq + _bdot(ds, kj) * scale, dcq + jnp.sum(ds, axis=1, keepdims=True)))
                return tuple(new)

            init = tuple((jnp.zeros((tq, LANES), F32), jnp.zeros((tq, 1), F32)) for _ in range(2))
            (dq0, dcq0), (dq1, dcq1) = lax.fori_loop(0, _fox_key_blocks(i, tq, tk), kloop, init)
            dq_ref[qrows, :] = jnp.where(lo, dq0, dq1).astype(dq_ref.dtype)
            dcq_ref[0, i] = dcq0
            dcq_ref[1, i] = dcq1
            return 0

        lax.fori_loop(0, nq, qloop, 0)
        dk_ref[...] = dk_acc[...].astype(dk_ref.dtype)
        dv_ref[...] = dv_acc[...].astype(dv_ref.dtype)

    seq = lambda col0: pl.BlockSpec((S, LANES), lambda g: (g // 4, col0 + g % 4))
    col = pl.BlockSpec((2, nq, tq, 1), lambda g: (g, 0, 0, 0))
    row = pl.BlockSpec((2, nk, 1, tk), lambda g: (g, 0, 0, 0))
    out = jax.ShapeDtypeStruct((T, BR_W), BF16)
    return pl.pallas_call(
        body, grid=(B * 4,), in_specs=[seq(0), seq(4), seq(8), seq(0), seq(0), col, row, col],
        out_specs=[seq(0), seq(0), seq(0), row, col],
        out_shape=[out, out, out, jax.ShapeDtypeStruct(ck.shape, F32), jax.ShapeDtypeStruct(cq.shape, F32)],
        scratch_shapes=[pltpu.VMEM((S, LANES), F32), pltpu.VMEM((S, LANES), F32)], name="fox_bwd",
        compiler_params=_params(("parallel",)),
    )(qkv, qkv, qkv, o, do, cq, ck, lse)


def _mem_probs(qh, kh):
    s = _bdot(qh, kh, NT_DIMS) * (MEM_HEAD_DIM ** -0.5)
    e = jnp.exp(s - jnp.max(s, axis=1, keepdims=True))
    return e / jnp.sum(e, axis=1, keepdims=True)


def _mem_fwd(q, mem_kv, B, tq=512):
    T = q.shape[0]
    S, Lm = T // B, mem_kv.shape[0] // B
    tq = min(tq, S)
    n = S // tq

    def body(q_ref, k_ref, v_ref, o_ref):
        for h in range(MEM_HEADS):
            sl = slice(h * MEM_HEAD_DIM, (h + 1) * MEM_HEAD_DIM)
            p = _mem_probs(q_ref[:, sl], k_ref[:, sl])
            o_ref[:, sl] = _bdot(p, v_ref[:, sl]).astype(o_ref.dtype)

    qs = pl.BlockSpec((tq, BR_W), lambda b, i: (b * n + i, 0))
    return pl.pallas_call(
        body, grid=(B, n),
        in_specs=[qs, pl.BlockSpec((Lm, BR_W), lambda b, i: (b, 0)), pl.BlockSpec((Lm, BR_W), lambda b, i: (b, 1))],
        out_specs=qs, out_shape=jax.ShapeDtypeStruct((T, BR_W), BF16), name="mem_fwd",
        compiler_params=_params(("parallel", "parallel")),
    )(q, mem_kv, mem_kv)


def _mem_bwd(q, mem_kv, do, B, tq=512):
    T = q.shape[0]
    S, Lm = T // B, mem_kv.shape[0] // B
    tq = min(tq, S)
    n = S // tq
    scale = MEM_HEAD_DIM ** -0.5

    def body(q_ref, k_ref, v_ref, do_ref, dq_ref, dk_ref, dv_ref):
        @pl.when(pl.program_id(1) == 0)
        def _():
            dk_ref[...] = jnp.zeros(dk_ref.shape, F32)
            dv_ref[...] = jnp.zeros(dv_ref.shape, F32)
        for h in range(MEM_HEADS):
            sl = slice(h * MEM_HEAD_DIM, (h + 1) * MEM_HEAD_DIM)
            qh, kh, vh, doh = q_ref[:, sl], k_ref[:, sl], v_ref[:, sl], do_ref[:, sl]
            p = _mem_probs(qh, kh)
            dp = _bdot(doh, vh, NT_DIMS)
            ds = p * (dp - jnp.sum(p * dp, axis=1, keepdims=True))
            dq_ref[:, sl] = (_bdot(ds, kh) * scale).astype(dq_ref.dtype)
            dk_ref[:, sl] += _bdot(ds, qh, TN_DIMS) * scale
            dv_ref[:, sl] += _bdot(p, doh, TN_DIMS)

    qs = pl.BlockSpec((tq, BR_W), lambda b, i: (b * n + i, 0))
    kv = pl.BlockSpec((Lm, BR_W), lambda b, i: (b, 0))
    return pl.pallas_call(
        body, grid=(B, n),
        in_specs=[qs, kv, pl.BlockSpec((Lm, BR_W), lambda b, i: (b, 1)), qs], out_specs=[qs, kv, kv],
        out_shape=[jax.ShapeDtypeStruct((T, BR_W), BF16), jax.ShapeDtypeStruct((B * Lm, BR_W), F32),
                   jax.ShapeDtypeStruct((B * Lm, BR_W), F32)], name="mem_bwd",
        compiler_params=_params(("parallel", "arbitrary")),
    )(q, mem_kv, mem_kv, do)


def _head_ones():
    h = np.arange(BR_W) // HEAD_DIM
    return jnp.asarray((h[:, None] == h[None, :]).astype(np.float32))


def _rw_prep(p, pp, mu, w0, w1, a0, w2, g_up, k_k, k_a, bd):
    ps = p + (pp - p) * mu
    r, k, v = ps[:, 0:512], ps[:, 512:1024], ps[:, 1024:1536]
    wa, gd = ps[:, 1536:1664], ps[:, 1664:1792]
    th = jnp.tanh(wa)
    z = w0 + _bdot(th, w1)
    wl = -jnp.exp(_log_sigmoid(z) - 0.5)
    w = jnp.exp(wl)
    a = _sigmoid(a0 + _bdot(wa, w2))
    sg = _sigmoid(gd)
    g = _bdot(sg, g_up)
    kq = k * k_k
    n2 = _hdot(kq * kq, bd)
    inv = lax.rsqrt(jnp.maximum(n2, 1e-24))
    kk = kq * inv
    k2 = k * (1.0 + (a - 1.0) * k_a)
    return dict(ps=ps, r=r, k=k, v=v, wa=wa, th=th, z=z, wl=wl, w=w, a=a, sg=sg, g=g, kq=kq, n2=n2, inv=inv, kk=kk, k2=k2)


def _keycol_selector(tm, Tc):
    e = np.zeros((tm, (tm // Tc) * LANES), np.float32)
    for t in range(tm):
        c, tl = divmod(t, Tc)
        e[t, c * LANES + tl] = e[t, c * LANES + Tc + tl] = 1.0
    return jnp.asarray(e, BF16)


def _rw_prep_fwd(p, pp, consts, B, Tc, tm=256):
    assert 2 * Tc == LANES
    T = p.shape[0]
    S = T // B
    nb, cpb = S // tm, tm // Tc
    sel = _keycol_selector(tm, Tc)
    nc = len(consts)

    def body(*refs):
        t = _rw_prep(*[r[...] for r in refs[:2 + nc]])
        sel_ref = refs[2 + nc]
        rows, cols = refs[3 + nc:7 + nc], refs[7 + nc:]
        for ref, val in zip(rows, (t["r"], t["k2"], t["v"], t["g"])):
            ref[...] = val
        lo = lax.broadcasted_iota(jnp.int32, (HEAD_DIM, LANES), 1) < HEAD_DIM
        operands = (t["w"], -t["kk"], t["kk"] * t["a"], t["k2"], t["r"])
        for n, (ref, x) in enumerate(zip(cols, operands)):
            terms = _split_bf16(x) if n == 0 else (x.astype(BF16),)
            for hp in range(4):
                xt = sum(lax.dot_general(tt[:, hp * LANES:(hp + 1) * LANES], sel_ref[...], TN_DIMS,
                                         preferred_element_type=F32) for tt in terms)
                for c in range(cpb):
                    blk = xt[:, c * LANES:(c + 1) * LANES]
                    ref[hp, c] = jnp.where(lo, blk[0:HEAD_DIM], blk[HEAD_DIM:2 * HEAD_DIM]).astype(ref.dtype)

    row_spec = lambda w: pl.BlockSpec((tm, w), lambda i: (i, 0))
    col_spec = pl.BlockSpec((None, 4, cpb, HEAD_DIM, LANES), lambda i: (i // nb, 0, i % nb, 0, 0))
    col_shape = lambda dt: jax.ShapeDtypeStruct((B, 4, S // Tc, HEAD_DIM, LANES), dt)
    out = pl.pallas_call(
        body, grid=(T // tm,),
        in_specs=[row_spec(RWKV_COLS)] * 2 + [pl.BlockSpec(c.shape, lambda i: (0, 0)) for c in consts]
        + [pl.BlockSpec(sel.shape, lambda i: (0, 0))],
        out_specs=[row_spec(BR_W)] * 4 + [col_spec] * 5,
        out_shape=[jax.ShapeDtypeStruct((T, BR_W), F32)] * 4 + [col_shape(F32)] + [col_shape(BF16)] * 4,
        name="rwkv_prep_fwd", compiler_params=_params(("parallel",)),
    )(p, pp, *consts, sel)
    return out[:4], [c.reshape(B * 4, S // Tc, HEAD_DIM, LANES) for c in out[4:]]


def _rw_prep_bwd(p, pp, cots, consts):
    def fn(p, pp, dr1, dr2, dw, dk21, dk22, dv1, dv2, dav, dbv, dg, mu, w0, w1, a0, w2, g_up, k_k, k_a, bd):
        t = _rw_prep(p, pp, mu, w0, w1, a0, w2, g_up, k_k, k_a, bd)
        dr, dk2, dv = dr1 + dr2, dk21 + dk22, dv1 + dv2
        a, k, kk, kq, inv = t["a"], t["k"], t["kk"], t["kq"], t["inv"]
        dkk = dbv * a - dav
        da = dbv * kk + dk2 * k * k_a
        dk = dk2 * (1.0 + (a - 1.0) * k_a)
        d_k_a = _rowsum(dk2 * k * (a - 1.0))
        proj = _hdot(dkk * kq, bd)
        dkq = dkk * inv - jnp.where(t["n2"] > 1e-24, kq * inv * inv * inv * proj, 0.0)
        dk = dk + dkq * k_k
        d_k_k = _rowsum(dkq * k)
        dpa = da * a * (1.0 - a)
        d_a0 = _rowsum(dpa)
        dwa = _bdot(dpa, w2, NT_DIMS)
        d_w2 = _bdot(t["wa"], dpa, TN_DIMS)
        dz = dw * t["w"] * t["wl"] * (1.0 - _sigmoid(t["z"]))
        d_w0 = _rowsum(dz)
        th = t["th"]
        dwa = dwa + _bdot(dz, w1, NT_DIMS) * (1.0 - th * th)
        d_w1 = _bdot(th, dz, TN_DIMS)
        sg = t["sg"]
        dgd = _bdot(dg, g_up, NT_DIMS) * sg * (1.0 - sg)
        d_g_up = _bdot(sg, dg, TN_DIMS)
        dps = jnp.concatenate([dr, dk, dv, dwa, dgd], axis=1)
        d_mu = _rowsum(dps * (pp - p))
        return dps * (1.0 - mu), dps * mu, d_mu, d_w0, d_w1, d_a0, d_w2, d_g_up, d_k_k, d_k_a

    accs = [((1, RWKV_COLS), F32), ((1, BR_W), F32), ((LANES, BR_W), F32), ((1, BR_W), F32), ((LANES, BR_W), F32),
            ((LANES, BR_W), F32), ((1, BR_W), F32), ((1, BR_W), F32)]
    return _rowwise("rwkv_prep_bwd", fn, [p, pp] + list(cots), consts, [(RWKV_COLS, F32)] * 2, accs, tm=128)


def _rw_head(y, r, k2, v, g, gn_g, gn_b, r_k, bd):
    mean = _hdot(y, bd) * (1.0 / HEAD_DIM)
    yc = y - mean
    rs = lax.rsqrt(_hdot(yc * yc, bd) * (1.0 / HEAD_DIM) + GN_EPS)
    yn = yc * rs
    bs = _hdot(r * k2 * r_k, bd)
    return yn, rs, bs, yn * gn_g + gn_b + bs * v


def _rw_head_fwd(y, r, k2, v, g, consts):
    def fn(y, r, k2, v, g, *c):
        return _rw_head(y, r, k2, v, g, *c)[3] * g
    return _rowwise("rwkv_head_fwd", fn, [y, r, k2, v, g], consts, [(BR_W, BF16)])[0]


def _rw_head_bwd(dout, y, r, k2, v, g, consts):
    def fn(dout, y, r, k2, v, g, gn_g, gn_b, r_k, bd):
        dout = dout.astype(F32)
        yn, rs, bs, zz = _rw_head(y, r, k2, v, g, gn_g, gn_b, r_k, bd)
        dg = dout * zz
        dz = dout * g
        dyn = dz * gn_g
        inv_n = 1.0 / HEAD_DIM
        dy = rs * (dyn - _hdot(dyn, bd) * inv_n - yn * (_hdot(dyn * yn, bd) * inv_n))
        dq = _hdot(dz * v, bd)
        return dy, dg, dq * k2 * r_k, dq * r * r_k, dz * bs, _rowsum(dz * yn), _rowsum(dz), _rowsum(dq * r * k2)
    return _rowwise("rwkv_head_bwd", fn, [dout, y, r, k2, v, g], consts, [(BR_W, F32)] * 5, [((1, BR_W), F32)] * 3)


SCAN_TC = 64


def _scan_onehot(Tc):
    w = np.zeros((Tc // 2, 2 * Tc, 2 * LANES), np.float32)
    for tt in range(Tc // 2):
        for u in range(2):
            for h in range(2):
                w[tt, h * Tc + 2 * tt + u, u * LANES + h * HEAD_DIM: u * LANES + (h + 1) * HEAD_DIM] = 1.0
    return jnp.asarray(w, BF16)


def _split_bf16(x):
    hi = x.astype(BF16)
    return hi, (x - hi.astype(F32)).astype(BF16)


def _key_tiles(l_w, others, onehot):
    dot = lambda x: jnp.dot(x, onehot, preferred_element_type=F32)
    whi, wmid = l_w
    return [dot(whi) + dot(wmid)] + [dot(o) for o in others]


def _rw_scan_fwd(LW, LA, LB, LK, LR, v, P=2):
    NP, nc, _, Tc2 = LW.shape
    Tc = Tc2 // 2
    S = nc * Tc
    onehot = _scan_onehot(Tc)
    npb = 4 // P

    def body(lw, la, lb, lk, lr, v_ref, oh_ref, y_ref, sa_ref, sb_ref, st):
        @pl.when(pl.program_id(1) == 0)
        def _():
            st[...] = jnp.zeros(st.shape, F32)
        s = [st[p] for p in range(P)]
        cols = [(_split_bf16(lw[p]), [ref[p].astype(BF16) for ref in (la, lb, lk)]) for p in range(P)]
        r_cols = [lr[p].astype(F32) for p in range(P)]
        head0 = lax.broadcasted_iota(jnp.int32, (HEAD_DIM, LANES), 1) < HEAD_DIM
        for tt in range(Tc // 2):
            tiles = [_key_tiles(c[0], c[1], oh_ref[tt]) for c in cols]
            for u in range(2):
                t = 2 * tt + u
                for p in range(P):
                    W, A, Bt, Kt = (x[:, u * LANES:(u + 1) * LANES] for x in tiles[p])
                    R = jnp.where(head0, r_cols[p][:, t:t + 1], r_cols[p][:, Tc + t:Tc + t + 1])
                    ls = slice(p * LANES, (p + 1) * LANES)
                    sb_ref[p, t] = s[p]
                    sa = _rowsum(s[p] * A)
                    s[p] = s[p] * W + Bt * sa + Kt * v_ref[t:t + 1, ls]
                    y_ref[t:t + 1, ls] = _rowsum(s[p] * R)
                    sa_ref[t:t + 1, ls] = sa
        for p in range(P):
            st[p] = s[p]

    lspec = pl.BlockSpec((P, None, HEAD_DIM, Tc2), lambda g, c: (g, c, 0, 0))
    rows = pl.BlockSpec((Tc, P * LANES), lambda g, c: ((g // npb) * nc + c, g % npb))
    rowshape = jax.ShapeDtypeStruct(v.shape, F32)
    return pl.pallas_call(
        body, grid=(NP // P, nc), in_specs=[lspec] * 5 + [rows, pl.BlockSpec(onehot.shape, lambda g, c: (0, 0, 0))],
        out_specs=[rows, rows, pl.BlockSpec((P, Tc, HEAD_DIM, LANES), lambda g, c: (g, c, 0, 0))],
        out_shape=[rowshape, rowshape, jax.ShapeDtypeStruct((NP, S, HEAD_DIM, LANES), F32)],
        scratch_shapes=[pltpu.VMEM((P, HEAD_DIM, LANES), F32)], name="rwkv_scan_fwd",
        compiler_params=_params(("parallel", "arbitrary")),
    )(LW, LA, LB, LK, LR, v, onehot)


SCAN_G_ROWS = 16


def _rw_scan_bwd(LW, LA, LB, LK, LR, v, sa, dy, sb, P=4):
    NP, nc, _, Tc2 = LW.shape
    Tc = Tc2 // 2
    onehot = _scan_onehot(Tc)
    npb = 4 // P

    def body(lw, la, lb, lk, lr, v_ref, sa_ref, dy_ref, sb_ref, oh_ref, dv_ref, dk_ref, db_ref, dw_ref, dr_ref, da_ref, dst):
        @pl.when(pl.program_id(1) == 0)
        def _():
            dst[...] = jnp.zeros(dst.shape, F32)
        rid = lax.broadcasted_iota(jnp.int32, (SCAN_G_ROWS, LANES), 0)
        lane = lax.broadcasted_iota(jnp.int32, (SCAN_G_ROWS, LANES), 1)
        own = (((rid % 2) == 0) == (lane < HEAD_DIM)) & (rid < 10)
        lo = lane[0:1] < HEAD_DIM
        nt = lambda rows, tile: lax.dot_general(rows.astype(BF16), tile.astype(BF16), NT_DIMS, preferred_element_type=F32)
        ds = [dst[p] for p in range(P)]
        cols = [(_split_bf16(lw[p]), [ref[p].astype(BF16) for ref in (la, lb, lk)]) for p in range(P)]
        r_cols = [lr[p].astype(F32) for p in range(P)]
        head0 = lax.broadcasted_iota(jnp.int32, (HEAD_DIM, LANES), 1) < HEAD_DIM
        for tt in reversed(range(Tc // 2)):
            tiles = [_key_tiles(c[0], c[1], oh_ref[tt]) for c in cols]
            pending = [[] for _ in range(P)]
            for u in (1, 0):
                t = 2 * tt + u
                for p in range(P):
                    W, A, Bt, Kt = (x[:, u * LANES:(u + 1) * LANES] for x in tiles[p])
                    R = jnp.where(head0, r_cols[p][:, t:t + 1], r_cols[p][:, Tc + t:Tc + t + 1])
                    ls = slice(p * LANES, (p + 1) * LANES)
                    vr, sar, dyr = (ref[t:t + 1, ls] for ref in (v_ref, sa_ref, dy_ref))
                    sp = sb_ref[p, t]
                    s_t = sp * W + Bt * sar + Kt * vr
                    d = ds[p] + R * dyr
                    dv_ref[t:t + 1, ls] = _rowsum(d * Kt)
                    dsar = _rowsum(d * Bt)
                    rows = jnp.where(rid < 2, vr, jnp.where(rid < 4, sar, jnp.where(rid < 6, 1.0, jnp.where(rid < 8, dyr, dsar))))
                    pending[p].append((t, jnp.where(own, rows, 0.0), [d, d * sp, s_t, sp]))
                    ds[p] = d * W + A * dsar
            for p in range(P):
                ls = slice(p * LANES, (p + 1) * LANES)
                (t1, rows1, tiles1), (t0, rows0, tiles0) = pending[p]
                g2 = nt(jnp.concatenate([rows1, rows0], axis=0), jnp.concatenate(tiles1 + tiles0, axis=0))
                for t, g in ((t1, g2[0:SCAN_G_ROWS, 0:2 * LANES]), (t0, g2[SCAN_G_ROWS:, 2 * LANES:])):
                    ga, gb = g[:, 0:LANES], g[:, LANES:2 * LANES]
                    ra, rb = pltpu.roll(ga, HEAD_DIM, 1), pltpu.roll(gb, HEAD_DIM, 1)
                    dk_ref[t:t + 1, ls] = jnp.where(lo, ga[0:1], ra[1:2])
                    db_ref[t:t + 1, ls] = jnp.where(lo, ga[2:3], ra[3:4])
                    dw_ref[t:t + 1, ls] = jnp.where(lo, ra[4:5], ga[5:6])
                    dr_ref[t:t + 1, ls] = jnp.where(lo, gb[6:7], rb[7:8])
                    da_ref[t:t + 1, ls] = jnp.where(lo, rb[8:9], gb[9:10])
        for p in range(P):
            dst[p] = ds[p]

    rev = lambda g, c: (g, nc - 1 - c, 0, 0)
    lspec = pl.BlockSpec((P, None, HEAD_DIM, Tc2), rev)
    rows = pl.BlockSpec((Tc, P * LANES), lambda g, c: ((g // npb) * nc + nc - 1 - c, g % npb))
    return pl.pallas_call(
        body, grid=(NP // P, nc),
        in_specs=[lspec] * 5 + [rows] * 3 + [pl.BlockSpec((P, Tc, HEAD_DIM, LANES), rev),
                                             pl.BlockSpec(onehot.shape, lambda g, c: (0, 0, 0))],
        out_specs=[rows] * 6, out_shape=[jax.ShapeDtypeStruct(v.shape, F32)] * 6,
        scratch_shapes=[pltpu.VMEM((P, HEAD_DIM, LANES), F32)], name="rwkv_scan_bwd",
        compiler_params=_params(("parallel", "arbitrary")),
    )(LW, LA, LB, LK, LR, v, sa, dy, sb, onehot)


def _shift_prev(p, B):
    T, W = p.shape
    return jnp.pad(p.reshape(B, T // B, W), ((0, 0), (1, 0), (0, 0)))[:, :-1].reshape(T, W)


def _shift_next(p, B):
    T, W = p.shape
    return jnp.pad(p.reshape(B, T // B, W), ((0, 0), (0, 1), (0, 0)))[:, 1:].reshape(T, W)


FOX_FWD_BLOCKS = (512, 1024)
FOX_BWD_BLOCKS = (512, 512)


def _layer_step(x, mem, target, W, late_weights=None, early_grads=None, scan_tc=SCAN_TC, fox_fwd_t=FOX_FWD_BLOCKS,
                fox_bwd_t=FOX_BWD_BLOCKS):
    B, S, _ = x.shape
    T = B * S
    x2, tgt2 = x.reshape(T, D_MODEL), target.reshape(T, D_MODEL)
    mem2 = mem.reshape(-1, D_MODEL)
    w_in_t = W["w_in"]
    wt_qkv, wt_rw, wt_mq, wt_gate = (w_in_t[lo:hi] for lo, hi in (COL_QKV, COL_RW, COL_MQ, COL_GATE))
    wt_f = jnp.pad(w_in_t[COL_F[0]:COL_F[1]], ((0, LANES - N_HEADS), (0, 0)))
    row = lambda v: v.reshape(1, -1).astype(F32)
    pre1_g, post1_g, pre2_g, post2_g, mem_g = (row(W[n]) for n in ("pre1_g", "post1_g", "pre2_g", "post2_g", "mem_norm_g"))

    u = _rowwise("rms_pre1", lambda x, g: x * _rms_stat(x) * g, [x2], [pre1_g], [(D_MODEL, BF16)])[0]
    qkv = _mm("proj_qkv", u, wt_qkv, tb=True, out_dtype=BF16)
    f_pad = _mm("proj_f", u, wt_f, tb=True)
    p_rw = _mm("proj_rwkv", u, wt_rw, tb=True)
    memq = _mm("proj_memq", u, wt_mq, tb=True, out_dtype=BF16)
    gate = _mm("proj_gate", u, wt_gate, tb=True, out_dtype=BF16)

    bias_col = W["fox_f_bias"].reshape(N_HEADS, 1).astype(F32)
    f8t = f_pad[:, :N_HEADS].reshape(B, S, N_HEADS).transpose(0, 2, 1)
    c = _fox_c_fwd(f8t, bias_col)
    G = B * N_HEADS
    q_blocks = lambda a, t: a.reshape(G, S // min(t, S), min(t, S), 1)
    k_blocks = lambda a, t: a.reshape(G, S // min(t, S), 1, min(t, S))
    c_col = c.reshape(G, S, 1)
    fox_out, lse = _fox_fwd(qkv, q_blocks(c_col, fox_fwd_t[0]), k_blocks(c, fox_fwd_t[1]), B)

    bd = _head_ones()
    zpad = jnp.zeros((64, BR_W), F32)
    w1 = jnp.concatenate([W["rwkv_w_up"].astype(F32), zpad], axis=0)
    w2 = jnp.concatenate([zpad, W["rwkv_a_up"].astype(F32)], axis=0)
    prep_consts = [row(W["rwkv_mu"]), row(W["rwkv_w0"]), w1, row(W["rwkv_a0"]), w2, W["rwkv_g_up"].astype(F32),
                   row(W["rwkv_k_k"]), row(W["rwkv_k_a"]), bd]
    p_prev = _shift_prev(p_rw, B)
    (rr, rk2, rv, rg), scan_cols = _rw_prep_fwd(p_rw, p_prev, prep_consts, B, scan_tc)
    ry, rsa, sb = _rw_scan_fwd(*scan_cols, rv)
    head_consts = [row(W["rwkv_gn_g"]), row(W["rwkv_gn_b"]), row(W["rwkv_r_k"]), bd]
    rwkv_out = _rw_head_fwd(ry, rr, rk2, rv, rg, head_consts)

    if late_weights is not None:
        W = {**W, **late_weights(rwkv_out)}

    mn = _rowwise("rms_mem", lambda m, g: m * _rms_stat(m) * g, [mem2], [mem_g], [(D_MODEL, BF16)])[0]
    mem_kv = _mm("proj_memkv", mn, W["w_mem_kv"], out_dtype=BF16)
    mem_out = _mem_fwd(memq, mem_kv, B)

    fo = [_mm("branch_" + n, a, W[n], tb=True, out_dtype=BF16)
          for n, a in (("w_fox_out", fox_out), ("w_rwkv_out", rwkv_out), ("w_mem_out", mem_out))]

    def merge(gate, f0, f1, f2):
        return sum(_sigmoid(gate[:, i * D_MODEL:(i + 1) * D_MODEL]) * f for i, f in enumerate((f0, f1, f2)))
    merged = _rowwise("merge", merge, [gate] + fo, [], [(D_MODEL, BF16)])[0]
    y1 = _mm("proj_o", merged, W["w_o"])

    def mid(x, y1, g1, g2):
        h1 = x + y1 * _rms_stat(y1) * g1
        return h1, h1 * _rms_stat(h1) * g2
    h1, u2 = _rowwise("norm_mid", mid, [x2, y1], [post1_g, pre2_g], [(D_MODEL, F32), (D_MODEL, BF16)])
    gt = _mm("ffn_gate", u2, W["w_ffn_gate"], tb=True, out_dtype=BF16)
    up = _mm("ffn_up", u2, W["w_ffn_up"], tb=True, out_dtype=BF16)
    act = _rowwise("swiglu", lambda gt, up: gt * _sigmoid(gt) * up, [gt, up], [], [(D_FF, BF16)])[0]
    ffn = _mm("ffn_down", act, W["w_ffn_down"])

    def tail(h1, ffn, tgt, g):
        err = h1 + ffn * _rms_stat(ffn) * g - tgt
        dh2 = err * (1.0 / D_MODEL)
        dffn, dg = _rms_bwd(dh2, ffn, g)
        loss = 0.5 * jnp.sum(jnp.sum(err * err, axis=1, keepdims=True) * (1.0 / D_MODEL), axis=0, keepdims=True)
        return dh2, dffn, dg, jnp.broadcast_to(loss, (1, LANES))
    dh2, dffn, d_post2, loss = _rowwise("loss_tail", tail, [h1, ffn, tgt2], [post2_g], [(D_MODEL, F32), (D_MODEL, BF16)],
                                        [((1, D_MODEL), F32), ((1, LANES), F32)])
    gw = {"post2_g": d_post2}
    dact = _mm("d_act", dffn, W["w_ffn_down"], tb=True, out_dtype=BF16)
    gw["w_ffn_down"] = _mm("g_ffn_down", act, dffn, ta=True, out_dtype=BF16)

    def swiglu_bwd(dact, gt, up):
        s = _sigmoid(gt)
        return dact * up * s * (1.0 + gt * (1.0 - s)), dact * gt * s
    dgt, dup = _rowwise("swiglu_bwd", swiglu_bwd, [dact, gt, up], [], [(D_FF, BF16)] * 2)
    du2 = _mm("d_u2_gate", dgt, W["w_ffn_gate"])
    du2 = _mm("d_u2_up", dup, W["w_ffn_up"], add=du2)
    gw["w_ffn_gate"] = _mm("g_ffn_gate", dgt, u2, ta=True, out_dtype=BF16)
    gw["w_ffn_up"] = _mm("g_ffn_up", dup, u2, ta=True, out_dtype=BF16)

    def mid_bwd(du2, dh2, h1, y1, g1, g2):
        dh1_n, d_pre2 = _rms_bwd(du2, h1, g2)
        dh1 = dh2 + dh1_n
        dy1, d_post1 = _rms_bwd(dh1, y1, g1)
        return dh1, dy1, d_post1, d_pre2
    dh1, dy1, gw["post1_g"], gw["pre2_g"] = _rowwise(
        "norm_mid_bwd", mid_bwd, [du2, dh2, h1, y1], [post1_g, pre2_g], [(D_MODEL, F32), (D_MODEL, BF16)],
        [((1, D_MODEL), F32)] * 2)
    dmerged = _mm("d_merged", dy1, W["w_o"], tb=True, out_dtype=BF16)
    gw["w_o"] = _mm("g_w_o", merged, dy1, ta=True, out_dtype=BF16)

    def merge_bwd(dm, gate, f0, f1, f2):
        s = [_sigmoid(gate[:, i * D_MODEL:(i + 1) * D_MODEL]) for i in range(3)]
        dgate = jnp.concatenate([dm * f * si * (1.0 - si) for f, si in zip((f0, f1, f2), s)], axis=1)
        return dm * s[0], dm * s[1], dm * s[2], dgate
    dfo0, dfo1, dfo2, dgate = _rowwise("merge_bwd", merge_bwd, [dmerged, gate] + fo, [],
                                       [(D_MODEL, BF16)] * 3 + [(3 * D_MODEL, BF16)])
    d_branch = {}
    for n, a, dfo in (("w_fox_out", fox_out, dfo0), ("w_rwkv_out", rwkv_out, dfo1), ("w_mem_out", mem_out, dfo2)):
        d_branch[n] = _mm("d_in_" + n, dfo, W[n], out_dtype=BF16)
        gw[n] = _mm("g_" + n, dfo, a, ta=True, out_dtype=BF16)

    dmemq, dkm, dvm = _mem_bwd(memq, mem_kv, d_branch["w_mem_out"], B)
    dmem_kv = jnp.concatenate([dkm, dvm], axis=1)
    gw["w_mem_kv"] = _mm("g_w_mem_kv", mn, dmem_kv, ta=True, out_dtype=BF16)
    dmn = _mm("d_mn", dmem_kv, W["w_mem_kv"], tb=True)
    gw["mem_norm_g"] = _rowwise("rms_mem_bwd", lambda d, m, g: _rms_bwd(d, m, g)[1], [dmn, mem2], [mem_g], [],
                                [((1, D_MODEL), F32)])[0]
    if early_grads is not None:
        d_branch["w_fox_out"] = early_grads(gw, d_branch["w_fox_out"])

    dfq, dfk, dfv, dck, dcq = _fox_bwd(qkv, fox_out, d_branch["w_fox_out"], q_blocks(c_col, fox_bwd_t[0]),
                                       k_blocks(c, fox_bwd_t[1]), q_blocks(lse, fox_bwd_t[0]), B)
    df8t, dbias = _fox_c_bwd(dck.reshape(B, N_HEADS, S) + dcq.reshape(B, N_HEADS, S), f8t, bias_col)
    gw["fox_f_bias"] = jnp.sum(dbias, axis=0).reshape(1, N_HEADS)
    dqkv = jnp.concatenate([dfq, dfk, dfv], axis=1)
    df_pad = jnp.pad(df8t.transpose(0, 2, 1).reshape(T, N_HEADS), ((0, 0), (0, LANES - N_HEADS))).astype(BF16)

    dry, drg, dr_h, dk2_h, dv_h, gw["rwkv_gn_g"], gw["rwkv_gn_b"], gw["rwkv_r_k"] = _rw_head_bwd(
        d_branch["w_rwkv_out"], ry, rr, rk2, rv, rg, head_consts)
    dv_s, dk2_s, db_s, dw_s, dr_s, da_s = _rw_scan_bwd(*scan_cols, rv, rsa, dry, sb)
    dP, dPp, gw["rwkv_mu"], gw["rwkv_w0"], d_w1, gw["rwkv_a0"], d_w2, gw["rwkv_g_up"], gw["rwkv_k_k"], gw["rwkv_k_a"] = \
        _rw_prep_bwd(p_rw, p_prev, [dr_s, dr_h, dw_s, dk2_s, dk2_h, dv_s, dv_h, da_s, db_s, drg], prep_consts)
    gw["rwkv_w_up"], gw["rwkv_a_up"] = d_w1[:64], d_w2[64:]
    dp_rw = (dP + _shift_next(dPp, B)).astype(BF16)

    du = _mm("d_u_qkv", dqkv, wt_qkv)
    du = _mm("d_u_f", df_pad, wt_f, add=du)
    du = _mm("d_u_rwkv", dp_rw, wt_rw, add=du)
    du = _mm("d_u_memq", dmemq, wt_mq, add=du)
    du = _mm("d_u_gate", dgate, wt_gate, add=du)
    gw["w_in"] = jnp.concatenate(
        [_mm("g_w_qkv", dqkv, u, ta=True, out_dtype=BF16), _mm("g_w_f", df_pad, u, ta=True, out_dtype=BF16)[:N_HEADS],
         _mm("g_w_rwkv", dp_rw, u, ta=True, out_dtype=BF16), _mm("g_w_memq", dmemq, u, ta=True, out_dtype=BF16),
         _mm("g_w_gate", dgate, u, ta=True, out_dtype=BF16)], axis=0)

    def pre1_bwd(du, dh1, x, g):
        dx, dg = _rms_bwd(du, x, g)
        return dh1 + dx, dg
    dx, gw["pre1_g"] = _rowwise("rms_pre1_bwd", pre1_bwd, [du, dh1, x2], [pre1_g], [(D_MODEL, F32)], [((1, D_MODEL), F32)])
    return loss[0, 0], dx.reshape(B, S, D_MODEL), gw


TRANSPOSED = ("w_in", "w_ffn_gate", "w_ffn_up", "w_fox_out", "w_rwkv_out", "w_mem_out")
LORA = ("rwkv_w_up", "rwkv_a_up", "rwkv_g_up")
ROW_SHARDED = ("w_mem_kv", "w_o", "w_ffn_down")
FIRST = ("w_in",) + LORA
LATER = ("w_ffn_gate", "w_ffn_up", "w_mem_kv", "w_o", "w_ffn_down", "w_fox_out", "w_rwkv_out", "w_mem_out")
BIG = FIRST + LATER
SMALL = ("pre1_g", "post1_g", "pre2_g", "post2_g", "mem_norm_g", "fox_f_bias", "rwkv_mu", "rwkv_w0", "rwkv_a0", "rwkv_k_k",
         "rwkv_k_a", "rwkv_r_k", "rwkv_gn_g", "rwkv_gn_b")
WEIGHTS = ("pre1_g", "post1_g", "pre2_g", "post2_g", "mem_norm_g", "w_in", "fox_f_bias", "rwkv_mu", "rwkv_w0", "rwkv_w_up",
           "rwkv_a0", "rwkv_a_up", "rwkv_g_up", "rwkv_k_k", "rwkv_k_a", "rwkv_r_k", "rwkv_gn_g", "rwkv_gn_b", "w_mem_kv",
           "w_fox_out", "w_rwkv_out", "w_mem_out", "w_o", "w_ffn_gate", "w_ffn_up", "w_ffn_down")
WIRE_W = 1024
WIRE_ROW_ALIGN = 16
WIRE_HALF_ALIGN = 128


def _wire_rows(name, shard_shape):
    r, c = shard_shape
    if name in ROW_SHARDED:
        return r
    return -(-c // WIRE_ROW_ALIGN) * WIRE_ROW_ALIGN if r == WIRE_W else (r * c) // WIRE_W


def _to_wire(name, a):
    if name not in ROW_SHARDED:
        a = jnp.swapaxes(a, -1, -2)
    lead, (n, w) = a.shape[:-2], a.shape[-2:]
    if w != WIRE_W:
        return a.reshape(lead + ((n * w) // WIRE_W, WIRE_W))
    return jnp.pad(a, [(0, 0)] * len(lead) + [(0, (-n) % WIRE_ROW_ALIGN), (0, 0)])


def _from_wire(name, a, shard_shape):
    r, c = shard_shape
    if name in ROW_SHARDED:
        return a
    return a[..., :c, :] if r == WIRE_W else a.reshape(a.shape[:-2] + (c, r))


def _wire_layout(shard_shapes, names):
    layout, off = {}, 0
    for n in names:
        rows = _wire_rows(n, shard_shapes[n])
        layout[n] = (off, rows)
        off += rows
    return layout, -(-off // (2 * WIRE_HALF_ALIGN)) * WIRE_HALF_ALIGN


def _wire_pack(blocks, half_rows):
    a = jnp.concatenate(blocks, axis=-2)
    lead = a.shape[:-2]
    a = jnp.pad(a, [(0, 0)] * len(lead) + [(0, 2 * half_rows - a.shape[-2]), (0, 0)])
    return a.reshape(lead + (2, half_rows, WIRE_W))


def _my_place():
    return lax.axis_index("x"), lax.axis_index("y"), lax.axis_index("c")


def _other_chips(x, y):
    return [(1 - x, y), (x, 1 - y), (1 - x, 1 - y)]


ANY = pl.BlockSpec(memory_space=pl.ANY)


def _gather_weights(packed):
    _, R, L = packed.shape
    me = 2 * lax.axis_index("x") + lax.axis_index("y")
    base = lax.dynamic_update_index_in_dim(jnp.zeros((N_CHIPS, 2, R, L), packed.dtype), packed, me, 0)

    def body(in_ref, base_ref, out_ref, send_sems, recv_sems):
        x, y, c = _my_place()
        chip = lambda px, py: 2 * px + py
        sibling = (x, y, 1 - c)
        others = _other_chips(x, y)

        def copy(k, src, dst, to):
            return pltpu.make_async_remote_copy(src_ref=src, dst_ref=dst, send_sem=send_sems.at[k], recv_sem=recv_sems.at[k],
                                                device_id=to, device_id_type=MESH)

        sends = [copy(j, in_ref.at[c], out_ref.at[chip(x, y), c], (px, py, c)) for j, (px, py) in enumerate(others)]
        for cp in sends:
            cp.start()
        passed = [copy(3 + j, out_ref.at[chip(px, py), c], out_ref.at[chip(px, py), c], sibling)
                  for j, (px, py) in enumerate(others)]
        for j, (px, py) in enumerate(others):
            copy(j, in_ref.at[c], out_ref.at[chip(px, py), c], (px, py, c)).wait_recv()
            passed[j].start()
        for j, (px, py) in enumerate(others):
            copy(3 + j, in_ref.at[1 - c], out_ref.at[chip(px, py), 1 - c], sibling).wait_recv()
        for cp in sends + passed:
            cp.wait_send()

    return pl.pallas_call(
        body, out_shape=jax.ShapeDtypeStruct(base.shape, base.dtype), in_specs=[ANY, ANY], out_specs=ANY,
        input_output_aliases={1: 0}, scratch_shapes=[pltpu.SemaphoreType.DMA((6,)), pltpu.SemaphoreType.DMA((6,))],
        name="gather_weights",
    )(packed, base)


HBM_SPEC = pl.BlockSpec(memory_space=pltpu.HBM)
SEM_SPEC = pl.BlockSpec(memory_space=pltpu.SEMAPHORE)
DATAFLOW = pltpu.SideEffectType.DATAFLOW_SIDE_EFFECTING


def _in_hbm(a):
    return pltpu.with_memory_space_constraint(a, pltpu.HBM)


def _split_start(name, n_copies, src, land, start_copies):
    def body(src_ref, land_ref, send_sems, recv_sems, src_thru, land_thru, token):
        start_copies(src_ref, land_ref, send_sems, recv_sems)
        token[...] = jnp.zeros(token.shape, token.dtype)

    return pl.pallas_call(
        body, name=name,
        out_shape=(pltpu.SemaphoreType.DMA((n_copies,)), pltpu.SemaphoreType.DMA((n_copies,)), pltpu.HBM(src.shape, src.dtype),
                   pltpu.HBM(land.shape, land.dtype), jax.ShapeDtypeStruct((8, LANES), F32)),
        in_specs=(HBM_SPEC, HBM_SPEC),
        out_specs=(SEM_SPEC, SEM_SPEC, HBM_SPEC, HBM_SPEC, pl.BlockSpec(memory_space=pltpu.VMEM)),
        input_output_aliases={0: 2, 1: 3}, compiler_params=pltpu.CompilerParams(has_side_effects=DATAFLOW),
    )(_in_hbm(src), _in_hbm(land))


def _split_wait(name, handle, after, wait_copies):
    send_sems, recv_sems, src, land = handle[:4]

    def body(src_ref, land_ref, send_sems, recv_sems, after_ref, src_dead, land_out):
        wait_copies(src_ref, land_ref, send_sems, recv_sems)

    return pl.pallas_call(
        body, name=name, out_shape=(pltpu.HBM(src.shape, src.dtype), pltpu.HBM(land.shape, land.dtype)),
        in_specs=(HBM_SPEC, HBM_SPEC, SEM_SPEC, SEM_SPEC, ANY), out_specs=(HBM_SPEC, HBM_SPEC),
        input_output_aliases={0: 0, 1: 1}, compiler_params=pltpu.CompilerParams(has_side_effects=DATAFLOW),
    )(src, land, send_sems, recv_sems, after)[1]


def _late_gather_copies(src_ref, land_ref, send_sems, recv_sems):
    x, y, c = _my_place()
    me = 2 * x + y
    started, awaited = [], []
    for j, (px, py) in enumerate(_other_chips(x, y)):
        for core in range(2):
            started.append(pltpu.make_async_remote_copy(
                src_ref=src_ref.at[c], dst_ref=land_ref.at[me, c], send_sem=send_sems.at[2 * j + core],
                recv_sem=recv_sems.at[2 * j + c], device_id=(px, py, core), device_id_type=MESH))
            awaited.append(pltpu.make_async_remote_copy(
                src_ref=src_ref.at[core], dst_ref=land_ref.at[2 * px + py, core], send_sem=send_sems.at[2 * j + core],
                recv_sem=recv_sems.at[2 * j + core], device_id=(px, py, core), device_id_type=MESH))
    return started, awaited


def _gather_late_start(packed):
    _, R, L = packed.shape
    me = 2 * lax.axis_index("x") + lax.axis_index("y")
    land = lax.dynamic_update_index_in_dim(jnp.zeros((N_CHIPS, 2, R, L), packed.dtype), packed, me, 0)

    def start(src_ref, land_ref, send_sems, recv_sems):
        for cp in _late_gather_copies(src_ref, land_ref, send_sems, recv_sems)[0]:
            cp.start()

    return _split_start("gather_late_start", 6, packed, land, start)


def _gather_late_wait(handle, after):
    def wait(src_ref, land_ref, send_sems, recv_sems):
        started, awaited = _late_gather_copies(src_ref, land_ref, send_sems, recv_sems)
        for cp in started:
            cp.wait_send()
        for cp in awaited:
            cp.wait_recv()

    return _split_wait("gather_late_wait", handle, after, wait)


def _early_scatter_copies(src_ref, land_ref, send_sems, recv_sems):
    x, y, c = _my_place()
    me = 4 * x + 2 * y + c
    copies = []
    for k in range(1, N_DEV):
        px, py, pc = x ^ (k >> 2), y ^ ((k >> 1) & 1), c ^ (k & 1)
        copies.append(pltpu.make_async_remote_copy(
            src_ref=src_ref.at[2 * px + py, pc], dst_ref=land_ref.at[me], send_sem=send_sems.at[k - 1],
            recv_sem=recv_sems.at[k - 1], device_id=(px, py, pc), device_id_type=MESH))
    return copies


def _scatter_early_start(parts):
    n, _, R, L = parts.shape
    x, y, c = _my_place()
    own = lax.dynamic_index_in_dim(lax.dynamic_index_in_dim(parts, 2 * x + y, 0, keepdims=False), c, 0, keepdims=False)
    land = lax.dynamic_update_index_in_dim(jnp.zeros((N_DEV, R, L), parts.dtype), own, 4 * x + 2 * y + c, 0)

    def start(src_ref, land_ref, send_sems, recv_sems):
        for cp in _early_scatter_copies(src_ref, land_ref, send_sems, recv_sems):
            cp.start()

    return _split_start("scatter_early_start", N_DEV - 1, parts, land, start)


def _scatter_early_wait(handle, after):
    def wait(src_ref, land_ref, send_sems, recv_sems):
        for cp in _early_scatter_copies(src_ref, land_ref, send_sems, recv_sems):
            cp.wait_send()
            cp.wait_recv()

    return _split_wait("scatter_early_wait", handle, after, wait)


def _sum_slots(parts):
    n, R, L = parts.shape
    tr = _tile_rows(R, WIRE_HALF_ALIGN)

    def body(p_ref, o_ref):
        acc = p_ref[0].astype(F32)
        for i in range(1, n):
            acc = acc + p_ref[i].astype(F32)
        o_ref[...] = acc

    return pl.pallas_call(
        body, grid=(R // tr,), in_specs=[pl.BlockSpec((n, tr, L), lambda i: (0, i, 0))],
        out_specs=pl.BlockSpec((tr, L), lambda i: (i, 0)), out_shape=jax.ShapeDtypeStruct((R, L), F32), name="sum_slots",
        compiler_params=_params(("parallel",)),
    )(parts)


def _pair_exchange(parts):
    n, _, R, L = parts.shape

    def body(in_ref, out_ref, send_sems, recv_sems):
        x, y, c = _my_place()
        copies = [pltpu.make_async_remote_copy(src_ref=in_ref.at[s, 1 - c], dst_ref=out_ref.at[s], send_sem=send_sems.at[s],
                                               recv_sem=recv_sems.at[s], device_id=(x, y, 1 - c), device_id_type=MESH)
                  for s in range(n)]
        for cp in copies:
            cp.start()
        for cp in copies:
            cp.wait()

    return pl.pallas_call(
        body, out_shape=jax.ShapeDtypeStruct((n, R, L), parts.dtype), in_specs=[ANY], out_specs=ANY,
        scratch_shapes=[pltpu.SemaphoreType.DMA((n,)), pltpu.SemaphoreType.DMA((n,))], name="pair_exchange",
    )(parts)


def _pair_add(a, b):
    n, R, L = a.shape
    tr = _tile_rows(R, WIRE_HALF_ALIGN)

    def body(a_ref, b_ref, o_ref):
        o_ref[...] = (a_ref[...].astype(F32) + b_ref[...].astype(F32)).astype(o_ref.dtype)

    spec = pl.BlockSpec((n, tr, L), lambda i: (0, i, 0))
    return pl.pallas_call(
        body, grid=(R // tr,), in_specs=[spec, spec], out_specs=spec, out_shape=jax.ShapeDtypeStruct(a.shape, a.dtype),
        name="pair_add", compiler_params=_params(("parallel",)),
    )(a, b)


def _scatter_grads(parts):
    n, R, L = parts.shape

    def body(in_ref, out_ref, send_sems, recv_sems):
        x, y, c = _my_place()
        copies = [pltpu.make_async_remote_copy(src_ref=in_ref.at[2 * px + py], dst_ref=out_ref.at[j], send_sem=send_sems.at[j],
                                               recv_sem=recv_sems.at[j], device_id=(px, py, c), device_id_type=MESH)
                  for j, (px, py) in enumerate(_other_chips(x, y))]
        for cp in copies:
            cp.start()
        for cp in copies:
            cp.wait()

    return pl.pallas_call(
        body, out_shape=jax.ShapeDtypeStruct((n - 1, R, L), parts.dtype), in_specs=[ANY], out_specs=ANY,
        scratch_shapes=[pltpu.SemaphoreType.DMA((3,)), pltpu.SemaphoreType.DMA((3,))], name="scatter_grads",
    )(parts)


def _sum_chips(own, others):
    n, R, L = others.shape
    tr = _tile_rows(R, WIRE_HALF_ALIGN)

    def body(a_ref, p_ref, o_ref):
        acc = a_ref[...].astype(F32)
        for i in range(n):
            acc = acc + p_ref[i].astype(F32)
        o_ref[...] = acc

    return pl.pallas_call(
        body, grid=(R // tr,), in_specs=[pl.BlockSpec((tr, L), lambda i: (i, 0)), pl.BlockSpec((n, tr, L), lambda i: (0, i, 0))],
        out_specs=pl.BlockSpec((tr, L), lambda i: (i, 0)), out_shape=jax.ShapeDtypeStruct((R, L), F32), name="sum_grads",
        compiler_params=_params(("parallel",)),
    )(own, others)


def _tile_rows(R, cap=2048):
    best = 8
    for t in range(8, min(R, cap) + 1, 8):
        if R % t == 0:
            best = t
    return best if R % 8 == 0 else R


def _swap_halves(half):
    def body(in_ref, out_ref, send_sem, recv_sem):
        x, y, c = _my_place()
        cp = pltpu.make_async_remote_copy(src_ref=in_ref, dst_ref=out_ref, send_sem=send_sem, recv_sem=recv_sem,
                                          device_id=(x, y, 1 - c), device_id_type=MESH)
        cp.start()
        cp.wait()

    return pl.pallas_call(
        body, out_shape=jax.ShapeDtypeStruct(half.shape, half.dtype), in_specs=[ANY], out_specs=ANY,
        scratch_shapes=[pltpu.SemaphoreType.DMA, pltpu.SemaphoreType.DMA], name="swap_halves",
    )(half)


def _allreduce_small(v):
    R, L = v.shape

    def body(in_ref, out_ref, buf, send_sems, recv_sems):
        x, y, c = _my_place()
        me = 4 * x + 2 * y + c
        buf[me] = in_ref[...]
        started = []
        for k in range(1, N_DEV):
            to = (x ^ (k >> 2), y ^ ((k >> 1) & 1), c ^ (k & 1))
            cp = pltpu.make_async_remote_copy(src_ref=in_ref, dst_ref=buf.at[me], send_sem=send_sems.at[k - 1],
                                              recv_sem=recv_sems.at[k - 1], device_id=to, device_id_type=MESH)
            cp.start()
            started.append(cp)
        for cp in started:
            cp.wait()
        acc = buf[0]
        for i in range(1, N_DEV):
            acc = acc + buf[i]
        out_ref[...] = acc

    vm = pl.BlockSpec(memory_space=pltpu.VMEM)
    return pl.pallas_call(
        body, out_shape=jax.ShapeDtypeStruct((R, L), F32), in_specs=[vm], out_specs=vm,
        scratch_shapes=[pltpu.VMEM((N_DEV, R, L), F32), pltpu.SemaphoreType.DMA((7,)), pltpu.SemaphoreType.DMA((7,))],
        name="allreduce_small",
    )(v)


def _adamw(name, w, g, m, v):
    shape = w.shape
    C = shape[-1]
    R = int(np.prod(shape[:-1]))
    args = [a.reshape(R, C).astype(F32) for a in (w, g, m, v)]
    tr = _tile_rows(R, 256)

    def body(w_ref, g_ref, m_ref, v_ref, d_ref, nm_ref, nv_ref):
        g = g_ref[...]
        m = ADAM_B1 * m_ref[...] + (1.0 - ADAM_B1) * g
        v = ADAM_B2 * v_ref[...] + (1.0 - ADAM_B2) * (g * g)
        m_hat = m / (1.0 - ADAM_B1 ** ADAM_STEP)
        v_hat = v / (1.0 - ADAM_B2 ** ADAM_STEP)
        d_ref[...] = -ADAM_LR * (m_hat / (jnp.sqrt(v_hat) + ADAM_EPS) + ADAM_WD * w_ref[...])
        nm_ref[...] = m
        nv_ref[...] = v

    spec = pl.BlockSpec((tr, C), lambda i: (i, 0))
    out = pl.pallas_call(
        body, grid=(R // tr,), in_specs=[spec] * 4, out_specs=[spec] * 3,
        out_shape=[jax.ShapeDtypeStruct((R, C), F32)] * 3, name="adamw_" + name, compiler_params=_params(("parallel",)),
    )(*args)
    return [o.reshape(shape) for o in out]


def kernel(x, mem, pre1_g, post1_g, pre2_g, post2_g, mem_norm_g, w_in, fox_f_bias, rwkv_mu, rwkv_w0, rwkv_w_up, rwkv_a0, rwkv_a_up, rwkv_g_up, rwkv_k_k, rwkv_k_a, rwkv_r_k, rwkv_gn_g, rwkv_gn_b, w_mem_kv, w_fox_out, w_rwkv_out, w_mem_out, w_o, w_ffn_gate, w_ffn_up, w_ffn_down, loss_target, m_pre1_g, m_post1_g, m_pre2_g, m_post2_g, m_mem_norm_g, m_w_in, m_fox_f_bias, m_rwkv_mu, m_rwkv_w0, m_rwkv_w_up, m_rwkv_a0, m_rwkv_a_up, m_rwkv_g_up, m_rwkv_k_k, m_rwkv_k_a, m_rwkv_r_k, m_rwkv_gn_g, m_rwkv_gn_b, m_w_mem_kv, m_w_fox_out, m_w_rwkv_out, m_w_mem_out, m_w_o, m_w_ffn_gate, m_w_ffn_up, m_w_ffn_down, v_pre1_g, v_post1_g, v_pre2_g, v_post2_g, v_mem_norm_g, v_w_in, v_fox_f_bias, v_rwkv_mu, v_rwkv_w0, v_rwkv_w_up, v_rwkv_a0, v_rwkv_a_up, v_rwkv_g_up, v_rwkv_k_k, v_rwkv_k_a, v_rwkv_r_k, v_rwkv_gn_g, v_rwkv_gn_b, v_w_mem_kv, v_w_fox_out, v_w_rwkv_out, v_w_mem_out, v_w_o, v_w_ffn_gate, v_w_ffn_up, v_w_ffn_down):
    given = dict(locals())
    w_loc = {n: given[n] for n in WEIGHTS}
    m_loc = {n: given["m_" + n] for n in WEIGHTS}
    v_loc = {n: given["v_" + n] for n in WEIGHTS}

    shard_shapes = {n: tuple(w_loc[n].shape[1:]) for n in BIG}
    groups = {names: _wire_layout(shard_shapes, names) for names in (FIRST, LATER)}
    core = lax.axis_index("c")

    def pack_weights(names):
        return _wire_pack([_to_wire(n, w_loc[n][0].astype(BF16)) for n in names], groups[names][1])

    def unpack_weights(gathered, names):
        layout, half_rows = groups[names]
        gathered = gathered.reshape(N_CHIPS, 2 * half_rows, WIRE_W)
        out = {}
        for n in names:
            off, rows = layout[n]
            blocks = _from_wire(n, gathered[:, off:off + rows], shard_shapes[n])
            if n in LORA:
                out[n] = blocks.transpose(2, 0, 1).reshape(blocks.shape[2], -1)
            else:
                out[n] = blocks.reshape(-1, blocks.shape[2])
        return out

    def pack_grads(gw, names):
        blocks = []
        for n in names:
            r, c = shard_shapes[n]
            g = gw[n].astype(BF16)
            if n in LORA:
                g = g.reshape(r, N_CHIPS, c).transpose(1, 0, 2)
            elif n in TRANSPOSED:
                g = jnp.swapaxes(g.reshape(N_CHIPS, c, r), 1, 2)
            else:
                g = g.reshape(N_CHIPS, r, c)
            blocks.append(_to_wire(n, g))
        return _wire_pack(blocks, groups[names][1])

    def unpack_grads(half, other, names):
        layout, _ = groups[names]
        reduced = jnp.where(core == 0, jnp.concatenate([half, other]), jnp.concatenate([other, half]))
        out = {}
        for n in names:
            off, rows = layout[n]
            g = _from_wire(n, reduced[off:off + rows], shard_shapes[n])
            out[n] = g if n in ROW_SHARDED else g.T
        return out

    first = _gather_weights(pack_weights(FIRST))
    late = _gather_late_start(pack_weights(LATER) + (first[0, 0, 0, 0] * 0).astype(BF16))
    W = unpack_weights(first, FIRST)
    W.update({n: w_loc[n][0] for n in SMALL})
    W["pre1_g"] = W["pre1_g"] + late[4][0, 0]
    early = []

    def late_weights(after):
        return unpack_weights(_gather_late_wait(late, after), LATER)

    def early_grads(gw, thru):
        early.append(_scatter_early_start(pack_grads(gw, LATER)))
        return thru + early[0][4][0, 0].astype(thru.dtype)

    loss, grad_x, gw = _layer_step(x, mem, loss_target, W, late_weights, early_grads)

    packed = pack_grads(gw, FIRST)
    own_halves = lax.dynamic_index_in_dim(packed, core, axis=1, keepdims=False)
    chip_sums = _pair_add(own_halves, _pair_exchange(packed))
    own_chip = lax.dynamic_index_in_dim(chip_sums, 2 * lax.axis_index("x") + lax.axis_index("y"), axis=0, keepdims=False)
    half_first = _sum_chips(own_chip, _scatter_grads(chip_sums))
    half_later = _sum_slots(_scatter_early_wait(early[0], half_first))
    rows_first = groups[FIRST][1]
    other = _swap_halves(jnp.concatenate([half_first, half_later]))
    g_shard = {**unpack_grads(half_first, other[:rows_first], FIRST), **unpack_grads(half_later, other[rows_first:], LATER)}

    small_shapes = [w_loc[n].shape[1:] for n in SMALL] + [(1,)]
    n_small = sum(int(np.prod(s)) for s in small_shapes)
    small_rows = -(-n_small // (8 * LANES)) * 8
    flat = jnp.concatenate([gw[n].reshape(-1) for n in SMALL] + [loss.reshape(1)])
    flat = jnp.pad(flat, (0, small_rows * LANES - n_small)).reshape(small_rows, LANES).reshape(-1)
    small, off = [], 0
    flat = _allreduce_small(flat.reshape(small_rows, LANES)).reshape(-1)
    for s in small_shapes:
        cnt = int(np.prod(s))
        small.append(flat[off:off + cnt].reshape(s))
        off += cnt
    g_small = dict(zip(SMALL, small[:-1]))
    loss = small[-1][0]

    grads, deltas, new_m, new_v = [], [], [], []
    for n in WEIGHTS:
        g = (g_shard[n] if n in g_shard else g_small[n]).reshape(w_loc[n].shape)
        d, nm, nv = _adamw(n, w_loc[n], g, m_loc[n], v_loc[n])
        grads.append(g)
        deltas.append(d)
        new_m.append(nm)
        new_v.append(nv)
    return (loss, grad_x, *grads, *deltas, *new_m, *new_v)
```

```python
import functools
import math

import numpy as np
import jax
import jax.numpy as jnp
from jax import lax
from jax.experimental import pallas as pl
from jax.experimental.pallas import tpu as pltpu

F32, BF16 = jnp.float32, jnp.bfloat16
MESH = pl.DeviceIdType.MESH

D_MODEL = 1024
HEAD_DIM = 64
N_HEADS = 8
BR_W = 512
MEM_HEADS = 4
MEM_HEAD_DIM = 128
D_FF = 2816
NORM_EPS = 1e-6
GN_EPS = 64e-5
N_CHIPS = 4
N_DEV = 8
LANES = 128
VMEM_LIMIT = 48 * 1024 * 1024

ADAM_LR, ADAM_B1, ADAM_B2, ADAM_EPS, ADAM_WD, ADAM_STEP = 0.001, 0.9, 0.999, 1e-08, 0.01, 10

FOX_COLS = 3 * BR_W + N_HEADS
RWKV_COLS = 3 * BR_W + 64 + 64 + 128
COL_QKV = (0, 3 * BR_W)
COL_F = (3 * BR_W, FOX_COLS)
COL_RW = (FOX_COLS, FOX_COLS + RWKV_COLS)
COL_MQ = (COL_RW[1], COL_RW[1] + BR_W)
COL_GATE = (COL_MQ[1], COL_MQ[1] + 3 * D_MODEL)

NT_DIMS = (((1,), (1,)), ((), ()))
TN_DIMS = (((0,), (0,)), ((), ()))


def _params(sem=None, **kw):
    return pltpu.CompilerParams(dimension_semantics=sem, vmem_limit_bytes=VMEM_LIMIT, **kw)


def _sigmoid(x):
    return 1.0 / (1.0 + jnp.exp(-x))


def _log_sigmoid(x):
    return jnp.minimum(x, 0.0) - jnp.log(1.0 + jnp.exp(-jnp.abs(x)))


def _bdot(a, b, dims=None):
    a, b = a.astype(BF16), b.astype(BF16)
    if dims is None:
        return jnp.dot(a, b, preferred_element_type=F32)
    return lax.dot_general(a, b, dims, preferred_element_type=F32)


def _hdot(a, ones):
    ones = ones.astype(BF16)
    hi = a.astype(BF16)
    r1 = a - hi.astype(F32)
    mid = r1.astype(BF16)
    lo = (r1 - mid.astype(F32)).astype(BF16)
    dot = lambda x: jnp.dot(x, ones, preferred_element_type=F32)
    return dot(hi) + dot(mid) + dot(lo)


def _tile(n, cap):
    best = None
    for t in range(LANES, min(n, cap) + 1, LANES):
        if n % t == 0:
            best = t
    return best or n


def _rowwise(name, fn, rows, consts, outs, accs=(), tm=256):
    T = rows[0].shape[0]
    tm = min(tm, T)
    assert T % tm == 0
    nr, nc, no, na = len(rows), len(consts), len(outs), len(accs)

    def body(*refs):
        res = fn(*[r[...].astype(F32) for r in refs[:nr + nc]])
        if not isinstance(res, (tuple, list)):
            res = (res,)
        orefs, arefs = refs[nr + nc:nr + nc + no], refs[nr + nc + no:]
        for ref, val in zip(orefs, res[:no]):
            ref[...] = val.astype(ref.dtype)
        if na:
            @pl.when(pl.program_id(0) == 0)
            def _():
                for ref in arefs:
                    ref[...] = jnp.zeros(ref.shape, ref.dtype)
            for ref, val in zip(arefs, res[no:]):
                ref[...] += val

    in_specs = ([pl.BlockSpec((tm, r.shape[1]), lambda i: (i, 0)) for r in rows]
                + [pl.BlockSpec(c.shape, lambda i: (0, 0)) for c in consts])
    out_specs = ([pl.BlockSpec((tm, w), lambda i: (i, 0)) for w, _ in outs]
                 + [pl.BlockSpec(s, lambda i: (0, 0)) for s, _ in accs])
    out_shape = ([jax.ShapeDtypeStruct((T, w), dt) for w, dt in outs]
                 + [jax.ShapeDtypeStruct(s, dt) for s, dt in accs])
    return pl.pallas_call(
        body, grid=(T // tm,), in_specs=in_specs, out_specs=out_specs, out_shape=out_shape, name=name,
        compiler_params=_params(("arbitrary",) if na else ("parallel",)),
    )(*rows, *consts)


MM_TILE_CAP = 1408
MM_WHOLE_K = 2048


def _mm(name, a, b, ta=False, tb=False, out_dtype=F32, add=None):
    M, K = (a.shape[1], a.shape[0]) if ta else a.shape
    K2, N = (b.shape[1], b.shape[0]) if tb else b.shape
    assert K == K2
    tm, tn = _tile(M, MM_TILE_CAP), _tile(N, MM_TILE_CAP)
    tk = K if K <= MM_WHOLE_K else _tile(K, MM_TILE_CAP)
    assert M % tm == 0 and N % tn == 0 and K % tk == 0
    nk = K // tk
    a_dim, b_dim = (0 if ta else 1), (1 if tb else 0)

    def body(*refs):
        a_ref, b_ref = refs[0], refs[1]
        n_in = 2 if add is None else 3
        o_ref, acc = refs[n_in], (refs[n_in + 1] if nk > 1 else None)
        k = pl.program_id(2)
        part = lax.dot_general(a_ref[...].astype(BF16), b_ref[...].astype(BF16),
                               (((a_dim,), (b_dim,)), ((), ())), preferred_element_type=F32)

        def finish(r):
            if add is not None:
                r = r + refs[2][...].astype(F32)
            o_ref[...] = r.astype(o_ref.dtype)

        if nk == 1:
            finish(part)
            return

        @pl.when(k == 0)
        def _():
            acc[...] = part

        @pl.when(k > 0)
        def _():
            acc[...] += part

        @pl.when(k == nk - 1)
        def _():
            finish(acc[...])

    a_spec = pl.BlockSpec((tk, tm), lambda i, j, k: (k, i)) if ta else pl.BlockSpec((tm, tk), lambda i, j, k: (i, k))
    b_spec = pl.BlockSpec((tn, tk), lambda i, j, k: (j, k)) if tb else pl.BlockSpec((tk, tn), lambda i, j, k: (k, j))
    o_spec = pl.BlockSpec((tm, tn), lambda i, j, k: (i, j))
    ins, in_specs = [a, b], [a_spec, b_spec]
    if add is not None:
        ins.append(add)
        in_specs.append(o_spec)
    return pl.pallas_call(
        body, grid=(M // tm, N // tn, nk), in_specs=in_specs, out_specs=o_spec,
        out_shape=jax.ShapeDtypeStruct((M, N), out_dtype), scratch_shapes=[pltpu.VMEM((tm, tn), F32)] if nk > 1 else [],
        name=name, compiler_params=_params(("parallel", "parallel", "arbitrary")),
    )(*ins)


def _rowsum(x):
    return jnp.sum(x, axis=0, keepdims=True)


def _rms_stat(x):
    return lax.rsqrt(jnp.mean(x * x, axis=-1, keepdims=True) + NORM_EPS)


def _rms_bwd(dy, x, g):
    r = _rms_stat(x)
    xn = x * r
    dxn = dy * g
    dx = r * (dxn - xn * jnp.mean(dxn * xn, axis=-1, keepdims=True))
    return dx, _rowsum(dy * xn)


def _fox_c_fwd(f8t, bias_col, tc=256):
    B, H, S = f8t.shape
    tc = min(tc, S)

    def body(f_ref, b_ref, c_ref, carry):
        @pl.when(pl.program_id(1) == 0)
        def _():
            carry[...] = jnp.zeros(carry.shape, F32)
        lf = _log_sigmoid(f_ref[...] + b_ref[...])
        row = lax.broadcasted_iota(jnp.int32, (tc, tc), 0)
        col = lax.broadcasted_iota(jnp.int32, (tc, tc), 1)
        c = _hdot(lf, (row <= col).astype(F32)) + carry[...]
        c_ref[...] = c
        carry[...] = c[:, tc - 1:tc]

    return pl.pallas_call(
        body, grid=(B, S // tc),
        in_specs=[pl.BlockSpec((None, H, tc), lambda b, i: (b, 0, i)), pl.BlockSpec((H, 1), lambda b, i: (0, 0))],
        out_specs=pl.BlockSpec((None, H, tc), lambda b, i: (b, 0, i)),
        out_shape=jax.ShapeDtypeStruct((B, H, S), F32), scratch_shapes=[pltpu.VMEM((H, 1), F32)], name="fox_c_fwd",
        compiler_params=_params(("parallel", "arbitrary")),
    )(f8t, bias_col)


def _fox_c_bwd(dc, f8t, bias_col, tc=256):
    B, H, S = f8t.shape
    tc = min(tc, S)
    n = S // tc

    def body(dc_ref, f_ref, b_ref, df_ref, db_ref, carry):
        @pl.when(pl.program_id(1) == 0)
        def _():
            carry[...] = jnp.zeros(carry.shape, F32)
            db_ref[...] = jnp.zeros(db_ref.shape, F32)
        row = lax.broadcasted_iota(jnp.int32, (tc, tc), 0)
        col = lax.broadcasted_iota(jnp.int32, (tc, tc), 1)
        dlf = _hdot(dc_ref[...], (row >= col).astype(F32)) + carry[...]
        z = f_ref[...] + b_ref[...]
        df = dlf * (1.0 - _sigmoid(z))
        df_ref[...] = df
        db_ref[...] += jnp.sum(df, axis=1, keepdims=True)
        carry[...] = dlf[:, 0:1]

    rev = lambda b, i: (b, 0, n - 1 - i)
    return pl.pallas_call(
        body, grid=(B, n),
        in_specs=[pl.BlockSpec((None, H, tc), rev), pl.BlockSpec((None, H, tc), rev), pl.BlockSpec((H, 1), lambda b, i: (0, 0))],
        out_specs=[pl.BlockSpec((None, H, tc), rev), pl.BlockSpec((None, H, 1), lambda b, i: (b, 0, 0))],
        out_shape=[jax.ShapeDtypeStruct((B, H, S), F32), jax.ShapeDtypeStruct((B, H, 1), F32)],
        scratch_shapes=[pltpu.VMEM((H, 1), F32)], name="fox_c_bwd",
        compiler_params=_params(("parallel", "arbitrary")),
    )(dc, f8t, bias_col)


NEG_BIG = -1e30


def _fox_logits(q, kj, cq, ckj, i, j, tq, tk, scale):
    s = _bdot(q, kj, NT_DIMS) * scale + (cq - ckj)
    row = lax.broadcasted_iota(jnp.int32, (tq, tk), 0)
    col = lax.broadcasted_iota(jnp.int32, (tq, tk), 1)
    return s, col <= row + (i * tq - j * tk)


def _fox_key_blocks(i, tq, tk):
    return (i * tq + tq - 1) // tk + 1


def _fox_fwd(qkv, cq, ck, B):
    T = qkv.shape[0]
    S = T // B
    (nq, tq), (nk, tk) = cq.shape[1:3], (ck.shape[1], ck.shape[3])
    scale = HEAD_DIM ** -0.5

    def body(q_ref, k_ref, v_ref, cq_ref, ck_ref, o_ref, lse_ref):
        i = pl.program_id(1)
        lo = lax.broadcasted_iota(jnp.int32, (tq, LANES), 1) < HEAD_DIM
        q = q_ref[...]
        qh = (jnp.where(lo, q, 0), jnp.where(lo, 0, q))

        def step(j, carry):
            rows = pl.ds(pl.multiple_of(j * tk, tk), tk)
            kj, vj = k_ref[rows, :], v_ref[rows, :]
            new = []
            for h in range(2):
                m, l, acc = carry[h]
                s, ok = _fox_logits(qh[h], kj, cq_ref[h], ck_ref[h, j], i, j, tq, tk, scale)
                s = jnp.where(ok, s, NEG_BIG)
                m2 = jnp.maximum(m, jnp.max(s, axis=1, keepdims=True))
                p = jnp.exp(s - m2)
                al = jnp.exp(m - m2)
                new.append((m2, al * l + jnp.sum(p, axis=1, keepdims=True), al * acc + _bdot(p, vj)))
            return tuple(new)

        init = tuple((jnp.full((tq, 1), NEG_BIG, F32), jnp.zeros((tq, 1), F32), jnp.zeros((tq, LANES), F32)) for _ in range(2))
        (m0, l0, a0), (m1, l1, a1) = lax.fori_loop(0, _fox_key_blocks(i, tq, tk), step, init)
        o_ref[...] = jnp.where(lo, a0 / l0, a1 / l1).astype(o_ref.dtype)
        lse_ref[0] = m0 + jnp.log(l0)
        lse_ref[1] = m1 + jnp.log(l1)

    seq = lambda col0: pl.BlockSpec((S, LANES), lambda g, i: (g // 4, col0 + g % 4))
    blk = lambda col0: pl.BlockSpec((tq, LANES), lambda g, i: ((g // 4) * nq + i, col0 + g % 4))
    col = pl.BlockSpec((2, None, tq, 1), lambda g, i: (g, i, 0, 0))
    return pl.pallas_call(
        body, grid=(B * 4, nq), in_specs=[blk(0), seq(4), seq(8), col, pl.BlockSpec((2, nk, 1, tk), lambda g, i: (g, 0, 0, 0))],
        out_specs=[blk(0), col],
        out_shape=[jax.ShapeDtypeStruct((T, BR_W), BF16), jax.ShapeDtypeStruct(cq.shape, F32)], name="fox_fwd",
        compiler_params=_params(("parallel", "parallel")),
    )(qkv, qkv, qkv, cq, ck)


def _fox_bwd(qkv, o, do, cq, ck, lse, B):
    T = qkv.shape[0]
    S = T // B
    (nq, tq), (nk, tk) = cq.shape[1:3], (ck.shape[1], ck.shape[3])
    scale = HEAD_DIM ** -0.5

    def body(q_ref, k_ref, v_ref, o_ref, do_ref, cq_ref, ck_ref, lse_ref, dq_ref, dk_ref, dv_ref, dck_ref, dcq_ref,
             dk_acc, dv_acc):
        dk_acc[...] = jnp.zeros(dk_acc.shape, F32)
        dv_acc[...] = jnp.zeros(dv_acc.shape, F32)
        dck_ref[...] = jnp.zeros(dck_ref.shape, F32)
        lo = lax.broadcasted_iota(jnp.int32, (tq, LANES), 1) < HEAD_DIM

        def qloop(i, _):
            qrows = pl.ds(pl.multiple_of(i * tq, tq), tq)
            q, do_i, o_i = q_ref[qrows, :], do_ref[qrows, :], o_ref[qrows, :].astype(F32)
            qh = (jnp.where(lo, q, 0), jnp.where(lo, 0, q))
            doh = (jnp.where(lo, do_i, 0), jnp.where(lo, 0, do_i))
            delta = [jnp.sum(doh[h].astype(F32) * o_i, axis=1, keepdims=True) for h in range(2)]

            def kloop(j, carry):
                krows = pl.ds(pl.multiple_of(j * tk, tk), tk)
                kj, vj = k_ref[krows, :], v_ref[krows, :]
                new = []
                for h in range(2):
                    dq, dcq = carry[h]
                    s, ok = _fox_logits(qh[h], kj, cq_ref[h, i], ck_ref[h, j], i, j, tq, tk, scale)
                    p = jnp.where(ok, jnp.exp(s - lse_ref[h, i]), 0.0)
                    ds = p * (_bdot(doh[h], vj, NT_DIMS) - delta[h])
                    dv_acc[krows, :] += _bdot(p, doh[h], TN_DIMS)
                    dk_acc[krows, :] += _bdot(ds, qh[h], TN_DIMS) * scale
                    dck_ref[h, j] += -_rowsum(ds)
                    new.append((dq + _bdot(ds, kj) * scale, dcq + jnp.sum(ds, axis=1, keepdims=True)))
                return tuple(new)

            init = tuple((jnp.zeros((tq, LANES), F32), jnp.zeros((tq, 1), F32)) for _ in range(2))
            (dq0, dcq0), (dq1, dcq1) = lax.fori_loop(0, _fox_key_blocks(i, tq, tk), kloop, init)
            dq_ref[qrows, :] = jnp.where(lo, dq0, dq1).astype(dq_ref.dtype)
            dcq_ref[0, i] = dcq0
            dcq_ref[1, i] = dcq1
            return 0

        lax.fori_loop(0, nq, qloop, 0)
        dk_ref[...] = dk_acc[...].astype(dk_ref.dtype)
        dv_ref[...] = dv_acc[...].astype(dv_ref.dtype)

    seq = lambda col0: pl.BlockSpec((S, LANES), lambda g: (g // 4, col0 + g % 4))
    col = pl.BlockSpec((2, nq, tq, 1), lambda g: (g, 0, 0, 0))
    row = pl.BlockSpec((2, nk, 1, tk), lambda g: (g, 0, 0, 0))
    out = jax.ShapeDtypeStruct((T, BR_W), BF16)
    return pl.pallas_call(
        body, grid=(B * 4,), in_specs=[seq(0), seq(4), seq(8), seq(0), seq(0), col, row, col],
        out_specs=[seq(0), seq(0), seq(0), row, col],
        out_shape=[out, out, out, jax.ShapeDtypeStruct(ck.shape, F32), jax.ShapeDtypeStruct(cq.shape, F32)],
        scratch_shapes=[pltpu.VMEM((S, LANES), F32), pltpu.VMEM((S, LANES), F32)], name="fox_bwd",
        compiler_params=_params(("parallel",)),
    )(qkv, qkv, qkv, o, do, cq, ck, lse)


def _mem_probs(qh, kh):
    s = _bdot(qh, kh, NT_DIMS) * (MEM_HEAD_DIM ** -0.5)
    e = jnp.exp(s - jnp.max(s, axis=1, keepdims=True))
    return e / jnp.sum(e, axis=1, keepdims=True)


def _mem_fwd(q, mem_kv, B, tq=512):
    T = q.shape[0]
    S, Lm = T // B, mem_kv.shape[0] // B
    tq = min(tq, S)
    n = S // tq

    def body(q_ref, k_ref, v_ref, o_ref):
        for h in range(MEM_HEADS):
            sl = slice(h * MEM_HEAD_DIM, (h + 1) * MEM_HEAD_DIM)
            p = _mem_probs(q_ref[:, sl], k_ref[:, sl])
            o_ref[:, sl] = _bdot(p, v_ref[:, sl]).astype(o_ref.dtype)

    qs = pl.BlockSpec((tq, BR_W), lambda b, i: (b * n + i, 0))
    return pl.pallas_call(
        body, grid=(B, n),
        in_specs=[qs, pl.BlockSpec((Lm, BR_W), lambda b, i: (b, 0)), pl.BlockSpec((Lm, BR_W), lambda b, i: (b, 1))],
        out_specs=qs, out_shape=jax.ShapeDtypeStruct((T, BR_W), BF16), name="mem_fwd",
        compiler_params=_params(("parallel", "parallel")),
    )(q, mem_kv, mem_kv)


def _mem_bwd(q, mem_kv, do, B, tq=512):
    T = q.shape[0]
    S, Lm = T // B, mem_kv.shape[0] // B
    tq = min(tq, S)
    n = S // tq
    scale = MEM_HEAD_DIM ** -0.5

    def body(q_ref, k_ref, v_ref, do_ref, dq_ref, dk_ref, dv_ref):
        @pl.when(pl.program_id(1) == 0)
        def _():
            dk_ref[...] = jnp.zeros(dk_ref.shape, F32)
            dv_ref[...] = jnp.zeros(dv_ref.shape, F32)
        for h in range(MEM_HEADS):
            sl = slice(h * MEM_HEAD_DIM, (h + 1) * MEM_HEAD_DIM)
            qh, kh, vh, doh = q_ref[:, sl], k_ref[:, sl], v_ref[:, sl], do_ref[:, sl]
            p = _mem_probs(qh, kh)
            dp = _bdot(doh, vh, NT_DIMS)
            ds = p * (dp - jnp.sum(p * dp, axis=1, keepdims=True))
            dq_ref[:, sl] = (_bdot(ds, kh) * scale).astype(dq_ref.dtype)
            dk_ref[:, sl] += _bdot(ds, qh, TN_DIMS) * scale
            dv_ref[:, sl] += _bdot(p, doh, TN_DIMS)

    qs = pl.BlockSpec((tq, BR_W), lambda b, i: (b * n + i, 0))
    kv = pl.BlockSpec((Lm, BR_W), lambda b, i: (b, 0))
    return pl.pallas_call(
        body, grid=(B, n),
        in_specs=[qs, kv, pl.BlockSpec((Lm, BR_W), lambda b, i: (b, 1)), qs], out_specs=[qs, kv, kv],
        out_shape=[jax.ShapeDtypeStruct((T, BR_W), BF16), jax.ShapeDtypeStruct((B * Lm, BR_W), F32),
                   jax.ShapeDtypeStruct((B * Lm, BR_W), F32)], name="mem_bwd",
        compiler_params=_params(("parallel", "arbitrary")),
    )(q, mem_kv, mem_kv, do)


def _head_ones():
    h = np.arange(BR_W) // HEAD_DIM
    return jnp.asarray((h[:, None] == h[None, :]).astype(np.float32))


def _rw_prep(p, pp, mu, w0, w1, a0, w2, g_up, k_k, k_a, bd):
    ps = p + (pp - p) * mu
    r, k, v = ps[:, 0:512], ps[:, 512:1024], ps[:, 1024:1536]
    wa, gd = ps[:, 1536:1664], ps[:, 1664:1792]
    th = jnp.tanh(wa)
    z = w0 + _bdot(th, w1)
    wl = -jnp.exp(_log_sigmoid(z) - 0.5)
    w = jnp.exp(wl)
    a = _sigmoid(a0 + _bdot(wa, w2))
    sg = _sigmoid(gd)
    g = _bdot(sg, g_up)
    kq = k * k_k
    n2 = _hdot(kq * kq, bd)
    inv = lax.rsqrt(jnp.maximum(n2, 1e-24))
    kk = kq * inv
    k2 = k * (1.0 + (a - 1.0) * k_a)
    return dict(ps=ps, r=r, k=k, v=v, wa=wa, th=th, z=z, wl=wl, w=w, a=a, sg=sg, g=g, kq=kq, n2=n2, inv=inv, kk=kk, k2=k2)


def _keycol_selector(tm, Tc):
    e = np.zeros((tm, (tm // Tc) * LANES), np.float32)
    for t in range(tm):
        c, tl = divmod(t, Tc)
        e[t, c * LANES + tl] = e[t, c * LANES + Tc + tl] = 1.0
    return jnp.asarray(e, BF16)


def _rw_prep_fwd(p, pp, consts, B, Tc, tm=256):
    assert 2 * Tc == LANES
    T = p.shape[0]
    S = T // B
    nb, cpb = S // tm, tm // Tc
    sel = _keycol_selector(tm, Tc)
    nc = len(consts)

    def body(*refs):
        t = _rw_prep(*[r[...] for r in refs[:2 + nc]])
        sel_ref = refs[2 + nc]
        rows, cols = refs[3 + nc:7 + nc], refs[7 + nc:]
        for ref, val in zip(rows, (t["r"], t["k2"], t["v"], t["g"])):
            ref[...] = val
        lo = lax.broadcasted_iota(jnp.int32, (HEAD_DIM, LANES), 1) < HEAD_DIM
        operands = (t["w"], -t["kk"], t["kk"] * t["a"], t["k2"], t["r"])
        for n, (ref, x) in enumerate(zip(cols, operands)):
            terms = _split_bf16(x) if n == 0 else (x.astype(BF16),)
            for hp in range(4):
                xt = sum(lax.dot_general(tt[:, hp * LANES:(hp + 1) * LANES], sel_ref[...], TN_DIMS,
                                         preferred_element_type=F32) for tt in terms)
                for c in range(cpb):
                    blk = xt[:, c * LANES:(c + 1) * LANES]
                    ref[hp, c] = jnp.where(lo, blk[0:HEAD_DIM], blk[HEAD_DIM:2 * HEAD_DIM]).astype(ref.dtype)

    row_spec = lambda w: pl.BlockSpec((tm, w), lambda i: (i, 0))
    col_spec = pl.BlockSpec((None, 4, cpb, HEAD_DIM, LANES), lambda i: (i // nb, 0, i % nb, 0, 0))
    col_shape = lambda dt: jax.ShapeDtypeStruct((B, 4, S // Tc, HEAD_DIM, LANES), dt)
    out = pl.pallas_call(
        body, grid=(T // tm,),
        in_specs=[row_spec(RWKV_COLS)] * 2 + [pl.BlockSpec(c.shape, lambda i: (0, 0)) for c in consts]
        + [pl.BlockSpec(sel.shape, lambda i: (0, 0))],
        out_specs=[row_spec(BR_W)] * 4 + [col_spec] * 5,
        out_shape=[jax.ShapeDtypeStruct((T, BR_W), F32)] * 4 + [col_shape(F32)] + [col_shape(BF16)] * 4,
        name="rwkv_prep_fwd", compiler_params=_params(("parallel",)),
    )(p, pp, *consts, sel)
    return out[:4], [c.reshape(B * 4, S // Tc, HEAD_DIM, LANES) for c in out[4:]]


def _rw_prep_bwd(p, pp, cots, consts):
    def fn(p, pp, dr1, dr2, dw, dk21, dk22, dv1, dv2, dav, dbv, dg, mu, w0, w1, a0, w2, g_up, k_k, k_a, bd):
        t = _rw_prep(p, pp, mu, w0, w1, a0, w2, g_up, k_k, k_a, bd)
        dr, dk2, dv = dr1 + dr2, dk21 + dk22, dv1 + dv2
        a, k, kk, kq, inv = t["a"], t["k"], t["kk"], t["kq"], t["inv"]
        dkk = dbv * a - dav
        da = dbv * kk + dk2 * k * k_a
        dk = dk2 * (1.0 + (a - 1.0) * k_a)
        d_k_a = _rowsum(dk2 * k * (a - 1.0))
        proj = _hdot(dkk * kq, bd)
        dkq = dkk * inv - jnp.where(t["n2"] > 1e-24, kq * inv * inv * inv * proj, 0.0)
        dk = dk + dkq * k_k
        d_k_k = _rowsum(dkq * k)
        dpa = da * a * (1.0 - a)
        d_a0 = _rowsum(dpa)
        dwa = _bdot(dpa, w2, NT_DIMS)
        d_w2 = _bdot(t["wa"], dpa, TN_DIMS)
        dz = dw * t["w"] * t["wl"] * (1.0 - _sigmoid(t["z"]))
        d_w0 = _rowsum(dz)
        th = t["th"]
        dwa = dwa + _bdot(dz, w1, NT_DIMS) * (1.0 - th * th)
        d_w1 = _bdot(th, dz, TN_DIMS)
        sg = t["sg"]
        dgd = _bdot(dg, g_up, NT_DIMS) * sg * (1.0 - sg)
        d_g_up = _bdot(sg, dg, TN_DIMS)
        dps = jnp.concatenate([dr, dk, dv, dwa, dgd], axis=1)
        d_mu = _rowsum(dps * (pp - p))
        return dps * (1.0 - mu), dps * mu, d_mu, d_w0, d_w1, d_a0, d_w2, d_g_up, d_k_k, d_k_a

    accs = [((1, RWKV_COLS), F32), ((1, BR_W), F32), ((LANES, BR_W), F32), ((1, BR_W), F32), ((LANES, BR_W), F32),
            ((LANES, BR_W), F32), ((1, BR_W), F32), ((1, BR_W), F32)]
    return _rowwise("rwkv_prep_bwd", fn, [p, pp] + list(cots), consts, [(RWKV_COLS, BF16)] * 2, accs, tm=128)


def _rw_head(y, r, k2, v, g, gn_g, gn_b, r_k, bd):
    mean = _hdot(y, bd) * (1.0 / HEAD_DIM)
    yc = y - mean
    rs = lax.rsqrt(_hdot(yc * yc, bd) * (1.0 / HEAD_DIM) + GN_EPS)
    yn = yc * rs
    bs = _hdot(r * k2 * r_k, bd)
    return yn, rs, bs, yn * gn_g + gn_b + bs * v


def _rw_head_fwd(y, r, k2, v, g, consts):
    def fn(y, r, k2, v, g, *c):
        return _rw_head(y, r, k2, v, g, *c)[3] * g
    return _rowwise("rwkv_head_fwd", fn, [y, r, k2, v, g], consts, [(BR_W, BF16)])[0]


def _rw_head_bwd(dout, y, r, k2, v, g, consts):
    def fn(dout, y, r, k2, v, g, gn_g, gn_b, r_k, bd):
        dout = dout.astype(F32)
        yn, rs, bs, zz = _rw_head(y, r, k2, v, g, gn_g, gn_b, r_k, bd)
        dg = dout * zz
        dz = dout * g
        dyn = dz * gn_g
        inv_n = 1.0 / HEAD_DIM
        dy = rs * (dyn - _hdot(dyn, bd) * inv_n - yn * (_hdot(dyn * yn, bd) * inv_n))
        dq = _hdot(dz * v, bd)
        return dy, dg, dq * k2 * r_k, dq * r * r_k, dz * bs, _rowsum(dz * yn), _rowsum(dz), _rowsum(dq * r * k2)
    return _rowwise("rwkv_head_bwd", fn, [dout, y, r, k2, v, g], consts, [(BR_W, F32)] * 5, [((1, BR_W), F32)] * 3)


SCAN_TC = 64


def _scan_onehot(Tc):
    w = np.zeros((Tc // 2, 2 * Tc, 2 * LANES), np.float32)
    for tt in range(Tc // 2):
        for u in range(2):
            for h in range(2):
                w[tt, h * Tc + 2 * tt + u, u * LANES + h * HEAD_DIM: u * LANES + (h + 1) * HEAD_DIM] = 1.0
    return jnp.asarray(w, BF16)


def _split_bf16(x):
    hi = x.astype(BF16)
    return hi, (x - hi.astype(F32)).astype(BF16)


def _key_tiles(l_w, others, onehot):
    dot = lambda x: jnp.dot(x, onehot, preferred_element_type=F32)
    whi, wmid = l_w
    return [dot(whi) + dot(wmid)] + [dot(o) for o in others]


def _rw_scan_fwd(LW, LA, LB, LK, LR, v, P=4):
    NP, nc, _, Tc2 = LW.shape
    Tc = Tc2 // 2
    S = nc * Tc
    onehot = _scan_onehot(Tc)
    npb = 4 // P

    def body(lw, la, lb, lk, lr, v_ref, oh_ref, y_ref, sa_ref, sb_ref, st):
        @pl.when(pl.program_id(1) == 0)
        def _():
            st[...] = jnp.zeros(st.shape, F32)
        s = [st[p] for p in range(P)]
        cols = [(_split_bf16(lw[p]), [ref[p].astype(BF16) for ref in (la, lb, lk)]) for p in range(P)]
        r_cols = [lr[p].astype(F32) for p in range(P)]
        head0 = lax.broadcasted_iota(jnp.int32, (HEAD_DIM, LANES), 1) < HEAD_DIM
        for tt in range(Tc // 2):
            tiles = [_key_tiles(c[0], c[1], oh_ref[tt]) for c in cols]
            for u in range(2):
                t = 2 * tt + u
                for p in range(P):
                    W, A, Bt, Kt = (x[:, u * LANES:(u + 1) * LANES] for x in tiles[p])
                    R = jnp.where(head0, r_cols[p][:, t:t + 1], r_cols[p][:, Tc + t:Tc + t + 1])
                    ls = slice(p * LANES, (p + 1) * LANES)
                    sb_ref[p, t] = s[p]
                    sa = _rowsum(s[p] * A)
                    s[p] = s[p] * W + Bt * sa + Kt * v_ref[t:t + 1, ls]
                    y_ref[t:t + 1, ls] = _rowsum(s[p] * R)
                    sa_ref[t:t + 1, ls] = sa
        for p in range(P):
            st[p] = s[p]

    lspec = pl.BlockSpec((P, None, HEAD_DIM, Tc2), lambda g, c: (g, c, 0, 0))
    rows = pl.BlockSpec((Tc, P * LANES), lambda g, c: ((g // npb) * nc + c, g % npb))
    rowshape = jax.ShapeDtypeStruct(v.shape, F32)
    return pl.pallas_call(
        body, grid=(NP // P, nc), in_specs=[lspec] * 5 + [rows, pl.BlockSpec(onehot.shape, lambda g, c: (0, 0, 0))],
        out_specs=[rows, rows, pl.BlockSpec((P, Tc, HEAD_DIM, LANES), lambda g, c: (g, c, 0, 0))],
        out_shape=[rowshape, rowshape, jax.ShapeDtypeStruct((NP, S, HEAD_DIM, LANES), F32)],
        scratch_shapes=[pltpu.VMEM((P, HEAD_DIM, LANES), F32)], name="rwkv_scan_fwd",
        compiler_params=_params(("parallel", "arbitrary")),
    )(LW, LA, LB, LK, LR, v, onehot)


SCAN_G_ROWS = 16


def _rw_scan_bwd(LW, LA, LB, LK, LR, v, sa, dy, sb, P=4):
    NP, nc, _, Tc2 = LW.shape
    Tc = Tc2 // 2
    onehot = _scan_onehot(Tc)
    npb = 4 // P

    def body(lw, la, lb, lk, lr, v_ref, sa_ref, dy_ref, sb_ref, oh_ref, dv_ref, dk_ref, db_ref, dw_ref, dr_ref, da_ref, dst):
        @pl.when(pl.program_id(1) == 0)
        def _():
            dst[...] = jnp.zeros(dst.shape, F32)
        rid = lax.broadcasted_iota(jnp.int32, (SCAN_G_ROWS, LANES), 0)
        lane = lax.broadcasted_iota(jnp.int32, (SCAN_G_ROWS, LANES), 1)
        own = (((rid % 2) == 0) == (lane < HEAD_DIM)) & (rid < 10)
        lo = lane[0:1] < HEAD_DIM
        nt = lambda rows, tile: lax.dot_general(rows.astype(BF16), tile.astype(BF16), NT_DIMS, preferred_element_type=F32)
        ds = [dst[p] for p in range(P)]
        cols = [(_split_bf16(lw[p]), [ref[p].astype(BF16) for ref in (la, lb, lk)]) for p in range(P)]
        r_cols = [lr[p].astype(F32) for p in range(P)]
        head0 = lax.broadcasted_iota(jnp.int32, (HEAD_DIM, LANES), 1) < HEAD_DIM
        for tt in reversed(range(Tc // 2)):
            tiles = [_key_tiles(c[0], c[1], oh_ref[tt]) for c in cols]
            pending = [[] for _ in range(P)]
            for u in (1, 0):
                t = 2 * tt + u
                for p in range(P):
                    W, A, Bt, Kt = (x[:, u * LANES:(u + 1) * LANES] for x in tiles[p])
                    R = jnp.where(head0, r_cols[p][:, t:t + 1], r_cols[p][:, Tc + t:Tc + t + 1])
                    ls = slice(p * LANES, (p + 1) * LANES)
                    vr, sar, dyr = (ref[t:t + 1, ls] for ref in (v_ref, sa_ref, dy_ref))
                    sp = sb_ref[p, t]
                    s_t = sp * W + Bt * sar + Kt * vr
                    d = ds[p] + R * dyr
                    dv_ref[t:t + 1, ls] = _rowsum(d * Kt)
                    dsar = _rowsum(d * Bt)
                    rows = jnp.where(rid < 2, vr, jnp.where(rid < 4, sar, jnp.where(rid < 6, 1.0, jnp.where(rid < 8, dyr, dsar))))
                    pending[p].append((t, jnp.where(own, rows, 0.0), [d, d * sp, s_t, sp]))
                    ds[p] = d * W + A * dsar
            for p in range(P):
                ls = slice(p * LANES, (p + 1) * LANES)
                (t1, rows1, tiles1), (t0, rows0, tiles0) = pending[p]
                g2 = nt(jnp.concatenate([rows1, rows0], axis=0), jnp.concatenate(tiles1 + tiles0, axis=0))
                for t, g in ((t1, g2[0:SCAN_G_ROWS, 0:2 * LANES]), (t0, g2[SCAN_G_ROWS:, 2 * LANES:])):
                    ga, gb = g[:, 0:LANES], g[:, LANES:2 * LANES]
                    ra, rb = pltpu.roll(ga, HEAD_DIM, 1), pltpu.roll(gb, HEAD_DIM, 1)
                    dk_ref[t:t + 1, ls] = jnp.where(lo, ga[0:1], ra[1:2])
                    db_ref[t:t + 1, ls] = jnp.where(lo, ga[2:3], ra[3:4])
                    dw_ref[t:t + 1, ls] = jnp.where(lo, ra[4:5], ga[5:6])
                    dr_ref[t:t + 1, ls] = jnp.where(lo, gb[6:7], rb[7:8])
                    da_ref[t:t + 1, ls] = jnp.where(lo, rb[8:9], gb[9:10])
        for p in range(P):
            dst[p] = ds[p]

    rev = lambda g, c: (g, nc - 1 - c, 0, 0)
    lspec = pl.BlockSpec((P, None, HEAD_DIM, Tc2), rev)
    rows = pl.BlockSpec((Tc, P * LANES), lambda g, c: ((g // npb) * nc + nc - 1 - c, g % npb))
    return pl.pallas_call(
        body, grid=(NP // P, nc),
        in_specs=[lspec] * 5 + [rows] * 3 + [pl.BlockSpec((P, Tc, HEAD_DIM, LANES), rev),
                                             pl.BlockSpec(onehot.shape, lambda g, c: (0, 0, 0))],
        out_specs=[rows] * 6, out_shape=[jax.ShapeDtypeStruct(v.shape, F32)] * 6,
        scratch_shapes=[pltpu.VMEM((P, HEAD_DIM, LANES), F32)], name="rwkv_scan_bwd",
        compiler_params=_params(("parallel", "arbitrary")),
    )(LW, LA, LB, LK, LR, v, sa, dy, sb, onehot)


def _shift_prev(p, B):
    T, W = p.shape
    return jnp.pad(p.reshape(B, T // B, W), ((0, 0), (1, 0), (0, 0)))[:, :-1].reshape(T, W)


def _shift_next(p, B):
    T, W = p.shape
    return jnp.pad(p.reshape(B, T // B, W), ((0, 0), (0, 1), (0, 0)))[:, 1:].reshape(T, W)


FOX_FWD_BLOCKS = (512, 1024)
FOX_BWD_BLOCKS = (512, 512)


def _layer_step(x, mem, target, W, late_weights=None, early_grads=None, scan_tc=SCAN_TC, fox_fwd_t=FOX_FWD_BLOCKS,
                fox_bwd_t=FOX_BWD_BLOCKS):
    B, S, _ = x.shape
    T = B * S
    x2, tgt2 = x.reshape(T, D_MODEL), target.reshape(T, D_MODEL)
    mem2 = mem.reshape(-1, D_MODEL)
    w_in_t = W["w_in"]
    wt_qkv, wt_rw, wt_mq, wt_gate = (w_in_t[lo:hi] for lo, hi in (COL_QKV, COL_RW, COL_MQ, COL_GATE))
    wt_f = jnp.pad(w_in_t[COL_F[0]:COL_F[1]], ((0, LANES - N_HEADS), (0, 0)))
    row = lambda v: v.reshape(1, -1).astype(F32)
    pre1_g, post1_g, pre2_g, post2_g, mem_g = (row(W[n]) for n in ("pre1_g", "post1_g", "pre2_g", "post2_g", "mem_norm_g"))

    u = _rowwise("rms_pre1", lambda x, g: x * _rms_stat(x) * g, [x2], [pre1_g], [(D_MODEL, BF16)])[0]
    qkv = _mm("proj_qkv", u, wt_qkv, tb=True, out_dtype=BF16)
    f_pad = _mm("proj_f", u, wt_f, tb=True)
    p_rw = _mm("proj_rwkv", u, wt_rw, tb=True)
    memq = _mm("proj_memq", u, wt_mq, tb=True, out_dtype=BF16)
    gate = _mm("proj_gate", u, wt_gate, tb=True, out_dtype=BF16)

    bias_col = W["fox_f_bias"].reshape(N_HEADS, 1).astype(F32)
    f8t = f_pad[:, :N_HEADS].reshape(B, S, N_HEADS).transpose(0, 2, 1)
    c = _fox_c_fwd(f8t, bias_col)
    G = B * N_HEADS
    q_blocks = lambda a, t: a.reshape(G, S // min(t, S), min(t, S), 1)
    k_blocks = lambda a, t: a.reshape(G, S // min(t, S), 1, min(t, S))
    c_col = c.reshape(G, S, 1)
    fox_out, lse = _fox_fwd(qkv, q_blocks(c_col, fox_fwd_t[0]), k_blocks(c, fox_fwd_t[1]), B)

    bd = _head_ones()
    zpad = jnp.zeros((64, BR_W), F32)
    w1 = jnp.concatenate([W["rwkv_w_up"].astype(F32), zpad], axis=0)
    w2 = jnp.concatenate([zpad, W["rwkv_a_up"].astype(F32)], axis=0)
    prep_consts = [row(W["rwkv_mu"]), row(W["rwkv_w0"]), w1, row(W["rwkv_a0"]), w2, W["rwkv_g_up"].astype(F32),
                   row(W["rwkv_k_k"]), row(W["rwkv_k_a"]), bd]
    p_prev = _shift_prev(p_rw, B)
    (rr, rk2, rv, rg), scan_cols = _rw_prep_fwd(p_rw, p_prev, prep_consts, B, scan_tc)
    ry, rsa, sb = _rw_scan_fwd(*scan_cols, rv)
    head_consts = [row(W["rwkv_gn_g"]), row(W["rwkv_gn_b"]), row(W["rwkv_r_k"]), bd]
    rwkv_out = _rw_head_fwd(ry, rr, rk2, rv, rg, head_consts)

    if late_weights is not None:
        W = {**W, **late_weights(rwkv_out)}

    mn = _rowwise("rms_mem", lambda m, g: m * _rms_stat(m) * g, [mem2], [mem_g], [(D_MODEL, BF16)])[0]
    mem_kv = _mm("proj_memkv", mn, W["w_mem_kv"], out_dtype=BF16)
    mem_out = _mem_fwd(memq, mem_kv, B)

    fo = [_mm("branch_" + n, a, W[n], tb=True, out_dtype=BF16)
          for n, a in (("w_fox_out", fox_out), ("w_rwkv_out", rwkv_out), ("w_mem_out", mem_out))]

    def merge(gate, f0, f1, f2):
        return sum(_sigmoid(gate[:, i * D_MODEL:(i + 1) * D_MODEL]) * f for i, f in enumerate((f0, f1, f2)))
    merged = _rowwise("merge", merge, [gate] + fo, [], [(D_MODEL, BF16)])[0]
    y1 = _mm("proj_o", merged, W["w_o"])

    def mid(x, y1, g1, g2):
        h1 = x + y1 * _rms_stat(y1) * g1
        return h1, h1 * _rms_stat(h1) * g2
    h1, u2 = _rowwise("norm_mid", mid, [x2, y1], [post1_g, pre2_g], [(D_MODEL, F32), (D_MODEL, BF16)])
    gt = _mm("ffn_gate", u2, W["w_ffn_gate"], tb=True, out_dtype=BF16)
    up = _mm("ffn_up", u2, W["w_ffn_up"], tb=True, out_dtype=BF16)
    act = _rowwise("swiglu", lambda gt, up: gt * _sigmoid(gt) * up, [gt, up], [], [(D_FF, BF16)])[0]
    ffn = _mm("ffn_down", act, W["w_ffn_down"])

    def tail(h1, ffn, tgt, g):
        err = h1 + ffn * _rms_stat(ffn) * g - tgt
        dh2 = err * (1.0 / D_MODEL)
        dffn, dg = _rms_bwd(dh2, ffn, g)
        loss = 0.5 * jnp.sum(jnp.sum(err * err, axis=1, keepdims=True) * (1.0 / D_MODEL), axis=0, keepdims=True)
        return dh2, dffn, dg, jnp.broadcast_to(loss, (1, LANES))
    dh2, dffn, d_post2, loss = _rowwise("loss_tail", tail, [h1, ffn, tgt2], [post2_g], [(D_MODEL, F32), (D_MODEL, BF16)],
                                        [((1, D_MODEL), F32), ((1, LANES), F32)])
    gw = {"post2_g": d_post2}
    dact = _mm("d_act", dffn, W["w_ffn_down"], tb=True, out_dtype=BF16)
    gw["w_ffn_down"] = _mm("g_ffn_down", act, dffn, ta=True, out_dtype=BF16)

    def swiglu_bwd(dact, gt, up):
        s = _sigmoid(gt)
        return dact * up * s * (1.0 + gt * (1.0 - s)), dact * gt * s
    dgt, dup = _rowwise("swiglu_bwd", swiglu_bwd, [dact, gt, up], [], [(D_FF, BF16)] * 2)
    du2 = _mm("d_u2_gate", dgt, W["w_ffn_gate"])
    du2 = _mm("d_u2_up", dup, W["w_ffn_up"], add=du2)
    gw["w_ffn_gate"] = _mm("g_ffn_gate", dgt, u2, ta=True, out_dtype=BF16)
    gw["w_ffn_up"] = _mm("g_ffn_up", dup, u2, ta=True, out_dtype=BF16)

    def mid_bwd(du2, dh2, h1, y1, g1, g2):
        dh1_n, d_pre2 = _rms_bwd(du2, h1, g2)
        dh1 = dh2 + dh1_n
        dy1, d_post1 = _rms_bwd(dh1, y1, g1)
        return dh1, dy1, d_post1, d_pre2
    dh1, dy1, gw["post1_g"], gw["pre2_g"] = _rowwise(
        "norm_mid_bwd", mid_bwd, [du2, dh2, h1, y1], [post1_g, pre2_g], [(D_MODEL, F32), (D_MODEL, BF16)],
        [((1, D_MODEL), F32)] * 2)
    dmerged = _mm("d_merged", dy1, W["w_o"], tb=True, out_dtype=BF16)
    gw["w_o"] = _mm("g_w_o", merged, dy1, ta=True, out_dtype=BF16)

    def merge_bwd(dm, gate, f0, f1, f2):
        s = [_sigmoid(gate[:, i * D_MODEL:(i + 1) * D_MODEL]) for i in range(3)]
        dgate = jnp.concatenate([dm * f * si * (1.0 - si) for f, si in zip((f0, f1, f2), s)], axis=1)
        return dm * s[0], dm * s[1], dm * s[2], dgate
    dfo0, dfo1, dfo2, dgate = _rowwise("merge_bwd", merge_bwd, [dmerged, gate] + fo, [],
                                       [(D_MODEL, BF16)] * 3 + [(3 * D_MODEL, BF16)])
    d_branch = {}
    for n, a, dfo in (("w_fox_out", fox_out, dfo0), ("w_rwkv_out", rwkv_out, dfo1), ("w_mem_out", mem_out, dfo2)):
        d_branch[n] = _mm("d_in_" + n, dfo, W[n], out_dtype=BF16)
        gw[n] = _mm("g_" + n, dfo, a, ta=True, out_dtype=BF16)

    dmemq, dkm, dvm = _mem_bwd(memq, mem_kv, d_branch["w_mem_out"], B)
    dmem_kv = jnp.concatenate([dkm, dvm], axis=1)
    gw["w_mem_kv"] = _mm("g_w_mem_kv", mn, dmem_kv, ta=True, out_dtype=BF16)
    dmn = _mm("d_mn", dmem_kv, W["w_mem_kv"], tb=True)
    gw["mem_norm_g"] = _rowwise("rms_mem_bwd", lambda d, m, g: _rms_bwd(d, m, g)[1], [dmn, mem2], [mem_g], [],
                                [((1, D_MODEL), F32)])[0]
    if early_grads is not None:
        d_branch["w_fox_out"] = early_grads(gw, d_branch["w_fox_out"])

    dfq, dfk, dfv, dck, dcq = _fox_bwd(qkv, fox_out, d_branch["w_fox_out"], q_blocks(c_col, fox_bwd_t[0]),
                                       k_blocks(c, fox_bwd_t[1]), q_blocks(lse, fox_bwd_t[0]), B)
    df8t, dbias = _fox_c_bwd(dck.reshape(B, N_HEADS, S) + dcq.reshape(B, N_HEADS, S), f8t, bias_col)
    gw["fox_f_bias"] = jnp.sum(dbias, axis=0).reshape(1, N_HEADS)
    dqkv = jnp.concatenate([dfq, dfk, dfv], axis=1)
    df_pad = jnp.pad(df8t.transpose(0, 2, 1).reshape(T, N_HEADS), ((0, 0), (0, LANES - N_HEADS))).astype(BF16)

    dry, drg, dr_h, dk2_h, dv_h, gw["rwkv_gn_g"], gw["rwkv_gn_b"], gw["rwkv_r_k"] = _rw_head_bwd(
        d_branch["w_rwkv_out"], ry, rr, rk2, rv, rg, head_consts)
    dv_s, dk2_s, db_s, dw_s, dr_s, da_s = _rw_scan_bwd(*scan_cols, rv, rsa, dry, sb)
    dP, dPp, gw["rwkv_mu"], gw["rwkv_w0"], d_w1, gw["rwkv_a0"], d_w2, gw["rwkv_g_up"], gw["rwkv_k_k"], gw["rwkv_k_a"] = \
        _rw_prep_bwd(p_rw, p_prev, [dr_s, dr_h, dw_s, dk2_s, dk2_h, dv_s, dv_h, da_s, db_s, drg], prep_consts)
    gw["rwkv_w_up"], gw["rwkv_a_up"] = d_w1[:64], d_w2[64:]
    dp_rw = dP + _shift_next(dPp, B)

    du = _mm("d_u_qkv", dqkv, wt_qkv)
    du = _mm("d_u_f", df_pad, wt_f, add=du)
    du = _mm("d_u_rwkv", dp_rw, wt_rw, add=du)
    du = _mm("d_u_memq", dmemq, wt_mq, add=du)
    du = _mm("d_u_gate", dgate, wt_gate, add=du)
    gw["w_in"] = jnp.concatenate(
        [_mm("g_w_qkv", dqkv, u, ta=True, out_dtype=BF16), _mm("g_w_f", df_pad, u, ta=True, out_dtype=BF16)[:N_HEADS],
         _mm("g_w_rwkv", dp_rw, u, ta=True, out_dtype=BF16), _mm("g_w_memq", dmemq, u, ta=True, out_dtype=BF16),
         _mm("g_w_gate", dgate, u, ta=True, out_dtype=BF16)], axis=0)

    def pre1_bwd(du, dh1, x, g):
        dx, dg = _rms_bwd(du, x, g)
        return dh1 + dx, dg
    dx, gw["pre1_g"] = _rowwise("rms_pre1_bwd", pre1_bwd, [du, dh1, x2], [pre1_g], [(D_MODEL, F32)], [((1, D_MODEL), F32)])
    return loss[0, 0], dx.reshape(B, S, D_MODEL), gw


TRANSPOSED = ("w_in", "w_ffn_gate", "w_ffn_up", "w_fox_out", "w_rwkv_out", "w_mem_out")
LORA = ("rwkv_w_up", "rwkv_a_up", "rwkv_g_up")
ROW_SHARDED = ("w_mem_kv", "w_o", "w_ffn_down")
FIRST = ("w_in",) + LORA
LATER = ("w_ffn_gate", "w_ffn_up", "w_mem_kv", "w_o", "w_ffn_down", "w_fox_out", "w_rwkv_out", "w_mem_out")
BIG = FIRST + LATER
SMALL = ("pre1_g", "post1_g", "pre2_g", "post2_g", "mem_norm_g", "fox_f_bias", "rwkv_mu", "rwkv_w0", "rwkv_a0", "rwkv_k_k",
         "rwkv_k_a", "rwkv_r_k", "rwkv_gn_g", "rwkv_gn_b")
WEIGHTS = ("pre1_g", "post1_g", "pre2_g", "post2_g", "mem_norm_g", "w_in", "fox_f_bias", "rwkv_mu", "rwkv_w0", "rwkv_w_up",
           "rwkv_a0", "rwkv_a_up", "rwkv_g_up", "rwkv_k_k", "rwkv_k_a", "rwkv_r_k", "rwkv_gn_g", "rwkv_gn_b", "w_mem_kv",
           "w_fox_out", "w_rwkv_out", "w_mem_out", "w_o", "w_ffn_gate", "w_ffn_up", "w_ffn_down")
WIRE_W = 1024
WIRE_ROW_ALIGN = 16
WIRE_HALF_ALIGN = 128


def _wire_rows(name, shard_shape):
    r, c = shard_shape
    if name in ROW_SHARDED:
        return r
    return -(-c // WIRE_ROW_ALIGN) * WIRE_ROW_ALIGN if r == WIRE_W else (r * c) // WIRE_W


def _to_wire(name, a):
    if name not in ROW_SHARDED:
        a = jnp.swapaxes(a, -1, -2)
    lead, (n, w) = a.shape[:-2], a.shape[-2:]
    if w != WIRE_W:
        return a.reshape(lead + ((n * w) // WIRE_W, WIRE_W))
    return jnp.pad(a, [(0, 0)] * len(lead) + [(0, (-n) % WIRE_ROW_ALIGN), (0, 0)])


def _from_wire(name, a, shard_shape):
    r, c = shard_shape
    if name in ROW_SHARDED:
        return a
    return a[..., :c, :] if r == WIRE_W else a.reshape(a.shape[:-2] + (c, r))


def _wire_layout(shard_shapes, names):
    layout, off = {}, 0
    for n in names:
        rows = _wire_rows(n, shard_shapes[n])
        layout[n] = (off, rows)
        off += rows
    return layout, -(-off // (2 * WIRE_HALF_ALIGN)) * WIRE_HALF_ALIGN


def _wire_pack(blocks, half_rows):
    a = jnp.concatenate(blocks, axis=-2)
    lead = a.shape[:-2]
    a = jnp.pad(a, [(0, 0)] * len(lead) + [(0, 2 * half_rows - a.shape[-2]), (0, 0)])
    return a.reshape(lead + (2, half_rows, WIRE_W))


def _my_place():
    return lax.axis_index("x"), lax.axis_index("y"), lax.axis_index("c")


def _other_chips(x, y):
    return [(1 - x, y), (x, 1 - y), (1 - x, 1 - y)]


ANY = pl.BlockSpec(memory_space=pl.ANY)


def _gather_weights(packed):
    _, R, L = packed.shape
    me = 2 * lax.axis_index("x") + lax.axis_index("y")
    base = lax.dynamic_update_index_in_dim(jnp.zeros((N_CHIPS, 2, R, L), packed.dtype), packed, me, 0)

    def body(in_ref, base_ref, out_ref, send_sems, recv_sems):
        x, y, c = _my_place()
        chip = lambda px, py: 2 * px + py
        sibling = (x, y, 1 - c)
        others = _other_chips(x, y)

        def copy(k, src, dst, to):
            return pltpu.make_async_remote_copy(src_ref=src, dst_ref=dst, send_sem=send_sems.at[k], recv_sem=recv_sems.at[k],
                                                device_id=to, device_id_type=MESH)

        sends = [copy(j, in_ref.at[c], out_ref.at[chip(x, y), c], (px, py, c)) for j, (px, py) in enumerate(others)]
        for cp in sends:
            cp.start()
        passed = [copy(3 + j, out_ref.at[chip(px, py), c], out_ref.at[chip(px, py), c], sibling)
                  for j, (px, py) in enumerate(others)]
        for j, (px, py) in enumerate(others):
            copy(j, in_ref.at[c], out_ref.at[chip(px, py), c], (px, py, c)).wait_recv()
            passed[j].start()
        for j, (px, py) in enumerate(others):
            copy(3 + j, in_ref.at[1 - c], out_ref.at[chip(px, py), 1 - c], sibling).wait_recv()
        for cp in sends + passed:
            cp.wait_send()

    return pl.pallas_call(
        body, out_shape=jax.ShapeDtypeStruct(base.shape, base.dtype), in_specs=[ANY, ANY], out_specs=ANY,
        input_output_aliases={1: 0}, scratch_shapes=[pltpu.SemaphoreType.DMA((6,)), pltpu.SemaphoreType.DMA((6,))],
        name="gather_weights",
    )(packed, base)


HBM_SPEC = pl.BlockSpec(memory_space=pltpu.HBM)
SEM_SPEC = pl.BlockSpec(memory_space=pltpu.SEMAPHORE)
DATAFLOW = pltpu.SideEffectType.DATAFLOW_SIDE_EFFECTING


def _in_hbm(a):
    return pltpu.with_memory_space_constraint(a, pltpu.HBM)


def _split_start(name, n_copies, src, land, start_copies):
    def body(src_ref, land_ref, send_sems, recv_sems, src_thru, land_thru, token):
        start_copies(src_ref, land_ref, send_sems, recv_sems)
        token[...] = jnp.zeros(token.shape, token.dtype)

    return pl.pallas_call(
        body, name=name,
        out_shape=(pltpu.SemaphoreType.DMA((n_copies,)), pltpu.SemaphoreType.DMA((n_copies,)), pltpu.HBM(src.shape, src.dtype),
                   pltpu.HBM(land.shape, land.dtype), jax.ShapeDtypeStruct((8, LANES), F32)),
        in_specs=(HBM_SPEC, HBM_SPEC),
        out_specs=(SEM_SPEC, SEM_SPEC, HBM_SPEC, HBM_SPEC, pl.BlockSpec(memory_space=pltpu.VMEM)),
        input_output_aliases={0: 2, 1: 3}, compiler_params=pltpu.CompilerParams(has_side_effects=DATAFLOW),
    )(_in_hbm(src), _in_hbm(land))


def _split_wait(name, handle, after, wait_copies):
    send_sems, recv_sems, src, land = handle[:4]

    def body(src_ref, land_ref, send_sems, recv_sems, after_ref, src_dead, land_out):
        wait_copies(src_ref, land_ref, send_sems, recv_sems)

    return pl.pallas_call(
        body, name=name, out_shape=(pltpu.HBM(src.shape, src.dtype), pltpu.HBM(land.shape, land.dtype)),
        in_specs=(HBM_SPEC, HBM_SPEC, SEM_SPEC, SEM_SPEC, ANY), out_specs=(HBM_SPEC, HBM_SPEC),
        input_output_aliases={0: 0, 1: 1}, compiler_params=pltpu.CompilerParams(has_side_effects=DATAFLOW),
    )(src, land, send_sems, recv_sems, after)[1]


def _late_gather_copies(src_ref, land_ref, send_sems, recv_sems):
    x, y, c = _my_place()
    me = 2 * x + y
    started, awaited = [], []
    for j, (px, py) in enumerate(_other_chips(x, y)):
        for core in range(2):
            started.append(pltpu.make_async_remote_copy(
                src_ref=src_ref.at[c], dst_ref=land_ref.at[me, c], send_sem=send_sems.at[2 * j + core],
                recv_sem=recv_sems.at[2 * j + c], device_id=(px, py, core), device_id_type=MESH))
            awaited.append(pltpu.make_async_remote_copy(
                src_ref=src_ref.at[core], dst_ref=land_ref.at[2 * px + py, core], send_sem=send_sems.at[2 * j + core],
                recv_sem=recv_sems.at[2 * j + core], device_id=(px, py, core), device_id_type=MESH))
    return started, awaited


def _gather_late_start(packed):
    _, R, L = packed.shape
    me = 2 * lax.axis_index("x") + lax.axis_index("y")
    land = lax.dynamic_update_index_in_dim(jnp.zeros((N_CHIPS, 2, R, L), packed.dtype), packed, me, 0)

    def start(src_ref, land_ref, send_sems, recv_sems):
        for cp in _late_gather_copies(src_ref, land_ref, send_sems, recv_sems)[0]:
            cp.start()

    return _split_start("gather_late_start", 6, packed, land, start)


def _gather_late_wait(handle, after):
    def wait(src_ref, land_ref, send_sems, recv_sems):
        started, awaited = _late_gather_copies(src_ref, land_ref, send_sems, recv_sems)
        for cp in started:
            cp.wait_send()
        for cp in awaited:
            cp.wait_recv()

    return _split_wait("gather_late_wait", handle, after, wait)


def _early_scatter_copies(src_ref, land_ref, send_sems, recv_sems):
    x, y, c = _my_place()
    me = 4 * x + 2 * y + c
    copies = []
    for k in range(1, N_DEV):
        px, py, pc = x ^ (k >> 2), y ^ ((k >> 1) & 1), c ^ (k & 1)
        copies.append(pltpu.make_async_remote_copy(
            src_ref=src_ref.at[2 * px + py, pc], dst_ref=land_ref.at[me], send_sem=send_sems.at[k - 1],
            recv_sem=recv_sems.at[k - 1], device_id=(px, py, pc), device_id_type=MESH))
    return copies


def _scatter_early_start(parts):
    n, _, R, L = parts.shape
    x, y, c = _my_place()
    own = lax.dynamic_index_in_dim(lax.dynamic_index_in_dim(parts, 2 * x + y, 0, keepdims=False), c, 0, keepdims=False)
    land = lax.dynamic_update_index_in_dim(jnp.zeros((N_DEV, R, L), parts.dtype), own, 4 * x + 2 * y + c, 0)

    def start(src_ref, land_ref, send_sems, recv_sems):
        for cp in _early_scatter_copies(src_ref, land_ref, send_sems, recv_sems):
            cp.start()

    return _split_start("scatter_early_start", N_DEV - 1, parts, land, start)


def _scatter_early_wait(handle, after):
    def wait(src_ref, land_ref, send_sems, recv_sems):
        for cp in _early_scatter_copies(src_ref, land_ref, send_sems, recv_sems):
            cp.wait_send()
            cp.wait_recv()

    return _split_wait("scatter_early_wait", handle, after, wait)


def _sum_slots(parts):
    n, R, L = parts.shape
    tr = _tile_rows(R, WIRE_HALF_ALIGN)

    def body(p_ref, o_ref):
        acc = p_ref[0].astype(F32)
        for i in range(1, n):
            acc = acc + p_ref[i].astype(F32)
        o_ref[...] = acc

    return pl.pallas_call(
        body, grid=(R // tr,), in_specs=[pl.BlockSpec((n, tr, L), lambda i: (0, i, 0))],
        out_specs=pl.BlockSpec((tr, L), lambda i: (i, 0)), out_shape=jax.ShapeDtypeStruct((R, L), F32), name="sum_slots",
        compiler_params=_params(("parallel",)),
    )(parts)


def _pair_exchange(parts):
    n, _, R, L = parts.shape

    def body(in_ref, out_ref, send_sems, recv_sems):
        x, y, c = _my_place()
        copies = [pltpu.make_async_remote_copy(src_ref=in_ref.at[s, 1 - c], dst_ref=out_ref.at[s], send_sem=send_sems.at[s],
                                               recv_sem=recv_sems.at[s], device_id=(x, y, 1 - c), device_id_type=MESH)
                  for s in range(n)]
        for cp in copies:
            cp.start()
        for cp in copies:
            cp.wait()

    return pl.pallas_call(
        body, out_shape=jax.ShapeDtypeStruct((n, R, L), parts.dtype), in_specs=[ANY], out_specs=ANY,
        scratch_shapes=[pltpu.SemaphoreType.DMA((n,)), pltpu.SemaphoreType.DMA((n,))], name="pair_exchange",
    )(parts)


def _pair_add(a, b):
    n, R, L = a.shape
    tr = _tile_rows(R, WIRE_HALF_ALIGN)

    def body(a_ref, b_ref, o_ref):
        o_ref[...] = (a_ref[...].astype(F32) + b_ref[...].astype(F32)).astype(o_ref.dtype)

    spec = pl.BlockSpec((n, tr, L), lambda i: (0, i, 0))
    return pl.pallas_call(
        body, grid=(R // tr,), in_specs=[spec, spec], out_specs=spec, out_shape=jax.ShapeDtypeStruct(a.shape, a.dtype),
        name="pair_add", compiler_params=_params(("parallel",)),
    )(a, b)


def _scatter_grads(parts):
    n, R, L = parts.shape

    def body(in_ref, out_ref, send_sems, recv_sems):
        x, y, c = _my_place()
        copies = [pltpu.make_async_remote_copy(src_ref=in_ref.at[2 * px + py], dst_ref=out_ref.at[j], send_sem=send_sems.at[j],
                                               recv_sem=recv_sems.at[j], device_id=(px, py, c), device_id_type=MESH)
                  for j, (px, py) in enumerate(_other_chips(x, y))]
        for cp in copies:
            cp.start()
        for cp in copies:
            cp.wait()

    return pl.pallas_call(
        body, out_shape=jax.ShapeDtypeStruct((n - 1, R, L), parts.dtype), in_specs=[ANY], out_specs=ANY,
        scratch_shapes=[pltpu.SemaphoreType.DMA((3,)), pltpu.SemaphoreType.DMA((3,))], name="scatter_grads",
    )(parts)


def _sum_chips(own, others):
    n, R, L = others.shape
    tr = _tile_rows(R, WIRE_HALF_ALIGN)

    def body(a_ref, p_ref, o_ref):
        acc = a_ref[...].astype(F32)
        for i in range(n):
            acc = acc + p_ref[i].astype(F32)
        o_ref[...] = acc

    return pl.pallas_call(
        body, grid=(R // tr,), in_specs=[pl.BlockSpec((tr, L), lambda i: (i, 0)), pl.BlockSpec((n, tr, L), lambda i: (0, i, 0))],
        out_specs=pl.BlockSpec((tr, L), lambda i: (i, 0)), out_shape=jax.ShapeDtypeStruct((R, L), F32), name="sum_grads",
        compiler_params=_params(("parallel",)),
    )(own, others)


def _tile_rows(R, cap=2048):
    best = 8
    for t in range(8, min(R, cap) + 1, 8):
        if R % t == 0:
            best = t
    return best if R % 8 == 0 else R


def _swap_halves(half):
    def body(in_ref, out_ref, send_sem, recv_sem):
        x, y, c = _my_place()
        cp = pltpu.make_async_remote_copy(src_ref=in_ref, dst_ref=out_ref, send_sem=send_sem, recv_sem=recv_sem,
                                          device_id=(x, y, 1 - c), device_id_type=MESH)
        cp.start()
        cp.wait()

    return pl.pallas_call(
        body, out_shape=jax.ShapeDtypeStruct(half.shape, half.dtype), in_specs=[ANY], out_specs=ANY,
        scratch_shapes=[pltpu.SemaphoreType.DMA, pltpu.SemaphoreType.DMA], name="swap_halves",
    )(half)


def _allreduce_small(v):
    R, L = v.shape

    def body(in_ref, out_ref, buf, send_sems, recv_sems):
        x, y, c = _my_place()
        me = 4 * x + 2 * y + c
        buf[me] = in_ref[...]
        started = []
        for k in range(1, N_DEV):
            to = (x ^ (k >> 2), y ^ ((k >> 1) & 1), c ^ (k & 1))
            cp = pltpu.make_async_remote_copy(src_ref=in_ref, dst_ref=buf.at[me], send_sem=send_sems.at[k - 1],
                                              recv_sem=recv_sems.at[k - 1], device_id=to, device_id_type=MESH)
            cp.start()
            started.append(cp)
        for cp in started:
            cp.wait()
        acc = buf[0]
        for i in range(1, N_DEV):
            acc = acc + buf[i]
        out_ref[...] = acc

    vm = pl.BlockSpec(memory_space=pltpu.VMEM)
    return pl.pallas_call(
        body, out_shape=jax.ShapeDtypeStruct((R, L), F32), in_specs=[vm], out_specs=vm,
        scratch_shapes=[pltpu.VMEM((N_DEV, R, L), F32), pltpu.SemaphoreType.DMA((7,)), pltpu.SemaphoreType.DMA((7,))],
        name="allreduce_small",
    )(v)


def _adamw(name, w, g, m, v):
    shape = w.shape
    C = shape[-1]
    R = int(np.prod(shape[:-1]))
    args = [a.reshape(R, C).astype(F32) for a in (w, g, m, v)]
    tr = _tile_rows(R, 256)

    def body(w_ref, g_ref, m_ref, v_ref, d_ref, nm_ref, nv_ref):
        g = g_ref[...]
        m = ADAM_B1 * m_ref[...] + (1.0 - ADAM_B1) * g
        v = ADAM_B2 * v_ref[...] + (1.0 - ADAM_B2) * (g * g)
        m_hat = m / (1.0 - ADAM_B1 ** ADAM_STEP)
        v_hat = v / (1.0 - ADAM_B2 ** ADAM_STEP)
        d_ref[...] = -ADAM_LR * (m_hat / (jnp.sqrt(v_hat) + ADAM_EPS) + ADAM_WD * w_ref[...])
        nm_ref[...] = m
        nv_ref[...] = v

    spec = pl.BlockSpec((tr, C), lambda i: (i, 0))
    out = pl.pallas_call(
        body, grid=(R // tr,), in_specs=[spec] * 4, out_specs=[spec] * 3,
        out_shape=[jax.ShapeDtypeStruct((R, C), F32)] * 3, name="adamw_" + name, compiler_params=_params(("parallel",)),
    )(*args)
    return [o.reshape(shape) for o in out]


def kernel(x, mem, pre1_g, post1_g, pre2_g, post2_g, mem_norm_g, w_in, fox_f_bias, rwkv_mu, rwkv_w0, rwkv_w_up, rwkv_a0, rwkv_a_up, rwkv_g_up, rwkv_k_k, rwkv_k_a, rwkv_r_k, rwkv_gn_g, rwkv_gn_b, w_mem_kv, w_fox_out, w_rwkv_out, w_mem_out, w_o, w_ffn_gate, w_ffn_up, w_ffn_down, loss_target, m_pre1_g, m_post1_g, m_pre2_g, m_post2_g, m_mem_norm_g, m_w_in, m_fox_f_bias, m_rwkv_mu, m_rwkv_w0, m_rwkv_w_up, m_rwkv_a0, m_rwkv_a_up, m_rwkv_g_up, m_rwkv_k_k, m_rwkv_k_a, m_rwkv_r_k, m_rwkv_gn_g, m_rwkv_gn_b, m_w_mem_kv, m_w_fox_out, m_w_rwkv_out, m_w_mem_out, m_w_o, m_w_ffn_gate, m_w_ffn_up, m_w_ffn_down, v_pre1_g, v_post1_g, v_pre2_g, v_post2_g, v_mem_norm_g, v_w_in, v_fox_f_bias, v_rwkv_mu, v_rwkv_w0, v_rwkv_w_up, v_rwkv_a0, v_rwkv_a_up, v_rwkv_g_up, v_rwkv_k_k, v_rwkv_k_a, v_rwkv_r_k, v_rwkv_gn_g, v_rwkv_gn_b, v_w_mem_kv, v_w_fox_out, v_w_rwkv_out, v_w_mem_out, v_w_o, v_w_ffn_gate, v_w_ffn_up, v_w_ffn_down):
    given = dict(locals())
    w_loc = {n: given[n] for n in WEIGHTS}
    m_loc = {n: given["m_" + n] for n in WEIGHTS}
    v_loc = {n: given["v_" + n] for n in WEIGHTS}

    shard_shapes = {n: tuple(w_loc[n].shape[1:]) for n in BIG}
    groups = {names: _wire_layout(shard_shapes, names) for names in (FIRST, LATER)}
    core = lax.axis_index("c")

    def pack_weights(names):
        return _wire_pack([_to_wire(n, w_loc[n][0].astype(BF16)) for n in names], groups[names][1])

    def unpack_weights(gathered, names):
        layout, half_rows = groups[names]
        gathered = gathered.reshape(N_CHIPS, 2 * half_rows, WIRE_W)
        out = {}
        for n in names:
            off, rows = layout[n]
            blocks = _from_wire(n, gathered[:, off:off + rows], shard_shapes[n])
            if n in LORA:
                out[n] = blocks.transpose(2, 0, 1).reshape(blocks.shape[2], -1)
            else:
                out[n] = blocks.reshape(-1, blocks.shape[2])
        return out

    def pack_grads(gw, names):
        blocks = []
        for n in names:
            r, c = shard_shapes[n]
            g = gw[n].astype(BF16)
            if n in LORA:
                g = g.reshape(r, N_CHIPS, c).transpose(1, 0, 2)
            elif n in TRANSPOSED:
                g = jnp.swapaxes(g.reshape(N_CHIPS, c, r), 1, 2)
            else:
                g = g.reshape(N_CHIPS, r, c)
            blocks.append(_to_wire(n, g))
        return _wire_pack(blocks, groups[names][1])

    def unpack_grads(half, other, names):
        layout, _ = groups[names]
        reduced = jnp.where(core == 0, jnp.concatenate([half, other]), jnp.concatenate([other, half]))
        out = {}
        for n in names:
            off, rows = layout[n]
            g = _from_wire(n, reduced[off:off + rows], shard_shapes[n])
            out[n] = g if n in ROW_SHARDED else g.T
        return out

    first = _gather_weights(pack_weights(FIRST))
    late = _gather_late_start(pack_weights(LATER) + (first[0, 0, 0, 0] * 0).astype(BF16))
    W = unpack_weights(first, FIRST)
    W.update({n: w_loc[n][0] for n in SMALL})
    W["pre1_g"] = W["pre1_g"] + late[4][0, 0]
    early = []

    def late_weights(after):
        return unpack_weights(_gather_late_wait(late, after), LATER)

    def early_grads(gw, thru):
        early.append(_scatter_early_start(pack_grads(gw, LATER)))
        return thru + early[0][4][0, 0].astype(thru.dtype)

    loss, grad_x, gw = _layer_step(x, mem, loss_target, W, late_weights, early_grads)

    packed = pack_grads(gw, FIRST)
    own_halves = lax.dynamic_index_in_dim(packed, core, axis=1, keepdims=False)
    chip_sums = _pair_add(own_halves, _pair_exchange(packed))
    own_chip = lax.dynamic_index_in_dim(chip_sums, 2 * lax.axis_index("x") + lax.axis_index("y"), axis=0, keepdims=False)
    half_first = _sum_chips(own_chip, _scatter_grads(chip_sums))
    half_later = _sum_slots(_scatter_early_wait(early[0], half_first))
    rows_first = groups[FIRST][1]
    other = _swap_halves(jnp.concatenate([half_first, half_later]))
    g_shard = {**unpack_grads(half_first, other[:rows_first], FIRST), **unpack_grads(half_later, other[rows_first:], LATER)}

    small_shapes = [w_loc[n].shape[1:] for n in SMALL] + [(1,)]
    n_small = sum(int(np.prod(s)) for s in small_shapes)
    small_rows = -(-n_small // (8 * LANES)) * 8
    flat = jnp.concatenate([gw[n].reshape(-1) for n in SMALL] + [loss.reshape(1)])
    flat = jnp.pad(flat, (0, small_rows * LANES - n_small)).reshape(small_rows, LANES).reshape(-1)
    small, off = [], 0
    flat = _allreduce_small(flat.reshape(small_rows, LANES)).reshape(-1)
    for s in small_shapes:
        cnt = int(np.prod(s))
        small.append(flat[off:off + cnt].reshape(s))
        off += cnt
    g_small = dict(zip(SMALL, small[:-1]))
    loss = small[-1][0]

    grads, deltas, new_m, new_v = [], [], [], []
    for n in WEIGHTS:
        g = (g_shard[n] if n in g_shard else g_small[n]).reshape(w_loc[n].shape)
        d, nm, nv = _adamw(n, w_loc[n], g, m_loc[n], v_loc[n])
        grads.append(g)
        deltas.append(d)
        new_m.append(nm)
        new_v.append(nv)
    return (loss, grad_x, *grads, *deltas, *new_m, *new_v)
```

```python
import functools
import math

import numpy as np
import jax
import jax.numpy as jnp
from jax import lax
from jax.experimental import pallas as pl
from jax.experimental.pallas import tpu as pltpu

F32, BF16 = jnp.float32, jnp.bfloat16
MESH = pl.DeviceIdType.MESH

D_MODEL = 1024
HEAD_DIM = 64
N_HEADS = 8
BR_W = 512
MEM_HEADS = 4
MEM_HEAD_DIM = 128
D_FF = 2816
NORM_EPS = 1e-6
GN_EPS = 64e-5
N_CHIPS = 4
N_DEV = 8
LANES = 128
VMEM_LIMIT = 48 * 1024 * 1024

ADAM_LR, ADAM_B1, ADAM_B2, ADAM_EPS, ADAM_WD, ADAM_STEP = 0.001, 0.9, 0.999, 1e-08, 0.01, 10

FOX_COLS = 3 * BR_W + N_HEADS
RWKV_COLS = 3 * BR_W + 64 + 64 + 128
COL_QKV = (0, 3 * BR_W)
COL_F = (3 * BR_W, FOX_COLS)
COL_RW = (FOX_COLS, FOX_COLS + RWKV_COLS)
COL_MQ = (COL_RW[1], COL_RW[1] + BR_W)
COL_GATE = (COL_MQ[1], COL_MQ[1] + 3 * D_MODEL)

NT_DIMS = (((1,), (1,)), ((), ()))
TN_DIMS = (((0,), (0,)), ((), ()))


def _params(sem=None, **kw):
    return pltpu.CompilerParams(dimension_semantics=sem, vmem_limit_bytes=VMEM_LIMIT, **kw)


def _sigmoid(x):
    return 1.0 / (1.0 + jnp.exp(-x))


def _log_sigmoid(x):
    return jnp.minimum(x, 0.0) - jnp.log(1.0 + jnp.exp(-jnp.abs(x)))


def _bdot(a, b, dims=None):
    a, b = a.astype(BF16), b.astype(BF16)
    if dims is None:
        return jnp.dot(a, b, preferred_element_type=F32)
    return lax.dot_general(a, b, dims, preferred_element_type=F32)


def _hdot(a, ones):
    ones = ones.astype(BF16)
    hi = a.astype(BF16)
    r1 = a - hi.astype(F32)
    mid = r1.astype(BF16)
    lo = (r1 - mid.astype(F32)).astype(BF16)
    dot = lambda x: jnp.dot(x, ones, preferred_element_type=F32)
    return dot(hi) + dot(mid) + dot(lo)


def _tile(n, cap):
    best = None
    for t in range(LANES, min(n, cap) + 1, LANES):
        if n % t == 0:
            best = t
    return best or n


def _rowwise(name, fn, rows, consts, outs, accs=(), tm=256):
    T = rows[0].shape[0]
    tm = min(tm, T)
    assert T % tm == 0
    nr, nc, no, na = len(rows), len(consts), len(outs), len(accs)

    def body(*refs):
        res = fn(*[r[...].astype(F32) for r in refs[:nr + nc]])
        if not isinstance(res, (tuple, list)):
            res = (res,)
        orefs, arefs = refs[nr + nc:nr + nc + no], refs[nr + nc + no:]
        for ref, val in zip(orefs, res[:no]):
            ref[...] = val.astype(ref.dtype)
        if na:
            @pl.when(pl.program_id(0) == 0)
            def _():
                for ref in arefs:
                    ref[...] = jnp.zeros(ref.shape, ref.dtype)
            for ref, val in zip(arefs, res[no:]):
                ref[...] += val

    in_specs = ([pl.BlockSpec((tm, r.shape[1]), lambda i: (i, 0)) for r in rows]
                + [pl.BlockSpec(c.shape, lambda i: (0, 0)) for c in consts])
    out_specs = ([pl.BlockSpec((tm, w), lambda i: (i, 0)) for w, _ in outs]
                 + [pl.BlockSpec(s, lambda i: (0, 0)) for s, _ in accs])
    out_shape = ([jax.ShapeDtypeStruct((T, w), dt) for w, dt in outs]
                 + [jax.ShapeDtypeStruct(s, dt) for s, dt in accs])
    return pl.pallas_call(
        body, grid=(T // tm,), in_specs=in_specs, out_specs=out_specs, out_shape=out_shape, name=name,
        compiler_params=_params(("arbitrary",) if na else ("parallel",)),
    )(*rows, *consts)


MM_TILE_CAP = 1408
MM_WHOLE_K = 2048


def _mm(name, a, b, ta=False, tb=False, out_dtype=F32, add=None):
    M, K = (a.shape[1], a.shape[0]) if ta else a.shape
    K2, N = (b.shape[1], b.shape[0]) if tb else b.shape
    assert K == K2
    tm, tn = _tile(M, MM_TILE_CAP), _tile(N, MM_TILE_CAP)
    tk = K if K <= MM_WHOLE_K else _tile(K, MM_TILE_CAP)
    assert M % tm == 0 and N % tn == 0 and K % tk == 0
    nk = K // tk
    a_dim, b_dim = (0 if ta else 1), (1 if tb else 0)

    def body(*refs):
        a_ref, b_ref = refs[0], refs[1]
        n_in = 2 if add is None else 3
        o_ref, acc = refs[n_in], (refs[n_in + 1] if nk > 1 else None)
        k = pl.program_id(2)
        part = lax.dot_general(a_ref[...].astype(BF16), b_ref[...].astype(BF16),
                               (((a_dim,), (b_dim,)), ((), ())), preferred_element_type=F32)

        def finish(r):
            if add is not None:
                r = r + refs[2][...].astype(F32)
            o_ref[...] = r.astype(o_ref.dtype)

        if nk == 1:
            finish(part)
            return

        @pl.when(k == 0)
        def _():
            acc[...] = part

        @pl.when(k > 0)
        def _():
            acc[...] += part

        @pl.when(k == nk - 1)
        def _():
            finish(acc[...])

    a_spec = pl.BlockSpec((tk, tm), lambda i, j, k: (k, i)) if ta else pl.BlockSpec((tm, tk), lambda i, j, k: (i, k))
    b_spec = pl.BlockSpec((tn, tk), lambda i, j, k: (j, k)) if tb else pl.BlockSpec((tk, tn), lambda i, j, k: (k, j))
    o_spec = pl.BlockSpec((tm, tn), lambda i, j, k: (i, j))
    ins, in_specs = [a, b], [a_spec, b_spec]
    if add is not None:
        ins.append(add)
        in_specs.append(o_spec)
    return pl.pallas_call(
        body, grid=(M // tm, N // tn, nk), in_specs=in_specs, out_specs=o_spec,
        out_shape=jax.ShapeDtypeStruct((M, N), out_dtype), scratch_shapes=[pltpu.VMEM((tm, tn), F32)] if nk > 1 else [],
        name=name, compiler_params=_params(("parallel", "parallel", "arbitrary")),
    )(*ins)


def _rowsum(x):
    return jnp.sum(x, axis=0, keepdims=True)


def _rms_stat(x):
    return lax.rsqrt(jnp.mean(x * x, axis=-1, keepdims=True) + NORM_EPS)


def _rms_bwd(dy, x, g):
    r = _rms_stat(x)
    xn = x * r
    dxn = dy * g
    dx = r * (dxn - xn * jnp.mean(dxn * xn, axis=-1, keepdims=True))
    return dx, _rowsum(dy * xn)


def _fox_c_fwd(f8t, bias_col, tc=256):
    B, H, S = f8t.shape
    tc = min(tc, S)

    def body(f_ref, b_ref, c_ref, carry):
        @pl.when(pl.program_id(1) == 0)
        def _():
            carry[...] = jnp.zeros(carry.shape, F32)
        lf = _log_sigmoid(f_ref[...] + b_ref[...])
        row = lax.broadcasted_iota(jnp.int32, (tc, tc), 0)
        col = lax.broadcasted_iota(jnp.int32, (tc, tc), 1)
        c = _hdot(lf, (row <= col).astype(F32)) + carry[...]
        c_ref[...] = c
        carry[...] = c[:, tc - 1:tc]

    return pl.pallas_call(
        body, grid=(B, S // tc),
        in_specs=[pl.BlockSpec((None, H, tc), lambda b, i: (b, 0, i)), pl.BlockSpec((H, 1), lambda b, i: (0, 0))],
        out_specs=pl.BlockSpec((None, H, tc), lambda b, i: (b, 0, i)),
        out_shape=jax.ShapeDtypeStruct((B, H, S), F32), scratch_shapes=[pltpu.VMEM((H, 1), F32)], name="fox_c_fwd",
        compiler_params=_params(("parallel", "arbitrary")),
    )(f8t, bias_col)


def _fox_c_bwd(dc, f8t, bias_col, tc=256):
    B, H, S = f8t.shape
    tc = min(tc, S)
    n = S // tc

    def body(dc_ref, f_ref, b_ref, df_ref, db_ref, carry):
        @pl.when(pl.program_id(1) == 0)
        def _():
            carry[...] = jnp.zeros(carry.shape, F32)
            db_ref[...] = jnp.zeros(db_ref.shape, F32)
        row = lax.broadcasted_iota(jnp.int32, (tc, tc), 0)
        col = lax.broadcasted_iota(jnp.int32, (tc, tc), 1)
        dlf = _hdot(dc_ref[...], (row >= col).astype(F32)) + carry[...]
        z = f_ref[...] + b_ref[...]
        df = dlf * (1.0 - _sigmoid(z))
        df_ref[...] = df
        db_ref[...] += jnp.sum(df, axis=1, keepdims=True)
        carry[...] = dlf[:, 0:1]

    rev = lambda b, i: (b, 0, n - 1 - i)
    return pl.pallas_call(
        body, grid=(B, n),
        in_specs=[pl.BlockSpec((None, H, tc), rev), pl.BlockSpec((None, H, tc), rev), pl.BlockSpec((H, 1), lambda b, i: (0, 0))],
        out_specs=[pl.BlockSpec((None, H, tc), rev), pl.BlockSpec((None, H, 1), lambda b, i: (b, 0, 0))],
        out_shape=[jax.ShapeDtypeStruct((B, H, S), F32), jax.ShapeDtypeStruct((B, H, 1), F32)],
        scratch_shapes=[pltpu.VMEM((H, 1), F32)], name="fox_c_bwd",
        compiler_params=_params(("parallel", "arbitrary")),
    )(dc, f8t, bias_col)


NEG_BIG = -1e30


def _fox_logits(q, kj, cq, ckj, i, j, tq, tk, scale):
    s = _bdot(q, kj, NT_DIMS) * scale + (cq - ckj)
    row = lax.broadcasted_iota(jnp.int32, (tq, tk), 0)
    col = lax.broadcasted_iota(jnp.int32, (tq, tk), 1)
    return s, col <= row + (i * tq - j * tk)


def _fox_key_blocks(i, tq, tk):
    return (i * tq + tq - 1) // tk + 1


def _fox_fwd(qkv, cq, ck, B):
    T = qkv.shape[0]
    S = T // B
    (nq, tq), (nk, tk) = cq.shape[1:3], (ck.shape[1], ck.shape[3])
    scale = HEAD_DIM ** -0.5

    def body(q_ref, k_ref, v_ref, cq_ref, ck_ref, o_ref, lse_ref):
        i = pl.program_id(1)
        lo = lax.broadcasted_iota(jnp.int32, (tq, LANES), 1) < HEAD_DIM
        q = q_ref[...]
        qh = (jnp.where(lo, q, 0), jnp.where(lo, 0, q))

        def step(j, carry):
            rows = pl.ds(pl.multiple_of(j * tk, tk), tk)
            kj, vj = k_ref[rows, :], v_ref[rows, :]
            new = []
            for h in range(2):
                m, l, acc = carry[h]
                s, ok = _fox_logits(qh[h], kj, cq_ref[h], ck_ref[h, j], i, j, tq, tk, scale)
                s = jnp.where(ok, s, NEG_BIG)
                m2 = jnp.maximum(m, jnp.max(s, axis=1, keepdims=True))
                p = jnp.exp(s - m2)
                al = jnp.exp(m - m2)
                new.append((m2, al * l + jnp.sum(p, axis=1, keepdims=True), al * acc + _bdot(p, vj)))
            return tuple(new)

        init = tuple((jnp.full((tq, 1), NEG_BIG, F32), jnp.zeros((tq, 1), F32), jnp.zeros((tq, LANES), F32)) for _ in range(2))
        (m0, l0, a0), (m1, l1, a1) = lax.fori_loop(0, _fox_key_blocks(i, tq, tk), step, init)
        o_ref[...] = jnp.where(lo, a0 / l0, a1 / l1).astype(o_ref.dtype)
        lse_ref[0] = m0 + jnp.log(l0)
        lse_ref[1] = m1 + jnp.log(l1)

    seq = lambda col0: pl.BlockSpec((S, LANES), lambda g, i: (g // 4, col0 + g % 4))
    blk = lambda col0: pl.BlockSpec((tq, LANES), lambda g, i: ((g // 4) * nq + i, col0 + g % 4))
    col = pl.BlockSpec((2, None, tq, 1), lambda g, i: (g, i, 0, 0))
    return pl.pallas_call(
        body, grid=(B * 4, nq), in_specs=[blk(0), seq(4), seq(8), col, pl.BlockSpec((2, nk, 1, tk), lambda g, i: (g, 0, 0, 0))],
        out_specs=[blk(0), col],
        out_shape=[jax.ShapeDtypeStruct((T, BR_W), BF16), jax.ShapeDtypeStruct(cq.shape, F32)], name="fox_fwd",
        compiler_params=_params(("parallel", "parallel")),
    )(qkv, qkv, qkv, cq, ck)


def _fox_bwd(qkv, o, do, cq, ck, lse, B):
    T = qkv.shape[0]
    S = T // B
    (nq, tq), (nk, tk) = cq.shape[1:3], (ck.shape[1], ck.shape[3])
    scale = HEAD_DIM ** -0.5

    def body(q_ref, k_ref, v_ref, o_ref, do_ref, cq_ref, ck_ref, lse_ref, dq_ref, dk_ref, dv_ref, dck_ref, dcq_ref,
             dk_acc, dv_acc):
        dk_acc[...] = jnp.zeros(dk_acc.shape, F32)
        dv_acc[...] = jnp.zeros(dv_acc.shape, F32)
        dck_ref[...] = jnp.zeros(dck_ref.shape, F32)
        lo = lax.broadcasted_iota(jnp.int32, (tq, LANES), 1) < HEAD_DIM

        def qloop(i, _):
            qrows = pl.ds(pl.multiple_of(i * tq, tq), tq)
            q, do_i, o_i = q_ref[qrows, :], do_ref[qrows, :], o_ref[qrows, :].astype(F32)
            qh = (jnp.where(lo, q, 0), jnp.where(lo, 0, q))
            doh = (jnp.where(lo, do_i, 0), jnp.where(lo, 0, do_i))
            delta = [jnp.sum(doh[h].astype(F32) * o_i, axis=1, keepdims=True) for h in range(2)]

            def kloop(j, carry):
                krows = pl.ds(pl.multiple_of(j * tk, tk), tk)
                kj, vj = k_ref[krows, :], v_ref[krows, :]
                new = []
                for h in range(2):
                    dq, dcq = carry[h]
                    s, ok = _fox_logits(qh[h], kj, cq_ref[h, i], ck_ref[h, j], i, j, tq, tk, scale)
                    p = jnp.where(ok, jnp.exp(s - lse_ref[h, i]), 0.0)
                    ds = p * (_bdot(doh[h], vj, NT_DIMS) - delta[h])
                    dv_acc[krows, :] += _bdot(p, doh[h], TN_DIMS)
                    dk_acc[krows, :] += _bdot(ds, qh[h], TN_DIMS) * scale
                    dck_ref[h, j] += -_rowsum(ds)
                    new.append((dq + _bdot(ds, kj) * scale, dcq + jnp.sum(ds, axis=1, keepdims=True)))
                return tuple(new)

            init = tuple((jnp.zeros((tq, LANES), F32), jnp.zeros((tq, 1), F32)) for _ in range(2))
            (dq0, dcq0), (dq1, dcq1) = lax.fori_loop(0, _fox_key_blocks(i, tq, tk), kloop, init)
            dq_ref[qrows, :] = jnp.where(lo, dq0, dq1).astype(dq_ref.dtype)
            dcq_ref[0, i] = dcq0
            dcq_ref[1, i] = dcq1
            return 0

        lax.fori_loop(0, nq, qloop, 0)
        dk_ref[...] = dk_acc[...].astype(dk_ref.dtype)
        dv_ref[...] = dv_acc[...].astype(dv_ref.dtype)

    seq = lambda col0: pl.BlockSpec((S, LANES), lambda g: (g // 4, col0 + g % 4))
    col = pl.BlockSpec((2, nq, tq, 1), lambda g: (g, 0, 0, 0))
    row = pl.BlockSpec((2, nk, 1, tk), lambda g: (g, 0, 0, 0))
    out = jax.ShapeDtypeStruct((T, BR_W), BF16)
    return pl.pallas_call(
        body, grid=(B * 4,), in_specs=[seq(0), seq(4), seq(8), seq(0), seq(0), col, row, col],
        out_specs=[seq(0), seq(0), seq(0), row, col],
        out_shape=[out, out, out, jax.ShapeDtypeStruct(ck.shape, F32), jax.ShapeDtypeStruct(cq.shape, F32)],
        scratch_shapes=[pltpu.VMEM((S, LANES), F32), pltpu.VMEM((S, LANES), F32)], name="fox_bwd",
        compiler_params=_params(("parallel",)),
    )(qkv, qkv, qkv, o, do, cq, ck, lse)


def _mem_probs(qh, kh):
    s = _bdot(qh, kh, NT_DIMS) * (MEM_HEAD_DIM ** -0.5)
    e = jnp.exp(s - jnp.max(s, axis=1, keepdims=True))
    return e / jnp.sum(e, axis=1, keepdims=True)


def _mem_fwd(q, mem_kv, B, tq=512):
    T = q.shape[0]
    S, Lm = T // B, mem_kv.shape[0] // B
    tq = min(tq, S)
    n = S // tq

    def body(q_ref, k_ref, v_ref, o_ref):
        for h in range(MEM_HEADS):
            sl = slice(h * MEM_HEAD_DIM, (h + 1) * MEM_HEAD_DIM)
            p = _mem_probs(q_ref[:, sl], k_ref[:, sl])
            o_ref[:, sl] = _bdot(p, v_ref[:, sl]).astype(o_ref.dtype)

    qs = pl.BlockSpec((tq, BR_W), lambda b, i: (b * n + i, 0))
    return pl.pallas_call(
        body, grid=(B, n),
        in_specs=[qs, pl.BlockSpec((Lm, BR_W), lambda b, i: (b, 0)), pl.BlockSpec((Lm, BR_W), lambda b, i: (b, 1))],
        out_specs=qs, out_shape=jax.ShapeDtypeStruct((T, BR_W), BF16), name="mem_fwd",
        compiler_params=_params(("parallel", "parallel")),
    )(q, mem_kv, mem_kv)


def _mem_bwd(q, mem_kv, do, B, tq=512):
    T = q.shape[0]
    S, Lm = T // B, mem_kv.shape[0] // B
    tq = min(tq, S)
    n = S // tq
    scale = MEM_HEAD_DIM ** -0.5

    def body(q_ref, k_ref, v_ref, do_ref, dq_ref, dk_ref, dv_ref):
        @pl.when(pl.program_id(1) == 0)
        def _():
            dk_ref[...] = jnp.zeros(dk_ref.shape, F32)
            dv_ref[...] = jnp.zeros(dv_ref.shape, F32)
        for h in range(MEM_HEADS):
            sl = slice(h * MEM_HEAD_DIM, (h + 1) * MEM_HEAD_DIM)
            qh, kh, vh, doh = q_ref[:, sl], k_ref[:, sl], v_ref[:, sl], do_ref[:, sl]
            p = _mem_probs(qh, kh)
            dp = _bdot(doh, vh, NT_DIMS)
            ds = p * (dp - jnp.sum(p * dp, axis=1, keepdims=True))
            dq_ref[:, sl] = (_bdot(ds, kh) * scale).astype(dq_ref.dtype)
            dk_ref[:, sl] += _bdot(ds, qh, TN_DIMS) * scale
            dv_ref[:, sl] += _bdot(p, doh, TN_DIMS)

    qs = pl.BlockSpec((tq, BR_W), lambda b, i: (b * n + i, 0))
    kv = pl.BlockSpec((Lm, BR_W), lambda b, i: (b, 0))
    return pl.pallas_call(
        body, grid=(B, n),
        in_specs=[qs, kv, pl.BlockSpec((Lm, BR_W), lambda b, i: (b, 1)), qs], out_specs=[qs, kv, kv],
        out_shape=[jax.ShapeDtypeStruct((T, BR_W), BF16), jax.ShapeDtypeStruct((B * Lm, BR_W), F32),
                   jax.ShapeDtypeStruct((B * Lm, BR_W), F32)], name="mem_bwd",
        compiler_params=_params(("parallel", "arbitrary")),
    )(q, mem_kv, mem_kv, do)


def _head_ones():
    h = np.arange(BR_W) // HEAD_DIM
    return jnp.asarray((h[:, None] == h[None, :]).astype(np.float32))


def _rw_prep(p, pp, mu, w0, w1, a0, w2, g_up, k_k, k_a, bd):
    ps = p + (pp - p) * mu
    r, k, v = ps[:, 0:512], ps[:, 512:1024], ps[:, 1024:1536]
    wa, gd = ps[:, 1536:1664], ps[:, 1664:1792]
    th = jnp.tanh(wa)
    z = w0 + _bdot(th, w1)
    wl = -jnp.exp(_log_sigmoid(z) - 0.5)
    w = jnp.exp(wl)
    a = _sigmoid(a0 + _bdot(wa, w2))
    sg = _sigmoid(gd)
    g = _bdot(sg, g_up)
    kq = k * k_k
    n2 = _hdot(kq * kq, bd)
    inv = lax.rsqrt(jnp.maximum(n2, 1e-24))
    kk = kq * inv
    k2 = k * (1.0 + (a - 1.0) * k_a)
    return dict(ps=ps, r=r, k=k, v=v, wa=wa, th=th, z=z, wl=wl, w=w, a=a, sg=sg, g=g, kq=kq, n2=n2, inv=inv, kk=kk, k2=k2)


def _keycol_selector(tm, Tc):
    e = np.zeros((tm, (tm // Tc) * LANES), np.float32)
    for t in range(tm):
        c, tl = divmod(t, Tc)
        e[t, c * LANES + tl] = e[t, c * LANES + Tc + tl] = 1.0
    return jnp.asarray(e, BF16)


def _rw_prep_fwd(p, pp, consts, B, Tc, tm=256):
    assert 2 * Tc == LANES
    T = p.shape[0]
    S = T // B
    nb, cpb = S // tm, tm // Tc
    sel = _keycol_selector(tm, Tc)
    nc = len(consts)

    def body(*refs):
        t = _rw_prep(*[r[...] for r in refs[:2 + nc]])
        sel_ref = refs[2 + nc]
        rows, cols = refs[3 + nc:7 + nc], refs[7 + nc:]
        for ref, val in zip(rows, (t["r"], t["k2"], t["v"], t["g"])):
            ref[...] = val
        lo = lax.broadcasted_iota(jnp.int32, (HEAD_DIM, LANES), 1) < HEAD_DIM
        operands = (t["w"], -t["kk"], t["kk"] * t["a"], t["k2"], t["r"])
        for n, (ref, x) in enumerate(zip(cols, operands)):
            terms = _split_bf16(x) if ref.dtype == F32 else (x.astype(BF16),)
            for hp in range(4):
                xt = sum(lax.dot_general(tt[:, hp * LANES:(hp + 1) * LANES], sel_ref[...], TN_DIMS,
                                         preferred_element_type=F32) for tt in terms)
                for c in range(cpb):
                    blk = xt[:, c * LANES:(c + 1) * LANES]
                    ref[hp, c] = jnp.where(lo, blk[0:HEAD_DIM], blk[HEAD_DIM:2 * HEAD_DIM]).astype(ref.dtype)

    row_spec = lambda w: pl.BlockSpec((tm, w), lambda i: (i, 0))
    col_spec = pl.BlockSpec((None, 4, cpb, HEAD_DIM, LANES), lambda i: (i // nb, 0, i % nb, 0, 0))
    col_shape = lambda dt: jax.ShapeDtypeStruct((B, 4, S // Tc, HEAD_DIM, LANES), dt)
    out = pl.pallas_call(
        body, grid=(T // tm,),
        in_specs=[row_spec(RWKV_COLS)] * 2 + [pl.BlockSpec(c.shape, lambda i: (0, 0)) for c in consts]
        + [pl.BlockSpec(sel.shape, lambda i: (0, 0))],
        out_specs=[row_spec(BR_W)] * 4 + [col_spec] * 5,
        out_shape=[jax.ShapeDtypeStruct((T, BR_W), F32)] * 4 + [col_shape(F32)] + [col_shape(BF16)] * 3 + [col_shape(F32)],
        name="rwkv_prep_fwd", compiler_params=_params(("parallel",)),
    )(p, pp, *consts, sel)
    return out[:4], [c.reshape(B * 4, S // Tc, HEAD_DIM, LANES) for c in out[4:]]


def _rw_prep_bwd(p, pp, cots, consts):
    def fn(p, pp, dr1, dr2, dw, dk21, dk22, dv1, dv2, dav, dbv, dg, mu, w0, w1, a0, w2, g_up, k_k, k_a, bd):
        t = _rw_prep(p, pp, mu, w0, w1, a0, w2, g_up, k_k, k_a, bd)
        dr, dk2, dv = dr1 + dr2, dk21 + dk22, dv1 + dv2
        a, k, kk, kq, inv = t["a"], t["k"], t["kk"], t["kq"], t["inv"]
        dkk = dbv * a - dav
        da = dbv * kk + dk2 * k * k_a
        dk = dk2 * (1.0 + (a - 1.0) * k_a)
        d_k_a = _rowsum(dk2 * k * (a - 1.0))
        proj = _hdot(dkk * kq, bd)
        dkq = dkk * inv - jnp.where(t["n2"] > 1e-24, kq * inv * inv * inv * proj, 0.0)
        dk = dk + dkq * k_k
        d_k_k = _rowsum(dkq * k)
        dpa = da * a * (1.0 - a)
        d_a0 = _rowsum(dpa)
        dwa = _bdot(dpa, w2, NT_DIMS)
        d_w2 = _bdot(t["wa"], dpa, TN_DIMS)
        dz = dw * t["w"] * t["wl"] * (1.0 - _sigmoid(t["z"]))
        d_w0 = _rowsum(dz)
        th = t["th"]
        dwa = dwa + _bdot(dz, w1, NT_DIMS) * (1.0 - th * th)
        d_w1 = _bdot(th, dz, TN_DIMS)
        sg = t["sg"]
        dgd = _bdot(dg, g_up, NT_DIMS) * sg * (1.0 - sg)
        d_g_up = _bdot(sg, dg, TN_DIMS)
        dps = jnp.concatenate([dr, dk, dv, dwa, dgd], axis=1)
        d_mu = _rowsum(dps * (pp - p))
        return dps * (1.0 - mu), dps * mu, d_mu, d_w0, d_w1, d_a0, d_w2, d_g_up, d_k_k, d_k_a

    accs = [((1, RWKV_COLS), F32), ((1, BR_W), F32), ((LANES, BR_W), F32), ((1, BR_W), F32), ((LANES, BR_W), F32),
            ((LANES, BR_W), F32), ((1, BR_W), F32), ((1, BR_W), F32)]
    return _rowwise("rwkv_prep_bwd", fn, [p, pp] + list(cots), consts, [(RWKV_COLS, BF16)] * 2, accs, tm=128)


def _rw_head(y, r, k2, v, g, gn_g, gn_b, r_k, bd):
    mean = _hdot(y, bd) * (1.0 / HEAD_DIM)
    yc = y - mean
    rs = lax.rsqrt(_hdot(yc * yc, bd) * (1.0 / HEAD_DIM) + GN_EPS)
    yn = yc * rs
    bs = _hdot(r * k2 * r_k, bd)
    return yn, rs, bs, yn * gn_g + gn_b + bs * v


def _rw_head_fwd(y, r, k2, v, g, consts):
    def fn(y, r, k2, v, g, *c):
        return _rw_head(y, r, k2, v, g, *c)[3] * g
    return _rowwise("rwkv_head_fwd", fn, [y, r, k2, v, g], consts, [(BR_W, BF16)])[0]


def _rw_head_bwd(dout, y, r, k2, v, g, consts):
    def fn(dout, y, r, k2, v, g, gn_g, gn_b, r_k, bd):
        dout = dout.astype(F32)
        yn, rs, bs, zz = _rw_head(y, r, k2, v, g, gn_g, gn_b, r_k, bd)
        dg = dout * zz
        dz = dout * g
        dyn = dz * gn_g
        inv_n = 1.0 / HEAD_DIM
        dy = rs * (dyn - _hdot(dyn, bd) * inv_n - yn * (_hdot(dyn * yn, bd) * inv_n))
        dq = _hdot(dz * v, bd)
        return dy, dg, dq * k2 * r_k, dq * r * r_k, dz * bs, _rowsum(dz * yn), _rowsum(dz), _rowsum(dq * r * k2)
    return _rowwise("rwkv_head_bwd", fn, [dout, y, r, k2, v, g], consts, [(BR_W, F32)] * 5, [((1, BR_W), F32)] * 3)


SCAN_TC = 64


def _scan_onehot(Tc):
    w = np.zeros((Tc // 2, 2 * Tc, 2 * LANES), np.float32)
    for tt in range(Tc // 2):
        for u in range(2):
            for h in range(2):
                w[tt, h * Tc + 2 * tt + u, u * LANES + h * HEAD_DIM: u * LANES + (h + 1) * HEAD_DIM] = 1.0
    return jnp.asarray(w, BF16)


def _split_bf16(x):
    hi = x.astype(BF16)
    return hi, (x - hi.astype(F32)).astype(BF16)


def _key_tiles(l_w, others, onehot):
    dot = lambda x: jnp.dot(x, onehot, preferred_element_type=F32)
    whi, wmid = l_w
    return [dot(whi) + dot(wmid)] + [dot(o) for o in others]


def _rw_scan_fwd(LW, LA, LB, LK, LR, v, P=4):
    NP, nc, _, Tc2 = LW.shape
    Tc = Tc2 // 2
    S = nc * Tc
    onehot = _scan_onehot(Tc)
    npb = 4 // P

    def body(lw, la, lb, lk, lr, v_ref, oh_ref, y_ref, sa_ref, sb_ref, st):
        @pl.when(pl.program_id(1) == 0)
        def _():
            st[...] = jnp.zeros(st.shape, F32)
        s = [st[p] for p in range(P)]
        cols = [(_split_bf16(lw[p]), [ref[p].astype(BF16) for ref in (la, lb, lk)]) for p in range(P)]
        r_cols = [lr[p].astype(F32) for p in range(P)]
        head0 = lax.broadcasted_iota(jnp.int32, (HEAD_DIM, LANES), 1) < HEAD_DIM
        for tt in range(Tc // 2):
            tiles = [_key_tiles(c[0], c[1], oh_ref[tt]) for c in cols]
            for u in range(2):
                t = 2 * tt + u
                for p in range(P):
                    W, A, Bt, Kt = (x[:, u * LANES:(u + 1) * LANES] for x in tiles[p])
                    R = jnp.where(head0, r_cols[p][:, t:t + 1], r_cols[p][:, Tc + t:Tc + t + 1])
                    ls = slice(p * LANES, (p + 1) * LANES)
                    sb_ref[p, t] = s[p]
                    sa = _rowsum(s[p] * A)
                    s[p] = s[p] * W + Bt * sa + Kt * v_ref[t:t + 1, ls]
                    y_ref[t:t + 1, ls] = _rowsum(s[p] * R)
                    sa_ref[t:t + 1, ls] = sa
        for p in range(P):
            st[p] = s[p]

    lspec = pl.BlockSpec((P, None, HEAD_DIM, Tc2), lambda g, c: (g, c, 0, 0))
    rows = pl.BlockSpec((Tc, P * LANES), lambda g, c: ((g // npb) * nc + c, g % npb))
    rowshape = jax.ShapeDtypeStruct(v.shape, F32)
    return pl.pallas_call(
        body, grid=(NP // P, nc), in_specs=[lspec] * 5 + [rows, pl.BlockSpec(onehot.shape, lambda g, c: (0, 0, 0))],
        out_specs=[rows, rows, pl.BlockSpec((P, Tc, HEAD_DIM, LANES), lambda g, c: (g, c, 0, 0))],
        out_shape=[rowshape, rowshape, jax.ShapeDtypeStruct((NP, S, HEAD_DIM, LANES), F32)],
        scratch_shapes=[pltpu.VMEM((P, HEAD_DIM, LANES), F32)], name="rwkv_scan_fwd",
        compiler_params=_params(("parallel", "arbitrary")),
    )(LW, LA, LB, LK, LR, v, onehot)


SCAN_G_ROWS = 16


def _rw_scan_bwd(LW, LA, LB, LK, LR, v, sa, dy, sb, P=4):
    NP, nc, _, Tc2 = LW.shape
    Tc = Tc2 // 2
    onehot = _scan_onehot(Tc)
    npb = 4 // P

    def body(lw, la, lb, lk, lr, v_ref, sa_ref, dy_ref, sb_ref, oh_ref, dv_ref, dk_ref, db_ref, dw_ref, dr_ref, da_ref, dst):
        @pl.when(pl.program_id(1) == 0)
        def _():
            dst[...] = jnp.zeros(dst.shape, F32)
        rid = lax.broadcasted_iota(jnp.int32, (SCAN_G_ROWS, LANES), 0)
        lane = lax.broadcasted_iota(jnp.int32, (SCAN_G_ROWS, LANES), 1)
        own = (((rid % 2) == 0) == (lane < HEAD_DIM)) & (rid < 10)
        lo = lane[0:1] < HEAD_DIM
        nt = lambda rows, tile: lax.dot_general(rows.astype(BF16), tile.astype(BF16), NT_DIMS, preferred_element_type=F32)
        ds = [dst[p] for p in range(P)]
        cols = [(_split_bf16(lw[p]), [ref[p].astype(BF16) for ref in (la, lb, lk)]) for p in range(P)]
        r_cols = [lr[p].astype(F32) for p in range(P)]
        head0 = lax.broadcasted_iota(jnp.int32, (HEAD_DIM, LANES), 1) < HEAD_DIM
        for tt in reversed(range(Tc // 2)):
            tiles = [_key_tiles(c[0], c[1], oh_ref[tt]) for c in cols]
            pending = [[] for _ in range(P)]
            for u in (1, 0):
                t = 2 * tt + u
                for p in range(P):
                    W, A, Bt, Kt = (x[:, u * LANES:(u + 1) * LANES] for x in tiles[p])
                    R = jnp.where(head0, r_cols[p][:, t:t + 1], r_cols[p][:, Tc + t:Tc + t + 1])
                    ls = slice(p * LANES, (p + 1) * LANES)
                    vr, sar, dyr = (ref[t:t + 1, ls] for ref in (v_ref, sa_ref, dy_ref))
                    sp = sb_ref[p, t]
                    s_t = sp * W + Bt * sar + Kt * vr
                    d = ds[p] + R * dyr
                    dv_ref[t:t + 1, ls] = _rowsum(d * Kt)
                    dsar = _rowsum(d * Bt)
                    rows = jnp.where(rid < 2, vr, jnp.where(rid < 4, sar, jnp.where(rid < 6, 1.0, jnp.where(rid < 8, dyr, dsar))))
                    pending[p].append((t, jnp.where(own, rows, 0.0), [d, d * sp, s_t, sp]))
                    ds[p] = d * W + A * dsar
            for p in range(P):
                ls = slice(p * LANES, (p + 1) * LANES)
                (t1, rows1, tiles1), (t0, rows0, tiles0) = pending[p]
                g2 = nt(jnp.concatenate([rows1, rows0], axis=0), jnp.concatenate(tiles1 + tiles0, axis=0))
                for t, g in ((t1, g2[0:SCAN_G_ROWS, 0:2 * LANES]), (t0, g2[SCAN_G_ROWS:, 2 * LANES:])):
                    ga, gb = g[:, 0:LANES], g[:, LANES:2 * LANES]
                    ra, rb = pltpu.roll(ga, HEAD_DIM, 1), pltpu.roll(gb, HEAD_DIM, 1)
                    dk_ref[t:t + 1, ls] = jnp.where(lo, ga[0:1], ra[1:2])
                    db_ref[t:t + 1, ls] = jnp.where(lo, ga[2:3], ra[3:4])
                    dw_ref[t:t + 1, ls] = jnp.where(lo, ra[4:5], ga[5:6])
                    dr_ref[t:t + 1, ls] = jnp.where(lo, gb[6:7], rb[7:8])
                    da_ref[t:t + 1, ls] = jnp.where(lo, rb[8:9], gb[9:10])
        for p in range(P):
            dst[p] = ds[p]

    rev = lambda g, c: (g, nc - 1 - c, 0, 0)
    lspec = pl.BlockSpec((P, None, HEAD_DIM, Tc2), rev)
    rows = pl.BlockSpec((Tc, P * LANES), lambda g, c: ((g // npb) * nc + nc - 1 - c, g % npb))
    return pl.pallas_call(
        body, grid=(NP // P, nc),
        in_specs=[lspec] * 5 + [rows] * 3 + [pl.BlockSpec((P, Tc, HEAD_DIM, LANES), rev),
                                             pl.BlockSpec(onehot.shape, lambda g, c: (0, 0, 0))],
        out_specs=[rows] * 6, out_shape=[jax.ShapeDtypeStruct(v.shape, F32)] * 6,
        scratch_shapes=[pltpu.VMEM((P, HEAD_DIM, LANES), F32)], name="rwkv_scan_bwd",
        compiler_params=_params(("parallel", "arbitrary")),
    )(LW, LA, LB, LK, LR, v, sa, dy, sb, onehot)


def _shift_prev(p, B):
    T, W = p.shape
    return jnp.pad(p.reshape(B, T // B, W), ((0, 0), (1, 0), (0, 0)))[:, :-1].reshape(T, W)


def _shift_next(p, B):
    T, W = p.shape
    return jnp.pad(p.reshape(B, T // B, W), ((0, 0), (0, 1), (0, 0)))[:, 1:].reshape(T, W)


FOX_FWD_BLOCKS = (512, 1024)
FOX_BWD_BLOCKS = (512, 512)


def _layer_step(x, mem, target, W, late_weights=None, early_grads=None, scan_tc=SCAN_TC, fox_fwd_t=FOX_FWD_BLOCKS,
                fox_bwd_t=FOX_BWD_BLOCKS):
    B, S, _ = x.shape
    T = B * S
    x2, tgt2 = x.reshape(T, D_MODEL), target.reshape(T, D_MODEL)
    mem2 = mem.reshape(-1, D_MODEL)
    w_in_t = W["w_in"]
    wt_qkv, wt_rw, wt_mq, wt_gate = (w_in_t[lo:hi] for lo, hi in (COL_QKV, COL_RW, COL_MQ, COL_GATE))
    wt_f = jnp.pad(w_in_t[COL_F[0]:COL_F[1]], ((0, LANES - N_HEADS), (0, 0)))
    row = lambda v: v.reshape(1, -1).astype(F32)
    pre1_g, post1_g, pre2_g, post2_g, mem_g = (row(W[n]) for n in ("pre1_g", "post1_g", "pre2_g", "post2_g", "mem_norm_g"))

    u = _rowwise("rms_pre1", lambda x, g: x * _rms_stat(x) * g, [x2], [pre1_g], [(D_MODEL, BF16)])[0]
    qkv = _mm("proj_qkv", u, wt_qkv, tb=True, out_dtype=BF16)
    f_pad = _mm("proj_f", u, wt_f, tb=True)
    p_rw = _mm("proj_rwkv", u, wt_rw, tb=True)
    memq = _mm("proj_memq", u, wt_mq, tb=True, out_dtype=BF16)
    gate = _mm("proj_gate", u, wt_gate, tb=True, out_dtype=BF16)

    bias_col = W["fox_f_bias"].reshape(N_HEADS, 1).astype(F32)
    f8t = f_pad[:, :N_HEADS].reshape(B, S, N_HEADS).transpose(0, 2, 1)
    c = _fox_c_fwd(f8t, bias_col)
    G = B * N_HEADS
    q_blocks = lambda a, t: a.reshape(G, S // min(t, S), min(t, S), 1)
    k_blocks = lambda a, t: a.reshape(G, S // min(t, S), 1, min(t, S))
    c_col = c.reshape(G, S, 1)
    fox_out, lse = _fox_fwd(qkv, q_blocks(c_col, fox_fwd_t[0]), k_blocks(c, fox_fwd_t[1]), B)

    bd = _head_ones()
    zpad = jnp.zeros((64, BR_W), F32)
    w1 = jnp.concatenate([W["rwkv_w_up"].astype(F32), zpad], axis=0)
    w2 = jnp.concatenate([zpad, W["rwkv_a_up"].astype(F32)], axis=0)
    prep_consts = [row(W["rwkv_mu"]), row(W["rwkv_w0"]), w1, row(W["rwkv_a0"]), w2, W["rwkv_g_up"].astype(F32),
                   row(W["rwkv_k_k"]), row(W["rwkv_k_a"]), bd]
    p_prev = _shift_prev(p_rw, B)
    (rr, rk2, rv, rg), scan_cols = _rw_prep_fwd(p_rw, p_prev, prep_consts, B, scan_tc)
    ry, rsa, sb = _rw_scan_fwd(*scan_cols, rv)
    head_consts = [row(W["rwkv_gn_g"]), row(W["rwkv_gn_b"]), row(W["rwkv_r_k"]), bd]
    rwkv_out = _rw_head_fwd(ry, rr, rk2, rv, rg, head_consts)

    if late_weights is not None:
        W = {**W, **late_weights(rwkv_out)}

    mn = _rowwise("rms_mem", lambda m, g: m * _rms_stat(m) * g, [mem2], [mem_g], [(D_MODEL, BF16)])[0]
    mem_kv = _mm("proj_memkv", mn, W["w_mem_kv"], out_dtype=BF16)
    mem_out = _mem_fwd(memq, mem_kv, B)

    fo = [_mm("branch_" + n, a, W[n], tb=True, out_dtype=BF16)
          for n, a in (("w_fox_out", fox_out), ("w_rwkv_out", rwkv_out), ("w_mem_out", mem_out))]

    def merge(gate, f0, f1, f2):
        return sum(_sigmoid(gate[:, i * D_MODEL:(i + 1) * D_MODEL]) * f for i, f in enumerate((f0, f1, f2)))
    merged = _rowwise("merge", merge, [gate] + fo, [], [(D_MODEL, BF16)])[0]
    y1 = _mm("proj_o", merged, W["w_o"])

    def mid(x, y1, g1, g2):
        h1 = x + y1 * _rms_stat(y1) * g1
        return h1, h1 * _rms_stat(h1) * g2
    h1, u2 = _rowwise("norm_mid", mid, [x2, y1], [post1_g, pre2_g], [(D_MODEL, F32), (D_MODEL, BF16)])
    gt = _mm("ffn_gate", u2, W["w_ffn_gate"], tb=True, out_dtype=BF16)
    up = _mm("ffn_up", u2, W["w_ffn_up"], tb=True, out_dtype=BF16)
    act = _rowwise("swiglu", lambda gt, up: gt * _sigmoid(gt) * up, [gt, up], [], [(D_FF, BF16)])[0]
    ffn = _mm("ffn_down", act, W["w_ffn_down"])

    def tail(h1, ffn, tgt, g):
        err = h1 + ffn * _rms_stat(ffn) * g - tgt
        dh2 = err * (1.0 / D_MODEL)
        dffn, dg = _rms_bwd(dh2, ffn, g)
        loss = 0.5 * jnp.sum(jnp.sum(err * err, axis=1, keepdims=True) * (1.0 / D_MODEL), axis=0, keepdims=True)
        return dh2, dffn, dg, jnp.broadcast_to(loss, (1, LANES))
    dh2, dffn, d_post2, loss = _rowwise("loss_tail", tail, [h1, ffn, tgt2], [post2_g], [(D_MODEL, F32), (D_MODEL, BF16)],
                                        [((1, D_MODEL), F32), ((1, LANES), F32)])
    gw = {"post2_g": d_post2}
    dact = _mm("d_act", dffn, W["w_ffn_down"], tb=True, out_dtype=BF16)
    gw["w_ffn_down"] = _mm("g_ffn_down", act, dffn, ta=True, out_dtype=BF16)

    def swiglu_bwd(dact, gt, up):
        s = _sigmoid(gt)
        return dact * up * s * (1.0 + gt * (1.0 - s)), dact * gt * s
    dgt, dup = _rowwise("swiglu_bwd", swiglu_bwd, [dact, gt, up], [], [(D_FF, BF16)] * 2)
    du2 = _mm("d_u2_gate", dgt, W["w_ffn_gate"])
    du2 = _mm("d_u2_up", dup, W["w_ffn_up"], add=du2)
    gw["w_ffn_gate"] = _mm("g_ffn_gate", dgt, u2, ta=True, out_dtype=BF16)
    gw["w_ffn_up"] = _mm("g_ffn_up", dup, u2, ta=True, out_dtype=BF16)

    def mid_bwd(du2, dh2, h1, y1, g1, g2):
        dh1_n, d_pre2 = _rms_bwd(du2, h1, g2)
        dh1 = dh2 + dh1_n
        dy1, d_post1 = _rms_bwd(dh1, y1, g1)
        return dh1, dy1, d_post1, d_pre2
    dh1, dy1, gw["post1_g"], gw["pre2_g"] = _rowwise(
        "norm_mid_bwd", mid_bwd, [du2, dh2, h1, y1], [post1_g, pre2_g], [(D_MODEL, F32), (D_MODEL, BF16)],
        [((1, D_MODEL), F32)] * 2)
    dmerged = _mm("d_merged", dy1, W["w_o"], tb=True, out_dtype=BF16)
    gw["w_o"] = _mm("g_w_o", merged, dy1, ta=True, out_dtype=BF16)

    def merge_bwd(dm, gate, f0, f1, f2):
        s = [_sigmoid(gate[:, i * D_MODEL:(i + 1) * D_MODEL]) for i in range(3)]
        dgate = jnp.concatenate([dm * f * si * (1.0 - si) for f, si in zip((f0, f1, f2), s)], axis=1)
        return dm * s[0], dm * s[1], dm * s[2], dgate
    dfo0, dfo1, dfo2, dgate = _rowwise("merge_bwd", merge_bwd, [dmerged, gate] + fo, [],
                                       [(D_MODEL, BF16)] * 3 + [(3 * D_MODEL, BF16)])
    d_branch = {}
    for n, a, dfo in (("w_fox_out", fox_out, dfo0), ("w_rwkv_out", rwkv_out, dfo1), ("w_mem_out", mem_out, dfo2)):
        d_branch[n] = _mm("d_in_" + n, dfo, W[n], out_dtype=BF16)
        gw[n] = _mm("g_" + n, dfo, a, ta=True, out_dtype=BF16)

    dmemq, dkm, dvm = _mem_bwd(memq, mem_kv, d_branch["w_mem_out"], B)
    dmem_kv = jnp.concatenate([dkm, dvm], axis=1)
    gw["w_mem_kv"] = _mm("g_w_mem_kv", mn, dmem_kv, ta=True, out_dtype=BF16)
    dmn = _mm("d_mn", dmem_kv, W["w_mem_kv"], tb=True)
    gw["mem_norm_g"] = _rowwise("rms_mem_bwd", lambda d, m, g: _rms_bwd(d, m, g)[1], [dmn, mem2], [mem_g], [],
                                [((1, D_MODEL), F32)])[0]
    if early_grads is not None:
        d_branch["w_fox_out"] = early_grads(gw, d_branch["w_fox_out"])

    dfq, dfk, dfv, dck, dcq = _fox_bwd(qkv, fox_out, d_branch["w_fox_out"], q_blocks(c_col, fox_bwd_t[0]),
                                       k_blocks(c, fox_bwd_t[1]), q_blocks(lse, fox_bwd_t[0]), B)
    df8t, dbias = _fox_c_bwd(dck.reshape(B, N_HEADS, S) + dcq.reshape(B, N_HEADS, S), f8t, bias_col)
    gw["fox_f_bias"] = jnp.sum(dbias, axis=0).reshape(1, N_HEADS)
    dqkv = jnp.concatenate([dfq, dfk, dfv], axis=1)
    df_pad = jnp.pad(df8t.transpose(0, 2, 1).reshape(T, N_HEADS), ((0, 0), (0, LANES - N_HEADS))).astype(BF16)

    dry, drg, dr_h, dk2_h, dv_h, gw["rwkv_gn_g"], gw["rwkv_gn_b"], gw["rwkv_r_k"] = _rw_head_bwd(
        d_branch["w_rwkv_out"], ry, rr, rk2, rv, rg, head_consts)
    dv_s, dk2_s, db_s, dw_s, dr_s, da_s = _rw_scan_bwd(*scan_cols, rv, rsa, dry, sb)
    dP, dPp, gw["rwkv_mu"], gw["rwkv_w0"], d_w1, gw["rwkv_a0"], d_w2, gw["rwkv_g_up"], gw["rwkv_k_k"], gw["rwkv_k_a"] = \
        _rw_prep_bwd(p_rw, p_prev, [dr_s, dr_h, dw_s, dk2_s, dk2_h, dv_s, dv_h, da_s, db_s, drg], prep_consts)
    gw["rwkv_w_up"], gw["rwkv_a_up"] = d_w1[:64], d_w2[64:]
    dp_rw = dP + _shift_next(dPp, B)

    du = _mm("d_u_qkv", dqkv, wt_qkv)
    du = _mm("d_u_f", df_pad, wt_f, add=du)
    du = _mm("d_u_rwkv", dp_rw, wt_rw, add=du)
    du = _mm("d_u_memq", dmemq, wt_mq, add=du)
    du = _mm("d_u_gate", dgate, wt_gate, add=du)
    gw["w_in"] = jnp.concatenate(
        [_mm("g_w_qkv", dqkv, u, ta=True, out_dtype=BF16), _mm("g_w_f", df_pad, u, ta=True, out_dtype=BF16)[:N_HEADS],
         _mm("g_w_rwkv", dp_rw, u, ta=True, out_dtype=BF16), _mm("g_w_memq", dmemq, u, ta=True, out_dtype=BF16),
         _mm("g_w_gate", dgate, u, ta=True, out_dtype=BF16)], axis=0)

    def pre1_bwd(du, dh1, x, g):
        dx, dg = _rms_bwd(du, x, g)
        return dh1 + dx, dg
    dx, gw["pre1_g"] = _rowwise("rms_pre1_bwd", pre1_bwd, [du, dh1, x2], [pre1_g], [(D_MODEL, F32)], [((1, D_MODEL), F32)])
    return loss[0, 0], dx.reshape(B, S, D_MODEL), gw


TRANSPOSED = ("w_in", "w_ffn_gate", "w_ffn_up", "w_fox_out", "w_rwkv_out", "w_mem_out")
LORA = ("rwkv_w_up", "rwkv_a_up", "rwkv_g_up")
ROW_SHARDED = ("w_mem_kv", "w_o", "w_ffn_down")
FIRST = ("w_in",) + LORA
LATER = ("w_ffn_gate", "w_ffn_up", "w_mem_kv", "w_o", "w_ffn_down", "w_fox_out", "w_rwkv_out", "w_mem_out")
BIG = FIRST + LATER
SMALL = ("pre1_g", "post1_g", "pre2_g", "post2_g", "mem_norm_g", "fox_f_bias", "rwkv_mu", "rwkv_w0", "rwkv_a0", "rwkv_k_k",
         "rwkv_k_a", "rwkv_r_k", "rwkv_gn_g", "rwkv_gn_b")
WEIGHTS = ("pre1_g", "post1_g", "pre2_g", "post2_g", "mem_norm_g", "w_in", "fox_f_bias", "rwkv_mu", "rwkv_w0", "rwkv_w_up",
           "rwkv_a0", "rwkv_a_up", "rwkv_g_up", "rwkv_k_k", "rwkv_k_a", "rwkv_r_k", "rwkv_gn_g", "rwkv_gn_b", "w_mem_kv",
           "w_fox_out", "w_rwkv_out", "w_mem_out", "w_o", "w_ffn_gate", "w_ffn_up", "w_ffn_down")
WIRE_W = 1024
WIRE_ROW_ALIGN = 16
WIRE_HALF_ALIGN = 128


def _wire_rows(name, shard_shape):
    r, c = shard_shape
    if name in ROW_SHARDED:
        return r
    return -(-c // WIRE_ROW_ALIGN) * WIRE_ROW_ALIGN if r == WIRE_W else (r * c) // WIRE_W


def _to_wire(name, a):
    if name not in ROW_SHARDED:
        a = jnp.swapaxes(a, -1, -2)
    lead, (n, w) = a.shape[:-2], a.shape[-2:]
    if w != WIRE_W:
        return a.reshape(lead + ((n * w) // WIRE_W, WIRE_W))
    return jnp.pad(a, [(0, 0)] * len(lead) + [(0, (-n) % WIRE_ROW_ALIGN), (0, 0)])


def _from_wire(name, a, shard_shape):
    r, c = shard_shape
    if name in ROW_SHARDED:
        return a
    return a[..., :c, :] if r == WIRE_W else a.reshape(a.shape[:-2] + (c, r))


def _wire_layout(shard_shapes, names):
    layout, off = {}, 0
    for n in names:
        rows = _wire_rows(n, shard_shapes[n])
        layout[n] = (off, rows)
        off += rows
    return layout, -(-off // (2 * WIRE_HALF_ALIGN)) * WIRE_HALF_ALIGN


def _wire_pack(blocks, half_rows):
    a = jnp.concatenate(blocks, axis=-2)
    lead = a.shape[:-2]
    a = jnp.pad(a, [(0, 0)] * len(lead) + [(0, 2 * half_rows - a.shape[-2]), (0, 0)])
    return a.reshape(lead + (2, half_rows, WIRE_W))


def _my_place():
    return lax.axis_index("x"), lax.axis_index("y"), lax.axis_index("c")


def _other_chips(x, y):
    return [(1 - x, y), (x, 1 - y), (1 - x, 1 - y)]


ANY = pl.BlockSpec(memory_space=pl.ANY)


def _gather_weights(packed):
    _, R, L = packed.shape
    me = 2 * lax.axis_index("x") + lax.axis_index("y")
    base = lax.dynamic_update_index_in_dim(jnp.zeros((N_CHIPS, 2, R, L), packed.dtype), packed, me, 0)

    def body(in_ref, base_ref, out_ref, send_sems, recv_sems):
        x, y, c = _my_place()
        chip = lambda px, py: 2 * px + py
        sibling = (x, y, 1 - c)
        others = _other_chips(x, y)

        def copy(k, src, dst, to):
            return pltpu.make_async_remote_copy(src_ref=src, dst_ref=dst, send_sem=send_sems.at[k], recv_sem=recv_sems.at[k],
                                                device_id=to, device_id_type=MESH)

        sends = [copy(j, in_ref.at[c], out_ref.at[chip(x, y), c], (px, py, c)) for j, (px, py) in enumerate(others)]
        for cp in sends:
            cp.start()
        passed = [copy(3 + j, out_ref.at[chip(px, py), c], out_ref.at[chip(px, py), c], sibling)
                  for j, (px, py) in enumerate(others)]
        for j, (px, py) in enumerate(others):
            copy(j, in_ref.at[c], out_ref.at[chip(px, py), c], (px, py, c)).wait_recv()
            passed[j].start()
        for j, (px, py) in enumerate(others):
            copy(3 + j, in_ref.at[1 - c], out_ref.at[chip(px, py), 1 - c], sibling).wait_recv()
        for cp in sends + passed:
            cp.wait_send()

    return pl.pallas_call(
        body, out_shape=jax.ShapeDtypeStruct(base.shape, base.dtype), in_specs=[ANY, ANY], out_specs=ANY,
        input_output_aliases={1: 0}, scratch_shapes=[pltpu.SemaphoreType.DMA((6,)), pltpu.SemaphoreType.DMA((6,))],
        name="gather_weights",
    )(packed, base)


HBM_SPEC = pl.BlockSpec(memory_space=pltpu.HBM)
SEM_SPEC = pl.BlockSpec(memory_space=pltpu.SEMAPHORE)
DATAFLOW = pltpu.SideEffectType.DATAFLOW_SIDE_EFFECTING


def _in_hbm(a):
    return pltpu.with_memory_space_constraint(a, pltpu.HBM)


def _split_start(name, n_copies, src, land, start_copies):
    def body(src_ref, land_ref, send_sems, recv_sems, src_thru, land_thru, token):
        start_copies(src_ref, land_ref, send_sems, recv_sems)
        token[...] = jnp.zeros(token.shape, token.dtype)

    return pl.pallas_call(
        body, name=name,
        out_shape=(pltpu.SemaphoreType.DMA((n_copies,)), pltpu.SemaphoreType.DMA((n_copies,)), pltpu.HBM(src.shape, src.dtype),
                   pltpu.HBM(land.shape, land.dtype), jax.ShapeDtypeStruct((8, LANES), F32)),
        in_specs=(HBM_SPEC, HBM_SPEC),
        out_specs=(SEM_SPEC, SEM_SPEC, HBM_SPEC, HBM_SPEC, pl.BlockSpec(memory_space=pltpu.VMEM)),
        input_output_aliases={0: 2, 1: 3}, compiler_params=pltpu.CompilerParams(has_side_effects=DATAFLOW),
    )(_in_hbm(src), _in_hbm(land))


def _split_wait(name, handle, after, wait_copies):
    send_sems, recv_sems, src, land = handle[:4]

    def body(src_ref, land_ref, send_sems, recv_sems, after_ref, src_dead, land_out):
        wait_copies(src_ref, land_ref, send_sems, recv_sems)

    return pl.pallas_call(
        body, name=name, out_shape=(pltpu.HBM(src.shape, src.dtype), pltpu.HBM(land.shape, land.dtype)),
        in_specs=(HBM_SPEC, HBM_SPEC, SEM_SPEC, SEM_SPEC, ANY), out_specs=(HBM_SPEC, HBM_SPEC),
        input_output_aliases={0: 0, 1: 1}, compiler_params=pltpu.CompilerParams(has_side_effects=DATAFLOW),
    )(src, land, send_sems, recv_sems, after)[1]


def _late_gather_copies(src_ref, land_ref, send_sems, recv_sems):
    x, y, c = _my_place()
    me = 2 * x + y
    started, awaited = [], []
    for j, (px, py) in enumerate(_other_chips(x, y)):
        for core in range(2):
            started.append(pltpu.make_async_remote_copy(
                src_ref=src_ref.at[c], dst_ref=land_ref.at[me, c], send_sem=send_sems.at[2 * j + core],
                recv_sem=recv_sems.at[2 * j + c], device_id=(px, py, core), device_id_type=MESH))
            awaited.append(pltpu.make_async_remote_copy(
                src_ref=src_ref.at[core], dst_ref=land_ref.at[2 * px + py, core], send_sem=send_sems.at[2 * j + core],
                recv_sem=recv_sems.at[2 * j + core], device_id=(px, py, core), device_id_type=MESH))
    return started, awaited


def _gather_late_start(packed):
    _, R, L = packed.shape
    me = 2 * lax.axis_index("x") + lax.axis_index("y")
    land = lax.dynamic_update_index_in_dim(jnp.zeros((N_CHIPS, 2, R, L), packed.dtype), packed, me, 0)

    def start(src_ref, land_ref, send_sems, recv_sems):
        for cp in _late_gather_copies(src_ref, land_ref, send_sems, recv_sems)[0]:
            cp.start()

    return _split_start("gather_late_start", 6, packed, land, start)


def _gather_late_wait(handle, after):
    def wait(src_ref, land_ref, send_sems, recv_sems):
        started, awaited = _late_gather_copies(src_ref, land_ref, send_sems, recv_sems)
        for cp in started:
            cp.wait_send()
        for cp in awaited:
            cp.wait_recv()

    return _split_wait("gather_late_wait", handle, after, wait)


def _early_scatter_copies(src_ref, land_ref, send_sems, recv_sems):
    x, y, c = _my_place()
    me = 4 * x + 2 * y + c
    copies = []
    for k in range(1, N_DEV):
        px, py, pc = x ^ (k >> 2), y ^ ((k >> 1) & 1), c ^ (k & 1)
        copies.append(pltpu.make_async_remote_copy(
            src_ref=src_ref.at[2 * px + py, pc], dst_ref=land_ref.at[me], send_sem=send_sems.at[k - 1],
            recv_sem=recv_sems.at[k - 1], device_id=(px, py, pc), device_id_type=MESH))
    return copies


def _scatter_early_start(parts):
    n, _, R, L = parts.shape
    x, y, c = _my_place()
    own = lax.dynamic_index_in_dim(lax.dynamic_index_in_dim(parts, 2 * x + y, 0, keepdims=False), c, 0, keepdims=False)
    land = lax.dynamic_update_index_in_dim(jnp.zeros((N_DEV, R, L), parts.dtype), own, 4 * x + 2 * y + c, 0)

    def start(src_ref, land_ref, send_sems, recv_sems):
        for cp in _early_scatter_copies(src_ref, land_ref, send_sems, recv_sems):
            cp.start()

    return _split_start("scatter_early_start", N_DEV - 1, parts, land, start)


def _scatter_early_wait(handle, after):
    def wait(src_ref, land_ref, send_sems, recv_sems):
        for cp in _early_scatter_copies(src_ref, land_ref, send_sems, recv_sems):
            cp.wait_send()
            cp.wait_recv()

    return _split_wait("scatter_early_wait", handle, after, wait)


def _sum_slots(parts):
    n, R, L = parts.shape
    tr = _tile_rows(R, WIRE_HALF_ALIGN)

    def body(p_ref, o_ref):
        acc = p_ref[0].astype(F32)
        for i in range(1, n):
            acc = acc + p_ref[i].astype(F32)
        o_ref[...] = acc

    return pl.pallas_call(
        body, grid=(R // tr,), in_specs=[pl.BlockSpec((n, tr, L), lambda i: (0, i, 0))],
        out_specs=pl.BlockSpec((tr, L), lambda i: (i, 0)), out_shape=jax.ShapeDtypeStruct((R, L), F32), name="sum_slots",
        compiler_params=_params(("parallel",)),
    )(parts)


def _pair_exchange(parts):
    n, _, R, L = parts.shape

    def body(in_ref, out_ref, send_sems, recv_sems):
        x, y, c = _my_place()
        copies = [pltpu.make_async_remote_copy(src_ref=in_ref.at[s, 1 - c], dst_ref=out_ref.at[s], send_sem=send_sems.at[s],
                                               recv_sem=recv_sems.at[s], device_id=(x, y, 1 - c), device_id_type=MESH)
                  for s in range(n)]
        for cp in copies:
            cp.start()
        for cp in copies:
            cp.wait()

    return pl.pallas_call(
        body, out_shape=jax.ShapeDtypeStruct((n, R, L), parts.dtype), in_specs=[ANY], out_specs=ANY,
        scratch_shapes=[pltpu.SemaphoreType.DMA((n,)), pltpu.SemaphoreType.DMA((n,))], name="pair_exchange",
    )(parts)


def _pair_add(a, b):
    n, R, L = a.shape
    tr = _tile_rows(R, WIRE_HALF_ALIGN)

    def body(a_ref, b_ref, o_ref):
        o_ref[...] = (a_ref[...].astype(F32) + b_ref[...].astype(F32)).astype(o_ref.dtype)

    spec = pl.BlockSpec((n, tr, L), lambda i: (0, i, 0))
    return pl.pallas_call(
        body, grid=(R // tr,), in_specs=[spec, spec], out_specs=spec, out_shape=jax.ShapeDtypeStruct(a.shape, a.dtype),
        name="pair_add", compiler_params=_params(("parallel",)),
    )(a, b)


def _scatter_grads(parts):
    n, R, L = parts.shape

    def body(in_ref, out_ref, send_sems, recv_sems):
        x, y, c = _my_place()
        copies = [pltpu.make_async_remote_copy(src_ref=in_ref.at[2 * px + py], dst_ref=out_ref.at[j], send_sem=send_sems.at[j],
                                               recv_sem=recv_sems.at[j], device_id=(px, py, c), device_id_type=MESH)
                  for j, (px, py) in enumerate(_other_chips(x, y))]
        for cp in copies:
            cp.start()
        for cp in copies:
            cp.wait()

    return pl.pallas_call(
        body, out_shape=jax.ShapeDtypeStruct((n - 1, R, L), parts.dtype), in_specs=[ANY], out_specs=ANY,
        scratch_shapes=[pltpu.SemaphoreType.DMA((3,)), pltpu.SemaphoreType.DMA((3,))], name="scatter_grads",
    )(parts)


def _sum_chips(own, others):
    n, R, L = others.shape
    tr = _tile_rows(R, WIRE_HALF_ALIGN)

    def body(a_ref, p_ref, o_ref):
        acc = a_ref[...].astype(F32)
        for i in range(n):
            acc = acc + p_ref[i].astype(F32)
        o_ref[...] = acc

    return pl.pallas_call(
        body, grid=(R // tr,), in_specs=[pl.BlockSpec((tr, L), lambda i: (i, 0)), pl.BlockSpec((n, tr, L), lambda i: (0, i, 0))],
        out_specs=pl.BlockSpec((tr, L), lambda i: (i, 0)), out_shape=jax.ShapeDtypeStruct((R, L), F32), name="sum_grads",
        compiler_params=_params(("parallel",)),
    )(own, others)


def _tile_rows(R, cap=2048):
    best = 8
    for t in range(8, min(R, cap) + 1, 8):
        if R % t == 0:
            best = t
    return best if R % 8 == 0 else R


def _swap_halves(half):
    def body(in_ref, out_ref, send_sem, recv_sem):
        x, y, c = _my_place()
        cp = pltpu.make_async_remote_copy(src_ref=in_ref, dst_ref=out_ref, send_sem=send_sem, recv_sem=recv_sem,
                                          device_id=(x, y, 1 - c), device_id_type=MESH)
        cp.start()
        cp.wait()

    return pl.pallas_call(
        body, out_shape=jax.ShapeDtypeStruct(half.shape, half.dtype), in_specs=[ANY], out_specs=ANY,
        scratch_shapes=[pltpu.SemaphoreType.DMA, pltpu.SemaphoreType.DMA], name="swap_halves",
    )(half)


def _allreduce_small(v):
    R, L = v.shape

    def body(in_ref, out_ref, buf, send_sems, recv_sems):
        x, y, c = _my_place()
        me = 4 * x + 2 * y + c
        buf[me] = in_ref[...]
        started = []
        for k in range(1, N_DEV):
            to = (x ^ (k >> 2), y ^ ((k >> 1) & 1), c ^ (k & 1))
            cp = pltpu.make_async_remote_copy(src_ref=in_ref, dst_ref=buf.at[me], send_sem=send_sems.at[k - 1],
                                              recv_sem=recv_sems.at[k - 1], device_id=to, device_id_type=MESH)
            cp.start()
            started.append(cp)
        for cp in started:
            cp.wait()
        acc = buf[0]
        for i in range(1, N_DEV):
            acc = acc + buf[i]
        out_ref[...] = acc

    vm = pl.BlockSpec(memory_space=pltpu.VMEM)
    return pl.pallas_call(
        body, out_shape=jax.ShapeDtypeStruct((R, L), F32), in_specs=[vm], out_specs=vm,
        scratch_shapes=[pltpu.VMEM((N_DEV, R, L), F32), pltpu.SemaphoreType.DMA((7,)), pltpu.SemaphoreType.DMA((7,))],
        name="allreduce_small",
    )(v)


def _adamw(name, w, g, m, v):
    shape = w.shape
    C = shape[-1]
    R = int(np.prod(shape[:-1]))
    args = [a.reshape(R, C).astype(F32) for a in (w, g, m, v)]
    if R % 8 == 0 or C % LANES != 0:
        tr, tc = _tile_rows(R, 256), C
    else:
        tr, tc = R, LANES

    def body(w_ref, g_ref, m_ref, v_ref, d_ref, nm_ref, nv_ref):
        g = g_ref[...]
        m = ADAM_B1 * m_ref[...] + (1.0 - ADAM_B1) * g
        v = ADAM_B2 * v_ref[...] + (1.0 - ADAM_B2) * (g * g)
        m_hat = m / (1.0 - ADAM_B1 ** ADAM_STEP)
        v_hat = v / (1.0 - ADAM_B2 ** ADAM_STEP)
        d_ref[...] = -ADAM_LR * (m_hat / (jnp.sqrt(v_hat) + ADAM_EPS) + ADAM_WD * w_ref[...])
        nm_ref[...] = m
        nv_ref[...] = v

    spec = pl.BlockSpec((tr, tc), lambda i, j: (i, j))
    out = pl.pallas_call(
        body, grid=(R // tr, C // tc), in_specs=[spec] * 4, out_specs=[spec] * 3,
        out_shape=[jax.ShapeDtypeStruct((R, C), F32)] * 3, name="adamw_" + name,
        compiler_params=_params(("parallel", "parallel")),
    )(*args)
    return [o.reshape(shape) for o in out]


def kernel(x, mem, pre1_g, post1_g, pre2_g, post2_g, mem_norm_g, w_in, fox_f_bias, rwkv_mu, rwkv_w0, rwkv_w_up, rwkv_a0, rwkv_a_up, rwkv_g_up, rwkv_k_k, rwkv_k_a, rwkv_r_k, rwkv_gn_g, rwkv_gn_b, w_mem_kv, w_fox_out, w_rwkv_out, w_mem_out, w_o, w_ffn_gate, w_ffn_up, w_ffn_down, loss_target, m_pre1_g, m_post1_g, m_pre2_g, m_post2_g, m_mem_norm_g, m_w_in, m_fox_f_bias, m_rwkv_mu, m_rwkv_w0, m_rwkv_w_up, m_rwkv_a0, m_rwkv_a_up, m_rwkv_g_up, m_rwkv_k_k, m_rwkv_k_a, m_rwkv_r_k, m_rwkv_gn_g, m_rwkv_gn_b, m_w_mem_kv, m_w_fox_out, m_w_rwkv_out, m_w_mem_out, m_w_o, m_w_ffn_gate, m_w_ffn_up, m_w_ffn_down, v_pre1_g, v_post1_g, v_pre2_g, v_post2_g, v_mem_norm_g, v_w_in, v_fox_f_bias, v_rwkv_mu, v_rwkv_w0, v_rwkv_w_up, v_rwkv_a0, v_rwkv_a_up, v_rwkv_g_up, v_rwkv_k_k, v_rwkv_k_a, v_rwkv_r_k, v_rwkv_gn_g, v_rwkv_gn_b, v_w_mem_kv, v_w_fox_out, v_w_rwkv_out, v_w_mem_out, v_w_o, v_w_ffn_gate, v_w_ffn_up, v_w_ffn_down):
    given = dict(locals())
    w_loc = {n: given[n] for n in WEIGHTS}
    m_loc = {n: given["m_" + n] for n in WEIGHTS}
    v_loc = {n: given["v_" + n] for n in WEIGHTS}

    shard_shapes = {n: tuple(w_loc[n].shape[1:]) for n in BIG}
    groups = {names: _wire_layout(shard_shapes, names) for names in (FIRST, LATER)}
    core = lax.axis_index("c")

    def pack_weights(names):
        return _wire_pack([_to_wire(n, w_loc[n][0].astype(BF16)) for n in names], groups[names][1])

    def unpack_weights(gathered, names):
        layout, half_rows = groups[names]
        gathered = gathered.reshape(N_CHIPS, 2 * half_rows, WIRE_W)
        out = {}
        for n in names:
            off, rows = layout[n]
            blocks = _from_wire(n, gathered[:, off:off + rows], shard_shapes[n])
            if n in LORA:
                out[n] = blocks.transpose(2, 0, 1).reshape(blocks.shape[2], -1)
            else:
                out[n] = blocks.reshape(-1, blocks.shape[2])
        return out

    def pack_grads(gw, names):
        blocks = []
        for n in names:
            r, c = shard_shapes[n]
            g = gw[n].astype(BF16)
            if n in LORA:
                g = g.reshape(r, N_CHIPS, c).transpose(1, 0, 2)
            elif n in TRANSPOSED:
                g = jnp.swapaxes(g.reshape(N_CHIPS, c, r), 1, 2)
            else:
                g = g.reshape(N_CHIPS, r, c)
            blocks.append(_to_wire(n, g))
        return _wire_pack(blocks, groups[names][1])

    def unpack_grads(half, other, names):
        layout, _ = groups[names]
        reduced = jnp.where(core == 0, jnp.concatenate([half, other]), jnp.concatenate([other, half]))
        out = {}
        for n in names:
            off, rows = layout[n]
            g = _from_wire(n, reduced[off:off + rows], shard_shapes[n])
            out[n] = g.T if n in LORA else g
        return out

    first = _gather_weights(pack_weights(FIRST))
    late = _gather_late_start(pack_weights(LATER) + (first[0, 0, 0, 0] * 0).astype(BF16))
    W = unpack_weights(first, FIRST)
    W.update({n: w_loc[n][0] for n in SMALL})
    W["pre1_g"] = W["pre1_g"] + late[4][0, 0]
    early = []

    def late_weights(after):
        return unpack_weights(_gather_late_wait(late, after), LATER)

    def early_grads(gw, thru):
        early.append(_scatter_early_start(pack_grads(gw, LATER)))
        return thru + early[0][4][0, 0].astype(thru.dtype)

    loss, grad_x, gw = _layer_step(x, mem, loss_target, W, late_weights, early_grads)

    packed = pack_grads(gw, FIRST)
    own_halves = lax.dynamic_index_in_dim(packed, core, axis=1, keepdims=False)
    chip_sums = _pair_add(own_halves, _pair_exchange(packed))
    own_chip = lax.dynamic_index_in_dim(chip_sums, 2 * lax.axis_index("x") + lax.axis_index("y"), axis=0, keepdims=False)
    half_first = _sum_chips(own_chip, _scatter_grads(chip_sums))
    half_later = _sum_slots(_scatter_early_wait(early[0], half_first))
    rows_first = groups[FIRST][1]
    other = _swap_halves(jnp.concatenate([half_first, half_later]))
    g_shard = {**unpack_grads(half_first, other[:rows_first], FIRST), **unpack_grads(half_later, other[rows_first:], LATER)}

    small_shapes = [w_loc[n].shape[1:] for n in SMALL] + [(1,)]
    n_small = sum(int(np.prod(s)) for s in small_shapes)
    small_rows = -(-n_small // (8 * LANES)) * 8
    flat = jnp.concatenate([gw[n].reshape(-1) for n in SMALL] + [loss.reshape(1)])
    flat = jnp.pad(flat, (0, small_rows * LANES - n_small)).reshape(small_rows, LANES).reshape(-1)
    small, off = [], 0
    flat = _allreduce_small(flat.reshape(small_rows, LANES)).reshape(-1)
    for s in small_shapes:
        cnt = int(np.prod(s))
        small.append(flat[off:off + cnt].reshape(s))
        off += cnt
    g_small = dict(zip(SMALL, small[:-1]))
    loss = small[-1][0]

    grads, deltas, new_m, new_v = [], [], [], []
    flip = lambda a: jnp.swapaxes(a, -1, -2)
    for n in WEIGHTS:
        if n in TRANSPOSED:
            g_t = g_shard[n][None]
            d, nm, nv = (flip(o) for o in _adamw(n, flip(w_loc[n]), g_t, flip(m_loc[n]), flip(v_loc[n])))
            g = flip(g_t)
        else:
            g = (g_shard[n] if n in g_shard else g_small[n]).reshape(w_loc[n].shape)
            d, nm, nv = _adamw(n, w_loc[n], g, m_loc[n], v_loc[n])
        grads.append(g)
        deltas.append(d)
        new_m.append(nm)
        new_v.append(nv)
    return (loss, grad_x, *grads, *deltas, *new_m, *new_v)
```

```python
import numpy as np
import jax
import jax.numpy as jnp
from jax import lax
from jax.experimental import pallas as pl
from jax.experimental.pallas import tpu as pltpu

F32, BF16 = jnp.float32, jnp.bfloat16
MESH = pl.DeviceIdType.MESH

D_MODEL = 1024
HEAD_DIM = 64
N_HEADS = 8
BR_W = 512
MEM_HEADS = 4
MEM_HEAD_DIM = 128
D_FF = 2816
NORM_EPS = 1e-6
GN_EPS = 64e-5
N_CHIPS = 4
N_DEV = 8
LANES = 128
VMEM_LIMIT = 48 * 1024 * 1024

ADAM_LR, ADAM_B1, ADAM_B2, ADAM_EPS, ADAM_WD, ADAM_STEP = 0.001, 0.9, 0.999, 1e-08, 0.01, 10

FOX_COLS = 3 * BR_W + N_HEADS
RWKV_COLS = 3 * BR_W + 64 + 64 + 128
COL_QKV = (0, 3 * BR_W)
COL_F = (3 * BR_W, FOX_COLS)
COL_RW = (FOX_COLS, FOX_COLS + RWKV_COLS)
COL_MQ = (COL_RW[1], COL_RW[1] + BR_W)
COL_GATE = (COL_MQ[1], COL_MQ[1] + 3 * D_MODEL)

NT_DIMS = (((1,), (1,)), ((), ()))
TN_DIMS = (((0,), (0,)), ((), ()))


def _params(sem=None, **kw):
    return pltpu.CompilerParams(dimension_semantics=sem, vmem_limit_bytes=VMEM_LIMIT, **kw)


def _sigmoid(x):
    return 1.0 / (1.0 + jnp.exp(-x))


def _log_sigmoid(x):
    return jnp.minimum(x, 0.0) - jnp.log(1.0 + jnp.exp(-jnp.abs(x)))


def _bdot(a, b, dims=None):
    a, b = a.astype(BF16), b.astype(BF16)
    if dims is None:
        return jnp.dot(a, b, preferred_element_type=F32)
    return lax.dot_general(a, b, dims, preferred_element_type=F32)


def _hdot(a, ones):
    ones = ones.astype(BF16)
    hi = a.astype(BF16)
    r1 = a - hi.astype(F32)
    mid = r1.astype(BF16)
    lo = (r1 - mid.astype(F32)).astype(BF16)
    dot = lambda x: jnp.dot(x, ones, preferred_element_type=F32)
    return dot(hi) + dot(mid) + dot(lo)


def _tile(n, cap):
    best = None
    for t in range(LANES, min(n, cap) + 1, LANES):
        if n % t == 0:
            best = t
    return best or n


def _rowwise(name, fn, rows, consts, outs, accs=(), tm=256):
    T = rows[0].shape[0]
    tm = min(tm, T)
    assert T % tm == 0
    nr, nc, no, na = len(rows), len(consts), len(outs), len(accs)

    def body(*refs):
        res = fn(*[r[...].astype(F32) for r in refs[:nr + nc]])
        if not isinstance(res, (tuple, list)):
            res = (res,)
        orefs, arefs = refs[nr + nc:nr + nc + no], refs[nr + nc + no:]
        for ref, val in zip(orefs, res[:no]):
            ref[...] = val.astype(ref.dtype)
        if na:
            @pl.when(pl.program_id(0) == 0)
            def _():
                for ref in arefs:
                    ref[...] = jnp.zeros(ref.shape, ref.dtype)
            for ref, val in zip(arefs, res[no:]):
                ref[...] += val

    in_specs = ([pl.BlockSpec((tm, r.shape[1]), lambda i: (i, 0)) for r in rows]
                + [pl.BlockSpec(c.shape, lambda i: (0, 0)) for c in consts])
    out_specs = ([pl.BlockSpec((tm, w), lambda i: (i, 0)) for w, _ in outs]
                 + [pl.BlockSpec(s, lambda i: (0, 0)) for s, _ in accs])
    out_shape = ([jax.ShapeDtypeStruct((T, w), dt) for w, dt in outs]
                 + [jax.ShapeDtypeStruct(s, dt) for s, dt in accs])
    return pl.pallas_call(
        body, grid=(T // tm,), in_specs=in_specs, out_specs=out_specs, out_shape=out_shape, name=name,
        compiler_params=_params(("arbitrary",) if na else ("parallel",)),
    )(*rows, *consts)


MM_TILE_CAP = 1408
MM_WHOLE_K = 2048


def _mm(name, a, b, ta=False, tb=False, out_dtype=F32, add=None):
    M, K = (a.shape[1], a.shape[0]) if ta else a.shape
    K2, N = (b.shape[1], b.shape[0]) if tb else b.shape
    assert K == K2
    tm, tn = _tile(M, MM_TILE_CAP), _tile(N, MM_TILE_CAP)
    tk = K if K <= MM_WHOLE_K else _tile(K, MM_TILE_CAP)
    assert M % tm == 0 and N % tn == 0 and K % tk == 0
    nk = K // tk
    a_dim, b_dim = (0 if ta else 1), (1 if tb else 0)

    def body(*refs):
        a_ref, b_ref = refs[0], refs[1]
        n_in = 2 if add is None else 3
        o_ref, acc = refs[n_in], (refs[n_in + 1] if nk > 1 else None)
        k = pl.program_id(2)
        part = lax.dot_general(a_ref[...].astype(BF16), b_ref[...].astype(BF16),
                               (((a_dim,), (b_dim,)), ((), ())), preferred_element_type=F32)

        def finish(r):
            if add is not None:
                r = r + refs[2][...].astype(F32)
            o_ref[...] = r.astype(o_ref.dtype)

        if nk == 1:
            finish(part)
            return

        @pl.when(k == 0)
        def _():
            acc[...] = part

        @pl.when(k > 0)
        def _():
            acc[...] += part

        @pl.when(k == nk - 1)
        def _():
            finish(acc[...])

    a_spec = pl.BlockSpec((tk, tm), lambda i, j, k: (k, i)) if ta else pl.BlockSpec((tm, tk), lambda i, j, k: (i, k))
    b_spec = pl.BlockSpec((tn, tk), lambda i, j, k: (j, k)) if tb else pl.BlockSpec((tk, tn), lambda i, j, k: (k, j))
    o_spec = pl.BlockSpec((tm, tn), lambda i, j, k: (i, j))
    ins, in_specs = [a, b], [a_spec, b_spec]
    if add is not None:
        ins.append(add)
        in_specs.append(o_spec)
    return pl.pallas_call(
        body, grid=(M // tm, N // tn, nk), in_specs=in_specs, out_specs=o_spec,
        out_shape=jax.ShapeDtypeStruct((M, N), out_dtype), scratch_shapes=[pltpu.VMEM((tm, tn), F32)] if nk > 1 else [],
        name=name, compiler_params=_params(("parallel", "parallel", "arbitrary")),
    )(*ins)


def _rowsum(x):
    return jnp.sum(x, axis=0, keepdims=True)


def _rms_stat(x):
    return lax.rsqrt(jnp.mean(x * x, axis=-1, keepdims=True) + NORM_EPS)


def _rms_bwd(dy, x, g):
    r = _rms_stat(x)
    xn = x * r
    dxn = dy * g
    dx = r * (dxn - xn * jnp.mean(dxn * xn, axis=-1, keepdims=True))
    return dx, _rowsum(dy * xn)


def _fox_c_fwd(f8t, bias_col, tc=256):
    B, H, S = f8t.shape
    tc = min(tc, S)

    def body(f_ref, b_ref, c_ref, carry):
        @pl.when(pl.program_id(1) == 0)
        def _():
            carry[...] = jnp.zeros(carry.shape, F32)
        lf = _log_sigmoid(f_ref[...] + b_ref[...])
        row = lax.broadcasted_iota(jnp.int32, (tc, tc), 0)
        col = lax.broadcasted_iota(jnp.int32, (tc, tc), 1)
        c = _hdot(lf, (row <= col).astype(F32)) + carry[...]
        c_ref[...] = c
        carry[...] = c[:, tc - 1:tc]

    return pl.pallas_call(
        body, grid=(B, S // tc),
        in_specs=[pl.BlockSpec((None, H, tc), lambda b, i: (b, 0, i)), pl.BlockSpec((H, 1), lambda b, i: (0, 0))],
        out_specs=pl.BlockSpec((None, H, tc), lambda b, i: (b, 0, i)),
        out_shape=jax.ShapeDtypeStruct((B, H, S), F32), scratch_shapes=[pltpu.VMEM((H, 1), F32)], name="fox_c_fwd",
        compiler_params=_params(("parallel", "arbitrary")),
    )(f8t, bias_col)


def _fox_c_bwd(dc, f8t, bias_col, tc=256):
    B, H, S = f8t.shape
    tc = min(tc, S)
    n = S // tc

    def body(dc_ref, f_ref, b_ref, df_ref, db_ref, carry):
        @pl.when(pl.program_id(1) == 0)
        def _():
            carry[...] = jnp.zeros(carry.shape, F32)
            db_ref[...] = jnp.zeros(db_ref.shape, F32)
        row = lax.broadcasted_iota(jnp.int32, (tc, tc), 0)
        col = lax.broadcasted_iota(jnp.int32, (tc, tc), 1)
        dlf = _hdot(dc_ref[...], (row >= col).astype(F32)) + carry[...]
        z = f_ref[...] + b_ref[...]
        df = dlf * (1.0 - _sigmoid(z))
        df_ref[...] = df
        db_ref[...] += jnp.sum(df, axis=1, keepdims=True)
        carry[...] = dlf[:, 0:1]

    rev = lambda b, i: (b, 0, n - 1 - i)
    return pl.pallas_call(
        body, grid=(B, n),
        in_specs=[pl.BlockSpec((None, H, tc), rev), pl.BlockSpec((None, H, tc), rev), pl.BlockSpec((H, 1), lambda b, i: (0, 0))],
        out_specs=[pl.BlockSpec((None, H, tc), rev), pl.BlockSpec((None, H, 1), lambda b, i: (b, 0, 0))],
        out_shape=[jax.ShapeDtypeStruct((B, H, S), F32), jax.ShapeDtypeStruct((B, H, 1), F32)],
        scratch_shapes=[pltpu.VMEM((H, 1), F32)], name="fox_c_bwd",
        compiler_params=_params(("parallel", "arbitrary")),
    )(dc, f8t, bias_col)


NEG_BIG = -1e30


def _fox_logits(q, kj, cq, ckj, i, j, tq, tk, scale):
    s = _bdot(q, kj, NT_DIMS) * scale + (cq - ckj)
    row = lax.broadcasted_iota(jnp.int32, (tq, tk), 0)
    col = lax.broadcasted_iota(jnp.int32, (tq, tk), 1)
    return s, col <= row + (i * tq - j * tk)


def _fox_key_blocks(i, tq, tk):
    return (i * tq + tq - 1) // tk + 1


def _fox_fwd(qkv, cq, ck, B):
    T = qkv.shape[0]
    S = T // B
    (nq, tq), (nk, tk) = cq.shape[1:3], (ck.shape[1], ck.shape[3])
    scale = HEAD_DIM ** -0.5

    def body(q_ref, k_ref, v_ref, cq_ref, ck_ref, o_ref, lse_ref):
        i = pl.program_id(1)
        lo = lax.broadcasted_iota(jnp.int32, (tq, LANES), 1) < HEAD_DIM
        q = q_ref[...]
        qh = (jnp.where(lo, q, 0), jnp.where(lo, 0, q))

        def step(j, carry):
            rows = pl.ds(pl.multiple_of(j * tk, tk), tk)
            kj, vj = k_ref[rows, :], v_ref[rows, :]
            new = []
            for h in range(2):
                m, l, acc = carry[h]
                s, ok = _fox_logits(qh[h], kj, cq_ref[h], ck_ref[h, j], i, j, tq, tk, scale)
                s = jnp.where(ok, s, NEG_BIG)
                m2 = jnp.maximum(m, jnp.max(s, axis=1, keepdims=True))
                p = jnp.exp(s - m2)
                al = jnp.exp(m - m2)
                new.append((m2, al * l + jnp.sum(p, axis=1, keepdims=True), al * acc + _bdot(p, vj)))
            return tuple(new)

        init = tuple((jnp.full((tq, 1), NEG_BIG, F32), jnp.zeros((tq, 1), F32), jnp.zeros((tq, LANES), F32)) for _ in range(2))
        (m0, l0, a0), (m1, l1, a1) = lax.fori_loop(0, _fox_key_blocks(i, tq, tk), step, init)
        o_ref[...] = jnp.where(lo, a0 / l0, a1 / l1).astype(o_ref.dtype)
        lse_ref[0] = m0 + jnp.log(l0)
        lse_ref[1] = m1 + jnp.log(l1)

    seq = lambda col0: pl.BlockSpec((S, LANES), lambda g, i: (g // 4, col0 + g % 4))
    blk = lambda col0: pl.BlockSpec((tq, LANES), lambda g, i: ((g // 4) * nq + i, col0 + g % 4))
    col = pl.BlockSpec((2, None, tq, 1), lambda g, i: (g, i, 0, 0))
    return pl.pallas_call(
        body, grid=(B * 4, nq), in_specs=[blk(0), seq(4), seq(8), col, pl.BlockSpec((2, nk, 1, tk), lambda g, i: (g, 0, 0, 0))],
        out_specs=[blk(0), col],
        out_shape=[jax.ShapeDtypeStruct((T, BR_W), BF16), jax.ShapeDtypeStruct(cq.shape, F32)], name="fox_fwd",
        compiler_params=_params(("parallel", "parallel")),
    )(qkv, qkv, qkv, cq, ck)


def _fox_bwd(qkv, o, do, cq, ck, lse, B):
    T = qkv.shape[0]
    S = T // B
    (nq, tq), (nk, tk) = cq.shape[1:3], (ck.shape[1], ck.shape[3])
    scale = HEAD_DIM ** -0.5

    def body(q_ref, k_ref, v_ref, o_ref, do_ref, cq_ref, ck_ref, lse_ref, dq_ref, dk_ref, dv_ref, dck_ref, dcq_ref,
             dk_acc, dv_acc):
        dk_acc[...] = jnp.zeros(dk_acc.shape, F32)
        dv_acc[...] = jnp.zeros(dv_acc.shape, F32)
        dck_ref[...] = jnp.zeros(dck_ref.shape, F32)
        lo = lax.broadcasted_iota(jnp.int32, (tq, LANES), 1) < HEAD_DIM

        def qloop(i, _):
            qrows = pl.ds(pl.multiple_of(i * tq, tq), tq)
            q, do_i, o_i = q_ref[qrows, :], do_ref[qrows, :], o_ref[qrows, :].astype(F32)
            qh = (jnp.where(lo, q, 0), jnp.where(lo, 0, q))
            doh = (jnp.where(lo, do_i, 0), jnp.where(lo, 0, do_i))
            delta = [jnp.sum(doh[h].astype(F32) * o_i, axis=1, keepdims=True) for h in range(2)]

            def kloop(j, carry):
                krows = pl.ds(pl.multiple_of(j * tk, tk), tk)
                kj, vj = k_ref[krows, :], v_ref[krows, :]
                new = []
                for h in range(2):
                    dq, dcq = carry[h]
                    s, ok = _fox_logits(qh[h], kj, cq_ref[h, i], ck_ref[h, j], i, j, tq, tk, scale)
                    p = jnp.where(ok, jnp.exp(s - lse_ref[h, i]), 0.0)
                    ds = p * (_bdot(doh[h], vj, NT_DIMS) - delta[h])
                    dv_acc[krows, :] += _bdot(p, doh[h], TN_DIMS)
                    dk_acc[krows, :] += _bdot(ds, qh[h], TN_DIMS) * scale
                    dck_ref[h, j] += -_rowsum(ds)
                    new.append((dq + _bdot(ds, kj) * scale, dcq + jnp.sum(ds, axis=1, keepdims=True)))
                return tuple(new)

            init = tuple((jnp.zeros((tq, LANES), F32), jnp.zeros((tq, 1), F32)) for _ in range(2))
            (dq0, dcq0), (dq1, dcq1) = lax.fori_loop(0, _fox_key_blocks(i, tq, tk), kloop, init)
            dq_ref[qrows, :] = jnp.where(lo, dq0, dq1).astype(dq_ref.dtype)
            dcq_ref[0, i] = dcq0
            dcq_ref[1, i] = dcq1
            return 0

        lax.fori_loop(0, nq, qloop, 0)
        dk_ref[...] = dk_acc[...].astype(dk_ref.dtype)
        dv_ref[...] = dv_acc[...].astype(dv_ref.dtype)

    seq = lambda col0: pl.BlockSpec((S, LANES), lambda g: (g // 4, col0 + g % 4))
    col = pl.BlockSpec((2, nq, tq, 1), lambda g: (g, 0, 0, 0))
    row = pl.BlockSpec((2, nk, 1, tk), lambda g: (g, 0, 0, 0))
    out = jax.ShapeDtypeStruct((T, BR_W), BF16)
    return pl.pallas_call(
        body, grid=(B * 4,), in_specs=[seq(0), seq(4), seq(8), seq(0), seq(0), col, row, col],
        out_specs=[seq(0), seq(0), seq(0), row, col],
        out_shape=[out, out, out, jax.ShapeDtypeStruct(ck.shape, F32), jax.ShapeDtypeStruct(cq.shape, F32)],
        scratch_shapes=[pltpu.VMEM((S, LANES), F32), pltpu.VMEM((S, LANES), F32)], name="fox_bwd",
        compiler_params=_params(("parallel",)),
    )(qkv, qkv, qkv, o, do, cq, ck, lse)


def _mem_probs(qh, kh):
    s = _bdot(qh, kh, NT_DIMS) * (MEM_HEAD_DIM ** -0.5)
    e = jnp.exp(s - jnp.max(s, axis=1, keepdims=True))
    return e / jnp.sum(e, axis=1, keepdims=True)


def _mem_fwd(q, mem_kv, B, tq=512):
    T = q.shape[0]
    S, Lm = T // B, mem_kv.shape[0] // B
    tq = min(tq, S)
    n = S // tq

    def body(q_ref, k_ref, v_ref, o_ref):
        for h in range(MEM_HEADS):
            sl = slice(h * MEM_HEAD_DIM, (h + 1) * MEM_HEAD_DIM)
            p = _mem_probs(q_ref[:, sl], k_ref[:, sl])
            o_ref[:, sl] = _bdot(p, v_ref[:, sl]).astype(o_ref.dtype)

    qs = pl.BlockSpec((tq, BR_W), lambda b, i: (b * n + i, 0))
    return pl.pallas_call(
        body, grid=(B, n),
        in_specs=[qs, pl.BlockSpec((Lm, BR_W), lambda b, i: (b, 0)), pl.BlockSpec((Lm, BR_W), lambda b, i: (b, 1))],
        out_specs=qs, out_shape=jax.ShapeDtypeStruct((T, BR_W), BF16), name="mem_fwd",
        compiler_params=_params(("parallel", "parallel")),
    )(q, mem_kv, mem_kv)


def _mem_bwd(q, mem_kv, do, B, tq=512):
    T = q.shape[0]
    S, Lm = T // B, mem_kv.shape[0] // B
    tq = min(tq, S)
    n = S // tq
    scale = MEM_HEAD_DIM ** -0.5

    def body(q_ref, k_ref, v_ref, do_ref, dq_ref, dk_ref, dv_ref):
        @pl.when(pl.program_id(1) == 0)
        def _():
            dk_ref[...] = jnp.zeros(dk_ref.shape, F32)
            dv_ref[...] = jnp.zeros(dv_ref.shape, F32)
        for h in range(MEM_HEADS):
            sl = slice(h * MEM_HEAD_DIM, (h + 1) * MEM_HEAD_DIM)
            qh, kh, vh, doh = q_ref[:, sl], k_ref[:, sl], v_ref[:, sl], do_ref[:, sl]
            p = _mem_probs(qh, kh)
            dp = _bdot(doh, vh, NT_DIMS)
            ds = p * (dp - jnp.sum(p * dp, axis=1, keepdims=True))
            dq_ref[:, sl] = (_bdot(ds, kh) * scale).astype(dq_ref.dtype)
            dk_ref[:, sl] += _bdot(ds, qh, TN_DIMS) * scale
            dv_ref[:, sl] += _bdot(p, doh, TN_DIMS)

    qs = pl.BlockSpec((tq, BR_W), lambda b, i: (b * n + i, 0))
    kv = pl.BlockSpec((Lm, BR_W), lambda b, i: (b, 0))
    return pl.pallas_call(
        body, grid=(B, n),
        in_specs=[qs, kv, pl.BlockSpec((Lm, BR_W), lambda b, i: (b, 1)), qs], out_specs=[qs, kv, kv],
        out_shape=[jax.ShapeDtypeStruct((T, BR_W), BF16), jax.ShapeDtypeStruct((B * Lm, BR_W), F32),
                   jax.ShapeDtypeStruct((B * Lm, BR_W), F32)], name="mem_bwd",
        compiler_params=_params(("parallel", "arbitrary")),
    )(q, mem_kv, mem_kv, do)


def _head_ones():
    h = np.arange(BR_W) // HEAD_DIM
    return jnp.asarray((h[:, None] == h[None, :]).astype(np.float32))


def _rw_prep(p, pp, mu, w0, w1, a0, w2, g_up, k_k, k_a, bd):
    ps = p + (pp - p) * mu
    r, k, v = ps[:, 0:512], ps[:, 512:1024], ps[:, 1024:1536]
    wa, gd = ps[:, 1536:1664], ps[:, 1664:1792]
    th = jnp.tanh(wa)
    z = w0 + _bdot(th, w1)
    wl = -jnp.exp(_log_sigmoid(z) - 0.5)
    w = jnp.exp(wl)
    a = _sigmoid(a0 + _bdot(wa, w2))
    sg = _sigmoid(gd)
    g = _bdot(sg, g_up)
    kq = k * k_k
    n2 = _hdot(kq * kq, bd)
    inv = lax.rsqrt(jnp.maximum(n2, 1e-24))
    kk = kq * inv
    k2 = k * (1.0 + (a - 1.0) * k_a)
    return dict(ps=ps, r=r, k=k, v=v, wa=wa, th=th, z=z, wl=wl, w=w, a=a, sg=sg, g=g, kq=kq, n2=n2, inv=inv, kk=kk, k2=k2)


def _keycol_selector(tm, Tc):
    e = np.zeros((tm, (tm // Tc) * LANES), np.float32)
    for t in range(tm):
        c, tl = divmod(t, Tc)
        e[t, c * LANES + tl] = e[t, c * LANES + Tc + tl] = 1.0
    return jnp.asarray(e, BF16)


def _rw_prep_fwd(p, pp, consts, B, Tc, tm=256):
    assert 2 * Tc == LANES
    T = p.shape[0]
    S = T // B
    nb, cpb = S // tm, tm // Tc
    sel = _keycol_selector(tm, Tc)
    nc = len(consts)

    def body(*refs):
        t = _rw_prep(*[r[...] for r in refs[:2 + nc]])
        sel_ref = refs[2 + nc]
        rows, cols = refs[3 + nc:7 + nc], refs[7 + nc:]
        for ref, val in zip(rows, (t["r"], t["k2"], t["v"], t["g"])):
            ref[...] = val
        lo = lax.broadcasted_iota(jnp.int32, (HEAD_DIM, LANES), 1) < HEAD_DIM
        operands = (t["w"], -t["kk"], t["kk"] * t["a"], t["k2"], t["r"])
        for n, (ref, x) in enumerate(zip(cols, operands)):
            terms = _split_bf16(x) if ref.dtype == F32 else (x.astype(BF16),)
            for hp in range(4):
                xt = sum(lax.dot_general(tt[:, hp * LANES:(hp + 1) * LANES], sel_ref[...], TN_DIMS,
                                         preferred_element_type=F32) for tt in terms)
                for c in range(cpb):
                    blk = xt[:, c * LANES:(c + 1) * LANES]
                    ref[hp, c] = jnp.where(lo, blk[0:HEAD_DIM], blk[HEAD_DIM:2 * HEAD_DIM]).astype(ref.dtype)

    row_spec = lambda w: pl.BlockSpec((tm, w), lambda i: (i, 0))
    col_spec = pl.BlockSpec((None, 4, cpb, HEAD_DIM, LANES), lambda i: (i // nb, 0, i % nb, 0, 0))
    col_shape = lambda dt: jax.ShapeDtypeStruct((B, 4, S // Tc, HEAD_DIM, LANES), dt)
    out = pl.pallas_call(
        body, grid=(T // tm,),
        in_specs=[row_spec(RWKV_COLS)] * 2 + [pl.BlockSpec(c.shape, lambda i: (0, 0)) for c in consts]
        + [pl.BlockSpec(sel.shape, lambda i: (0, 0))],
        out_specs=[row_spec(BR_W)] * 4 + [col_spec] * 5,
        out_shape=[jax.ShapeDtypeStruct((T, BR_W), F32)] * 4 + [col_shape(F32)] + [col_shape(BF16)] * 3 + [col_shape(F32)],
        name="rwkv_prep_fwd", compiler_params=_params(("parallel",)),
    )(p, pp, *consts, sel)
    return out[:4], [c.reshape(B * 4, S // Tc, HEAD_DIM, LANES) for c in out[4:]]


def _rw_prep_bwd(p, pp, cots, consts):
    def fn(p, pp, dr1, dr2, dw, dk21, dk22, dv1, dv2, dav, dbv, dg, mu, w0, w1, a0, w2, g_up, k_k, k_a, bd):
        t = _rw_prep(p, pp, mu, w0, w1, a0, w2, g_up, k_k, k_a, bd)
        dr, dk2, dv = dr1 + dr2, dk21 + dk22, dv1 + dv2
        a, k, kk, kq, inv = t["a"], t["k"], t["kk"], t["kq"], t["inv"]
        dkk = dbv * a - dav
        da = dbv * kk + dk2 * k * k_a
        dk = dk2 * (1.0 + (a - 1.0) * k_a)
        d_k_a = _rowsum(dk2 * k * (a - 1.0))
        proj = _hdot(dkk * kq, bd)
        dkq = dkk * inv - jnp.where(t["n2"] > 1e-24, kq * inv * inv * inv * proj, 0.0)
        dk = dk + dkq * k_k
        d_k_k = _rowsum(dkq * k)
        dpa = da * a * (1.0 - a)
        d_a0 = _rowsum(dpa)
        dwa = _bdot(dpa, w2, NT_DIMS)
        d_w2 = _bdot(t["wa"], dpa, TN_DIMS)
        dz = dw * t["w"] * t["wl"] * (1.0 - _sigmoid(t["z"]))
        d_w0 = _rowsum(dz)
        th = t["th"]
        dwa = dwa + _bdot(dz, w1, NT_DIMS) * (1.0 - th * th)
        d_w1 = _bdot(th, dz, TN_DIMS)
        sg = t["sg"]
        dgd = _bdot(dg, g_up, NT_DIMS) * sg * (1.0 - sg)
        d_g_up = _bdot(sg, dg, TN_DIMS)
        dps = jnp.concatenate([dr, dk, dv, dwa, dgd], axis=1)
        d_mu = _rowsum(dps * (pp - p))
        return dps * (1.0 - mu), dps * mu, d_mu, d_w0, d_w1, d_a0, d_w2, d_g_up, d_k_k, d_k_a

    accs = [((1, RWKV_COLS), F32), ((1, BR_W), F32), ((LANES, BR_W), F32), ((1, BR_W), F32), ((LANES, BR_W), F32),
            ((LANES, BR_W), F32), ((1, BR_W), F32), ((1, BR_W), F32)]
    return _rowwise("rwkv_prep_bwd", fn, [p, pp] + list(cots), consts, [(RWKV_COLS, BF16)] * 2, accs, tm=256)


def _rw_head(y, r, k2, v, g, gn_g, gn_b, r_k, bd):
    mean = _hdot(y, bd) * (1.0 / HEAD_DIM)
    yc = y - mean
    rs = lax.rsqrt(_hdot(yc * yc, bd) * (1.0 / HEAD_DIM) + GN_EPS)
    yn = yc * rs
    bs = _hdot(r * k2 * r_k, bd)
    return yn, rs, bs, yn * gn_g + gn_b + bs * v


def _rw_head_fwd(y, r, k2, v, g, consts):
    def fn(y, r, k2, v, g, *c):
        return _rw_head(y, r, k2, v, g, *c)[3] * g
    return _rowwise("rwkv_head_fwd", fn, [y, r, k2, v, g], consts, [(BR_W, BF16)])[0]


def _rw_head_bwd(dout, y, r, k2, v, g, consts):
    def fn(dout, y, r, k2, v, g, gn_g, gn_b, r_k, bd):
        dout = dout.astype(F32)
        yn, rs, bs, zz = _rw_head(y, r, k2, v, g, gn_g, gn_b, r_k, bd)
        dg = dout * zz
        dz = dout * g
        dyn = dz * gn_g
        inv_n = 1.0 / HEAD_DIM
        dy = rs * (dyn - _hdot(dyn, bd) * inv_n - yn * (_hdot(dyn * yn, bd) * inv_n))
        dq = _hdot(dz * v, bd)
        return dy, dg, dq * k2 * r_k, dq * r * r_k, dz * bs, _rowsum(dz * yn), _rowsum(dz), _rowsum(dq * r * k2)
    return _rowwise("rwkv_head_bwd", fn, [dout, y, r, k2, v, g], consts, [(BR_W, F32)] * 5, [((1, BR_W), F32)] * 3)


SCAN_TC = 64


def _scan_onehot(Tc):
    w = np.zeros((Tc // 2, 2 * Tc, 2 * LANES), np.float32)
    for tt in range(Tc // 2):
        for u in range(2):
            for h in range(2):
                w[tt, h * Tc + 2 * tt + u, u * LANES + h * HEAD_DIM: u * LANES + (h + 1) * HEAD_DIM] = 1.0
    return jnp.asarray(w, BF16)


def _split_bf16(x):
    hi = x.astype(BF16)
    return hi, (x - hi.astype(F32)).astype(BF16)


def _key_tiles(l_w, others, onehot):
    dot = lambda x: jnp.dot(x, onehot, preferred_element_type=F32)
    whi, wmid = l_w
    return [dot(whi) + dot(wmid)] + [dot(o) for o in others]


def _rw_scan_fwd(LW, LA, LB, LK, LR, v, P=4):
    NP, nc, _, Tc2 = LW.shape
    Tc = Tc2 // 2
    S = nc * Tc
    onehot = _scan_onehot(Tc)
    npb = 4 // P

    def body(lw, la, lb, lk, lr, v_ref, oh_ref, y_ref, sa_ref, sb_ref, st):
        @pl.when(pl.program_id(1) == 0)
        def _():
            st[...] = jnp.zeros(st.shape, F32)
        s = [st[p] for p in range(P)]
        cols = [(_split_bf16(lw[p]), [ref[p].astype(BF16) for ref in (la, lb, lk)]) for p in range(P)]
        r_cols = [lr[p].astype(F32) for p in range(P)]
        head0 = lax.broadcasted_iota(jnp.int32, (HEAD_DIM, LANES), 1) < HEAD_DIM
        for tt in range(Tc // 2):
            tiles = [_key_tiles(c[0], c[1], oh_ref[tt]) for c in cols]
            for u in range(2):
                t = 2 * tt + u
                for p in range(P):
                    W, A, Bt, Kt = (x[:, u * LANES:(u + 1) * LANES] for x in tiles[p])
                    R = jnp.where(head0, r_cols[p][:, t:t + 1], r_cols[p][:, Tc + t:Tc + t + 1])
                    ls = slice(p * LANES, (p + 1) * LANES)
                    sb_ref[p, t] = s[p]
                    sa = _rowsum(s[p] * A)
                    s[p] = s[p] * W + Bt * sa + Kt * v_ref[t:t + 1, ls]
                    y_ref[t:t + 1, ls] = _rowsum(s[p] * R)
                    sa_ref[t:t + 1, ls] = sa
        for p in range(P):
            st[p] = s[p]

    lspec = pl.BlockSpec((P, None, HEAD_DIM, Tc2), lambda g, c: (g, c, 0, 0))
    rows = pl.BlockSpec((Tc, P * LANES), lambda g, c: ((g // npb) * nc + c, g % npb))
    rowshape = jax.ShapeDtypeStruct(v.shape, F32)
    return pl.pallas_call(
        body, grid=(NP // P, nc), in_specs=[lspec] * 5 + [rows, pl.BlockSpec(onehot.shape, lambda g, c: (0, 0, 0))],
        out_specs=[rows, rows, pl.BlockSpec((P, Tc, HEAD_DIM, LANES), lambda g, c: (g, c, 0, 0))],
        out_shape=[rowshape, rowshape, jax.ShapeDtypeStruct((NP, S, HEAD_DIM, LANES), F32)],
        scratch_shapes=[pltpu.VMEM((P, HEAD_DIM, LANES), F32)], name="rwkv_scan_fwd",
        compiler_params=_params(("parallel", "arbitrary")),
    )(LW, LA, LB, LK, LR, v, onehot)


SCAN_G_ROWS = 16


def _rw_scan_bwd(LW, LA, LB, LK, LR, v, sa, dy, sb, P=4):
    NP, nc, _, Tc2 = LW.shape
    Tc = Tc2 // 2
    onehot = _scan_onehot(Tc)
    npb = 4 // P

    def body(lw, la, lb, lk, lr, v_ref, sa_ref, dy_ref, sb_ref, oh_ref, dv_ref, dk_ref, db_ref, dw_ref, dr_ref, da_ref, dst):
        @pl.when(pl.program_id(1) == 0)
        def _():
            dst[...] = jnp.zeros(dst.shape, F32)
        rid = lax.broadcasted_iota(jnp.int32, (SCAN_G_ROWS, LANES), 0)
        lane = lax.broadcasted_iota(jnp.int32, (SCAN_G_ROWS, LANES), 1)
        own = (((rid % 2) == 0) == (lane < HEAD_DIM)) & (rid < 10)
        lo = lane[0:1] < HEAD_DIM
        nt = lambda rows, tile: lax.dot_general(rows.astype(BF16), tile.astype(BF16), NT_DIMS, preferred_element_type=F32)
        ds = [dst[p] for p in range(P)]
        cols = [(_split_bf16(lw[p]), [ref[p].astype(BF16) for ref in (la, lb, lk)]) for p in range(P)]
        r_cols = [lr[p].astype(F32) for p in range(P)]
        head0 = lax.broadcasted_iota(jnp.int32, (HEAD_DIM, LANES), 1) < HEAD_DIM
        for tt in reversed(range(Tc // 2)):
            tiles = [_key_tiles(c[0], c[1], oh_ref[tt]) for c in cols]
            pending = [[] for _ in range(P)]
            for u in (1, 0):
                t = 2 * tt + u
                for p in range(P):
                    W, A, Bt, Kt = (x[:, u * LANES:(u + 1) * LANES] for x in tiles[p])
                    R = jnp.where(head0, r_cols[p][:, t:t + 1], r_cols[p][:, Tc + t:Tc + t + 1])
                    ls = slice(p * LANES, (p + 1) * LANES)
                    vr, sar, dyr = (ref[t:t + 1, ls] for ref in (v_ref, sa_ref, dy_ref))
                    sp = sb_ref[p, t]
                    s_t = sp * W + Bt * sar + Kt * vr
                    d = ds[p] + R * dyr
                    dv_ref[t:t + 1, ls] = _rowsum(d * Kt)
                    dsar = _rowsum(d * Bt)
                    rows = jnp.where(rid < 2, vr, jnp.where(rid < 4, sar, jnp.where(rid < 6, 1.0, jnp.where(rid < 8, dyr, dsar))))
                    pending[p].append((t, jnp.where(own, rows, 0.0), [d, d * sp, s_t, sp]))
                    ds[p] = d * W + A * dsar
            for p in range(P):
                ls = slice(p * LANES, (p + 1) * LANES)
                (t1, rows1, tiles1), (t0, rows0, tiles0) = pending[p]
                g2 = nt(jnp.concatenate([rows1, rows0], axis=0), jnp.concatenate(tiles1 + tiles0, axis=0))
                for t, g in ((t1, g2[0:SCAN_G_ROWS, 0:2 * LANES]), (t0, g2[SCAN_G_ROWS:, 2 * LANES:])):
                    ga, gb = g[:, 0:LANES], g[:, LANES:2 * LANES]
                    ra, rb = pltpu.roll(ga, HEAD_DIM, 1), pltpu.roll(gb, HEAD_DIM, 1)
                    dk_ref[t:t + 1, ls] = jnp.where(lo, ga[0:1], ra[1:2])
                    db_ref[t:t + 1, ls] = jnp.where(lo, ga[2:3], ra[3:4])
                    dw_ref[t:t + 1, ls] = jnp.where(lo, ra[4:5], ga[5:6])
                    dr_ref[t:t + 1, ls] = jnp.where(lo, gb[6:7], rb[7:8])
                    da_ref[t:t + 1, ls] = jnp.where(lo, rb[8:9], gb[9:10])
        for p in range(P):
            dst[p] = ds[p]

    rev = lambda g, c: (g, nc - 1 - c, 0, 0)
    lspec = pl.BlockSpec((P, None, HEAD_DIM, Tc2), rev)
    rows = pl.BlockSpec((Tc, P * LANES), lambda g, c: ((g // npb) * nc + nc - 1 - c, g % npb))
    return pl.pallas_call(
        body, grid=(NP // P, nc),
        in_specs=[lspec] * 5 + [rows] * 3 + [pl.BlockSpec((P, Tc, HEAD_DIM, LANES), rev),
                                             pl.BlockSpec(onehot.shape, lambda g, c: (0, 0, 0))],
        out_specs=[rows] * 6, out_shape=[jax.ShapeDtypeStruct(v.shape, F32)] * 6,
        scratch_shapes=[pltpu.VMEM((P, HEAD_DIM, LANES), F32)], name="rwkv_scan_bwd",
        compiler_params=_params(("parallel", "arbitrary")),
    )(LW, LA, LB, LK, LR, v, sa, dy, sb, onehot)


def _shift_prev(p, B):
    T, W = p.shape
    return jnp.pad(p.reshape(B, T // B, W), ((0, 0), (1, 0), (0, 0)))[:, :-1].reshape(T, W)


def _shift_next(p, B):
    T, W = p.shape
    return jnp.pad(p.reshape(B, T // B, W), ((0, 0), (0, 1), (0, 0)))[:, 1:].reshape(T, W)


FOX_FWD_BLOCKS = (512, 1024)
FOX_BWD_BLOCKS = (512, 512)


def _layer_step(x, mem, target, W, late_weights=None, early_grads=None, scan_tc=SCAN_TC, fox_fwd_t=FOX_FWD_BLOCKS,
                fox_bwd_t=FOX_BWD_BLOCKS):
    B, S, _ = x.shape
    T = B * S
    x2, tgt2 = x.reshape(T, D_MODEL), target.reshape(T, D_MODEL)
    mem2 = mem.reshape(-1, D_MODEL)
    w_in_t = W["w_in"]
    wt_qkv, wt_rw, wt_mq, wt_gate = (w_in_t[lo:hi] for lo, hi in (COL_QKV, COL_RW, COL_MQ, COL_GATE))
    wt_f = jnp.pad(w_in_t[COL_F[0]:COL_F[1]], ((0, LANES - N_HEADS), (0, 0)))
    row = lambda v: v.reshape(1, -1).astype(F32)
    pre1_g, post1_g, pre2_g, post2_g, mem_g = (row(W[n]) for n in ("pre1_g", "post1_g", "pre2_g", "post2_g", "mem_norm_g"))

    u = _rowwise("rms_pre1", lambda x, g: x * _rms_stat(x) * g, [x2], [pre1_g], [(D_MODEL, BF16)])[0]
    qkv = _mm("proj_qkv", u, wt_qkv, tb=True, out_dtype=BF16)
    f_pad = _mm("proj_f", u, wt_f, tb=True)
    p_rw = _mm("proj_rwkv", u, wt_rw, tb=True)
    memq = _mm("proj_memq", u, wt_mq, tb=True, out_dtype=BF16)
    gate = _mm("proj_gate", u, wt_gate, tb=True, out_dtype=BF16)

    bias_col = W["fox_f_bias"].reshape(N_HEADS, 1).astype(F32)
    f8t = f_pad[:, :N_HEADS].reshape(B, S, N_HEADS).transpose(0, 2, 1)
    c = _fox_c_fwd(f8t, bias_col)
    G = B * N_HEADS
    q_blocks = lambda a, t: a.reshape(G, S // min(t, S), min(t, S), 1)
    k_blocks = lambda a, t: a.reshape(G, S // min(t, S), 1, min(t, S))
    c_col = c.reshape(G, S, 1)
    fox_out, lse = _fox_fwd(qkv, q_blocks(c_col, fox_fwd_t[0]), k_blocks(c, fox_fwd_t[1]), B)

    bd = _head_ones()
    zpad = jnp.zeros((64, BR_W), F32)
    w1 = jnp.concatenate([W["rwkv_w_up"].astype(F32), zpad], axis=0)
    w2 = jnp.concatenate([zpad, W["rwkv_a_up"].astype(F32)], axis=0)
    prep_consts = [row(W["rwkv_mu"]), row(W["rwkv_w0"]), w1, row(W["rwkv_a0"]), w2, W["rwkv_g_up"].astype(F32),
                   row(W["rwkv_k_k"]), row(W["rwkv_k_a"]), bd]
    p_prev = _shift_prev(p_rw, B)
    (rr, rk2, rv, rg), scan_cols = _rw_prep_fwd(p_rw, p_prev, prep_consts, B, scan_tc)
    ry, rsa, sb = _rw_scan_fwd(*scan_cols, rv)
    head_consts = [row(W["rwkv_gn_g"]), row(W["rwkv_gn_b"]), row(W["rwkv_r_k"]), bd]
    rwkv_out = _rw_head_fwd(ry, rr, rk2, rv, rg, head_consts)

    if late_weights is not None:
        W = {**W, **late_weights(rwkv_out)}

    mn = _rowwise("rms_mem", lambda m, g: m * _rms_stat(m) * g, [mem2], [mem_g], [(D_MODEL, BF16)])[0]
    mem_kv = _mm("proj_memkv", mn, W["w_mem_kv"], out_dtype=BF16)
    mem_out = _mem_fwd(memq, mem_kv, B)

    fo = [_mm("branch_" + n, a, W[n], tb=True, out_dtype=BF16)
          for n, a in (("w_fox_out", fox_out), ("w_rwkv_out", rwkv_out), ("w_mem_out", mem_out))]

    def merge(gate, f0, f1, f2):
        return sum(_sigmoid(gate[:, i * D_MODEL:(i + 1) * D_MODEL]) * f for i, f in enumerate((f0, f1, f2)))
    merged = _rowwise("merge", merge, [gate] + fo, [], [(D_MODEL, BF16)])[0]
    y1 = _mm("proj_o", merged, W["w_o"])

    def mid(x, y1, g1, g2):
        h1 = x + y1 * _rms_stat(y1) * g1
        return h1, h1 * _rms_stat(h1) * g2
    h1, u2 = _rowwise("norm_mid", mid, [x2, y1], [post1_g, pre2_g], [(D_MODEL, F32), (D_MODEL, BF16)])
    gt = _mm("ffn_gate", u2, W["w_ffn_gate"], tb=True, out_dtype=BF16)
    up = _mm("ffn_up", u2, W["w_ffn_up"], tb=True, out_dtype=BF16)
    act = _rowwise("swiglu", lambda gt, up: gt * _sigmoid(gt) * up, [gt, up], [], [(D_FF, BF16)])[0]
    ffn = _mm("ffn_down", act, W["w_ffn_down"])

    def tail(h1, ffn, tgt, g):
        err = h1 + ffn * _rms_stat(ffn) * g - tgt
        dh2 = err * (1.0 / D_MODEL)
        dffn, dg = _rms_bwd(dh2, ffn, g)
        loss = 0.5 * jnp.sum(jnp.sum(err * err, axis=1, keepdims=True) * (1.0 / D_MODEL), axis=0, keepdims=True)
        return dh2, dffn, dg, jnp.broadcast_to(loss, (1, LANES))
    dh2, dffn, d_post2, loss = _rowwise("loss_tail", tail, [h1, ffn, tgt2], [post2_g], [(D_MODEL, F32), (D_MODEL, BF16)],
                                        [((1, D_MODEL), F32), ((1, LANES), F32)])
    gw = {"post2_g": d_post2}
    dact = _mm("d_act", dffn, W["w_ffn_down"], tb=True, out_dtype=BF16)
    gw["w_ffn_down"] = _mm("g_ffn_down", act, dffn, ta=True, out_dtype=BF16)

    def swiglu_bwd(dact, gt, up):
        s = _sigmoid(gt)
        return dact * up * s * (1.0 + gt * (1.0 - s)), dact * gt * s
    dgt, dup = _rowwise("swiglu_bwd", swiglu_bwd, [dact, gt, up], [], [(D_FF, BF16)] * 2)
    du2 = _mm("d_u2_gate", dgt, W["w_ffn_gate"])
    du2 = _mm("d_u2_up", dup, W["w_ffn_up"], add=du2)
    gw["w_ffn_gate"] = _mm("g_ffn_gate", dgt, u2, ta=True, out_dtype=BF16)
    gw["w_ffn_up"] = _mm("g_ffn_up", dup, u2, ta=True, out_dtype=BF16)

    def mid_bwd(du2, dh2, h1, y1, g1, g2):
        dh1_n, d_pre2 = _rms_bwd(du2, h1, g2)
        dh1 = dh2 + dh1_n
        dy1, d_post1 = _rms_bwd(dh1, y1, g1)
        return dh1, dy1, d_post1, d_pre2
    dh1, dy1, gw["post1_g"], gw["pre2_g"] = _rowwise(
        "norm_mid_bwd", mid_bwd, [du2, dh2, h1, y1], [post1_g, pre2_g], [(D_MODEL, F32), (D_MODEL, BF16)],
        [((1, D_MODEL), F32)] * 2)
    dmerged = _mm("d_merged", dy1, W["w_o"], tb=True, out_dtype=BF16)
    gw["w_o"] = _mm("g_w_o", merged, dy1, ta=True, out_dtype=BF16)

    def merge_bwd(dm, gate, f0, f1, f2):
        s = [_sigmoid(gate[:, i * D_MODEL:(i + 1) * D_MODEL]) for i in range(3)]
        dgate = jnp.concatenate([dm * f * si * (1.0 - si) for f, si in zip((f0, f1, f2), s)], axis=1)
        return dm * s[0], dm * s[1], dm * s[2], dgate
    dfo0, dfo1, dfo2, dgate = _rowwise("merge_bwd", merge_bwd, [dmerged, gate] + fo, [],
                                       [(D_MODEL, BF16)] * 3 + [(3 * D_MODEL, BF16)])
    d_branch = {}
    for n, a, dfo in (("w_fox_out", fox_out, dfo0), ("w_rwkv_out", rwkv_out, dfo1), ("w_mem_out", mem_out, dfo2)):
        d_branch[n] = _mm("d_in_" + n, dfo, W[n], out_dtype=BF16)
        gw[n] = _mm("g_" + n, dfo, a, ta=True, out_dtype=BF16)

    dmemq, dkm, dvm = _mem_bwd(memq, mem_kv, d_branch["w_mem_out"], B)
    dmem_kv = jnp.concatenate([dkm, dvm], axis=1)
    gw["w_mem_kv"] = _mm("g_w_mem_kv", mn, dmem_kv, ta=True, out_dtype=BF16)
    dmn = _mm("d_mn", dmem_kv, W["w_mem_kv"], tb=True)
    gw["mem_norm_g"] = _rowwise("rms_mem_bwd", lambda d, m, g: _rms_bwd(d, m, g)[1], [dmn, mem2], [mem_g], [],
                                [((1, D_MODEL), F32)])[0]
    if early_grads is not None:
        d_branch["w_fox_out"] = early_grads(gw, d_branch["w_fox_out"])

    dfq, dfk, dfv, dck, dcq = _fox_bwd(qkv, fox_out, d_branch["w_fox_out"], q_blocks(c_col, fox_bwd_t[0]),
                                       k_blocks(c, fox_bwd_t[1]), q_blocks(lse, fox_bwd_t[0]), B)
    df8t, dbias = _fox_c_bwd(dck.reshape(B, N_HEADS, S) + dcq.reshape(B, N_HEADS, S), f8t, bias_col)
    gw["fox_f_bias"] = jnp.sum(dbias, axis=0).reshape(1, N_HEADS)
    dqkv = jnp.concatenate([dfq, dfk, dfv], axis=1)
    df_pad = jnp.pad(df8t.transpose(0, 2, 1).reshape(T, N_HEADS), ((0, 0), (0, LANES - N_HEADS))).astype(BF16)

    dry, drg, dr_h, dk2_h, dv_h, gw["rwkv_gn_g"], gw["rwkv_gn_b"], gw["rwkv_r_k"] = _rw_head_bwd(
        d_branch["w_rwkv_out"], ry, rr, rk2, rv, rg, head_consts)
    dv_s, dk2_s, db_s, dw_s, dr_s, da_s = _rw_scan_bwd(*scan_cols, rv, rsa, dry, sb)
    dP, dPp, gw["rwkv_mu"], gw["rwkv_w0"], d_w1, gw["rwkv_a0"], d_w2, gw["rwkv_g_up"], gw["rwkv_k_k"], gw["rwkv_k_a"] = \
        _rw_prep_bwd(p_rw, p_prev, [dr_s, dr_h, dw_s, dk2_s, dk2_h, dv_s, dv_h, da_s, db_s, drg], prep_consts)
    gw["rwkv_w_up"], gw["rwkv_a_up"] = d_w1[:64], d_w2[64:]
    dp_rw = dP + _shift_next(dPp, B)

    du = _mm("d_u_qkv", dqkv, wt_qkv)
    du = _mm("d_u_f", df_pad, wt_f, add=du)
    du = _mm("d_u_rwkv", dp_rw, wt_rw, add=du)
    du = _mm("d_u_memq", dmemq, wt_mq, add=du)
    du = _mm("d_u_gate", dgate, wt_gate, add=du)
    gw["w_in"] = jnp.concatenate(
        [_mm("g_w_qkv", dqkv, u, ta=True, out_dtype=BF16), _mm("g_w_f", df_pad, u, ta=True, out_dtype=BF16)[:N_HEADS],
         _mm("g_w_rwkv", dp_rw, u, ta=True, out_dtype=BF16), _mm("g_w_memq", dmemq, u, ta=True, out_dtype=BF16),
         _mm("g_w_gate", dgate, u, ta=True, out_dtype=BF16)], axis=0)

    def pre1_bwd(du, dh1, x, g):
        dx, dg = _rms_bwd(du, x, g)
        return dh1 + dx, dg
    dx, gw["pre1_g"] = _rowwise("rms_pre1_bwd", pre1_bwd, [du, dh1, x2], [pre1_g], [(D_MODEL, F32)], [((1, D_MODEL), F32)])
    return loss[0, 0], dx.reshape(B, S, D_MODEL), gw


TRANSPOSED = ("w_in", "w_ffn_gate", "w_ffn_up", "w_fox_out", "w_rwkv_out", "w_mem_out")
LORA = ("rwkv_w_up", "rwkv_a_up", "rwkv_g_up")
ROW_SHARDED = ("w_mem_kv", "w_o", "w_ffn_down")
FIRST = ("w_in",) + LORA
LATER = ("w_ffn_gate", "w_ffn_up", "w_mem_kv", "w_o", "w_ffn_down", "w_fox_out", "w_rwkv_out", "w_mem_out")
BIG = FIRST + LATER
SMALL = ("pre1_g", "post1_g", "pre2_g", "post2_g", "mem_norm_g", "fox_f_bias", "rwkv_mu", "rwkv_w0", "rwkv_a0", "rwkv_k_k",
         "rwkv_k_a", "rwkv_r_k", "rwkv_gn_g", "rwkv_gn_b")
WEIGHTS = ("pre1_g", "post1_g", "pre2_g", "post2_g", "mem_norm_g", "w_in", "fox_f_bias", "rwkv_mu", "rwkv_w0", "rwkv_w_up",
           "rwkv_a0", "rwkv_a_up", "rwkv_g_up", "rwkv_k_k", "rwkv_k_a", "rwkv_r_k", "rwkv_gn_g", "rwkv_gn_b", "w_mem_kv",
           "w_fox_out", "w_rwkv_out", "w_mem_out", "w_o", "w_ffn_gate", "w_ffn_up", "w_ffn_down")
WIRE_W = 1024
WIRE_ROW_ALIGN = 16
WIRE_HALF_ALIGN = 128


def _wire_rows(name, shard_shape):
    r, c = shard_shape
    if name in ROW_SHARDED:
        return r
    return -(-c // WIRE_ROW_ALIGN) * WIRE_ROW_ALIGN if r == WIRE_W else (r * c) // WIRE_W


def _to_wire(name, a):
    if name not in ROW_SHARDED:
        a = jnp.swapaxes(a, -1, -2)
    lead, (n, w) = a.shape[:-2], a.shape[-2:]
    if w != WIRE_W:
        return a.reshape(lead + ((n * w) // WIRE_W, WIRE_W))
    return jnp.pad(a, [(0, 0)] * len(lead) + [(0, (-n) % WIRE_ROW_ALIGN), (0, 0)])


def _from_wire(name, a, shard_shape):
    r, c = shard_shape
    if name in ROW_SHARDED:
        return a
    return a[..., :c, :] if r == WIRE_W else a.reshape(a.shape[:-2] + (c, r))


def _wire_layout(shard_shapes, names):
    layout, off = {}, 0
    for n in names:
        rows = _wire_rows(n, shard_shapes[n])
        layout[n] = (off, rows)
        off += rows
    return layout, -(-off // (2 * WIRE_HALF_ALIGN)) * WIRE_HALF_ALIGN


def _wire_pack(blocks, half_rows):
    a = jnp.concatenate(blocks, axis=-2)
    lead = a.shape[:-2]
    a = jnp.pad(a, [(0, 0)] * len(lead) + [(0, 2 * half_rows - a.shape[-2]), (0, 0)])
    return a.reshape(lead + (2, half_rows, WIRE_W))


def _my_place():
    return lax.axis_index("x"), lax.axis_index("y"), lax.axis_index("c")


def _other_chips(x, y):
    return [(1 - x, y), (x, 1 - y), (1 - x, 1 - y)]


ANY = pl.BlockSpec(memory_space=pl.ANY)


def _gather_weights(packed):
    _, R, L = packed.shape
    me = 2 * lax.axis_index("x") + lax.axis_index("y")
    base = lax.dynamic_update_index_in_dim(jnp.zeros((N_CHIPS, 2, R, L), packed.dtype), packed, me, 0)

    def body(in_ref, base_ref, out_ref, send_sems, recv_sems):
        x, y, c = _my_place()
        chip = lambda px, py: 2 * px + py
        sibling = (x, y, 1 - c)
        others = _other_chips(x, y)

        def copy(k, src, dst, to):
            return pltpu.make_async_remote_copy(src_ref=src, dst_ref=dst, send_sem=send_sems.at[k], recv_sem=recv_sems.at[k],
                                                device_id=to, device_id_type=MESH)

        sends = [copy(j, in_ref.at[c], out_ref.at[chip(x, y), c], (px, py, c)) for j, (px, py) in enumerate(others)]
        for cp in sends:
            cp.start()
        passed = [copy(3 + j, out_ref.at[chip(px, py), c], out_ref.at[chip(px, py), c], sibling)
                  for j, (px, py) in enumerate(others)]
        for j, (px, py) in enumerate(others):
            copy(j, in_ref.at[c], out_ref.at[chip(px, py), c], (px, py, c)).wait_recv()
            passed[j].start()
        for j, (px, py) in enumerate(others):
            copy(3 + j, in_ref.at[1 - c], out_ref.at[chip(px, py), 1 - c], sibling).wait_recv()
        for cp in sends + passed:
            cp.wait_send()

    return pl.pallas_call(
        body, out_shape=jax.ShapeDtypeStruct(base.shape, base.dtype), in_specs=[ANY, ANY], out_specs=ANY,
        input_output_aliases={1: 0}, scratch_shapes=[pltpu.SemaphoreType.DMA((6,)), pltpu.SemaphoreType.DMA((6,))],
        name="gather_weights",
    )(packed, base)


HBM_SPEC = pl.BlockSpec(memory_space=pltpu.HBM)
SEM_SPEC = pl.BlockSpec(memory_space=pltpu.SEMAPHORE)
DATAFLOW = pltpu.SideEffectType.DATAFLOW_SIDE_EFFECTING


def _in_hbm(a):
    return pltpu.with_memory_space_constraint(a, pltpu.HBM)


def _split_start(name, n_copies, src, land, start_copies):
    def body(src_ref, land_ref, send_sems, recv_sems, src_thru, land_thru, token):
        start_copies(src_ref, land_ref, send_sems, recv_sems)
        token[...] = jnp.zeros(token.shape, token.dtype)

    return pl.pallas_call(
        body, name=name,
        out_shape=(pltpu.SemaphoreType.DMA((n_copies,)), pltpu.SemaphoreType.DMA((n_copies,)), pltpu.HBM(src.shape, src.dtype),
                   pltpu.HBM(land.shape, land.dtype), jax.ShapeDtypeStruct((8, LANES), F32)),
        in_specs=(HBM_SPEC, HBM_SPEC),
        out_specs=(SEM_SPEC, SEM_SPEC, HBM_SPEC, HBM_SPEC, pl.BlockSpec(memory_space=pltpu.VMEM)),
        input_output_aliases={0: 2, 1: 3}, compiler_params=pltpu.CompilerParams(has_side_effects=DATAFLOW),
    )(_in_hbm(src), _in_hbm(land))


def _split_wait(name, handle, after, wait_copies):
    send_sems, recv_sems, src, land = handle[:4]

    def body(src_ref, land_ref, send_sems, recv_sems, after_ref, src_dead, land_out):
        wait_copies(src_ref, land_ref, send_sems, recv_sems)

    return pl.pallas_call(
        body, name=name, out_shape=(pltpu.HBM(src.shape, src.dtype), pltpu.HBM(land.shape, land.dtype)),
        in_specs=(HBM_SPEC, HBM_SPEC, SEM_SPEC, SEM_SPEC, ANY), out_specs=(HBM_SPEC, HBM_SPEC),
        input_output_aliases={0: 0, 1: 1}, compiler_params=pltpu.CompilerParams(has_side_effects=DATAFLOW),
    )(src, land, send_sems, recv_sems, after)[1]


def _late_gather_copies(src_ref, land_ref, send_sems, recv_sems):
    x, y, c = _my_place()
    me = 2 * x + y
    started, awaited = [], []
    for j, (px, py) in enumerate(_other_chips(x, y)):
        for core in range(2):
            started.append(pltpu.make_async_remote_copy(
                src_ref=src_ref.at[c], dst_ref=land_ref.at[me, c], send_sem=send_sems.at[2 * j + core],
                recv_sem=recv_sems.at[2 * j + c], device_id=(px, py, core), device_id_type=MESH))
            awaited.append(pltpu.make_async_remote_copy(
                src_ref=src_ref.at[core], dst_ref=land_ref.at[2 * px + py, core], send_sem=send_sems.at[2 * j + core],
                recv_sem=recv_sems.at[2 * j + core], device_id=(px, py, core), device_id_type=MESH))
    return started, awaited


def _gather_late_start(packed):
    _, R, L = packed.shape
    me = 2 * lax.axis_index("x") + lax.axis_index("y")
    land = lax.dynamic_update_index_in_dim(jnp.zeros((N_CHIPS, 2, R, L), packed.dtype), packed, me, 0)

    def start(src_ref, land_ref, send_sems, recv_sems):
        for cp in _late_gather_copies(src_ref, land_ref, send_sems, recv_sems)[0]:
            cp.start()

    return _split_start("gather_late_start", 6, packed, land, start)


def _gather_late_wait(handle, after):
    def wait(src_ref, land_ref, send_sems, recv_sems):
        started, awaited = _late_gather_copies(src_ref, land_ref, send_sems, recv_sems)
        for cp in started:
            cp.wait_send()
        for cp in awaited:
            cp.wait_recv()

    return _split_wait("gather_late_wait", handle, after, wait)


def _early_scatter_copies(src_ref, land_ref, send_sems, recv_sems):
    x, y, c = _my_place()
    me = 4 * x + 2 * y + c
    copies = []
    for k in range(1, N_DEV):
        px, py, pc = x ^ (k >> 2), y ^ ((k >> 1) & 1), c ^ (k & 1)
        copies.append(pltpu.make_async_remote_copy(
            src_ref=src_ref.at[2 * px + py, pc], dst_ref=land_ref.at[me], send_sem=send_sems.at[k - 1],
            recv_sem=recv_sems.at[k - 1], device_id=(px, py, pc), device_id_type=MESH))
    return copies


def _scatter_early_start(parts):
    n, _, R, L = parts.shape
    x, y, c = _my_place()
    own = lax.dynamic_index_in_dim(lax.dynamic_index_in_dim(parts, 2 * x + y, 0, keepdims=False), c, 0, keepdims=False)
    land = lax.dynamic_update_index_in_dim(jnp.zeros((N_DEV, R, L), parts.dtype), own, 4 * x + 2 * y + c, 0)

    def start(src_ref, land_ref, send_sems, recv_sems):
        for cp in _early_scatter_copies(src_ref, land_ref, send_sems, recv_sems):
            cp.start()

    return _split_start("scatter_early_start", N_DEV - 1, parts, land, start)


def _scatter_early_wait(handle, after):
    def wait(src_ref, land_ref, send_sems, recv_sems):
        for cp in _early_scatter_copies(src_ref, land_ref, send_sems, recv_sems):
            cp.wait_send()
            cp.wait_recv()

    return _split_wait("scatter_early_wait", handle, after, wait)


def _sum_slots(parts):
    n, R, L = parts.shape
    tr = _tile_rows(R, WIRE_HALF_ALIGN)

    def body(p_ref, o_ref):
        acc = p_ref[0].astype(F32)
        for i in range(1, n):
            acc = acc + p_ref[i].astype(F32)
        o_ref[...] = acc

    return pl.pallas_call(
        body, grid=(R // tr,), in_specs=[pl.BlockSpec((n, tr, L), lambda i: (0, i, 0))],
        out_specs=pl.BlockSpec((tr, L), lambda i: (i, 0)), out_shape=jax.ShapeDtypeStruct((R, L), F32), name="sum_slots",
        compiler_params=_params(("parallel",)),
    )(parts)


def _pair_exchange(parts):
    n, _, R, L = parts.shape

    def body(in_ref, out_ref, send_sems, recv_sems):
        x, y, c = _my_place()
        copies = [pltpu.make_async_remote_copy(src_ref=in_ref.at[s, 1 - c], dst_ref=out_ref.at[s], send_sem=send_sems.at[s],
                                               recv_sem=recv_sems.at[s], device_id=(x, y, 1 - c), device_id_type=MESH)
                  for s in range(n)]
        for cp in copies:
            cp.start()
        for cp in copies:
            cp.wait()

    return pl.pallas_call(
        body, out_shape=jax.ShapeDtypeStruct((n, R, L), parts.dtype), in_specs=[ANY], out_specs=ANY,
        scratch_shapes=[pltpu.SemaphoreType.DMA((n,)), pltpu.SemaphoreType.DMA((n,))], name="pair_exchange",
    )(parts)


def _pair_add(a, b):
    n, R, L = a.shape
    tr = _tile_rows(R, WIRE_HALF_ALIGN)

    def body(a_ref, b_ref, o_ref):
        o_ref[...] = (a_ref[...].astype(F32) + b_ref[...].astype(F32)).astype(o_ref.dtype)

    spec = pl.BlockSpec((n, tr, L), lambda i: (0, i, 0))
    return pl.pallas_call(
        body, grid=(R // tr,), in_specs=[spec, spec], out_specs=spec, out_shape=jax.ShapeDtypeStruct(a.shape, a.dtype),
        name="pair_add", compiler_params=_params(("parallel",)),
    )(a, b)


def _scatter_grads(parts):
    n, R, L = parts.shape

    def body(in_ref, out_ref, send_sems, recv_sems):
        x, y, c = _my_place()
        copies = [pltpu.make_async_remote_copy(src_ref=in_ref.at[2 * px + py], dst_ref=out_ref.at[j], send_sem=send_sems.at[j],
                                               recv_sem=recv_sems.at[j], device_id=(px, py, c), device_id_type=MESH)
                  for j, (px, py) in enumerate(_other_chips(x, y))]
        for cp in copies:
            cp.start()
        for cp in copies:
            cp.wait()

    return pl.pallas_call(
        body, out_shape=jax.ShapeDtypeStruct((n - 1, R, L), parts.dtype), in_specs=[ANY], out_specs=ANY,
        scratch_shapes=[pltpu.SemaphoreType.DMA((3,)), pltpu.SemaphoreType.DMA((3,))], name="scatter_grads",
    )(parts)


def _sum_chips(own, others):
    n, R, L = others.shape
    tr = _tile_rows(R, WIRE_HALF_ALIGN)

    def body(a_ref, p_ref, o_ref):
        acc = a_ref[...].astype(F32)
        for i in range(n):
            acc = acc + p_ref[i].astype(F32)
        o_ref[...] = acc

    return pl.pallas_call(
        body, grid=(R // tr,), in_specs=[pl.BlockSpec((tr, L), lambda i: (i, 0)), pl.BlockSpec((n, tr, L), lambda i: (0, i, 0))],
        out_specs=pl.BlockSpec((tr, L), lambda i: (i, 0)), out_shape=jax.ShapeDtypeStruct((R, L), F32), name="sum_grads",
        compiler_params=_params(("parallel",)),
    )(own, others)


def _tile_rows(R, cap=2048):
    best = 8
    for t in range(8, min(R, cap) + 1, 8):
        if R % t == 0:
            best = t
    return best if R % 8 == 0 else R


def _swap_halves(half):
    def body(in_ref, out_ref, send_sem, recv_sem):
        x, y, c = _my_place()
        cp = pltpu.make_async_remote_copy(src_ref=in_ref, dst_ref=out_ref, send_sem=send_sem, recv_sem=recv_sem,
                                          device_id=(x, y, 1 - c), device_id_type=MESH)
        cp.start()
        cp.wait()

    return pl.pallas_call(
        body, out_shape=jax.ShapeDtypeStruct(half.shape, half.dtype), in_specs=[ANY], out_specs=ANY,
        scratch_shapes=[pltpu.SemaphoreType.DMA, pltpu.SemaphoreType.DMA], name="swap_halves",
    )(half)


def _allreduce_small(v):
    R, L = v.shape

    def body(in_ref, out_ref, buf, send_sems, recv_sems):
        x, y, c = _my_place()
        me = 4 * x + 2 * y + c
        buf[me] = in_ref[...]
        started = []
        for k in range(1, N_DEV):
            to = (x ^ (k >> 2), y ^ ((k >> 1) & 1), c ^ (k & 1))
            cp = pltpu.make_async_remote_copy(src_ref=in_ref, dst_ref=buf.at[me], send_sem=send_sems.at[k - 1],
                                              recv_sem=recv_sems.at[k - 1], device_id=to, device_id_type=MESH)
            cp.start()
            started.append(cp)
        for cp in started:
            cp.wait()
        acc = buf[0]
        for i in range(1, N_DEV):
            acc = acc + buf[i]
        out_ref[...] = acc

    vm = pl.BlockSpec(memory_space=pltpu.VMEM)
    return pl.pallas_call(
        body, out_shape=jax.ShapeDtypeStruct((R, L), F32), in_specs=[vm], out_specs=vm,
        scratch_shapes=[pltpu.VMEM((N_DEV, R, L), F32), pltpu.SemaphoreType.DMA((7,)), pltpu.SemaphoreType.DMA((7,))],
        name="allreduce_small",
    )(v)


def _adamw(name, w, g, m, v):
    shape = w.shape
    C = shape[-1]
    R = int(np.prod(shape[:-1]))
    args = [a.reshape(R, C).astype(F32) for a in (w, g, m, v)]
    if R % 8 == 0 or C % LANES != 0:
        tr, tc = _tile_rows(R, 256), C
    else:
        tr, tc = R, LANES

    def body(w_ref, g_ref, m_ref, v_ref, d_ref, nm_ref, nv_ref):
        g = g_ref[...]
        m = ADAM_B1 * m_ref[...] + (1.0 - ADAM_B1) * g
        v = ADAM_B2 * v_ref[...] + (1.0 - ADAM_B2) * (g * g)
        m_hat = m / (1.0 - ADAM_B1 ** ADAM_STEP)
        v_hat = v / (1.0 - ADAM_B2 ** ADAM_STEP)
        d_ref[...] = -ADAM_LR * (m_hat / (jnp.sqrt(v_hat) + ADAM_EPS) + ADAM_WD * w_ref[...])
        nm_ref[...] = m
        nv_ref[...] = v

    spec = pl.BlockSpec((tr, tc), lambda i, j: (i, j))
    out = pl.pallas_call(
        body, grid=(R // tr, C // tc), in_specs=[spec] * 4, out_specs=[spec] * 3,
        out_shape=[jax.ShapeDtypeStruct((R, C), F32)] * 3, name="adamw_" + name,
        compiler_params=_params(("parallel", "parallel")),
    )(*args)
    return [o.reshape(shape) for o in out]


def kernel(x, mem, pre1_g, post1_g, pre2_g, post2_g, mem_norm_g, w_in, fox_f_bias, rwkv_mu, rwkv_w0, rwkv_w_up, rwkv_a0, rwkv_a_up, rwkv_g_up, rwkv_k_k, rwkv_k_a, rwkv_r_k, rwkv_gn_g, rwkv_gn_b, w_mem_kv, w_fox_out, w_rwkv_out, w_mem_out, w_o, w_ffn_gate, w_ffn_up, w_ffn_down, loss_target, m_pre1_g, m_post1_g, m_pre2_g, m_post2_g, m_mem_norm_g, m_w_in, m_fox_f_bias, m_rwkv_mu, m_rwkv_w0, m_rwkv_w_up, m_rwkv_a0, m_rwkv_a_up, m_rwkv_g_up, m_rwkv_k_k, m_rwkv_k_a, m_rwkv_r_k, m_rwkv_gn_g, m_rwkv_gn_b, m_w_mem_kv, m_w_fox_out, m_w_rwkv_out, m_w_mem_out, m_w_o, m_w_ffn_gate, m_w_ffn_up, m_w_ffn_down, v_pre1_g, v_post1_g, v_pre2_g, v_post2_g, v_mem_norm_g, v_w_in, v_fox_f_bias, v_rwkv_mu, v_rwkv_w0, v_rwkv_w_up, v_rwkv_a0, v_rwkv_a_up, v_rwkv_g_up, v_rwkv_k_k, v_rwkv_k_a, v_rwkv_r_k, v_rwkv_gn_g, v_rwkv_gn_b, v_w_mem_kv, v_w_fox_out, v_w_rwkv_out, v_w_mem_out, v_w_o, v_w_ffn_gate, v_w_ffn_up, v_w_ffn_down):
    given = dict(locals())
    w_loc = {n: given[n] for n in WEIGHTS}
    m_loc = {n: given["m_" + n] for n in WEIGHTS}
    v_loc = {n: given["v_" + n] for n in WEIGHTS}

    shard_shapes = {n: tuple(w_loc[n].shape[1:]) for n in BIG}
    groups = {names: _wire_layout(shard_shapes, names) for names in (FIRST, LATER)}
    core = lax.axis_index("c")

    def pack_weights(names):
        return _wire_pack([_to_wire(n, w_loc[n][0].astype(BF16)) for n in names], groups[names][1])

    def unpack_weights(gathered, names):
        layout, half_rows = groups[names]
        gathered = gathered.reshape(N_CHIPS, 2 * half_rows, WIRE_W)
        out = {}
        for n in names:
            off, rows = layout[n]
            blocks = _from_wire(n, gathered[:, off:off + rows], shard_shapes[n])
            if n in LORA:
                out[n] = blocks.transpose(2, 0, 1).reshape(blocks.shape[2], -1)
            else:
                out[n] = blocks.reshape(-1, blocks.shape[2])
        return out

    def pack_grads(gw, names):
        blocks = []
        for n in names:
            r, c = shard_shapes[n]
            g = gw[n].astype(BF16)
            if n in LORA:
                g = g.reshape(r, N_CHIPS, c).transpose(1, 0, 2)
            elif n in TRANSPOSED:
                g = jnp.swapaxes(g.reshape(N_CHIPS, c, r), 1, 2)
            else:
                g = g.reshape(N_CHIPS, r, c)
            blocks.append(_to_wire(n, g))
        return _wire_pack(blocks, groups[names][1])

    def unpack_grads(half, other, names):
        layout, _ = groups[names]
        reduced = jnp.where(core == 0, jnp.concatenate([half, other]), jnp.concatenate([other, half]))
        out = {}
        for n in names:
            off, rows = layout[n]
            g = _from_wire(n, reduced[off:off + rows], shard_shapes[n])
            out[n] = g.T if n in LORA else g
        return out

    first = _gather_weights(pack_weights(FIRST))
    late = _gather_late_start(pack_weights(LATER) + (first[0, 0, 0, 0] * 0).astype(BF16))
    W = unpack_weights(first, FIRST)
    W.update({n: w_loc[n][0] for n in SMALL})
    W["pre1_g"] = W["pre1_g"] + late[4][0, 0]
    early = []

    def late_weights(after):
        return unpack_weights(_gather_late_wait(late, after), LATER)

    def early_grads(gw, thru):
        early.append(_scatter_early_start(pack_grads(gw, LATER)))
        return thru + early[0][4][0, 0].astype(thru.dtype)

    loss, grad_x, gw = _layer_step(x, mem, loss_target, W, late_weights, early_grads)

    packed = pack_grads(gw, FIRST)
    own_halves = lax.dynamic_index_in_dim(packed, core, axis=1, keepdims=False)
    chip_sums = _pair_add(own_halves, _pair_exchange(packed))
    own_chip = lax.dynamic_index_in_dim(chip_sums, 2 * lax.axis_index("x") + lax.axis_index("y"), axis=0, keepdims=False)
    half_first = _sum_chips(own_chip, _scatter_grads(chip_sums))
    half_later = _sum_slots(_scatter_early_wait(early[0], half_first))
    rows_first = groups[FIRST][1]
    other = _swap_halves(jnp.concatenate([half_first, half_later]))
    g_shard = {**unpack_grads(half_first, other[:rows_first], FIRST), **unpack_grads(half_later, other[rows_first:], LATER)}

    small_shapes = [w_loc[n].shape[1:] for n in SMALL] + [(1,)]
    n_small = sum(int(np.prod(s)) for s in small_shapes)
    small_rows = -(-n_small // (8 * LANES)) * 8
    flat = jnp.concatenate([gw[n].reshape(-1) for n in SMALL] + [loss.reshape(1)])
    flat = jnp.pad(flat, (0, small_rows * LANES - n_small)).reshape(small_rows, LANES).reshape(-1)
    small, off = [], 0
    flat = _allreduce_small(flat.reshape(small_rows, LANES)).reshape(-1)
    for s in small_shapes:
        cnt = int(np.prod(s))
        small.append(flat[off:off + cnt].reshape(s))
        off += cnt
    g_small = dict(zip(SMALL, small[:-1]))
    loss = small[-1][0]

    grads, deltas, new_m, new_v = [], [], [], []
    flip = lambda a: jnp.swapaxes(a, -1, -2)
    for n in WEIGHTS:
        if n in TRANSPOSED:
            g_t = g_shard[n][None]
            d, nm, nv = (flip(o) for o in _adamw(n, flip(w_loc[n]), g_t, flip(m_loc[n]), flip(v_loc[n])))
            g = flip(g_t)
        else:
            g = (g_shard[n] if n in g_shard else g_small[n]).reshape(w_loc[n].shape)
            d, nm, nv = _adamw(n, w_loc[n], g, m_loc[n], v_loc[n])
        grads.append(g)
        deltas.append(d)
        new_m.append(nm)
        new_v.append(nv)
    return (loss, grad_x, *grads, *deltas, *new_m, *new_v)
```

```python
import numpy as np
import jax
import jax.numpy as jnp
from jax import lax
from jax.experimental import pallas as pl
from jax.experimental.pallas import tpu as pltpu

F32, BF16 = jnp.float32, jnp.bfloat16
MESH = pl.DeviceIdType.MESH

D_MODEL = 1024
HEAD_DIM = 64
N_HEADS = 8
BR_W = 512
MEM_HEADS = 4
MEM_HEAD_DIM = 128
D_FF = 2816
NORM_EPS = 1e-6
GN_EPS = 64e-5
N_CHIPS = 4
N_DEV = 8
LANES = 128
VMEM_LIMIT = 48 * 1024 * 1024

ADAM_LR, ADAM_B1, ADAM_B2, ADAM_EPS, ADAM_WD, ADAM_STEP = 0.001, 0.9, 0.999, 1e-08, 0.01, 10

FOX_COLS = 3 * BR_W + N_HEADS
RWKV_COLS = 3 * BR_W + 64 + 64 + 128
COL_QKV = (0, 3 * BR_W)
COL_F = (3 * BR_W, FOX_COLS)
COL_RW = (FOX_COLS, FOX_COLS + RWKV_COLS)
COL_MQ = (COL_RW[1], COL_RW[1] + BR_W)
COL_GATE = (COL_MQ[1], COL_MQ[1] + 3 * D_MODEL)

NT_DIMS = (((1,), (1,)), ((), ()))
TN_DIMS = (((0,), (0,)), ((), ()))


def _params(sem=None, **kw):
    return pltpu.CompilerParams(dimension_semantics=sem, vmem_limit_bytes=VMEM_LIMIT, **kw)


def _sigmoid(x):
    return 1.0 / (1.0 + jnp.exp(-x))


def _log_sigmoid(x):
    return jnp.minimum(x, 0.0) - jnp.log(1.0 + jnp.exp(-jnp.abs(x)))


def _bdot(a, b, dims=None):
    a, b = a.astype(BF16), b.astype(BF16)
    if dims is None:
        return jnp.dot(a, b, preferred_element_type=F32)
    return lax.dot_general(a, b, dims, preferred_element_type=F32)


def _hdot(a, ones):
    ones = ones.astype(BF16)
    hi = a.astype(BF16)
    r1 = a - hi.astype(F32)
    mid = r1.astype(BF16)
    lo = (r1 - mid.astype(F32)).astype(BF16)
    dot = lambda x: jnp.dot(x, ones, preferred_element_type=F32)
    return dot(hi) + dot(mid) + dot(lo)


def _tile(n, cap):
    best = None
    for t in range(LANES, min(n, cap) + 1, LANES):
        if n % t == 0:
            best = t
    return best or n


def _rowwise(name, fn, rows, consts, outs, accs=(), tm=256):
    T = rows[0].shape[0]
    tm = min(tm, T)
    assert T % tm == 0
    nr, nc, no, na = len(rows), len(consts), len(outs), len(accs)

    def body(*refs):
        res = fn(*[r[...].astype(F32) for r in refs[:nr + nc]])
        if not isinstance(res, (tuple, list)):
            res = (res,)
        orefs, arefs = refs[nr + nc:nr + nc + no], refs[nr + nc + no:]
        for ref, val in zip(orefs, res[:no]):
            ref[...] = val.astype(ref.dtype)
        if na:
            @pl.when(pl.program_id(0) == 0)
            def _():
                for ref in arefs:
                    ref[...] = jnp.zeros(ref.shape, ref.dtype)
            for ref, val in zip(arefs, res[no:]):
                ref[...] += val

    in_specs = ([pl.BlockSpec((tm, r.shape[1]), lambda i: (i, 0)) for r in rows]
                + [pl.BlockSpec(c.shape, lambda i: (0, 0)) for c in consts])
    out_specs = ([pl.BlockSpec((tm, w), lambda i: (i, 0)) for w, _ in outs]
                 + [pl.BlockSpec(s, lambda i: (0, 0)) for s, _ in accs])
    out_shape = ([jax.ShapeDtypeStruct((T, w), dt) for w, dt in outs]
                 + [jax.ShapeDtypeStruct(s, dt) for s, dt in accs])
    return pl.pallas_call(
        body, grid=(T // tm,), in_specs=in_specs, out_specs=out_specs, out_shape=out_shape, name=name,
        compiler_params=_params(("arbitrary",) if na else ("parallel",)),
    )(*rows, *consts)


MM_TILE_CAP = 1408
MM_WHOLE_K = 2048


def _mm(name, a, b, ta=False, tb=False, out_dtype=F32, add=None):
    M, K = (a.shape[1], a.shape[0]) if ta else a.shape
    K2, N = (b.shape[1], b.shape[0]) if tb else b.shape
    assert K == K2
    tm, tn = _tile(M, MM_TILE_CAP), _tile(N, MM_TILE_CAP)
    tk = K if K <= MM_WHOLE_K else _tile(K, MM_TILE_CAP)
    assert M % tm == 0 and N % tn == 0 and K % tk == 0
    nk = K // tk
    a_dim, b_dim = (0 if ta else 1), (1 if tb else 0)

    def body(*refs):
        a_ref, b_ref = refs[0], refs[1]
        n_in = 2 if add is None else 3
        o_ref, acc = refs[n_in], (refs[n_in + 1] if nk > 1 else None)
        k = pl.program_id(2)
        part = lax.dot_general(a_ref[...].astype(BF16), b_ref[...].astype(BF16),
                               (((a_dim,), (b_dim,)), ((), ())), preferred_element_type=F32)

        def finish(r):
            if add is not None:
                r = r + refs[2][...].astype(F32)
            o_ref[...] = r.astype(o_ref.dtype)

        if nk == 1:
            finish(part)
            return

        @pl.when(k == 0)
        def _():
            acc[...] = part

        @pl.when(k > 0)
        def _():
            acc[...] += part

        @pl.when(k == nk - 1)
        def _():
            finish(acc[...])

    a_spec = pl.BlockSpec((tk, tm), lambda i, j, k: (k, i)) if ta else pl.BlockSpec((tm, tk), lambda i, j, k: (i, k))
    b_spec = pl.BlockSpec((tn, tk), lambda i, j, k: (j, k)) if tb else pl.BlockSpec((tk, tn), lambda i, j, k: (k, j))
    o_spec = pl.BlockSpec((tm, tn), lambda i, j, k: (i, j))
    ins, in_specs = [a, b], [a_spec, b_spec]
    if add is not None:
        ins.append(add)
        in_specs.append(o_spec)
    return pl.pallas_call(
        body, grid=(M // tm, N // tn, nk), in_specs=in_specs, out_specs=o_spec,
        out_shape=jax.ShapeDtypeStruct((M, N), out_dtype), scratch_shapes=[pltpu.VMEM((tm, tn), F32)] if nk > 1 else [],
        name=name, compiler_params=_params(("parallel", "parallel", "arbitrary")),
    )(*ins)


def _rowsum(x):
    return jnp.sum(x, axis=0, keepdims=True)


def _rms_stat(x):
    return lax.rsqrt(jnp.mean(x * x, axis=-1, keepdims=True) + NORM_EPS)


def _rms_bwd(dy, x, g):
    r = _rms_stat(x)
    xn = x * r
    dxn = dy * g
    dx = r * (dxn - xn * jnp.mean(dxn * xn, axis=-1, keepdims=True))
    return dx, _rowsum(dy * xn)


def _fox_c_fwd(f8t, bias_col, tc=256):
    B, H, S = f8t.shape
    tc = min(tc, S)

    def body(f_ref, b_ref, c_ref, carry):
        @pl.when(pl.program_id(1) == 0)
        def _():
            carry[...] = jnp.zeros(carry.shape, F32)
        lf = _log_sigmoid(f_ref[...] + b_ref[...])
        row = lax.broadcasted_iota(jnp.int32, (tc, tc), 0)
        col = lax.broadcasted_iota(jnp.int32, (tc, tc), 1)
        c = _hdot(lf, (row <= col).astype(F32)) + carry[...]
        c_ref[...] = c
        carry[...] = c[:, tc - 1:tc]

    return pl.pallas_call(
        body, grid=(B, S // tc),
        in_specs=[pl.BlockSpec((None, H, tc), lambda b, i: (b, 0, i)), pl.BlockSpec((H, 1), lambda b, i: (0, 0))],
        out_specs=pl.BlockSpec((None, H, tc), lambda b, i: (b, 0, i)),
        out_shape=jax.ShapeDtypeStruct((B, H, S), F32), scratch_shapes=[pltpu.VMEM((H, 1), F32)], name="fox_c_fwd",
        compiler_params=_params(("parallel", "arbitrary")),
    )(f8t, bias_col)


def _fox_c_bwd(dc, f8t, bias_col, tc=256):
    B, H, S = f8t.shape
    tc = min(tc, S)
    n = S // tc

    def body(dc_ref, f_ref, b_ref, df_ref, db_ref, carry):
        @pl.when(pl.program_id(1) == 0)
        def _():
            carry[...] = jnp.zeros(carry.shape, F32)
            db_ref[...] = jnp.zeros(db_ref.shape, F32)
        row = lax.broadcasted_iota(jnp.int32, (tc, tc), 0)
        col = lax.broadcasted_iota(jnp.int32, (tc, tc), 1)
        dlf = _hdot(dc_ref[...], (row >= col).astype(F32)) + carry[...]
        z = f_ref[...] + b_ref[...]
        df = dlf * (1.0 - _sigmoid(z))
        df_ref[...] = df
        db_ref[...] += jnp.sum(df, axis=1, keepdims=True)
        carry[...] = dlf[:, 0:1]

    rev = lambda b, i: (b, 0, n - 1 - i)
    return pl.pallas_call(
        body, grid=(B, n),
        in_specs=[pl.BlockSpec((None, H, tc), rev), pl.BlockSpec((None, H, tc), rev), pl.BlockSpec((H, 1), lambda b, i: (0, 0))],
        out_specs=[pl.BlockSpec((None, H, tc), rev), pl.BlockSpec((None, H, 1), lambda b, i: (b, 0, 0))],
        out_shape=[jax.ShapeDtypeStruct((B, H, S), F32), jax.ShapeDtypeStruct((B, H, 1), F32)],
        scratch_shapes=[pltpu.VMEM((H, 1), F32)], name="fox_c_bwd",
        compiler_params=_params(("parallel", "arbitrary")),
    )(dc, f8t, bias_col)


NEG_BIG = -1e30


def _fox_logits(q, kj, ckj, i, j, tq, tk, scale):
    s = _bdot(q, kj, NT_DIMS) * scale - ckj
    row = lax.broadcasted_iota(jnp.int32, (tq, tk), 0)
    col = lax.broadcasted_iota(jnp.int32, (tq, tk), 1)
    return s, col <= row + (i * tq - j * tk)


def _fox_key_blocks(i, tq, tk):
    return (i * tq + tq - 1) // tk + 1


def _fox_fwd(qkv, ck, B, tq):
    T = qkv.shape[0]
    S = T // B
    nq, (nk, tk) = S // tq, (ck.shape[1], ck.shape[3])
    scale = HEAD_DIM ** -0.5

    def body(q_ref, k_ref, v_ref, ck_ref, o_ref, lse_ref):
        i = pl.program_id(1)
        lo = lax.broadcasted_iota(jnp.int32, (tq, LANES), 1) < HEAD_DIM
        q = q_ref[...]
        qh = (jnp.where(lo, q, 0), jnp.where(lo, 0, q))

        def step(j, carry):
            rows = pl.ds(pl.multiple_of(j * tk, tk), tk)
            kj, vj = k_ref[rows, :], v_ref[rows, :]
            new = []
            for h in range(2):
                m, l, acc = carry[h]
                s, ok = _fox_logits(qh[h], kj, ck_ref[h, j], i, j, tq, tk, scale)
                s = jnp.where(ok, s, NEG_BIG)
                m2 = jnp.maximum(m, jnp.max(s, axis=1, keepdims=True))
                p = jnp.exp(s - m2)
                al = jnp.exp(m - m2)
                new.append((m2, al * l + jnp.sum(p, axis=1, keepdims=True), al * acc + _bdot(p, vj)))
            return tuple(new)

        init = tuple((jnp.full((tq, 1), NEG_BIG, F32), jnp.zeros((tq, 1), F32), jnp.zeros((tq, LANES), F32)) for _ in range(2))
        (m0, l0, a0), (m1, l1, a1) = lax.fori_loop(0, _fox_key_blocks(i, tq, tk), step, init)
        o_ref[...] = jnp.where(lo, a0 / l0, a1 / l1).astype(o_ref.dtype)
        lse_ref[0] = m0 + jnp.log(l0)
        lse_ref[1] = m1 + jnp.log(l1)

    seq = lambda col0: pl.BlockSpec((S, LANES), lambda g, i: (g // 4, col0 + g % 4))
    blk = lambda col0: pl.BlockSpec((tq, LANES), lambda g, i: ((g // 4) * nq + i, col0 + g % 4))
    col = pl.BlockSpec((2, None, tq, 1), lambda g, i: (g, i, 0, 0))
    return pl.pallas_call(
        body, grid=(B * 4, nq), in_specs=[blk(0), seq(4), seq(8), pl.BlockSpec((2, nk, 1, tk), lambda g, i: (g, 0, 0, 0))],
        out_specs=[blk(0), col],
        out_shape=[jax.ShapeDtypeStruct((T, BR_W), BF16), jax.ShapeDtypeStruct((B * N_HEADS, nq, tq, 1), F32)],
        name="fox_fwd", compiler_params=_params(("parallel", "parallel")),
    )(qkv, qkv, qkv, ck)


def _fox_bwd(qkv, o, do, ck, lse, B):
    T = qkv.shape[0]
    S = T // B
    (nq, tq), (nk, tk) = lse.shape[1:3], (ck.shape[1], ck.shape[3])
    scale = HEAD_DIM ** -0.5

    def body(q_ref, k_ref, v_ref, o_ref, do_ref, ck_ref, lse_ref, dq_ref, dk_ref, dv_ref, dck_ref, dcq_ref,
             dk_acc, dv_acc):
        dk_acc[...] = jnp.zeros(dk_acc.shape, F32)
        dv_acc[...] = jnp.zeros(dv_acc.shape, F32)
        dck_ref[...] = jnp.zeros(dck_ref.shape, F32)
        lo = lax.broadcasted_iota(jnp.int32, (tq, LANES), 1) < HEAD_DIM

        def qloop(i, _):
            qrows = pl.ds(pl.multiple_of(i * tq, tq), tq)
            q, do_i, o_i = q_ref[qrows, :], do_ref[qrows, :], o_ref[qrows, :].astype(F32)
            qh = (jnp.where(lo, q, 0), jnp.where(lo, 0, q))
            doh = (jnp.where(lo, do_i, 0), jnp.where(lo, 0, do_i))
            delta = [jnp.sum(doh[h].astype(F32) * o_i, axis=1, keepdims=True) for h in range(2)]

            def kloop(j, carry):
                krows = pl.ds(pl.multiple_of(j * tk, tk), tk)
                kj, vj = k_ref[krows, :], v_ref[krows, :]
                new = []
                for h in range(2):
                    dq, dcq = carry[h]
                    s, ok = _fox_logits(qh[h], kj, ck_ref[h, j], i, j, tq, tk, scale)
                    p = jnp.where(ok, jnp.exp(s - lse_ref[h, i]), 0.0)
                    ds = p * (_bdot(doh[h], vj, NT_DIMS) - delta[h])
                    dv_acc[krows, :] += _bdot(p, doh[h], TN_DIMS)
                    dk_acc[krows, :] += _bdot(ds, qh[h], TN_DIMS) * scale
                    dck_ref[h, j] += -_rowsum(ds)
                    new.append((dq + _bdot(ds, kj) * scale, dcq + jnp.sum(ds, axis=1, keepdims=True)))
                return tuple(new)

            init = tuple((jnp.zeros((tq, LANES), F32), jnp.zeros((tq, 1), F32)) for _ in range(2))
            (dq0, dcq0), (dq1, dcq1) = lax.fori_loop(0, _fox_key_blocks(i, tq, tk), kloop, init)
            dq_ref[qrows, :] = jnp.where(lo, dq0, dq1).astype(dq_ref.dtype)
            dcq_ref[0, i] = dcq0
            dcq_ref[1, i] = dcq1
            return 0

        lax.fori_loop(0, nq, qloop, 0)
        dk_ref[...] = dk_acc[...].astype(dk_ref.dtype)
        dv_ref[...] = dv_acc[...].astype(dv_ref.dtype)

    seq = lambda col0: pl.BlockSpec((S, LANES), lambda g: (g // 4, col0 + g % 4))
    col = pl.BlockSpec((2, nq, tq, 1), lambda g: (g, 0, 0, 0))
    row = pl.BlockSpec((2, nk, 1, tk), lambda g: (g, 0, 0, 0))
    out = jax.ShapeDtypeStruct((T, BR_W), BF16)
    return pl.pallas_call(
        body, grid=(B * 4,), in_specs=[seq(0), seq(4), seq(8), seq(0), seq(0), row, col],
        out_specs=[seq(0), seq(0), seq(0), row, col],
        out_shape=[out, out, out, jax.ShapeDtypeStruct(ck.shape, F32), jax.ShapeDtypeStruct(lse.shape, F32)],
        scratch_shapes=[pltpu.VMEM((S, LANES), F32), pltpu.VMEM((S, LANES), F32)], name="fox_bwd",
        compiler_params=_params(("parallel",)),
    )(qkv, qkv, qkv, o, do, ck, lse)


def _mem_probs(qh, kh):
    s = _bdot(qh, kh, NT_DIMS) * (MEM_HEAD_DIM ** -0.5)
    e = jnp.exp(s - jnp.max(s, axis=1, keepdims=True))
    return e / jnp.sum(e, axis=1, keepdims=True)


def _mem_fwd(q, mem_kv, B, tq=512):
    T = q.shape[0]
    S, Lm = T // B, mem_kv.shape[0] // B
    tq = min(tq, S)
    n = S // tq

    def body(q_ref, k_ref, v_ref, o_ref):
        for h in range(MEM_HEADS):
            sl = slice(h * MEM_HEAD_DIM, (h + 1) * MEM_HEAD_DIM)
            p = _mem_probs(q_ref[:, sl], k_ref[:, sl])
            o_ref[:, sl] = _bdot(p, v_ref[:, sl]).astype(o_ref.dtype)

    qs = pl.BlockSpec((tq, BR_W), lambda b, i: (b * n + i, 0))
    return pl.pallas_call(
        body, grid=(B, n),
        in_specs=[qs, pl.BlockSpec((Lm, BR_W), lambda b, i: (b, 0)), pl.BlockSpec((Lm, BR_W), lambda b, i: (b, 1))],
        out_specs=qs, out_shape=jax.ShapeDtypeStruct((T, BR_W), BF16), name="mem_fwd",
        compiler_params=_params(("parallel", "parallel")),
    )(q, mem_kv, mem_kv)


def _mem_bwd(q, mem_kv, do, B, tq=512):
    T = q.shape[0]
    S, Lm = T // B, mem_kv.shape[0] // B
    tq = min(tq, S)
    n = S // tq
    scale = MEM_HEAD_DIM ** -0.5

    def body(q_ref, k_ref, v_ref, do_ref, dq_ref, dk_ref, dv_ref):
        @pl.when(pl.program_id(1) == 0)
        def _():
            dk_ref[...] = jnp.zeros(dk_ref.shape, F32)
            dv_ref[...] = jnp.zeros(dv_ref.shape, F32)
        for h in range(MEM_HEADS):
            sl = slice(h * MEM_HEAD_DIM, (h + 1) * MEM_HEAD_DIM)
            qh, kh, vh, doh = q_ref[:, sl], k_ref[:, sl], v_ref[:, sl], do_ref[:, sl]
            p = _mem_probs(qh, kh)
            dp = _bdot(doh, vh, NT_DIMS)
            ds = p * (dp - jnp.sum(p * dp, axis=1, keepdims=True))
            dq_ref[:, sl] = (_bdot(ds, kh) * scale).astype(dq_ref.dtype)
            dk_ref[:, sl] += _bdot(ds, qh, TN_DIMS) * scale
            dv_ref[:, sl] += _bdot(p, doh, TN_DIMS)

    qs = pl.BlockSpec((tq, BR_W), lambda b, i: (b * n + i, 0))
    kv = pl.BlockSpec((Lm, BR_W), lambda b, i: (b, 0))
    return pl.pallas_call(
        body, grid=(B, n),
        in_specs=[qs, kv, pl.BlockSpec((Lm, BR_W), lambda b, i: (b, 1)), qs], out_specs=[qs, kv, kv],
        out_shape=[jax.ShapeDtypeStruct((T, BR_W), BF16), jax.ShapeDtypeStruct((B * Lm, BR_W), F32),
                   jax.ShapeDtypeStruct((B * Lm, BR_W), F32)], name="mem_bwd",
        compiler_params=_params(("parallel", "arbitrary")),
    )(q, mem_kv, mem_kv, do)


def _head_ones():
    h = np.arange(BR_W) // HEAD_DIM
    return jnp.asarray((h[:, None] == h[None, :]).astype(np.float32))


def _rw_prep(p, pp, mu, w0, w1, a0, w2, g_up, k_k, k_a, bd):
    ps = p + (pp - p) * mu
    r, k, v = ps[:, 0:512], ps[:, 512:1024], ps[:, 1024:1536]
    wa, gd = ps[:, 1536:1664], ps[:, 1664:1792]
    th = jnp.tanh(wa)
    z = w0 + _bdot(th, w1)
    wl = -jnp.exp(_log_sigmoid(z) - 0.5)
    w = jnp.exp(wl)
    a = _sigmoid(a0 + _bdot(wa, w2))
    sg = _sigmoid(gd)
    g = _bdot(sg, g_up)
    kq = k * k_k
    n2 = _hdot(kq * kq, bd)
    inv = lax.rsqrt(jnp.maximum(n2, 1e-24))
    kk = kq * inv
    k2 = k * (1.0 + (a - 1.0) * k_a)
    return dict(ps=ps, r=r, k=k, v=v, wa=wa, th=th, z=z, wl=wl, w=w, a=a, sg=sg, g=g, kq=kq, n2=n2, inv=inv, kk=kk, k2=k2)


def _keycol_selector(tm, Tc):
    e = np.zeros((tm, (tm // Tc) * LANES), np.float32)
    for t in range(tm):
        c, tl = divmod(t, Tc)
        e[t, c * LANES + tl] = e[t, c * LANES + Tc + tl] = 1.0
    return jnp.asarray(e, BF16)


def _rw_prep_fwd(p, pp, consts, B, Tc, tm=256):
    assert 2 * Tc == LANES
    T = p.shape[0]
    S = T // B
    nb, cpb = S // tm, tm // Tc
    sel = _keycol_selector(tm, Tc)
    nc = len(consts)

    def body(*refs):
        t = _rw_prep(*[r[...] for r in refs[:2 + nc]])
        sel_ref = refs[2 + nc]
        rows, cols = refs[3 + nc:7 + nc], refs[7 + nc:]
        for ref, val in zip(rows, (t["r"], t["k2"], t["v"], t["g"])):
            ref[...] = val
        lo = lax.broadcasted_iota(jnp.int32, (HEAD_DIM, LANES), 1) < HEAD_DIM
        operands = (t["w"], -t["kk"], t["kk"] * t["a"], t["k2"], t["r"])
        for n, (ref, x) in enumerate(zip(cols, operands)):
            terms = _split_bf16(x) if ref.dtype == F32 else (x.astype(BF16),)
            for hp in range(4):
                xt = sum(lax.dot_general(tt[:, hp * LANES:(hp + 1) * LANES], sel_ref[...], TN_DIMS,
                                         preferred_element_type=F32) for tt in terms)
                for c in range(cpb):
                    blk = xt[:, c * LANES:(c + 1) * LANES]
                    ref[hp, c] = jnp.where(lo, blk[0:HEAD_DIM], blk[HEAD_DIM:2 * HEAD_DIM]).astype(ref.dtype)

    row_spec = lambda w: pl.BlockSpec((tm, w), lambda i: (i, 0))
    col_spec = pl.BlockSpec((None, 4, cpb, HEAD_DIM, LANES), lambda i: (i // nb, 0, i % nb, 0, 0))
    col_shape = lambda dt: jax.ShapeDtypeStruct((B, 4, S // Tc, HEAD_DIM, LANES), dt)
    out = pl.pallas_call(
        body, grid=(T // tm,),
        in_specs=[row_spec(RWKV_COLS)] * 2 + [pl.BlockSpec(c.shape, lambda i: (0, 0)) for c in consts]
        + [pl.BlockSpec(sel.shape, lambda i: (0, 0))],
        out_specs=[row_spec(BR_W)] * 4 + [col_spec] * 5,
        out_shape=[jax.ShapeDtypeStruct((T, BR_W), F32)] * 4 + [col_shape(F32)] + [col_shape(BF16)] * 3 + [col_shape(F32)],
        name="rwkv_prep_fwd", compiler_params=_params(("parallel",)),
    )(p, pp, *consts, sel)
    return out[:4], [c.reshape(B * 4, S // Tc, HEAD_DIM, LANES) for c in out[4:]]


def _rw_prep_bwd(p, pp, cots, consts):
    def fn(p, pp, dr1, dr2, dw, dk21, dk22, dv1, dv2, dav, dbv, dg, mu, w0, w1, a0, w2, g_up, k_k, k_a, bd):
        t = _rw_prep(p, pp, mu, w0, w1, a0, w2, g_up, k_k, k_a, bd)
        dr, dk2, dv = dr1 + dr2, dk21 + dk22, dv1 + dv2
        a, k, kk, kq, inv = t["a"], t["k"], t["kk"], t["kq"], t["inv"]
        dkk = dbv * a - dav
        da = dbv * kk + dk2 * k * k_a
        dk = dk2 * (1.0 + (a - 1.0) * k_a)
        d_k_a = _rowsum(dk2 * k * (a - 1.0))
        proj = _hdot(dkk * kq, bd)
        dkq = dkk * inv - jnp.where(t["n2"] > 1e-24, kq * inv * inv * inv * proj, 0.0)
        dk = dk + dkq * k_k
        d_k_k = _rowsum(dkq * k)
        dpa = da * a * (1.0 - a)
        d_a0 = _rowsum(dpa)
        dwa = _bdot(dpa, w2, NT_DIMS)
        d_w2 = _bdot(t["wa"], dpa, TN_DIMS)
        dz = dw * t["w"] * t["wl"] * (1.0 - _sigmoid(t["z"]))
        d_w0 = _rowsum(dz)
        th = t["th"]
        dwa = dwa + _bdot(dz, w1, NT_DIMS) * (1.0 - th * th)
        d_w1 = _bdot(th, dz, TN_DIMS)
        sg = t["sg"]
        dgd = _bdot(dg, g_up, NT_DIMS) * sg * (1.0 - sg)
        d_g_up = _bdot(sg, dg, TN_DIMS)
        dps = jnp.concatenate([dr, dk, dv, dwa, dgd], axis=1)
        d_mu = _rowsum(dps * (pp - p))
        return dps * (1.0 - mu), dps * mu, d_mu, d_w0, d_w1, d_a0, d_w2, d_g_up, d_k_k, d_k_a

    accs = [((1, RWKV_COLS), F32), ((1, BR_W), F32), ((LANES, BR_W), F32), ((1, BR_W), F32), ((LANES, BR_W), F32),
            ((LANES, BR_W), F32), ((1, BR_W), F32), ((1, BR_W), F32)]
    return _rowwise("rwkv_prep_bwd", fn, [p, pp] + list(cots), consts, [(RWKV_COLS, BF16)] * 2, accs, tm=256)


def _rw_head(y, r, k2, v, g, gn_g, gn_b, r_k, bd):
    mean = _hdot(y, bd) * (1.0 / HEAD_DIM)
    yc = y - mean
    rs = lax.rsqrt(_hdot(yc * yc, bd) * (1.0 / HEAD_DIM) + GN_EPS)
    yn = yc * rs
    bs = _hdot(r * k2 * r_k, bd)
    return yn, rs, bs, yn * gn_g + gn_b + bs * v


def _rw_head_fwd(y, r, k2, v, g, consts):
    def fn(y, r, k2, v, g, *c):
        return _rw_head(y, r, k2, v, g, *c)[3] * g
    return _rowwise("rwkv_head_fwd", fn, [y, r, k2, v, g], consts, [(BR_W, BF16)])[0]


def _rw_head_bwd(dout, y, r, k2, v, g, consts):
    def fn(dout, y, r, k2, v, g, gn_g, gn_b, r_k, bd):
        dout = dout.astype(F32)
        yn, rs, bs, zz = _rw_head(y, r, k2, v, g, gn_g, gn_b, r_k, bd)
        dg = dout * zz
        dz = dout * g
        dyn = dz * gn_g
        inv_n = 1.0 / HEAD_DIM
        dy = rs * (dyn - _hdot(dyn, bd) * inv_n - yn * (_hdot(dyn * yn, bd) * inv_n))
        dq = _hdot(dz * v, bd)
        return dy, dg, dq * k2 * r_k, dq * r * r_k, dz * bs, _rowsum(dz * yn), _rowsum(dz), _rowsum(dq * r * k2)
    return _rowwise("rwkv_head_bwd", fn, [dout, y, r, k2, v, g], consts, [(BR_W, F32)] * 5, [((1, BR_W), F32)] * 3)


SCAN_TC = 64


def _scan_onehot(Tc):
    w = np.zeros((Tc // 2, 2 * Tc, 2 * LANES), np.float32)
    for tt in range(Tc // 2):
        for u in range(2):
            for h in range(2):
                w[tt, h * Tc + 2 * tt + u, u * LANES + h * HEAD_DIM: u * LANES + (h + 1) * HEAD_DIM] = 1.0
    return jnp.asarray(w, BF16)


def _split_bf16(x):
    hi = x.astype(BF16)
    return hi, (x - hi.astype(F32)).astype(BF16)


def _key_tiles(l_w, others, onehot):
    dot = lambda x: jnp.dot(x, onehot, preferred_element_type=F32)
    whi, wmid = l_w
    return [dot(whi) + dot(wmid)] + [dot(o) for o in others]


def _rw_scan_fwd(LW, LA, LB, LK, LR, v, P=4):
    NP, nc, _, Tc2 = LW.shape
    Tc = Tc2 // 2
    S = nc * Tc
    onehot = _scan_onehot(Tc)
    npb = 4 // P

    def body(lw, la, lb, lk, lr, v_ref, oh_ref, y_ref, sa_ref, sb_ref, st):
        @pl.when(pl.program_id(1) == 0)
        def _():
            st[...] = jnp.zeros(st.shape, F32)
        s = [st[p] for p in range(P)]
        cols = [(_split_bf16(lw[p]), [ref[p].astype(BF16) for ref in (la, lb, lk)]) for p in range(P)]
        r_cols = [lr[p].astype(F32) for p in range(P)]
        head0 = lax.broadcasted_iota(jnp.int32, (HEAD_DIM, LANES), 1) < HEAD_DIM
        for tt in range(Tc // 2):
            tiles = [_key_tiles(c[0], c[1], oh_ref[tt]) for c in cols]
            for u in range(2):
                t = 2 * tt + u
                for p in range(P):
                    W, A, Bt, Kt = (x[:, u * LANES:(u + 1) * LANES] for x in tiles[p])
                    R = jnp.where(head0, r_cols[p][:, t:t + 1], r_cols[p][:, Tc + t:Tc + t + 1])
                    ls = slice(p * LANES, (p + 1) * LANES)
                    sb_ref[p, t] = s[p]
                    sa = _rowsum(s[p] * A)
                    s[p] = s[p] * W + Bt * sa + Kt * v_ref[t:t + 1, ls]
                    y_ref[t:t + 1, ls] = _rowsum(s[p] * R)
                    sa_ref[t:t + 1, ls] = sa
        for p in range(P):
            st[p] = s[p]

    lspec = pl.BlockSpec((P, None, HEAD_DIM, Tc2), lambda g, c: (g, c, 0, 0))
    rows = pl.BlockSpec((Tc, P * LANES), lambda g, c: ((g // npb) * nc + c, g % npb))
    rowshape = jax.ShapeDtypeStruct(v.shape, F32)
    return pl.pallas_call(
        body, grid=(NP // P, nc), in_specs=[lspec] * 5 + [rows, pl.BlockSpec(onehot.shape, lambda g, c: (0, 0, 0))],
        out_specs=[rows, rows, pl.BlockSpec((P, Tc, HEAD_DIM, LANES), lambda g, c: (g, c, 0, 0))],
        out_shape=[rowshape, rowshape, jax.ShapeDtypeStruct((NP, S, HEAD_DIM, LANES), F32)],
        scratch_shapes=[pltpu.VMEM((P, HEAD_DIM, LANES), F32)], name="rwkv_scan_fwd",
        compiler_params=_params(("parallel", "arbitrary")),
    )(LW, LA, LB, LK, LR, v, onehot)


SCAN_G_ROWS = 16


def _rw_scan_bwd(LW, LA, LB, LK, LR, v, sa, dy, sb, P=4):
    NP, nc, _, Tc2 = LW.shape
    Tc = Tc2 // 2
    onehot = _scan_onehot(Tc)
    npb = 4 // P

    def body(lw, la, lb, lk, lr, v_ref, sa_ref, dy_ref, sb_ref, oh_ref, dv_ref, dk_ref, db_ref, dw_ref, dr_ref, da_ref, dst):
        @pl.when(pl.program_id(1) == 0)
        def _():
            dst[...] = jnp.zeros(dst.shape, F32)
        rid = lax.broadcasted_iota(jnp.int32, (SCAN_G_ROWS, LANES), 0)
        lane = lax.broadcasted_iota(jnp.int32, (SCAN_G_ROWS, LANES), 1)
        own = (((rid % 2) == 0) == (lane < HEAD_DIM)) & (rid < 10)
        lo = lane[0:1] < HEAD_DIM
        nt = lambda rows, tile: lax.dot_general(rows.astype(BF16), tile.astype(BF16), NT_DIMS, preferred_element_type=F32)
        ds = [dst[p] for p in range(P)]
        cols = [(_split_bf16(lw[p]), [ref[p].astype(BF16) for ref in (la, lb, lk)]) for p in range(P)]
        r_cols = [lr[p].astype(F32) for p in range(P)]
        head0 = lax.broadcasted_iota(jnp.int32, (HEAD_DIM, LANES), 1) < HEAD_DIM
        for tt in reversed(range(Tc // 2)):
            tiles = [_key_tiles(c[0], c[1], oh_ref[tt]) for c in cols]
            pending = [[] for _ in range(P)]
            for u in (1, 0):
                t = 2 * tt + u
                for p in range(P):
                    W, A, Bt, Kt = (x[:, u * LANES:(u + 1) * LANES] for x in tiles[p])
                    R = jnp.where(head0, r_cols[p][:, t:t + 1], r_cols[p][:, Tc + t:Tc + t + 1])
                    ls = slice(p * LANES, (p + 1) * LANES)
                    vr, sar, dyr = (ref[t:t + 1, ls] for ref in (v_ref, sa_ref, dy_ref))
                    sp = sb_ref[p, t]
                    s_t = sp * W + Bt * sar + Kt * vr
                    d = ds[p] + R * dyr
                    dv_ref[t:t + 1, ls] = _rowsum(d * Kt)
                    dsar = _rowsum(d * Bt)
                    rows = jnp.where(rid < 2, vr, jnp.where(rid < 4, sar, jnp.where(rid < 6, 1.0, jnp.where(rid < 8, dyr, dsar))))
                    pending[p].append((t, jnp.where(own, rows, 0.0), [d, d * sp, s_t, sp]))
                    ds[p] = d * W + A * dsar
            for p in range(P):
                ls = slice(p * LANES, (p + 1) * LANES)
                (t1, rows1, tiles1), (t0, rows0, tiles0) = pending[p]
                g2 = nt(jnp.concatenate([rows1, rows0], axis=0), jnp.concatenate(tiles1 + tiles0, axis=0))
                for t, g in ((t1, g2[0:SCAN_G_ROWS, 0:2 * LANES]), (t0, g2[SCAN_G_ROWS:, 2 * LANES:])):
                    ga, gb = g[:, 0:LANES], g[:, LANES:2 * LANES]
                    ra, rb = pltpu.roll(ga, HEAD_DIM, 1), pltpu.roll(gb, HEAD_DIM, 1)
                    dk_ref[t:t + 1, ls] = jnp.where(lo, ga[0:1], ra[1:2])
                    db_ref[t:t + 1, ls] = jnp.where(lo, ga[2:3], ra[3:4])
                    dw_ref[t:t + 1, ls] = jnp.where(lo, ra[4:5], ga[5:6])
                    dr_ref[t:t + 1, ls] = jnp.where(lo, gb[6:7], rb[7:8])
                    da_ref[t:t + 1, ls] = jnp.where(lo, rb[8:9], gb[9:10])
        for p in range(P):
            dst[p] = ds[p]

    rev = lambda g, c: (g, nc - 1 - c, 0, 0)
    lspec = pl.BlockSpec((P, None, HEAD_DIM, Tc2), rev)
    rows = pl.BlockSpec((Tc, P * LANES), lambda g, c: ((g // npb) * nc + nc - 1 - c, g % npb))
    return pl.pallas_call(
        body, grid=(NP // P, nc),
        in_specs=[lspec] * 5 + [rows] * 3 + [pl.BlockSpec((P, Tc, HEAD_DIM, LANES), rev),
                                             pl.BlockSpec(onehot.shape, lambda g, c: (0, 0, 0))],
        out_specs=[rows] * 6, out_shape=[jax.ShapeDtypeStruct(v.shape, F32)] * 6,
        scratch_shapes=[pltpu.VMEM((P, HEAD_DIM, LANES), F32)], name="rwkv_scan_bwd",
        compiler_params=_params(("parallel", "arbitrary")),
    )(LW, LA, LB, LK, LR, v, sa, dy, sb, onehot)


def _shift_prev(p, B):
    T, W = p.shape
    return jnp.pad(p.reshape(B, T // B, W), ((0, 0), (1, 0), (0, 0)))[:, :-1].reshape(T, W)


def _shift_next(p, B):
    T, W = p.shape
    return jnp.pad(p.reshape(B, T // B, W), ((0, 0), (0, 1), (0, 0)))[:, 1:].reshape(T, W)


FOX_FWD_BLOCKS = (512, 1024)
FOX_BWD_BLOCKS = (512, 512)


def _layer_step(x, mem, target, W, late_weights=None, early_grads=None, scan_tc=SCAN_TC, fox_fwd_t=FOX_FWD_BLOCKS,
                fox_bwd_t=FOX_BWD_BLOCKS):
    B, S, _ = x.shape
    T = B * S
    x2, tgt2 = x.reshape(T, D_MODEL), target.reshape(T, D_MODEL)
    mem2 = mem.reshape(-1, D_MODEL)
    w_in_t = W["w_in"]
    wt_qkv, wt_rw, wt_mq, wt_gate = (w_in_t[lo:hi] for lo, hi in (COL_QKV, COL_RW, COL_MQ, COL_GATE))
    wt_f = jnp.pad(w_in_t[COL_F[0]:COL_F[1]], ((0, LANES - N_HEADS), (0, 0)))
    row = lambda v: v.reshape(1, -1).astype(F32)
    pre1_g, post1_g, pre2_g, post2_g, mem_g = (row(W[n]) for n in ("pre1_g", "post1_g", "pre2_g", "post2_g", "mem_norm_g"))

    u = _rowwise("rms_pre1", lambda x, g: x * _rms_stat(x) * g, [x2], [pre1_g], [(D_MODEL, BF16)])[0]
    qkv = _mm("proj_qkv", u, wt_qkv, tb=True, out_dtype=BF16)
    f_pad = _mm("proj_f", u, wt_f, tb=True)
    p_rw = _mm("proj_rwkv", u, wt_rw, tb=True)
    memq = _mm("proj_memq", u, wt_mq, tb=True, out_dtype=BF16)
    gate = _mm("proj_gate", u, wt_gate, tb=True, out_dtype=BF16)

    bias_col = W["fox_f_bias"].reshape(N_HEADS, 1).astype(F32)
    f8t = f_pad[:, :N_HEADS].reshape(B, S, N_HEADS).transpose(0, 2, 1)
    c = _fox_c_fwd(f8t, bias_col)
    G = B * N_HEADS
    q_blocks = lambda a, t: a.reshape(G, S // min(t, S), min(t, S), 1)
    k_blocks = lambda a, t: a.reshape(G, S // min(t, S), 1, min(t, S))
    fox_out, lse = _fox_fwd(qkv, k_blocks(c, fox_fwd_t[1]), B, min(fox_fwd_t[0], S))

    bd = _head_ones()
    zpad = jnp.zeros((64, BR_W), F32)
    w1 = jnp.concatenate([W["rwkv_w_up"].astype(F32), zpad], axis=0)
    w2 = jnp.concatenate([zpad, W["rwkv_a_up"].astype(F32)], axis=0)
    prep_consts = [row(W["rwkv_mu"]), row(W["rwkv_w0"]), w1, row(W["rwkv_a0"]), w2, W["rwkv_g_up"].astype(F32),
                   row(W["rwkv_k_k"]), row(W["rwkv_k_a"]), bd]
    p_prev = _shift_prev(p_rw, B)
    (rr, rk2, rv, rg), scan_cols = _rw_prep_fwd(p_rw, p_prev, prep_consts, B, scan_tc)
    ry, rsa, sb = _rw_scan_fwd(*scan_cols, rv)
    head_consts = [row(W["rwkv_gn_g"]), row(W["rwkv_gn_b"]), row(W["rwkv_r_k"]), bd]
    rwkv_out = _rw_head_fwd(ry, rr, rk2, rv, rg, head_consts)

    if late_weights is not None:
        W = {**W, **late_weights(rwkv_out)}

    mn = _rowwise("rms_mem", lambda m, g: m * _rms_stat(m) * g, [mem2], [mem_g], [(D_MODEL, BF16)])[0]
    mem_kv = _mm("proj_memkv", mn, W["w_mem_kv"], out_dtype=BF16)
    mem_out = _mem_fwd(memq, mem_kv, B)

    fo = [_mm("branch_" + n, a, W[n], tb=True, out_dtype=BF16)
          for n, a in (("w_fox_out", fox_out), ("w_rwkv_out", rwkv_out), ("w_mem_out", mem_out))]

    def merge(gate, f0, f1, f2):
        return sum(_sigmoid(gate[:, i * D_MODEL:(i + 1) * D_MODEL]) * f for i, f in enumerate((f0, f1, f2)))
    merged = _rowwise("merge", merge, [gate] + fo, [], [(D_MODEL, BF16)])[0]
    y1 = _mm("proj_o", merged, W["w_o"])

    def mid(x, y1, g1, g2):
        h1 = x + y1 * _rms_stat(y1) * g1
        return h1, h1 * _rms_stat(h1) * g2
    h1, u2 = _rowwise("norm_mid", mid, [x2, y1], [post1_g, pre2_g], [(D_MODEL, F32), (D_MODEL, BF16)])
    gt = _mm("ffn_gate", u2, W["w_ffn_gate"], tb=True, out_dtype=BF16)
    up = _mm("ffn_up", u2, W["w_ffn_up"], tb=True, out_dtype=BF16)
    act = _rowwise("swiglu", lambda gt, up: gt * _sigmoid(gt) * up, [gt, up], [], [(D_FF, BF16)])[0]
    ffn = _mm("ffn_down", act, W["w_ffn_down"])

    def tail(h1, ffn, tgt, g):
        err = h1 + ffn * _rms_stat(ffn) * g - tgt
        dh2 = err * (1.0 / D_MODEL)
        dffn, dg = _rms_bwd(dh2, ffn, g)
        loss = 0.5 * jnp.sum(jnp.sum(err * err, axis=1, keepdims=True) * (1.0 / D_MODEL), axis=0, keepdims=True)
        return dh2, dffn, dg, jnp.broadcast_to(loss, (1, LANES))
    dh2, dffn, d_post2, loss = _rowwise("loss_tail", tail, [h1, ffn, tgt2], [post2_g], [(D_MODEL, F32), (D_MODEL, BF16)],
                                        [((1, D_MODEL), F32), ((1, LANES), F32)])
    gw = {"post2_g": d_post2}
    dact = _mm("d_act", dffn, W["w_ffn_down"], tb=True, out_dtype=BF16)
    gw["w_ffn_down"] = _mm("g_ffn_down", act, dffn, ta=True, out_dtype=BF16)

    def swiglu_bwd(dact, gt, up):
        s = _sigmoid(gt)
        return dact * up * s * (1.0 + gt * (1.0 - s)), dact * gt * s
    dgt, dup = _rowwise("swiglu_bwd", swiglu_bwd, [dact, gt, up], [], [(D_FF, BF16)] * 2)
    du2 = _mm("d_u2_gate", dgt, W["w_ffn_gate"])
    du2 = _mm("d_u2_up", dup, W["w_ffn_up"], add=du2)
    gw["w_ffn_gate"] = _mm("g_ffn_gate", dgt, u2, ta=True, out_dtype=BF16)
    gw["w_ffn_up"] = _mm("g_ffn_up", dup, u2, ta=True, out_dtype=BF16)

    def mid_bwd(du2, dh2, h1, y1, g1, g2):
        dh1_n, d_pre2 = _rms_bwd(du2, h1, g2)
        dh1 = dh2 + dh1_n
        dy1, d_post1 = _rms_bwd(dh1, y1, g1)
        return dh1, dy1, d_post1, d_pre2
    dh1, dy1, gw["post1_g"], gw["pre2_g"] = _rowwise(
        "norm_mid_bwd", mid_bwd, [du2, dh2, h1, y1], [post1_g, pre2_g], [(D_MODEL, F32), (D_MODEL, BF16)],
        [((1, D_MODEL), F32)] * 2)
    dmerged = _mm("d_merged", dy1, W["w_o"], tb=True, out_dtype=BF16)
    gw["w_o"] = _mm("g_w_o", merged, dy1, ta=True, out_dtype=BF16)

    def merge_bwd(dm, gate, f0, f1, f2):
        s = [_sigmoid(gate[:, i * D_MODEL:(i + 1) * D_MODEL]) for i in range(3)]
        dgate = jnp.concatenate([dm * f * si * (1.0 - si) for f, si in zip((f0, f1, f2), s)], axis=1)
        return dm * s[0], dm * s[1], dm * s[2], dgate
    dfo0, dfo1, dfo2, dgate = _rowwise("merge_bwd", merge_bwd, [dmerged, gate] + fo, [],
                                       [(D_MODEL, BF16)] * 3 + [(3 * D_MODEL, BF16)])
    d_branch = {}
    for n, a, dfo in (("w_fox_out", fox_out, dfo0), ("w_rwkv_out", rwkv_out, dfo1), ("w_mem_out", mem_out, dfo2)):
        d_branch[n] = _mm("d_in_" + n, dfo, W[n], out_dtype=BF16)
        gw[n] = _mm("g_" + n, dfo, a, ta=True, out_dtype=BF16)

    dmemq, dkm, dvm = _mem_bwd(memq, mem_kv, d_branch["w_mem_out"], B)
    dmem_kv = jnp.concatenate([dkm, dvm], axis=1)
    gw["w_mem_kv"] = _mm("g_w_mem_kv", mn, dmem_kv, ta=True, out_dtype=BF16)
    dmn = _mm("d_mn", dmem_kv, W["w_mem_kv"], tb=True)
    gw["mem_norm_g"] = _rowwise("rms_mem_bwd", lambda d, m, g: _rms_bwd(d, m, g)[1], [dmn, mem2], [mem_g], [],
                                [((1, D_MODEL), F32)])[0]
    if early_grads is not None:
        c = early_grads(gw, c)

    dfq, dfk, dfv, dck, dcq = _fox_bwd(qkv, fox_out, d_branch["w_fox_out"], k_blocks(c, fox_bwd_t[1]),
                                       q_blocks(lse, fox_bwd_t[0]), B)
    df8t, dbias = _fox_c_bwd(dck.reshape(B, N_HEADS, S) + dcq.reshape(B, N_HEADS, S), f8t, bias_col)
    gw["fox_f_bias"] = jnp.sum(dbias, axis=0).reshape(1, N_HEADS)
    dqkv = jnp.concatenate([dfq, dfk, dfv], axis=1)
    df_pad = jnp.pad(df8t.transpose(0, 2, 1).reshape(T, N_HEADS), ((0, 0), (0, LANES - N_HEADS))).astype(BF16)

    dry, drg, dr_h, dk2_h, dv_h, gw["rwkv_gn_g"], gw["rwkv_gn_b"], gw["rwkv_r_k"] = _rw_head_bwd(
        d_branch["w_rwkv_out"], ry, rr, rk2, rv, rg, head_consts)
    dv_s, dk2_s, db_s, dw_s, dr_s, da_s = _rw_scan_bwd(*scan_cols, rv, rsa, dry, sb)
    dP, dPp, gw["rwkv_mu"], gw["rwkv_w0"], d_w1, gw["rwkv_a0"], d_w2, gw["rwkv_g_up"], gw["rwkv_k_k"], gw["rwkv_k_a"] = \
        _rw_prep_bwd(p_rw, p_prev, [dr_s, dr_h, dw_s, dk2_s, dk2_h, dv_s, dv_h, da_s, db_s, drg], prep_consts)
    gw["rwkv_w_up"], gw["rwkv_a_up"] = d_w1[:64], d_w2[64:]
    dp_rw = dP + _shift_next(dPp, B)

    du = _mm("d_u_qkv", dqkv, wt_qkv)
    du = _mm("d_u_f", df_pad, wt_f, add=du)
    du = _mm("d_u_rwkv", dp_rw, wt_rw, add=du)
    du = _mm("d_u_memq", dmemq, wt_mq, add=du)
    du = _mm("d_u_gate", dgate, wt_gate, add=du)
    gw["w_in"] = jnp.concatenate(
        [_mm("g_w_qkv", dqkv, u, ta=True, out_dtype=BF16), _mm("g_w_f", df_pad, u, ta=True, out_dtype=BF16)[:N_HEADS],
         _mm("g_w_rwkv", dp_rw, u, ta=True, out_dtype=BF16), _mm("g_w_memq", dmemq, u, ta=True, out_dtype=BF16),
         _mm("g_w_gate", dgate, u, ta=True, out_dtype=BF16)], axis=0)

    def pre1_bwd(du, dh1, x, g):
        dx, dg = _rms_bwd(du, x, g)
        return dh1 + dx, dg
    dx, gw["pre1_g"] = _rowwise("rms_pre1_bwd", pre1_bwd, [du, dh1, x2], [pre1_g], [(D_MODEL, F32)], [((1, D_MODEL), F32)])
    return loss[0, 0], dx.reshape(B, S, D_MODEL), gw


TRANSPOSED = ("w_in", "w_ffn_gate", "w_ffn_up", "w_fox_out", "w_rwkv_out", "w_mem_out")
LORA = ("rwkv_w_up", "rwkv_a_up", "rwkv_g_up")
ROW_SHARDED = ("w_mem_kv", "w_o", "w_ffn_down")
FIRST = ("w_in",) + LORA
LATER = ("w_ffn_gate", "w_ffn_up", "w_mem_kv", "w_o", "w_ffn_down", "w_fox_out", "w_rwkv_out", "w_mem_out")
BIG = FIRST + LATER
SMALL = ("pre1_g", "post1_g", "pre2_g", "post2_g", "mem_norm_g", "fox_f_bias", "rwkv_mu", "rwkv_w0", "rwkv_a0", "rwkv_k_k",
         "rwkv_k_a", "rwkv_r_k", "rwkv_gn_g", "rwkv_gn_b")
WEIGHTS = ("pre1_g", "post1_g", "pre2_g", "post2_g", "mem_norm_g", "w_in", "fox_f_bias", "rwkv_mu", "rwkv_w0", "rwkv_w_up",
           "rwkv_a0", "rwkv_a_up", "rwkv_g_up", "rwkv_k_k", "rwkv_k_a", "rwkv_r_k", "rwkv_gn_g", "rwkv_gn_b", "w_mem_kv",
           "w_fox_out", "w_rwkv_out", "w_mem_out", "w_o", "w_ffn_gate", "w_ffn_up", "w_ffn_down")
WIRE_W = 1024
WIRE_ROW_ALIGN = 16
WIRE_HALF_ALIGN = 128


def _wire_rows(name, shard_shape):
    r, c = shard_shape
    if name in ROW_SHARDED:
        return r
    return -(-c // WIRE_ROW_ALIGN) * WIRE_ROW_ALIGN if r == WIRE_W else (r * c) // WIRE_W


def _to_wire(name, a):
    if name not in ROW_SHARDED:
        a = jnp.swapaxes(a, -1, -2)
    lead, (n, w) = a.shape[:-2], a.shape[-2:]
    if w != WIRE_W:
        return a.reshape(lead + ((n * w) // WIRE_W, WIRE_W))
    return jnp.pad(a, [(0, 0)] * len(lead) + [(0, (-n) % WIRE_ROW_ALIGN), (0, 0)])


def _from_wire(name, a, shard_shape):
    r, c = shard_shape
    if name in ROW_SHARDED:
        return a
    return a[..., :c, :] if r == WIRE_W else a.reshape(a.shape[:-2] + (c, r))


def _wire_layout(shard_shapes, names):
    layout, off = {}, 0
    for n in names:
        rows = _wire_rows(n, shard_shapes[n])
        layout[n] = (off, rows)
        off += rows
    return layout, -(-off // (2 * WIRE_HALF_ALIGN)) * WIRE_HALF_ALIGN


def _wire_pack(blocks, half_rows):
    a = jnp.concatenate(blocks, axis=-2)
    lead = a.shape[:-2]
    a = jnp.pad(a, [(0, 0)] * len(lead) + [(0, 2 * half_rows - a.shape[-2]), (0, 0)])
    return a.reshape(lead + (2, half_rows, WIRE_W))


def _my_place():
    return lax.axis_index("x"), lax.axis_index("y"), lax.axis_index("c")


def _other_chips(x, y):
    return [(1 - x, y), (x, 1 - y), (1 - x, 1 - y)]


ANY = pl.BlockSpec(memory_space=pl.ANY)


def _gather_weights(packed):
    _, R, L = packed.shape
    me = 2 * lax.axis_index("x") + lax.axis_index("y")
    base = lax.dynamic_update_index_in_dim(jnp.zeros((N_CHIPS, 2, R, L), packed.dtype), packed, me, 0)

    def body(in_ref, base_ref, out_ref, send_sems, recv_sems):
        x, y, c = _my_place()
        chip = lambda px, py: 2 * px + py
        sibling = (x, y, 1 - c)
        others = _other_chips(x, y)

        def copy(k, src, dst, to):
            return pltpu.make_async_remote_copy(src_ref=src, dst_ref=dst, send_sem=send_sems.at[k], recv_sem=recv_sems.at[k],
                                                device_id=to, device_id_type=MESH)

        sends = [copy(j, in_ref.at[c], out_ref.at[chip(x, y), c], (px, py, c)) for j, (px, py) in enumerate(others)]
        for cp in sends:
            cp.start()
        passed = [copy(3 + j, out_ref.at[chip(px, py), c], out_ref.at[chip(px, py), c], sibling)
                  for j, (px, py) in enumerate(others)]
        for j, (px, py) in enumerate(others):
            copy(j, in_ref.at[c], out_ref.at[chip(px, py), c], (px, py, c)).wait_recv()
            passed[j].start()
        for j, (px, py) in enumerate(others):
            copy(3 + j, in_ref.at[1 - c], out_ref.at[chip(px, py), 1 - c], sibling).wait_recv()
        for cp in sends + passed:
            cp.wait_send()

    return pl.pallas_call(
        body, out_shape=jax.ShapeDtypeStruct(base.shape, base.dtype), in_specs=[ANY, ANY], out_specs=ANY,
        input_output_aliases={1: 0}, scratch_shapes=[pltpu.SemaphoreType.DMA((6,)), pltpu.SemaphoreType.DMA((6,))],
        name="gather_weights",
    )(packed, base)


HBM_SPEC = pl.BlockSpec(memory_space=pltpu.HBM)
SEM_SPEC = pl.BlockSpec(memory_space=pltpu.SEMAPHORE)
DATAFLOW = pltpu.SideEffectType.DATAFLOW_SIDE_EFFECTING


def _in_hbm(a):
    return pltpu.with_memory_space_constraint(a, pltpu.HBM)


def _split_start(name, n_copies, src, land, start_copies):
    def body(src_ref, land_ref, send_sems, recv_sems, src_thru, land_thru, token):
        start_copies(src_ref, land_ref, send_sems, recv_sems)
        token[...] = jnp.zeros(token.shape, token.dtype)

    return pl.pallas_call(
        body, name=name,
        out_shape=(pltpu.SemaphoreType.DMA((n_copies,)), pltpu.SemaphoreType.DMA((n_copies,)), pltpu.HBM(src.shape, src.dtype),
                   pltpu.HBM(land.shape, land.dtype), jax.ShapeDtypeStruct((8, LANES), F32)),
        in_specs=(HBM_SPEC, HBM_SPEC),
        out_specs=(SEM_SPEC, SEM_SPEC, HBM_SPEC, HBM_SPEC, pl.BlockSpec(memory_space=pltpu.VMEM)),
        input_output_aliases={0: 2, 1: 3}, compiler_params=pltpu.CompilerParams(has_side_effects=DATAFLOW),
    )(_in_hbm(src), _in_hbm(land))


def _split_wait(name, handle, after, wait_copies):
    send_sems, recv_sems, src, land = handle[:4]

    def body(src_ref, land_ref, send_sems, recv_sems, after_ref, src_dead, land_out):
        wait_copies(src_ref, land_ref, send_sems, recv_sems)

    return pl.pallas_call(
        body, name=name, out_shape=(pltpu.HBM(src.shape, src.dtype), pltpu.HBM(land.shape, land.dtype)),
        in_specs=(HBM_SPEC, HBM_SPEC, SEM_SPEC, SEM_SPEC, ANY), out_specs=(HBM_SPEC, HBM_SPEC),
        input_output_aliases={0: 0, 1: 1}, compiler_params=pltpu.CompilerParams(has_side_effects=DATAFLOW),
    )(src, land, send_sems, recv_sems, after)[1]


def _late_gather_copies(src_ref, land_ref, send_sems, recv_sems):
    x, y, c = _my_place()
    me = 2 * x + y
    started, awaited = [], []
    for j, (px, py) in enumerate(_other_chips(x, y)):
        for core in range(2):
            started.append(pltpu.make_async_remote_copy(
                src_ref=src_ref.at[c], dst_ref=land_ref.at[me, c], send_sem=send_sems.at[2 * j + core],
                recv_sem=recv_sems.at[2 * j + c], device_id=(px, py, core), device_id_type=MESH))
            awaited.append(pltpu.make_async_remote_copy(
                src_ref=src_ref.at[core], dst_ref=land_ref.at[2 * px + py, core], send_sem=send_sems.at[2 * j + core],
                recv_sem=recv_sems.at[2 * j + core], device_id=(px, py, core), device_id_type=MESH))
    return started, awaited


def _gather_late_start(packed):
    _, R, L = packed.shape
    me = 2 * lax.axis_index("x") + lax.axis_index("y")
    land = lax.dynamic_update_index_in_dim(jnp.zeros((N_CHIPS, 2, R, L), packed.dtype), packed, me, 0)

    def start(src_ref, land_ref, send_sems, recv_sems):
        for cp in _late_gather_copies(src_ref, land_ref, send_sems, recv_sems)[0]:
            cp.start()

    return _split_start("gather_late_start", 6, packed, land, start)


def _gather_late_wait(handle, after):
    def wait(src_ref, land_ref, send_sems, recv_sems):
        started, awaited = _late_gather_copies(src_ref, land_ref, send_sems, recv_sems)
        for cp in started:
            cp.wait_send()
        for cp in awaited:
            cp.wait_recv()

    return _split_wait("gather_late_wait", handle, after, wait)


def _early_scatter_copies(src_ref, land_ref, send_sems, recv_sems):
    x, y, c = _my_place()
    me = 4 * x + 2 * y + c
    copies = []
    for k in range(1, N_DEV):
        px, py, pc = x ^ (k >> 2), y ^ ((k >> 1) & 1), c ^ (k & 1)
        copies.append(pltpu.make_async_remote_copy(
            src_ref=src_ref.at[2 * px + py, pc], dst_ref=land_ref.at[me], send_sem=send_sems.at[k - 1],
            recv_sem=recv_sems.at[k - 1], device_id=(px, py, pc), device_id_type=MESH))
    return copies


def _scatter_early_start(parts):
    n, _, R, L = parts.shape
    x, y, c = _my_place()
    own = lax.dynamic_index_in_dim(lax.dynamic_index_in_dim(parts, 2 * x + y, 0, keepdims=False), c, 0, keepdims=False)
    land = lax.dynamic_update_index_in_dim(jnp.zeros((N_DEV, R, L), parts.dtype), own, 4 * x + 2 * y + c, 0)

    def start(src_ref, land_ref, send_sems, recv_sems):
        for cp in _early_scatter_copies(src_ref, land_ref, send_sems, recv_sems):
            cp.start()

    return _split_start("scatter_early_start", N_DEV - 1, parts, land, start)


def _scatter_early_wait(handle, after):
    def wait(src_ref, land_ref, send_sems, recv_sems):
        for cp in _early_scatter_copies(src_ref, land_ref, send_sems, recv_sems):
            cp.wait_send()
            cp.wait_recv()

    return _split_wait("scatter_early_wait", handle, after, wait)


def _sum_slots(parts):
    n, R, L = parts.shape
    tr = _tile_rows(R, WIRE_HALF_ALIGN)

    def body(p_ref, o_ref):
        acc = p_ref[0].astype(F32)
        for i in range(1, n):
            acc = acc + p_ref[i].astype(F32)
        o_ref[...] = acc

    return pl.pallas_call(
        body, grid=(R // tr,), in_specs=[pl.BlockSpec((n, tr, L), lambda i: (0, i, 0))],
        out_specs=pl.BlockSpec((tr, L), lambda i: (i, 0)), out_shape=jax.ShapeDtypeStruct((R, L), F32), name="sum_slots",
        compiler_params=_params(("parallel",)),
    )(parts)


def _pair_exchange(parts):
    n, _, R, L = parts.shape

    def body(in_ref, out_ref, send_sems, recv_sems):
        x, y, c = _my_place()
        copies = [pltpu.make_async_remote_copy(src_ref=in_ref.at[s, 1 - c], dst_ref=out_ref.at[s], send_sem=send_sems.at[s],
                                               recv_sem=recv_sems.at[s], device_id=(x, y, 1 - c), device_id_type=MESH)
                  for s in range(n)]
        for cp in copies:
            cp.start()
        for cp in copies:
            cp.wait()

    return pl.pallas_call(
        body, out_shape=jax.ShapeDtypeStruct((n, R, L), parts.dtype), in_specs=[ANY], out_specs=ANY,
        scratch_shapes=[pltpu.SemaphoreType.DMA((n,)), pltpu.SemaphoreType.DMA((n,))], name="pair_exchange",
    )(parts)


def _pair_add(a, b):
    n, R, L = a.shape
    tr = _tile_rows(R, WIRE_HALF_ALIGN)

    def body(a_ref, b_ref, o_ref):
        o_ref[...] = (a_ref[...].astype(F32) + b_ref[...].astype(F32)).astype(o_ref.dtype)

    spec = pl.BlockSpec((n, tr, L), lambda i: (0, i, 0))
    return pl.pallas_call(
        body, grid=(R // tr,), in_specs=[spec, spec], out_specs=spec, out_shape=jax.ShapeDtypeStruct(a.shape, a.dtype),
        name="pair_add", compiler_params=_params(("parallel",)),
    )(a, b)


def _scatter_grads(parts):
    n, R, L = parts.shape

    def body(in_ref, out_ref, send_sems, recv_sems):
        x, y, c = _my_place()
        copies = [pltpu.make_async_remote_copy(src_ref=in_ref.at[2 * px + py], dst_ref=out_ref.at[j], send_sem=send_sems.at[j],
                                               recv_sem=recv_sems.at[j], device_id=(px, py, c), device_id_type=MESH)
                  for j, (px, py) in enumerate(_other_chips(x, y))]
        for cp in copies:
            cp.start()
        for cp in copies:
            cp.wait()

    return pl.pallas_call(
        body, out_shape=jax.ShapeDtypeStruct((n - 1, R, L), parts.dtype), in_specs=[ANY], out_specs=ANY,
        scratch_shapes=[pltpu.SemaphoreType.DMA((3,)), pltpu.SemaphoreType.DMA((3,))], name="scatter_grads",
    )(parts)


def _sum_chips(own, others):
    n, R, L = others.shape
    tr = _tile_rows(R, WIRE_HALF_ALIGN)

    def body(a_ref, p_ref, o_ref):
        acc = a_ref[...].astype(F32)
        for i in range(n):
            acc = acc + p_ref[i].astype(F32)
        o_ref[...] = acc

    return pl.pallas_call(
        body, grid=(R // tr,), in_specs=[pl.BlockSpec((tr, L), lambda i: (i, 0)), pl.BlockSpec((n, tr, L), lambda i: (0, i, 0))],
        out_specs=pl.BlockSpec((tr, L), lambda i: (i, 0)), out_shape=jax.ShapeDtypeStruct((R, L), F32), name="sum_grads",
        compiler_params=_params(("parallel",)),
    )(own, others)


def _tile_rows(R, cap=2048):
    best = 8
    for t in range(8, min(R, cap) + 1, 8):
        if R % t == 0:
            best = t
    return best if R % 8 == 0 else R


def _swap_halves(half):
    def body(in_ref, out_ref, send_sem, recv_sem):
        x, y, c = _my_place()
        cp = pltpu.make_async_remote_copy(src_ref=in_ref, dst_ref=out_ref, send_sem=send_sem, recv_sem=recv_sem,
                                          device_id=(x, y, 1 - c), device_id_type=MESH)
        cp.start()
        cp.wait()

    return pl.pallas_call(
        body, out_shape=jax.ShapeDtypeStruct(half.shape, half.dtype), in_specs=[ANY], out_specs=ANY,
        scratch_shapes=[pltpu.SemaphoreType.DMA, pltpu.SemaphoreType.DMA], name="swap_halves",
    )(half)


def _allreduce_small(v):
    R, L = v.shape

    def body(in_ref, out_ref, buf, send_sems, recv_sems):
        x, y, c = _my_place()
        me = 4 * x + 2 * y + c
        buf[me] = in_ref[...]
        started = []
        for k in range(1, N_DEV):
            to = (x ^ (k >> 2), y ^ ((k >> 1) & 1), c ^ (k & 1))
            cp = pltpu.make_async_remote_copy(src_ref=in_ref, dst_ref=buf.at[me], send_sem=send_sems.at[k - 1],
                                              recv_sem=recv_sems.at[k - 1], device_id=to, device_id_type=MESH)
            cp.start()
            started.append(cp)
        for cp in started:
            cp.wait()
        acc = buf[0]
        for i in range(1, N_DEV):
            acc = acc + buf[i]
        out_ref[...] = acc

    vm = pl.BlockSpec(memory_space=pltpu.VMEM)
    return pl.pallas_call(
        body, out_shape=jax.ShapeDtypeStruct((R, L), F32), in_specs=[vm], out_specs=vm,
        scratch_shapes=[pltpu.VMEM((N_DEV, R, L), F32), pltpu.SemaphoreType.DMA((7,)), pltpu.SemaphoreType.DMA((7,))],
        name="allreduce_small",
    )(v)


def _adamw(name, w, g, m, v):
    shape = w.shape
    C = shape[-1]
    R = int(np.prod(shape[:-1]))
    args = [a.reshape(R, C).astype(F32) for a in (w, g, m, v)]
    if R % 8 == 0 or C % LANES != 0:
        tr, tc = _tile_rows(R, 256), C
    else:
        tr, tc = R, LANES

    def body(w_ref, g_ref, m_ref, v_ref, d_ref, nm_ref, nv_ref):
        g = g_ref[...]
        m = ADAM_B1 * m_ref[...] + (1.0 - ADAM_B1) * g
        v = ADAM_B2 * v_ref[...] + (1.0 - ADAM_B2) * (g * g)
        m_hat = m / (1.0 - ADAM_B1 ** ADAM_STEP)
        v_hat = v / (1.0 - ADAM_B2 ** ADAM_STEP)
        d_ref[...] = -ADAM_LR * (m_hat / (jnp.sqrt(v_hat) + ADAM_EPS) + ADAM_WD * w_ref[...])
        nm_ref[...] = m
        nv_ref[...] = v

    spec = pl.BlockSpec((tr, tc), lambda i, j: (i, j))
    out = pl.pallas_call(
        body, grid=(R // tr, C // tc), in_specs=[spec] * 4, out_specs=[spec] * 3,
        out_shape=[jax.ShapeDtypeStruct((R, C), F32)] * 3, name="adamw_" + name,
        compiler_params=_params(("parallel", "parallel")),
    )(*args)
    return [o.reshape(shape) for o in out]


def kernel(x, mem, pre1_g, post1_g, pre2_g, post2_g, mem_norm_g, w_in, fox_f_bias, rwkv_mu, rwkv_w0, rwkv_w_up, rwkv_a0, rwkv_a_up, rwkv_g_up, rwkv_k_k, rwkv_k_a, rwkv_r_k, rwkv_gn_g, rwkv_gn_b, w_mem_kv, w_fox_out, w_rwkv_out, w_mem_out, w_o, w_ffn_gate, w_ffn_up, w_ffn_down, loss_target, m_pre1_g, m_post1_g, m_pre2_g, m_post2_g, m_mem_norm_g, m_w_in, m_fox_f_bias, m_rwkv_mu, m_rwkv_w0, m_rwkv_w_up, m_rwkv_a0, m_rwkv_a_up, m_rwkv_g_up, m_rwkv_k_k, m_rwkv_k_a, m_rwkv_r_k, m_rwkv_gn_g, m_rwkv_gn_b, m_w_mem_kv, m_w_fox_out, m_w_rwkv_out, m_w_mem_out, m_w_o, m_w_ffn_gate, m_w_ffn_up, m_w_ffn_down, v_pre1_g, v_post1_g, v_pre2_g, v_post2_g, v_mem_norm_g, v_w_in, v_fox_f_bias, v_rwkv_mu, v_rwkv_w0, v_rwkv_w_up, v_rwkv_a0, v_rwkv_a_up, v_rwkv_g_up, v_rwkv_k_k, v_rwkv_k_a, v_rwkv_r_k, v_rwkv_gn_g, v_rwkv_gn_b, v_w_mem_kv, v_w_fox_out, v_w_rwkv_out, v_w_mem_out, v_w_o, v_w_ffn_gate, v_w_ffn_up, v_w_ffn_down):
    given = dict(locals())
    w_loc = {n: given[n] for n in WEIGHTS}
    m_loc = {n: given["m_" + n] for n in WEIGHTS}
    v_loc = {n: given["v_" + n] for n in WEIGHTS}

    shard_shapes = {n: tuple(w_loc[n].shape[1:]) for n in BIG}
    groups = {names: _wire_layout(shard_shapes, names) for names in (FIRST, LATER)}
    core = lax.axis_index("c")

    def pack_weights(names):
        return _wire_pack([_to_wire(n, w_loc[n][0].astype(BF16)) for n in names], groups[names][1])

    def unpack_weights(gathered, names):
        layout, half_rows = groups[names]
        gathered = gathered.reshape(N_CHIPS, 2 * half_rows, WIRE_W)
        out = {}
        for n in names:
            off, rows = layout[n]
            blocks = _from_wire(n, gathered[:, off:off + rows], shard_shapes[n])
            if n in LORA:
                out[n] = blocks.transpose(2, 0, 1).reshape(blocks.shape[2], -1)
            else:
                out[n] = blocks.reshape(-1, blocks.shape[2])
        return out

    def pack_grads(gw, names):
        blocks = []
        for n in names:
            r, c = shard_shapes[n]
            g = gw[n].astype(BF16)
            if n in LORA:
                g = g.reshape(r, N_CHIPS, c).transpose(1, 0, 2)
            elif n in TRANSPOSED:
                g = jnp.swapaxes(g.reshape(N_CHIPS, c, r), 1, 2)
            else:
                g = g.reshape(N_CHIPS, r, c)
            blocks.append(_to_wire(n, g))
        return _wire_pack(blocks, groups[names][1])

    def unpack_grads(half, other, names):
        layout, _ = groups[names]
        reduced = jnp.where(core == 0, jnp.concatenate([half, other]), jnp.concatenate([other, half]))
        out = {}
        for n in names:
            off, rows = layout[n]
            g = _from_wire(n, reduced[off:off + rows], shard_shapes[n])
            out[n] = g.T if n in LORA else g
        return out

    first = _gather_weights(pack_weights(FIRST))
    late = _gather_late_start(pack_weights(LATER) + (first[0, 0, 0, 0] * 0).astype(BF16))
    W = unpack_weights(first, FIRST)
    W.update({n: w_loc[n][0] for n in SMALL})
    W["pre1_g"] = W["pre1_g"] + late[4][0, 0]
    early = []

    def late_weights(after):
        return unpack_weights(_gather_late_wait(late, after), LATER)

    def early_grads(gw, thru):
        early.append(_scatter_early_start(pack_grads(gw, LATER)))
        return thru + early[0][4][0, 0].astype(thru.dtype)

    loss, grad_x, gw = _layer_step(x, mem, loss_target, W, late_weights, early_grads)

    packed = pack_grads(gw, FIRST)
    own_halves = lax.dynamic_index_in_dim(packed, core, axis=1, keepdims=False)
    chip_sums = _pair_add(own_halves, _pair_exchange(packed))
    own_chip = lax.dynamic_index_in_dim(chip_sums, 2 * lax.axis_index("x") + lax.axis_index("y"), axis=0, keepdims=False)
    half_first = _sum_chips(own_chip, _scatter_grads(chip_sums))
    half_later = _sum_slots(_scatter_early_wait(early[0], half_first))
    rows_first = groups[FIRST][1]
    other = _swap_halves(jnp.concatenate([half_first, half_later]))
    g_shard = {**unpack_grads(half_first, other[:rows_first], FIRST), **unpack_grads(half_later, other[rows_first:], LATER)}

    small_shapes = [w_loc[n].shape[1:] for n in SMALL] + [(1,)]
    n_small = sum(int(np.prod(s)) for s in small_shapes)
    small_rows = -(-n_small // (8 * LANES)) * 8
    flat = jnp.concatenate([gw[n].reshape(-1) for n in SMALL] + [loss.reshape(1)])
    flat = jnp.pad(flat, (0, small_rows * LANES - n_small)).reshape(small_rows, LANES).reshape(-1)
    small, off = [], 0
    flat = _allreduce_small(flat.reshape(small_rows, LANES)).reshape(-1)
    for s in small_shapes:
        cnt = int(np.prod(s))
        small.append(flat[off:off + cnt].reshape(s))
        off += cnt
    g_small = dict(zip(SMALL, small[:-1]))
    loss = small[-1][0]

    grads, deltas, new_m, new_v = [], [], [], []
    flip = lambda a: jnp.swapaxes(a, -1, -2)
    for n in WEIGHTS:
        if n in TRANSPOSED:
            g_t = g_shard[n][None]
            d, nm, nv = (flip(o) for o in _adamw(n, flip(w_loc[n]), g_t, flip(m_loc[n]), flip(v_loc[n])))
            g = flip(g_t)
        else:
            g = (g_shard[n] if n in g_shard else g_small[n]).reshape(w_loc[n].shape)
            d, nm, nv = _adamw(n, w_loc[n], g, m_loc[n], v_loc[n])
        grads.append(g)
        deltas.append(d)
        new_m.append(nm)
        new_v.append(nv)
    return (loss, grad_x, *grads, *deltas, *new_m, *new_v)
```

```python
import numpy as np
import jax
import jax.numpy as jnp
from jax import lax
from jax.experimental import pallas as pl
from jax.experimental.pallas import tpu as pltpu

F32, BF16 = jnp.float32, jnp.bfloat16
MESH = pl.DeviceIdType.MESH

D_MODEL = 1024
HEAD_DIM = 64
N_HEADS = 8
BR_W = 512
MEM_HEADS = 4
MEM_HEAD_DIM = 128
D_FF = 2816
NORM_EPS = 1e-6
GN_EPS = 64e-5
N_CHIPS = 4
N_DEV = 8
LANES = 128
VMEM_LIMIT = 48 * 1024 * 1024

ADAM_LR, ADAM_B1, ADAM_B2, ADAM_EPS, ADAM_WD, ADAM_STEP = 0.001, 0.9, 0.999, 1e-08, 0.01, 10

FOX_COLS = 3 * BR_W + N_HEADS
RWKV_COLS = 3 * BR_W + 64 + 64 + 128
COL_QKV = (0, 3 * BR_W)
COL_F = (3 * BR_W, FOX_COLS)
COL_RW = (FOX_COLS, FOX_COLS + RWKV_COLS)
COL_MQ = (COL_RW[1], COL_RW[1] + BR_W)
COL_GATE = (COL_MQ[1], COL_MQ[1] + 3 * D_MODEL)

NT_DIMS = (((1,), (1,)), ((), ()))
TN_DIMS = (((0,), (0,)), ((), ()))


def _params(sem=None, **kw):
    return pltpu.CompilerParams(dimension_semantics=sem, vmem_limit_bytes=VMEM_LIMIT, **kw)


def _sigmoid(x):
    return 1.0 / (1.0 + jnp.exp(-x))


def _log_sigmoid(x):
    return jnp.minimum(x, 0.0) - jnp.log(1.0 + jnp.exp(-jnp.abs(x)))


def _bdot(a, b, dims=None):
    a, b = a.astype(BF16), b.astype(BF16)
    if dims is None:
        return jnp.dot(a, b, preferred_element_type=F32)
    return lax.dot_general(a, b, dims, preferred_element_type=F32)


def _hdot(a, ones, terms=2):
    ones = ones.astype(BF16)
    dot = lambda x: jnp.dot(x, ones, preferred_element_type=F32)
    hi = a.astype(BF16)
    r1 = a - hi.astype(F32)
    mid = r1.astype(BF16)
    if terms == 2:
        return dot(hi) + dot(mid)
    return dot(hi) + dot(mid) + dot((r1 - mid.astype(F32)).astype(BF16))


def _tile(n, cap):
    best = None
    for t in range(LANES, min(n, cap) + 1, LANES):
        if n % t == 0:
            best = t
    return best or n


def _rowwise(name, fn, rows, consts, outs, accs=(), tm=256):
    T = rows[0].shape[0]
    tm = min(tm, T)
    assert T % tm == 0
    nr, nc, no, na = len(rows), len(consts), len(outs), len(accs)

    def body(*refs):
        res = fn(*[r[...].astype(F32) for r in refs[:nr + nc]])
        if not isinstance(res, (tuple, list)):
            res = (res,)
        orefs, arefs = refs[nr + nc:nr + nc + no], refs[nr + nc + no:]
        for ref, val in zip(orefs, res[:no]):
            ref[...] = val.astype(ref.dtype)
        if na:
            @pl.when(pl.program_id(0) == 0)
            def _():
                for ref in arefs:
                    ref[...] = jnp.zeros(ref.shape, ref.dtype)
            for ref, val in zip(arefs, res[no:]):
                ref[...] += val

    in_specs = ([pl.BlockSpec((tm, r.shape[1]), lambda i: (i, 0)) for r in rows]
                + [pl.BlockSpec(c.shape, lambda i: (0, 0)) for c in consts])
    out_specs = ([pl.BlockSpec((tm, w), lambda i: (i, 0)) for w, _ in outs]
                 + [pl.BlockSpec(s, lambda i: (0, 0)) for s, _ in accs])
    out_shape = ([jax.ShapeDtypeStruct((T, w), dt) for w, dt in outs]
                 + [jax.ShapeDtypeStruct(s, dt) for s, dt in accs])
    return pl.pallas_call(
        body, grid=(T // tm,), in_specs=in_specs, out_specs=out_specs, out_shape=out_shape, name=name,
        compiler_params=_params(("arbitrary",) if na else ("parallel",)),
    )(*rows, *consts)


MM_TILE_CAP = 1408
MM_WHOLE_K = 2048


def _mm(name, a, b, ta=False, tb=False, out_dtype=F32, add=None):
    M, K = (a.shape[1], a.shape[0]) if ta else a.shape
    K2, N = (b.shape[1], b.shape[0]) if tb else b.shape
    assert K == K2
    tm, tn = _tile(M, MM_TILE_CAP), _tile(N, MM_TILE_CAP)
    tk = K if K <= MM_WHOLE_K else _tile(K, MM_TILE_CAP)
    assert M % tm == 0 and N % tn == 0 and K % tk == 0
    nk = K // tk
    a_dim, b_dim = (0 if ta else 1), (1 if tb else 0)

    def body(*refs):
        a_ref, b_ref = refs[0], refs[1]
        n_in = 2 if add is None else 3
        o_ref, acc = refs[n_in], (refs[n_in + 1] if nk > 1 else None)
        k = pl.program_id(2)
        part = lax.dot_general(a_ref[...].astype(BF16), b_ref[...].astype(BF16),
                               (((a_dim,), (b_dim,)), ((), ())), preferred_element_type=F32)

        def finish(r):
            if add is not None:
                r = r + refs[2][...].astype(F32)
            o_ref[...] = r.astype(o_ref.dtype)

        if nk == 1:
            finish(part)
            return

        @pl.when(k == 0)
        def _():
            acc[...] = part

        @pl.when(k > 0)
        def _():
            acc[...] += part

        @pl.when(k == nk - 1)
        def _():
            finish(acc[...])

    a_spec = pl.BlockSpec((tk, tm), lambda i, j, k: (k, i)) if ta else pl.BlockSpec((tm, tk), lambda i, j, k: (i, k))
    b_spec = pl.BlockSpec((tn, tk), lambda i, j, k: (j, k)) if tb else pl.BlockSpec((tk, tn), lambda i, j, k: (k, j))
    o_spec = pl.BlockSpec((tm, tn), lambda i, j, k: (i, j))
    ins, in_specs = [a, b], [a_spec, b_spec]
    if add is not None:
        ins.append(add)
        in_specs.append(o_spec)
    return pl.pallas_call(
        body, grid=(M // tm, N // tn, nk), in_specs=in_specs, out_specs=o_spec,
        out_shape=jax.ShapeDtypeStruct((M, N), out_dtype), scratch_shapes=[pltpu.VMEM((tm, tn), F32)] if nk > 1 else [],
        name=name, compiler_params=_params(("parallel", "parallel", "arbitrary")),
    )(*ins)


def _rowsum(x):
    return jnp.sum(x, axis=0, keepdims=True)


def _rms_stat(x):
    return lax.rsqrt(jnp.mean(x * x, axis=-1, keepdims=True) + NORM_EPS)


def _rms_bwd(dy, x, g):
    r = _rms_stat(x)
    xn = x * r
    dxn = dy * g
    dx = r * (dxn - xn * jnp.mean(dxn * xn, axis=-1, keepdims=True))
    return dx, _rowsum(dy * xn)


def _fox_c_fwd(f8t, bias_col, tc=256):
    B, H, S = f8t.shape
    tc = min(tc, S)

    def body(f_ref, b_ref, c_ref, carry):
        @pl.when(pl.program_id(1) == 0)
        def _():
            carry[...] = jnp.zeros(carry.shape, F32)
        lf = _log_sigmoid(f_ref[...] + b_ref[...])
        row = lax.broadcasted_iota(jnp.int32, (tc, tc), 0)
        col = lax.broadcasted_iota(jnp.int32, (tc, tc), 1)
        c = _hdot(lf, (row <= col).astype(F32), terms=3) + carry[...]
        c_ref[...] = c
        carry[...] = c[:, tc - 1:tc]

    return pl.pallas_call(
        body, grid=(B, S // tc),
        in_specs=[pl.BlockSpec((None, H, tc), lambda b, i: (b, 0, i)), pl.BlockSpec((H, 1), lambda b, i: (0, 0))],
        out_specs=pl.BlockSpec((None, H, tc), lambda b, i: (b, 0, i)),
        out_shape=jax.ShapeDtypeStruct((B, H, S), F32), scratch_shapes=[pltpu.VMEM((H, 1), F32)], name="fox_c_fwd",
        compiler_params=_params(("parallel", "arbitrary")),
    )(f8t, bias_col)


def _fox_c_bwd(dc, f8t, bias_col, tc=256):
    B, H, S = f8t.shape
    tc = min(tc, S)
    n = S // tc

    def body(dc_ref, f_ref, b_ref, df_ref, db_ref, carry):
        @pl.when(pl.program_id(1) == 0)
        def _():
            carry[...] = jnp.zeros(carry.shape, F32)
            db_ref[...] = jnp.zeros(db_ref.shape, F32)
        row = lax.broadcasted_iota(jnp.int32, (tc, tc), 0)
        col = lax.broadcasted_iota(jnp.int32, (tc, tc), 1)
        dlf = _hdot(dc_ref[...], (row >= col).astype(F32), terms=3) + carry[...]
        z = f_ref[...] + b_ref[...]
        df = dlf * (1.0 - _sigmoid(z))
        df_ref[...] = df
        db_ref[...] += jnp.sum(df, axis=1, keepdims=True)
        carry[...] = dlf[:, 0:1]

    rev = lambda b, i: (b, 0, n - 1 - i)
    return pl.pallas_call(
        body, grid=(B, n),
        in_specs=[pl.BlockSpec((None, H, tc), rev), pl.BlockSpec((None, H, tc), rev), pl.BlockSpec((H, 1), lambda b, i: (0, 0))],
        out_specs=[pl.BlockSpec((None, H, tc), rev), pl.BlockSpec((None, H, 1), lambda b, i: (b, 0, 0))],
        out_shape=[jax.ShapeDtypeStruct((B, H, S), F32), jax.ShapeDtypeStruct((B, H, 1), F32)],
        scratch_shapes=[pltpu.VMEM((H, 1), F32)], name="fox_c_bwd",
        compiler_params=_params(("parallel", "arbitrary")),
    )(dc, f8t, bias_col)


NEG_BIG = -1e30


def _fox_logits(q, kj, ckj, i, j, tq, tk, scale):
    s = _bdot(q, kj, NT_DIMS) * scale - ckj
    row = lax.broadcasted_iota(jnp.int32, (tq, tk), 0)
    col = lax.broadcasted_iota(jnp.int32, (tq, tk), 1)
    return s, col <= row + (i * tq - j * tk)


def _fox_key_blocks(i, tq, tk):
    return (i * tq + tq - 1) // tk + 1


def _fox_fwd(qkv, ck, B, tq):
    T = qkv.shape[0]
    S = T // B
    nq, (nk, tk) = S // tq, (ck.shape[1], ck.shape[3])
    scale = HEAD_DIM ** -0.5

    def body(q_ref, k_ref, v_ref, ck_ref, o_ref, lse_ref):
        i = pl.program_id(1)
        lo = lax.broadcasted_iota(jnp.int32, (tq, LANES), 1) < HEAD_DIM
        q = q_ref[...]
        qh = (jnp.where(lo, q, 0), jnp.where(lo, 0, q))

        def step(j, carry):
            rows = pl.ds(pl.multiple_of(j * tk, tk), tk)
            kj, vj = k_ref[rows, :], v_ref[rows, :]
            new = []
            for h in range(2):
                m, l, acc = carry[h]
                s, ok = _fox_logits(qh[h], kj, ck_ref[h, j], i, j, tq, tk, scale)
                s = jnp.where(ok, s, NEG_BIG)
                m2 = jnp.maximum(m, jnp.max(s, axis=1, keepdims=True))
                p = jnp.exp(s - m2)
                al = jnp.exp(m - m2)
                new.append((m2, al * l + jnp.sum(p, axis=1, keepdims=True), al * acc + _bdot(p, vj)))
            return tuple(new)

        init = tuple((jnp.full((tq, 1), NEG_BIG, F32), jnp.zeros((tq, 1), F32), jnp.zeros((tq, LANES), F32)) for _ in range(2))
        (m0, l0, a0), (m1, l1, a1) = lax.fori_loop(0, _fox_key_blocks(i, tq, tk), step, init)
        o_ref[...] = jnp.where(lo, a0 / l0, a1 / l1).astype(o_ref.dtype)
        lse_ref[0] = m0 + jnp.log(l0)
        lse_ref[1] = m1 + jnp.log(l1)

    seq = lambda col0: pl.BlockSpec((S, LANES), lambda g, i: (g // 4, col0 + g % 4))
    blk = lambda col0: pl.BlockSpec((tq, LANES), lambda g, i: ((g // 4) * nq + i, col0 + g % 4))
    col = pl.BlockSpec((2, None, tq, 1), lambda g, i: (g, i, 0, 0))
    return pl.pallas_call(
        body, grid=(B * 4, nq), in_specs=[blk(0), seq(4), seq(8), pl.BlockSpec((2, nk, 1, tk), lambda g, i: (g, 0, 0, 0))],
        out_specs=[blk(0), col],
        out_shape=[jax.ShapeDtypeStruct((T, BR_W), BF16), jax.ShapeDtypeStruct((B * N_HEADS, nq, tq, 1), F32)],
        name="fox_fwd", compiler_params=_params(("parallel", "parallel")),
    )(qkv, qkv, qkv, ck)


def _fox_bwd(qkv, o, do, ck, lse, B):
    T = qkv.shape[0]
    S = T // B
    (nq, tq), (nk, tk) = lse.shape[1:3], (ck.shape[1], ck.shape[3])
    scale = HEAD_DIM ** -0.5

    def body(q_ref, k_ref, v_ref, o_ref, do_ref, ck_ref, lse_ref, dq_ref, dk_ref, dv_ref, dck_ref, dcq_ref,
             dk_acc, dv_acc):
        dk_acc[...] = jnp.zeros(dk_acc.shape, F32)
        dv_acc[...] = jnp.zeros(dv_acc.shape, F32)
        dck_ref[...] = jnp.zeros(dck_ref.shape, F32)
        lo = lax.broadcasted_iota(jnp.int32, (tq, LANES), 1) < HEAD_DIM

        def qloop(i, _):
            qrows = pl.ds(pl.multiple_of(i * tq, tq), tq)
            q, do_i, o_i = q_ref[qrows, :], do_ref[qrows, :], o_ref[qrows, :].astype(F32)
            qh = (jnp.where(lo, q, 0), jnp.where(lo, 0, q))
            doh = (jnp.where(lo, do_i, 0), jnp.where(lo, 0, do_i))
            delta = [jnp.sum(doh[h].astype(F32) * o_i, axis=1, keepdims=True) for h in range(2)]

            def kloop(j, carry):
                krows = pl.ds(pl.multiple_of(j * tk, tk), tk)
                kj, vj = k_ref[krows, :], v_ref[krows, :]
                new = []
                for h in range(2):
                    dq, dcq = carry[h]
                    s, ok = _fox_logits(qh[h], kj, ck_ref[h, j], i, j, tq, tk, scale)
                    p = jnp.where(ok, jnp.exp(s - lse_ref[h, i]), 0.0)
                    ds = p * (_bdot(doh[h], vj, NT_DIMS) - delta[h])
                    dv_acc[krows, :] += _bdot(p, doh[h], TN_DIMS)
                    dk_acc[krows, :] += _bdot(ds, qh[h], TN_DIMS) * scale
                    dck_ref[h, j] += -_rowsum(ds)
                    new.append((dq + _bdot(ds, kj) * scale, dcq + jnp.sum(ds, axis=1, keepdims=True)))
                return tuple(new)

            init = tuple((jnp.zeros((tq, LANES), F32), jnp.zeros((tq, 1), F32)) for _ in range(2))
            (dq0, dcq0), (dq1, dcq1) = lax.fori_loop(0, _fox_key_blocks(i, tq, tk), kloop, init)
            dq_ref[qrows, :] = jnp.where(lo, dq0, dq1).astype(dq_ref.dtype)
            dcq_ref[0, i] = dcq0
            dcq_ref[1, i] = dcq1
            return 0

        lax.fori_loop(0, nq, qloop, 0)
        dk_ref[...] = dk_acc[...].astype(dk_ref.dtype)
        dv_ref[...] = dv_acc[...].astype(dv_ref.dtype)

    seq = lambda col0: pl.BlockSpec((S, LANES), lambda g: (g // 4, col0 + g % 4))
    col = pl.BlockSpec((2, nq, tq, 1), lambda g: (g, 0, 0, 0))
    row = pl.BlockSpec((2, nk, 1, tk), lambda g: (g, 0, 0, 0))
    out = jax.ShapeDtypeStruct((T, BR_W), BF16)
    return pl.pallas_call(
        body, grid=(B * 4,), in_specs=[seq(0), seq(4), seq(8), seq(0), seq(0), row, col],
        out_specs=[seq(0), seq(0), seq(0), row, col],
        out_shape=[out, out, out, jax.ShapeDtypeStruct(ck.shape, F32), jax.ShapeDtypeStruct(lse.shape, F32)],
        scratch_shapes=[pltpu.VMEM((S, LANES), F32), pltpu.VMEM((S, LANES), F32)], name="fox_bwd",
        compiler_params=_params(("parallel",)),
    )(qkv, qkv, qkv, o, do, ck, lse)


def _mem_probs(qh, kh):
    s = _bdot(qh, kh, NT_DIMS) * (MEM_HEAD_DIM ** -0.5)
    e = jnp.exp(s - jnp.max(s, axis=1, keepdims=True))
    return e / jnp.sum(e, axis=1, keepdims=True)


def _mem_fwd(q, mem_kv, B, tq=512):
    T = q.shape[0]
    S, Lm = T // B, mem_kv.shape[0] // B
    tq = min(tq, S)
    n = S // tq

    def body(q_ref, k_ref, v_ref, o_ref):
        for h in range(MEM_HEADS):
            sl = slice(h * MEM_HEAD_DIM, (h + 1) * MEM_HEAD_DIM)
            p = _mem_probs(q_ref[:, sl], k_ref[:, sl])
            o_ref[:, sl] = _bdot(p, v_ref[:, sl]).astype(o_ref.dtype)

    qs = pl.BlockSpec((tq, BR_W), lambda b, i: (b * n + i, 0))
    return pl.pallas_call(
        body, grid=(B, n),
        in_specs=[qs, pl.BlockSpec((Lm, BR_W), lambda b, i: (b, 0)), pl.BlockSpec((Lm, BR_W), lambda b, i: (b, 1))],
        out_specs=qs, out_shape=jax.ShapeDtypeStruct((T, BR_W), BF16), name="mem_fwd",
        compiler_params=_params(("parallel", "parallel")),
    )(q, mem_kv, mem_kv)


def _mem_bwd(q, mem_kv, do, B, tq=512):
    T = q.shape[0]
    S, Lm = T // B, mem_kv.shape[0] // B
    tq = min(tq, S)
    n = S // tq
    scale = MEM_HEAD_DIM ** -0.5

    def body(q_ref, k_ref, v_ref, do_ref, dq_ref, dk_ref, dv_ref):
        @pl.when(pl.program_id(1) == 0)
        def _():
            dk_ref[...] = jnp.zeros(dk_ref.shape, F32)
            dv_ref[...] = jnp.zeros(dv_ref.shape, F32)
        for h in range(MEM_HEADS):
            sl = slice(h * MEM_HEAD_DIM, (h + 1) * MEM_HEAD_DIM)
            qh, kh, vh, doh = q_ref[:, sl], k_ref[:, sl], v_ref[:, sl], do_ref[:, sl]
            p = _mem_probs(qh, kh)
            dp = _bdot(doh, vh, NT_DIMS)
            ds = p * (dp - jnp.sum(p * dp, axis=1, keepdims=True))
            dq_ref[:, sl] = (_bdot(ds, kh) * scale).astype(dq_ref.dtype)
            dk_ref[:, sl] += _bdot(ds, qh, TN_DIMS) * scale
            dv_ref[:, sl] += _bdot(p, doh, TN_DIMS)

    qs = pl.BlockSpec((tq, BR_W), lambda b, i: (b * n + i, 0))
    kv = pl.BlockSpec((Lm, BR_W), lambda b, i: (b, 0))
    return pl.pallas_call(
        body, grid=(B, n),
        in_specs=[qs, kv, pl.BlockSpec((Lm, BR_W), lambda b, i: (b, 1)), qs], out_specs=[qs, kv, kv],
        out_shape=[jax.ShapeDtypeStruct((T, BR_W), BF16), jax.ShapeDtypeStruct((B * Lm, BR_W), F32),
                   jax.ShapeDtypeStruct((B * Lm, BR_W), F32)], name="mem_bwd",
        compiler_params=_params(("parallel", "arbitrary")),
    )(q, mem_kv, mem_kv, do)


def _head_ones():
    h = np.arange(BR_W) // HEAD_DIM
    return jnp.asarray((h[:, None] == h[None, :]).astype(np.float32))


def _rw_prep(p, pp, mu, w0, w1, a0, w2, g_up, k_k, k_a, bd):
    ps = p + (pp - p) * mu
    r, k, v = ps[:, 0:512], ps[:, 512:1024], ps[:, 1024:1536]
    wa, gd = ps[:, 1536:1664], ps[:, 1664:1792]
    th = jnp.tanh(wa)
    z = w0 + _bdot(th, w1)
    wl = -jnp.exp(_log_sigmoid(z) - 0.5)
    w = jnp.exp(wl)
    a = _sigmoid(a0 + _bdot(wa, w2))
    sg = _sigmoid(gd)
    g = _bdot(sg, g_up)
    kq = k * k_k
    n2 = _hdot(kq * kq, bd)
    inv = lax.rsqrt(jnp.maximum(n2, 1e-24))
    kk = kq * inv
    k2 = k * (1.0 + (a - 1.0) * k_a)
    return dict(ps=ps, r=r, k=k, v=v, wa=wa, th=th, z=z, wl=wl, w=w, a=a, sg=sg, g=g, kq=kq, n2=n2, inv=inv, kk=kk, k2=k2)


def _keycol_selector(tm, Tc):
    e = np.zeros((tm, (tm // Tc) * LANES), np.float32)
    for t in range(tm):
        c, tl = divmod(t, Tc)
        e[t, c * LANES + tl] = e[t, c * LANES + Tc + tl] = 1.0
    return jnp.asarray(e, BF16)


def _rw_prep_fwd(p, pp, consts, B, Tc, tm=256):
    assert 2 * Tc == LANES
    T = p.shape[0]
    S = T // B
    nb, cpb = S // tm, tm // Tc
    sel = _keycol_selector(tm, Tc)
    nc = len(consts)

    def body(*refs):
        t = _rw_prep(*[r[...] for r in refs[:2 + nc]])
        sel_ref = refs[2 + nc]
        rows, cols = refs[3 + nc:7 + nc], refs[7 + nc:]
        for ref, val in zip(rows, (t["r"], t["k2"], t["v"], t["g"])):
            ref[...] = val
        lo = lax.broadcasted_iota(jnp.int32, (HEAD_DIM, LANES), 1) < HEAD_DIM
        operands = (t["w"], -t["kk"], t["kk"] * t["a"], t["k2"], t["r"])
        for n, (ref, x) in enumerate(zip(cols, operands)):
            terms = _split_bf16(x) if ref.dtype == F32 else (x.astype(BF16),)
            for hp in range(4):
                xt = sum(lax.dot_general(tt[:, hp * LANES:(hp + 1) * LANES], sel_ref[...], TN_DIMS,
                                         preferred_element_type=F32) for tt in terms)
                for c in range(cpb):
                    blk = xt[:, c * LANES:(c + 1) * LANES]
                    ref[hp, c] = jnp.where(lo, blk[0:HEAD_DIM], blk[HEAD_DIM:2 * HEAD_DIM]).astype(ref.dtype)

    row_spec = lambda w: pl.BlockSpec((tm, w), lambda i: (i, 0))
    col_spec = pl.BlockSpec((None, 4, cpb, HEAD_DIM, LANES), lambda i: (i // nb, 0, i % nb, 0, 0))
    col_shape = lambda dt: jax.ShapeDtypeStruct((B, 4, S // Tc, HEAD_DIM, LANES), dt)
    out = pl.pallas_call(
        body, grid=(T // tm,),
        in_specs=[row_spec(RWKV_COLS)] * 2 + [pl.BlockSpec(c.shape, lambda i: (0, 0)) for c in consts]
        + [pl.BlockSpec(sel.shape, lambda i: (0, 0))],
        out_specs=[row_spec(BR_W)] * 4 + [col_spec] * 5,
        out_shape=[jax.ShapeDtypeStruct((T, BR_W), F32)] * 4 + [col_shape(F32)] + [col_shape(BF16)] * 3 + [col_shape(F32)],
        name="rwkv_prep_fwd", compiler_params=_params(("parallel",)),
    )(p, pp, *consts, sel)
    return out[:4], [c.reshape(B * 4, S // Tc, HEAD_DIM, LANES) for c in out[4:]]


def _rw_prep_bwd(p, pp, cots, consts):
    def fn(p, pp, dr1, dr2, dw, dk21, dk22, dv1, dv2, dav, dbv, dg, mu, w0, w1, a0, w2, g_up, k_k, k_a, bd):
        t = _rw_prep(p, pp, mu, w0, w1, a0, w2, g_up, k_k, k_a, bd)
        dr, dk2, dv = dr1 + dr2, dk21 + dk22, dv1 + dv2
        a, k, kk, kq, inv = t["a"], t["k"], t["kk"], t["kq"], t["inv"]
        dkk = dbv * a - dav
        da = dbv * kk + dk2 * k * k_a
        dk = dk2 * (1.0 + (a - 1.0) * k_a)
        d_k_a = _rowsum(dk2 * k * (a - 1.0))
        proj = _hdot(dkk * kq, bd)
        dkq = dkk * inv - jnp.where(t["n2"] > 1e-24, kq * inv * inv * inv * proj, 0.0)
        dk = dk + dkq * k_k
        d_k_k = _rowsum(dkq * k)
        dpa = da * a * (1.0 - a)
        d_a0 = _rowsum(dpa)
        dwa = _bdot(dpa, w2, NT_DIMS)
        d_w2 = _bdot(t["wa"], dpa, TN_DIMS)
        dz = dw * t["w"] * t["wl"] * (1.0 - _sigmoid(t["z"]))
        d_w0 = _rowsum(dz)
        th = t["th"]
        dwa = dwa + _bdot(dz, w1, NT_DIMS) * (1.0 - th * th)
        d_w1 = _bdot(th, dz, TN_DIMS)
        sg = t["sg"]
        dgd = _bdot(dg, g_up, NT_DIMS) * sg * (1.0 - sg)
        d_g_up = _bdot(sg, dg, TN_DIMS)
        dps = jnp.concatenate([dr, dk, dv, dwa, dgd], axis=1)
        d_mu = _rowsum(dps * (pp - p))
        return dps * (1.0 - mu), dps * mu, d_mu, d_w0, d_w1, d_a0, d_w2, d_g_up, d_k_k, d_k_a

    accs = [((1, RWKV_COLS), F32), ((1, BR_W), F32), ((LANES, BR_W), F32), ((1, BR_W), F32), ((LANES, BR_W), F32),
            ((LANES, BR_W), F32), ((1, BR_W), F32), ((1, BR_W), F32)]
    return _rowwise("rwkv_prep_bwd", fn, [p, pp] + list(cots), consts, [(RWKV_COLS, BF16)] * 2, accs, tm=256)


def _rw_head(y, r, k2, v, g, gn_g, gn_b, r_k, bd):
    mean = _hdot(y, bd) * (1.0 / HEAD_DIM)
    yc = y - mean
    rs = lax.rsqrt(_hdot(yc * yc, bd) * (1.0 / HEAD_DIM) + GN_EPS)
    yn = yc * rs
    bs = _hdot(r * k2 * r_k, bd)
    return yn, rs, bs, yn * gn_g + gn_b + bs * v


def _rw_head_fwd(y, r, k2, v, g, consts):
    def fn(y, r, k2, v, g, *c):
        return _rw_head(y, r, k2, v, g, *c)[3] * g
    return _rowwise("rwkv_head_fwd", fn, [y, r, k2, v, g], consts, [(BR_W, BF16)])[0]


def _rw_head_bwd(dout, y, r, k2, v, g, consts):
    def fn(dout, y, r, k2, v, g, gn_g, gn_b, r_k, bd):
        dout = dout.astype(F32)
        yn, rs, bs, zz = _rw_head(y, r, k2, v, g, gn_g, gn_b, r_k, bd)
        dg = dout * zz
        dz = dout * g
        dyn = dz * gn_g
        inv_n = 1.0 / HEAD_DIM
        dy = rs * (dyn - _hdot(dyn, bd) * inv_n - yn * (_hdot(dyn * yn, bd) * inv_n))
        dq = _hdot(dz * v, bd)
        return dy, dg, dq * k2 * r_k, dq * r * r_k, dz * bs, _rowsum(dz * yn), _rowsum(dz), _rowsum(dq * r * k2)
    return _rowwise("rwkv_head_bwd", fn, [dout, y, r, k2, v, g], consts, [(BR_W, F32)] * 5, [((1, BR_W), F32)] * 3)


SCAN_TC = 64


def _scan_onehot(Tc):
    w = np.zeros((Tc // 2, 2 * Tc, 2 * LANES), np.float32)
    for tt in range(Tc // 2):
        for u in range(2):
            for h in range(2):
                w[tt, h * Tc + 2 * tt + u, u * LANES + h * HEAD_DIM: u * LANES + (h + 1) * HEAD_DIM] = 1.0
    return jnp.asarray(w, BF16)


def _split_bf16(x):
    hi = x.astype(BF16)
    return hi, (x - hi.astype(F32)).astype(BF16)


def _key_tiles(l_w, others, onehot):
    dot = lambda x: jnp.dot(x, onehot, preferred_element_type=F32)
    whi, wmid = l_w
    return [dot(whi) + dot(wmid)] + [dot(o) for o in others]


def _rw_scan_fwd(LW, LA, LB, LK, LR, v, P=4):
    NP, nc, _, Tc2 = LW.shape
    Tc = Tc2 // 2
    S = nc * Tc
    onehot = _scan_onehot(Tc)
    npb = 4 // P

    def body(lw, la, lb, lk, lr, v_ref, oh_ref, y_ref, sa_ref, sb_ref, st):
        @pl.when(pl.program_id(1) == 0)
        def _():
            st[...] = jnp.zeros(st.shape, F32)
        s = [st[p] for p in range(P)]
        cols = [(_split_bf16(lw[p]), [ref[p].astype(BF16) for ref in (la, lb, lk)]) for p in range(P)]
        r_cols = [lr[p].astype(F32) for p in range(P)]
        head0 = lax.broadcasted_iota(jnp.int32, (HEAD_DIM, LANES), 1) < HEAD_DIM
        for tt in range(Tc // 2):
            tiles = [_key_tiles(c[0], c[1], oh_ref[tt]) for c in cols]
            for u in range(2):
                t = 2 * tt + u
                for p in range(P):
                    W, A, Bt, Kt = (x[:, u * LANES:(u + 1) * LANES] for x in tiles[p])
                    R = jnp.where(head0, r_cols[p][:, t:t + 1], r_cols[p][:, Tc + t:Tc + t + 1])
                    ls = slice(p * LANES, (p + 1) * LANES)
                    sb_ref[p, t] = s[p]
                    sa = _rowsum(s[p] * A)
                    s[p] = s[p] * W + Bt * sa + Kt * v_ref[t:t + 1, ls]
                    y_ref[t:t + 1, ls] = _rowsum(s[p] * R)
                    sa_ref[t:t + 1, ls] = sa
        for p in range(P):
            st[p] = s[p]

    lspec = pl.BlockSpec((P, None, HEAD_DIM, Tc2), lambda g, c: (g, c, 0, 0))
    rows = pl.BlockSpec((Tc, P * LANES), lambda g, c: ((g // npb) * nc + c, g % npb))
    rowshape = jax.ShapeDtypeStruct(v.shape, F32)
    return pl.pallas_call(
        body, grid=(NP // P, nc), in_specs=[lspec] * 5 + [rows, pl.BlockSpec(onehot.shape, lambda g, c: (0, 0, 0))],
        out_specs=[rows, rows, pl.BlockSpec((P, Tc, HEAD_DIM, LANES), lambda g, c: (g, c, 0, 0))],
        out_shape=[rowshape, rowshape, jax.ShapeDtypeStruct((NP, S, HEAD_DIM, LANES), F32)],
        scratch_shapes=[pltpu.VMEM((P, HEAD_DIM, LANES), F32)], name="rwkv_scan_fwd",
        compiler_params=_params(("parallel", "arbitrary")),
    )(LW, LA, LB, LK, LR, v, onehot)


SCAN_G_ROWS = 16


def _rw_scan_bwd(LW, LA, LB, LK, LR, v, sa, dy, sb, P=4):
    NP, nc, _, Tc2 = LW.shape
    Tc = Tc2 // 2
    onehot = _scan_onehot(Tc)
    npb = 4 // P

    def body(lw, la, lb, lk, lr, v_ref, sa_ref, dy_ref, sb_ref, oh_ref, dv_ref, dk_ref, db_ref, dw_ref, dr_ref, da_ref, dst):
        @pl.when(pl.program_id(1) == 0)
        def _():
            dst[...] = jnp.zeros(dst.shape, F32)
        rid = lax.broadcasted_iota(jnp.int32, (SCAN_G_ROWS, LANES), 0)
        lane = lax.broadcasted_iota(jnp.int32, (SCAN_G_ROWS, LANES), 1)
        own = (((rid % 2) == 0) == (lane < HEAD_DIM)) & (rid < 10)
        lo = lane[0:1] < HEAD_DIM
        nt = lambda rows, tile: lax.dot_general(rows.astype(BF16), tile.astype(BF16), NT_DIMS, preferred_element_type=F32)
        ds = [dst[p] for p in range(P)]
        cols = [(_split_bf16(lw[p]), [ref[p].astype(BF16) for ref in (la, lb, lk)]) for p in range(P)]
        r_cols = [lr[p].astype(F32) for p in range(P)]
        head0 = lax.broadcasted_iota(jnp.int32, (HEAD_DIM, LANES), 1) < HEAD_DIM
        for tt in reversed(range(Tc // 2)):
            tiles = [_key_tiles(c[0], c[1], oh_ref[tt]) for c in cols]
            pending = [[] for _ in range(P)]
            for u in (1, 0):
                t = 2 * tt + u
                for p in range(P):
                    W, A, Bt, Kt = (x[:, u * LANES:(u + 1) * LANES] for x in tiles[p])
                    R = jnp.where(head0, r_cols[p][:, t:t + 1], r_cols[p][:, Tc + t:Tc + t + 1])
                    ls = slice(p * LANES, (p + 1) * LANES)
                    vr, sar, dyr = (ref[t:t + 1, ls] for ref in (v_ref, sa_ref, dy_ref))
                    sp = sb_ref[p, t]
                    s_t = sp * W + Bt * sar + Kt * vr
                    d = ds[p] + R * dyr
                    dv_ref[t:t + 1, ls] = _rowsum(d * Kt)
                    dsar = _rowsum(d * Bt)
                    rows = jnp.where(rid < 2, vr, jnp.where(rid < 4, sar, jnp.where(rid < 6, 1.0, jnp.where(rid < 8, dyr, dsar))))
                    pending[p].append((t, jnp.where(own, rows, 0.0), [d, d * sp, s_t, sp]))
                    ds[p] = d * W + A * dsar
            for p in range(P):
                ls = slice(p * LANES, (p + 1) * LANES)
                (t1, rows1, tiles1), (t0, rows0, tiles0) = pending[p]
                g2 = nt(jnp.concatenate([rows1, rows0], axis=0), jnp.concatenate(tiles1 + tiles0, axis=0))
                for t, g in ((t1, g2[0:SCAN_G_ROWS, 0:2 * LANES]), (t0, g2[SCAN_G_ROWS:, 2 * LANES:])):
                    ga, gb = g[:, 0:LANES], g[:, LANES:2 * LANES]
                    ra, rb = pltpu.roll(ga, HEAD_DIM, 1), pltpu.roll(gb, HEAD_DIM, 1)
                    dk_ref[t:t + 1, ls] = jnp.where(lo, ga[0:1], ra[1:2])
                    db_ref[t:t + 1, ls] = jnp.where(lo, ga[2:3], ra[3:4])
                    dw_ref[t:t + 1, ls] = jnp.where(lo, ra[4:5], ga[5:6])
                    dr_ref[t:t + 1, ls] = jnp.where(lo, gb[6:7], rb[7:8])
                    da_ref[t:t + 1, ls] = jnp.where(lo, rb[8:9], gb[9:10])
        for p in range(P):
            dst[p] = ds[p]

    rev = lambda g, c: (g, nc - 1 - c, 0, 0)
    lspec = pl.BlockSpec((P, None, HEAD_DIM, Tc2), rev)
    rows = pl.BlockSpec((Tc, P * LANES), lambda g, c: ((g // npb) * nc + nc - 1 - c, g % npb))
    return pl.pallas_call(
        body, grid=(NP // P, nc),
        in_specs=[lspec] * 5 + [rows] * 3 + [pl.BlockSpec((P, Tc, HEAD_DIM, LANES), rev),
                                             pl.BlockSpec(onehot.shape, lambda g, c: (0, 0, 0))],
        out_specs=[rows] * 6, out_shape=[jax.ShapeDtypeStruct(v.shape, F32)] * 6,
        scratch_shapes=[pltpu.VMEM((P, HEAD_DIM, LANES), F32)], name="rwkv_scan_bwd",
        compiler_params=_params(("parallel", "arbitrary")),
    )(LW, LA, LB, LK, LR, v, sa, dy, sb, onehot)


def _shift_prev(p, B):
    T, W = p.shape
    return jnp.pad(p.reshape(B, T // B, W), ((0, 0), (1, 0), (0, 0)))[:, :-1].reshape(T, W)


def _shift_next(p, B):
    T, W = p.shape
    return jnp.pad(p.reshape(B, T // B, W), ((0, 0), (0, 1), (0, 0)))[:, 1:].reshape(T, W)


FOX_FWD_BLOCKS = (512, 1024)
FOX_BWD_BLOCKS = (512, 512)


def _layer_step(x, mem, target, W, late_weights=None, early_grads=None, scan_tc=SCAN_TC, fox_fwd_t=FOX_FWD_BLOCKS,
                fox_bwd_t=FOX_BWD_BLOCKS):
    B, S, _ = x.shape
    T = B * S
    x2, tgt2 = x.reshape(T, D_MODEL), target.reshape(T, D_MODEL)
    mem2 = mem.reshape(-1, D_MODEL)
    w_in_t = W["w_in"]
    wt_qkv, wt_rw, wt_mq, wt_gate = (w_in_t[lo:hi] for lo, hi in (COL_QKV, COL_RW, COL_MQ, COL_GATE))
    wt_f = jnp.pad(w_in_t[COL_F[0]:COL_F[1]], ((0, LANES - N_HEADS), (0, 0)))
    row = lambda v: v.reshape(1, -1).astype(F32)
    pre1_g, post1_g, pre2_g, post2_g, mem_g = (row(W[n]) for n in ("pre1_g", "post1_g", "pre2_g", "post2_g", "mem_norm_g"))

    u = _rowwise("rms_pre1", lambda x, g: x * _rms_stat(x) * g, [x2], [pre1_g], [(D_MODEL, BF16)])[0]
    qkv = _mm("proj_qkv", u, wt_qkv, tb=True, out_dtype=BF16)
    f_pad = _mm("proj_f", u, wt_f, tb=True)
    p_rw = _mm("proj_rwkv", u, wt_rw, tb=True)
    memq = _mm("proj_memq", u, wt_mq, tb=True, out_dtype=BF16)
    gate = _mm("proj_gate", u, wt_gate, tb=True, out_dtype=BF16)

    bias_col = W["fox_f_bias"].reshape(N_HEADS, 1).astype(F32)
    f8t = f_pad[:, :N_HEADS].reshape(B, S, N_HEADS).transpose(0, 2, 1)
    c = _fox_c_fwd(f8t, bias_col)
    G = B * N_HEADS
    q_blocks = lambda a, t: a.reshape(G, S // min(t, S), min(t, S), 1)
    k_blocks = lambda a, t: a.reshape(G, S // min(t, S), 1, min(t, S))
    fox_out, lse = _fox_fwd(qkv, k_blocks(c, fox_fwd_t[1]), B, min(fox_fwd_t[0], S))

    bd = _head_ones()
    zpad = jnp.zeros((64, BR_W), F32)
    w1 = jnp.concatenate([W["rwkv_w_up"].astype(F32), zpad], axis=0)
    w2 = jnp.concatenate([zpad, W["rwkv_a_up"].astype(F32)], axis=0)
    prep_consts = [row(W["rwkv_mu"]), row(W["rwkv_w0"]), w1, row(W["rwkv_a0"]), w2, W["rwkv_g_up"].astype(F32),
                   row(W["rwkv_k_k"]), row(W["rwkv_k_a"]), bd]
    p_prev = _shift_prev(p_rw, B)
    (rr, rk2, rv, rg), scan_cols = _rw_prep_fwd(p_rw, p_prev, prep_consts, B, scan_tc)
    ry, rsa, sb = _rw_scan_fwd(*scan_cols, rv)
    head_consts = [row(W["rwkv_gn_g"]), row(W["rwkv_gn_b"]), row(W["rwkv_r_k"]), bd]
    rwkv_out = _rw_head_fwd(ry, rr, rk2, rv, rg, head_consts)

    if late_weights is not None:
        W = {**W, **late_weights(rwkv_out)}

    mn = _rowwise("rms_mem", lambda m, g: m * _rms_stat(m) * g, [mem2], [mem_g], [(D_MODEL, BF16)])[0]
    mem_kv = _mm("proj_memkv", mn, W["w_mem_kv"], out_dtype=BF16)
    mem_out = _mem_fwd(memq, mem_kv, B)

    fo = [_mm("branch_" + n, a, W[n], tb=True, out_dtype=BF16)
          for n, a in (("w_fox_out", fox_out), ("w_rwkv_out", rwkv_out), ("w_mem_out", mem_out))]

    def merge(gate, f0, f1, f2):
        return sum(_sigmoid(gate[:, i * D_MODEL:(i + 1) * D_MODEL]) * f for i, f in enumerate((f0, f1, f2)))
    merged = _rowwise("merge", merge, [gate] + fo, [], [(D_MODEL, BF16)])[0]
    y1 = _mm("proj_o", merged, W["w_o"])

    def mid(x, y1, g1, g2):
        h1 = x + y1 * _rms_stat(y1) * g1
        return h1, h1 * _rms_stat(h1) * g2
    h1, u2 = _rowwise("norm_mid", mid, [x2, y1], [post1_g, pre2_g], [(D_MODEL, F32), (D_MODEL, BF16)])
    gt = _mm("ffn_gate", u2, W["w_ffn_gate"], tb=True, out_dtype=BF16)
    up = _mm("ffn_up", u2, W["w_ffn_up"], tb=True, out_dtype=BF16)
    act = _rowwise("swiglu", lambda gt, up: gt * _sigmoid(gt) * up, [gt, up], [], [(D_FF, BF16)])[0]
    ffn = _mm("ffn_down", act, W["w_ffn_down"])

    def tail(h1, ffn, tgt, g):
        err = h1 + ffn * _rms_stat(ffn) * g - tgt
        dh2 = err * (1.0 / D_MODEL)
        dffn, dg = _rms_bwd(dh2, ffn, g)
        loss = 0.5 * jnp.sum(jnp.sum(err * err, axis=1, keepdims=True) * (1.0 / D_MODEL), axis=0, keepdims=True)
        return dh2, dffn, dg, jnp.broadcast_to(loss, (1, LANES))
    dh2, dffn, d_post2, loss = _rowwise("loss_tail", tail, [h1, ffn, tgt2], [post2_g], [(D_MODEL, F32), (D_MODEL, BF16)],
                                        [((1, D_MODEL), F32), ((1, LANES), F32)])
    gw = {"post2_g": d_post2}
    dact = _mm("d_act", dffn, W["w_ffn_down"], tb=True, out_dtype=BF16)
    gw["w_ffn_down"] = _mm("g_ffn_down", act, dffn, ta=True, out_dtype=BF16)

    def swiglu_bwd(dact, gt, up):
        s = _sigmoid(gt)
        return dact * up * s * (1.0 + gt * (1.0 - s)), dact * gt * s
    dgt, dup = _rowwise("swiglu_bwd", swiglu_bwd, [dact, gt, up], [], [(D_FF, BF16)] * 2)
    du2 = _mm("d_u2_gate", dgt, W["w_ffn_gate"])
    du2 = _mm("d_u2_up", dup, W["w_ffn_up"], add=du2)
    gw["w_ffn_gate"] = _mm("g_ffn_gate", dgt, u2, ta=True, out_dtype=BF16)
    gw["w_ffn_up"] = _mm("g_ffn_up", dup, u2, ta=True, out_dtype=BF16)

    def mid_bwd(du2, dh2, h1, y1, g1, g2):
        dh1_n, d_pre2 = _rms_bwd(du2, h1, g2)
        dh1 = dh2 + dh1_n
        dy1, d_post1 = _rms_bwd(dh1, y1, g1)
        return dh1, dy1, d_post1, d_pre2
    dh1, dy1, gw["post1_g"], gw["pre2_g"] = _rowwise(
        "norm_mid_bwd", mid_bwd, [du2, dh2, h1, y1], [post1_g, pre2_g], [(D_MODEL, F32), (D_MODEL, BF16)],
        [((1, D_MODEL), F32)] * 2)
    dmerged = _mm("d_merged", dy1, W["w_o"], tb=True, out_dtype=BF16)
    gw["w_o"] = _mm("g_w_o", merged, dy1, ta=True, out_dtype=BF16)

    def merge_bwd(dm, gate, f0, f1, f2):
        s = [_sigmoid(gate[:, i * D_MODEL:(i + 1) * D_MODEL]) for i in range(3)]
        dgate = jnp.concatenate([dm * f * si * (1.0 - si) for f, si in zip((f0, f1, f2), s)], axis=1)
        return dm * s[0], dm * s[1], dm * s[2], dgate
    dfo0, dfo1, dfo2, dgate = _rowwise("merge_bwd", merge_bwd, [dmerged, gate] + fo, [],
                                       [(D_MODEL, BF16)] * 3 + [(3 * D_MODEL, BF16)])
    d_branch = {}
    for n, a, dfo in (("w_fox_out", fox_out, dfo0), ("w_rwkv_out", rwkv_out, dfo1), ("w_mem_out", mem_out, dfo2)):
        d_branch[n] = _mm("d_in_" + n, dfo, W[n], out_dtype=BF16)
        gw[n] = _mm("g_" + n, dfo, a, ta=True, out_dtype=BF16)

    dmemq, dkm, dvm = _mem_bwd(memq, mem_kv, d_branch["w_mem_out"], B)
    dmem_kv = jnp.concatenate([dkm, dvm], axis=1)
    gw["w_mem_kv"] = _mm("g_w_mem_kv", mn, dmem_kv, ta=True, out_dtype=BF16)
    dmn = _mm("d_mn", dmem_kv, W["w_mem_kv"], tb=True)
    gw["mem_norm_g"] = _rowwise("rms_mem_bwd", lambda d, m, g: _rms_bwd(d, m, g)[1], [dmn, mem2], [mem_g], [],
                                [((1, D_MODEL), F32)])[0]
    if early_grads is not None:
        c = early_grads(gw, c)

    dfq, dfk, dfv, dck, dcq = _fox_bwd(qkv, fox_out, d_branch["w_fox_out"], k_blocks(c, fox_bwd_t[1]),
                                       q_blocks(lse, fox_bwd_t[0]), B)
    df8t, dbias = _fox_c_bwd(dck.reshape(B, N_HEADS, S) + dcq.reshape(B, N_HEADS, S), f8t, bias_col)
    gw["fox_f_bias"] = jnp.sum(dbias, axis=0).reshape(1, N_HEADS)
    dqkv = jnp.concatenate([dfq, dfk, dfv], axis=1)
    df_pad = jnp.pad(df8t.transpose(0, 2, 1).reshape(T, N_HEADS), ((0, 0), (0, LANES - N_HEADS))).astype(BF16)

    dry, drg, dr_h, dk2_h, dv_h, gw["rwkv_gn_g"], gw["rwkv_gn_b"], gw["rwkv_r_k"] = _rw_head_bwd(
        d_branch["w_rwkv_out"], ry, rr, rk2, rv, rg, head_consts)
    dv_s, dk2_s, db_s, dw_s, dr_s, da_s = _rw_scan_bwd(*scan_cols, rv, rsa, dry, sb)
    dP, dPp, gw["rwkv_mu"], gw["rwkv_w0"], d_w1, gw["rwkv_a0"], d_w2, gw["rwkv_g_up"], gw["rwkv_k_k"], gw["rwkv_k_a"] = \
        _rw_prep_bwd(p_rw, p_prev, [dr_s, dr_h, dw_s, dk2_s, dk2_h, dv_s, dv_h, da_s, db_s, drg], prep_consts)
    gw["rwkv_w_up"], gw["rwkv_a_up"] = d_w1[:64], d_w2[64:]
    dp_rw = dP + _shift_next(dPp, B)

    du = _mm("d_u_qkv", dqkv, wt_qkv)
    du = _mm("d_u_f", df_pad, wt_f, add=du)
    du = _mm("d_u_rwkv", dp_rw, wt_rw, add=du)
    du = _mm("d_u_memq", dmemq, wt_mq, add=du)
    du = _mm("d_u_gate", dgate, wt_gate, add=du)
    gw["w_in"] = jnp.concatenate(
        [_mm("g_w_qkv", dqkv, u, ta=True, out_dtype=BF16), _mm("g_w_f", df_pad, u, ta=True, out_dtype=BF16)[:N_HEADS],
         _mm("g_w_rwkv", dp_rw, u, ta=True, out_dtype=BF16), _mm("g_w_memq", dmemq, u, ta=True, out_dtype=BF16),
         _mm("g_w_gate", dgate, u, ta=True, out_dtype=BF16)], axis=0)

    def pre1_bwd(du, dh1, x, g):
        dx, dg = _rms_bwd(du, x, g)
        return dh1 + dx, dg
    dx, gw["pre1_g"] = _rowwise("rms_pre1_bwd", pre1_bwd, [du, dh1, x2], [pre1_g], [(D_MODEL, F32)], [((1, D_MODEL), F32)])
    return loss[0, 0], dx.reshape(B, S, D_MODEL), gw


TRANSPOSED = ("w_in", "w_ffn_gate", "w_ffn_up", "w_fox_out", "w_rwkv_out", "w_mem_out")
LORA = ("rwkv_w_up", "rwkv_a_up", "rwkv_g_up")
ROW_SHARDED = ("w_mem_kv", "w_o", "w_ffn_down")
FIRST = ("w_in",) + LORA
LATER = ("w_ffn_gate", "w_ffn_up", "w_mem_kv", "w_o", "w_ffn_down", "w_fox_out", "w_rwkv_out", "w_mem_out")
BIG = FIRST + LATER
SMALL = ("pre1_g", "post1_g", "pre2_g", "post2_g", "mem_norm_g", "fox_f_bias", "rwkv_mu", "rwkv_w0", "rwkv_a0", "rwkv_k_k",
         "rwkv_k_a", "rwkv_r_k", "rwkv_gn_g", "rwkv_gn_b")
WEIGHTS = ("pre1_g", "post1_g", "pre2_g", "post2_g", "mem_norm_g", "w_in", "fox_f_bias", "rwkv_mu", "rwkv_w0", "rwkv_w_up",
           "rwkv_a0", "rwkv_a_up", "rwkv_g_up", "rwkv_k_k", "rwkv_k_a", "rwkv_r_k", "rwkv_gn_g", "rwkv_gn_b", "w_mem_kv",
           "w_fox_out", "w_rwkv_out", "w_mem_out", "w_o", "w_ffn_gate", "w_ffn_up", "w_ffn_down")
WIRE_W = 1024
WIRE_ROW_ALIGN = 16
WIRE_HALF_ALIGN = 128


def _wire_rows(name, shard_shape):
    r, c = shard_shape
    if name in ROW_SHARDED:
        return r
    return -(-c // WIRE_ROW_ALIGN) * WIRE_ROW_ALIGN if r == WIRE_W else (r * c) // WIRE_W


def _to_wire(name, a):
    if name not in ROW_SHARDED:
        a = jnp.swapaxes(a, -1, -2)
    lead, (n, w) = a.shape[:-2], a.shape[-2:]
    if w != WIRE_W:
        return a.reshape(lead + ((n * w) // WIRE_W, WIRE_W))
    return jnp.pad(a, [(0, 0)] * len(lead) + [(0, (-n) % WIRE_ROW_ALIGN), (0, 0)])


def _from_wire(name, a, shard_shape):
    r, c = shard_shape
    if name in ROW_SHARDED:
        return a
    return a[..., :c, :] if r == WIRE_W else a.reshape(a.shape[:-2] + (c, r))


def _wire_layout(shard_shapes, names):
    layout, off = {}, 0
    for n in names:
        rows = _wire_rows(n, shard_shapes[n])
        layout[n] = (off, rows)
        off += rows
    return layout, -(-off // (2 * WIRE_HALF_ALIGN)) * WIRE_HALF_ALIGN


def _wire_pack(blocks, half_rows):
    a = jnp.concatenate(blocks, axis=-2)
    lead = a.shape[:-2]
    a = jnp.pad(a, [(0, 0)] * len(lead) + [(0, 2 * half_rows - a.shape[-2]), (0, 0)])
    return a.reshape(lead + (2, half_rows, WIRE_W))


def _my_place():
    return lax.axis_index("x"), lax.axis_index("y"), lax.axis_index("c")


def _other_chips(x, y):
    return [(1 - x, y), (x, 1 - y), (1 - x, 1 - y)]


ANY = pl.BlockSpec(memory_space=pl.ANY)


def _gather_weights(packed):
    _, R, L = packed.shape
    me = 2 * lax.axis_index("x") + lax.axis_index("y")
    base = lax.dynamic_update_index_in_dim(jnp.zeros((N_CHIPS, 2, R, L), packed.dtype), packed, me, 0)

    def body(in_ref, base_ref, out_ref, send_sems, recv_sems):
        x, y, c = _my_place()
        chip = lambda px, py: 2 * px + py
        sibling = (x, y, 1 - c)
        others = _other_chips(x, y)

        def copy(k, src, dst, to):
            return pltpu.make_async_remote_copy(src_ref=src, dst_ref=dst, send_sem=send_sems.at[k], recv_sem=recv_sems.at[k],
                                                device_id=to, device_id_type=MESH)

        sends = [copy(j, in_ref.at[c], out_ref.at[chip(x, y), c], (px, py, c)) for j, (px, py) in enumerate(others)]
        for cp in sends:
            cp.start()
        passed = [copy(3 + j, out_ref.at[chip(px, py), c], out_ref.at[chip(px, py), c], sibling)
                  for j, (px, py) in enumerate(others)]
        for j, (px, py) in enumerate(others):
            copy(j, in_ref.at[c], out_ref.at[chip(px, py), c], (px, py, c)).wait_recv()
            passed[j].start()
        for j, (px, py) in enumerate(others):
            copy(3 + j, in_ref.at[1 - c], out_ref.at[chip(px, py), 1 - c], sibling).wait_recv()
        for cp in sends + passed:
            cp.wait_send()

    return pl.pallas_call(
        body, out_shape=jax.ShapeDtypeStruct(base.shape, base.dtype), in_specs=[ANY, ANY], out_specs=ANY,
        input_output_aliases={1: 0}, scratch_shapes=[pltpu.SemaphoreType.DMA((6,)), pltpu.SemaphoreType.DMA((6,))],
        name="gather_weights",
    )(packed, base)


HBM_SPEC = pl.BlockSpec(memory_space=pltpu.HBM)
SEM_SPEC = pl.BlockSpec(memory_space=pltpu.SEMAPHORE)
DATAFLOW = pltpu.SideEffectType.DATAFLOW_SIDE_EFFECTING


def _in_hbm(a):
    return pltpu.with_memory_space_constraint(a, pltpu.HBM)


def _split_start(name, n_copies, src, land, start_copies):
    def body(src_ref, land_ref, send_sems, recv_sems, src_thru, land_thru, token):
        start_copies(src_ref, land_ref, send_sems, recv_sems)
        token[...] = jnp.zeros(token.shape, token.dtype)

    return pl.pallas_call(
        body, name=name,
        out_shape=(pltpu.SemaphoreType.DMA((n_copies,)), pltpu.SemaphoreType.DMA((n_copies,)), pltpu.HBM(src.shape, src.dtype),
                   pltpu.HBM(land.shape, land.dtype), jax.ShapeDtypeStruct((8, LANES), F32)),
        in_specs=(HBM_SPEC, HBM_SPEC),
        out_specs=(SEM_SPEC, SEM_SPEC, HBM_SPEC, HBM_SPEC, pl.BlockSpec(memory_space=pltpu.VMEM)),
        input_output_aliases={0: 2, 1: 3}, compiler_params=pltpu.CompilerParams(has_side_effects=DATAFLOW),
    )(_in_hbm(src), _in_hbm(land))


def _split_wait(name, handle, after, wait_copies):
    send_sems, recv_sems, src, land = handle[:4]

    def body(src_ref, land_ref, send_sems, recv_sems, after_ref, src_dead, land_out):
        wait_copies(src_ref, land_ref, send_sems, recv_sems)

    return pl.pallas_call(
        body, name=name, out_shape=(pltpu.HBM(src.shape, src.dtype), pltpu.HBM(land.shape, land.dtype)),
        in_specs=(HBM_SPEC, HBM_SPEC, SEM_SPEC, SEM_SPEC, ANY), out_specs=(HBM_SPEC, HBM_SPEC),
        input_output_aliases={0: 0, 1: 1}, compiler_params=pltpu.CompilerParams(has_side_effects=DATAFLOW),
    )(src, land, send_sems, recv_sems, after)[1]


def _late_gather_copies(src_ref, land_ref, send_sems, recv_sems):
    x, y, c = _my_place()
    me = 2 * x + y
    started, awaited = [], []
    for j, (px, py) in enumerate(_other_chips(x, y)):
        for core in range(2):
            started.append(pltpu.make_async_remote_copy(
                src_ref=src_ref.at[c], dst_ref=land_ref.at[me, c], send_sem=send_sems.at[2 * j + core],
                recv_sem=recv_sems.at[2 * j + c], device_id=(px, py, core), device_id_type=MESH))
            awaited.append(pltpu.make_async_remote_copy(
                src_ref=src_ref.at[core], dst_ref=land_ref.at[2 * px + py, core], send_sem=send_sems.at[2 * j + core],
                recv_sem=recv_sems.at[2 * j + core], device_id=(px, py, core), device_id_type=MESH))
    return started, awaited


def _gather_late_start(packed):
    _, R, L = packed.shape
    me = 2 * lax.axis_index("x") + lax.axis_index("y")
    land = lax.dynamic_update_index_in_dim(jnp.zeros((N_CHIPS, 2, R, L), packed.dtype), packed, me, 0)

    def start(src_ref, land_ref, send_sems, recv_sems):
        for cp in _late_gather_copies(src_ref, land_ref, send_sems, recv_sems)[0]:
            cp.start()

    return _split_start("gather_late_start", 6, packed, land, start)


def _gather_late_wait(handle, after):
    def wait(src_ref, land_ref, send_sems, recv_sems):
        started, awaited = _late_gather_copies(src_ref, land_ref, send_sems, recv_sems)
        for cp in started:
            cp.wait_send()
        for cp in awaited:
            cp.wait_recv()

    return _split_wait("gather_late_wait", handle, after, wait)


def _early_scatter_copies(src_ref, land_ref, send_sems, recv_sems):
    x, y, c = _my_place()
    me = 4 * x + 2 * y + c
    copies = []
    for k in range(1, N_DEV):
        px, py, pc = x ^ (k >> 2), y ^ ((k >> 1) & 1), c ^ (k & 1)
        copies.append(pltpu.make_async_remote_copy(
            src_ref=src_ref.at[2 * px + py, pc], dst_ref=land_ref.at[me], send_sem=send_sems.at[k - 1],
            recv_sem=recv_sems.at[k - 1], device_id=(px, py, pc), device_id_type=MESH))
    return copies


def _scatter_early_start(parts):
    n, _, R, L = parts.shape
    x, y, c = _my_place()
    own = lax.dynamic_index_in_dim(lax.dynamic_index_in_dim(parts, 2 * x + y, 0, keepdims=False), c, 0, keepdims=False)
    land = lax.dynamic_update_index_in_dim(jnp.zeros((N_DEV, R, L), parts.dtype), own, 4 * x + 2 * y + c, 0)

    def start(src_ref, land_ref, send_sems, recv_sems):
        for cp in _early_scatter_copies(src_ref, land_ref, send_sems, recv_sems):
            cp.start()

    return _split_start("scatter_early_start", N_DEV - 1, parts, land, start)


def _scatter_early_wait(handle, after):
    def wait(src_ref, land_ref, send_sems, recv_sems):
        for cp in _early_scatter_copies(src_ref, land_ref, send_sems, recv_sems):
            cp.wait_send()
            cp.wait_recv()

    return _split_wait("scatter_early_wait", handle, after, wait)


def _sum_slots(parts):
    n, R, L = parts.shape
    tr = _tile_rows(R, WIRE_HALF_ALIGN)

    def body(p_ref, o_ref):
        acc = p_ref[0].astype(F32)
        for i in range(1, n):
            acc = acc + p_ref[i].astype(F32)
        o_ref[...] = acc

    return pl.pallas_call(
        body, grid=(R // tr,), in_specs=[pl.BlockSpec((n, tr, L), lambda i: (0, i, 0))],
        out_specs=pl.BlockSpec((tr, L), lambda i: (i, 0)), out_shape=jax.ShapeDtypeStruct((R, L), F32), name="sum_slots",
        compiler_params=_params(("parallel",)),
    )(parts)


def _pair_exchange(parts):
    n, _, R, L = parts.shape

    def body(in_ref, out_ref, send_sems, recv_sems):
        x, y, c = _my_place()
        copies = [pltpu.make_async_remote_copy(src_ref=in_ref.at[s, 1 - c], dst_ref=out_ref.at[s], send_sem=send_sems.at[s],
                                               recv_sem=recv_sems.at[s], device_id=(x, y, 1 - c), device_id_type=MESH)
                  for s in range(n)]
        for cp in copies:
            cp.start()
        for cp in copies:
            cp.wait()

    return pl.pallas_call(
        body, out_shape=jax.ShapeDtypeStruct((n, R, L), parts.dtype), in_specs=[ANY], out_specs=ANY,
        scratch_shapes=[pltpu.SemaphoreType.DMA((n,)), pltpu.SemaphoreType.DMA((n,))], name="pair_exchange",
    )(parts)


def _pair_add(a, b):
    n, R, L = a.shape
    tr = _tile_rows(R, WIRE_HALF_ALIGN)

    def body(a_ref, b_ref, o_ref):
        o_ref[...] = (a_ref[...].astype(F32) + b_ref[...].astype(F32)).astype(o_ref.dtype)

    spec = pl.BlockSpec((n, tr, L), lambda i: (0, i, 0))
    return pl.pallas_call(
        body, grid=(R // tr,), in_specs=[spec, spec], out_specs=spec, out_shape=jax.ShapeDtypeStruct(a.shape, a.dtype),
        name="pair_add", compiler_params=_params(("parallel",)),
    )(a, b)


def _scatter_grads(parts):
    n, R, L = parts.shape

    def body(in_ref, out_ref, send_sems, recv_sems):
        x, y, c = _my_place()
        copies = [pltpu.make_async_remote_copy(src_ref=in_ref.at[2 * px + py], dst_ref=out_ref.at[j], send_sem=send_sems.at[j],
                                               recv_sem=recv_sems.at[j], device_id=(px, py, c), device_id_type=MESH)
                  for j, (px, py) in enumerate(_other_chips(x, y))]
        for cp in copies:
            cp.start()
        for cp in copies:
            cp.wait()

    return pl.pallas_call(
        body, out_shape=jax.ShapeDtypeStruct((n - 1, R, L), parts.dtype), in_specs=[ANY], out_specs=ANY,
        scratch_shapes=[pltpu.SemaphoreType.DMA((3,)), pltpu.SemaphoreType.DMA((3,))], name="scatter_grads",
    )(parts)


def _sum_chips(own, others):
    n, R, L = others.shape
    tr = _tile_rows(R, WIRE_HALF_ALIGN)

    def body(a_ref, p_ref, o_ref):
        acc = a_ref[...].astype(F32)
        for i in range(n):
            acc = acc + p_ref[i].astype(F32)
        o_ref[...] = acc

    return pl.pallas_call(
        body, grid=(R // tr,), in_specs=[pl.BlockSpec((tr, L), lambda i: (i, 0)), pl.BlockSpec((n, tr, L), lambda i: (0, i, 0))],
        out_specs=pl.BlockSpec((tr, L), lambda i: (i, 0)), out_shape=jax.ShapeDtypeStruct((R, L), F32), name="sum_grads",
        compiler_params=_params(("parallel",)),
    )(own, others)


def _tile_rows(R, cap=2048):
    best = 8
    for t in range(8, min(R, cap) + 1, 8):
        if R % t == 0:
            best = t
    return best if R % 8 == 0 else R


def _swap_halves(half):
    def body(in_ref, out_ref, send_sem, recv_sem):
        x, y, c = _my_place()
        cp = pltpu.make_async_remote_copy(src_ref=in_ref, dst_ref=out_ref, send_sem=send_sem, recv_sem=recv_sem,
                                          device_id=(x, y, 1 - c), device_id_type=MESH)
        cp.start()
        cp.wait()

    return pl.pallas_call(
        body, out_shape=jax.ShapeDtypeStruct(half.shape, half.dtype), in_specs=[ANY], out_specs=ANY,
        scratch_shapes=[pltpu.SemaphoreType.DMA, pltpu.SemaphoreType.DMA], name="swap_halves",
    )(half)


def _allreduce_small(v):
    R, L = v.shape

    def body(in_ref, out_ref, buf, send_sems, recv_sems):
        x, y, c = _my_place()
        me = 4 * x + 2 * y + c
        buf[me] = in_ref[...]
        started = []
        for k in range(1, N_DEV):
            to = (x ^ (k >> 2), y ^ ((k >> 1) & 1), c ^ (k & 1))
            cp = pltpu.make_async_remote_copy(src_ref=in_ref, dst_ref=buf.at[me], send_sem=send_sems.at[k - 1],
                                              recv_sem=recv_sems.at[k - 1], device_id=to, device_id_type=MESH)
            cp.start()
            started.append(cp)
        for cp in started:
            cp.wait()
        acc = buf[0]
        for i in range(1, N_DEV):
            acc = acc + buf[i]
        out_ref[...] = acc

    vm = pl.BlockSpec(memory_space=pltpu.VMEM)
    return pl.pallas_call(
        body, out_shape=jax.ShapeDtypeStruct((R, L), F32), in_specs=[vm], out_specs=vm,
        scratch_shapes=[pltpu.VMEM((N_DEV, R, L), F32), pltpu.SemaphoreType.DMA((7,)), pltpu.SemaphoreType.DMA((7,))],
        name="allreduce_small",
    )(v)


def _adamw(name, w, g, m, v):
    shape = w.shape
    C = shape[-1]
    R = int(np.prod(shape[:-1]))
    args = [a.reshape(R, C).astype(F32) for a in (w, g, m, v)]
    if R % 8 == 0 or C % LANES != 0:
        tr, tc = _tile_rows(R, 256), C
    else:
        tr, tc = R, LANES

    def body(w_ref, g_ref, m_ref, v_ref, d_ref, nm_ref, nv_ref):
        d_ref[...], nm_ref[...], nv_ref[...] = _adamw_math(w_ref[...], g_ref[...], m_ref[...], v_ref[...])

    spec = pl.BlockSpec((tr, tc), lambda i, j: (i, j))
    out = pl.pallas_call(
        body, grid=(R // tr, C // tc), in_specs=[spec] * 4, out_specs=[spec] * 3,
        out_shape=[jax.ShapeDtypeStruct((R, C), F32)] * 3, name="adamw_" + name,
        compiler_params=_params(("parallel", "parallel")),
    )(*args)
    return [o.reshape(shape) for o in out]


def _adamw_math(w, g, m, v):
    m = ADAM_B1 * m + (1.0 - ADAM_B1) * g
    v = ADAM_B2 * v + (1.0 - ADAM_B2) * (g * g)
    m_hat = m / (1.0 - ADAM_B1 ** ADAM_STEP)
    v_hat = v / (1.0 - ADAM_B2 ** ADAM_STEP)
    return -ADAM_LR * (m_hat / (jnp.sqrt(v_hat) + ADAM_EPS) + ADAM_WD * w), m, v


def _adamw_small(ws, gs, ms, vs):
    n = len(ws)
    two_d = lambda a: a.reshape(-1, a.shape[-1]).astype(F32)
    args = [two_d(a) for group in (ws, gs, ms, vs) for a in group]

    def body(*refs):
        ins, outs = refs[:4 * n], refs[4 * n:]
        for i in range(n):
            res = _adamw_math(*[ins[k * n + i][...] for k in range(4)])
            for k in range(3):
                outs[k * n + i][...] = res[k]

    out = pl.pallas_call(
        body, out_shape=[jax.ShapeDtypeStruct(a.shape, F32) for a in args[:n]] * 3, name="adamw_small",
        compiler_params=_params(),
    )(*args)
    return [[out[k * n + i].reshape(ws[i].shape) for i in range(n)] for k in range(3)]


def kernel(x, mem, pre1_g, post1_g, pre2_g, post2_g, mem_norm_g, w_in, fox_f_bias, rwkv_mu, rwkv_w0, rwkv_w_up, rwkv_a0, rwkv_a_up, rwkv_g_up, rwkv_k_k, rwkv_k_a, rwkv_r_k, rwkv_gn_g, rwkv_gn_b, w_mem_kv, w_fox_out, w_rwkv_out, w_mem_out, w_o, w_ffn_gate, w_ffn_up, w_ffn_down, loss_target, m_pre1_g, m_post1_g, m_pre2_g, m_post2_g, m_mem_norm_g, m_w_in, m_fox_f_bias, m_rwkv_mu, m_rwkv_w0, m_rwkv_w_up, m_rwkv_a0, m_rwkv_a_up, m_rwkv_g_up, m_rwkv_k_k, m_rwkv_k_a, m_rwkv_r_k, m_rwkv_gn_g, m_rwkv_gn_b, m_w_mem_kv, m_w_fox_out, m_w_rwkv_out, m_w_mem_out, m_w_o, m_w_ffn_gate, m_w_ffn_up, m_w_ffn_down, v_pre1_g, v_post1_g, v_pre2_g, v_post2_g, v_mem_norm_g, v_w_in, v_fox_f_bias, v_rwkv_mu, v_rwkv_w0, v_rwkv_w_up, v_rwkv_a0, v_rwkv_a_up, v_rwkv_g_up, v_rwkv_k_k, v_rwkv_k_a, v_rwkv_r_k, v_rwkv_gn_g, v_rwkv_gn_b, v_w_mem_kv, v_w_fox_out, v_w_rwkv_out, v_w_mem_out, v_w_o, v_w_ffn_gate, v_w_ffn_up, v_w_ffn_down):
    given = dict(locals())
    w_loc = {n: given[n] for n in WEIGHTS}
    m_loc = {n: given["m_" + n] for n in WEIGHTS}
    v_loc = {n: given["v_" + n] for n in WEIGHTS}

    shard_shapes = {n: tuple(w_loc[n].shape[1:]) for n in BIG}
    groups = {names: _wire_layout(shard_shapes, names) for names in (FIRST, LATER)}
    core = lax.axis_index("c")

    def pack_weights(names):
        return _wire_pack([_to_wire(n, w_loc[n][0].astype(BF16)) for n in names], groups[names][1])

    def unpack_weights(gathered, names):
        layout, half_rows = groups[names]
        gathered = gathered.reshape(N_CHIPS, 2 * half_rows, WIRE_W)
        out = {}
        for n in names:
            off, rows = layout[n]
            blocks = _from_wire(n, gathered[:, off:off + rows], shard_shapes[n])
            if n in LORA:
                out[n] = blocks.transpose(2, 0, 1).reshape(blocks.shape[2], -1)
            else:
                out[n] = blocks.reshape(-1, blocks.shape[2])
        return out

    def pack_grads(gw, names):
        blocks = []
        for n in names:
            r, c = shard_shapes[n]
            g = gw[n].astype(BF16)
            if n in LORA:
                g = g.reshape(r, N_CHIPS, c).transpose(1, 0, 2)
            elif n in TRANSPOSED:
                g = jnp.swapaxes(g.reshape(N_CHIPS, c, r), 1, 2)
            else:
                g = g.reshape(N_CHIPS, r, c)
            blocks.append(_to_wire(n, g))
        return _wire_pack(blocks, groups[names][1])

    def unpack_grads(half, other, names):
        layout, _ = groups[names]
        reduced = jnp.where(core == 0, jnp.concatenate([half, other]), jnp.concatenate([other, half]))
        out = {}
        for n in names:
            off, rows = layout[n]
            g = _from_wire(n, reduced[off:off + rows], shard_shapes[n])
            out[n] = g.T if n in LORA else g
        return out

    first = _gather_weights(pack_weights(FIRST))
    late = _gather_late_start(pack_weights(LATER) + (first[0, 0, 0, 0] * 0).astype(BF16))
    W = unpack_weights(first, FIRST)
    W.update({n: w_loc[n][0] for n in SMALL})
    W["pre1_g"] = W["pre1_g"] + late[4][0, 0]
    early = []

    def late_weights(after):
        return unpack_weights(_gather_late_wait(late, after), LATER)

    def early_grads(gw, thru):
        early.append(_scatter_early_start(pack_grads(gw, LATER)))
        return thru + early[0][4][0, 0].astype(thru.dtype)

    loss, grad_x, gw = _layer_step(x, mem, loss_target, W, late_weights, early_grads)

    packed = pack_grads(gw, FIRST)
    own_halves = lax.dynamic_index_in_dim(packed, core, axis=1, keepdims=False)
    chip_sums = _pair_add(own_halves, _pair_exchange(packed))
    own_chip = lax.dynamic_index_in_dim(chip_sums, 2 * lax.axis_index("x") + lax.axis_index("y"), axis=0, keepdims=False)
    half_first = _sum_chips(own_chip, _scatter_grads(chip_sums))
    half_later = _sum_slots(_scatter_early_wait(early[0], half_first))
    rows_first = groups[FIRST][1]
    other = _swap_halves(jnp.concatenate([half_first, half_later]))
    g_shard = {**unpack_grads(half_first, other[:rows_first], FIRST), **unpack_grads(half_later, other[rows_first:], LATER)}

    small_shapes = [w_loc[n].shape[1:] for n in SMALL] + [(1,)]
    n_small = sum(int(np.prod(s)) for s in small_shapes)
    small_rows = -(-n_small // (8 * LANES)) * 8
    flat = jnp.concatenate([gw[n].reshape(-1) for n in SMALL] + [loss.reshape(1)])
    flat = jnp.pad(flat, (0, small_rows * LANES - n_small)).reshape(small_rows, LANES).reshape(-1)
    small, off = [], 0
    flat = _allreduce_small(flat.reshape(small_rows, LANES)).reshape(-1)
    for s in small_shapes:
        cnt = int(np.prod(s))
        small.append(flat[off:off + cnt].reshape(s))
        off += cnt
    g_small = dict(zip(SMALL, small[:-1]))
    loss = small[-1][0]

    flip = lambda a: jnp.swapaxes(a, -1, -2)
    result = {}
    small_g = [g_small[n].reshape(w_loc[n].shape) for n in SMALL]
    small = _adamw_small([w_loc[n] for n in SMALL], small_g, [m_loc[n] for n in SMALL], [v_loc[n] for n in SMALL])
    for i, n in enumerate(SMALL):
        result[n] = (small_g[i], small[0][i], small[1][i], small[2][i])
    for n in BIG:
        if n in TRANSPOSED:
            g_t = g_shard[n][None]
            d, nm, nv = (flip(o) for o in _adamw(n, flip(w_loc[n]), g_t, flip(m_loc[n]), flip(v_loc[n])))
            result[n] = (flip(g_t), d, nm, nv)
        else:
            g = g_shard[n].reshape(w_loc[n].shape)
            result[n] = (g, *_adamw(n, w_loc[n], g, m_loc[n], v_loc[n]))
    return (loss, grad_x, *[result[n][k] for k in range(4) for n in WEIGHTS])
```

```python
import numpy as np
import jax
import jax.numpy as jnp
from jax import lax
from jax.experimental import pallas as pl
from jax.experimental.pallas import tpu as pltpu

F32, BF16 = jnp.float32, jnp.bfloat16
MESH = pl.DeviceIdType.MESH

D_MODEL = 1024
HEAD_DIM = 64
N_HEADS = 8
BR_W = 512
MEM_HEADS = 4
MEM_HEAD_DIM = 128
D_FF = 2816
NORM_EPS = 1e-6
GN_EPS = 64e-5
N_CHIPS = 4
N_DEV = 8
LANES = 128
VMEM_LIMIT = 48 * 1024 * 1024

ADAM_LR, ADAM_B1, ADAM_B2, ADAM_EPS, ADAM_WD, ADAM_STEP = 0.001, 0.9, 0.999, 1e-08, 0.01, 10

FOX_COLS = 3 * BR_W + N_HEADS
RWKV_COLS = 3 * BR_W + 64 + 64 + 128
COL_QKV = (0, 3 * BR_W)
COL_F = (3 * BR_W, FOX_COLS)
COL_RW = (FOX_COLS, FOX_COLS + RWKV_COLS)
COL_MQ = (COL_RW[1], COL_RW[1] + BR_W)
COL_GATE = (COL_MQ[1], COL_MQ[1] + 3 * D_MODEL)

NT_DIMS = (((1,), (1,)), ((), ()))
TN_DIMS = (((0,), (0,)), ((), ()))


def _params(sem=None, **kw):
    return pltpu.CompilerParams(dimension_semantics=sem, vmem_limit_bytes=VMEM_LIMIT, **kw)


def _sigmoid(x):
    return 1.0 / (1.0 + jnp.exp(-x))


def _log_sigmoid(x):
    return jnp.minimum(x, 0.0) - jnp.log(1.0 + jnp.exp(-jnp.abs(x)))


def _bdot(a, b, dims=None):
    a, b = a.astype(BF16), b.astype(BF16)
    if dims is None:
        return jnp.dot(a, b, preferred_element_type=F32)
    return lax.dot_general(a, b, dims, preferred_element_type=F32)


def _hdot(a, ones, terms=2):
    ones = ones.astype(BF16)
    dot = lambda x: jnp.dot(x, ones, preferred_element_type=F32)
    hi = a.astype(BF16)
    r1 = a - hi.astype(F32)
    mid = r1.astype(BF16)
    if terms == 2:
        return dot(hi) + dot(mid)
    return dot(hi) + dot(mid) + dot((r1 - mid.astype(F32)).astype(BF16))


def _tile(n, cap):
    best = None
    for t in range(LANES, min(n, cap) + 1, LANES):
        if n % t == 0:
            best = t
    return best or n


def _rowwise(name, fn, rows, consts, outs, accs=(), tm=256):
    T = rows[0].shape[0]
    tm = min(tm, T)
    assert T % tm == 0
    nr, nc, no, na = len(rows), len(consts), len(outs), len(accs)

    def body(*refs):
        res = fn(*[r[...].astype(F32) for r in refs[:nr + nc]])
        if not isinstance(res, (tuple, list)):
            res = (res,)
        orefs, arefs = refs[nr + nc:nr + nc + no], refs[nr + nc + no:]
        for ref, val in zip(orefs, res[:no]):
            ref[...] = val.astype(ref.dtype)
        if na:
            @pl.when(pl.program_id(0) == 0)
            def _():
                for ref in arefs:
                    ref[...] = jnp.zeros(ref.shape, ref.dtype)
            for ref, val in zip(arefs, res[no:]):
                ref[...] += val

    in_specs = ([pl.BlockSpec((tm, r.shape[1]), lambda i: (i, 0)) for r in rows]
                + [pl.BlockSpec(c.shape, lambda i: (0, 0)) for c in consts])
    out_specs = ([pl.BlockSpec((tm, w), lambda i: (i, 0)) for w, _ in outs]
                 + [pl.BlockSpec(s, lambda i: (0, 0)) for s, _ in accs])
    out_shape = ([jax.ShapeDtypeStruct((T, w), dt) for w, dt in outs]
                 + [jax.ShapeDtypeStruct(s, dt) for s, dt in accs])
    return pl.pallas_call(
        body, grid=(T // tm,), in_specs=in_specs, out_specs=out_specs, out_shape=out_shape, name=name,
        compiler_params=_params(("arbitrary",) if na else ("parallel",)),
    )(*rows, *consts)


MM_TILE_CAP = 1408
MM_WHOLE_K = 2816


def _mm(name, a, b, ta=False, tb=False, out_dtype=F32, add=None):
    M, K = (a.shape[1], a.shape[0]) if ta else a.shape
    K2, N = (b.shape[1], b.shape[0]) if tb else b.shape
    assert K == K2
    tm, tn = _tile(M, MM_TILE_CAP), _tile(N, MM_TILE_CAP)
    tk = K if K <= MM_WHOLE_K else _tile(K, MM_TILE_CAP)
    assert M % tm == 0 and N % tn == 0 and K % tk == 0
    nk = K // tk
    a_dim, b_dim = (0 if ta else 1), (1 if tb else 0)

    def body(*refs):
        a_ref, b_ref = refs[0], refs[1]
        n_in = 2 if add is None else 3
        o_ref, acc = refs[n_in], (refs[n_in + 1] if nk > 1 else None)
        k = pl.program_id(2)
        part = lax.dot_general(a_ref[...].astype(BF16), b_ref[...].astype(BF16),
                               (((a_dim,), (b_dim,)), ((), ())), preferred_element_type=F32)

        def finish(r):
            if add is not None:
                r = r + refs[2][...].astype(F32)
            o_ref[...] = r.astype(o_ref.dtype)

        if nk == 1:
            finish(part)
            return

        @pl.when(k == 0)
        def _():
            acc[...] = part

        @pl.when(k > 0)
        def _():
            acc[...] += part

        @pl.when(k == nk - 1)
        def _():
            finish(acc[...])

    a_spec = pl.BlockSpec((tk, tm), lambda i, j, k: (k, i)) if ta else pl.BlockSpec((tm, tk), lambda i, j, k: (i, k))
    b_spec = pl.BlockSpec((tn, tk), lambda i, j, k: (j, k)) if tb else pl.BlockSpec((tk, tn), lambda i, j, k: (k, j))
    o_spec = pl.BlockSpec((tm, tn), lambda i, j, k: (i, j))
    ins, in_specs = [a, b], [a_spec, b_spec]
    if add is not None:
        ins.append(add)
        in_specs.append(o_spec)
    return pl.pallas_call(
        body, grid=(M // tm, N // tn, nk), in_specs=in_specs, out_specs=o_spec,
        out_shape=jax.ShapeDtypeStruct((M, N), out_dtype), scratch_shapes=[pltpu.VMEM((tm, tn), F32)] if nk > 1 else [],
        name=name, compiler_params=_params(("parallel", "parallel", "arbitrary")),
    )(*ins)


def _rowsum(x):
    return jnp.sum(x, axis=0, keepdims=True)


def _rms_stat(x):
    return lax.rsqrt(jnp.mean(x * x, axis=-1, keepdims=True) + NORM_EPS)


def _rms_bwd(dy, x, g):
    r = _rms_stat(x)
    xn = x * r
    dxn = dy * g
    dx = r * (dxn - xn * jnp.mean(dxn * xn, axis=-1, keepdims=True))
    return dx, _rowsum(dy * xn)


def _fox_c_fwd(f8t, bias_col, tc=256):
    B, H, S = f8t.shape
    tc = min(tc, S)

    def body(f_ref, b_ref, c_ref, carry):
        @pl.when(pl.program_id(1) == 0)
        def _():
            carry[...] = jnp.zeros(carry.shape, F32)
        lf = _log_sigmoid(f_ref[...] + b_ref[...])
        row = lax.broadcasted_iota(jnp.int32, (tc, tc), 0)
        col = lax.broadcasted_iota(jnp.int32, (tc, tc), 1)
        c = _hdot(lf, (row <= col).astype(F32), terms=3) + carry[...]
        c_ref[...] = c
        carry[...] = c[:, tc - 1:tc]

    return pl.pallas_call(
        body, grid=(B, S // tc),
        in_specs=[pl.BlockSpec((None, H, tc), lambda b, i: (b, 0, i)), pl.BlockSpec((H, 1), lambda b, i: (0, 0))],
        out_specs=pl.BlockSpec((None, H, tc), lambda b, i: (b, 0, i)),
        out_shape=jax.ShapeDtypeStruct((B, H, S), F32), scratch_shapes=[pltpu.VMEM((H, 1), F32)], name="fox_c_fwd",
        compiler_params=_params(("parallel", "arbitrary")),
    )(f8t, bias_col)


def _fox_c_bwd(dc, f8t, bias_col, tc=256):
    B, H, S = f8t.shape
    tc = min(tc, S)
    n = S // tc

    def body(dc_ref, f_ref, b_ref, df_ref, db_ref, carry):
        @pl.when(pl.program_id(1) == 0)
        def _():
            carry[...] = jnp.zeros(carry.shape, F32)
            db_ref[...] = jnp.zeros(db_ref.shape, F32)
        row = lax.broadcasted_iota(jnp.int32, (tc, tc), 0)
        col = lax.broadcasted_iota(jnp.int32, (tc, tc), 1)
        dlf = _hdot(dc_ref[...], (row >= col).astype(F32), terms=3) + carry[...]
        z = f_ref[...] + b_ref[...]
        df = dlf * (1.0 - _sigmoid(z))
        df_ref[...] = df
        db_ref[...] += jnp.sum(df, axis=1, keepdims=True)
        carry[...] = dlf[:, 0:1]

    rev = lambda b, i: (b, 0, n - 1 - i)
    return pl.pallas_call(
        body, grid=(B, n),
        in_specs=[pl.BlockSpec((None, H, tc), rev), pl.BlockSpec((None, H, tc), rev), pl.BlockSpec((H, 1), lambda b, i: (0, 0))],
        out_specs=[pl.BlockSpec((None, H, tc), rev), pl.BlockSpec((None, H, 1), lambda b, i: (b, 0, 0))],
        out_shape=[jax.ShapeDtypeStruct((B, H, S), F32), jax.ShapeDtypeStruct((B, H, 1), F32)],
        scratch_shapes=[pltpu.VMEM((H, 1), F32)], name="fox_c_bwd",
        compiler_params=_params(("parallel", "arbitrary")),
    )(dc, f8t, bias_col)


NEG_BIG = -1e30


def _fox_logits(q, kj, ckj, i, j, tq, tk, scale):
    s = _bdot(q, kj, NT_DIMS) * scale - ckj
    row = lax.broadcasted_iota(jnp.int32, (tq, tk), 0)
    col = lax.broadcasted_iota(jnp.int32, (tq, tk), 1)
    return s, col <= row + (i * tq - j * tk)


def _fox_key_blocks(i, tq, tk):
    return (i * tq + tq - 1) // tk + 1


def _fox_fwd(qkv, ck, B, tq):
    T = qkv.shape[0]
    S = T // B
    nq, (nk, tk) = S // tq, (ck.shape[1], ck.shape[3])
    scale = HEAD_DIM ** -0.5

    def body(q_ref, k_ref, v_ref, ck_ref, o_ref, lse_ref):
        i = pl.program_id(1)
        lo = lax.broadcasted_iota(jnp.int32, (tq, LANES), 1) < HEAD_DIM
        q = q_ref[...]
        qh = (jnp.where(lo, q, 0), jnp.where(lo, 0, q))

        def step(j, carry):
            rows = pl.ds(pl.multiple_of(j * tk, tk), tk)
            kj, vj = k_ref[rows, :], v_ref[rows, :]
            new = []
            for h in range(2):
                m, l, acc = carry[h]
                s, ok = _fox_logits(qh[h], kj, ck_ref[h, j], i, j, tq, tk, scale)
                s = jnp.where(ok, s, NEG_BIG)
                m2 = jnp.maximum(m, jnp.max(s, axis=1, keepdims=True))
                p = jnp.exp(s - m2)
                al = jnp.exp(m - m2)
                new.append((m2, al * l + jnp.sum(p, axis=1, keepdims=True), al * acc + _bdot(p, vj)))
            return tuple(new)

        init = tuple((jnp.full((tq, 1), NEG_BIG, F32), jnp.zeros((tq, 1), F32), jnp.zeros((tq, LANES), F32)) for _ in range(2))
        (m0, l0, a0), (m1, l1, a1) = lax.fori_loop(0, _fox_key_blocks(i, tq, tk), step, init)
        o_ref[...] = jnp.where(lo, a0 / l0, a1 / l1).astype(o_ref.dtype)
        lse_ref[0] = m0 + jnp.log(l0)
        lse_ref[1] = m1 + jnp.log(l1)

    seq = lambda col0: pl.BlockSpec((S, LANES), lambda g, i: (g // 4, col0 + g % 4))
    blk = lambda col0: pl.BlockSpec((tq, LANES), lambda g, i: ((g // 4) * nq + i, col0 + g % 4))
    col = pl.BlockSpec((2, None, tq, 1), lambda g, i: (g, i, 0, 0))
    return pl.pallas_call(
        body, grid=(B * 4, nq), in_specs=[blk(0), seq(4), seq(8), pl.BlockSpec((2, nk, 1, tk), lambda g, i: (g, 0, 0, 0))],
        out_specs=[blk(0), col],
        out_shape=[jax.ShapeDtypeStruct((T, BR_W), BF16), jax.ShapeDtypeStruct((B * N_HEADS, nq, tq, 1), F32)],
        name="fox_fwd", compiler_params=_params(("parallel", "parallel")),
    )(qkv, qkv, qkv, ck)


def _fox_bwd(qkv, o, do, ck, lse, B):
    T = qkv.shape[0]
    S = T // B
    (nq, tq), (nk, tk) = lse.shape[1:3], (ck.shape[1], ck.shape[3])
    scale = HEAD_DIM ** -0.5

    def body(q_ref, k_ref, v_ref, o_ref, do_ref, ck_ref, lse_ref, dq_ref, dk_ref, dv_ref, dck_ref, dcq_ref,
             dk_acc, dv_acc):
        dk_acc[...] = jnp.zeros(dk_acc.shape, F32)
        dv_acc[...] = jnp.zeros(dv_acc.shape, F32)
        dck_ref[...] = jnp.zeros(dck_ref.shape, F32)
        lo = lax.broadcasted_iota(jnp.int32, (tq, LANES), 1) < HEAD_DIM

        def qloop(i, _):
            qrows = pl.ds(pl.multiple_of(i * tq, tq), tq)
            q, do_i, o_i = q_ref[qrows, :], do_ref[qrows, :], o_ref[qrows, :].astype(F32)
            qh = (jnp.where(lo, q, 0), jnp.where(lo, 0, q))
            doh = (jnp.where(lo, do_i, 0), jnp.where(lo, 0, do_i))
            delta = [jnp.sum(doh[h].astype(F32) * o_i, axis=1, keepdims=True) for h in range(2)]

            def kloop(j, carry):
                krows = pl.ds(pl.multiple_of(j * tk, tk), tk)
                kj, vj = k_ref[krows, :], v_ref[krows, :]
                new = []
                for h in range(2):
                    dq, dcq = carry[h]
                    s, ok = _fox_logits(qh[h], kj, ck_ref[h, j], i, j, tq, tk, scale)
                    p = jnp.where(ok, jnp.exp(s - lse_ref[h, i]), 0.0)
                    ds = p * (_bdot(doh[h], vj, NT_DIMS) - delta[h])
                    dv_acc[krows, :] += _bdot(p, doh[h], TN_DIMS)
                    dk_acc[krows, :] += _bdot(ds, qh[h], TN_DIMS) * scale
                    dck_ref[h, j] += -_rowsum(ds)
                    new.append((dq + _bdot(ds, kj) * scale, dcq + jnp.sum(ds, axis=1, keepdims=True)))
                return tuple(new)

            init = tuple((jnp.zeros((tq, LANES), F32), jnp.zeros((tq, 1), F32)) for _ in range(2))
            (dq0, dcq0), (dq1, dcq1) = lax.fori_loop(0, _fox_key_blocks(i, tq, tk), kloop, init)
            dq_ref[qrows, :] = jnp.where(lo, dq0, dq1).astype(dq_ref.dtype)
            dcq_ref[0, i] = dcq0
            dcq_ref[1, i] = dcq1
            return 0

        lax.fori_loop(0, nq, qloop, 0)
        dk_ref[...] = dk_acc[...].astype(dk_ref.dtype)
        dv_ref[...] = dv_acc[...].astype(dv_ref.dtype)

    seq = lambda col0: pl.BlockSpec((S, LANES), lambda g: (g // 4, col0 + g % 4))
    col = pl.BlockSpec((2, nq, tq, 1), lambda g: (g, 0, 0, 0))
    row = pl.BlockSpec((2, nk, 1, tk), lambda g: (g, 0, 0, 0))
    out = jax.ShapeDtypeStruct((T, BR_W), BF16)
    return pl.pallas_call(
        body, grid=(B * 4,), in_specs=[seq(0), seq(4), seq(8), seq(0), seq(0), row, col],
        out_specs=[seq(0), seq(0), seq(0), row, col],
        out_shape=[out, out, out, jax.ShapeDtypeStruct(ck.shape, F32), jax.ShapeDtypeStruct(lse.shape, F32)],
        scratch_shapes=[pltpu.VMEM((S, LANES), F32), pltpu.VMEM((S, LANES), F32)], name="fox_bwd",
        compiler_params=_params(("parallel",)),
    )(qkv, qkv, qkv, o, do, ck, lse)


def _mem_probs(qh, kh):
    s = _bdot(qh, kh, NT_DIMS) * (MEM_HEAD_DIM ** -0.5)
    e = jnp.exp(s - jnp.max(s, axis=1, keepdims=True))
    return e / jnp.sum(e, axis=1, keepdims=True)


def _mem_fwd(q, mem_kv, B, tq=512):
    T = q.shape[0]
    S, Lm = T // B, mem_kv.shape[0] // B
    tq = min(tq, S)
    n = S // tq

    def body(q_ref, k_ref, v_ref, o_ref):
        for h in range(MEM_HEADS):
            sl = slice(h * MEM_HEAD_DIM, (h + 1) * MEM_HEAD_DIM)
            p = _mem_probs(q_ref[:, sl], k_ref[:, sl])
            o_ref[:, sl] = _bdot(p, v_ref[:, sl]).astype(o_ref.dtype)

    qs = pl.BlockSpec((tq, BR_W), lambda b, i: (b * n + i, 0))
    return pl.pallas_call(
        body, grid=(B, n),
        in_specs=[qs, pl.BlockSpec((Lm, BR_W), lambda b, i: (b, 0)), pl.BlockSpec((Lm, BR_W), lambda b, i: (b, 1))],
        out_specs=qs, out_shape=jax.ShapeDtypeStruct((T, BR_W), BF16), name="mem_fwd",
        compiler_params=_params(("parallel", "parallel")),
    )(q, mem_kv, mem_kv)


def _mem_bwd(q, mem_kv, do, B, tq=512):
    T = q.shape[0]
    S, Lm = T // B, mem_kv.shape[0] // B
    tq = min(tq, S)
    n = S // tq
    scale = MEM_HEAD_DIM ** -0.5

    def body(q_ref, k_ref, v_ref, do_ref, dq_ref, dk_ref, dv_ref):
        @pl.when(pl.program_id(1) == 0)
        def _():
            dk_ref[...] = jnp.zeros(dk_ref.shape, F32)
            dv_ref[...] = jnp.zeros(dv_ref.shape, F32)
        for h in range(MEM_HEADS):
            sl = slice(h * MEM_HEAD_DIM, (h + 1) * MEM_HEAD_DIM)
            qh, kh, vh, doh = q_ref[:, sl], k_ref[:, sl], v_ref[:, sl], do_ref[:, sl]
            p = _mem_probs(qh, kh)
            dp = _bdot(doh, vh, NT_DIMS)
            ds = p * (dp - jnp.sum(p * dp, axis=1, keepdims=True))
            dq_ref[:, sl] = (_bdot(ds, kh) * scale).astype(dq_ref.dtype)
            dk_ref[:, sl] += _bdot(ds, qh, TN_DIMS) * scale
            dv_ref[:, sl] += _bdot(p, doh, TN_DIMS)

    qs = pl.BlockSpec((tq, BR_W), lambda b, i: (b * n + i, 0))
    kv = pl.BlockSpec((Lm, BR_W), lambda b, i: (b, 0))
    return pl.pallas_call(
        body, grid=(B, n),
        in_specs=[qs, kv, pl.BlockSpec((Lm, BR_W), lambda b, i: (b, 1)), qs], out_specs=[qs, kv, kv],
        out_shape=[jax.ShapeDtypeStruct((T, BR_W), BF16), jax.ShapeDtypeStruct((B * Lm, BR_W), F32),
                   jax.ShapeDtypeStruct((B * Lm, BR_W), F32)], name="mem_bwd",
        compiler_params=_params(("parallel", "arbitrary")),
    )(q, mem_kv, mem_kv, do)


def _head_ones():
    h = np.arange(BR_W) // HEAD_DIM
    return jnp.asarray((h[:, None] == h[None, :]).astype(np.float32))


def _rw_prep(p, pp, mu, w0, w1, a0, w2, g_up, k_k, k_a, bd):
    ps = p + (pp - p) * mu
    r, k, v = ps[:, 0:512], ps[:, 512:1024], ps[:, 1024:1536]
    wa, gd = ps[:, 1536:1664], ps[:, 1664:1792]
    th = jnp.tanh(wa)
    z = w0 + _bdot(th, w1)
    wl = -jnp.exp(_log_sigmoid(z) - 0.5)
    w = jnp.exp(wl)
    a = _sigmoid(a0 + _bdot(wa, w2))
    sg = _sigmoid(gd)
    g = _bdot(sg, g_up)
    kq = k * k_k
    n2 = _hdot(kq * kq, bd)
    inv = lax.rsqrt(jnp.maximum(n2, 1e-24))
    kk = kq * inv
    k2 = k * (1.0 + (a - 1.0) * k_a)
    return dict(ps=ps, r=r, k=k, v=v, wa=wa, th=th, z=z, wl=wl, w=w, a=a, sg=sg, g=g, kq=kq, n2=n2, inv=inv, kk=kk, k2=k2)


def _keycol_selector(tm, Tc):
    e = np.zeros((tm, (tm // Tc) * LANES), np.float32)
    for t in range(tm):
        c, tl = divmod(t, Tc)
        e[t, c * LANES + tl] = e[t, c * LANES + Tc + tl] = 1.0
    return jnp.asarray(e, BF16)


def _rw_prep_fwd(p, pp, consts, B, Tc, tm=256):
    assert 2 * Tc == LANES
    T = p.shape[0]
    S = T // B
    nb, cpb = S // tm, tm // Tc
    sel = _keycol_selector(tm, Tc)
    nc = len(consts)

    def body(*refs):
        t = _rw_prep(*[r[...] for r in refs[:2 + nc]])
        sel_ref = refs[2 + nc]
        rows, cols = refs[3 + nc:7 + nc], refs[7 + nc:]
        for ref, val in zip(rows, (t["r"], t["k2"], t["v"], t["g"])):
            ref[...] = val
        lo = lax.broadcasted_iota(jnp.int32, (HEAD_DIM, LANES), 1) < HEAD_DIM
        operands = (t["w"], -t["kk"], t["kk"] * t["a"], t["k2"], t["r"])
        for n, (ref, x) in enumerate(zip(cols, operands)):
            terms = _split_bf16(x) if ref.dtype == F32 else (x.astype(BF16),)
            for hp in range(4):
                xt = sum(lax.dot_general(tt[:, hp * LANES:(hp + 1) * LANES], sel_ref[...], TN_DIMS,
                                         preferred_element_type=F32) for tt in terms)
                for c in range(cpb):
                    blk = xt[:, c * LANES:(c + 1) * LANES]
                    ref[hp, c] = jnp.where(lo, blk[0:HEAD_DIM], blk[HEAD_DIM:2 * HEAD_DIM]).astype(ref.dtype)

    row_spec = lambda w: pl.BlockSpec((tm, w), lambda i: (i, 0))
    col_spec = pl.BlockSpec((None, 4, cpb, HEAD_DIM, LANES), lambda i: (i // nb, 0, i % nb, 0, 0))
    col_shape = lambda dt: jax.ShapeDtypeStruct((B, 4, S // Tc, HEAD_DIM, LANES), dt)
    out = pl.pallas_call(
        body, grid=(T // tm,),
        in_specs=[row_spec(RWKV_COLS)] * 2 + [pl.BlockSpec(c.shape, lambda i: (0, 0)) for c in consts]
        + [pl.BlockSpec(sel.shape, lambda i: (0, 0))],
        out_specs=[row_spec(BR_W)] * 4 + [col_spec] * 5,
        out_shape=[jax.ShapeDtypeStruct((T, BR_W), F32)] * 4 + [col_shape(F32)] + [col_shape(BF16)] * 3 + [col_shape(F32)],
        name="rwkv_prep_fwd", compiler_params=_params(("parallel",)),
    )(p, pp, *consts, sel)
    return out[:4], [c.reshape(B * 4, S // Tc, HEAD_DIM, LANES) for c in out[4:]]


def _rw_prep_bwd(p, pp, cots, consts):
    def fn(p, pp, dr1, dr2, dw, dk21, dk22, dv1, dv2, dav, dbv, dg, mu, w0, w1, a0, w2, g_up, k_k, k_a, bd):
        t = _rw_prep(p, pp, mu, w0, w1, a0, w2, g_up, k_k, k_a, bd)
        dr, dk2, dv = dr1 + dr2, dk21 + dk22, dv1 + dv2
        a, k, kk, kq, inv = t["a"], t["k"], t["kk"], t["kq"], t["inv"]
        dkk = dbv * a - dav
        da = dbv * kk + dk2 * k * k_a
        dk = dk2 * (1.0 + (a - 1.0) * k_a)
        d_k_a = _rowsum(dk2 * k * (a - 1.0))
        proj = _hdot(dkk * kq, bd)
        dkq = dkk * inv - jnp.where(t["n2"] > 1e-24, kq * inv * inv * inv * proj, 0.0)
        dk = dk + dkq * k_k
        d_k_k = _rowsum(dkq * k)
        dpa = da * a * (1.0 - a)
        d_a0 = _rowsum(dpa)
        dwa = _bdot(dpa, w2, NT_DIMS)
        d_w2 = _bdot(t["wa"], dpa, TN_DIMS)
        dz = dw * t["w"] * t["wl"] * (1.0 - _sigmoid(t["z"]))
        d_w0 = _rowsum(dz)
        th = t["th"]
        dwa = dwa + _bdot(dz, w1, NT_DIMS) * (1.0 - th * th)
        d_w1 = _bdot(th, dz, TN_DIMS)
        sg = t["sg"]
        dgd = _bdot(dg, g_up, NT_DIMS) * sg * (1.0 - sg)
        d_g_up = _bdot(sg, dg, TN_DIMS)
        dps = jnp.concatenate([dr, dk, dv, dwa, dgd], axis=1)
        d_mu = _rowsum(dps * (pp - p))
        return dps * (1.0 - mu), dps * mu, d_mu, d_w0, d_w1, d_a0, d_w2, d_g_up, d_k_k, d_k_a

    accs = [((1, RWKV_COLS), F32), ((1, BR_W), F32), ((LANES, BR_W), F32), ((1, BR_W), F32), ((LANES, BR_W), F32),
            ((LANES, BR_W), F32), ((1, BR_W), F32), ((1, BR_W), F32)]
    return _rowwise("rwkv_prep_bwd", fn, [p, pp] + list(cots), consts, [(RWKV_COLS, BF16)] * 2, accs, tm=256)


def _rw_head(y, r, k2, v, g, gn_g, gn_b, r_k, bd):
    mean = _hdot(y, bd) * (1.0 / HEAD_DIM)
    yc = y - mean
    rs = lax.rsqrt(_hdot(yc * yc, bd) * (1.0 / HEAD_DIM) + GN_EPS)
    yn = yc * rs
    bs = _hdot(r * k2 * r_k, bd)
    return yn, rs, bs, yn * gn_g + gn_b + bs * v


def _rw_head_fwd(y, r, k2, v, g, consts):
    def fn(y, r, k2, v, g, *c):
        return _rw_head(y, r, k2, v, g, *c)[3] * g
    return _rowwise("rwkv_head_fwd", fn, [y, r, k2, v, g], consts, [(BR_W, BF16)])[0]


def _rw_head_bwd(dout, y, r, k2, v, g, consts):
    def fn(dout, y, r, k2, v, g, gn_g, gn_b, r_k, bd):
        dout = dout.astype(F32)
        yn, rs, bs, zz = _rw_head(y, r, k2, v, g, gn_g, gn_b, r_k, bd)
        dg = dout * zz
        dz = dout * g
        dyn = dz * gn_g
        inv_n = 1.0 / HEAD_DIM
        dy = rs * (dyn - _hdot(dyn, bd) * inv_n - yn * (_hdot(dyn * yn, bd) * inv_n))
        dq = _hdot(dz * v, bd)
        return dy, dg, dq * k2 * r_k, dq * r * r_k, dz * bs, _rowsum(dz * yn), _rowsum(dz), _rowsum(dq * r * k2)
    return _rowwise("rwkv_head_bwd", fn, [dout, y, r, k2, v, g], consts, [(BR_W, F32)] * 5, [((1, BR_W), F32)] * 3)


SCAN_TC = 64


def _scan_onehot(Tc):
    w = np.zeros((Tc // 2, 2 * Tc, 2 * LANES), np.float32)
    for tt in range(Tc // 2):
        for u in range(2):
            for h in range(2):
                w[tt, h * Tc + 2 * tt + u, u * LANES + h * HEAD_DIM: u * LANES + (h + 1) * HEAD_DIM] = 1.0
    return jnp.asarray(w, BF16)


def _split_bf16(x):
    hi = x.astype(BF16)
    return hi, (x - hi.astype(F32)).astype(BF16)


def _key_tiles(l_w, others, onehot):
    dot = lambda x: jnp.dot(x, onehot, preferred_element_type=F32)
    whi, wmid = l_w
    return [dot(whi) + dot(wmid)] + [dot(o) for o in others]


def _rw_scan_fwd(LW, LA, LB, LK, LR, v, P=4):
    NP, nc, _, Tc2 = LW.shape
    Tc = Tc2 // 2
    S = nc * Tc
    onehot = _scan_onehot(Tc)
    npb = 4 // P

    def body(lw, la, lb, lk, lr, v_ref, oh_ref, y_ref, sa_ref, sb_ref, st):
        @pl.when(pl.program_id(1) == 0)
        def _():
            st[...] = jnp.zeros(st.shape, F32)
        s = [st[p] for p in range(P)]
        cols = [(_split_bf16(lw[p]), [ref[p].astype(BF16) for ref in (la, lb, lk)]) for p in range(P)]
        r_cols = [lr[p].astype(F32) for p in range(P)]
        head0 = lax.broadcasted_iota(jnp.int32, (HEAD_DIM, LANES), 1) < HEAD_DIM
        for tt in range(Tc // 2):
            tiles = [_key_tiles(c[0], c[1], oh_ref[tt]) for c in cols]
            for u in range(2):
                t = 2 * tt + u
                for p in range(P):
                    W, A, Bt, Kt = (x[:, u * LANES:(u + 1) * LANES] for x in tiles[p])
                    R = jnp.where(head0, r_cols[p][:, t:t + 1], r_cols[p][:, Tc + t:Tc + t + 1])
                    ls = slice(p * LANES, (p + 1) * LANES)
                    sb_ref[p, t] = s[p]
                    sa = _rowsum(s[p] * A)
                    s[p] = s[p] * W + Bt * sa + Kt * v_ref[t:t + 1, ls]
                    y_ref[t:t + 1, ls] = _rowsum(s[p] * R)
                    sa_ref[t:t + 1, ls] = sa
        for p in range(P):
            st[p] = s[p]

    lspec = pl.BlockSpec((P, None, HEAD_DIM, Tc2), lambda g, c: (g, c, 0, 0))
    rows = pl.BlockSpec((Tc, P * LANES), lambda g, c: ((g // npb) * nc + c, g % npb))
    rowshape = jax.ShapeDtypeStruct(v.shape, F32)
    return pl.pallas_call(
        body, grid=(NP // P, nc), in_specs=[lspec] * 5 + [rows, pl.BlockSpec(onehot.shape, lambda g, c: (0, 0, 0))],
        out_specs=[rows, rows, pl.BlockSpec((P, Tc, HEAD_DIM, LANES), lambda g, c: (g, c, 0, 0))],
        out_shape=[rowshape, rowshape, jax.ShapeDtypeStruct((NP, S, HEAD_DIM, LANES), F32)],
        scratch_shapes=[pltpu.VMEM((P, HEAD_DIM, LANES), F32)], name="rwkv_scan_fwd",
        compiler_params=_params(("parallel", "arbitrary")),
    )(LW, LA, LB, LK, LR, v, onehot)


SCAN_G_ROWS = 16


def _rw_scan_bwd(LW, LA, LB, LK, LR, v, sa, dy, sb, P=4):
    NP, nc, _, Tc2 = LW.shape
    Tc = Tc2 // 2
    onehot = _scan_onehot(Tc)
    npb = 4 // P

    def body(lw, la, lb, lk, lr, v_ref, sa_ref, dy_ref, sb_ref, oh_ref, dv_ref, dk_ref, db_ref, dw_ref, dr_ref, da_ref, dst):
        @pl.when(pl.program_id(1) == 0)
        def _():
            dst[...] = jnp.zeros(dst.shape, F32)
        rid = lax.broadcasted_iota(jnp.int32, (SCAN_G_ROWS, LANES), 0)
        lane = lax.broadcasted_iota(jnp.int32, (SCAN_G_ROWS, LANES), 1)
        own = (((rid % 2) == 0) == (lane < HEAD_DIM)) & (rid < 10)
        lo = lane[0:1] < HEAD_DIM
        nt = lambda rows, tile: lax.dot_general(rows.astype(BF16), tile.astype(BF16), NT_DIMS, preferred_element_type=F32)
        ds = [dst[p] for p in range(P)]
        cols = [(_split_bf16(lw[p]), [ref[p].astype(BF16) for ref in (la, lb, lk)]) for p in range(P)]
        r_cols = [lr[p].astype(F32) for p in range(P)]
        head0 = lax.broadcasted_iota(jnp.int32, (HEAD_DIM, LANES), 1) < HEAD_DIM
        for tt in reversed(range(Tc // 2)):
            tiles = [_key_tiles(c[0], c[1], oh_ref[tt]) for c in cols]
            pending = [[] for _ in range(P)]
            for u in (1, 0):
                t = 2 * tt + u
                for p in range(P):
                    W, A, Bt, Kt = (x[:, u * LANES:(u + 1) * LANES] for x in tiles[p])
                    R = jnp.where(head0, r_cols[p][:, t:t + 1], r_cols[p][:, Tc + t:Tc + t + 1])
                    ls = slice(p * LANES, (p + 1) * LANES)
                    vr, sar, dyr = (ref[t:t + 1, ls] for ref in (v_ref, sa_ref, dy_ref))
                    sp = sb_ref[p, t]
                    s_t = sp * W + Bt * sar + Kt * vr
                    d = ds[p] + R * dyr
                    dv_ref[t:t + 1, ls] = _rowsum(d * Kt)
                    dsar = _rowsum(d * Bt)
                    rows = jnp.where(rid < 2, vr, jnp.where(rid < 4, sar, jnp.where(rid < 6, 1.0, jnp.where(rid < 8, dyr, dsar))))
                    pending[p].append((t, jnp.where(own, rows, 0.0), [d, d * sp, s_t, sp]))
                    ds[p] = d * W + A * dsar
            for p in range(P):
                ls = slice(p * LANES, (p + 1) * LANES)
                (t1, rows1, tiles1), (t0, rows0, tiles0) = pending[p]
                g2 = nt(jnp.concatenate([rows1, rows0], axis=0), jnp.concatenate(tiles1 + tiles0, axis=0))
                for t, g in ((t1, g2[0:SCAN_G_ROWS, 0:2 * LANES]), (t0, g2[SCAN_G_ROWS:, 2 * LANES:])):
                    ga, gb = g[:, 0:LANES], g[:, LANES:2 * LANES]
                    ra, rb = pltpu.roll(ga, HEAD_DIM, 1), pltpu.roll(gb, HEAD_DIM, 1)
                    dk_ref[t:t + 1, ls] = jnp.where(lo, ga[0:1], ra[1:2])
                    db_ref[t:t + 1, ls] = jnp.where(lo, ga[2:3], ra[3:4])
                    dw_ref[t:t + 1, ls] = jnp.where(lo, ra[4:5], ga[5:6])
                    dr_ref[t:t + 1, ls] = jnp.where(lo, gb[6:7], rb[7:8])
                    da_ref[t:t + 1, ls] = jnp.where(lo, rb[8:9], gb[9:10])
        for p in range(P):
            dst[p] = ds[p]

    rev = lambda g, c: (g, nc - 1 - c, 0, 0)
    lspec = pl.BlockSpec((P, None, HEAD_DIM, Tc2), rev)
    rows = pl.BlockSpec((Tc, P * LANES), lambda g, c: ((g // npb) * nc + nc - 1 - c, g % npb))
    return pl.pallas_call(
        body, grid=(NP // P, nc),
        in_specs=[lspec] * 5 + [rows] * 3 + [pl.BlockSpec((P, Tc, HEAD_DIM, LANES), rev),
                                             pl.BlockSpec(onehot.shape, lambda g, c: (0, 0, 0))],
        out_specs=[rows] * 6, out_shape=[jax.ShapeDtypeStruct(v.shape, F32)] * 6,
        scratch_shapes=[pltpu.VMEM((P, HEAD_DIM, LANES), F32)], name="rwkv_scan_bwd",
        compiler_params=_params(("parallel", "arbitrary")),
    )(LW, LA, LB, LK, LR, v, sa, dy, sb, onehot)


def _shift_prev(p, B):
    T, W = p.shape
    return jnp.pad(p.reshape(B, T // B, W), ((0, 0), (1, 0), (0, 0)))[:, :-1].reshape(T, W)


def _shift_next(p, B):
    T, W = p.shape
    return jnp.pad(p.reshape(B, T // B, W), ((0, 0), (0, 1), (0, 0)))[:, 1:].reshape(T, W)


FOX_FWD_BLOCKS = (512, 1024)
FOX_BWD_BLOCKS = (512, 512)


def _layer_step(x, mem, target, W, late_weights=None, early_grads=None, scan_tc=SCAN_TC, fox_fwd_t=FOX_FWD_BLOCKS,
                fox_bwd_t=FOX_BWD_BLOCKS):
    B, S, _ = x.shape
    T = B * S
    x2, tgt2 = x.reshape(T, D_MODEL), target.reshape(T, D_MODEL)
    mem2 = mem.reshape(-1, D_MODEL)
    w_in_t = W["w_in"]
    wt_qkv, wt_rw, wt_mq, wt_gate = (w_in_t[lo:hi] for lo, hi in (COL_QKV, COL_RW, COL_MQ, COL_GATE))
    wt_f = jnp.pad(w_in_t[COL_F[0]:COL_F[1]], ((0, LANES - N_HEADS), (0, 0)))
    row = lambda v: v.reshape(1, -1).astype(F32)
    pre1_g, post1_g, pre2_g, post2_g, mem_g = (row(W[n]) for n in ("pre1_g", "post1_g", "pre2_g", "post2_g", "mem_norm_g"))

    u = _rowwise("rms_pre1", lambda x, g: x * _rms_stat(x) * g, [x2], [pre1_g], [(D_MODEL, BF16)])[0]
    qkv = _mm("proj_qkv", u, wt_qkv, tb=True, out_dtype=BF16)
    f_pad = _mm("proj_f", u, wt_f, tb=True)
    p_rw = _mm("proj_rwkv", u, wt_rw, tb=True)
    memq = _mm("proj_memq", u, wt_mq, tb=True, out_dtype=BF16)
    gate = _mm("proj_gate", u, wt_gate, tb=True, out_dtype=BF16)

    bias_col = W["fox_f_bias"].reshape(N_HEADS, 1).astype(F32)
    f8t = f_pad[:, :N_HEADS].reshape(B, S, N_HEADS).transpose(0, 2, 1)
    c = _fox_c_fwd(f8t, bias_col)
    G = B * N_HEADS
    q_blocks = lambda a, t: a.reshape(G, S // min(t, S), min(t, S), 1)
    k_blocks = lambda a, t: a.reshape(G, S // min(t, S), 1, min(t, S))
    fox_out, lse = _fox_fwd(qkv, k_blocks(c, fox_fwd_t[1]), B, min(fox_fwd_t[0], S))

    bd = _head_ones()
    zpad = jnp.zeros((64, BR_W), F32)
    w1 = jnp.concatenate([W["rwkv_w_up"].astype(F32), zpad], axis=0)
    w2 = jnp.concatenate([zpad, W["rwkv_a_up"].astype(F32)], axis=0)
    prep_consts = [row(W["rwkv_mu"]), row(W["rwkv_w0"]), w1, row(W["rwkv_a0"]), w2, W["rwkv_g_up"].astype(F32),
                   row(W["rwkv_k_k"]), row(W["rwkv_k_a"]), bd]
    p_prev = _shift_prev(p_rw, B)
    (rr, rk2, rv, rg), scan_cols = _rw_prep_fwd(p_rw, p_prev, prep_consts, B, scan_tc)
    ry, rsa, sb = _rw_scan_fwd(*scan_cols, rv)
    head_consts = [row(W["rwkv_gn_g"]), row(W["rwkv_gn_b"]), row(W["rwkv_r_k"]), bd]
    rwkv_out = _rw_head_fwd(ry, rr, rk2, rv, rg, head_consts)

    if late_weights is not None:
        W = {**W, **late_weights(rwkv_out)}

    mn = _rowwise("rms_mem", lambda m, g: m * _rms_stat(m) * g, [mem2], [mem_g], [(D_MODEL, BF16)])[0]
    mem_kv = _mm("proj_memkv", mn, W["w_mem_kv"], out_dtype=BF16)
    mem_out = _mem_fwd(memq, mem_kv, B)

    fo = [_mm("branch_" + n, a, W[n], tb=True, out_dtype=BF16)
          for n, a in (("w_fox_out", fox_out), ("w_rwkv_out", rwkv_out), ("w_mem_out", mem_out))]

    def merge(gate, f0, f1, f2):
        return sum(_sigmoid(gate[:, i * D_MODEL:(i + 1) * D_MODEL]) * f for i, f in enumerate((f0, f1, f2)))
    merged = _rowwise("merge", merge, [gate] + fo, [], [(D_MODEL, BF16)])[0]
    y1 = _mm("proj_o", merged, W["w_o"])

    def mid(x, y1, g1, g2):
        h1 = x + y1 * _rms_stat(y1) * g1
        return h1, h1 * _rms_stat(h1) * g2
    h1, u2 = _rowwise("norm_mid", mid, [x2, y1], [post1_g, pre2_g], [(D_MODEL, F32), (D_MODEL, BF16)])
    gt = _mm("ffn_gate", u2, W["w_ffn_gate"], tb=True, out_dtype=BF16)
    up = _mm("ffn_up", u2, W["w_ffn_up"], tb=True, out_dtype=BF16)
    act = _rowwise("swiglu", lambda gt, up: gt * _sigmoid(gt) * up, [gt, up], [], [(D_FF, BF16)])[0]
    ffn = _mm("ffn_down", act, W["w_ffn_down"])

    def tail(h1, ffn, tgt, g):
        err = h1 + ffn * _rms_stat(ffn) * g - tgt
        dh2 = err * (1.0 / D_MODEL)
        dffn, dg = _rms_bwd(dh2, ffn, g)
        loss = 0.5 * jnp.sum(jnp.sum(err * err, axis=1, keepdims=True) * (1.0 / D_MODEL), axis=0, keepdims=True)
        return dh2, dffn, dg, jnp.broadcast_to(loss, (1, LANES))
    dh2, dffn, d_post2, loss = _rowwise("loss_tail", tail, [h1, ffn, tgt2], [post2_g], [(D_MODEL, F32), (D_MODEL, BF16)],
                                        [((1, D_MODEL), F32), ((1, LANES), F32)])
    gw = {"post2_g": d_post2}
    dact = _mm("d_act", dffn, W["w_ffn_down"], tb=True, out_dtype=BF16)
    gw["w_ffn_down"] = _mm("g_ffn_down", act, dffn, ta=True, out_dtype=BF16)

    def swiglu_bwd(dact, gt, up):
        s = _sigmoid(gt)
        return dact * up * s * (1.0 + gt * (1.0 - s)), dact * gt * s
    dgt, dup = _rowwise("swiglu_bwd", swiglu_bwd, [dact, gt, up], [], [(D_FF, BF16)] * 2)
    du2 = _mm("d_u2_gate", dgt, W["w_ffn_gate"])
    du2 = _mm("d_u2_up", dup, W["w_ffn_up"], add=du2)
    gw["w_ffn_gate"] = _mm("g_ffn_gate", dgt, u2, ta=True, out_dtype=BF16)
    gw["w_ffn_up"] = _mm("g_ffn_up", dup, u2, ta=True, out_dtype=BF16)

    def mid_bwd(du2, dh2, h1, y1, g1, g2):
        dh1_n, d_pre2 = _rms_bwd(du2, h1, g2)
        dh1 = dh2 + dh1_n
        dy1, d_post1 = _rms_bwd(dh1, y1, g1)
        return dh1, dy1, d_post1, d_pre2
    dh1, dy1, gw["post1_g"], gw["pre2_g"] = _rowwise(
        "norm_mid_bwd", mid_bwd, [du2, dh2, h1, y1], [post1_g, pre2_g], [(D_MODEL, F32), (D_MODEL, BF16)],
        [((1, D_MODEL), F32)] * 2)
    dmerged = _mm("d_merged", dy1, W["w_o"], tb=True, out_dtype=BF16)
    gw["w_o"] = _mm("g_w_o", merged, dy1, ta=True, out_dtype=BF16)

    def merge_bwd(dm, gate, f0, f1, f2):
        s = [_sigmoid(gate[:, i * D_MODEL:(i + 1) * D_MODEL]) for i in range(3)]
        dgate = jnp.concatenate([dm * f * si * (1.0 - si) for f, si in zip((f0, f1, f2), s)], axis=1)
        return dm * s[0], dm * s[1], dm * s[2], dgate
    dfo0, dfo1, dfo2, dgate = _rowwise("merge_bwd", merge_bwd, [dmerged, gate] + fo, [],
                                       [(D_MODEL, BF16)] * 3 + [(3 * D_MODEL, BF16)])
    d_branch = {}
    for n, a, dfo in (("w_fox_out", fox_out, dfo0), ("w_rwkv_out", rwkv_out, dfo1), ("w_mem_out", mem_out, dfo2)):
        d_branch[n] = _mm("d_in_" + n, dfo, W[n], out_dtype=BF16)
        gw[n] = _mm("g_" + n, dfo, a, ta=True, out_dtype=BF16)

    dmemq, dkm, dvm = _mem_bwd(memq, mem_kv, d_branch["w_mem_out"], B)
    dmem_kv = jnp.concatenate([dkm, dvm], axis=1)
    gw["w_mem_kv"] = _mm("g_w_mem_kv", mn, dmem_kv, ta=True, out_dtype=BF16)
    dmn = _mm("d_mn", dmem_kv, W["w_mem_kv"], tb=True)
    gw["mem_norm_g"] = _rowwise("rms_mem_bwd", lambda d, m, g: _rms_bwd(d, m, g)[1], [dmn, mem2], [mem_g], [],
                                [((1, D_MODEL), F32)])[0]
    if early_grads is not None:
        c = early_grads(gw, c)

    dfq, dfk, dfv, dck, dcq = _fox_bwd(qkv, fox_out, d_branch["w_fox_out"], k_blocks(c, fox_bwd_t[1]),
                                       q_blocks(lse, fox_bwd_t[0]), B)
    df8t, dbias = _fox_c_bwd(dck.reshape(B, N_HEADS, S) + dcq.reshape(B, N_HEADS, S), f8t, bias_col)
    gw["fox_f_bias"] = jnp.sum(dbias, axis=0).reshape(1, N_HEADS)
    dqkv = jnp.concatenate([dfq, dfk, dfv], axis=1)
    df_pad = jnp.pad(df8t.transpose(0, 2, 1).reshape(T, N_HEADS), ((0, 0), (0, LANES - N_HEADS))).astype(BF16)

    dry, drg, dr_h, dk2_h, dv_h, gw["rwkv_gn_g"], gw["rwkv_gn_b"], gw["rwkv_r_k"] = _rw_head_bwd(
        d_branch["w_rwkv_out"], ry, rr, rk2, rv, rg, head_consts)
    dv_s, dk2_s, db_s, dw_s, dr_s, da_s = _rw_scan_bwd(*scan_cols, rv, rsa, dry, sb)
    dP, dPp, gw["rwkv_mu"], gw["rwkv_w0"], d_w1, gw["rwkv_a0"], d_w2, gw["rwkv_g_up"], gw["rwkv_k_k"], gw["rwkv_k_a"] = \
        _rw_prep_bwd(p_rw, p_prev, [dr_s, dr_h, dw_s, dk2_s, dk2_h, dv_s, dv_h, da_s, db_s, drg], prep_consts)
    gw["rwkv_w_up"], gw["rwkv_a_up"] = d_w1[:64], d_w2[64:]
    dp_rw = dP + _shift_next(dPp, B)

    du = _mm("d_u_qkv", dqkv, wt_qkv)
    du = _mm("d_u_f", df_pad, wt_f, add=du)
    du = _mm("d_u_rwkv", dp_rw, wt_rw, add=du)
    du = _mm("d_u_memq", dmemq, wt_mq, add=du)
    du = _mm("d_u_gate", dgate, wt_gate, add=du)
    gw["w_in"] = jnp.concatenate(
        [_mm("g_w_qkv", dqkv, u, ta=True, out_dtype=BF16), _mm("g_w_f", df_pad, u, ta=True, out_dtype=BF16)[:N_HEADS],
         _mm("g_w_rwkv", dp_rw, u, ta=True, out_dtype=BF16), _mm("g_w_memq", dmemq, u, ta=True, out_dtype=BF16),
         _mm("g_w_gate", dgate, u, ta=True, out_dtype=BF16)], axis=0)

    def pre1_bwd(du, dh1, x, g):
        dx, dg = _rms_bwd(du, x, g)
        return dh1 + dx, dg
    dx, gw["pre1_g"] = _rowwise("rms_pre1_bwd", pre1_bwd, [du, dh1, x2], [pre1_g], [(D_MODEL, F32)], [((1, D_MODEL), F32)])
    return loss[0, 0], dx.reshape(B, S, D_MODEL), gw


TRANSPOSED = ("w_in", "w_ffn_gate", "w_ffn_up", "w_fox_out", "w_rwkv_out", "w_mem_out")
LORA = ("rwkv_w_up", "rwkv_a_up", "rwkv_g_up")
ROW_SHARDED = ("w_mem_kv", "w_o", "w_ffn_down")
FIRST = ("w_in",) + LORA
LATER = ("w_ffn_gate", "w_ffn_up", "w_mem_kv", "w_o", "w_ffn_down", "w_fox_out", "w_rwkv_out", "w_mem_out")
BIG = FIRST + LATER
SMALL = ("pre1_g", "post1_g", "pre2_g", "post2_g", "mem_norm_g", "fox_f_bias", "rwkv_mu", "rwkv_w0", "rwkv_a0", "rwkv_k_k",
         "rwkv_k_a", "rwkv_r_k", "rwkv_gn_g", "rwkv_gn_b")
WEIGHTS = ("pre1_g", "post1_g", "pre2_g", "post2_g", "mem_norm_g", "w_in", "fox_f_bias", "rwkv_mu", "rwkv_w0", "rwkv_w_up",
           "rwkv_a0", "rwkv_a_up", "rwkv_g_up", "rwkv_k_k", "rwkv_k_a", "rwkv_r_k", "rwkv_gn_g", "rwkv_gn_b", "w_mem_kv",
           "w_fox_out", "w_rwkv_out", "w_mem_out", "w_o", "w_ffn_gate", "w_ffn_up", "w_ffn_down")
WIRE_W = 1024
WIRE_ROW_ALIGN = 16
WIRE_HALF_ALIGN = 128


def _wire_rows(name, shard_shape):
    r, c = shard_shape
    if name in ROW_SHARDED:
        return r
    return -(-c // WIRE_ROW_ALIGN) * WIRE_ROW_ALIGN if r == WIRE_W else (r * c) // WIRE_W


def _to_wire(name, a):
    if name not in ROW_SHARDED:
        a = jnp.swapaxes(a, -1, -2)
    lead, (n, w) = a.shape[:-2], a.shape[-2:]
    if w != WIRE_W:
        return a.reshape(lead + ((n * w) // WIRE_W, WIRE_W))
    return jnp.pad(a, [(0, 0)] * len(lead) + [(0, (-n) % WIRE_ROW_ALIGN), (0, 0)])


def _from_wire(name, a, shard_shape):
    r, c = shard_shape
    if name in ROW_SHARDED:
        return a
    return a[..., :c, :] if r == WIRE_W else a.reshape(a.shape[:-2] + (c, r))


def _wire_layout(shard_shapes, names):
    layout, off = {}, 0
    for n in names:
        rows = _wire_rows(n, shard_shapes[n])
        layout[n] = (off, rows)
        off += rows
    return layout, -(-off // (2 * WIRE_HALF_ALIGN)) * WIRE_HALF_ALIGN


def _wire_pack(blocks, half_rows):
    a = jnp.concatenate(blocks, axis=-2)
    lead = a.shape[:-2]
    a = jnp.pad(a, [(0, 0)] * len(lead) + [(0, 2 * half_rows - a.shape[-2]), (0, 0)])
    return a.reshape(lead + (2, half_rows, WIRE_W))


def _my_place():
    return lax.axis_index("x"), lax.axis_index("y"), lax.axis_index("c")


def _other_chips(x, y):
    return [(1 - x, y), (x, 1 - y), (1 - x, 1 - y)]


ANY = pl.BlockSpec(memory_space=pl.ANY)


def _gather_weights(packed):
    _, R, L = packed.shape
    me = 2 * lax.axis_index("x") + lax.axis_index("y")
    base = lax.dynamic_update_index_in_dim(jnp.zeros((N_CHIPS, 2, R, L), packed.dtype), packed, me, 0)

    def body(in_ref, base_ref, out_ref, send_sems, recv_sems):
        x, y, c = _my_place()
        chip = lambda px, py: 2 * px + py
        sibling = (x, y, 1 - c)
        others = _other_chips(x, y)

        def copy(k, src, dst, to):
            return pltpu.make_async_remote_copy(src_ref=src, dst_ref=dst, send_sem=send_sems.at[k], recv_sem=recv_sems.at[k],
                                                device_id=to, device_id_type=MESH)

        sends = [copy(j, in_ref.at[c], out_ref.at[chip(x, y), c], (px, py, c)) for j, (px, py) in enumerate(others)]
        for cp in sends:
            cp.start()
        passed = [copy(3 + j, out_ref.at[chip(px, py), c], out_ref.at[chip(px, py), c], sibling)
                  for j, (px, py) in enumerate(others)]
        for j, (px, py) in enumerate(others):
            copy(j, in_ref.at[c], out_ref.at[chip(px, py), c], (px, py, c)).wait_recv()
            passed[j].start()
        for j, (px, py) in enumerate(others):
            copy(3 + j, in_ref.at[1 - c], out_ref.at[chip(px, py), 1 - c], sibling).wait_recv()
        for cp in sends + passed:
            cp.wait_send()

    return pl.pallas_call(
        body, out_shape=jax.ShapeDtypeStruct(base.shape, base.dtype), in_specs=[ANY, ANY], out_specs=ANY,
        input_output_aliases={1: 0}, scratch_shapes=[pltpu.SemaphoreType.DMA((6,)), pltpu.SemaphoreType.DMA((6,))],
        name="gather_weights",
    )(packed, base)


HBM_SPEC = pl.BlockSpec(memory_space=pltpu.HBM)
SEM_SPEC = pl.BlockSpec(memory_space=pltpu.SEMAPHORE)
DATAFLOW = pltpu.SideEffectType.DATAFLOW_SIDE_EFFECTING


def _in_hbm(a):
    return pltpu.with_memory_space_constraint(a, pltpu.HBM)


def _split_start(name, n_copies, src, land, start_copies):
    def body(src_ref, land_ref, send_sems, recv_sems, src_thru, land_thru, token):
        start_copies(src_ref, land_ref, send_sems, recv_sems)
        token[...] = jnp.zeros(token.shape, token.dtype)

    return pl.pallas_call(
        body, name=name,
        out_shape=(pltpu.SemaphoreType.DMA((n_copies,)), pltpu.SemaphoreType.DMA((n_copies,)), pltpu.HBM(src.shape, src.dtype),
                   pltpu.HBM(land.shape, land.dtype), jax.ShapeDtypeStruct((8, LANES), F32)),
        in_specs=(HBM_SPEC, HBM_SPEC),
        out_specs=(SEM_SPEC, SEM_SPEC, HBM_SPEC, HBM_SPEC, pl.BlockSpec(memory_space=pltpu.VMEM)),
        input_output_aliases={0: 2, 1: 3}, compiler_params=pltpu.CompilerParams(has_side_effects=DATAFLOW),
    )(_in_hbm(src), _in_hbm(land))


def _split_wait(name, handle, after, wait_copies):
    send_sems, recv_sems, src, land = handle[:4]

    def body(src_ref, land_ref, send_sems, recv_sems, after_ref, src_dead, land_out):
        wait_copies(src_ref, land_ref, send_sems, recv_sems)

    return pl.pallas_call(
        body, name=name, out_shape=(pltpu.HBM(src.shape, src.dtype), pltpu.HBM(land.shape, land.dtype)),
        in_specs=(HBM_SPEC, HBM_SPEC, SEM_SPEC, SEM_SPEC, ANY), out_specs=(HBM_SPEC, HBM_SPEC),
        input_output_aliases={0: 0, 1: 1}, compiler_params=pltpu.CompilerParams(has_side_effects=DATAFLOW),
    )(src, land, send_sems, recv_sems, after)[1]


def _late_gather_copies(src_ref, land_ref, send_sems, recv_sems):
    x, y, c = _my_place()
    me = 2 * x + y
    started, awaited = [], []
    for j, (px, py) in enumerate(_other_chips(x, y)):
        for core in range(2):
            started.append(pltpu.make_async_remote_copy(
                src_ref=src_ref.at[c], dst_ref=land_ref.at[me, c], send_sem=send_sems.at[2 * j + core],
                recv_sem=recv_sems.at[2 * j + c], device_id=(px, py, core), device_id_type=MESH))
            awaited.append(pltpu.make_async_remote_copy(
                src_ref=src_ref.at[core], dst_ref=land_ref.at[2 * px + py, core], send_sem=send_sems.at[2 * j + core],
                recv_sem=recv_sems.at[2 * j + core], device_id=(px, py, core), device_id_type=MESH))
    return started, awaited


def _gather_late_start(packed):
    _, R, L = packed.shape
    me = 2 * lax.axis_index("x") + lax.axis_index("y")
    land = lax.dynamic_update_index_in_dim(jnp.zeros((N_CHIPS, 2, R, L), packed.dtype), packed, me, 0)

    def start(src_ref, land_ref, send_sems, recv_sems):
        for cp in _late_gather_copies(src_ref, land_ref, send_sems, recv_sems)[0]:
            cp.start()

    return _split_start("gather_late_start", 6, packed, land, start)


def _gather_late_wait(handle, after):
    def wait(src_ref, land_ref, send_sems, recv_sems):
        started, awaited = _late_gather_copies(src_ref, land_ref, send_sems, recv_sems)
        for cp in started:
            cp.wait_send()
        for cp in awaited:
            cp.wait_recv()

    return _split_wait("gather_late_wait", handle, after, wait)


def _early_scatter_copies(src_ref, land_ref, send_sems, recv_sems):
    x, y, c = _my_place()
    me = 4 * x + 2 * y + c
    copies = []
    for k in range(1, N_DEV):
        px, py, pc = x ^ (k >> 2), y ^ ((k >> 1) & 1), c ^ (k & 1)
        copies.append(pltpu.make_async_remote_copy(
            src_ref=src_ref.at[2 * px + py, pc], dst_ref=land_ref.at[me], send_sem=send_sems.at[k - 1],
            recv_sem=recv_sems.at[k - 1], device_id=(px, py, pc), device_id_type=MESH))
    return copies


def _scatter_early_start(parts):
    n, _, R, L = parts.shape
    x, y, c = _my_place()
    own = lax.dynamic_index_in_dim(lax.dynamic_index_in_dim(parts, 2 * x + y, 0, keepdims=False), c, 0, keepdims=False)
    land = lax.dynamic_update_index_in_dim(jnp.zeros((N_DEV, R, L), parts.dtype), own, 4 * x + 2 * y + c, 0)

    def start(src_ref, land_ref, send_sems, recv_sems):
        for cp in _early_scatter_copies(src_ref, land_ref, send_sems, recv_sems):
            cp.start()

    return _split_start("scatter_early_start", N_DEV - 1, parts, land, start)


def _scatter_early_wait(handle, after):
    def wait(src_ref, land_ref, send_sems, recv_sems):
        for cp in _early_scatter_copies(src_ref, land_ref, send_sems, recv_sems):
            cp.wait_send()
            cp.wait_recv()

    return _split_wait("scatter_early_wait", handle, after, wait)


def _sum_slots(parts):
    n, R, L = parts.shape
    tr = _tile_rows(R, WIRE_HALF_ALIGN)

    def body(p_ref, o_ref):
        acc = p_ref[0].astype(F32)
        for i in range(1, n):
            acc = acc + p_ref[i].astype(F32)
        o_ref[...] = acc

    return pl.pallas_call(
        body, grid=(R // tr,), in_specs=[pl.BlockSpec((n, tr, L), lambda i: (0, i, 0))],
        out_specs=pl.BlockSpec((tr, L), lambda i: (i, 0)), out_shape=jax.ShapeDtypeStruct((R, L), F32), name="sum_slots",
        compiler_params=_params(("parallel",)),
    )(parts)


def _pair_exchange(parts):
    n, _, R, L = parts.shape

    def body(in_ref, out_ref, send_sems, recv_sems):
        x, y, c = _my_place()
        copies = [pltpu.make_async_remote_copy(src_ref=in_ref.at[s, 1 - c], dst_ref=out_ref.at[s], send_sem=send_sems.at[s],
                                               recv_sem=recv_sems.at[s], device_id=(x, y, 1 - c), device_id_type=MESH)
                  for s in range(n)]
        for cp in copies:
            cp.start()
        for cp in copies:
            cp.wait()

    return pl.pallas_call(
        body, out_shape=jax.ShapeDtypeStruct((n, R, L), parts.dtype), in_specs=[ANY], out_specs=ANY,
        scratch_shapes=[pltpu.SemaphoreType.DMA((n,)), pltpu.SemaphoreType.DMA((n,))], name="pair_exchange",
    )(parts)


def _pair_add(a, b):
    n, R, L = a.shape
    tr = _tile_rows(R, WIRE_HALF_ALIGN)

    def body(a_ref, b_ref, o_ref):
        o_ref[...] = (a_ref[...].astype(F32) + b_ref[...].astype(F32)).astype(o_ref.dtype)

    spec = pl.BlockSpec((n, tr, L), lambda i: (0, i, 0))
    return pl.pallas_call(
        body, grid=(R // tr,), in_specs=[spec, spec], out_specs=spec, out_shape=jax.ShapeDtypeStruct(a.shape, a.dtype),
        name="pair_add", compiler_params=_params(("parallel",)),
    )(a, b)


def _scatter_grads(parts):
    n, R, L = parts.shape

    def body(in_ref, out_ref, send_sems, recv_sems):
        x, y, c = _my_place()
        copies = [pltpu.make_async_remote_copy(src_ref=in_ref.at[2 * px + py], dst_ref=out_ref.at[j], send_sem=send_sems.at[j],
                                               recv_sem=recv_sems.at[j], device_id=(px, py, c), device_id_type=MESH)
                  for j, (px, py) in enumerate(_other_chips(x, y))]
        for cp in copies:
            cp.start()
        for cp in copies:
            cp.wait()

    return pl.pallas_call(
        body, out_shape=jax.ShapeDtypeStruct((n - 1, R, L), parts.dtype), in_specs=[ANY], out_specs=ANY,
        scratch_shapes=[pltpu.SemaphoreType.DMA((3,)), pltpu.SemaphoreType.DMA((3,))], name="scatter_grads",
    )(parts)


def _sum_chips(own, others):
    n, R, L = others.shape
    tr = _tile_rows(R, WIRE_HALF_ALIGN)

    def body(a_ref, p_ref, o_ref):
        acc = a_ref[...].astype(F32)
        for i in range(n):
            acc = acc + p_ref[i].astype(F32)
        o_ref[...] = acc

    return pl.pallas_call(
        body, grid=(R // tr,), in_specs=[pl.BlockSpec((tr, L), lambda i: (i, 0)), pl.BlockSpec((n, tr, L), lambda i: (0, i, 0))],
        out_specs=pl.BlockSpec((tr, L), lambda i: (i, 0)), out_shape=jax.ShapeDtypeStruct((R, L), F32), name="sum_grads",
        compiler_params=_params(("parallel",)),
    )(own, others)


def _tile_rows(R, cap=2048):
    best = 8
    for t in range(8, min(R, cap) + 1, 8):
        if R % t == 0:
            best = t
    return best if R % 8 == 0 else R


def _swap_halves(half):
    def body(in_ref, out_ref, send_sem, recv_sem):
        x, y, c = _my_place()
        cp = pltpu.make_async_remote_copy(src_ref=in_ref, dst_ref=out_ref, send_sem=send_sem, recv_sem=recv_sem,
                                          device_id=(x, y, 1 - c), device_id_type=MESH)
        cp.start()
        cp.wait()

    return pl.pallas_call(
        body, out_shape=jax.ShapeDtypeStruct(half.shape, half.dtype), in_specs=[ANY], out_specs=ANY,
        scratch_shapes=[pltpu.SemaphoreType.DMA, pltpu.SemaphoreType.DMA], name="swap_halves",
    )(half)


def _allreduce_small(v):
    R, L = v.shape

    def body(in_ref, out_ref, buf, send_sems, recv_sems):
        x, y, c = _my_place()
        me = 4 * x + 2 * y + c
        buf[me] = in_ref[...]
        started = []
        for k in range(1, N_DEV):
            to = (x ^ (k >> 2), y ^ ((k >> 1) & 1), c ^ (k & 1))
            cp = pltpu.make_async_remote_copy(src_ref=in_ref, dst_ref=buf.at[me], send_sem=send_sems.at[k - 1],
                                              recv_sem=recv_sems.at[k - 1], device_id=to, device_id_type=MESH)
            cp.start()
            started.append(cp)
        for cp in started:
            cp.wait()
        acc = buf[0]
        for i in range(1, N_DEV):
            acc = acc + buf[i]
        out_ref[...] = acc

    vm = pl.BlockSpec(memory_space=pltpu.VMEM)
    return pl.pallas_call(
        body, out_shape=jax.ShapeDtypeStruct((R, L), F32), in_specs=[vm], out_specs=vm,
        scratch_shapes=[pltpu.VMEM((N_DEV, R, L), F32), pltpu.SemaphoreType.DMA((7,)), pltpu.SemaphoreType.DMA((7,))],
        name="allreduce_small",
    )(v)


def _adamw(name, w, g, m, v):
    shape = w.shape
    C = shape[-1]
    R = int(np.prod(shape[:-1]))
    args = [a.reshape(R, C).astype(F32) for a in (w, g, m, v)]
    if R % 8 == 0 or C % LANES != 0:
        tr, tc = _tile_rows(R, 256), C
    else:
        tr, tc = R, LANES

    def body(w_ref, g_ref, m_ref, v_ref, d_ref, nm_ref, nv_ref):
        d_ref[...], nm_ref[...], nv_ref[...] = _adamw_math(w_ref[...], g_ref[...], m_ref[...], v_ref[...])

    spec = pl.BlockSpec((tr, tc), lambda i, j: (i, j))
    out = pl.pallas_call(
        body, grid=(R // tr, C // tc), in_specs=[spec] * 4, out_specs=[spec] * 3,
        out_shape=[jax.ShapeDtypeStruct((R, C), F32)] * 3, name="adamw_" + name,
        compiler_params=_params(("parallel", "parallel")),
    )(*args)
    return [o.reshape(shape) for o in out]


def _adamw_math(w, g, m, v):
    m = ADAM_B1 * m + (1.0 - ADAM_B1) * g
    v = ADAM_B2 * v + (1.0 - ADAM_B2) * (g * g)
    m_hat = m / (1.0 - ADAM_B1 ** ADAM_STEP)
    v_hat = v / (1.0 - ADAM_B2 ** ADAM_STEP)
    return -ADAM_LR * (m_hat / (jnp.sqrt(v_hat) + ADAM_EPS) + ADAM_WD * w), m, v


def _adamw_small(ws, gs, ms, vs):
    n = len(ws)
    two_d = lambda a: a.reshape(-1, a.shape[-1]).astype(F32)
    args = [two_d(a) for group in (ws, gs, ms, vs) for a in group]

    def body(*refs):
        ins, outs = refs[:4 * n], refs[4 * n:]
        for i in range(n):
            res = _adamw_math(*[ins[k * n + i][...] for k in range(4)])
            for k in range(3):
                outs[k * n + i][...] = res[k]

    out = pl.pallas_call(
        body, out_shape=[jax.ShapeDtypeStruct(a.shape, F32) for a in args[:n]] * 3, name="adamw_small",
        compiler_params=_params(),
    )(*args)
    return [[out[k * n + i].reshape(ws[i].shape) for i in range(n)] for k in range(3)]


def kernel(x, mem, pre1_g, post1_g, pre2_g, post2_g, mem_norm_g, w_in, fox_f_bias, rwkv_mu, rwkv_w0, rwkv_w_up, rwkv_a0, rwkv_a_up, rwkv_g_up, rwkv_k_k, rwkv_k_a, rwkv_r_k, rwkv_gn_g, rwkv_gn_b, w_mem_kv, w_fox_out, w_rwkv_out, w_mem_out, w_o, w_ffn_gate, w_ffn_up, w_ffn_down, loss_target, m_pre1_g, m_post1_g, m_pre2_g, m_post2_g, m_mem_norm_g, m_w_in, m_fox_f_bias, m_rwkv_mu, m_rwkv_w0, m_rwkv_w_up, m_rwkv_a0, m_rwkv_a_up, m_rwkv_g_up, m_rwkv_k_k, m_rwkv_k_a, m_rwkv_r_k, m_rwkv_gn_g, m_rwkv_gn_b, m_w_mem_kv, m_w_fox_out, m_w_rwkv_out, m_w_mem_out, m_w_o, m_w_ffn_gate, m_w_ffn_up, m_w_ffn_down, v_pre1_g, v_post1_g, v_pre2_g, v_post2_g, v_mem_norm_g, v_w_in, v_fox_f_bias, v_rwkv_mu, v_rwkv_w0, v_rwkv_w_up, v_rwkv_a0, v_rwkv_a_up, v_rwkv_g_up, v_rwkv_k_k, v_rwkv_k_a, v_rwkv_r_k, v_rwkv_gn_g, v_rwkv_gn_b, v_w_mem_kv, v_w_fox_out, v_w_rwkv_out, v_w_mem_out, v_w_o, v_w_ffn_gate, v_w_ffn_up, v_w_ffn_down):
    given = dict(locals())
    w_loc = {n: given[n] for n in WEIGHTS}
    m_loc = {n: given["m_" + n] for n in WEIGHTS}
    v_loc = {n: given["v_" + n] for n in WEIGHTS}

    shard_shapes = {n: tuple(w_loc[n].shape[1:]) for n in BIG}
    groups = {names: _wire_layout(shard_shapes, names) for names in (FIRST, LATER)}
    core = lax.axis_index("c")

    def pack_weights(names):
        return _wire_pack([_to_wire(n, w_loc[n][0].astype(BF16)) for n in names], groups[names][1])

    def unpack_weights(gathered, names):
        layout, half_rows = groups[names]
        gathered = gathered.reshape(N_CHIPS, 2 * half_rows, WIRE_W)
        out = {}
        for n in names:
            off, rows = layout[n]
            blocks = _from_wire(n, gathered[:, off:off + rows], shard_shapes[n])
            if n in LORA:
                out[n] = blocks.transpose(2, 0, 1).reshape(blocks.shape[2], -1)
            else:
                out[n] = blocks.reshape(-1, blocks.shape[2])
        return out

    def pack_grads(gw, names):
        blocks = []
        for n in names:
            r, c = shard_shapes[n]
            g = gw[n].astype(BF16)
            if n in LORA:
                g = g.reshape(r, N_CHIPS, c).transpose(1, 0, 2)
            elif n in TRANSPOSED:
                g = jnp.swapaxes(g.reshape(N_CHIPS, c, r), 1, 2)
            else:
                g = g.reshape(N_CHIPS, r, c)
            blocks.append(_to_wire(n, g))
        return _wire_pack(blocks, groups[names][1])

    def unpack_grads(half, other, names):
        layout, _ = groups[names]
        reduced = jnp.where(core == 0, jnp.concatenate([half, other]), jnp.concatenate([other, half]))
        out = {}
        for n in names:
            off, rows = layout[n]
            g = _from_wire(n, reduced[off:off + rows], shard_shapes[n])
            out[n] = g.T if n in LORA else g
        return out

    first = _gather_weights(pack_weights(FIRST))
    late = _gather_late_start(pack_weights(LATER) + (first[0, 0, 0, 0] * 0).astype(BF16))
    W = unpack_weights(first, FIRST)
    W.update({n: w_loc[n][0] for n in SMALL})
    W["pre1_g"] = W["pre1_g"] + late[4][0, 0]
    early = []

    def late_weights(after):
        return unpack_weights(_gather_late_wait(late, after), LATER)

    def early_grads(gw, thru):
        early.append(_scatter_early_start(pack_grads(gw, LATER)))
        return thru + early[0][4][0, 0].astype(thru.dtype)

    loss, grad_x, gw = _layer_step(x, mem, loss_target, W, late_weights, early_grads)

    packed = pack_grads(gw, FIRST)
    own_halves = lax.dynamic_index_in_dim(packed, core, axis=1, keepdims=False)
    chip_sums = _pair_add(own_halves, _pair_exchange(packed))
    own_chip = lax.dynamic_index_in_dim(chip_sums, 2 * lax.axis_index("x") + lax.axis_index("y"), axis=0, keepdims=False)
    half_first = _sum_chips(own_chip, _scatter_grads(chip_sums))
    half_later = _sum_slots(_scatter_early_wait(early[0], half_first))
    rows_first = groups[FIRST][1]
    other = _swap_halves(jnp.concatenate([half_first, half_later]))
    g_shard = {**unpack_grads(half_first, other[:rows_first], FIRST), **unpack_grads(half_later, other[rows_first:], LATER)}

    small_shapes = [w_loc[n].shape[1:] for n in SMALL] + [(1,)]
    n_small = sum(int(np.prod(s)) for s in small_shapes)
    small_rows = -(-n_small // (8 * LANES)) * 8
    flat = jnp.concatenate([gw[n].reshape(-1) for n in SMALL] + [loss.reshape(1)])
    flat = jnp.pad(flat, (0, small_rows * LANES - n_small)).reshape(small_rows, LANES).reshape(-1)
    small, off = [], 0
    flat = _allreduce_small(flat.reshape(small_rows, LANES)).reshape(-1)
    for s in small_shapes:
        cnt = int(np.prod(s))
        small.append(flat[off:off + cnt].reshape(s))
        off += cnt
    g_small = dict(zip(SMALL, small[:-1]))
    loss = small[-1][0]

    flip = lambda a: jnp.swapaxes(a, -1, -2)
    result = {}
    small_g = [g_small[n].reshape(w_loc[n].shape) for n in SMALL]
    small = _adamw_small([w_loc[n] for n in SMALL], small_g, [m_loc[n] for n in SMALL], [v_loc[n] for n in SMALL])
    for i, n in enumerate(SMALL):
        result[n] = (small_g[i], small[0][i], small[1][i], small[2][i])
    for n in BIG:
        if n in TRANSPOSED:
            g_t = g_shard[n][None]
            d, nm, nv = (flip(o) for o in _adamw(n, flip(w_loc[n]), g_t, flip(m_loc[n]), flip(v_loc[n])))
            result[n] = (flip(g_t), d, nm, nv)
        else:
            g = g_shard[n].reshape(w_loc[n].shape)
            result[n] = (g, *_adamw(n, w_loc[n], g, m_loc[n], v_loc[n]))
    return (loss, grad_x, *[result[n][k] for k in range(4) for n in WEIGHTS])
```
